```python
import math
import jax, jax.numpy as jnp
from jax import lax
import numpy as np

D_MODEL = 2048
BATCH = 1
SEQ = 8192
DEPTH = 4
DEC_BATCH = 8
DEC_SEQ = 64
PAST_LEN = 1024

CHUNK = 64
N_MIXERS = 2
N_RET = (DEPTH + 1) // 2
N_SSM = DEPTH // 2
RET_HEADS = 8
RET_DK = D_MODEL // RET_HEADS
RET_DV = 2 * RET_DK
RET_QK = RET_HEADS * RET_DK
RET_VD = RET_HEADS * RET_DV
ROPE_BASE = 10000.0
SSM_GROUP = 16
SSM_GROUPS = D_MODEL // SSM_GROUP
SSM_STATE = 64
DT_MIN = 0.001
DT_MAX = 0.1
FFN_DIM = 2 * D_MODEL
CONV_W = 3
EPS = 1e-6

kernel_name = 'retnet_s5_streaming_hybrid_step'

F32 = jnp.float32


def rmsnorm(x, g):
    xf = x.astype(F32)
    y = xf * lax.rsqrt(jnp.mean(xf * xf, axis=-1, keepdims=True) + EPS)
    return (y * g.astype(F32)).astype(x.dtype)


def rotary(x, pos):
    half = RET_DK // 2
    inv = ROPE_BASE ** (-jnp.arange(half, dtype=F32) / half)
    ang = pos.astype(F32)[:, None] * inv[None, :]
    cos = jnp.cos(ang)[None, :, None, :]
    sin = jnp.sin(ang)[None, :, None, :]
    x1, x2 = x[..., :half], x[..., half:]
    return jnp.concatenate([x1 * cos - x2 * sin, x1 * sin + x2 * cos], axis=-1)


def retention_block(S, q, k, v, log_g):
    L = q.shape[1]
    idx = jnp.arange(L, dtype=F32)
    dist = jnp.abs(idx[:, None] - idx[None, :])
    intra = jnp.exp(dist[None] * log_g[:, None, None])
    scores = jnp.einsum('blhd,bmhd->bhlm', q, k) * intra[None]
    out = jnp.einsum('bhlm,bmhe->blhe', scores, v)
    cross = jnp.exp((idx + 1.0)[:, None] * log_g[None, :])
    out = out + jnp.einsum('blhd,bhde->blhe', q, S) * cross[None, :, :, None]
    k_dec = jnp.exp((L - 1.0 - idx)[:, None] * log_g[None, :])
    S_new = jnp.exp(L * log_g)[None, :, None, None] * S + jnp.einsum(
        'blhd,blhe->bhde', k * k_dec[None, :, :, None], v)
    return S_new, out


def to_blocks(t, nb, blk):
    B = t.shape[0]
    return jnp.moveaxis(t.reshape((B, nb, blk) + t.shape[2:]), 1, 0)


def from_blocks(t):
    t = jnp.moveaxis(t, 0, 1)
    return t.reshape((t.shape[0], t.shape[1] * t.shape[2]) + t.shape[3:])


def retention_mixer(h, S0, pos, w_in, gn, w_out):
    B, L, _ = h.shape
    proj = h @ w_in
    q, k, v, g = jnp.split(proj, [RET_QK, 2 * RET_QK, 2 * RET_QK + RET_VD], axis=-1)
    q = rotary(q.reshape(B, L, RET_HEADS, RET_DK).astype(F32), pos)
    k = rotary(k.reshape(B, L, RET_HEADS, RET_DK).astype(F32), pos) * (RET_DK ** -0.5)
    v = v.reshape(B, L, RET_HEADS, RET_DV).astype(F32)
    log_g = jnp.log1p(-jnp.exp2(-5.0 - jnp.arange(RET_HEADS, dtype=F32)))
    blk = min(L, CHUNK)
    nb = L // blk
    S_last, o = lax.scan(lambda S, xs: retention_block(S, xs[0], xs[1], xs[2], log_g),
                         S0.astype(F32), (to_blocks(q, nb, blk), to_blocks(k, nb, blk), to_blocks(v, nb, blk)))
    o = from_blocks(o)
    mu = jnp.mean(o, axis=-1, keepdims=True)
    var = jnp.mean(jnp.square(o - mu), axis=-1, keepdims=True)
    o = ((o - mu) * lax.rsqrt(var + EPS)).reshape(B, L, RET_VD) * gn.astype(F32)
    y = (o.astype(h.dtype) * jax.nn.silu(g)) @ w_out
    return y, S_last


def _ssm_combine(left, right):
    a1, b1 = left
    a2, b2 = right
    return a1 * a2, a2 * b1 + b2


def ssm_block(h0, u, A_bar, B_bar, C):
    bu = jnp.einsum('gpc,blgc->blgp', B_bar, u.astype(jnp.complex64))
    bu = bu.at[:, 0].add(A_bar[None] * h0)
    a = jnp.broadcast_to(A_bar, bu.shape)
    _, hs = lax.associative_scan(_ssm_combine, (a, bu), axis=1)
    y = jnp.einsum('gcp,blgp->blgc', C, hs).real
    return hs[:, -1], y


def ssm_mixer(h, h0, a_re, a_im, log_dt, b_re, b_im, c_re, c_im, d, w_glu):
    B, L, _ = h.shape
    A = lax.complex(a_re.astype(F32), a_im.astype(F32))
    dt = jnp.exp(log_dt.astype(F32))[:, None]
    A_bar = jnp.exp(A * dt)
    B_bar = ((A_bar - 1.0) / A)[..., None] * lax.complex(b_re.astype(F32), b_im.astype(F32))
    C = lax.complex(c_re.astype(F32), c_im.astype(F32))
    hf = h.astype(F32)
    u = hf.reshape(B, L, SSM_GROUPS, SSM_GROUP)
    blk = min(L, CHUNK)
    nb = L // blk
    h_last, y = lax.scan(lambda s, ub: ssm_block(s, ub, A_bar, B_bar, C), h0, to_blocks(u, nb, blk))
    y = from_blocks(y).reshape(B, L, D_MODEL) + d.astype(F32) * hf
    gl = jax.nn.gelu(y).astype(h.dtype)
    ga, gb = jnp.split(gl @ w_glu, 2, axis=-1)
    return ga * jax.nn.sigmoid(gb), h_last


def conv_ffn(h, prev, w_up, conv_w, conv_b, w_down):
    L = h.shape[1]
    a, b = jnp.split(h @ w_up, 2, axis=-1)
    a_pad = jnp.concatenate([prev.astype(a.dtype), a], axis=1)
    conv = conv_b
    for j in range(CONV_W):
        conv = conv + conv_w[j] * a_pad[:, j:j + L]
    y = (jax.nn.silu(conv) * b) @ w_down
    return y, a_pad[:, L:]


def setup_inputs(seed: int = 0) -> dict:
    key = jax.random.key(seed)
    ks = jax.random.split(key, 32)
    D, G, P = D_MODEL, SSM_GROUPS, SSM_STATE

    def nrm(k, shape, s):
        return jax.random.normal(k, shape, F32) * s

    return {
        'x_prompt': nrm(ks[0], (BATCH, SEQ, D), 1.0),
        'x_sample': nrm(ks[1], (DEC_BATCH, DEC_SEQ, D), 1.0),
        'state_ret': nrm(ks[2], (N_RET, DEC_BATCH, RET_HEADS, RET_DK, RET_DV), 0.05),
        'state_ssm_re': nrm(ks[3], (N_SSM, DEC_BATCH, G, P), 0.5),
        'state_ssm_im': nrm(ks[4], (N_SSM, DEC_BATCH, G, P), 0.5),
        'cache_conv': nrm(ks[5], (DEPTH, DEC_BATCH, CONV_W - 1, FFN_DIM), 1.0),
        'norm_mix': 1.0 + nrm(ks[6], (DEPTH, D), 0.01),
        'norm_ffn': 1.0 + nrm(ks[7], (DEPTH, D), 0.01),
        'norm_final': 1.0 + nrm(ks[8], (D,), 0.01),
        'ret_w_in': nrm(ks[9], (N_RET, D, 2 * RET_QK + 2 * RET_VD), D ** -0.5),
        'ret_gn': 1.0 + nrm(ks[10], (N_RET, RET_VD), 0.01),
        'ret_w_out': nrm(ks[11], (N_RET, RET_VD, D), RET_VD ** -0.5),
        'ssm_a_re': -0.5 + nrm(ks[12], (N_SSM, G, P), 0.01),
        'ssm_a_im': math.pi * jnp.arange(P, dtype=F32) + nrm(ks[13], (N_SSM, G, P), 0.01),
        'ssm_log_dt': jax.random.uniform(ks[14], (N_SSM, G), F32, math.log(DT_MIN), math.log(DT_MAX)),
        'ssm_b_re': nrm(ks[15], (N_SSM, G, P, SSM_GROUP), (2 * SSM_GROUP) ** -0.5),
        'ssm_b_im': nrm(ks[16], (N_SSM, G, P, SSM_GROUP), (2 * SSM_GROUP) ** -0.5),
        'ssm_c_re': nrm(ks[17], (N_SSM, G, SSM_GROUP, P), P ** -0.5),
        'ssm_c_im': nrm(ks[18], (N_SSM, G, SSM_GROUP, P), P ** -0.5),
        'ssm_d': nrm(ks[19], (N_SSM, D), 1.0),
        'ssm_w_glu': nrm(ks[20], (N_SSM, D, 2 * D), D ** -0.5),
        'ffn_w_up': nrm(ks[21], (DEPTH, D, 2 * FFN_DIM), D ** -0.5),
        'ffn_conv_w': nrm(ks[22], (DEPTH, CONV_W, FFN_DIM), CONV_W ** -0.5),
        'ffn_conv_b': nrm(ks[23], (DEPTH, FFN_DIM), 0.01),
        'ffn_w_down': nrm(ks[24], (DEPTH, FFN_DIM, D), FFN_DIM ** -0.5),
    }


def reference(x_prompt, x_sample, state_ret, state_ssm_re, state_ssm_im, cache_conv,
              norm_mix, norm_ffn, norm_final, ret_w_in, ret_gn, ret_w_out,
              ssm_a_re, ssm_a_im, ssm_log_dt, ssm_b_re, ssm_b_im, ssm_c_re, ssm_c_im, ssm_d, ssm_w_glu,
              ffn_w_up, ffn_conv_w, ffn_conv_b, ffn_w_down):

    def run_trunk(x, pos, s_ret, s_re, s_im, c_conv):
        new_ret, new_re, new_im, new_conv = [], [], [], []
        for i in range(DEPTH):
            j = i // N_MIXERS
            h = rmsnorm(x, norm_mix[i])
            if i % N_MIXERS == 0:
                y, s = retention_mixer(h, s_ret[j], pos, ret_w_in[j], ret_gn[j], ret_w_out[j])
                new_ret.append(s)
            else:
                h0 = lax.complex(s_re[j].astype(F32), s_im[j].astype(F32))
                y, s = ssm_mixer(h, h0, ssm_a_re[j], ssm_a_im[j], ssm_log_dt[j], ssm_b_re[j], ssm_b_im[j],
                                 ssm_c_re[j], ssm_c_im[j], ssm_d[j], ssm_w_glu[j])
                new_re.append(s.real)
                new_im.append(s.imag)
            x = x + y.astype(x.dtype)
            y, c = conv_ffn(rmsnorm(x, norm_ffn[i]), c_conv[i], ffn_w_up[i], ffn_conv_w[i],
                            ffn_conv_b[i], ffn_w_down[i])
            new_conv.append(c)
            x = x + y.astype(x.dtype)
        return (rmsnorm(x, norm_final), jnp.stack(new_ret), jnp.stack(new_re),
                jnp.stack(new_im), jnp.stack(new_conv))

    Bp, Lp, _ = x_prompt.shape
    pos_p = jnp.arange(Lp, dtype=jnp.int32)
    zr = jnp.zeros((N_RET, Bp, RET_HEADS, RET_DK, RET_DV), F32)
    zs = jnp.zeros((N_SSM, Bp, SSM_GROUPS, SSM_STATE), F32)
    zc = jnp.zeros((DEPTH, Bp, CONV_W - 1, FFN_DIM), x_prompt.dtype)
    y_prompt, ret_p, re_p, im_p, conv_p = run_trunk(x_prompt, pos_p, zr, zs, zs, zc)

    Ls = x_sample.shape[1]
    pos_s = PAST_LEN + jnp.arange(Ls, dtype=jnp.int32)
    y_sample, ret_s, re_s, im_s, conv_s = run_trunk(x_sample, pos_s, state_ret, state_ssm_re,
                                                    state_ssm_im, cache_conv)
    return (y_prompt, y_sample, ret_p, ret_s, re_p, im_p, re_s, im_s, conv_p, conv_s)
```

```python
import functools
import math

import numpy as np
import jax
import jax.numpy as jnp
from jax import lax
from jax.experimental import pallas as pl
from jax.experimental.pallas import tpu as pltpu

F32 = jnp.float32
BF16 = jnp.bfloat16

D_MODEL = 2048
SEQ = 8192
DEPTH = 4
DEC_BATCH = 8
DEC_SEQ = 64
PAST_LEN = 1024
CHUNK = 64
RET_HEADS = 8
RET_DK = D_MODEL // RET_HEADS
RET_DV = 2 * RET_DK
RET_QK = RET_HEADS * RET_DK
RET_VD = RET_HEADS * RET_DV
ROPE_BASE = 10000.0
SSM_GROUP = 16
SSM_GROUPS = D_MODEL // SSM_GROUP
SSM_STATE = 64
FFN_DIM = 2 * D_MODEL
CONV_W = 3
EPS = 1e-6

M_PROMPT = SEQ
M_SAMPLE = DEC_BATCH * DEC_SEQ
M_TOTAL = M_PROMPT + M_SAMPLE

LANES = 128
SUBLANES = 8
VMEM_LIMIT_BYTES = 56 * 1024 * 1024

BM = M_SAMPLE
N_ROW_TILES = M_TOTAL // BM
N_PROMPT_TILES = M_PROMPT // BM

RET_T = 256

SSM_T = 16
SSM_PAIRS = SSM_GROUPS // 2
SSM_PB = 4
SSM_SEGS = 8
SSM_STEPS_P = M_PROMPT // SSM_T // SSM_SEGS
SSM_STEPS_S = DEC_SEQ // SSM_T
SSM_NC = (SSM_STEPS_P + SSM_STEPS_S) * SSM_SEGS


def _params(n_axes):
    return pltpu.CompilerParams(dimension_semantics=("arbitrary",) * n_axes,
                                vmem_limit_bytes=VMEM_LIMIT_BYTES)


def _sigmoid(x):
    return 1.0 / (1.0 + jnp.exp(-x))


def _rope_kernel(cos_ref, sin_ref):
    m = pl.program_id(0)
    half = RET_DK // 2
    freq = lax.broadcasted_iota(jnp.int32, (1, half), 1).astype(F32)
    inv = ROPE_BASE ** (-freq / half)
    r = m * BM + lax.broadcasted_iota(jnp.int32, (BM, half), 0)
    pos = jnp.where(r < M_PROMPT, r, PAST_LEN + ((r - M_PROMPT) & (DEC_SEQ - 1)))
    ang = pos.astype(F32) * inv
    cos_ref[...] = jnp.cos(ang)
    sin_ref[...] = jnp.sin(ang)


def _rope_tables():
    half = RET_DK // 2
    spec = pl.BlockSpec((BM, half), lambda m: (m, 0))
    return pl.pallas_call(
        _rope_kernel,
        out_shape=(jax.ShapeDtypeStruct((M_TOTAL, half), F32),) * 2,
        grid=(N_ROW_TILES,),
        out_specs=(spec, spec),
        compiler_params=_params(1),
        name="rope_tables",
    )()


def _norm_kernel(x_ref, g_ref, o_ref):
    x = x_ref[...]
    ms = jnp.mean(x * x, axis=-1, keepdims=True)
    o_ref[...] = ((x * lax.rsqrt(ms + EPS)) * g_ref[...]).astype(o_ref.dtype)


def _rmsnorm(x, g, out_dtype):
    spec = pl.BlockSpec((BM, D_MODEL), lambda m: (m, 0))
    return pl.pallas_call(
        _norm_kernel,
        out_shape=jax.ShapeDtypeStruct((M_TOTAL, D_MODEL), out_dtype),
        grid=(N_ROW_TILES,),
        in_specs=[spec, pl.BlockSpec((1, D_MODEL), lambda m: (0, 0))],
        out_specs=spec,
        compiler_params=_params(1),
        name="rmsnorm",
    )(x, g.reshape(1, D_MODEL))


def _mm_retin_kernel(a_ref, w_ref, cos_ref, sin_ref, o_ref, wb_ref, *, bn):
    n = pl.program_id(0)
    m = pl.program_id(1)
    n_q = RET_QK // bn
    n_rot = 2 * n_q

    @pl.when(m == 0)
    def _():
        wb_ref[...] = w_ref[...].astype(BF16)

    acc = jnp.dot(a_ref[...], wb_ref[...], preferred_element_type=F32)

    @pl.when(n < n_rot)
    def _():
        scale = jnp.where(n >= n_q, RET_DK ** -0.5, 1.0).astype(F32)
        c = cos_ref[...]
        s = sin_ref[...]
        half = RET_DK // 2
        for hh in range(bn // RET_DK):
            lo = hh * RET_DK
            x1 = acc[:, lo:lo + half]
            x2 = acc[:, lo + half:lo + RET_DK]
            o_ref[:, lo:lo + half] = ((x1 * c - x2 * s) * scale).astype(BF16)
            o_ref[:, lo + half:lo + RET_DK] = ((x1 * s + x2 * c) * scale).astype(BF16)

    @pl.when(n >= n_rot)
    def _():
        o_ref[...] = acc.astype(BF16)


def _mm_retin(h, w, cos, sin):
    bn = 1024
    k, n_out = w.shape
    half = RET_DK // 2
    return pl.pallas_call(
        functools.partial(_mm_retin_kernel, bn=bn),
        out_shape=jax.ShapeDtypeStruct((M_TOTAL, n_out), BF16),
        grid=(n_out // bn, N_ROW_TILES),
        in_specs=[
            pl.BlockSpec((BM, k), lambda n, m: (m, 0)),
            pl.BlockSpec((k, bn), lambda n, m: (0, n)),
            pl.BlockSpec((BM, half), lambda n, m: (m, 0)),
            pl.BlockSpec((BM, half), lambda n, m: (m, 0)),
        ],
        out_specs=pl.BlockSpec((BM, bn), lambda n, m: (m, n)),
        scratch_shapes=[pltpu.VMEM((k, bn), BF16)],
        compiler_params=_params(2),
        name="mm_ret_in",
    )(h, w, cos, sin)


def _mm_res_kernel(a_ref, w_ref, x_ref, o_ref, wb_ref):
    @pl.when(pl.program_id(1) == 0)
    def _():
        wb_ref[...] = w_ref[...].astype(BF16)

    o_ref[...] = x_ref[...] + jnp.dot(a_ref[...], wb_ref[...], preferred_element_type=F32)


def _mm_res(a, w, x):
    bn = 512
    k, n_out = w.shape
    return pl.pallas_call(
        _mm_res_kernel,
        out_shape=jax.ShapeDtypeStruct((M_TOTAL, n_out), F32),
        grid=(n_out // bn, N_ROW_TILES),
        in_specs=[
            pl.BlockSpec((BM, k), lambda n, m: (m, 0)),
            pl.BlockSpec((k, bn), lambda n, m: (0, n)),
            pl.BlockSpec((BM, bn), lambda n, m: (m, n)),
        ],
        out_specs=pl.BlockSpec((BM, bn), lambda n, m: (m, n)),
        scratch_shapes=[pltpu.VMEM((k, bn), BF16)],
        compiler_params=_params(2),
        name="mm_residual",
    )(a, w, x)


def _mm_glu_kernel(a_ref, wa_ref, wg_ref, x_ref, o_ref, wab_ref, wgb_ref):
    @pl.when(pl.program_id(1) == 0)
    def _():
        wab_ref[...] = wa_ref[...].astype(BF16)
        wgb_ref[...] = wg_ref[...].astype(BF16)

    a = a_ref[...]
    ga = jnp.dot(a, wab_ref[...], preferred_element_type=F32)
    gb = jnp.dot(a, wgb_ref[...], preferred_element_type=F32)
    o_ref[...] = x_ref[...] + ga * _sigmoid(gb)


def _mm_glu(a, w, x):
    bn = 512
    k, n2 = w.shape
    n_out = n2 // 2
    nb = n_out // bn
    return pl.pallas_call(
        _mm_glu_kernel,
        out_shape=jax.ShapeDtypeStruct((M_TOTAL, n_out), F32),
        grid=(nb, N_ROW_TILES),
        in_specs=[
            pl.BlockSpec((BM, k), lambda n, m: (m, 0)),
            pl.BlockSpec((k, bn), lambda n, m: (0, n)),
            pl.BlockSpec((k, bn), lambda n, m: (0, n + nb)),
            pl.BlockSpec((BM, bn), lambda n, m: (m, n)),
        ],
        out_specs=pl.BlockSpec((BM, bn), lambda n, m: (m, n)),
        scratch_shapes=[pltpu.VMEM((k, bn), BF16), pltpu.VMEM((k, bn), BF16)],
        compiler_params=_params(2),
        name="mm_glu",
    )(a, w, w, x)


def _mm_ffnup_kernel(h_ref, wa_ref, wg_ref, cw_ref, cb_ref, p1_ref, p2_ref,
                     o_ref, tp_ref, ts_ref, wab_ref, wgb_ref, carry_ref):
    m = pl.program_id(1)

    @pl.when(m == 0)
    def _():
        wab_ref[...] = wa_ref[...].astype(BF16)
        wgb_ref[...] = wg_ref[...].astype(BF16)
        carry_ref[...] = jnp.zeros_like(carry_ref)

    h = h_ref[...]
    a = jnp.dot(h, wab_ref[...], preferred_element_type=F32)
    b = jnp.dot(h, wgb_ref[...], preferred_element_type=F32)
    row = lax.broadcasted_iota(jnp.int32, (BM, 1), 0)
    r1 = pltpu.roll(a, 1, axis=0)
    r2 = pltpu.roll(a, 2, axis=0)

    def finish(prev1, prev2):
        conv = cb_ref[...] + cw_ref[0:1, :] * prev2 + cw_ref[1:2, :] * prev1 + cw_ref[2:3, :] * a
        o_ref[...] = ((conv * _sigmoid(conv)) * b).astype(BF16)

    @pl.when(m < N_PROMPT_TILES)
    def _():
        c = carry_ref[...]
        last = c[SUBLANES - 1:SUBLANES, :]
        last2 = c[SUBLANES - 2:SUBLANES - 1, :]
        prev1 = jnp.where(row == 0, last, r1)
        prev2 = jnp.where(row == 0, last2, jnp.where(row == 1, last, r2))
        finish(prev1, prev2)
        carry_ref[...] = a[BM - SUBLANES:, :]

        @pl.when(m == N_PROMPT_TILES - 1)
        def _():
            tp_ref[...] = a[BM - SUBLANES:, :]

    @pl.when(m == N_PROMPT_TILES)
    def _():
        i = row & (DEC_SEQ - 1)
        prev1 = jnp.where(i == 0, p1_ref[...], r1)
        prev2 = jnp.where(i < 2, p2_ref[...], r2)
        finish(prev1, prev2)
        for s in range(DEC_BATCH):
            ts_ref[s * SUBLANES:(s + 1) * SUBLANES, :] = a[(s + 1) * DEC_SEQ - SUBLANES:(s + 1) * DEC_SEQ, :]


def _mm_ffnup(h, w, conv_w, conv_b, cache):
    bn = 512
    k = w.shape[0]
    nb = FFN_DIM // bn
    zeros = jnp.zeros((DEC_BATCH, DEC_SEQ, FFN_DIM), F32)
    p1 = zeros.at[:, 0].set(cache[:, 1]).reshape(M_SAMPLE, FFN_DIM)
    p2 = zeros.at[:, 0].set(cache[:, 0]).at[:, 1].set(cache[:, 1]).reshape(M_SAMPLE, FFN_DIM)
    return pl.pallas_call(
        _mm_ffnup_kernel,
        out_shape=(
            jax.ShapeDtypeStruct((M_TOTAL, FFN_DIM), BF16),
            jax.ShapeDtypeStruct((SUBLANES, FFN_DIM), F32),
            jax.ShapeDtypeStruct((DEC_BATCH * SUBLANES, FFN_DIM), F32),
        ),
        grid=(nb, N_ROW_TILES),
        in_specs=[
            pl.BlockSpec((BM, k), lambda n, m: (m, 0)),
            pl.BlockSpec((k, bn), lambda n, m: (0, n)),
            pl.BlockSpec((k, bn), lambda n, m: (0, n + nb)),
            pl.BlockSpec((CONV_W, bn), lambda n, m: (0, n)),
            pl.BlockSpec((1, bn), lambda n, m: (0, n)),
            pl.BlockSpec((BM, bn), lambda n, m: (0, n)),
            pl.BlockSpec((BM, bn), lambda n, m: (0, n)),
        ],
        out_specs=(
            pl.BlockSpec((BM, bn), lambda n, m: (m, n)),
            pl.BlockSpec((SUBLANES, bn), lambda n, m: (0, n)),
            pl.BlockSpec((DEC_BATCH * SUBLANES, bn), lambda n, m: (0, n)),
        ),
        scratch_shapes=[pltpu.VMEM((k, bn), BF16), pltpu.VMEM((k, bn), BF16),
                        pltpu.VMEM((SUBLANES, bn), F32)],
        compiler_params=_params(2),
        name="mm_ffn_up",
    )(h, w, w, conv_w, conv_b.reshape(1, FFN_DIM), p1, p2)


def _ret_kernel(*refs, T, has_init):
    if has_init:
        (lg_ref, q_ref, k_ref, v_ref, g_ref, gn_ref, s0_ref,
         o_ref, sl_ref, state_ref, mask_ref) = refs
    else:
        (lg_ref, q_ref, k_ref, v_ref, g_ref, gn_ref,
         o_ref, sl_ref, state_ref, mask_ref) = refs
    blk = pl.program_id(2)
    lg = lg_ref[...][:, 0:1]

    @pl.when(blk == 0)
    def _():
        if has_init:
            state_ref[...] = s0_ref[...]
        else:
            state_ref[...] = jnp.zeros_like(state_ref)
        i = lax.broadcasted_iota(jnp.int32, (T, T), 0)
        j = lax.broadcasted_iota(jnp.int32, (T, T), 1)
        dist = jnp.abs(i - j).astype(F32)
        shift = int(math.log2(CHUNK))
        visible = (j >> shift) <= (i >> shift)
        mask_ref[...] = jnp.where(visible, jnp.exp(dist * lg), 0.0)

    q = q_ref[...]
    k = k_ref[...]
    v = v_ref[...]
    state = state_ref[...]
    t = lax.broadcasted_iota(jnp.int32, (T, 1), 0).astype(F32)

    scores = lax.dot_general(q, k, (((1,), (1,)), ((), ())), preferred_element_type=F32)
    scores = scores * mask_ref[...]
    out = jnp.dot(scores.astype(BF16), v, preferred_element_type=F32)
    cross = jnp.exp((t + 1.0) * lg)
    out = out + jnp.dot(q, state.astype(BF16), preferred_element_type=F32) * cross

    k_dec = jnp.exp((T - 1.0 - t) * lg)
    kd = (k.astype(F32) * k_dec).astype(BF16)
    new_state = jnp.exp(float(T) * lg) * state + lax.dot_general(
        kd, v, (((0,), (0,)), ((), ())), preferred_element_type=F32)
    state_ref[...] = new_state

    mu = jnp.mean(out, axis=-1, keepdims=True)
    oc = out - mu
    var = jnp.mean(oc * oc, axis=-1, keepdims=True)
    normed = (oc * lax.rsqrt(var + EPS)) * gn_ref[...]
    g = g_ref[...].astype(F32)
    o_ref[...] = (normed * (g * _sigmoid(g))).astype(BF16)

    @pl.when(blk == pl.num_programs(2) - 1)
    def _():
        sl_ref[...] = new_state


def _retention(qkvg, gn, s0, *, row0, n_seq, n_blk, T):
    log_g = np.log1p(-np.exp2(-5.0 - np.arange(RET_HEADS, dtype=np.float32))).astype(np.float32)
    lg = jnp.asarray(np.broadcast_to(log_g[:, None, None], (RET_HEADS, 1, LANES)).copy())
    rb0 = row0 // T
    k_off = RET_QK // RET_DK
    v_off = 2 * RET_QK // RET_DV
    g_off = v_off + RET_VD // RET_DV
    has_init = s0 is not None

    def rows(h, s, b):
        return rb0 + s * n_blk + b

    in_specs = [
        pl.BlockSpec((None, 1, LANES), lambda h, s, b: (h, 0, 0)),
        pl.BlockSpec((T, RET_DK), lambda h, s, b: (rows(h, s, b), h)),
        pl.BlockSpec((T, RET_DK), lambda h, s, b: (rows(h, s, b), k_off + h)),
        pl.BlockSpec((T, RET_DV), lambda h, s, b: (rows(h, s, b), v_off + h)),
        pl.BlockSpec((T, RET_DV), lambda h, s, b: (rows(h, s, b), g_off + h)),
        pl.BlockSpec((1, RET_DV), lambda h, s, b: (0, h)),
    ]
    args = [lg, qkvg, qkvg, qkvg, qkvg, gn.reshape(1, RET_VD)]
    if has_init:
        in_specs.append(pl.BlockSpec((None, None, RET_DK, RET_DV), lambda h, s, b: (s, h, 0, 0)))
        args.append(s0)
    return pl.pallas_call(
        functools.partial(_ret_kernel, T=T, has_init=has_init),
        out_shape=(
            jax.ShapeDtypeStruct((n_seq * n_blk * T, RET_VD), BF16),
            jax.ShapeDtypeStruct((n_seq, RET_HEADS, RET_DK, RET_DV), F32),
        ),
        grid=(RET_HEADS, n_seq, n_blk),
        in_specs=in_specs,
        out_specs=(
            pl.BlockSpec((T, RET_DV), lambda h, s, b: (s * n_blk + b, h)),
            pl.BlockSpec((None, None, RET_DK, RET_DV), lambda h, s, b: (s, h, 0, 0)),
        ),
        scratch_shapes=[pltpu.VMEM((RET_DK, RET_DV), F32), pltpu.VMEM((T, T), F32)],
        compiler_params=_params(3),
        name="retention_init" if has_init else "retention",
    )(*args)


def _ssm_prep_kernel(ar_ref, ai_ref, ldt_ref, br_ref, bi_ref, cr_ref, ci_ref,
                     bc_ref, cc_ref, toep_ref, a16r_ref, a16i_ref, vr_s, vi_s):
    ar = ar_ref[...]
    ai = ai_ref[...]
    dt = jnp.exp(ldt_ref[...])
    mag = jnp.exp(ar * dt)
    ang = ai * dt
    abr = mag * jnp.cos(ang)
    abi = mag * jnp.sin(ang)
    den = ar * ar + ai * ai
    nr = abr - 1.0
    ni = abi
    cfr = (nr * ar + ni * ai) / den
    cfi = (ni * ar - nr * ai) / den

    def b4(z):
        return z[:, None]

    br = br_ref[...]
    bi = bi_ref[...]
    bbr = b4(cfr) * br - b4(cfi) * bi
    bbi = b4(cfr) * bi + b4(cfi) * br
    cre = cr_ref[...]
    cim = ci_ref[...]

    powers = []
    pr = jnp.ones_like(ar)
    pi = jnp.zeros_like(ar)
    for _ in range(SSM_T + 1):
        powers.append((pr, pi))
        pr, pi = pr * abr - pi * abi, pr * abi + pi * abr
    a16r_ref[...] = powers[SSM_T][0]
    a16i_ref[...] = powers[SSM_T][1]

    for m in range(SSM_T + 1):
        pr, pi = powers[m]
        vr = b4(pr) * cre - b4(pi) * cim
        vi = b4(pr) * cim + b4(pi) * cre
        if m < SSM_T:
            vr_s[:, :, m] = vr
            vi_s[:, :, m] = vi
        if m >= 1:
            cc_ref[:, :, m - 1, :, 0:LANES] = vr.astype(BF16)
            cc_ref[:, :, m - 1, :, LANES:2 * LANES] = (-vi).astype(BF16)

    for t in range(SSM_T):
        pr, pi = powers[SSM_T - 1 - t]
        bc_ref[:, :, t, :, 0:LANES] = (b4(pr) * bbr - b4(pi) * bbi).astype(BF16)
        bc_ref[:, :, t, :, LANES:2 * LANES] = (b4(pr) * bbi + b4(pi) * bbr).astype(BF16)

    width = SSM_T * SSM_GROUP
    lane = lax.broadcasted_iota(jnp.int32, (SSM_GROUP, width), 1)
    contract_last = (((1,), (1,)), ((), ()))
    for q in range(SSM_PB):
        for gl in range(2):
            vr_all = vr_s[q, gl].reshape(width, LANES)
            vi_all = vi_s[q, gl].reshape(width, LANES)
            krow = (lax.dot_general(bbr[q, gl], vr_all, contract_last,
                                    precision=lax.Precision.HIGHEST, preferred_element_type=F32)
                    - lax.dot_general(bbi[q, gl], vi_all, contract_last,
                                      precision=lax.Precision.HIGHEST, preferred_element_type=F32))
            for t in range(SSM_T):
                shifted = krow if t == 0 else pltpu.roll(krow, SSM_GROUP * t, axis=1)
                toep_ref[2 * q + gl, t] = jnp.where(lane >= SSM_GROUP * t, shifted, 0.0).astype(BF16)


def _pair_pack(x):
    x = x.reshape(SSM_PAIRS, 2, SSM_GROUP, SSM_STATE)
    lo = jnp.pad(x[:, 0], ((0, 0), (0, 0), (0, SSM_STATE)))
    hi = jnp.pad(x[:, 1], ((0, 0), (0, 0), (SSM_STATE, 0)))
    return jnp.stack([lo, hi], axis=1)


def _ssm_prep(a_re, a_im, log_dt, b_re, b_im, c_re, c_im):
    ar = a_re.reshape(SSM_PAIRS, 1, LANES)
    ai = a_im.reshape(SSM_PAIRS, 1, LANES)
    ldt = jnp.repeat(log_dt, SSM_STATE).reshape(SSM_PAIRS, 1, LANES)
    b2r = _pair_pack(jnp.transpose(b_re, (0, 2, 1)))
    b2i = _pair_pack(jnp.transpose(b_im, (0, 2, 1)))
    c2r = _pair_pack(c_re)
    c2i = _pair_pack(c_im)
    vec = pl.BlockSpec((SSM_PB, 1, LANES), lambda p: (p, 0, 0))
    mat = pl.BlockSpec((SSM_PB, 2, SSM_GROUP, LANES), lambda p: (p, 0, 0, 0))
    proj = pl.BlockSpec((SSM_PB, 2, SSM_T, SSM_GROUP, 2 * LANES), lambda p: (p, 0, 0, 0, 0))
    width = SSM_T * SSM_GROUP
    bc, cc, toep, a16r, a16i = pl.pallas_call(
        _ssm_prep_kernel,
        out_shape=(
            jax.ShapeDtypeStruct((SSM_PAIRS, 2, SSM_T, SSM_GROUP, 2 * LANES), BF16),
            jax.ShapeDtypeStruct((SSM_PAIRS, 2, SSM_T, SSM_GROUP, 2 * LANES), BF16),
            jax.ShapeDtypeStruct((SSM_GROUPS, SSM_T, SSM_GROUP, width), BF16),
            jax.ShapeDtypeStruct((SSM_PAIRS, 1, LANES), F32),
            jax.ShapeDtypeStruct((SSM_PAIRS, 1, LANES), F32),
        ),
        grid=(SSM_PAIRS // SSM_PB,),
        in_specs=[vec, vec, vec, mat, mat, mat, mat],
        out_specs=(proj, proj,
                   pl.BlockSpec((2 * SSM_PB, SSM_T, SSM_GROUP, width), lambda p: (p, 0, 0, 0)),
                   vec, vec),
        scratch_shapes=[pltpu.VMEM((SSM_PB, 2, SSM_T, SSM_GROUP, LANES), F32)] * 2,
        compiler_params=_params(1),
        name="ssm_prep",
    )(ar, ai, ldt, b2r, b2i, c2r, c2i)
    bc = bc.reshape(SSM_PAIRS, 2 * width, 2 * LANES)
    cc = cc.reshape(SSM_PAIRS, 2 * width, 2 * LANES)
    toep = toep.reshape(SSM_GROUPS, width, width)
    row = SSM_GROUPS * SSM_STATE
    a16r = jnp.broadcast_to(a16r.reshape(1, row), (SSM_SEGS, row))
    a16i = jnp.broadcast_to(a16i.reshape(1, row), (SSM_SEGS, row))
    return bc, cc, toep, a16r, a16i


def _ssm_main_kernel(u_ref, toep_ref, bc_ref, cc_ref, ar_ref, ai_ref, h0r_ref, h0i_ref,
                     y_ref, fpr_ref, fpi_ref, fsr_ref, fsi_ref,
                     dr_s, di_s, sr_s, si_s):
    width = SSM_T * SSM_GROUP
    w = SSM_PB * LANES

    for q in range(SSM_PB):
        d = jnp.dot(u_ref[:, 2 * q * width:2 * (q + 1) * width], bc_ref[q],
                    preferred_element_type=F32)
        dr_s[:, q * LANES:(q + 1) * LANES] = d[:, 0:LANES]
        di_s[:, q * LANES:(q + 1) * LANES] = d[:, LANES:2 * LANES]

    ar = ar_ref[...]
    ai = ai_ref[...]

    def scan(i, carry):
        sr, si = carry
        r0 = pl.multiple_of(i * SSM_SEGS, SSM_SEGS)
        sr_s[pl.ds(r0, SSM_SEGS), :] = sr
        si_s[pl.ds(r0, SSM_SEGS), :] = si
        dr = dr_s[pl.ds(r0, SSM_SEGS), :]
        di = di_s[pl.ds(r0, SSM_SEGS), :]
        return ar * sr - ai * si + dr, ar * si + ai * sr + di

    zero = jnp.zeros((SSM_SEGS, w), F32)
    er, ei = lax.fori_loop(0, SSM_STEPS_P, scan, (zero, zero))
    fr, fi = lax.fori_loop(SSM_STEPS_P, SSM_STEPS_P + SSM_STEPS_S, scan,
                           (h0r_ref[...], h0i_ref[...]))
    fsr_ref[...] = fr
    fsi_ref[...] = fi

    pr, pi = ar[0:1], ai[0:1]
    for _ in range(int(math.log2(SSM_STEPS_P))):
        pr, pi = pr * pr - pi * pi, 2.0 * pr * pi
    cr = jnp.zeros((1, w), F32)
    ci = jnp.zeros((1, w), F32)
    starts_r, starts_i = [], []
    for s in range(SSM_SEGS):
        starts_r.append(cr)
        starts_i.append(ci)
        cr, ci = (pr * cr - pi * ci + er[s:s + 1], pr * ci + pi * cr + ei[s:s + 1])
    fpr_ref[...] = jnp.broadcast_to(cr, (SSM_SEGS, w))
    fpi_ref[...] = jnp.broadcast_to(ci, (SSM_SEGS, w))
    seg_r = jnp.concatenate(starts_r, axis=0)
    seg_i = jnp.concatenate(starts_i, axis=0)

    def fix(i, carry):
        qr, qi = carry
        r0 = pl.multiple_of(i * SSM_SEGS, SSM_SEGS)
        sr_s[pl.ds(r0, SSM_SEGS), :] += qr * seg_r - qi * seg_i
        si_s[pl.ds(r0, SSM_SEGS), :] += qr * seg_i + qi * seg_r
        return qr * ar - qi * ai, qr * ai + qi * ar

    lax.fori_loop(0, SSM_STEPS_P, fix, (jnp.ones((SSM_SEGS, w), F32), zero))

    contract_last = (((1,), (1,)), ((), ()))
    for q in range(SSM_PB):
        sr = sr_s[:, q * LANES:(q + 1) * LANES].astype(BF16)
        si = si_s[:, q * LANES:(q + 1) * LANES].astype(BF16)
        cc = cc_ref[q]
        carried = (lax.dot_general(sr, cc[:, 0:LANES], contract_last, preferred_element_type=F32)
                   + lax.dot_general(si, cc[:, LANES:2 * LANES], contract_last,
                                     preferred_element_type=F32))
        for gl in range(2):
            c0 = (2 * q + gl) * width
            local = jnp.dot(u_ref[:, c0:c0 + width], toep_ref[2 * q + gl], preferred_element_type=F32)
            y_ref[:, c0:c0 + width] = local + carried[:, gl * width:(gl + 1) * width]


def _chunk_major(h):
    def relayout(x, segs, steps):
        x = x.reshape(segs, steps, SSM_T, SSM_GROUPS, SSM_GROUP)
        x = jnp.transpose(x, (1, 0, 3, 2, 4))
        return x.reshape(steps * segs, SSM_GROUPS * SSM_T * SSM_GROUP)
    return jnp.concatenate([relayout(h[:M_PROMPT], SSM_SEGS, SSM_STEPS_P),
                            relayout(h[M_PROMPT:], DEC_BATCH, SSM_STEPS_S)], axis=0)


def _token_major(y):
    def relayout(x, segs, steps):
        x = x.reshape(steps, segs, SSM_GROUPS, SSM_T, SSM_GROUP)
        x = jnp.transpose(x, (1, 0, 3, 2, 4))
        return x.reshape(segs * steps * SSM_T, D_MODEL)
    n_p = SSM_STEPS_P * SSM_SEGS
    return jnp.concatenate([relayout(y[:n_p], SSM_SEGS, SSM_STEPS_P),
                            relayout(y[n_p:], DEC_BATCH, SSM_STEPS_S)], axis=0)


def _ssm_main(u_t, toep, bc, cc, a16r, a16i, h0r, h0i):
    width = SSM_T * SSM_GROUP
    cols = 2 * SSM_PB * width
    w = SSM_PB * LANES
    row = SSM_GROUPS * SSM_STATE
    st = pl.BlockSpec((SSM_SEGS, w), lambda p: (0, p))
    st_shape = jax.ShapeDtypeStruct((SSM_SEGS, row), F32)
    return pl.pallas_call(
        _ssm_main_kernel,
        out_shape=(jax.ShapeDtypeStruct((SSM_NC, SSM_GROUPS * width), F32),
                   st_shape, st_shape, st_shape, st_shape),
        grid=(SSM_PAIRS // SSM_PB,),
        in_specs=[
            pl.BlockSpec((SSM_NC, cols), lambda p: (0, p)),
            pl.BlockSpec((2 * SSM_PB, width, width), lambda p: (p, 0, 0)),
            pl.BlockSpec((SSM_PB, 2 * width, 2 * LANES), lambda p: (p, 0, 0)),
            pl.BlockSpec((SSM_PB, 2 * width, 2 * LANES), lambda p: (p, 0, 0)),
            st, st, st, st,
        ],
        out_specs=(pl.BlockSpec((SSM_NC, cols), lambda p: (0, p)), st, st, st, st),
        scratch_shapes=[pltpu.VMEM((SSM_NC, w), F32)] * 4,
        compiler_params=_params(1),
        name="ssm_main",
    )(u_t, toep, bc, cc, a16r, a16i, h0r, h0i)


def _ssm_post_kernel(y_ref, x_ref, g_ref, d_ref, o_ref):
    x = x_ref[...]
    ms = jnp.mean(x * x, axis=-1, keepdims=True)
    h = (x * lax.rsqrt(ms + EPS)) * g_ref[...]
    y = y_ref[...] + d_ref[...] * h
    c = math.sqrt(2.0 / math.pi)
    gelu = 0.5 * y * (1.0 + jnp.tanh(c * (y + 0.044715 * (y * y * y))))
    o_ref[...] = gelu.astype(BF16)


def _ssm_post(y, x, g, d):
    spec = pl.BlockSpec((BM, D_MODEL), lambda m: (m, 0))
    vec = pl.BlockSpec((1, D_MODEL), lambda m: (0, 0))
    return pl.pallas_call(
        _ssm_post_kernel,
        out_shape=jax.ShapeDtypeStruct((M_TOTAL, D_MODEL), BF16),
        grid=(N_ROW_TILES,),
        in_specs=[spec, spec, vec, vec],
        out_specs=spec,
        compiler_params=_params(1),
        name="ssm_post",
    )(y, x, g.reshape(1, D_MODEL), d.reshape(1, D_MODEL))


def kernel(x_prompt, x_sample, state_ret, state_ssm_re, state_ssm_im, cache_conv, norm_mix, norm_ffn, norm_final, ret_w_in, ret_gn, ret_w_out, ssm_a_re, ssm_a_im, ssm_log_dt, ssm_b_re, ssm_b_im, ssm_c_re, ssm_c_im, ssm_d, ssm_w_glu, ffn_w_up, ffn_conv_w, ffn_conv_b, ffn_w_down):
    x = jnp.concatenate([x_prompt.reshape(M_PROMPT, D_MODEL), x_sample.reshape(M_SAMPLE, D_MODEL)], axis=0)
    cos, sin = _rope_tables()

    ret_p, ret_s = [], []
    re_p, im_p, re_s, im_s = [], [], [], []
    conv_p, conv_s = [], []
    state_row = SSM_GROUPS * SSM_STATE
    for i in range(DEPTH):
        j = i // 2
        if i % 2 == 0:
            h = _rmsnorm(x, norm_mix[i], BF16)
            qkvg = _mm_retin(h, ret_w_in[j], cos, sin)
            o_p, s_p = _retention(qkvg, ret_gn[j], None, row0=0, n_seq=1,
                                  n_blk=M_PROMPT // RET_T, T=RET_T)
            o_s, s_s = _retention(qkvg, ret_gn[j], state_ret[j], row0=M_PROMPT, n_seq=DEC_BATCH,
                                  n_blk=1, T=DEC_SEQ)
            ret_p.append(s_p)
            ret_s.append(s_s)
            x = _mm_res(jnp.concatenate([o_p, o_s], axis=0), ret_w_out[j], x)
        else:
            h = _rmsnorm(x, norm_mix[i], BF16)
            bc, cc, toep, a16r, a16i = _ssm_prep(ssm_a_re[j], ssm_a_im[j], ssm_log_dt[j],
                                                ssm_b_re[j], ssm_b_im[j], ssm_c_re[j], ssm_c_im[j])
            y_t, fpr, fpi, fsr, fsi = _ssm_main(
                _chunk_major(h), toep, bc, cc, a16r, a16i,
                state_ssm_re[j].reshape(DEC_BATCH, state_row),
                state_ssm_im[j].reshape(DEC_BATCH, state_row))
            re_p.append(fpr[0:1].reshape(1, SSM_GROUPS, SSM_STATE))
            im_p.append(fpi[0:1].reshape(1, SSM_GROUPS, SSM_STATE))
            re_s.append(fsr.reshape(DEC_BATCH, SSM_GROUPS, SSM_STATE))
            im_s.append(fsi.reshape(DEC_BATCH, SSM_GROUPS, SSM_STATE))
            gl = _ssm_post(_token_major(y_t), x, norm_mix[i], ssm_d[j])
            x = _mm_glu(gl, ssm_w_glu[j], x)

        h = _rmsnorm(x, norm_ffn[i], BF16)
        u, tail_p, tail_s = _mm_ffnup(h, ffn_w_up[i], ffn_conv_w[i], ffn_conv_b[i], cache_conv[i])
        conv_p.append(tail_p[SUBLANES - (CONV_W - 1):].reshape(1, CONV_W - 1, FFN_DIM))
        conv_s.append(tail_s.reshape(DEC_BATCH, SUBLANES, FFN_DIM)[:, SUBLANES - (CONV_W - 1):])
        x = _mm_res(u, ffn_w_down[i], x)

    y = _rmsnorm(x, norm_final, F32)
    y_prompt = y[:M_PROMPT].reshape(1, SEQ, D_MODEL)
    y_sample = y[M_PROMPT:].reshape(DEC_BATCH, DEC_SEQ, D_MODEL)
    return (y_prompt, y_sample, jnp.stack(ret_p), jnp.stack(ret_s),
            jnp.stack(re_p), jnp.stack(im_p), jnp.stack(re_s), jnp.stack(im_s),
            jnp.stack(conv_p), jnp.stack(conv_s))
```

```python
import functools
import math

import numpy as np
import jax
import jax.numpy as jnp
from jax import lax
from jax.experimental import pallas as pl
from jax.experimental.pallas import tpu as pltpu

F32 = jnp.float32
BF16 = jnp.bfloat16

D_MODEL = 2048
SEQ = 8192
DEPTH = 4
DEC_BATCH = 8
DEC_SEQ = 64
PAST_LEN = 1024
CHUNK = 64
RET_HEADS = 8
RET_DK = D_MODEL // RET_HEADS
RET_DV = 2 * RET_DK
RET_QK = RET_HEADS * RET_DK
RET_VD = RET_HEADS * RET_DV
ROPE_BASE = 10000.0
SSM_GROUP = 16
SSM_GROUPS = D_MODEL // SSM_GROUP
SSM_STATE = 64
FFN_DIM = 2 * D_MODEL
CONV_W = 3
EPS = 1e-6

M_PROMPT = SEQ
M_SAMPLE = DEC_BATCH * DEC_SEQ
M_TOTAL = M_PROMPT + M_SAMPLE

LANES = 128
SUBLANES = 8
VMEM_LIMIT_BYTES = 56 * 1024 * 1024

BM = M_SAMPLE
N_ROW_TILES = M_TOTAL // BM
N_PROMPT_TILES = M_PROMPT // BM

RET_T = 256

SSM_T = 16
SSM_PAIRS = SSM_GROUPS // 2
SSM_PB = 4
SSM_SEGS = 8
SSM_STEPS_P = M_PROMPT // SSM_T // SSM_SEGS
SSM_STEPS_S = DEC_SEQ // SSM_T
SSM_NC = (SSM_STEPS_P + SSM_STEPS_S) * SSM_SEGS


def _params(n_axes):
    return pltpu.CompilerParams(dimension_semantics=("arbitrary",) * n_axes,
                                vmem_limit_bytes=VMEM_LIMIT_BYTES)


def _sigmoid(x):
    return 1.0 / (1.0 + jnp.exp(-x))


def _rope_kernel(cos_ref, sin_ref):
    m = pl.program_id(0)
    half = RET_DK // 2
    freq = lax.broadcasted_iota(jnp.int32, (1, half), 1).astype(F32)
    inv = ROPE_BASE ** (-freq / half)
    r = m * BM + lax.broadcasted_iota(jnp.int32, (BM, half), 0)
    pos = jnp.where(r < M_PROMPT, r, PAST_LEN + ((r - M_PROMPT) & (DEC_SEQ - 1)))
    ang = pos.astype(F32) * inv
    cos_ref[...] = jnp.cos(ang)
    sin_ref[...] = jnp.sin(ang)


def _rope_tables():
    half = RET_DK // 2
    spec = pl.BlockSpec((BM, half), lambda m: (m, 0))
    return pl.pallas_call(
        _rope_kernel,
        out_shape=(jax.ShapeDtypeStruct((M_TOTAL, half), F32),) * 2,
        grid=(N_ROW_TILES,),
        out_specs=(spec, spec),
        compiler_params=_params(1),
        name="rope_tables",
    )()


def _norm_kernel(x_ref, g_ref, o_ref):
    x = x_ref[...]
    ms = jnp.mean(x * x, axis=-1, keepdims=True)
    o_ref[...] = ((x * lax.rsqrt(ms + EPS)) * g_ref[...]).astype(o_ref.dtype)


def _rmsnorm(x, g, out_dtype, tile0=0, n_tiles=N_ROW_TILES):
    return pl.pallas_call(
        _norm_kernel,
        out_shape=jax.ShapeDtypeStruct((n_tiles * BM, D_MODEL), out_dtype),
        grid=(n_tiles,),
        in_specs=[pl.BlockSpec((BM, D_MODEL), lambda m: (tile0 + m, 0)),
                  pl.BlockSpec((1, D_MODEL), lambda m: (0, 0))],
        out_specs=pl.BlockSpec((BM, D_MODEL), lambda m: (m, 0)),
        compiler_params=_params(1),
        name="rmsnorm",
    )(x, g.reshape(1, D_MODEL))


CHUNKS_PER_TILE = BM // 16
N_CHUNKS = M_TOTAL // 16
N_LANE_TILES = D_MODEL // LANES


def _norm_chunked_kernel(x_ref, g_ref, o_ref, h_s):
    x = x_ref[...]
    ms = jnp.mean(x * x, axis=-1, keepdims=True)
    h = (x * lax.rsqrt(ms + EPS)) * g_ref[...]
    for j in range(N_LANE_TILES):
        h_s[j] = h[:, j * LANES:(j + 1) * LANES]
    for t in range(SSM_T):
        for j in range(N_LANE_TILES):
            rows = h_s[j, pl.ds(t, CHUNKS_PER_TILE, stride=SSM_T), :]
            o_ref[j, :, t * LANES:(t + 1) * LANES] = rows.astype(BF16)


def _rmsnorm_chunked(x, g):
    return pl.pallas_call(
        _norm_chunked_kernel,
        out_shape=jax.ShapeDtypeStruct((N_LANE_TILES, N_CHUNKS, SSM_T * LANES), BF16),
        grid=(N_ROW_TILES,),
        in_specs=[pl.BlockSpec((BM, D_MODEL), lambda m: (m, 0)),
                  pl.BlockSpec((1, D_MODEL), lambda m: (0, 0))],
        out_specs=pl.BlockSpec((N_LANE_TILES, CHUNKS_PER_TILE, SSM_T * LANES), lambda m: (0, m, 0)),
        scratch_shapes=[pltpu.VMEM((N_LANE_TILES, BM, LANES), F32)],
        compiler_params=_params(1),
        name="rmsnorm_chunked",
    )(x, g.reshape(1, D_MODEL))


def _mm_retin_kernel(a_ref, w_ref, cos_ref, sin_ref, o_ref, wb_ref, *, bn):
    n = pl.program_id(0)
    m = pl.program_id(1)
    n_q = RET_QK // bn
    n_rot = 2 * n_q

    @pl.when(m == 0)
    def _():
        wb_ref[...] = w_ref[...].astype(BF16)

    acc = jnp.dot(a_ref[...], wb_ref[...], preferred_element_type=F32)

    @pl.when(n < n_rot)
    def _():
        scale = jnp.where(n >= n_q, RET_DK ** -0.5, 1.0).astype(F32)
        c = cos_ref[...]
        s = sin_ref[...]
        half = RET_DK // 2
        for hh in range(bn // RET_DK):
            lo = hh * RET_DK
            x1 = acc[:, lo:lo + half]
            x2 = acc[:, lo + half:lo + RET_DK]
            o_ref[:, lo:lo + half] = ((x1 * c - x2 * s) * scale).astype(BF16)
            o_ref[:, lo + half:lo + RET_DK] = ((x1 * s + x2 * c) * scale).astype(BF16)

    @pl.when(n >= n_rot)
    def _():
        o_ref[...] = acc.astype(BF16)


def _mm_retin(h, w, cos, sin):
    bn = 1024
    k, n_out = w.shape
    half = RET_DK // 2
    return pl.pallas_call(
        functools.partial(_mm_retin_kernel, bn=bn),
        out_shape=jax.ShapeDtypeStruct((M_TOTAL, n_out), BF16),
        grid=(n_out // bn, N_ROW_TILES),
        in_specs=[
            pl.BlockSpec((BM, k), lambda n, m: (m, 0)),
            pl.BlockSpec((k, bn), lambda n, m: (0, n)),
            pl.BlockSpec((BM, half), lambda n, m: (m, 0)),
            pl.BlockSpec((BM, half), lambda n, m: (m, 0)),
        ],
        out_specs=pl.BlockSpec((BM, bn), lambda n, m: (m, n)),
        scratch_shapes=[pltpu.VMEM((k, bn), BF16)],
        compiler_params=_params(2),
        name="mm_ret_in",
    )(h, w, cos, sin)


def _mm_res_kernel(*refs, split):
    if split:
        ap_ref, as_ref, w_ref, x_ref, o_ref, wb_ref = refs
    else:
        ap_ref, w_ref, x_ref, o_ref, wb_ref = refs
        as_ref = ap_ref
    m = pl.program_id(1)

    @pl.when(m == 0)
    def _():
        wb_ref[...] = w_ref[...].astype(BF16)

    def body(a_ref):
        o_ref[...] = x_ref[...] + jnp.dot(a_ref[...], wb_ref[...], preferred_element_type=F32)

    if split:
        pl.when(m < N_PROMPT_TILES)(lambda: body(ap_ref))
        pl.when(m == N_PROMPT_TILES)(lambda: body(as_ref))
    else:
        body(ap_ref)


def _mm_res(a, w, x, a_sample=None):
    bn = 512
    k, n_out = w.shape
    split = a_sample is not None
    if split:
        a_specs = [pl.BlockSpec((BM, k), lambda n, m: (jnp.minimum(m, N_PROMPT_TILES - 1), 0)),
                   pl.BlockSpec((BM, k), lambda n, m: (0, 0))]
        a_args = [a, a_sample]
    else:
        a_specs = [pl.BlockSpec((BM, k), lambda n, m: (m, 0))]
        a_args = [a]
    return pl.pallas_call(
        functools.partial(_mm_res_kernel, split=split),
        out_shape=jax.ShapeDtypeStruct((M_TOTAL, n_out), F32),
        grid=(n_out // bn, N_ROW_TILES),
        in_specs=a_specs + [
            pl.BlockSpec((k, bn), lambda n, m: (0, n)),
            pl.BlockSpec((BM, bn), lambda n, m: (m, n)),
        ],
        out_specs=pl.BlockSpec((BM, bn), lambda n, m: (m, n)),
        scratch_shapes=[pltpu.VMEM((k, bn), BF16)],
        compiler_params=_params(2),
        name="mm_residual",
    )(*a_args, w, x)


def _mm_glu_kernel(a_ref, wa_ref, wg_ref, x_ref, o_ref, wab_ref, wgb_ref):
    @pl.when(pl.program_id(1) == 0)
    def _():
        wab_ref[...] = wa_ref[...].astype(BF16)
        wgb_ref[...] = wg_ref[...].astype(BF16)

    a = a_ref[...]
    ga = jnp.dot(a, wab_ref[...], preferred_element_type=F32)
    gb = jnp.dot(a, wgb_ref[...], preferred_element_type=F32)
    o_ref[...] = x_ref[...] + ga * _sigmoid(gb)


def _mm_glu(a, w, x):
    bn = 512
    k, n2 = w.shape
    n_out = n2 // 2
    nb = n_out // bn
    return pl.pallas_call(
        _mm_glu_kernel,
        out_shape=jax.ShapeDtypeStruct((M_TOTAL, n_out), F32),
        grid=(nb, N_ROW_TILES),
        in_specs=[
            pl.BlockSpec((BM, k), lambda n, m: (m, 0)),
            pl.BlockSpec((k, bn), lambda n, m: (0, n)),
            pl.BlockSpec((k, bn), lambda n, m: (0, n + nb)),
            pl.BlockSpec((BM, bn), lambda n, m: (m, n)),
        ],
        out_specs=pl.BlockSpec((BM, bn), lambda n, m: (m, n)),
        scratch_shapes=[pltpu.VMEM((k, bn), BF16), pltpu.VMEM((k, bn), BF16)],
        compiler_params=_params(2),
        name="mm_glu",
    )(a, w, w, x)


def _mm_ffnup_kernel(h_ref, wa_ref, wg_ref, cw_ref, cb_ref, cache_ref,
                     o_ref, tp_ref, ts_ref, wab_ref, wgb_ref, carry_ref):
    m = pl.program_id(1)

    @pl.when(m == 0)
    def _():
        wab_ref[...] = wa_ref[...].astype(BF16)
        wgb_ref[...] = wg_ref[...].astype(BF16)
        carry_ref[...] = jnp.zeros_like(carry_ref)

    h = h_ref[...]
    a = jnp.dot(h, wab_ref[...], preferred_element_type=F32)
    b = jnp.dot(h, wgb_ref[...], preferred_element_type=F32)
    r1 = pltpu.roll(a, 1, axis=0)
    r2 = pltpu.roll(a, 2, axis=0)
    cb = cb_ref[...]
    w0 = cw_ref[0:1, :]
    w1 = cw_ref[1:2, :]
    w2 = cw_ref[2:3, :]

    def gated(a_rows, prev1, prev2, b_rows):
        conv = cb + w0 * prev2 + w1 * prev1 + w2 * a_rows
        return ((conv * _sigmoid(conv)) * b_rows).astype(BF16)

    o_ref[...] = gated(a, r1, r2, b)

    row8 = lax.broadcasted_iota(jnp.int32, (SUBLANES, 1), 0)

    def redo_head(r0, before1, before2):
        rows = slice(r0, r0 + SUBLANES)
        prev1 = jnp.where(row8 == 0, before1, r1[rows])
        prev2 = jnp.where(row8 == 0, before2, jnp.where(row8 == 1, before1, r2[rows]))
        o_ref[rows, :] = gated(a[rows], prev1, prev2, b[rows])

    @pl.when(m < N_PROMPT_TILES)
    def _():
        c = carry_ref[...]
        redo_head(0, c[SUBLANES - 1:SUBLANES, :], c[SUBLANES - 2:SUBLANES - 1, :])
        carry_ref[...] = a[BM - SUBLANES:, :]

        @pl.when(m == N_PROMPT_TILES - 1)
        def _():
            tp_ref[...] = a[BM - SUBLANES:, :]

    @pl.when(m == N_PROMPT_TILES)
    def _():
        for s in range(DEC_BATCH):
            c = cache_ref[s]
            redo_head(s * DEC_SEQ, c[1:2, :], c[0:1, :])
            ts_ref[s * SUBLANES:(s + 1) * SUBLANES, :] = a[(s + 1) * DEC_SEQ - SUBLANES:(s + 1) * DEC_SEQ, :]


def _mm_ffnup(h, w, conv_w, conv_b, cache):
    bn = 512
    k = w.shape[0]
    nb = FFN_DIM // bn
    return pl.pallas_call(
        _mm_ffnup_kernel,
        out_shape=(
            jax.ShapeDtypeStruct((M_TOTAL, FFN_DIM), BF16),
            jax.ShapeDtypeStruct((SUBLANES, FFN_DIM), F32),
            jax.ShapeDtypeStruct((DEC_BATCH * SUBLANES, FFN_DIM), F32),
        ),
        grid=(nb, N_ROW_TILES),
        in_specs=[
            pl.BlockSpec((BM, k), lambda n, m: (m, 0)),
            pl.BlockSpec((k, bn), lambda n, m: (0, n)),
            pl.BlockSpec((k, bn), lambda n, m: (0, n + nb)),
            pl.BlockSpec((CONV_W, bn), lambda n, m: (0, n)),
            pl.BlockSpec((1, bn), lambda n, m: (0, n)),
            pl.BlockSpec((DEC_BATCH, CONV_W - 1, bn), lambda n, m: (0, 0, n)),
        ],
        out_specs=(
            pl.BlockSpec((BM, bn), lambda n, m: (m, n)),
            pl.BlockSpec((SUBLANES, bn), lambda n, m: (0, n)),
            pl.BlockSpec((DEC_BATCH * SUBLANES, bn), lambda n, m: (0, n)),
        ),
        scratch_shapes=[pltpu.VMEM((k, bn), BF16), pltpu.VMEM((k, bn), BF16),
                        pltpu.VMEM((SUBLANES, bn), F32)],
        compiler_params=_params(2),
        name="mm_ffn_up",
    )(h, w, w, conv_w, conv_b.reshape(1, FFN_DIM), cache)


def _ret_kernel(*refs, T, has_init):
    if has_init:
        (lg_ref, q_ref, k_ref, v_ref, g_ref, gn_ref, s0_ref,
         o_ref, sl_ref, state_ref, mask_ref) = refs
    else:
        (lg_ref, q_ref, k_ref, v_ref, g_ref, gn_ref,
         o_ref, sl_ref, state_ref, mask_ref) = refs
    blk = pl.program_id(2)
    lg = lg_ref[...][:, 0:1]

    @pl.when(blk == 0)
    def _():
        if has_init:
            state_ref[...] = s0_ref[...]
        else:
            state_ref[...] = jnp.zeros_like(state_ref)
        i = lax.broadcasted_iota(jnp.int32, (T, T), 0)
        j = lax.broadcasted_iota(jnp.int32, (T, T), 1)
        dist = jnp.abs(i - j).astype(F32)
        shift = int(math.log2(CHUNK))
        visible = (j >> shift) <= (i >> shift)
        mask_ref[...] = jnp.where(visible, jnp.exp(dist * lg), 0.0)

    q = q_ref[...]
    k = k_ref[...]
    v = v_ref[...]
    state = state_ref[...]
    t = lax.broadcasted_iota(jnp.int32, (T, 1), 0).astype(F32)

    scores = lax.dot_general(q, k, (((1,), (1,)), ((), ())), preferred_element_type=F32)
    scores = scores * mask_ref[...]
    out = jnp.dot(scores.astype(BF16), v, preferred_element_type=F32)
    cross = jnp.exp((t + 1.0) * lg)
    out = out + jnp.dot(q, state.astype(BF16), preferred_element_type=F32) * cross

    k_dec = jnp.exp((T - 1.0 - t) * lg)
    kd = (k.astype(F32) * k_dec).astype(BF16)
    new_state = jnp.exp(float(T) * lg) * state + lax.dot_general(
        kd, v, (((0,), (0,)), ((), ())), preferred_element_type=F32)
    state_ref[...] = new_state

    mu = jnp.mean(out, axis=-1, keepdims=True)
    oc = out - mu
    var = jnp.mean(oc * oc, axis=-1, keepdims=True)
    normed = (oc * lax.rsqrt(var + EPS)) * gn_ref[...]
    g = g_ref[...].astype(F32)
    o_ref[...] = (normed * (g * _sigmoid(g))).astype(BF16)

    @pl.when(blk == pl.num_programs(2) - 1)
    def _():
        sl_ref[...] = new_state


def _retention(qkvg, gn, s0, *, row0, n_seq, n_blk, T):
    log_g = np.log1p(-np.exp2(-5.0 - np.arange(RET_HEADS, dtype=np.float32))).astype(np.float32)
    lg = jnp.asarray(np.broadcast_to(log_g[:, None, None], (RET_HEADS, 1, LANES)).copy())
    rb0 = row0 // T
    k_off = RET_QK // RET_DK
    v_off = 2 * RET_QK // RET_DV
    g_off = v_off + RET_VD // RET_DV
    has_init = s0 is not None

    def rows(h, s, b):
        return rb0 + s * n_blk + b

    in_specs = [
        pl.BlockSpec((None, 1, LANES), lambda h, s, b: (h, 0, 0)),
        pl.BlockSpec((T, RET_DK), lambda h, s, b: (rows(h, s, b), h)),
        pl.BlockSpec((T, RET_DK), lambda h, s, b: (rows(h, s, b), k_off + h)),
        pl.BlockSpec((T, RET_DV), lambda h, s, b: (rows(h, s, b), v_off + h)),
        pl.BlockSpec((T, RET_DV), lambda h, s, b: (rows(h, s, b), g_off + h)),
        pl.BlockSpec((1, RET_DV), lambda h, s, b: (0, h)),
    ]
    args = [lg, qkvg, qkvg, qkvg, qkvg, gn.reshape(1, RET_VD)]
    if has_init:
        in_specs.append(pl.BlockSpec((None, None, RET_DK, RET_DV), lambda h, s, b: (s, h, 0, 0)))
        args.append(s0)
    return pl.pallas_call(
        functools.partial(_ret_kernel, T=T, has_init=has_init),
        out_shape=(
            jax.ShapeDtypeStruct((n_seq * n_blk * T, RET_VD), BF16),
            jax.ShapeDtypeStruct((n_seq, RET_HEADS, RET_DK, RET_DV), F32),
        ),
        grid=(RET_HEADS, n_seq, n_blk),
        in_specs=in_specs,
        out_specs=(
            pl.BlockSpec((T, RET_DV), lambda h, s, b: (s * n_blk + b, h)),
            pl.BlockSpec((None, None, RET_DK, RET_DV), lambda h, s, b: (s, h, 0, 0)),
        ),
        scratch_shapes=[pltpu.VMEM((RET_DK, RET_DV), F32), pltpu.VMEM((T, T), F32)],
        compiler_params=_params(3),
        name="retention_init" if has_init else "retention",
    )(*args)


def _ssm_prep_kernel(ar_ref, ai_ref, ldt_ref, br_ref, bi_ref, cr_ref, ci_ref,
                     bc_ref, cc_ref, lag_ref, a16r_ref, a16i_ref, vr_s, vi_s):
    ar = ar_ref[...]
    ai = ai_ref[...]
    dt = jnp.exp(ldt_ref[...])
    mag = jnp.exp(ar * dt)
    ang = ai * dt
    abr = mag * jnp.cos(ang)
    abi = mag * jnp.sin(ang)
    den = ar * ar + ai * ai
    nr = abr - 1.0
    ni = abi
    cfr = (nr * ar + ni * ai) / den
    cfi = (ni * ar - nr * ai) / den

    def b4(z):
        return z[:, None]

    br = br_ref[...]
    bi = bi_ref[...]
    bbr = b4(cfr) * br - b4(cfi) * bi
    bbi = b4(cfr) * bi + b4(cfi) * br
    cre = cr_ref[...]
    cim = ci_ref[...]

    powers = []
    pr = jnp.ones_like(ar)
    pi = jnp.zeros_like(ar)
    for _ in range(SSM_T + 1):
        powers.append((pr, pi))
        pr, pi = pr * abr - pi * abi, pr * abi + pi * abr
    a16r_ref[...] = powers[SSM_T][0]
    a16i_ref[...] = powers[SSM_T][1]

    for m in range(SSM_T + 1):
        pr, pi = powers[m]
        vr = b4(pr) * cre - b4(pi) * cim
        vi = b4(pr) * cim + b4(pi) * cre
        if m < SSM_T:
            vr_s[:, :, m] = vr
            vi_s[:, :, m] = vi
        if m >= 1:
            cc_ref[:, :, m - 1, :, 0:LANES] = vr.astype(BF16)
            cc_ref[:, :, m - 1, :, LANES:2 * LANES] = (-vi).astype(BF16)

    for t in range(SSM_T):
        pr, pi = powers[SSM_T - 1 - t]
        bc_ref[:, :, t, :, 0:LANES] = (b4(pr) * bbr - b4(pi) * bbi).astype(BF16)
        bc_ref[:, :, t, :, LANES:2 * LANES] = (b4(pr) * bbi + b4(pi) * bbr).astype(BF16)

    width = SSM_T * SSM_GROUP
    lane = lax.broadcasted_iota(jnp.int32, (SSM_GROUP, LANES), 1)
    contract_last = (((1,), (1,)), ((), ()))
    for q in range(SSM_PB):
        for half in range(2):
            gl = 2 * q + half
            vr_all = vr_s[q, half].reshape(width, LANES)
            vi_all = vi_s[q, half].reshape(width, LANES)
            krow = (lax.dot_general(bbr[q, half], vr_all, contract_last,
                                    precision=lax.Precision.HIGHEST, preferred_element_type=F32)
                    - lax.dot_general(bbi[q, half], vi_all, contract_last,
                                      precision=lax.Precision.HIGHEST, preferred_element_type=F32))
            in_block = (lane >= gl * SSM_GROUP) & (lane < (gl + 1) * SSM_GROUP)
            for m in range(SSM_T):
                shift = ((gl - m) * SSM_GROUP) % width
                moved = krow if shift == 0 else pltpu.roll(krow, shift, axis=1)
                lag_ref[m, gl * SSM_GROUP:(gl + 1) * SSM_GROUP, :] = jnp.where(
                    in_block, moved[:, 0:LANES], 0.0).astype(BF16)


def _pair_pack(x):
    x = x.reshape(SSM_PAIRS, 2, SSM_GROUP, SSM_STATE)
    lo = jnp.pad(x[:, 0], ((0, 0), (0, 0), (0, SSM_STATE)))
    hi = jnp.pad(x[:, 1], ((0, 0), (0, 0), (SSM_STATE, 0)))
    return jnp.stack([lo, hi], axis=1)


def _ssm_prep(a_re, a_im, log_dt, b_re, b_im, c_re, c_im):
    assert 2 * SSM_PB * SSM_GROUP == LANES
    ar = a_re.reshape(SSM_PAIRS, 1, LANES)
    ai = a_im.reshape(SSM_PAIRS, 1, LANES)
    ldt = jnp.repeat(log_dt, SSM_STATE).reshape(SSM_PAIRS, 1, LANES)
    b2r = _pair_pack(jnp.transpose(b_re, (0, 2, 1)))
    b2i = _pair_pack(jnp.transpose(b_im, (0, 2, 1)))
    c2r = _pair_pack(c_re)
    c2i = _pair_pack(c_im)
    vec = pl.BlockSpec((SSM_PB, 1, LANES), lambda p: (p, 0, 0))
    mat = pl.BlockSpec((SSM_PB, 2, SSM_GROUP, LANES), lambda p: (p, 0, 0, 0))
    proj = pl.BlockSpec((SSM_PB, 2, SSM_T, SSM_GROUP, 2 * LANES), lambda p: (p, 0, 0, 0, 0))
    bc, cc, lag, a16r, a16i = pl.pallas_call(
        _ssm_prep_kernel,
        out_shape=(
            jax.ShapeDtypeStruct((SSM_PAIRS, 2, SSM_T, SSM_GROUP, 2 * LANES), BF16),
            jax.ShapeDtypeStruct((SSM_PAIRS, 2, SSM_T, SSM_GROUP, 2 * LANES), BF16),
            jax.ShapeDtypeStruct((N_LANE_TILES, SSM_T, LANES, LANES), BF16),
            jax.ShapeDtypeStruct((SSM_PAIRS, 1, LANES), F32),
            jax.ShapeDtypeStruct((SSM_PAIRS, 1, LANES), F32),
        ),
        grid=(N_LANE_TILES,),
        in_specs=[vec, vec, vec, mat, mat, mat, mat],
        out_specs=(proj, proj,
                   pl.BlockSpec((None, SSM_T, LANES, LANES), lambda p: (p, 0, 0, 0)),
                   vec, vec),
        scratch_shapes=[pltpu.VMEM((SSM_PB, 2, SSM_T, SSM_GROUP, LANES), F32)] * 2,
        compiler_params=_params(1),
        name="ssm_prep",
    )(ar, ai, ldt, b2r, b2i, c2r, c2i)
    row = SSM_GROUPS * SSM_STATE
    a16r = jnp.broadcast_to(a16r.reshape(1, row), (SSM_SEGS, row))
    a16i = jnp.broadcast_to(a16i.reshape(1, row), (SSM_SEGS, row))
    return bc, cc, lag, a16r, a16i


SSM_CHUNK_COLS = SSM_T * LANES
SSM_STATE_COLS = SSM_PB * LANES
SSM_CHUNKS_P = M_PROMPT // SSM_T


def _ssm_main_kernel(x_ref, lag_ref, bc_ref, cc_ref, ar_ref, ai_ref, h0r_ref, h0i_ref,
                     y_ref, fpr_ref, fpi_ref, fsr_ref, fsi_ref,
                     w_s, bcw_s, ccw_s, d_s, s_s):
    w = SSM_STATE_COLS

    @pl.when(pl.program_id(0) == 0)
    def _():
        w_s[...] = jnp.zeros_like(w_s)
        bcw_s[...] = jnp.zeros_like(bcw_s)
        ccw_s[...] = jnp.zeros_like(ccw_s)

    for t in range(SSM_T):
        for t2 in range(t, SSM_T):
            w_s[t * LANES:(t + 1) * LANES, t2 * LANES:(t2 + 1) * LANES] = lag_ref[t2 - t]
    for q in range(SSM_PB):
        for half in range(2):
            for t in range(SSM_T):
                r0 = t * LANES + (2 * q + half) * SSM_GROUP
                rows = slice(r0, r0 + SSM_GROUP)
                bcw_s[rows, q * LANES:(q + 1) * LANES] = bc_ref[q, half, t, :, 0:LANES]
                bcw_s[rows, w + q * LANES:w + (q + 1) * LANES] = bc_ref[q, half, t, :, LANES:2 * LANES]
                ccw_s[rows, q * LANES:(q + 1) * LANES] = cc_ref[q, half, t, :, 0:LANES]
                ccw_s[rows, w + q * LANES:w + (q + 1) * LANES] = cc_ref[q, half, t, :, LANES:2 * LANES]

    x = x_ref[...]
    d = jnp.dot(x, bcw_s[...], preferred_element_type=F32)
    for k in range(2 * SSM_PB):
        d_s[k] = d[:, k * LANES:(k + 1) * LANES]

    for k in range(SSM_PB):
        cols = slice(k * LANES, (k + 1) * LANES)
        ar = ar_ref[:, cols]
        ai = ai_ref[:, cols]

        def scan(sr, si, rows):
            s_s[k, rows, :] = sr
            s_s[SSM_PB + k, rows, :] = si
            dr = d_s[k, rows, :]
            di = d_s[SSM_PB + k, rows, :]
            return ar * sr - ai * si + dr, ar * si + ai * sr + di

        sr = jnp.zeros((SSM_SEGS, LANES), F32)
        si = sr
        for i in range(SSM_STEPS_P):
            sr, si = scan(sr, si, pl.ds(i, SSM_SEGS, stride=SSM_STEPS_P))
        er, ei = sr, si
        sr = h0r_ref[:, cols]
        si = h0i_ref[:, cols]
        for i in range(SSM_STEPS_S):
            sr, si = scan(sr, si, pl.ds(SSM_CHUNKS_P + i, DEC_BATCH, stride=SSM_STEPS_S))
        fsr_ref[:, cols] = sr
        fsi_ref[:, cols] = si

        pr, pi = ar[0:1], ai[0:1]
        for _ in range(int(math.log2(SSM_STEPS_P))):
            pr, pi = pr * pr - pi * pi, 2.0 * pr * pi
        cr = jnp.zeros((1, LANES), F32)
        ci = cr
        starts_r, starts_i = [], []
        for s in range(SSM_SEGS):
            starts_r.append(cr)
            starts_i.append(ci)
            cr, ci = (pr * cr - pi * ci + er[s:s + 1], pr * ci + pi * cr + ei[s:s + 1])
        fpr_ref[:, cols] = jnp.broadcast_to(cr, (SSM_SEGS, LANES))
        fpi_ref[:, cols] = jnp.broadcast_to(ci, (SSM_SEGS, LANES))
        seg_r = jnp.concatenate(starts_r, axis=0)
        seg_i = jnp.concatenate(starts_i, axis=0)

        qr = jnp.ones((SSM_SEGS, LANES), F32)
        qi = jnp.zeros((SSM_SEGS, LANES), F32)
        for i in range(SSM_STEPS_P):
            rows = pl.ds(i, SSM_SEGS, stride=SSM_STEPS_P)
            s_s[k, rows, :] = s_s[k, rows, :] + (qr * seg_r - qi * seg_i)
            s_s[SSM_PB + k, rows, :] = s_s[SSM_PB + k, rows, :] + (qr * seg_i + qi * seg_r)
            qr, qi = qr * ar - qi * ai, qr * ai + qi * ar

    states = jnp.concatenate([s_s[k].astype(BF16) for k in range(2 * SSM_PB)], axis=1)
    contract_last = (((1,), (1,)), ((), ()))
    carried = lax.dot_general(states, ccw_s[...], contract_last, preferred_element_type=F32)
    y_ref[...] = jnp.dot(x, w_s[...], preferred_element_type=F32) + carried


def _ssm_main(x_c, lag, bc, cc, a16r, a16i, h0r, h0i):
    w = SSM_STATE_COLS
    row = SSM_GROUPS * SSM_STATE
    st = pl.BlockSpec((SSM_SEGS, w), lambda j: (0, j))
    st_shape = jax.ShapeDtypeStruct((SSM_SEGS, row), F32)
    proj = pl.BlockSpec((SSM_PB, 2, SSM_T, SSM_GROUP, 2 * LANES), lambda j: (j, 0, 0, 0, 0))
    chunk_rows = pl.BlockSpec((None, N_CHUNKS, SSM_CHUNK_COLS), lambda j: (j, 0, 0))
    return pl.pallas_call(
        _ssm_main_kernel,
        out_shape=(jax.ShapeDtypeStruct((N_LANE_TILES, N_CHUNKS, SSM_CHUNK_COLS), F32),
                   st_shape, st_shape, st_shape, st_shape),
        grid=(N_LANE_TILES,),
        in_specs=[
            chunk_rows,
            pl.BlockSpec((None, SSM_T, LANES, LANES), lambda j: (j, 0, 0, 0)),
            proj, proj, st, st, st, st,
        ],
        out_specs=(chunk_rows, st, st, st, st),
        scratch_shapes=[
            pltpu.VMEM((SSM_CHUNK_COLS, SSM_CHUNK_COLS), BF16),
            pltpu.VMEM((SSM_CHUNK_COLS, 2 * w), BF16),
            pltpu.VMEM((SSM_CHUNK_COLS, 2 * w), BF16),
            pltpu.VMEM((2 * SSM_PB, N_CHUNKS, LANES), F32),
            pltpu.VMEM((2 * SSM_PB, N_CHUNKS, LANES), F32),
        ],
        compiler_params=_params(1),
        name="ssm_main",
    )(x_c, lag, bc, cc, a16r, a16i, h0r, h0i)


def _ssm_post_kernel(y_ref, x_ref, g_ref, d_ref, o_ref, y_s):
    for t in range(SSM_T):
        for j in range(N_LANE_TILES):
            y_s[j, pl.ds(t, CHUNKS_PER_TILE, stride=SSM_T), :] = y_ref[j, :, t * LANES:(t + 1) * LANES]
    x = x_ref[...]
    ms = jnp.mean(x * x, axis=-1, keepdims=True)
    h = (x * lax.rsqrt(ms + EPS)) * g_ref[...]
    c = math.sqrt(2.0 / math.pi)
    for j in range(N_LANE_TILES):
        cols = slice(j * LANES, (j + 1) * LANES)
        y = y_s[j] + d_ref[:, cols] * h[:, cols]
        gelu = 0.5 * y * (1.0 + jnp.tanh(c * (y + 0.044715 * (y * y * y))))
        o_ref[:, cols] = gelu.astype(BF16)


def _ssm_post(y_c, x, g, d):
    spec = pl.BlockSpec((BM, D_MODEL), lambda m: (m, 0))
    vec = pl.BlockSpec((1, D_MODEL), lambda m: (0, 0))
    return pl.pallas_call(
        _ssm_post_kernel,
        out_shape=jax.ShapeDtypeStruct((M_TOTAL, D_MODEL), BF16),
        grid=(N_ROW_TILES,),
        in_specs=[pl.BlockSpec((N_LANE_TILES, CHUNKS_PER_TILE, SSM_CHUNK_COLS), lambda m: (0, m, 0)),
                  spec, vec, vec],
        out_specs=spec,
        scratch_shapes=[pltpu.VMEM((N_LANE_TILES, BM, LANES), F32)],
        compiler_params=_params(1),
        name="ssm_post",
    )(y_c, x, g.reshape(1, D_MODEL), d.reshape(1, D_MODEL))


def kernel(x_prompt, x_sample, state_ret, state_ssm_re, state_ssm_im, cache_conv, norm_mix, norm_ffn, norm_final, ret_w_in, ret_gn, ret_w_out, ssm_a_re, ssm_a_im, ssm_log_dt, ssm_b_re, ssm_b_im, ssm_c_re, ssm_c_im, ssm_d, ssm_w_glu, ffn_w_up, ffn_conv_w, ffn_conv_b, ffn_w_down):
    x = jnp.concatenate([x_prompt.reshape(M_PROMPT, D_MODEL), x_sample.reshape(M_SAMPLE, D_MODEL)], axis=0)
    cos, sin = _rope_tables()

    ret_p, ret_s = [], []
    re_p, im_p, re_s, im_s = [], [], [], []
    conv_p, conv_s = [], []
    state_row = SSM_GROUPS * SSM_STATE
    for i in range(DEPTH):
        j = i // 2
        if i % 2 == 0:
            h = _rmsnorm(x, norm_mix[i], BF16)
            qkvg = _mm_retin(h, ret_w_in[j], cos, sin)
            o_p, s_p = _retention(qkvg, ret_gn[j], None, row0=0, n_seq=1,
                                  n_blk=M_PROMPT // RET_T, T=RET_T)
            o_s, s_s = _retention(qkvg, ret_gn[j], state_ret[j], row0=M_PROMPT, n_seq=DEC_BATCH,
                                  n_blk=1, T=DEC_SEQ)
            ret_p.append(s_p)
            ret_s.append(s_s)
            x = _mm_res(o_p, ret_w_out[j], x, a_sample=o_s)
        else:
            h_c = _rmsnorm_chunked(x, norm_mix[i])
            bc, cc, lag, a16r, a16i = _ssm_prep(ssm_a_re[j], ssm_a_im[j], ssm_log_dt[j],
                                               ssm_b_re[j], ssm_b_im[j], ssm_c_re[j], ssm_c_im[j])
            y_c, fpr, fpi, fsr, fsi = _ssm_main(
                h_c, lag, bc, cc, a16r, a16i,
                state_ssm_re[j].reshape(DEC_BATCH, state_row),
                state_ssm_im[j].reshape(DEC_BATCH, state_row))
            re_p.append(fpr[0:1].reshape(1, SSM_GROUPS, SSM_STATE))
            im_p.append(fpi[0:1].reshape(1, SSM_GROUPS, SSM_STATE))
            re_s.append(fsr.reshape(DEC_BATCH, SSM_GROUPS, SSM_STATE))
            im_s.append(fsi.reshape(DEC_BATCH, SSM_GROUPS, SSM_STATE))
            gl = _ssm_post(y_c, x, norm_mix[i], ssm_d[j])
            x = _mm_glu(gl, ssm_w_glu[j], x)

        h = _rmsnorm(x, norm_ffn[i], BF16)
        u, tail_p, tail_s = _mm_ffnup(h, ffn_w_up[i], ffn_conv_w[i], ffn_conv_b[i], cache_conv[i])
        conv_p.append(tail_p[SUBLANES - (CONV_W - 1):].reshape(1, CONV_W - 1, FFN_DIM))
        conv_s.append(tail_s.reshape(DEC_BATCH, SUBLANES, FFN_DIM)[:, SUBLANES - (CONV_W - 1):])
        x = _mm_res(u, ffn_w_down[i], x)

    y_prompt = _rmsnorm(x, norm_final, F32, 0, N_PROMPT_TILES).reshape(1, SEQ, D_MODEL)
    y_sample = _rmsnorm(x, norm_final, F32, N_PROMPT_TILES, 1).reshape(DEC_BATCH, DEC_SEQ, D_MODEL)
    return (y_prompt, y_sample, jnp.stack(ret_p), jnp.stack(ret_s),
            jnp.stack(re_p), jnp.stack(im_p), jnp.stack(re_s), jnp.stack(im_s),
            jnp.stack(conv_p), jnp.stack(conv_s))
```

```python
import functools
import math

import numpy as np
import jax
import jax.numpy as jnp
from jax import lax
from jax.experimental import pallas as pl
from jax.experimental.pallas import tpu as pltpu

F32 = jnp.float32
BF16 = jnp.bfloat16

D_MODEL = 2048
SEQ = 8192
DEPTH = 4
DEC_BATCH = 8
DEC_SEQ = 64
PAST_LEN = 1024
CHUNK = 64
RET_HEADS = 8
RET_DK = D_MODEL // RET_HEADS
RET_DV = 2 * RET_DK
RET_QK = RET_HEADS * RET_DK
RET_VD = RET_HEADS * RET_DV
ROPE_BASE = 10000.0
SSM_GROUP = 16
SSM_GROUPS = D_MODEL // SSM_GROUP
SSM_STATE = 64
FFN_DIM = 2 * D_MODEL
CONV_W = 3
EPS = 1e-6

M_PROMPT = SEQ
M_SAMPLE = DEC_BATCH * DEC_SEQ
M_TOTAL = M_PROMPT + M_SAMPLE

LANES = 128
SUBLANES = 8
VMEM_LIMIT_BYTES = 56 * 1024 * 1024

BM = M_SAMPLE
N_ROW_TILES = M_TOTAL // BM
N_PROMPT_TILES = M_PROMPT // BM

RET_T = 256

SSM_T = 16
SSM_PAIRS = SSM_GROUPS // 2
SSM_PB = 4
SSM_SEGS = 8
SSM_STEPS_P = M_PROMPT // SSM_T // SSM_SEGS
SSM_STEPS_S = DEC_SEQ // SSM_T
SSM_NC = (SSM_STEPS_P + SSM_STEPS_S) * SSM_SEGS


def _params(n_axes):
    return pltpu.CompilerParams(dimension_semantics=("arbitrary",) * n_axes,
                                vmem_limit_bytes=VMEM_LIMIT_BYTES)


def _sigmoid(x):
    return 1.0 / (1.0 + jnp.exp(-x))


def _rope_kernel(cos_ref, sin_ref):
    m = pl.program_id(0)
    half = RET_DK // 2
    freq = lax.broadcasted_iota(jnp.int32, (1, half), 1).astype(F32)
    inv = ROPE_BASE ** (-freq / half)
    r = m * BM + lax.broadcasted_iota(jnp.int32, (BM, half), 0)
    pos = jnp.where(r < M_PROMPT, r, PAST_LEN + ((r - M_PROMPT) & (DEC_SEQ - 1)))
    ang = pos.astype(F32) * inv
    cos_ref[...] = jnp.cos(ang)
    sin_ref[...] = jnp.sin(ang)


def _rope_tables():
    half = RET_DK // 2
    spec = pl.BlockSpec((BM, half), lambda m: (m, 0))
    return pl.pallas_call(
        _rope_kernel,
        out_shape=(jax.ShapeDtypeStruct((M_TOTAL, half), F32),) * 2,
        grid=(N_ROW_TILES,),
        out_specs=(spec, spec),
        compiler_params=_params(1),
        name="rope_tables",
    )()


def _norm_kernel(x_ref, g_ref, o_ref):
    x = x_ref[...]
    ms = jnp.mean(x * x, axis=-1, keepdims=True)
    o_ref[...] = ((x * lax.rsqrt(ms + EPS)) * g_ref[...]).astype(o_ref.dtype)


def _rmsnorm(x, g, out_dtype, tile0=0, n_tiles=N_ROW_TILES):
    return pl.pallas_call(
        _norm_kernel,
        out_shape=jax.ShapeDtypeStruct((n_tiles * BM, D_MODEL), out_dtype),
        grid=(n_tiles,),
        in_specs=[pl.BlockSpec((BM, D_MODEL), lambda m: (tile0 + m, 0)),
                  pl.BlockSpec((1, D_MODEL), lambda m: (0, 0))],
        out_specs=pl.BlockSpec((BM, D_MODEL), lambda m: (m, 0)),
        compiler_params=_params(1),
        name="rmsnorm",
    )(x, g.reshape(1, D_MODEL))


CHUNKS_PER_TILE = BM // 16
N_CHUNKS = M_TOTAL // 16
N_LANE_TILES = D_MODEL // LANES


def _norm_chunked_kernel(x_ref, g_ref, o_ref, h_s):
    x = x_ref[...]
    ms = jnp.mean(x * x, axis=-1, keepdims=True)
    h = (x * lax.rsqrt(ms + EPS)) * g_ref[...]
    for j in range(N_LANE_TILES):
        h_s[j] = h[:, j * LANES:(j + 1) * LANES]
    for t in range(SSM_T):
        for j in range(N_LANE_TILES):
            rows = h_s[j, pl.ds(t, CHUNKS_PER_TILE, stride=SSM_T), :]
            o_ref[j, :, t * LANES:(t + 1) * LANES] = rows.astype(BF16)


def _rmsnorm_chunked(x, g):
    return pl.pallas_call(
        _norm_chunked_kernel,
        out_shape=jax.ShapeDtypeStruct((N_LANE_TILES, N_CHUNKS, SSM_T * LANES), BF16),
        grid=(N_ROW_TILES,),
        in_specs=[pl.BlockSpec((BM, D_MODEL), lambda m: (m, 0)),
                  pl.BlockSpec((1, D_MODEL), lambda m: (0, 0))],
        out_specs=pl.BlockSpec((N_LANE_TILES, CHUNKS_PER_TILE, SSM_T * LANES), lambda m: (0, m, 0)),
        scratch_shapes=[pltpu.VMEM((N_LANE_TILES, BM, LANES), F32)],
        compiler_params=_params(1),
        name="rmsnorm_chunked",
    )(x, g.reshape(1, D_MODEL))


def _weight_spec(k, bn, layer, col0=0):
    return pl.BlockSpec((None, k, bn), lambda n, m: (layer, 0, col0 + n), pipeline_mode=pl.Buffered(1))


def _mm_retin_kernel(a_ref, w_ref, cos_ref, sin_ref, o_ref, wb_ref, *, bn):
    n = pl.program_id(0)
    m = pl.program_id(1)
    n_q = RET_QK // bn
    n_rot = 2 * n_q

    @pl.when(m == 0)
    def _():
        wb_ref[...] = w_ref[...].astype(BF16)

    acc = jnp.dot(a_ref[...], wb_ref[...], preferred_element_type=F32)

    @pl.when(n < n_rot)
    def _():
        scale = jnp.where(n >= n_q, RET_DK ** -0.5, 1.0).astype(F32)
        c = cos_ref[...]
        s = sin_ref[...]
        half = RET_DK // 2
        for hh in range(bn // RET_DK):
            lo = hh * RET_DK
            x1 = acc[:, lo:lo + half]
            x2 = acc[:, lo + half:lo + RET_DK]
            o_ref[:, lo:lo + half] = ((x1 * c - x2 * s) * scale).astype(BF16)
            o_ref[:, lo + half:lo + RET_DK] = ((x1 * s + x2 * c) * scale).astype(BF16)

    @pl.when(n >= n_rot)
    def _():
        o_ref[...] = acc.astype(BF16)


def _mm_retin(h, w, layer, cos, sin):
    bn = 2048
    _, k, n_out = w.shape
    half = RET_DK // 2
    return pl.pallas_call(
        functools.partial(_mm_retin_kernel, bn=bn),
        out_shape=jax.ShapeDtypeStruct((M_TOTAL, n_out), BF16),
        grid=(n_out // bn, N_ROW_TILES),
        in_specs=[
            pl.BlockSpec((BM, k), lambda n, m: (m, 0)),
            _weight_spec(k, bn, layer),
            pl.BlockSpec((BM, half), lambda n, m: (m, 0)),
            pl.BlockSpec((BM, half), lambda n, m: (m, 0)),
        ],
        out_specs=pl.BlockSpec((BM, bn), lambda n, m: (m, n)),
        scratch_shapes=[pltpu.VMEM((k, bn), BF16)],
        compiler_params=_params(2),
        name="mm_ret_in",
    )(h, w, cos, sin)


def _mm_res_kernel(*refs, split):
    if split:
        ap_ref, as_ref, w_ref, x_ref, o_ref, wb_ref = refs
    else:
        ap_ref, w_ref, x_ref, o_ref, wb_ref = refs
        as_ref = ap_ref
    m = pl.program_id(1)

    @pl.when(m == 0)
    def _():
        wb_ref[...] = w_ref[...].astype(BF16)

    def body(a_ref):
        o_ref[...] = x_ref[...] + jnp.dot(a_ref[...], wb_ref[...], preferred_element_type=F32)

    if split:
        pl.when(m < N_PROMPT_TILES)(lambda: body(ap_ref))
        pl.when(m == N_PROMPT_TILES)(lambda: body(as_ref))
    else:
        body(ap_ref)


def _mm_res(a, w, layer, x, a_sample=None):
    bn = 1024
    _, k, n_out = w.shape
    split = a_sample is not None
    if split:
        a_specs = [pl.BlockSpec((BM, k), lambda n, m: (jnp.minimum(m, N_PROMPT_TILES - 1), 0)),
                   pl.BlockSpec((BM, k), lambda n, m: (0, 0))]
        a_args = [a, a_sample]
    else:
        a_specs = [pl.BlockSpec((BM, k), lambda n, m: (m, 0))]
        a_args = [a]
    return pl.pallas_call(
        functools.partial(_mm_res_kernel, split=split),
        out_shape=jax.ShapeDtypeStruct((M_TOTAL, n_out), F32),
        grid=(n_out // bn, N_ROW_TILES),
        in_specs=a_specs + [
            _weight_spec(k, bn, layer),
            pl.BlockSpec((BM, bn), lambda n, m: (m, n)),
        ],
        out_specs=pl.BlockSpec((BM, bn), lambda n, m: (m, n)),
        scratch_shapes=[pltpu.VMEM((k, bn), BF16)],
        compiler_params=_params(2),
        name="mm_residual",
    )(*a_args, w, x)


def _mm_glu_kernel(a_ref, wa_ref, wg_ref, x_ref, o_ref, wab_ref, wgb_ref):
    @pl.when(pl.program_id(1) == 0)
    def _():
        wab_ref[...] = wa_ref[...].astype(BF16)
        wgb_ref[...] = wg_ref[...].astype(BF16)

    a = a_ref[...]
    ga = jnp.dot(a, wab_ref[...], preferred_element_type=F32)
    gb = jnp.dot(a, wgb_ref[...], preferred_element_type=F32)
    o_ref[...] = x_ref[...] + ga * _sigmoid(gb)


def _mm_glu(a, w, layer, x):
    bn = 1024
    _, k, n2 = w.shape
    n_out = n2 // 2
    nb = n_out // bn
    return pl.pallas_call(
        _mm_glu_kernel,
        out_shape=jax.ShapeDtypeStruct((M_TOTAL, n_out), F32),
        grid=(nb, N_ROW_TILES),
        in_specs=[
            pl.BlockSpec((BM, k), lambda n, m: (m, 0)),
            _weight_spec(k, bn, layer),
            _weight_spec(k, bn, layer, nb),
            pl.BlockSpec((BM, bn), lambda n, m: (m, n)),
        ],
        out_specs=pl.BlockSpec((BM, bn), lambda n, m: (m, n)),
        scratch_shapes=[pltpu.VMEM((k, bn), BF16), pltpu.VMEM((k, bn), BF16)],
        compiler_params=_params(2),
        name="mm_glu",
    )(a, w, w, x)


def _mm_ffnup_kernel(h_ref, wa_ref, wg_ref, cw_ref, cb_ref, cache_ref,
                     o_ref, tp_ref, ts_ref, wab_ref, wgb_ref, carry_ref):
    m = pl.program_id(1)

    @pl.when(m == 0)
    def _():
        wab_ref[...] = wa_ref[...].astype(BF16)
        wgb_ref[...] = wg_ref[...].astype(BF16)
        carry_ref[...] = jnp.zeros_like(carry_ref)

    h = h_ref[...]
    a = jnp.dot(h, wab_ref[...], preferred_element_type=F32)
    b = jnp.dot(h, wgb_ref[...], preferred_element_type=F32)
    r1 = pltpu.roll(a, 1, axis=0)
    r2 = pltpu.roll(a, 2, axis=0)
    cb = cb_ref[...]
    w0 = cw_ref[0:1, :]
    w1 = cw_ref[1:2, :]
    w2 = cw_ref[2:3, :]

    def gated(a_rows, prev1, prev2, b_rows):
        conv = cb + w0 * prev2 + w1 * prev1 + w2 * a_rows
        return ((conv * _sigmoid(conv)) * b_rows).astype(BF16)

    o_ref[...] = gated(a, r1, r2, b)

    row8 = lax.broadcasted_iota(jnp.int32, (SUBLANES, 1), 0)

    def redo_head(r0, before1, before2):
        rows = slice(r0, r0 + SUBLANES)
        prev1 = jnp.where(row8 == 0, before1, r1[rows])
        prev2 = jnp.where(row8 == 0, before2, jnp.where(row8 == 1, before1, r2[rows]))
        o_ref[rows, :] = gated(a[rows], prev1, prev2, b[rows])

    @pl.when(m < N_PROMPT_TILES)
    def _():
        c = carry_ref[...]
        redo_head(0, c[SUBLANES - 1:SUBLANES, :], c[SUBLANES - 2:SUBLANES - 1, :])
        carry_ref[...] = a[BM - SUBLANES:, :]

        @pl.when(m == N_PROMPT_TILES - 1)
        def _():
            tp_ref[...] = a[BM - SUBLANES:, :]

    @pl.when(m == N_PROMPT_TILES)
    def _():
        for s in range(DEC_BATCH):
            c = cache_ref[s]
            redo_head(s * DEC_SEQ, c[1:2, :], c[0:1, :])
            ts_ref[s * SUBLANES:(s + 1) * SUBLANES, :] = a[(s + 1) * DEC_SEQ - SUBLANES:(s + 1) * DEC_SEQ, :]


def _mm_ffnup(h, w, layer, conv_w, conv_b, cache):
    bn = 1024
    k = w.shape[1]
    nb = FFN_DIM // bn
    return pl.pallas_call(
        _mm_ffnup_kernel,
        out_shape=(
            jax.ShapeDtypeStruct((M_TOTAL, FFN_DIM), BF16),
            jax.ShapeDtypeStruct((SUBLANES, FFN_DIM), F32),
            jax.ShapeDtypeStruct((DEC_BATCH * SUBLANES, FFN_DIM), F32),
        ),
        grid=(nb, N_ROW_TILES),
        in_specs=[
            pl.BlockSpec((BM, k), lambda n, m: (m, 0)),
            _weight_spec(k, bn, layer),
            _weight_spec(k, bn, layer, nb),
            pl.BlockSpec((CONV_W, bn), lambda n, m: (0, n)),
            pl.BlockSpec((1, bn), lambda n, m: (0, n)),
            pl.BlockSpec((DEC_BATCH, CONV_W - 1, bn), lambda n, m: (0, 0, n)),
        ],
        out_specs=(
            pl.BlockSpec((BM, bn), lambda n, m: (m, n)),
            pl.BlockSpec((SUBLANES, bn), lambda n, m: (0, n)),
            pl.BlockSpec((DEC_BATCH * SUBLANES, bn), lambda n, m: (0, n)),
        ),
        scratch_shapes=[pltpu.VMEM((k, bn), BF16), pltpu.VMEM((k, bn), BF16),
                        pltpu.VMEM((SUBLANES, bn), F32)],
        compiler_params=_params(2),
        name="mm_ffn_up",
    )(h, w, w, conv_w, conv_b.reshape(1, FFN_DIM), cache)


def _ret_kernel(*refs, T, has_init):
    if has_init:
        (lg_ref, q_ref, k_ref, v_ref, g_ref, gn_ref, s0_ref,
         o_ref, sl_ref, state_ref, mask_ref) = refs
    else:
        (lg_ref, q_ref, k_ref, v_ref, g_ref, gn_ref,
         o_ref, sl_ref, state_ref, mask_ref) = refs
    blk = pl.program_id(2)
    lg = lg_ref[...][:, 0:1]

    @pl.when(blk == 0)
    def _():
        if has_init:
            state_ref[...] = s0_ref[...]
        else:
            state_ref[...] = jnp.zeros_like(state_ref)
        i = lax.broadcasted_iota(jnp.int32, (T, T), 0)
        j = lax.broadcasted_iota(jnp.int32, (T, T), 1)
        dist = jnp.abs(i - j).astype(F32)
        shift = int(math.log2(CHUNK))
        visible = (j >> shift) <= (i >> shift)
        mask_ref[...] = jnp.where(visible, jnp.exp(dist * lg), 0.0)

    q = q_ref[...]
    k = k_ref[...]
    v = v_ref[...]
    state = state_ref[...]
    t = lax.broadcasted_iota(jnp.int32, (T, 1), 0).astype(F32)

    scores = lax.dot_general(q, k, (((1,), (1,)), ((), ())), preferred_element_type=F32)
    scores = scores * mask_ref[...]
    out = jnp.dot(scores.astype(BF16), v, preferred_element_type=F32)
    cross = jnp.exp((t + 1.0) * lg)
    out = out + jnp.dot(q, state.astype(BF16), preferred_element_type=F32) * cross

    k_dec = jnp.exp((T - 1.0 - t) * lg)
    kd = (k.astype(F32) * k_dec).astype(BF16)
    new_state = jnp.exp(float(T) * lg) * state + lax.dot_general(
        kd, v, (((0,), (0,)), ((), ())), preferred_element_type=F32)
    state_ref[...] = new_state

    mu = jnp.mean(out, axis=-1, keepdims=True)
    oc = out - mu
    var = jnp.mean(oc * oc, axis=-1, keepdims=True)
    normed = (oc * lax.rsqrt(var + EPS)) * gn_ref[...]
    g = g_ref[...].astype(F32)
    o_ref[...] = (normed * (g * _sigmoid(g))).astype(BF16)

    @pl.when(blk == pl.num_programs(2) - 1)
    def _():
        sl_ref[...] = new_state


def _retention(qkvg, gn, s0, *, row0, n_seq, n_blk, T):
    log_g = np.log1p(-np.exp2(-5.0 - np.arange(RET_HEADS, dtype=np.float32))).astype(np.float32)
    lg = jnp.asarray(np.broadcast_to(log_g[:, None, None], (RET_HEADS, 1, LANES)).copy())
    rb0 = row0 // T
    k_off = RET_QK // RET_DK
    v_off = 2 * RET_QK // RET_DV
    g_off = v_off + RET_VD // RET_DV
    has_init = s0 is not None

    def rows(h, s, b):
        return rb0 + s * n_blk + b

    in_specs = [
        pl.BlockSpec((None, 1, LANES), lambda h, s, b: (h, 0, 0)),
        pl.BlockSpec((T, RET_DK), lambda h, s, b: (rows(h, s, b), h)),
        pl.BlockSpec((T, RET_DK), lambda h, s, b: (rows(h, s, b), k_off + h)),
        pl.BlockSpec((T, RET_DV), lambda h, s, b: (rows(h, s, b), v_off + h)),
        pl.BlockSpec((T, RET_DV), lambda h, s, b: (rows(h, s, b), g_off + h)),
        pl.BlockSpec((1, RET_DV), lambda h, s, b: (0, h)),
    ]
    args = [lg, qkvg, qkvg, qkvg, qkvg, gn.reshape(1, RET_VD)]
    if has_init:
        s0_stack, layer = s0
        in_specs.append(pl.BlockSpec((None, None, None, RET_DK, RET_DV),
                                     lambda h, s, b: (layer, s, h, 0, 0)))
        args.append(s0_stack)
    return pl.pallas_call(
        functools.partial(_ret_kernel, T=T, has_init=has_init),
        out_shape=(
            jax.ShapeDtypeStruct((n_seq * n_blk * T, RET_VD), BF16),
            jax.ShapeDtypeStruct((n_seq, RET_HEADS, RET_DK, RET_DV), F32),
        ),
        grid=(RET_HEADS, n_seq, n_blk),
        in_specs=in_specs,
        out_specs=(
            pl.BlockSpec((T, RET_DV), lambda h, s, b: (s * n_blk + b, h)),
            pl.BlockSpec((None, None, RET_DK, RET_DV), lambda h, s, b: (s, h, 0, 0)),
        ),
        scratch_shapes=[pltpu.VMEM((RET_DK, RET_DV), F32), pltpu.VMEM((T, T), F32)],
        compiler_params=_params(3),
        name="retention_init" if has_init else "retention",
    )(*args)


def _ssm_prep_kernel(ar_ref, ai_ref, ldt_ref, br_ref, bi_ref, cr_ref, ci_ref,
                     bc_ref, cc_ref, lag_ref, a16r_ref, a16i_ref, vr_s, vi_s):
    ar = ar_ref[...]
    ai = ai_ref[...]
    dt = jnp.exp(ldt_ref[...])
    mag = jnp.exp(ar * dt)
    ang = ai * dt
    abr = mag * jnp.cos(ang)
    abi = mag * jnp.sin(ang)
    den = ar * ar + ai * ai
    nr = abr - 1.0
    ni = abi
    cfr = (nr * ar + ni * ai) / den
    cfi = (ni * ar - nr * ai) / den

    def b4(z):
        return z[:, None]

    br = br_ref[...]
    bi = bi_ref[...]
    bbr = b4(cfr) * br - b4(cfi) * bi
    bbi = b4(cfr) * bi + b4(cfi) * br
    cre = cr_ref[...]
    cim = ci_ref[...]

    powers = []
    pr = jnp.ones_like(ar)
    pi = jnp.zeros_like(ar)
    for _ in range(SSM_T + 1):
        powers.append((pr, pi))
        pr, pi = pr * abr - pi * abi, pr * abi + pi * abr
    a16r_ref[...] = powers[SSM_T][0]
    a16i_ref[...] = powers[SSM_T][1]

    for m in range(SSM_T + 1):
        pr, pi = powers[m]
        vr = b4(pr) * cre - b4(pi) * cim
        vi = b4(pr) * cim + b4(pi) * cre
        if m < SSM_T:
            vr_s[:, :, m] = vr
            vi_s[:, :, m] = vi
        if m >= 1:
            cc_ref[:, :, m - 1, :, 0:LANES] = vr.astype(BF16)
            cc_ref[:, :, m - 1, :, LANES:2 * LANES] = (-vi).astype(BF16)

    for t in range(SSM_T):
        pr, pi = powers[SSM_T - 1 - t]
        bc_ref[:, :, t, :, 0:LANES] = (b4(pr) * bbr - b4(pi) * bbi).astype(BF16)
        bc_ref[:, :, t, :, LANES:2 * LANES] = (b4(pr) * bbi + b4(pi) * bbr).astype(BF16)

    width = SSM_T * SSM_GROUP
    lane = lax.broadcasted_iota(jnp.int32, (SSM_GROUP, LANES), 1)
    contract_last = (((1,), (1,)), ((), ()))
    for q in range(SSM_PB):
        for half in range(2):
            gl = 2 * q + half
            vr_all = vr_s[q, half].reshape(width, LANES)
            vi_all = vi_s[q, half].reshape(width, LANES)
            krow = (lax.dot_general(bbr[q, half], vr_all, contract_last,
                                    precision=lax.Precision.HIGHEST, preferred_element_type=F32)
                    - lax.dot_general(bbi[q, half], vi_all, contract_last,
                                      precision=lax.Precision.HIGHEST, preferred_element_type=F32))
            in_block = (lane >= gl * SSM_GROUP) & (lane < (gl + 1) * SSM_GROUP)
            for m in range(SSM_T):
                shift = ((gl - m) * SSM_GROUP) % width
                moved = krow if shift == 0 else pltpu.roll(krow, shift, axis=1)
                lag_ref[m, gl * SSM_GROUP:(gl + 1) * SSM_GROUP, :] = jnp.where(
                    in_block, moved[:, 0:LANES], 0.0).astype(BF16)


def _pair_pack(x):
    x = x.reshape(SSM_PAIRS, 2, SSM_GROUP, SSM_STATE)
    lo = jnp.pad(x[:, 0], ((0, 0), (0, 0), (0, SSM_STATE)))
    hi = jnp.pad(x[:, 1], ((0, 0), (0, 0), (SSM_STATE, 0)))
    return jnp.stack([lo, hi], axis=1)


def _ssm_prep(a_re, a_im, log_dt, b_re, b_im, c_re, c_im):
    assert 2 * SSM_PB * SSM_GROUP == LANES
    ar = a_re.reshape(SSM_PAIRS, 1, LANES)
    ai = a_im.reshape(SSM_PAIRS, 1, LANES)
    ldt = jnp.repeat(log_dt, SSM_STATE).reshape(SSM_PAIRS, 1, LANES)
    b2r = _pair_pack(jnp.transpose(b_re, (0, 2, 1)))
    b2i = _pair_pack(jnp.transpose(b_im, (0, 2, 1)))
    c2r = _pair_pack(c_re)
    c2i = _pair_pack(c_im)
    vec = pl.BlockSpec((SSM_PB, 1, LANES), lambda p: (p, 0, 0))
    mat = pl.BlockSpec((SSM_PB, 2, SSM_GROUP, LANES), lambda p: (p, 0, 0, 0))
    proj = pl.BlockSpec((SSM_PB, 2, SSM_T, SSM_GROUP, 2 * LANES), lambda p: (p, 0, 0, 0, 0))
    bc, cc, lag, a16r, a16i = pl.pallas_call(
        _ssm_prep_kernel,
        out_shape=(
            jax.ShapeDtypeStruct((SSM_PAIRS, 2, SSM_T, SSM_GROUP, 2 * LANES), BF16),
            jax.ShapeDtypeStruct((SSM_PAIRS, 2, SSM_T, SSM_GROUP, 2 * LANES), BF16),
            jax.ShapeDtypeStruct((N_LANE_TILES, SSM_T, LANES, LANES), BF16),
            jax.ShapeDtypeStruct((SSM_PAIRS, 1, LANES), F32),
            jax.ShapeDtypeStruct((SSM_PAIRS, 1, LANES), F32),
        ),
        grid=(N_LANE_TILES,),
        in_specs=[vec, vec, vec, mat, mat, mat, mat],
        out_specs=(proj, proj,
                   pl.BlockSpec((None, SSM_T, LANES, LANES), lambda p: (p, 0, 0, 0)),
                   vec, vec),
        scratch_shapes=[pltpu.VMEM((SSM_PB, 2, SSM_T, SSM_GROUP, LANES), F32)] * 2,
        compiler_params=_params(1),
        name="ssm_prep",
    )(ar, ai, ldt, b2r, b2i, c2r, c2i)
    row = SSM_GROUPS * SSM_STATE
    a16r = jnp.broadcast_to(a16r.reshape(1, row), (SSM_SEGS, row))
    a16i = jnp.broadcast_to(a16i.reshape(1, row), (SSM_SEGS, row))
    return bc, cc, lag, a16r, a16i


SSM_CHUNK_COLS = SSM_T * LANES
SSM_STATE_COLS = SSM_PB * LANES
SSM_CHUNKS_P = M_PROMPT // SSM_T


def _ssm_main_kernel(x_ref, lag_ref, bc_ref, cc_ref, ar_ref, ai_ref, h0r_ref, h0i_ref,
                     y_ref, fpr_ref, fpi_ref, fsr_ref, fsi_ref,
                     w_s, bcw_s, ccw_s, d_s, s_s):
    w = SSM_STATE_COLS

    @pl.when(pl.program_id(0) == 0)
    def _():
        w_s[...] = jnp.zeros_like(w_s)
        bcw_s[...] = jnp.zeros_like(bcw_s)
        ccw_s[...] = jnp.zeros_like(ccw_s)

    for t in range(SSM_T):
        for t2 in range(t, SSM_T):
            w_s[t * LANES:(t + 1) * LANES, t2 * LANES:(t2 + 1) * LANES] = lag_ref[t2 - t]
    for q in range(SSM_PB):
        for half in range(2):
            for t in range(SSM_T):
                r0 = t * LANES + (2 * q + half) * SSM_GROUP
                rows = slice(r0, r0 + SSM_GROUP)
                bcw_s[rows, q * LANES:(q + 1) * LANES] = bc_ref[q, half, t, :, 0:LANES]
                bcw_s[rows, w + q * LANES:w + (q + 1) * LANES] = bc_ref[q, half, t, :, LANES:2 * LANES]
                ccw_s[rows, q * LANES:(q + 1) * LANES] = cc_ref[q, half, t, :, 0:LANES]
                ccw_s[rows, w + q * LANES:w + (q + 1) * LANES] = cc_ref[q, half, t, :, LANES:2 * LANES]

    x = x_ref[...]
    d = jnp.dot(x, bcw_s[...], preferred_element_type=F32)
    for k in range(2 * SSM_PB):
        d_s[k] = d[:, k * LANES:(k + 1) * LANES]

    for k in range(SSM_PB):
        cols = slice(k * LANES, (k + 1) * LANES)
        ar = ar_ref[:, cols]
        ai = ai_ref[:, cols]

        def scan(sr, si, rows):
            s_s[k, rows, :] = sr
            s_s[SSM_PB + k, rows, :] = si
            dr = d_s[k, rows, :]
            di = d_s[SSM_PB + k, rows, :]
            return ar * sr - ai * si + dr, ar * si + ai * sr + di

        sr = jnp.zeros((SSM_SEGS, LANES), F32)
        si = sr
        for i in range(SSM_STEPS_P):
            sr, si = scan(sr, si, pl.ds(i, SSM_SEGS, stride=SSM_STEPS_P))
        er, ei = sr, si
        sr = h0r_ref[:, cols]
        si = h0i_ref[:, cols]
        for i in range(SSM_STEPS_S):
            sr, si = scan(sr, si, pl.ds(SSM_CHUNKS_P + i, DEC_BATCH, stride=SSM_STEPS_S))
        fsr_ref[:, cols] = sr
        fsi_ref[:, cols] = si

        pr, pi = ar[0:1], ai[0:1]
        for _ in range(int(math.log2(SSM_STEPS_P))):
            pr, pi = pr * pr - pi * pi, 2.0 * pr * pi
        cr = jnp.zeros((1, LANES), F32)
        ci = cr
        starts_r, starts_i = [], []
        for s in range(SSM_SEGS):
            starts_r.append(cr)
            starts_i.append(ci)
            cr, ci = (pr * cr - pi * ci + er[s:s + 1], pr * ci + pi * cr + ei[s:s + 1])
        fpr_ref[:, cols] = jnp.broadcast_to(cr, (SSM_SEGS, LANES))
        fpi_ref[:, cols] = jnp.broadcast_to(ci, (SSM_SEGS, LANES))
        seg_r = jnp.concatenate(starts_r, axis=0)
        seg_i = jnp.concatenate(starts_i, axis=0)

        qr = jnp.ones((SSM_SEGS, LANES), F32)
        qi = jnp.zeros((SSM_SEGS, LANES), F32)
        for i in range(SSM_STEPS_P):
            rows = pl.ds(i, SSM_SEGS, stride=SSM_STEPS_P)
            s_s[k, rows, :] = s_s[k, rows, :] + (qr * seg_r - qi * seg_i)
            s_s[SSM_PB + k, rows, :] = s_s[SSM_PB + k, rows, :] + (qr * seg_i + qi * seg_r)
            qr, qi = qr * ar - qi * ai, qr * ai + qi * ar

    states = jnp.concatenate([s_s[k].astype(BF16) for k in range(2 * SSM_PB)], axis=1)
    contract_last = (((1,), (1,)), ((), ()))
    carried = lax.dot_general(states, ccw_s[...], contract_last, preferred_element_type=F32)
    y_ref[...] = jnp.dot(x, w_s[...], preferred_element_type=F32) + carried


def _ssm_main(x_c, lag, bc, cc, a16r, a16i, h0r, h0i):
    w = SSM_STATE_COLS
    row = SSM_GROUPS * SSM_STATE
    st = pl.BlockSpec((SSM_SEGS, w), lambda j: (0, j))
    st_shape = jax.ShapeDtypeStruct((SSM_SEGS, row), F32)
    proj = pl.BlockSpec((SSM_PB, 2, SSM_T, SSM_GROUP, 2 * LANES), lambda j: (j, 0, 0, 0, 0))
    chunk_rows = pl.BlockSpec((None, N_CHUNKS, SSM_CHUNK_COLS), lambda j: (j, 0, 0))
    return pl.pallas_call(
        _ssm_main_kernel,
        out_shape=(jax.ShapeDtypeStruct((N_LANE_TILES, N_CHUNKS, SSM_CHUNK_COLS), F32),
                   st_shape, st_shape, st_shape, st_shape),
        grid=(N_LANE_TILES,),
        in_specs=[
            chunk_rows,
            pl.BlockSpec((None, SSM_T, LANES, LANES), lambda j: (j, 0, 0, 0)),
            proj, proj, st, st, st, st,
        ],
        out_specs=(chunk_rows, st, st, st, st),
        scratch_shapes=[
            pltpu.VMEM((SSM_CHUNK_COLS, SSM_CHUNK_COLS), BF16),
            pltpu.VMEM((SSM_CHUNK_COLS, 2 * w), BF16),
            pltpu.VMEM((SSM_CHUNK_COLS, 2 * w), BF16),
            pltpu.VMEM((2 * SSM_PB, N_CHUNKS, LANES), F32),
            pltpu.VMEM((2 * SSM_PB, N_CHUNKS, LANES), F32),
        ],
        compiler_params=_params(1),
        name="ssm_main",
    )(x_c, lag, bc, cc, a16r, a16i, h0r, h0i)


def _ssm_post_kernel(y_ref, x_ref, g_ref, d_ref, o_ref, y_s):
    for t in range(SSM_T):
        for j in range(N_LANE_TILES):
            y_s[j, pl.ds(t, CHUNKS_PER_TILE, stride=SSM_T), :] = y_ref[j, :, t * LANES:(t + 1) * LANES]
    x = x_ref[...]
    ms = jnp.mean(x * x, axis=-1, keepdims=True)
    h = (x * lax.rsqrt(ms + EPS)) * g_ref[...]
    c = math.sqrt(2.0 / math.pi)
    for j in range(N_LANE_TILES):
        cols = slice(j * LANES, (j + 1) * LANES)
        y = y_s[j] + d_ref[:, cols] * h[:, cols]
        gelu = 0.5 * y * (1.0 + jnp.tanh(c * (y + 0.044715 * (y * y * y))))
        o_ref[:, cols] = gelu.astype(BF16)


def _ssm_post(y_c, x, g, d):
    spec = pl.BlockSpec((BM, D_MODEL), lambda m: (m, 0))
    vec = pl.BlockSpec((1, D_MODEL), lambda m: (0, 0))
    return pl.pallas_call(
        _ssm_post_kernel,
        out_shape=jax.ShapeDtypeStruct((M_TOTAL, D_MODEL), BF16),
        grid=(N_ROW_TILES,),
        in_specs=[pl.BlockSpec((N_LANE_TILES, CHUNKS_PER_TILE, SSM_CHUNK_COLS), lambda m: (0, m, 0)),
                  spec, vec, vec],
        out_specs=spec,
        scratch_shapes=[pltpu.VMEM((N_LANE_TILES, BM, LANES), F32)],
        compiler_params=_params(1),
        name="ssm_post",
    )(y_c, x, g.reshape(1, D_MODEL), d.reshape(1, D_MODEL))


def kernel(x_prompt, x_sample, state_ret, state_ssm_re, state_ssm_im, cache_conv, norm_mix, norm_ffn, norm_final, ret_w_in, ret_gn, ret_w_out, ssm_a_re, ssm_a_im, ssm_log_dt, ssm_b_re, ssm_b_im, ssm_c_re, ssm_c_im, ssm_d, ssm_w_glu, ffn_w_up, ffn_conv_w, ffn_conv_b, ffn_w_down):
    x = jnp.concatenate([x_prompt.reshape(M_PROMPT, D_MODEL), x_sample.reshape(M_SAMPLE, D_MODEL)], axis=0)
    cos, sin = _rope_tables()

    ret_p, ret_s = [], []
    re_p, im_p, re_s, im_s = [], [], [], []
    conv_p, conv_s = [], []
    state_row = SSM_GROUPS * SSM_STATE
    for i in range(DEPTH):
        j = i // 2
        if i % 2 == 0:
            h = _rmsnorm(x, norm_mix[i], BF16)
            qkvg = _mm_retin(h, ret_w_in, j, cos, sin)
            o_p, s_p = _retention(qkvg, ret_gn[j], None, row0=0, n_seq=1,
                                  n_blk=M_PROMPT // RET_T, T=RET_T)
            o_s, s_s = _retention(qkvg, ret_gn[j], (state_ret, j), row0=M_PROMPT, n_seq=DEC_BATCH,
                                  n_blk=1, T=DEC_SEQ)
            ret_p.append(s_p)
            ret_s.append(s_s)
            x = _mm_res(o_p, ret_w_out, j, x, a_sample=o_s)
        else:
            h_c = _rmsnorm_chunked(x, norm_mix[i])
            bc, cc, lag, a16r, a16i = _ssm_prep(ssm_a_re[j], ssm_a_im[j], ssm_log_dt[j],
                                               ssm_b_re[j], ssm_b_im[j], ssm_c_re[j], ssm_c_im[j])
            y_c, fpr, fpi, fsr, fsi = _ssm_main(
                h_c, lag, bc, cc, a16r, a16i,
                state_ssm_re[j].reshape(DEC_BATCH, state_row),
                state_ssm_im[j].reshape(DEC_BATCH, state_row))
            re_p.append(fpr[0:1].reshape(1, SSM_GROUPS, SSM_STATE))
            im_p.append(fpi[0:1].reshape(1, SSM_GROUPS, SSM_STATE))
            re_s.append(fsr.reshape(DEC_BATCH, SSM_GROUPS, SSM_STATE))
            im_s.append(fsi.reshape(DEC_BATCH, SSM_GROUPS, SSM_STATE))
            gl = _ssm_post(y_c, x, norm_mix[i], ssm_d[j])
            x = _mm_glu(gl, ssm_w_glu, j, x)

        h = _rmsnorm(x, norm_ffn[i], BF16)
        u, tail_p, tail_s = _mm_ffnup(h, ffn_w_up, i, ffn_conv_w[i], ffn_conv_b[i], cache_conv[i])
        conv_p.append(tail_p[SUBLANES - (CONV_W - 1):].reshape(1, CONV_W - 1, FFN_DIM))
        conv_s.append(tail_s.reshape(DEC_BATCH, SUBLANES, FFN_DIM)[:, SUBLANES - (CONV_W - 1):])
        x = _mm_res(u, ffn_w_down, i, x)

    y_prompt = _rmsnorm(x, norm_final, F32, 0, N_PROMPT_TILES).reshape(1, SEQ, D_MODEL)
    y_sample = _rmsnorm(x, norm_final, F32, N_PROMPT_TILES, 1).reshape(DEC_BATCH, DEC_SEQ, D_MODEL)
    return (y_prompt, y_sample, jnp.stack(ret_p), jnp.stack(ret_s),
            jnp.stack(re_p), jnp.stack(im_p), jnp.stack(re_s), jnp.stack(im_s),
            jnp.stack(conv_p), jnp.stack(conv_s))
```

```python
import functools
import math

import numpy as np
import jax
import jax.numpy as jnp
from jax import lax
from jax.experimental import pallas as pl
from jax.experimental.pallas import tpu as pltpu

F32 = jnp.float32
BF16 = jnp.bfloat16

D_MODEL = 2048
SEQ = 8192
DEPTH = 4
DEC_BATCH = 8
DEC_SEQ = 64
PAST_LEN = 1024
CHUNK = 64
RET_HEADS = 8
RET_DK = D_MODEL // RET_HEADS
RET_DV = 2 * RET_DK
RET_QK = RET_HEADS * RET_DK
RET_VD = RET_HEADS * RET_DV
ROPE_BASE = 10000.0
SSM_GROUP = 16
SSM_GROUPS = D_MODEL // SSM_GROUP
SSM_STATE = 64
FFN_DIM = 2 * D_MODEL
CONV_W = 3
EPS = 1e-6

M_PROMPT = SEQ
M_SAMPLE = DEC_BATCH * DEC_SEQ
M_TOTAL = M_PROMPT + M_SAMPLE

LANES = 128
SUBLANES = 8
VMEM_LIMIT_BYTES = 56 * 1024 * 1024

BM = M_SAMPLE
N_ROW_TILES = M_TOTAL // BM
N_PROMPT_TILES = M_PROMPT // BM

RET_T = 256
RET_SUB = 2

SSM_T = 16
SSM_PAIRS = SSM_GROUPS // 2
SSM_PB = 4
SSM_SEGS = 8
SSM_STEPS_P = M_PROMPT // SSM_T // SSM_SEGS
SSM_STEPS_S = DEC_SEQ // SSM_T
SSM_NC = (SSM_STEPS_P + SSM_STEPS_S) * SSM_SEGS


def _params(n_axes):
    return pltpu.CompilerParams(dimension_semantics=("arbitrary",) * n_axes,
                                vmem_limit_bytes=VMEM_LIMIT_BYTES)


def _sigmoid(x):
    return 1.0 / (1.0 + jnp.exp(-x))


def _rope_kernel(cos_ref, sin_ref):
    m = pl.program_id(0)
    half = RET_DK // 2
    freq = lax.broadcasted_iota(jnp.int32, (1, half), 1).astype(F32)
    inv = ROPE_BASE ** (-freq / half)
    r = m * BM + lax.broadcasted_iota(jnp.int32, (BM, half), 0)
    pos = jnp.where(r < M_PROMPT, r, PAST_LEN + ((r - M_PROMPT) & (DEC_SEQ - 1)))
    ang = pos.astype(F32) * inv
    cos_ref[...] = jnp.cos(ang)
    sin_ref[...] = jnp.sin(ang)


def _rope_tables():
    half = RET_DK // 2
    spec = pl.BlockSpec((BM, half), lambda m: (m, 0))
    return pl.pallas_call(
        _rope_kernel,
        out_shape=(jax.ShapeDtypeStruct((M_TOTAL, half), F32),) * 2,
        grid=(N_ROW_TILES,),
        out_specs=(spec, spec),
        compiler_params=_params(1),
        name="rope_tables",
    )()


def _norm_kernel(x_ref, g_ref, o_ref):
    x = x_ref[...]
    ms = jnp.mean(x * x, axis=-1, keepdims=True)
    o_ref[...] = ((x * lax.rsqrt(ms + EPS)) * g_ref[...]).astype(o_ref.dtype)


def _rmsnorm(x, g, out_dtype, tile0=0, n_tiles=N_ROW_TILES):
    return pl.pallas_call(
        _norm_kernel,
        out_shape=jax.ShapeDtypeStruct((n_tiles * BM, D_MODEL), out_dtype),
        grid=(n_tiles,),
        in_specs=[pl.BlockSpec((BM, D_MODEL), lambda m: (tile0 + m, 0)),
                  pl.BlockSpec((1, D_MODEL), lambda m: (0, 0))],
        out_specs=pl.BlockSpec((BM, D_MODEL), lambda m: (m, 0)),
        compiler_params=_params(1),
        name="rmsnorm",
    )(x, g.reshape(1, D_MODEL))


def _norm_split_kernel(xp_ref, xs_ref, g_ref, o_ref):
    x = jnp.where(pl.program_id(0) < N_PROMPT_TILES, xp_ref[...], xs_ref[...])
    ms = jnp.mean(x * x, axis=-1, keepdims=True)
    o_ref[...] = ((x * lax.rsqrt(ms + EPS)) * g_ref[...]).astype(o_ref.dtype)


def _rmsnorm_split(x_prompt, x_sample, g, out_dtype):
    return pl.pallas_call(
        _norm_split_kernel,
        out_shape=jax.ShapeDtypeStruct((M_TOTAL, D_MODEL), out_dtype),
        grid=(N_ROW_TILES,),
        in_specs=[pl.BlockSpec((BM, D_MODEL), lambda m: (jnp.minimum(m, N_PROMPT_TILES - 1), 0)),
                  pl.BlockSpec((BM, D_MODEL), lambda m: (0, 0)),
                  pl.BlockSpec((1, D_MODEL), lambda m: (0, 0))],
        out_specs=pl.BlockSpec((BM, D_MODEL), lambda m: (m, 0)),
        compiler_params=_params(1),
        name="rmsnorm_split",
    )(x_prompt, x_sample, g.reshape(1, D_MODEL))


CHUNKS_PER_TILE = BM // 16
N_CHUNKS = M_TOTAL // 16
N_LANE_TILES = D_MODEL // LANES


def _norm_chunked_kernel(x_ref, g_ref, o_ref, h_s):
    x = x_ref[...]
    ms = jnp.mean(x * x, axis=-1, keepdims=True)
    h = (x * lax.rsqrt(ms + EPS)) * g_ref[...]
    for j in range(N_LANE_TILES):
        h_s[j] = h[:, j * LANES:(j + 1) * LANES]
    for t in range(SSM_T):
        for j in range(N_LANE_TILES):
            rows = h_s[j, pl.ds(t, CHUNKS_PER_TILE, stride=SSM_T), :]
            o_ref[j, :, t * LANES:(t + 1) * LANES] = rows.astype(BF16)


def _rmsnorm_chunked(x, g):
    return pl.pallas_call(
        _norm_chunked_kernel,
        out_shape=jax.ShapeDtypeStruct((N_LANE_TILES, N_CHUNKS, SSM_T * LANES), BF16),
        grid=(N_ROW_TILES,),
        in_specs=[pl.BlockSpec((BM, D_MODEL), lambda m: (m, 0)),
                  pl.BlockSpec((1, D_MODEL), lambda m: (0, 0))],
        out_specs=pl.BlockSpec((N_LANE_TILES, CHUNKS_PER_TILE, SSM_T * LANES), lambda m: (0, m, 0)),
        scratch_shapes=[pltpu.VMEM((N_LANE_TILES, BM, LANES), F32)],
        compiler_params=_params(1),
        name="rmsnorm_chunked",
    )(x, g.reshape(1, D_MODEL))


def _weight_spec(k, bn, layer, col0=0):
    return pl.BlockSpec((None, k, bn), lambda n, m: (layer, 0, col0 + n), pipeline_mode=pl.Buffered(1))


def _mm_retin_kernel(a_ref, w_ref, cos_ref, sin_ref, o_ref, wb_ref, *, bn):
    n = pl.program_id(0)
    m = pl.program_id(1)
    n_q = RET_QK // bn
    n_rot = 2 * n_q

    @pl.when(m == 0)
    def _():
        wb_ref[...] = w_ref[...].astype(BF16)

    acc = jnp.dot(a_ref[...], wb_ref[...], preferred_element_type=F32)

    @pl.when(n < n_rot)
    def _():
        scale = jnp.where(n >= n_q, RET_DK ** -0.5, 1.0).astype(F32)
        c = cos_ref[...]
        s = sin_ref[...]
        half = RET_DK // 2
        for hh in range(bn // RET_DK):
            lo = hh * RET_DK
            x1 = acc[:, lo:lo + half]
            x2 = acc[:, lo + half:lo + RET_DK]
            o_ref[:, lo:lo + half] = ((x1 * c - x2 * s) * scale).astype(BF16)
            o_ref[:, lo + half:lo + RET_DK] = ((x1 * s + x2 * c) * scale).astype(BF16)

    @pl.when(n >= n_rot)
    def _():
        o_ref[...] = acc.astype(BF16)


def _mm_retin(h, w, layer, cos, sin):
    bn = 2048
    _, k, n_out = w.shape
    half = RET_DK // 2
    return pl.pallas_call(
        functools.partial(_mm_retin_kernel, bn=bn),
        out_shape=jax.ShapeDtypeStruct((M_TOTAL, n_out), BF16),
        grid=(n_out // bn, N_ROW_TILES),
        in_specs=[
            pl.BlockSpec((BM, k), lambda n, m: (m, 0)),
            _weight_spec(k, bn, layer),
            pl.BlockSpec((BM, half), lambda n, m: (m, 0)),
            pl.BlockSpec((BM, half), lambda n, m: (m, 0)),
        ],
        out_specs=pl.BlockSpec((BM, bn), lambda n, m: (m, n)),
        scratch_shapes=[pltpu.VMEM((k, bn), BF16)],
        compiler_params=_params(2),
        name="mm_ret_in",
    )(h, w, cos, sin)


def _row_split_specs(cols, col_index):
    return [pl.BlockSpec((BM, cols), lambda n, m: (jnp.minimum(m, N_PROMPT_TILES - 1), col_index(n))),
            pl.BlockSpec((BM, cols), lambda n, m: (0, col_index(n)))]


def _mm_res_kernel(*refs, split_a, split_x):
    refs = list(refs)
    ap_ref = refs.pop(0)
    as_ref = refs.pop(0) if split_a else None
    w_ref = refs.pop(0)
    xp_ref = refs.pop(0)
    xs_ref = refs.pop(0) if split_x else None
    o_ref, wb_ref = refs
    m = pl.program_id(1)
    on_prompt = m < N_PROMPT_TILES

    @pl.when(m == 0)
    def _():
        wb_ref[...] = w_ref[...].astype(BF16)

    def body(a_ref):
        x = xp_ref[...]
        if split_x:
            x = jnp.where(on_prompt, x, xs_ref[...])
        o_ref[...] = x + jnp.dot(a_ref[...], wb_ref[...], preferred_element_type=F32)

    if split_a:
        pl.when(on_prompt)(lambda: body(ap_ref))
        pl.when(jnp.logical_not(on_prompt))(lambda: body(as_ref))
    else:
        body(ap_ref)


def _mm_res(a, w, layer, x):
    bn = 1024
    _, k, n_out = w.shape
    split_a = isinstance(a, tuple)
    split_x = isinstance(x, tuple)
    a_specs = _row_split_specs(k, lambda n: 0) if split_a else [pl.BlockSpec((BM, k), lambda n, m: (m, 0))]
    x_specs = _row_split_specs(bn, lambda n: n) if split_x else [pl.BlockSpec((BM, bn), lambda n, m: (m, n))]
    a_args = list(a) if split_a else [a]
    x_args = list(x) if split_x else [x]
    return pl.pallas_call(
        functools.partial(_mm_res_kernel, split_a=split_a, split_x=split_x),
        out_shape=jax.ShapeDtypeStruct((M_TOTAL, n_out), F32),
        grid=(n_out // bn, N_ROW_TILES),
        in_specs=a_specs + [_weight_spec(k, bn, layer)] + x_specs,
        out_specs=pl.BlockSpec((BM, bn), lambda n, m: (m, n)),
        scratch_shapes=[pltpu.VMEM((k, bn), BF16)],
        compiler_params=_params(2),
        name="mm_residual",
    )(*a_args, w, *x_args)


def _mm_glu_kernel(a_ref, wa_ref, wg_ref, x_ref, o_ref, wab_ref, wgb_ref):
    @pl.when(pl.program_id(1) == 0)
    def _():
        wab_ref[...] = wa_ref[...].astype(BF16)
        wgb_ref[...] = wg_ref[...].astype(BF16)

    a = a_ref[...]
    ga = jnp.dot(a, wab_ref[...], preferred_element_type=F32)
    gb = jnp.dot(a, wgb_ref[...], preferred_element_type=F32)
    o_ref[...] = x_ref[...] + ga * _sigmoid(gb)


def _mm_glu(a, w, layer, x):
    bn = 1024
    _, k, n2 = w.shape
    n_out = n2 // 2
    nb = n_out // bn
    return pl.pallas_call(
        _mm_glu_kernel,
        out_shape=jax.ShapeDtypeStruct((M_TOTAL, n_out), F32),
        grid=(nb, N_ROW_TILES),
        in_specs=[
            pl.BlockSpec((BM, k), lambda n, m: (m, 0)),
            _weight_spec(k, bn, layer),
            _weight_spec(k, bn, layer, nb),
            pl.BlockSpec((BM, bn), lambda n, m: (m, n)),
        ],
        out_specs=pl.BlockSpec((BM, bn), lambda n, m: (m, n)),
        scratch_shapes=[pltpu.VMEM((k, bn), BF16), pltpu.VMEM((k, bn), BF16)],
        compiler_params=_params(2),
        name="mm_glu",
    )(a, w, w, x)


def _mm_ffnup_kernel(h_ref, wa_ref, wg_ref, cw_ref, cb_ref, cache_ref,
                     o_ref, tp_ref, ts_ref, wab_ref, wgb_ref, carry_ref):
    m = pl.program_id(1)

    @pl.when(m == 0)
    def _():
        wab_ref[...] = wa_ref[...].astype(BF16)
        wgb_ref[...] = wg_ref[...].astype(BF16)
        carry_ref[...] = jnp.zeros_like(carry_ref)

    h = h_ref[...]
    a = jnp.dot(h, wab_ref[...], preferred_element_type=F32)
    b = jnp.dot(h, wgb_ref[...], preferred_element_type=F32)
    r1 = pltpu.roll(a, 1, axis=0)
    r2 = pltpu.roll(a, 2, axis=0)
    cb = cb_ref[...]
    w0 = cw_ref[0:1, :]
    w1 = cw_ref[1:2, :]
    w2 = cw_ref[2:3, :]

    def gated(a_rows, prev1, prev2, b_rows):
        conv = cb + w0 * prev2 + w1 * prev1 + w2 * a_rows
        return ((conv * _sigmoid(conv)) * b_rows).astype(BF16)

    o_ref[...] = gated(a, r1, r2, b)

    row8 = lax.broadcasted_iota(jnp.int32, (SUBLANES, 1), 0)

    def redo_head(r0, before1, before2):
        rows = slice(r0, r0 + SUBLANES)
        prev1 = jnp.where(row8 == 0, before1, r1[rows])
        prev2 = jnp.where(row8 == 0, before2, jnp.where(row8 == 1, before1, r2[rows]))
        o_ref[rows, :] = gated(a[rows], prev1, prev2, b[rows])

    @pl.when(m < N_PROMPT_TILES)
    def _():
        c = carry_ref[...]
        redo_head(0, c[SUBLANES - 1:SUBLANES, :], c[SUBLANES - 2:SUBLANES - 1, :])
        carry_ref[...] = a[BM - SUBLANES:, :]

        @pl.when(m == N_PROMPT_TILES - 1)
        def _():
            tp_ref[...] = a[BM - SUBLANES:, :]

    @pl.when(m == N_PROMPT_TILES)
    def _():
        for s in range(DEC_BATCH):
            c = cache_ref[s]
            redo_head(s * DEC_SEQ, c[1:2, :], c[0:1, :])
            ts_ref[s * SUBLANES:(s + 1) * SUBLANES, :] = a[(s + 1) * DEC_SEQ - SUBLANES:(s + 1) * DEC_SEQ, :]


def _mm_ffnup(h, w, layer, conv_w, conv_b, cache):
    bn = 1024
    k = w.shape[1]
    nb = FFN_DIM // bn
    return pl.pallas_call(
        _mm_ffnup_kernel,
        out_shape=(
            jax.ShapeDtypeStruct((M_TOTAL, FFN_DIM), BF16),
            jax.ShapeDtypeStruct((SUBLANES, FFN_DIM), F32),
            jax.ShapeDtypeStruct((DEC_BATCH * SUBLANES, FFN_DIM), F32),
        ),
        grid=(nb, N_ROW_TILES),
        in_specs=[
            pl.BlockSpec((BM, k), lambda n, m: (m, 0)),
            _weight_spec(k, bn, layer),
            _weight_spec(k, bn, layer, nb),
            pl.BlockSpec((CONV_W, bn), lambda n, m: (0, n)),
            pl.BlockSpec((1, bn), lambda n, m: (0, n)),
            pl.BlockSpec((DEC_BATCH, CONV_W - 1, bn), lambda n, m: (0, 0, n)),
        ],
        out_specs=(
            pl.BlockSpec((BM, bn), lambda n, m: (m, n)),
            pl.BlockSpec((SUBLANES, bn), lambda n, m: (0, n)),
            pl.BlockSpec((DEC_BATCH * SUBLANES, bn), lambda n, m: (0, n)),
        ),
        scratch_shapes=[pltpu.VMEM((k, bn), BF16), pltpu.VMEM((k, bn), BF16),
                        pltpu.VMEM((SUBLANES, bn), F32)],
        compiler_params=_params(2),
        name="mm_ffn_up",
    )(h, w, w, conv_w, conv_b.reshape(1, FFN_DIM), cache)


def _ret_kernel(*refs, T, sub, has_init):
    if has_init:
        (lg_ref, q_ref, k_ref, v_ref, g_ref, gn_ref, s0_ref,
         o_ref, sl_ref, state_ref, mask_ref) = refs
    else:
        (lg_ref, q_ref, k_ref, v_ref, g_ref, gn_ref,
         o_ref, sl_ref, state_ref, mask_ref) = refs
    blk = pl.program_id(2)
    lg = lg_ref[...][:, 0:1]

    @pl.when(blk == 0)
    def _():
        if has_init:
            state_ref[...] = s0_ref[...]
        else:
            state_ref[...] = jnp.zeros_like(state_ref)
        i = lax.broadcasted_iota(jnp.int32, (T, T), 0)
        j = lax.broadcasted_iota(jnp.int32, (T, T), 1)
        dist = jnp.abs(i - j).astype(F32)
        shift = int(math.log2(CHUNK))
        visible = (j >> shift) <= (i >> shift)
        mask_ref[...] = jnp.where(visible, jnp.exp(dist * lg), 0.0)

    t = lax.broadcasted_iota(jnp.int32, (T, 1), 0).astype(F32)
    cross = jnp.exp((t + 1.0) * lg)
    k_dec = jnp.exp((T - 1.0 - t) * lg)
    decay = jnp.exp(float(T) * lg)
    mask = mask_ref[...]
    gn = gn_ref[...]
    state = state_ref[...]
    for sb in range(sub):
        rows = slice(sb * T, (sb + 1) * T)
        q = q_ref[rows, :]
        k = k_ref[rows, :]
        v = v_ref[rows, :]
        scores = lax.dot_general(q, k, (((1,), (1,)), ((), ())), preferred_element_type=F32)
        scores = scores * mask
        out = jnp.dot(scores.astype(BF16), v, preferred_element_type=F32)
        out = out + jnp.dot(q, state.astype(BF16), preferred_element_type=F32) * cross

        kd = (k.astype(F32) * k_dec).astype(BF16)
        state = decay * state + lax.dot_general(
            kd, v, (((0,), (0,)), ((), ())), preferred_element_type=F32)

        mu = jnp.mean(out, axis=-1, keepdims=True)
        oc = out - mu
        var = jnp.mean(oc * oc, axis=-1, keepdims=True)
        normed = (oc * lax.rsqrt(var + EPS)) * gn
        g = g_ref[rows, :].astype(F32)
        o_ref[rows, :] = (normed * (g * _sigmoid(g))).astype(BF16)
    state_ref[...] = state

    @pl.when(blk == pl.num_programs(2) - 1)
    def _():
        sl_ref[...] = state


def _retention(qkvg, gn, s0, *, row0, n_seq, n_blk, T, sub=1):
    log_g = np.log1p(-np.exp2(-5.0 - np.arange(RET_HEADS, dtype=np.float32))).astype(np.float32)
    lg = jnp.asarray(np.broadcast_to(log_g[:, None, None], (RET_HEADS, 1, LANES)).copy())
    R = sub * T
    rb0 = row0 // R
    k_off = RET_QK // RET_DK
    v_off = 2 * RET_QK // RET_DV
    g_off = v_off + RET_VD // RET_DV
    has_init = s0 is not None

    def rows(h, s, b):
        return rb0 + s * n_blk + b

    in_specs = [
        pl.BlockSpec((None, 1, LANES), lambda h, s, b: (h, 0, 0)),
        pl.BlockSpec((R, RET_DK), lambda h, s, b: (rows(h, s, b), h)),
        pl.BlockSpec((R, RET_DK), lambda h, s, b: (rows(h, s, b), k_off + h)),
        pl.BlockSpec((R, RET_DV), lambda h, s, b: (rows(h, s, b), v_off + h)),
        pl.BlockSpec((R, RET_DV), lambda h, s, b: (rows(h, s, b), g_off + h)),
        pl.BlockSpec((1, RET_DV), lambda h, s, b: (0, h)),
    ]
    args = [lg, qkvg, qkvg, qkvg, qkvg, gn.reshape(1, RET_VD)]
    if has_init:
        s0_stack, layer = s0
        in_specs.append(pl.BlockSpec((None, None, None, RET_DK, RET_DV),
                                     lambda h, s, b: (layer, s, h, 0, 0)))
        args.append(s0_stack)
    return pl.pallas_call(
        functools.partial(_ret_kernel, T=T, sub=sub, has_init=has_init),
        out_shape=(
            jax.ShapeDtypeStruct((n_seq * n_blk * R, RET_VD), BF16),
            jax.ShapeDtypeStruct((n_seq, RET_HEADS, RET_DK, RET_DV), F32),
        ),
        grid=(RET_HEADS, n_seq, n_blk),
        in_specs=in_specs,
        out_specs=(
            pl.BlockSpec((R, RET_DV), lambda h, s, b: (s * n_blk + b, h)),
            pl.BlockSpec((None, None, RET_DK, RET_DV), lambda h, s, b: (s, h, 0, 0)),
        ),
        scratch_shapes=[pltpu.VMEM((RET_DK, RET_DV), F32), pltpu.VMEM((T, T), F32)],
        compiler_params=_params(3),
        name="retention_init" if has_init else "retention",
    )(*args)


def _ssm_prep_kernel(ar_ref, ai_ref, ldt_ref, br_ref, bi_ref, cr_ref, ci_ref,
                     bc_ref, cc_ref, lag_ref, a16r_ref, a16i_ref, vr_s, vi_s):
    ar = ar_ref[...]
    ai = ai_ref[...]
    dt = jnp.exp(ldt_ref[...])
    mag = jnp.exp(ar * dt)
    ang = ai * dt
    abr = mag * jnp.cos(ang)
    abi = mag * jnp.sin(ang)
    den = ar * ar + ai * ai
    nr = abr - 1.0
    ni = abi
    cfr = (nr * ar + ni * ai) / den
    cfi = (ni * ar - nr * ai) / den

    def b4(z):
        return z[:, None]

    br = br_ref[...]
    bi = bi_ref[...]
    bbr = b4(cfr) * br - b4(cfi) * bi
    bbi = b4(cfr) * bi + b4(cfi) * br
    cre = cr_ref[...]
    cim = ci_ref[...]

    powers = []
    pr = jnp.ones_like(ar)
    pi = jnp.zeros_like(ar)
    for _ in range(SSM_T + 1):
        powers.append((pr, pi))
        pr, pi = pr * abr - pi * abi, pr * abi + pi * abr
    a16r_ref[...] = powers[SSM_T][0]
    a16i_ref[...] = powers[SSM_T][1]

    for m in range(SSM_T + 1):
        pr, pi = powers[m]
        vr = b4(pr) * cre - b4(pi) * cim
        vi = b4(pr) * cim + b4(pi) * cre
        if m < SSM_T:
            vr_s[:, :, m] = vr
            vi_s[:, :, m] = vi
        if m >= 1:
            cc_ref[:, :, m - 1, :, 0:LANES] = vr.astype(BF16)
            cc_ref[:, :, m - 1, :, LANES:2 * LANES] = (-vi).astype(BF16)

    for t in range(SSM_T):
        pr, pi = powers[SSM_T - 1 - t]
        bc_ref[:, :, t, :, 0:LANES] = (b4(pr) * bbr - b4(pi) * bbi).astype(BF16)
        bc_ref[:, :, t, :, LANES:2 * LANES] = (b4(pr) * bbi + b4(pi) * bbr).astype(BF16)

    width = SSM_T * SSM_GROUP
    lane = lax.broadcasted_iota(jnp.int32, (SSM_GROUP, LANES), 1)
    contract_last = (((1,), (1,)), ((), ()))
    for q in range(SSM_PB):
        for half in range(2):
            gl = 2 * q + half
            vr_all = vr_s[q, half].reshape(width, LANES)
            vi_all = vi_s[q, half].reshape(width, LANES)
            krow = (lax.dot_general(bbr[q, half], vr_all, contract_last,
                                    precision=lax.Precision.HIGHEST, preferred_element_type=F32)
                    - lax.dot_general(bbi[q, half], vi_all, contract_last,
                                      precision=lax.Precision.HIGHEST, preferred_element_type=F32))
            in_block = (lane >= gl * SSM_GROUP) & (lane < (gl + 1) * SSM_GROUP)
            for m in range(SSM_T):
                shift = ((gl - m) * SSM_GROUP) % width
                moved = krow if shift == 0 else pltpu.roll(krow, shift, axis=1)
                lag_ref[m, gl * SSM_GROUP:(gl + 1) * SSM_GROUP, :] = jnp.where(
                    in_block, moved[:, 0:LANES], 0.0).astype(BF16)


def _pair_pack(x):
    x = x.reshape(SSM_PAIRS, 2, SSM_GROUP, SSM_STATE)
    lo = jnp.pad(x[:, 0], ((0, 0), (0, 0), (0, SSM_STATE)))
    hi = jnp.pad(x[:, 1], ((0, 0), (0, 0), (SSM_STATE, 0)))
    return jnp.stack([lo, hi], axis=1)


def _ssm_prep(a_re, a_im, log_dt, b_re, b_im, c_re, c_im):
    assert 2 * SSM_PB * SSM_GROUP == LANES
    ar = a_re.reshape(SSM_PAIRS, 1, LANES)
    ai = a_im.reshape(SSM_PAIRS, 1, LANES)
    ldt = jnp.repeat(log_dt, SSM_STATE).reshape(SSM_PAIRS, 1, LANES)
    b2r = _pair_pack(jnp.transpose(b_re, (0, 2, 1)))
    b2i = _pair_pack(jnp.transpose(b_im, (0, 2, 1)))
    c2r = _pair_pack(c_re)
    c2i = _pair_pack(c_im)
    vec = pl.BlockSpec((SSM_PB, 1, LANES), lambda p: (p, 0, 0))
    mat = pl.BlockSpec((SSM_PB, 2, SSM_GROUP, LANES), lambda p: (p, 0, 0, 0))
    proj = pl.BlockSpec((SSM_PB, 2, SSM_T, SSM_GROUP, 2 * LANES), lambda p: (p, 0, 0, 0, 0))
    bc, cc, lag, a16r, a16i = pl.pallas_call(
        _ssm_prep_kernel,
        out_shape=(
            jax.ShapeDtypeStruct((SSM_PAIRS, 2, SSM_T, SSM_GROUP, 2 * LANES), BF16),
            jax.ShapeDtypeStruct((SSM_PAIRS, 2, SSM_T, SSM_GROUP, 2 * LANES), BF16),
            jax.ShapeDtypeStruct((N_LANE_TILES, SSM_T, LANES, LANES), BF16),
            jax.ShapeDtypeStruct((SSM_PAIRS, 1, LANES), F32),
            jax.ShapeDtypeStruct((SSM_PAIRS, 1, LANES), F32),
        ),
        grid=(N_LANE_TILES,),
        in_specs=[vec, vec, vec, mat, mat, mat, mat],
        out_specs=(proj, proj,
                   pl.BlockSpec((None, SSM_T, LANES, LANES), lambda p: (p, 0, 0, 0)),
                   vec, vec),
        scratch_shapes=[pltpu.VMEM((SSM_PB, 2, SSM_T, SSM_GROUP, LANES), F32)] * 2,
        compiler_params=_params(1),
        name="ssm_prep",
    )(ar, ai, ldt, b2r, b2i, c2r, c2i)
    row = SSM_GROUPS * SSM_STATE
    a16r = jnp.broadcast_to(a16r.reshape(1, row), (SSM_SEGS, row))
    a16i = jnp.broadcast_to(a16i.reshape(1, row), (SSM_SEGS, row))
    return bc, cc, lag, a16r, a16i


SSM_CHUNK_COLS = SSM_T * LANES
SSM_STATE_COLS = SSM_PB * LANES
SSM_CHUNKS_P = M_PROMPT // SSM_T
SSM_COL_BLOCKS = 4


def _ssm_main_kernel(x_ref, lag_ref, bc_ref, cc_ref, ar_ref, ai_ref, h0r_ref, h0i_ref,
                     y_ref, fpr_ref, fpi_ref, fsr_ref, fsi_ref,
                     w_s, bcw_s, ccw_s, d_s, s_s):
    w = SSM_STATE_COLS

    @pl.when(pl.program_id(0) == 0)
    def _():
        w_s[...] = jnp.zeros_like(w_s)
        bcw_s[...] = jnp.zeros_like(bcw_s)
        ccw_s[...] = jnp.zeros_like(ccw_s)

    for t in range(SSM_T):
        for t2 in range(t, SSM_T):
            w_s[t * LANES:(t + 1) * LANES, t2 * LANES:(t2 + 1) * LANES] = lag_ref[t2 - t]
    for q in range(SSM_PB):
        for half in range(2):
            for t in range(SSM_T):
                r0 = t * LANES + (2 * q + half) * SSM_GROUP
                rows = slice(r0, r0 + SSM_GROUP)
                bcw_s[rows, q * LANES:(q + 1) * LANES] = bc_ref[q, half, t, :, 0:LANES]
                bcw_s[rows, w + q * LANES:w + (q + 1) * LANES] = bc_ref[q, half, t, :, LANES:2 * LANES]
                ccw_s[rows, q * LANES:(q + 1) * LANES] = cc_ref[q, half, t, :, 0:LANES]
                ccw_s[rows, w + q * LANES:w + (q + 1) * LANES] = cc_ref[q, half, t, :, LANES:2 * LANES]

    x = x_ref[...]
    for k2 in range(SSM_PB):
        d = jnp.dot(x, bcw_s[:, 2 * k2 * LANES:2 * (k2 + 1) * LANES], preferred_element_type=F32)
        d_s[2 * k2] = d[:, 0:LANES]
        d_s[2 * k2 + 1] = d[:, LANES:2 * LANES]

    tiles = range(SSM_PB)
    ar = [ar_ref[:, k * LANES:(k + 1) * LANES] for k in tiles]
    ai = [ai_ref[:, k * LANES:(k + 1) * LANES] for k in tiles]

    def advance(sr, si, rows, record):
        out_r, out_i = [], []
        for k in tiles:
            if record:
                s_s[k, rows, :] = sr[k]
                s_s[SSM_PB + k, rows, :] = si[k]
            dr = d_s[k, rows, :]
            di = d_s[SSM_PB + k, rows, :]
            out_r.append(ar[k] * sr[k] - ai[k] * si[k] + dr)
            out_i.append(ar[k] * si[k] + ai[k] * sr[k] + di)
        return out_r, out_i

    def prompt_rows(i):
        return pl.ds(i, SSM_SEGS, stride=SSM_STEPS_P)

    zeros = [jnp.zeros((SSM_SEGS, LANES), F32) for _ in tiles]
    er, ei = zeros, zeros
    for i in range(SSM_STEPS_P):
        er, ei = advance(er, ei, prompt_rows(i), record=False)

    sr, si = [], []
    for k in tiles:
        pr, pi = ar[k][0:1], ai[k][0:1]
        for _ in range(int(math.log2(SSM_STEPS_P))):
            pr, pi = pr * pr - pi * pi, 2.0 * pr * pi
        cr = jnp.zeros((1, LANES), F32)
        ci = cr
        starts_r, starts_i = [], []
        for s in range(SSM_SEGS):
            starts_r.append(cr)
            starts_i.append(ci)
            cr, ci = (pr * cr - pi * ci + er[k][s:s + 1], pr * ci + pi * cr + ei[k][s:s + 1])
        cols = slice(k * LANES, (k + 1) * LANES)
        fpr_ref[:, cols] = jnp.broadcast_to(cr, (SSM_SEGS, LANES))
        fpi_ref[:, cols] = jnp.broadcast_to(ci, (SSM_SEGS, LANES))
        sr.append(jnp.concatenate(starts_r, axis=0))
        si.append(jnp.concatenate(starts_i, axis=0))

    for i in range(SSM_STEPS_P):
        sr, si = advance(sr, si, prompt_rows(i), record=True)
    sr = [h0r_ref[:, k * LANES:(k + 1) * LANES] for k in tiles]
    si = [h0i_ref[:, k * LANES:(k + 1) * LANES] for k in tiles]
    for i in range(SSM_STEPS_S):
        sr, si = advance(sr, si, pl.ds(SSM_CHUNKS_P + i, DEC_BATCH, stride=SSM_STEPS_S), record=True)
    for k in tiles:
        fsr_ref[:, k * LANES:(k + 1) * LANES] = sr[k]
        fsi_ref[:, k * LANES:(k + 1) * LANES] = si[k]

    states = jnp.concatenate([s_s[k].astype(BF16) for k in range(2 * SSM_PB)], axis=1)
    contract_last = (((1,), (1,)), ((), ()))
    tokens_per_block = SSM_T // SSM_COL_BLOCKS
    for cb in range(SSM_COL_BLOCKS):
        k_hi = (cb + 1) * tokens_per_block * LANES
        cols = slice(cb * tokens_per_block * LANES, k_hi)
        carried = lax.dot_general(states, ccw_s[cols, :], contract_last, preferred_element_type=F32)
        y_ref[:, cols] = jnp.dot(x_ref[:, 0:k_hi], w_s[0:k_hi, cols],
                                 preferred_element_type=F32) + carried


def _ssm_main(x_c, lag, bc, cc, a16r, a16i, h0r, h0i):
    w = SSM_STATE_COLS
    row = SSM_GROUPS * SSM_STATE
    st = pl.BlockSpec((SSM_SEGS, w), lambda j: (0, j))
    st_shape = jax.ShapeDtypeStruct((SSM_SEGS, row), F32)
    proj = pl.BlockSpec((SSM_PB, 2, SSM_T, SSM_GROUP, 2 * LANES), lambda j: (j, 0, 0, 0, 0))
    chunk_rows = pl.BlockSpec((None, N_CHUNKS, SSM_CHUNK_COLS), lambda j: (j, 0, 0))
    return pl.pallas_call(
        _ssm_main_kernel,
        out_shape=(jax.ShapeDtypeStruct((N_LANE_TILES, N_CHUNKS, SSM_CHUNK_COLS), F32),
                   st_shape, st_shape, st_shape, st_shape),
        grid=(N_LANE_TILES,),
        in_specs=[
            chunk_rows,
            pl.BlockSpec((None, SSM_T, LANES, LANES), lambda j: (j, 0, 0, 0)),
            proj, proj, st, st, st, st,
        ],
        out_specs=(chunk_rows, st, st, st, st),
        scratch_shapes=[
            pltpu.VMEM((SSM_CHUNK_COLS, SSM_CHUNK_COLS), BF16),
            pltpu.VMEM((SSM_CHUNK_COLS, 2 * w), BF16),
            pltpu.VMEM((SSM_CHUNK_COLS, 2 * w), BF16),
            pltpu.VMEM((2 * SSM_PB, N_CHUNKS, LANES), F32),
            pltpu.VMEM((2 * SSM_PB, N_CHUNKS, LANES), F32),
        ],
        compiler_params=_params(1),
        name="ssm_main",
    )(x_c, lag, bc, cc, a16r, a16i, h0r, h0i)


def _ssm_post_kernel(y_ref, x_ref, g_ref, d_ref, o_ref, y_s):
    for t in range(SSM_T):
        for j in range(N_LANE_TILES):
            y_s[j, pl.ds(t, CHUNKS_PER_TILE, stride=SSM_T), :] = y_ref[j, :, t * LANES:(t + 1) * LANES]
    x = x_ref[...]
    ms = jnp.mean(x * x, axis=-1, keepdims=True)
    h = (x * lax.rsqrt(ms + EPS)) * g_ref[...]
    c = math.sqrt(2.0 / math.pi)
    for j in range(N_LANE_TILES):
        cols = slice(j * LANES, (j + 1) * LANES)
        y = y_s[j] + d_ref[:, cols] * h[:, cols]
        gelu = 0.5 * y * (1.0 + jnp.tanh(c * (y + 0.044715 * (y * y * y))))
        o_ref[:, cols] = gelu.astype(BF16)


def _ssm_post(y_c, x, g, d):
    spec = pl.BlockSpec((BM, D_MODEL), lambda m: (m, 0))
    vec = pl.BlockSpec((1, D_MODEL), lambda m: (0, 0))
    return pl.pallas_call(
        _ssm_post_kernel,
        out_shape=jax.ShapeDtypeStruct((M_TOTAL, D_MODEL), BF16),
        grid=(N_ROW_TILES,),
        in_specs=[pl.BlockSpec((N_LANE_TILES, CHUNKS_PER_TILE, SSM_CHUNK_COLS), lambda m: (0, m, 0)),
                  spec, vec, vec],
        out_specs=spec,
        scratch_shapes=[pltpu.VMEM((N_LANE_TILES, BM, LANES), F32)],
        compiler_params=_params(1),
        name="ssm_post",
    )(y_c, x, g.reshape(1, D_MODEL), d.reshape(1, D_MODEL))


def kernel(x_prompt, x_sample, state_ret, state_ssm_re, state_ssm_im, cache_conv, norm_mix, norm_ffn, norm_final, ret_w_in, ret_gn, ret_w_out, ssm_a_re, ssm_a_im, ssm_log_dt, ssm_b_re, ssm_b_im, ssm_c_re, ssm_c_im, ssm_d, ssm_w_glu, ffn_w_up, ffn_conv_w, ffn_conv_b, ffn_w_down):
    x = (x_prompt.reshape(M_PROMPT, D_MODEL), x_sample.reshape(M_SAMPLE, D_MODEL))
    cos, sin = _rope_tables()

    ret_p, ret_s = [], []
    re_p, im_p, re_s, im_s = [], [], [], []
    conv_p, conv_s = [], []
    state_row = SSM_GROUPS * SSM_STATE
    for i in range(DEPTH):
        j = i // 2
        if i % 2 == 0:
            if isinstance(x, tuple):
                h = _rmsnorm_split(x[0], x[1], norm_mix[i], BF16)
            else:
                h = _rmsnorm(x, norm_mix[i], BF16)
            qkvg = _mm_retin(h, ret_w_in, j, cos, sin)
            o_p, s_p = _retention(qkvg, ret_gn[j], None, row0=0, n_seq=1,
                                  n_blk=M_PROMPT // (RET_T * RET_SUB), T=RET_T, sub=RET_SUB)
            o_s, s_s = _retention(qkvg, ret_gn[j], (state_ret, j), row0=M_PROMPT, n_seq=DEC_BATCH,
                                  n_blk=1, T=DEC_SEQ)
            ret_p.append(s_p)
            ret_s.append(s_s)
            x = _mm_res((o_p, o_s), ret_w_out, j, x)
        else:
            h_c = _rmsnorm_chunked(x, norm_mix[i])
            bc, cc, lag, a16r, a16i = _ssm_prep(ssm_a_re[j], ssm_a_im[j], ssm_log_dt[j],
                                               ssm_b_re[j], ssm_b_im[j], ssm_c_re[j], ssm_c_im[j])
            y_c, fpr, fpi, fsr, fsi = _ssm_main(
                h_c, lag, bc, cc, a16r, a16i,
                state_ssm_re[j].reshape(DEC_BATCH, state_row),
                state_ssm_im[j].reshape(DEC_BATCH, state_row))
            re_p.append(fpr[0:1].reshape(1, SSM_GROUPS, SSM_STATE))
            im_p.append(fpi[0:1].reshape(1, SSM_GROUPS, SSM_STATE))
            re_s.append(fsr.reshape(DEC_BATCH, SSM_GROUPS, SSM_STATE))
            im_s.append(fsi.reshape(DEC_BATCH, SSM_GROUPS, SSM_STATE))
            gl = _ssm_post(y_c, x, norm_mix[i], ssm_d[j])
            x = _mm_glu(gl, ssm_w_glu, j, x)

        h = _rmsnorm(x, norm_ffn[i], BF16)
        u, tail_p, tail_s = _mm_ffnup(h, ffn_w_up, i, ffn_conv_w[i], ffn_conv_b[i], cache_conv[i])
        conv_p.append(tail_p[SUBLANES - (CONV_W - 1):].reshape(1, CONV_W - 1, FFN_DIM))
        conv_s.append(tail_s.reshape(DEC_BATCH, SUBLANES, FFN_DIM)[:, SUBLANES - (CONV_W - 1):])
        x = _mm_res(u, ffn_w_down, i, x)

    y_prompt = _rmsnorm(x, norm_final, F32, 0, N_PROMPT_TILES).reshape(1, SEQ, D_MODEL)
    y_sample = _rmsnorm(x, norm_final, F32, N_PROMPT_TILES, 1).reshape(DEC_BATCH, DEC_SEQ, D_MODEL)
    return (y_prompt, y_sample, jnp.stack(ret_p), jnp.stack(ret_s),
            jnp.stack(re_p), jnp.stack(im_p), jnp.stack(re_s), jnp.stack(im_s),
            jnp.stack(conv_p), jnp.stack(conv_s))
```

```python
import functools
import math

import numpy as np
import jax
import jax.numpy as jnp
from jax import lax
from jax.experimental import pallas as pl
from jax.experimental.pallas import tpu as pltpu

F32 = jnp.float32
BF16 = jnp.bfloat16

D_MODEL = 2048
SEQ = 8192
DEPTH = 4
DEC_BATCH = 8
DEC_SEQ = 64
PAST_LEN = 1024
CHUNK = 64
RET_HEADS = 8
RET_DK = D_MODEL // RET_HEADS
RET_DV = 2 * RET_DK
RET_QK = RET_HEADS * RET_DK
RET_VD = RET_HEADS * RET_DV
ROPE_BASE = 10000.0
SSM_GROUP = 16
SSM_GROUPS = D_MODEL // SSM_GROUP
SSM_STATE = 64
FFN_DIM = 2 * D_MODEL
CONV_W = 3
EPS = 1e-6

M_PROMPT = SEQ
M_SAMPLE = DEC_BATCH * DEC_SEQ
M_TOTAL = M_PROMPT + M_SAMPLE

LANES = 128
SUBLANES = 8
VMEM_LIMIT_BYTES = 56 * 1024 * 1024

BM = M_SAMPLE
N_ROW_TILES = M_TOTAL // BM
N_PROMPT_TILES = M_PROMPT // BM

RET_T = 256
RET_SUB = 2

SSM_T = 16
SSM_PAIRS = SSM_GROUPS // 2
SSM_PB = 4
SSM_SEGS = 8
SSM_STEPS_P = M_PROMPT // SSM_T // SSM_SEGS
SSM_STEPS_S = DEC_SEQ // SSM_T
SSM_NC = (SSM_STEPS_P + SSM_STEPS_S) * SSM_SEGS


def _params(n_axes):
    return pltpu.CompilerParams(dimension_semantics=("arbitrary",) * n_axes,
                                vmem_limit_bytes=VMEM_LIMIT_BYTES)


def _sigmoid(x):
    return 1.0 / (1.0 + jnp.exp(-x))


def _rope_kernel(cos_ref, sin_ref):
    m = pl.program_id(0)
    half = RET_DK // 2
    freq = lax.broadcasted_iota(jnp.int32, (1, half), 1).astype(F32)
    inv = ROPE_BASE ** (-freq / half)
    r = m * BM + lax.broadcasted_iota(jnp.int32, (BM, half), 0)
    pos = jnp.where(r < M_PROMPT, r, PAST_LEN + ((r - M_PROMPT) & (DEC_SEQ - 1)))
    ang = pos.astype(F32) * inv
    cos_ref[...] = jnp.cos(ang)
    sin_ref[...] = jnp.sin(ang)


def _rope_tables():
    half = RET_DK // 2
    spec = pl.BlockSpec((BM, half), lambda m: (m, 0))
    return pl.pallas_call(
        _rope_kernel,
        out_shape=(jax.ShapeDtypeStruct((M_TOTAL, half), F32),) * 2,
        grid=(N_ROW_TILES,),
        out_specs=(spec, spec),
        compiler_params=_params(1),
        name="rope_tables",
    )()


def _norm_kernel(x_ref, g_ref, o_ref):
    x = x_ref[...]
    ms = jnp.mean(x * x, axis=-1, keepdims=True)
    o_ref[...] = ((x * lax.rsqrt(ms + EPS)) * g_ref[...]).astype(o_ref.dtype)


def _rmsnorm(x, g, out_dtype, tile0=0, n_tiles=N_ROW_TILES):
    return pl.pallas_call(
        _norm_kernel,
        out_shape=jax.ShapeDtypeStruct((n_tiles * BM, D_MODEL), out_dtype),
        grid=(n_tiles,),
        in_specs=[pl.BlockSpec((BM, D_MODEL), lambda m: (tile0 + m, 0)),
                  pl.BlockSpec((1, D_MODEL), lambda m: (0, 0))],
        out_specs=pl.BlockSpec((BM, D_MODEL), lambda m: (m, 0)),
        compiler_params=_params(1),
        name="rmsnorm",
    )(x, g.reshape(1, D_MODEL))


def _norm_split_kernel(xp_ref, xs_ref, g_ref, o_ref):
    x = jnp.where(pl.program_id(0) < N_PROMPT_TILES, xp_ref[...], xs_ref[...])
    ms = jnp.mean(x * x, axis=-1, keepdims=True)
    o_ref[...] = ((x * lax.rsqrt(ms + EPS)) * g_ref[...]).astype(o_ref.dtype)


def _rmsnorm_split(x_prompt, x_sample, g, out_dtype):
    return pl.pallas_call(
        _norm_split_kernel,
        out_shape=jax.ShapeDtypeStruct((M_TOTAL, D_MODEL), out_dtype),
        grid=(N_ROW_TILES,),
        in_specs=[pl.BlockSpec((BM, D_MODEL), lambda m: (jnp.minimum(m, N_PROMPT_TILES - 1), 0)),
                  pl.BlockSpec((BM, D_MODEL), lambda m: (0, 0)),
                  pl.BlockSpec((1, D_MODEL), lambda m: (0, 0))],
        out_specs=pl.BlockSpec((BM, D_MODEL), lambda m: (m, 0)),
        compiler_params=_params(1),
        name="rmsnorm_split",
    )(x_prompt, x_sample, g.reshape(1, D_MODEL))


CHUNKS_PER_TILE = BM // 16
N_CHUNKS = M_TOTAL // 16
N_LANE_TILES = D_MODEL // LANES


def _norm_chunked_kernel(x_ref, g_ref, o_ref, h_s):
    x = x_ref[...]
    ms = jnp.mean(x * x, axis=-1, keepdims=True)
    h = (x * lax.rsqrt(ms + EPS)) * g_ref[...]
    for j in range(N_LANE_TILES):
        h_s[j] = h[:, j * LANES:(j + 1) * LANES]
    for t in range(SSM_T):
        for j in range(N_LANE_TILES):
            rows = h_s[j, pl.ds(t, CHUNKS_PER_TILE, stride=SSM_T), :]
            o_ref[j, :, t * LANES:(t + 1) * LANES] = rows.astype(BF16)


def _rmsnorm_chunked(x, g):
    return pl.pallas_call(
        _norm_chunked_kernel,
        out_shape=jax.ShapeDtypeStruct((N_LANE_TILES, N_CHUNKS, SSM_T * LANES), BF16),
        grid=(N_ROW_TILES,),
        in_specs=[pl.BlockSpec((BM, D_MODEL), lambda m: (m, 0)),
                  pl.BlockSpec((1, D_MODEL), lambda m: (0, 0))],
        out_specs=pl.BlockSpec((N_LANE_TILES, CHUNKS_PER_TILE, SSM_T * LANES), lambda m: (0, m, 0)),
        scratch_shapes=[pltpu.VMEM((N_LANE_TILES, BM, LANES), F32)],
        compiler_params=_params(1),
        name="rmsnorm_chunked",
    )(x, g.reshape(1, D_MODEL))


def _weight_spec(k, bn, layer, col0=0):
    return pl.BlockSpec((None, k, bn), lambda n, m: (layer, 0, col0 + n), pipeline_mode=pl.Buffered(1))


RET_IN_BN = 2048


def _mm_qk_kernel(a_ref, w_ref, cos_ref, sin_ref, o_ref, wb_ref):
    n = pl.program_id(0)
    n_q = RET_QK // RET_IN_BN

    @pl.when(pl.program_id(1) == 0)
    def _():
        wb_ref[...] = w_ref[...].astype(BF16)

    acc = jnp.dot(a_ref[...], wb_ref[...], preferred_element_type=F32)
    scale = jnp.where(n >= n_q, RET_DK ** -0.5, 1.0).astype(F32)
    c = cos_ref[...]
    s = sin_ref[...]
    half = RET_DK // 2
    for hh in range(RET_IN_BN // RET_DK):
        lo = hh * RET_DK
        x1 = acc[:, lo:lo + half]
        x2 = acc[:, lo + half:lo + RET_DK]
        o_ref[:, lo:lo + half] = ((x1 * c - x2 * s) * scale).astype(BF16)
        o_ref[:, lo + half:lo + RET_DK] = ((x1 * s + x2 * c) * scale).astype(BF16)


def _mm_cast_kernel(a_ref, w_ref, o_ref, wb_ref):
    @pl.when(pl.program_id(1) == 0)
    def _():
        wb_ref[...] = w_ref[...].astype(BF16)

    o_ref[...] = jnp.dot(a_ref[...], wb_ref[...], preferred_element_type=F32).astype(BF16)


def _mm_retin(h, w, layer, cos, sin):
    bn = RET_IN_BN
    _, k, n_out = w.shape
    half = RET_DK // 2
    n_qk = 2 * RET_QK // bn
    a_spec = pl.BlockSpec((BM, k), lambda n, m: (m, 0))
    out_spec = pl.BlockSpec((BM, bn), lambda n, m: (m, n))
    rope_spec = pl.BlockSpec((BM, half), lambda n, m: (m, 0))
    qk = pl.pallas_call(
        _mm_qk_kernel,
        out_shape=jax.ShapeDtypeStruct((M_TOTAL, 2 * RET_QK), BF16),
        grid=(n_qk, N_ROW_TILES),
        in_specs=[a_spec, _weight_spec(k, bn, layer), rope_spec, rope_spec],
        out_specs=out_spec,
        scratch_shapes=[pltpu.VMEM((k, bn), BF16)],
        compiler_params=_params(2),
        name="mm_ret_qk",
    )(h, w, cos, sin)
    vg = pl.pallas_call(
        _mm_cast_kernel,
        out_shape=jax.ShapeDtypeStruct((M_TOTAL, 2 * RET_VD), BF16),
        grid=(n_out // bn - n_qk, N_ROW_TILES),
        in_specs=[a_spec, _weight_spec(k, bn, layer, n_qk)],
        out_specs=out_spec,
        scratch_shapes=[pltpu.VMEM((k, bn), BF16)],
        compiler_params=_params(2),
        name="mm_ret_vg",
    )(h, w)
    return qk, vg


def _row_split_specs(cols, col_index):
    return [pl.BlockSpec((BM, cols), lambda n, m: (jnp.minimum(m, N_PROMPT_TILES - 1), col_index(n))),
            pl.BlockSpec((BM, cols), lambda n, m: (0, col_index(n)))]


def _mm_res_kernel(*refs, split_a, split_x):
    refs = list(refs)
    ap_ref = refs.pop(0)
    as_ref = refs.pop(0) if split_a else None
    w_ref = refs.pop(0)
    xp_ref = refs.pop(0)
    xs_ref = refs.pop(0) if split_x else None
    o_ref, wb_ref = refs
    m = pl.program_id(1)
    on_prompt = m < N_PROMPT_TILES

    @pl.when(m == 0)
    def _():
        wb_ref[...] = w_ref[...].astype(BF16)

    def body(a_ref):
        x = xp_ref[...]
        if split_x:
            x = jnp.where(on_prompt, x, xs_ref[...])
        o_ref[...] = x + jnp.dot(a_ref[...], wb_ref[...], preferred_element_type=F32)

    if split_a:
        pl.when(on_prompt)(lambda: body(ap_ref))
        pl.when(jnp.logical_not(on_prompt))(lambda: body(as_ref))
    else:
        body(ap_ref)


def _mm_res(a, w, layer, x):
    bn = 1024
    _, k, n_out = w.shape
    split_a = isinstance(a, tuple)
    split_x = isinstance(x, tuple)
    a_specs = _row_split_specs(k, lambda n: 0) if split_a else [pl.BlockSpec((BM, k), lambda n, m: (m, 0))]
    x_specs = _row_split_specs(bn, lambda n: n) if split_x else [pl.BlockSpec((BM, bn), lambda n, m: (m, n))]
    a_args = list(a) if split_a else [a]
    x_args = list(x) if split_x else [x]
    return pl.pallas_call(
        functools.partial(_mm_res_kernel, split_a=split_a, split_x=split_x),
        out_shape=jax.ShapeDtypeStruct((M_TOTAL, n_out), F32),
        grid=(n_out // bn, N_ROW_TILES),
        in_specs=a_specs + [_weight_spec(k, bn, layer)] + x_specs,
        out_specs=pl.BlockSpec((BM, bn), lambda n, m: (m, n)),
        scratch_shapes=[pltpu.VMEM((k, bn), BF16)],
        compiler_params=_params(2),
        name="mm_residual",
    )(*a_args, w, *x_args)


def _mm_glu_kernel(a_ref, wa_ref, wg_ref, x_ref, o_ref, wab_ref, wgb_ref):
    @pl.when(pl.program_id(1) == 0)
    def _():
        wab_ref[...] = wa_ref[...].astype(BF16)
        wgb_ref[...] = wg_ref[...].astype(BF16)

    a = a_ref[...]
    ga = jnp.dot(a, wab_ref[...], preferred_element_type=F32)
    gb = jnp.dot(a, wgb_ref[...], preferred_element_type=F32)
    o_ref[...] = x_ref[...] + ga * _sigmoid(gb)


def _mm_glu(a, w, layer, x):
    bn = 1024
    _, k, n2 = w.shape
    n_out = n2 // 2
    nb = n_out // bn
    return pl.pallas_call(
        _mm_glu_kernel,
        out_shape=jax.ShapeDtypeStruct((M_TOTAL, n_out), F32),
        grid=(nb, N_ROW_TILES),
        in_specs=[
            pl.BlockSpec((BM, k), lambda n, m: (m, 0)),
            _weight_spec(k, bn, layer),
            _weight_spec(k, bn, layer, nb),
            pl.BlockSpec((BM, bn), lambda n, m: (m, n)),
        ],
        out_specs=pl.BlockSpec((BM, bn), lambda n, m: (m, n)),
        scratch_shapes=[pltpu.VMEM((k, bn), BF16), pltpu.VMEM((k, bn), BF16)],
        compiler_params=_params(2),
        name="mm_glu",
    )(a, w, w, x)


def _mm_ffnup_kernel(h_ref, wa_ref, wg_ref, cw_ref, cb_ref, cache_ref,
                     o_ref, tp_ref, ts_ref, wab_ref, wgb_ref, carry_ref):
    m = pl.program_id(1)

    @pl.when(m == 0)
    def _():
        wab_ref[...] = wa_ref[...].astype(BF16)
        wgb_ref[...] = wg_ref[...].astype(BF16)
        carry_ref[...] = jnp.zeros_like(carry_ref)

    h = h_ref[...]
    a = jnp.dot(h, wab_ref[...], preferred_element_type=F32)
    b = jnp.dot(h, wgb_ref[...], preferred_element_type=F32)
    r1 = pltpu.roll(a, 1, axis=0)
    r2 = pltpu.roll(a, 2, axis=0)
    cb = cb_ref[...]
    w0 = cw_ref[0:1, :]
    w1 = cw_ref[1:2, :]
    w2 = cw_ref[2:3, :]

    def gated(a_rows, prev1, prev2, b_rows):
        conv = cb + w0 * prev2 + w1 * prev1 + w2 * a_rows
        return ((conv * _sigmoid(conv)) * b_rows).astype(BF16)

    o_ref[...] = gated(a, r1, r2, b)

    row8 = lax.broadcasted_iota(jnp.int32, (SUBLANES, 1), 0)
    heads = [tuple(z[s * DEC_SEQ:s * DEC_SEQ + SUBLANES] for z in (a, r1, r2, b))
             for s in range(DEC_BATCH)]
    tails = [a[(s + 1) * DEC_SEQ - SUBLANES:(s + 1) * DEC_SEQ] for s in range(DEC_BATCH)]

    def redo_head(s, before1, before2):
        a8, r1_8, r2_8, b8 = heads[s]
        prev1 = jnp.where(row8 == 0, before1, r1_8)
        prev2 = jnp.where(row8 == 0, before2, jnp.where(row8 == 1, before1, r2_8))
        o_ref[s * DEC_SEQ:s * DEC_SEQ + SUBLANES, :] = gated(a8, prev1, prev2, b8)

    @pl.when(m < N_PROMPT_TILES)
    def _():
        c = carry_ref[...]
        redo_head(0, c[SUBLANES - 1:SUBLANES, :], c[SUBLANES - 2:SUBLANES - 1, :])
        carry_ref[...] = tails[DEC_BATCH - 1]

        @pl.when(m == N_PROMPT_TILES - 1)
        def _():
            tp_ref[...] = tails[DEC_BATCH - 1]

    @pl.when(m == N_PROMPT_TILES)
    def _():
        for s in range(DEC_BATCH):
            c = cache_ref[s]
            redo_head(s, c[1:2, :], c[0:1, :])
            ts_ref[s * SUBLANES:(s + 1) * SUBLANES, :] = tails[s]


def _mm_ffnup(h, w, layer, conv_w, conv_b, cache):
    bn = 1024
    k = w.shape[1]
    nb = FFN_DIM // bn
    return pl.pallas_call(
        _mm_ffnup_kernel,
        out_shape=(
            jax.ShapeDtypeStruct((M_TOTAL, FFN_DIM), BF16),
            jax.ShapeDtypeStruct((SUBLANES, FFN_DIM), F32),
            jax.ShapeDtypeStruct((DEC_BATCH * SUBLANES, FFN_DIM), F32),
        ),
        grid=(nb, N_ROW_TILES),
        in_specs=[
            pl.BlockSpec((BM, k), lambda n, m: (m, 0)),
            _weight_spec(k, bn, layer),
            _weight_spec(k, bn, layer, nb),
            pl.BlockSpec((CONV_W, bn), lambda n, m: (0, n)),
            pl.BlockSpec((1, bn), lambda n, m: (0, n)),
            pl.BlockSpec((DEC_BATCH, CONV_W - 1, bn), lambda n, m: (0, 0, n)),
        ],
        out_specs=(
            pl.BlockSpec((BM, bn), lambda n, m: (m, n)),
            pl.BlockSpec((SUBLANES, bn), lambda n, m: (0, n)),
            pl.BlockSpec((DEC_BATCH * SUBLANES, bn), lambda n, m: (0, n)),
        ),
        scratch_shapes=[pltpu.VMEM((k, bn), BF16), pltpu.VMEM((k, bn), BF16),
                        pltpu.VMEM((SUBLANES, bn), F32)],
        compiler_params=_params(2),
        name="mm_ffn_up",
    )(h, w, w, conv_w, conv_b.reshape(1, FFN_DIM), cache)


def _ret_kernel(*refs, T, sub, has_init):
    if has_init:
        (lg_ref, q_ref, k_ref, v_ref, g_ref, gn_ref, s0_ref,
         o_ref, sl_ref, state_ref, mask_ref) = refs
    else:
        (lg_ref, q_ref, k_ref, v_ref, g_ref, gn_ref,
         o_ref, sl_ref, state_ref, mask_ref) = refs
    blk = pl.program_id(2)
    lg = lg_ref[...][:, 0:1]

    @pl.when(blk == 0)
    def _():
        if has_init:
            state_ref[...] = s0_ref[...]
        else:
            state_ref[...] = jnp.zeros_like(state_ref)
        i = lax.broadcasted_iota(jnp.int32, (T, T), 0)
        j = lax.broadcasted_iota(jnp.int32, (T, T), 1)
        dist = jnp.abs(i - j).astype(F32)
        shift = int(math.log2(CHUNK))
        visible = (j >> shift) <= (i >> shift)
        mask_ref[...] = jnp.where(visible, jnp.exp(dist * lg), 0.0)

    t = lax.broadcasted_iota(jnp.int32, (T, 1), 0).astype(F32)
    cross = jnp.exp((t + 1.0) * lg)
    k_dec = jnp.exp((T - 1.0 - t) * lg)
    decay = jnp.exp(float(T) * lg)
    mask = mask_ref[...]
    gn = gn_ref[...]
    state = state_ref[...]
    for sb in range(sub):
        rows = slice(sb * T, (sb + 1) * T)
        q = q_ref[rows, :]
        k = k_ref[rows, :]
        v = v_ref[rows, :]
        scores = lax.dot_general(q, k, (((1,), (1,)), ((), ())), preferred_element_type=F32)
        scores = scores * mask
        out = jnp.dot(scores.astype(BF16), v, preferred_element_type=F32)
        out = out + jnp.dot(q, state.astype(BF16), preferred_element_type=F32) * cross

        kd = (k.astype(F32) * k_dec).astype(BF16)
        state = decay * state + lax.dot_general(
            kd, v, (((0,), (0,)), ((), ())), preferred_element_type=F32)

        mu = jnp.mean(out, axis=-1, keepdims=True)
        oc = out - mu
        var = jnp.mean(oc * oc, axis=-1, keepdims=True)
        normed = (oc * lax.rsqrt(var + EPS)) * gn
        g = g_ref[rows, :].astype(F32)
        o_ref[rows, :] = (normed * (g * _sigmoid(g))).astype(BF16)
    state_ref[...] = state

    @pl.when(blk == pl.num_programs(2) - 1)
    def _():
        sl_ref[...] = state


def _retention(qk, vg, gn, s0, *, row0, n_seq, n_blk, T, sub=1):
    log_g = np.log1p(-np.exp2(-5.0 - np.arange(RET_HEADS, dtype=np.float32))).astype(np.float32)
    lg = jnp.asarray(np.broadcast_to(log_g[:, None, None], (RET_HEADS, 1, LANES)).copy())
    R = sub * T
    rb0 = row0 // R
    k_off = RET_QK // RET_DK
    g_off = RET_VD // RET_DV
    has_init = s0 is not None

    def rows(h, s, b):
        return rb0 + s * n_blk + b

    in_specs = [
        pl.BlockSpec((None, 1, LANES), lambda h, s, b: (h, 0, 0)),
        pl.BlockSpec((R, RET_DK), lambda h, s, b: (rows(h, s, b), h)),
        pl.BlockSpec((R, RET_DK), lambda h, s, b: (rows(h, s, b), k_off + h)),
        pl.BlockSpec((R, RET_DV), lambda h, s, b: (rows(h, s, b), h)),
        pl.BlockSpec((R, RET_DV), lambda h, s, b: (rows(h, s, b), g_off + h)),
        pl.BlockSpec((1, RET_DV), lambda h, s, b: (0, h)),
    ]
    args = [lg, qk, qk, vg, vg, gn.reshape(1, RET_VD)]
    if has_init:
        s0_stack, layer = s0
        in_specs.append(pl.BlockSpec((None, None, None, RET_DK, RET_DV),
                                     lambda h, s, b: (layer, s, h, 0, 0)))
        args.append(s0_stack)
    return pl.pallas_call(
        functools.partial(_ret_kernel, T=T, sub=sub, has_init=has_init),
        out_shape=(
            jax.ShapeDtypeStruct((n_seq * n_blk * R, RET_VD), BF16),
            jax.ShapeDtypeStruct((n_seq, RET_HEADS, RET_DK, RET_DV), F32),
        ),
        grid=(RET_HEADS, n_seq, n_blk),
        in_specs=in_specs,
        out_specs=(
            pl.BlockSpec((R, RET_DV), lambda h, s, b: (s * n_blk + b, h)),
            pl.BlockSpec((None, None, RET_DK, RET_DV), lambda h, s, b: (s, h, 0, 0)),
        ),
        scratch_shapes=[pltpu.VMEM((RET_DK, RET_DV), F32), pltpu.VMEM((T, T), F32)],
        compiler_params=_params(3),
        name="retention_init" if has_init else "retention",
    )(*args)


def _ssm_prep_kernel(ar_ref, ai_ref, ldt_ref, br_ref, bi_ref, cr_ref, ci_ref,
                     bc_ref, cc_ref, lag_ref, a16r_ref, a16i_ref, vr_s, vi_s):
    ar = ar_ref[...]
    ai = ai_ref[...]
    dt = jnp.exp(ldt_ref[...])
    mag = jnp.exp(ar * dt)
    ang = ai * dt
    abr = mag * jnp.cos(ang)
    abi = mag * jnp.sin(ang)
    den = ar * ar + ai * ai
    nr = abr - 1.0
    ni = abi
    cfr = (nr * ar + ni * ai) / den
    cfi = (ni * ar - nr * ai) / den

    def b4(z):
        return z[:, None]

    br = br_ref[...]
    bi = bi_ref[...]
    bbr = b4(cfr) * br - b4(cfi) * bi
    bbi = b4(cfr) * bi + b4(cfi) * br
    cre = cr_ref[...]
    cim = ci_ref[...]

    powers = []
    pr = jnp.ones_like(ar)
    pi = jnp.zeros_like(ar)
    for _ in range(SSM_T + 1):
        powers.append((pr, pi))
        pr, pi = pr * abr - pi * abi, pr * abi + pi * abr
    a16r_ref[...] = powers[SSM_T][0]
    a16i_ref[...] = powers[SSM_T][1]

    for m in range(SSM_T + 1):
        pr, pi = powers[m]
        vr = b4(pr) * cre - b4(pi) * cim
        vi = b4(pr) * cim + b4(pi) * cre
        if m < SSM_T:
            vr_s[:, :, m] = vr
            vi_s[:, :, m] = vi
        if m >= 1:
            cc_ref[:, :, m - 1, :, 0:LANES] = vr.astype(BF16)
            cc_ref[:, :, m - 1, :, LANES:2 * LANES] = (-vi).astype(BF16)

    for t in range(SSM_T):
        pr, pi = powers[SSM_T - 1 - t]
        bc_ref[:, :, t, :, 0:LANES] = (b4(pr) * bbr - b4(pi) * bbi).astype(BF16)
        bc_ref[:, :, t, :, LANES:2 * LANES] = (b4(pr) * bbi + b4(pi) * bbr).astype(BF16)

    width = SSM_T * SSM_GROUP
    lane = lax.broadcasted_iota(jnp.int32, (SSM_GROUP, LANES), 1)
    contract_last = (((1,), (1,)), ((), ()))
    for q in range(SSM_PB):
        for half in range(2):
            gl = 2 * q + half
            vr_all = vr_s[q, half].reshape(width, LANES)
            vi_all = vi_s[q, half].reshape(width, LANES)
            krow = (lax.dot_general(bbr[q, half], vr_all, contract_last,
                                    precision=lax.Precision.HIGHEST, preferred_element_type=F32)
                    - lax.dot_general(bbi[q, half], vi_all, contract_last,
                                      precision=lax.Precision.HIGHEST, preferred_element_type=F32))
            in_block = (lane >= gl * SSM_GROUP) & (lane < (gl + 1) * SSM_GROUP)
            for m in range(SSM_T):
                shift = ((gl - m) * SSM_GROUP) % width
                moved = krow if shift == 0 else pltpu.roll(krow, shift, axis=1)
                lag_ref[m, gl * SSM_GROUP:(gl + 1) * SSM_GROUP, :] = jnp.where(
                    in_block, moved[:, 0:LANES], 0.0).astype(BF16)


def _pair_pack(x):
    x = x.reshape(SSM_PAIRS, 2, SSM_GROUP, SSM_STATE)
    lo = jnp.pad(x[:, 0], ((0, 0), (0, 0), (0, SSM_STATE)))
    hi = jnp.pad(x[:, 1], ((0, 0), (0, 0), (SSM_STATE, 0)))
    return jnp.stack([lo, hi], axis=1)


def _ssm_prep(a_re, a_im, log_dt, b_re, b_im, c_re, c_im):
    assert 2 * SSM_PB * SSM_GROUP == LANES
    ar = a_re.reshape(SSM_PAIRS, 1, LANES)
    ai = a_im.reshape(SSM_PAIRS, 1, LANES)
    ldt = jnp.repeat(log_dt, SSM_STATE).reshape(SSM_PAIRS, 1, LANES)
    b2r = _pair_pack(jnp.transpose(b_re, (0, 2, 1)))
    b2i = _pair_pack(jnp.transpose(b_im, (0, 2, 1)))
    c2r = _pair_pack(c_re)
    c2i = _pair_pack(c_im)
    vec = pl.BlockSpec((SSM_PB, 1, LANES), lambda p: (p, 0, 0))
    mat = pl.BlockSpec((SSM_PB, 2, SSM_GROUP, LANES), lambda p: (p, 0, 0, 0))
    proj = pl.BlockSpec((SSM_PB, 2, SSM_T, SSM_GROUP, 2 * LANES), lambda p: (p, 0, 0, 0, 0))
    bc, cc, lag, a16r, a16i = pl.pallas_call(
        _ssm_prep_kernel,
        out_shape=(
            jax.ShapeDtypeStruct((SSM_PAIRS, 2, SSM_T, SSM_GROUP, 2 * LANES), BF16),
            jax.ShapeDtypeStruct((SSM_PAIRS, 2, SSM_T, SSM_GROUP, 2 * LANES), BF16),
            jax.ShapeDtypeStruct((N_LANE_TILES, SSM_T, LANES, LANES), BF16),
            jax.ShapeDtypeStruct((SSM_PAIRS, 1, LANES), F32),
            jax.ShapeDtypeStruct((SSM_PAIRS, 1, LANES), F32),
        ),
        grid=(N_LANE_TILES,),
        in_specs=[vec, vec, vec, mat, mat, mat, mat],
        out_specs=(proj, proj,
                   pl.BlockSpec((None, SSM_T, LANES, LANES), lambda p: (p, 0, 0, 0)),
                   vec, vec),
        scratch_shapes=[pltpu.VMEM((SSM_PB, 2, SSM_T, SSM_GROUP, LANES), F32)] * 2,
        compiler_params=_params(1),
        name="ssm_prep",
    )(ar, ai, ldt, b2r, b2i, c2r, c2i)
    row = SSM_GROUPS * SSM_STATE
    a16r = jnp.broadcast_to(a16r.reshape(1, row), (SSM_SEGS, row))
    a16i = jnp.broadcast_to(a16i.reshape(1, row), (SSM_SEGS, row))
    return bc, cc, lag, a16r, a16i


SSM_CHUNK_COLS = SSM_T * LANES
SSM_STATE_COLS = SSM_PB * LANES
SSM_CHUNKS_P = M_PROMPT // SSM_T
SSM_COL_BLOCKS = 4


def _ssm_main_kernel(x_ref, lag_ref, bc_ref, cc_ref, ar_ref, ai_ref, h0r_ref, h0i_ref,
                     y_ref, fpr_ref, fpi_ref, fsr_ref, fsi_ref,
                     w_s, bcw_s, ccw_s, d_s, s_s):
    w = SSM_STATE_COLS

    @pl.when(pl.program_id(0) == 0)
    def _():
        w_s[...] = jnp.zeros_like(w_s)
        bcw_s[...] = jnp.zeros_like(bcw_s)
        ccw_s[...] = jnp.zeros_like(ccw_s)

    for t in range(SSM_T):
        for t2 in range(t, SSM_T):
            w_s[t * LANES:(t + 1) * LANES, t2 * LANES:(t2 + 1) * LANES] = lag_ref[t2 - t]
    for q in range(SSM_PB):
        for half in range(2):
            for t in range(SSM_T):
                r0 = t * LANES + (2 * q + half) * SSM_GROUP
                rows = slice(r0, r0 + SSM_GROUP)
                bcw_s[rows, q * LANES:(q + 1) * LANES] = bc_ref[q, half, t, :, 0:LANES]
                bcw_s[rows, w + q * LANES:w + (q + 1) * LANES] = bc_ref[q, half, t, :, LANES:2 * LANES]
                ccw_s[rows, q * LANES:(q + 1) * LANES] = cc_ref[q, half, t, :, 0:LANES]
                ccw_s[rows, w + q * LANES:w + (q + 1) * LANES] = cc_ref[q, half, t, :, LANES:2 * LANES]

    x = x_ref[...]
    for k2 in range(SSM_PB):
        d = jnp.dot(x, bcw_s[:, 2 * k2 * LANES:2 * (k2 + 1) * LANES], preferred_element_type=F32)
        d_s[2 * k2] = d[:, 0:LANES]
        d_s[2 * k2 + 1] = d[:, LANES:2 * LANES]

    tiles = range(SSM_PB)
    ar = [ar_ref[:, k * LANES:(k + 1) * LANES] for k in tiles]
    ai = [ai_ref[:, k * LANES:(k + 1) * LANES] for k in tiles]

    def advance(sr, si, rows, record):
        out_r, out_i = [], []
        for k in tiles:
            if record:
                s_s[k, rows, :] = sr[k]
                s_s[SSM_PB + k, rows, :] = si[k]
            dr = d_s[k, rows, :]
            di = d_s[SSM_PB + k, rows, :]
            out_r.append(ar[k] * sr[k] - ai[k] * si[k] + dr)
            out_i.append(ar[k] * si[k] + ai[k] * sr[k] + di)
        return out_r, out_i

    def prompt_rows(i):
        return pl.ds(i, SSM_SEGS, stride=SSM_STEPS_P)

    zeros = [jnp.zeros((SSM_SEGS, LANES), F32) for _ in tiles]
    er, ei = zeros, zeros
    for i in range(SSM_STEPS_P):
        er, ei = advance(er, ei, prompt_rows(i), record=False)

    sr, si = [], []
    for k in tiles:
        pr, pi = ar[k][0:1], ai[k][0:1]
        for _ in range(int(math.log2(SSM_STEPS_P))):
            pr, pi = pr * pr - pi * pi, 2.0 * pr * pi
        cr = jnp.zeros((1, LANES), F32)
        ci = cr
        starts_r, starts_i = [], []
        for s in range(SSM_SEGS):
            starts_r.append(cr)
            starts_i.append(ci)
            cr, ci = (pr * cr - pi * ci + er[k][s:s + 1], pr * ci + pi * cr + ei[k][s:s + 1])
        cols = slice(k * LANES, (k + 1) * LANES)
        fpr_ref[:, cols] = jnp.broadcast_to(cr, (SSM_SEGS, LANES))
        fpi_ref[:, cols] = jnp.broadcast_to(ci, (SSM_SEGS, LANES))
        sr.append(jnp.concatenate(starts_r, axis=0))
        si.append(jnp.concatenate(starts_i, axis=0))

    for i in range(SSM_STEPS_P):
        sr, si = advance(sr, si, prompt_rows(i), record=True)
    sr = [h0r_ref[:, k * LANES:(k + 1) * LANES] for k in tiles]
    si = [h0i_ref[:, k * LANES:(k + 1) * LANES] for k in tiles]
    for i in range(SSM_STEPS_S):
        sr, si = advance(sr, si, pl.ds(SSM_CHUNKS_P + i, DEC_BATCH, stride=SSM_STEPS_S), record=True)
    for k in tiles:
        fsr_ref[:, k * LANES:(k + 1) * LANES] = sr[k]
        fsi_ref[:, k * LANES:(k + 1) * LANES] = si[k]

    states = jnp.concatenate([s_s[k].astype(BF16) for k in range(2 * SSM_PB)], axis=1)
    contract_last = (((1,), (1,)), ((), ()))
    tokens_per_block = SSM_T // SSM_COL_BLOCKS
    for cb in range(SSM_COL_BLOCKS):
        k_hi = (cb + 1) * tokens_per_block * LANES
        cols = slice(cb * tokens_per_block * LANES, k_hi)
        carried = lax.dot_general(states, ccw_s[cols, :], contract_last, preferred_element_type=F32)
        y_ref[:, cols] = jnp.dot(x_ref[:, 0:k_hi], w_s[0:k_hi, cols],
                                 preferred_element_type=F32) + carried


def _ssm_main(x_c, lag, bc, cc, a16r, a16i, h0r, h0i):
    w = SSM_STATE_COLS
    row = SSM_GROUPS * SSM_STATE
    st = pl.BlockSpec((SSM_SEGS, w), lambda j: (0, j))
    st_shape = jax.ShapeDtypeStruct((SSM_SEGS, row), F32)
    proj = pl.BlockSpec((SSM_PB, 2, SSM_T, SSM_GROUP, 2 * LANES), lambda j: (j, 0, 0, 0, 0))
    chunk_rows = pl.BlockSpec((None, N_CHUNKS, SSM_CHUNK_COLS), lambda j: (j, 0, 0))
    return pl.pallas_call(
        _ssm_main_kernel,
        out_shape=(jax.ShapeDtypeStruct((N_LANE_TILES, N_CHUNKS, SSM_CHUNK_COLS), F32),
                   st_shape, st_shape, st_shape, st_shape),
        grid=(N_LANE_TILES,),
        in_specs=[
            chunk_rows,
            pl.BlockSpec((None, SSM_T, LANES, LANES), lambda j: (j, 0, 0, 0)),
            proj, proj, st, st, st, st,
        ],
        out_specs=(chunk_rows, st, st, st, st),
        scratch_shapes=[
            pltpu.VMEM((SSM_CHUNK_COLS, SSM_CHUNK_COLS), BF16),
            pltpu.VMEM((SSM_CHUNK_COLS, 2 * w), BF16),
            pltpu.VMEM((SSM_CHUNK_COLS, 2 * w), BF16),
            pltpu.VMEM((2 * SSM_PB, N_CHUNKS, LANES), F32),
            pltpu.VMEM((2 * SSM_PB, N_CHUNKS, LANES), F32),
        ],
        compiler_params=_params(1),
        name="ssm_main",
    )(x_c, lag, bc, cc, a16r, a16i, h0r, h0i)


def _ssm_post_kernel(y_ref, x_ref, g_ref, d_ref, o_ref, y_s):
    for t in range(SSM_T):
        for j in range(N_LANE_TILES):
            y_s[j, pl.ds(t, CHUNKS_PER_TILE, stride=SSM_T), :] = y_ref[j, :, t * LANES:(t + 1) * LANES]
    x = x_ref[...]
    ms = jnp.mean(x * x, axis=-1, keepdims=True)
    h = (x * lax.rsqrt(ms + EPS)) * g_ref[...]
    c = math.sqrt(2.0 / math.pi)
    for j in range(N_LANE_TILES):
        cols = slice(j * LANES, (j + 1) * LANES)
        y = y_s[j] + d_ref[:, cols] * h[:, cols]
        gelu = 0.5 * y * (1.0 + jnp.tanh(c * (y + 0.044715 * (y * y * y))))
        o_ref[:, cols] = gelu.astype(BF16)


def _ssm_post(y_c, x, g, d):
    spec = pl.BlockSpec((BM, D_MODEL), lambda m: (m, 0))
    vec = pl.BlockSpec((1, D_MODEL), lambda m: (0, 0))
    return pl.pallas_call(
        _ssm_post_kernel,
        out_shape=jax.ShapeDtypeStruct((M_TOTAL, D_MODEL), BF16),
        grid=(N_ROW_TILES,),
        in_specs=[pl.BlockSpec((N_LANE_TILES, CHUNKS_PER_TILE, SSM_CHUNK_COLS), lambda m: (0, m, 0)),
                  spec, vec, vec],
        out_specs=spec,
        scratch_shapes=[pltpu.VMEM((N_LANE_TILES, BM, LANES), F32)],
        compiler_params=_params(1),
        name="ssm_post",
    )(y_c, x, g.reshape(1, D_MODEL), d.reshape(1, D_MODEL))


def kernel(x_prompt, x_sample, state_ret, state_ssm_re, state_ssm_im, cache_conv, norm_mix, norm_ffn, norm_final, ret_w_in, ret_gn, ret_w_out, ssm_a_re, ssm_a_im, ssm_log_dt, ssm_b_re, ssm_b_im, ssm_c_re, ssm_c_im, ssm_d, ssm_w_glu, ffn_w_up, ffn_conv_w, ffn_conv_b, ffn_w_down):
    x = (x_prompt.reshape(M_PROMPT, D_MODEL), x_sample.reshape(M_SAMPLE, D_MODEL))
    cos, sin = _rope_tables()

    ret_p, ret_s = [], []
    re_p, im_p, re_s, im_s = [], [], [], []
    conv_p, conv_s = [], []
    state_row = SSM_GROUPS * SSM_STATE
    for i in range(DEPTH):
        j = i // 2
        if i % 2 == 0:
            if isinstance(x, tuple):
                h = _rmsnorm_split(x[0], x[1], norm_mix[i], BF16)
            else:
                h = _rmsnorm(x, norm_mix[i], BF16)
            qk, vg = _mm_retin(h, ret_w_in, j, cos, sin)
            o_p, s_p = _retention(qk, vg, ret_gn[j], None, row0=0, n_seq=1,
                                  n_blk=M_PROMPT // (RET_T * RET_SUB), T=RET_T, sub=RET_SUB)
            o_s, s_s = _retention(qk, vg, ret_gn[j], (state_ret, j), row0=M_PROMPT, n_seq=DEC_BATCH,
                                  n_blk=1, T=DEC_SEQ)
            ret_p.append(s_p)
            ret_s.append(s_s)
            x = _mm_res((o_p, o_s), ret_w_out, j, x)
        else:
            h_c = _rmsnorm_chunked(x, norm_mix[i])
            bc, cc, lag, a16r, a16i = _ssm_prep(ssm_a_re[j], ssm_a_im[j], ssm_log_dt[j],
                                               ssm_b_re[j], ssm_b_im[j], ssm_c_re[j], ssm_c_im[j])
            y_c, fpr, fpi, fsr, fsi = _ssm_main(
                h_c, lag, bc, cc, a16r, a16i,
                state_ssm_re[j].reshape(DEC_BATCH, state_row),
                state_ssm_im[j].reshape(DEC_BATCH, state_row))
            re_p.append(fpr[0:1].reshape(1, SSM_GROUPS, SSM_STATE))
            im_p.append(fpi[0:1].reshape(1, SSM_GROUPS, SSM_STATE))
            re_s.append(fsr.reshape(DEC_BATCH, SSM_GROUPS, SSM_STATE))
            im_s.append(fsi.reshape(DEC_BATCH, SSM_GROUPS, SSM_STATE))
            gl = _ssm_post(y_c, x, norm_mix[i], ssm_d[j])
            x = _mm_glu(gl, ssm_w_glu, j, x)

        h = _rmsnorm(x, norm_ffn[i], BF16)
        u, tail_p, tail_s = _mm_ffnup(h, ffn_w_up, i, ffn_conv_w[i], ffn_conv_b[i], cache_conv[i])
        conv_p.append(tail_p[SUBLANES - (CONV_W - 1):].reshape(1, CONV_W - 1, FFN_DIM))
        conv_s.append(tail_s.reshape(DEC_BATCH, SUBLANES, FFN_DIM)[:, SUBLANES - (CONV_W - 1):])
        x = _mm_res(u, ffn_w_down, i, x)

    y_prompt = _rmsnorm(x, norm_final, F32, 0, N_PROMPT_TILES).reshape(1, SEQ, D_MODEL)
    y_sample = _rmsnorm(x, norm_final, F32, N_PROMPT_TILES, 1).reshape(DEC_BATCH, DEC_SEQ, D_MODEL)
    return (y_prompt, y_sample, jnp.stack(ret_p), jnp.stack(ret_s),
            jnp.stack(re_p), jnp.stack(im_p), jnp.stack(re_s), jnp.stack(im_s),
            jnp.stack(conv_p), jnp.stack(conv_s))
```

```python
import functools
import math

import numpy as np
import jax
import jax.numpy as jnp
from jax import lax
from jax.experimental import pallas as pl
from jax.experimental.pallas import tpu as pltpu

F32 = jnp.float32
BF16 = jnp.bfloat16

D_MODEL = 2048
SEQ = 8192
DEPTH = 4
DEC_BATCH = 8
DEC_SEQ = 64
PAST_LEN = 1024
CHUNK = 64
RET_HEADS = 8
RET_DK = D_MODEL // RET_HEADS
RET_DV = 2 * RET_DK
RET_QK = RET_HEADS * RET_DK
RET_VD = RET_HEADS * RET_DV
ROPE_BASE = 10000.0
SSM_GROUP = 16
SSM_GROUPS = D_MODEL // SSM_GROUP
SSM_STATE = 64
FFN_DIM = 2 * D_MODEL
CONV_W = 3
EPS = 1e-6

M_PROMPT = SEQ
M_SAMPLE = DEC_BATCH * DEC_SEQ
M_TOTAL = M_PROMPT + M_SAMPLE

LANES = 128
SUBLANES = 8
VMEM_LIMIT_BYTES = 56 * 1024 * 1024

BM = M_SAMPLE
N_ROW_TILES = M_TOTAL // BM
N_PROMPT_TILES = M_PROMPT // BM

RET_T = 256
RET_SUB = 2

SSM_T = 16
SSM_PAIRS = SSM_GROUPS // 2
SSM_PB = 4
SSM_SEGS = 8
SSM_STEPS_P = M_PROMPT // SSM_T // SSM_SEGS
SSM_STEPS_S = DEC_SEQ // SSM_T
SSM_NC = (SSM_STEPS_P + SSM_STEPS_S) * SSM_SEGS


def _params(n_axes):
    return pltpu.CompilerParams(dimension_semantics=("arbitrary",) * n_axes,
                                vmem_limit_bytes=VMEM_LIMIT_BYTES)


def _sigmoid(x):
    return 1.0 / (1.0 + jnp.exp(-x))


def _rope_kernel(cos_ref, sin_ref):
    m = pl.program_id(0)
    half = RET_DK // 2
    freq = lax.broadcasted_iota(jnp.int32, (1, half), 1).astype(F32)
    inv = ROPE_BASE ** (-freq / half)
    r = m * BM + lax.broadcasted_iota(jnp.int32, (BM, half), 0)
    pos = jnp.where(r < M_PROMPT, r, PAST_LEN + ((r - M_PROMPT) & (DEC_SEQ - 1)))
    ang = pos.astype(F32) * inv
    cos_ref[...] = jnp.cos(ang)
    sin_ref[...] = jnp.sin(ang)


def _rope_tables():
    half = RET_DK // 2
    spec = pl.BlockSpec((BM, half), lambda m: (m, 0))
    return pl.pallas_call(
        _rope_kernel,
        out_shape=(jax.ShapeDtypeStruct((M_TOTAL, half), F32),) * 2,
        grid=(N_ROW_TILES,),
        out_specs=(spec, spec),
        compiler_params=_params(1),
        name="rope_tables",
    )()


def _norm_kernel(x_ref, g_ref, o_ref):
    x = x_ref[...]
    ms = jnp.mean(x * x, axis=-1, keepdims=True)
    o_ref[...] = ((x * lax.rsqrt(ms + EPS)) * g_ref[...]).astype(o_ref.dtype)


def _rmsnorm(x, g, out_dtype, tile0=0, n_tiles=N_ROW_TILES):
    return pl.pallas_call(
        _norm_kernel,
        out_shape=jax.ShapeDtypeStruct((n_tiles * BM, D_MODEL), out_dtype),
        grid=(n_tiles,),
        in_specs=[pl.BlockSpec((BM, D_MODEL), lambda m: (tile0 + m, 0)),
                  pl.BlockSpec((1, D_MODEL), lambda m: (0, 0))],
        out_specs=pl.BlockSpec((BM, D_MODEL), lambda m: (m, 0)),
        compiler_params=_params(1),
        name="rmsnorm",
    )(x, g.reshape(1, D_MODEL))


def _norm_split_kernel(xp_ref, xs_ref, g_ref, o_ref):
    x = jnp.where(pl.program_id(0) < N_PROMPT_TILES, xp_ref[...], xs_ref[...])
    ms = jnp.mean(x * x, axis=-1, keepdims=True)
    o_ref[...] = ((x * lax.rsqrt(ms + EPS)) * g_ref[...]).astype(o_ref.dtype)


def _rmsnorm_split(x_prompt, x_sample, g, out_dtype):
    return pl.pallas_call(
        _norm_split_kernel,
        out_shape=jax.ShapeDtypeStruct((M_TOTAL, D_MODEL), out_dtype),
        grid=(N_ROW_TILES,),
        in_specs=[pl.BlockSpec((BM, D_MODEL), lambda m: (jnp.minimum(m, N_PROMPT_TILES - 1), 0)),
                  pl.BlockSpec((BM, D_MODEL), lambda m: (0, 0)),
                  pl.BlockSpec((1, D_MODEL), lambda m: (0, 0))],
        out_specs=pl.BlockSpec((BM, D_MODEL), lambda m: (m, 0)),
        compiler_params=_params(1),
        name="rmsnorm_split",
    )(x_prompt, x_sample, g.reshape(1, D_MODEL))


CHUNKS_PER_TILE = BM // 16
N_CHUNKS = M_TOTAL // 16
N_LANE_TILES = D_MODEL // LANES


def _norm_chunked_kernel(x_ref, g_ref, o_ref, h_s):
    x = x_ref[...]
    ms = jnp.mean(x * x, axis=-1, keepdims=True)
    h = (x * lax.rsqrt(ms + EPS)) * g_ref[...]
    for j in range(N_LANE_TILES):
        h_s[j] = h[:, j * LANES:(j + 1) * LANES]
    for t in range(SSM_T):
        for j in range(N_LANE_TILES):
            rows = h_s[j, pl.ds(t, CHUNKS_PER_TILE, stride=SSM_T), :]
            o_ref[j, :, t * LANES:(t + 1) * LANES] = rows.astype(BF16)


def _rmsnorm_chunked(x, g):
    return pl.pallas_call(
        _norm_chunked_kernel,
        out_shape=jax.ShapeDtypeStruct((N_LANE_TILES, N_CHUNKS, SSM_T * LANES), BF16),
        grid=(N_ROW_TILES,),
        in_specs=[pl.BlockSpec((BM, D_MODEL), lambda m: (m, 0)),
                  pl.BlockSpec((1, D_MODEL), lambda m: (0, 0))],
        out_specs=pl.BlockSpec((N_LANE_TILES, CHUNKS_PER_TILE, SSM_T * LANES), lambda m: (0, m, 0)),
        scratch_shapes=[pltpu.VMEM((N_LANE_TILES, BM, LANES), F32)],
        compiler_params=_params(1),
        name="rmsnorm_chunked",
    )(x, g.reshape(1, D_MODEL))


def _weight_spec(k, bn, layer, col0=0):
    return pl.BlockSpec((None, k, bn), lambda n, m: (layer, 0, col0 + n), pipeline_mode=pl.Buffered(1))


RET_IN_BN = 2048


def _mm_qk_kernel(a_ref, w_ref, cos_ref, sin_ref, o_ref, wb_ref):
    n = pl.program_id(0)
    n_q = RET_QK // RET_IN_BN

    @pl.when(pl.program_id(1) == 0)
    def _():
        wb_ref[...] = w_ref[...].astype(BF16)

    acc = jnp.dot(a_ref[...], wb_ref[...], preferred_element_type=F32)
    scale = jnp.where(n >= n_q, RET_DK ** -0.5, 1.0).astype(F32)
    c = cos_ref[...]
    s = sin_ref[...]
    half = RET_DK // 2
    for hh in range(RET_IN_BN // RET_DK):
        lo = hh * RET_DK
        x1 = acc[:, lo:lo + half]
        x2 = acc[:, lo + half:lo + RET_DK]
        o_ref[:, lo:lo + half] = ((x1 * c - x2 * s) * scale).astype(BF16)
        o_ref[:, lo + half:lo + RET_DK] = ((x1 * s + x2 * c) * scale).astype(BF16)


def _mm_cast_kernel(a_ref, w_ref, o_ref, wb_ref):
    @pl.when(pl.program_id(1) == 0)
    def _():
        wb_ref[...] = w_ref[...].astype(BF16)

    o_ref[...] = jnp.dot(a_ref[...], wb_ref[...], preferred_element_type=F32).astype(BF16)


def _mm_retin(h, w, layer, cos, sin):
    bn = RET_IN_BN
    _, k, n_out = w.shape
    half = RET_DK // 2
    n_qk = 2 * RET_QK // bn
    a_spec = pl.BlockSpec((BM, k), lambda n, m: (m, 0))
    out_spec = pl.BlockSpec((BM, bn), lambda n, m: (m, n))
    rope_spec = pl.BlockSpec((BM, half), lambda n, m: (m, 0))
    qk = pl.pallas_call(
        _mm_qk_kernel,
        out_shape=jax.ShapeDtypeStruct((M_TOTAL, 2 * RET_QK), BF16),
        grid=(n_qk, N_ROW_TILES),
        in_specs=[a_spec, _weight_spec(k, bn, layer), rope_spec, rope_spec],
        out_specs=out_spec,
        scratch_shapes=[pltpu.VMEM((k, bn), BF16)],
        compiler_params=_params(2),
        name="mm_ret_qk",
    )(h, w, cos, sin)
    vg = pl.pallas_call(
        _mm_cast_kernel,
        out_shape=jax.ShapeDtypeStruct((M_TOTAL, 2 * RET_VD), BF16),
        grid=(n_out // bn - n_qk, N_ROW_TILES),
        in_specs=[a_spec, _weight_spec(k, bn, layer, n_qk)],
        out_specs=out_spec,
        scratch_shapes=[pltpu.VMEM((k, bn), BF16)],
        compiler_params=_params(2),
        name="mm_ret_vg",
    )(h, w)
    return qk, vg


def _row_split_specs(cols, col_index):
    return [pl.BlockSpec((BM, cols), lambda n, m: (jnp.minimum(m, N_PROMPT_TILES - 1), col_index(n))),
            pl.BlockSpec((BM, cols), lambda n, m: (0, col_index(n)))]


def _mm_res_kernel(*refs, split_a, split_x):
    refs = list(refs)
    ap_ref = refs.pop(0)
    as_ref = refs.pop(0) if split_a else None
    w_ref = refs.pop(0)
    xp_ref = refs.pop(0)
    xs_ref = refs.pop(0) if split_x else None
    o_ref, wb_ref = refs
    m = pl.program_id(1)
    on_prompt = m < N_PROMPT_TILES

    @pl.when(m == 0)
    def _():
        wb_ref[...] = w_ref[...].astype(BF16)

    def body(a_ref):
        x = xp_ref[...]
        if split_x:
            x = jnp.where(on_prompt, x, xs_ref[...])
        o_ref[...] = x + jnp.dot(a_ref[...], wb_ref[...], preferred_element_type=F32)

    if split_a:
        pl.when(on_prompt)(lambda: body(ap_ref))
        pl.when(jnp.logical_not(on_prompt))(lambda: body(as_ref))
    else:
        body(ap_ref)


def _mm_res(a, w, layer, x):
    bn = 1024
    _, k, n_out = w.shape
    split_a = isinstance(a, tuple)
    split_x = isinstance(x, tuple)
    a_specs = _row_split_specs(k, lambda n: 0) if split_a else [pl.BlockSpec((BM, k), lambda n, m: (m, 0))]
    x_specs = _row_split_specs(bn, lambda n: n) if split_x else [pl.BlockSpec((BM, bn), lambda n, m: (m, n))]
    a_args = list(a) if split_a else [a]
    x_args = list(x) if split_x else [x]
    return pl.pallas_call(
        functools.partial(_mm_res_kernel, split_a=split_a, split_x=split_x),
        out_shape=jax.ShapeDtypeStruct((M_TOTAL, n_out), F32),
        grid=(n_out // bn, N_ROW_TILES),
        in_specs=a_specs + [_weight_spec(k, bn, layer)] + x_specs,
        out_specs=pl.BlockSpec((BM, bn), lambda n, m: (m, n)),
        scratch_shapes=[pltpu.VMEM((k, bn), BF16)],
        compiler_params=_params(2),
        name="mm_residual",
    )(*a_args, w, *x_args)


def _mm_glu_kernel(a_ref, wa_ref, wg_ref, x_ref, o_ref, wab_ref, wgb_ref):
    @pl.when(pl.program_id(1) == 0)
    def _():
        wab_ref[...] = wa_ref[...].astype(BF16)
        wgb_ref[...] = wg_ref[...].astype(BF16)

    a = a_ref[...]
    ga = jnp.dot(a, wab_ref[...], preferred_element_type=F32)
    gb = jnp.dot(a, wgb_ref[...], preferred_element_type=F32)
    o_ref[...] = x_ref[...] + ga * _sigmoid(gb)


def _mm_glu(a, w, layer, x):
    bn = 1024
    _, k, n2 = w.shape
    n_out = n2 // 2
    nb = n_out // bn
    return pl.pallas_call(
        _mm_glu_kernel,
        out_shape=jax.ShapeDtypeStruct((M_TOTAL, n_out), F32),
        grid=(nb, N_ROW_TILES),
        in_specs=[
            pl.BlockSpec((BM, k), lambda n, m: (m, 0)),
            _weight_spec(k, bn, layer),
            _weight_spec(k, bn, layer, nb),
            pl.BlockSpec((BM, bn), lambda n, m: (m, n)),
        ],
        out_specs=pl.BlockSpec((BM, bn), lambda n, m: (m, n)),
        scratch_shapes=[pltpu.VMEM((k, bn), BF16), pltpu.VMEM((k, bn), BF16)],
        compiler_params=_params(2),
        name="mm_glu",
    )(a, w, w, x)


def _mm_ffnup_kernel(h_ref, wa_ref, wg_ref, cw_ref, cb_ref, cache_ref,
                     o_ref, tp_ref, ts_ref, wab_ref, wgb_ref, carry_ref):
    m = pl.program_id(1)

    @pl.when(m == 0)
    def _():
        wab_ref[...] = wa_ref[...].astype(BF16)
        wgb_ref[...] = wg_ref[...].astype(BF16)
        carry_ref[...] = jnp.zeros_like(carry_ref)

    h = h_ref[...]
    a = jnp.dot(h, wab_ref[...], preferred_element_type=F32)
    b = jnp.dot(h, wgb_ref[...], preferred_element_type=F32)
    r1 = pltpu.roll(a, 1, axis=0)
    r2 = pltpu.roll(a, 2, axis=0)
    cb = cb_ref[...]
    w0 = cw_ref[0:1, :]
    w1 = cw_ref[1:2, :]
    w2 = cw_ref[2:3, :]

    def gated(a_rows, prev1, prev2, b_rows):
        conv = cb + w0 * prev2 + w1 * prev1 + w2 * a_rows
        return ((conv * _sigmoid(conv)) * b_rows).astype(BF16)

    o_ref[...] = gated(a, r1, r2, b)

    row8 = lax.broadcasted_iota(jnp.int32, (SUBLANES, 1), 0)
    heads = [tuple(z[s * DEC_SEQ:s * DEC_SEQ + SUBLANES] for z in (a, r1, r2, b))
             for s in range(DEC_BATCH)]
    tails = [a[(s + 1) * DEC_SEQ - SUBLANES:(s + 1) * DEC_SEQ] for s in range(DEC_BATCH)]

    def redo_head(s, before1, before2):
        a8, r1_8, r2_8, b8 = heads[s]
        prev1 = jnp.where(row8 == 0, before1, r1_8)
        prev2 = jnp.where(row8 == 0, before2, jnp.where(row8 == 1, before1, r2_8))
        o_ref[s * DEC_SEQ:s * DEC_SEQ + SUBLANES, :] = gated(a8, prev1, prev2, b8)

    @pl.when(m < N_PROMPT_TILES)
    def _():
        c = carry_ref[...]
        redo_head(0, c[SUBLANES - 1:SUBLANES, :], c[SUBLANES - 2:SUBLANES - 1, :])
        carry_ref[...] = tails[DEC_BATCH - 1]

        @pl.when(m == N_PROMPT_TILES - 1)
        def _():
            tp_ref[...] = tails[DEC_BATCH - 1]

    @pl.when(m == N_PROMPT_TILES)
    def _():
        for s in range(DEC_BATCH):
            c = cache_ref[s]
            redo_head(s, c[1:2, :], c[0:1, :])
            ts_ref[s * SUBLANES:(s + 1) * SUBLANES, :] = tails[s]


def _mm_ffnup(h, w, layer, conv_w, conv_b, cache):
    bn = 1024
    k = w.shape[1]
    nb = FFN_DIM // bn
    return pl.pallas_call(
        _mm_ffnup_kernel,
        out_shape=(
            jax.ShapeDtypeStruct((M_TOTAL, FFN_DIM), BF16),
            jax.ShapeDtypeStruct((SUBLANES, FFN_DIM), F32),
            jax.ShapeDtypeStruct((DEC_BATCH * SUBLANES, FFN_DIM), F32),
        ),
        grid=(nb, N_ROW_TILES),
        in_specs=[
            pl.BlockSpec((BM, k), lambda n, m: (m, 0)),
            _weight_spec(k, bn, layer),
            _weight_spec(k, bn, layer, nb),
            pl.BlockSpec((CONV_W, bn), lambda n, m: (0, n)),
            pl.BlockSpec((1, bn), lambda n, m: (0, n)),
            pl.BlockSpec((DEC_BATCH, CONV_W - 1, bn), lambda n, m: (0, 0, n)),
        ],
        out_specs=(
            pl.BlockSpec((BM, bn), lambda n, m: (m, n)),
            pl.BlockSpec((SUBLANES, bn), lambda n, m: (0, n)),
            pl.BlockSpec((DEC_BATCH * SUBLANES, bn), lambda n, m: (0, n)),
        ),
        scratch_shapes=[pltpu.VMEM((k, bn), BF16), pltpu.VMEM((k, bn), BF16),
                        pltpu.VMEM((SUBLANES, bn), F32)],
        compiler_params=_params(2),
        name="mm_ffn_up",
    )(h, w, w, conv_w, conv_b.reshape(1, FFN_DIM), cache)


def _ret_kernel(*refs, T, sub, has_init):
    if has_init:
        (lg_ref, q_ref, k_ref, v_ref, g_ref, gn_ref, s0_ref,
         o_ref, sl_ref, state_ref, mask_ref) = refs
    else:
        (lg_ref, q_ref, k_ref, v_ref, g_ref, gn_ref,
         o_ref, sl_ref, state_ref, mask_ref) = refs
    blk = pl.program_id(2)
    lg = lg_ref[...][:, 0:1]

    @pl.when(blk == 0)
    def _():
        if has_init:
            state_ref[...] = s0_ref[...]
        else:
            state_ref[...] = jnp.zeros_like(state_ref)
        i = lax.broadcasted_iota(jnp.int32, (T, T), 0)
        j = lax.broadcasted_iota(jnp.int32, (T, T), 1)
        dist = jnp.abs(i - j).astype(F32)
        shift = int(math.log2(CHUNK))
        visible = (j >> shift) <= (i >> shift)
        mask_ref[...] = jnp.where(visible, jnp.exp(dist * lg), 0.0)

    t = lax.broadcasted_iota(jnp.int32, (T, 1), 0).astype(F32)
    cross = jnp.exp((t + 1.0) * lg)
    k_dec = jnp.exp((T - 1.0 - t) * lg)
    decay = jnp.exp(float(T) * lg)
    mask = mask_ref[...]
    gn = gn_ref[...]
    state = state_ref[...]
    for sb in range(sub):
        rows = slice(sb * T, (sb + 1) * T)
        q = q_ref[rows, :]
        k = k_ref[rows, :]
        v = v_ref[rows, :]
        scores = lax.dot_general(q, k, (((1,), (1,)), ((), ())), preferred_element_type=F32)
        scores = scores * mask
        out = jnp.dot(scores.astype(BF16), v, preferred_element_type=F32)
        out = out + jnp.dot(q, state.astype(BF16), preferred_element_type=F32) * cross

        kd = (k.astype(F32) * k_dec).astype(BF16)
        state = decay * state + lax.dot_general(
            kd, v, (((0,), (0,)), ((), ())), preferred_element_type=F32)

        mu = jnp.mean(out, axis=-1, keepdims=True)
        oc = out - mu
        var = jnp.mean(oc * oc, axis=-1, keepdims=True)
        normed = (oc * lax.rsqrt(var + EPS)) * gn
        g = g_ref[rows, :].astype(F32)
        o_ref[rows, :] = (normed * (g * _sigmoid(g))).astype(BF16)
    state_ref[...] = state

    @pl.when(blk == pl.num_programs(2) - 1)
    def _():
        sl_ref[...] = state


def _retention(qk, vg, gn, s0, *, row0, n_seq, n_blk, T, sub=1):
    log_g = np.log1p(-np.exp2(-5.0 - np.arange(RET_HEADS, dtype=np.float32))).astype(np.float32)
    lg = jnp.asarray(np.broadcast_to(log_g[:, None, None], (RET_HEADS, 1, LANES)).copy())
    R = sub * T
    rb0 = row0 // R
    k_off = RET_QK // RET_DK
    g_off = RET_VD // RET_DV
    has_init = s0 is not None

    def rows(h, s, b):
        return rb0 + s * n_blk + b

    in_specs = [
        pl.BlockSpec((None, 1, LANES), lambda h, s, b: (h, 0, 0)),
        pl.BlockSpec((R, RET_DK), lambda h, s, b: (rows(h, s, b), h)),
        pl.BlockSpec((R, RET_DK), lambda h, s, b: (rows(h, s, b), k_off + h)),
        pl.BlockSpec((R, RET_DV), lambda h, s, b: (rows(h, s, b), h)),
        pl.BlockSpec((R, RET_DV), lambda h, s, b: (rows(h, s, b), g_off + h)),
        pl.BlockSpec((1, RET_DV), lambda h, s, b: (0, h)),
    ]
    args = [lg, qk, qk, vg, vg, gn.reshape(1, RET_VD)]
    if has_init:
        s0_stack, layer = s0
        in_specs.append(pl.BlockSpec((None, None, None, RET_DK, RET_DV),
                                     lambda h, s, b: (layer, s, h, 0, 0)))
        args.append(s0_stack)
    return pl.pallas_call(
        functools.partial(_ret_kernel, T=T, sub=sub, has_init=has_init),
        out_shape=(
            jax.ShapeDtypeStruct((n_seq * n_blk * R, RET_VD), BF16),
            jax.ShapeDtypeStruct((n_seq, RET_HEADS, RET_DK, RET_DV), F32),
        ),
        grid=(RET_HEADS, n_seq, n_blk),
        in_specs=in_specs,
        out_specs=(
            pl.BlockSpec((R, RET_DV), lambda h, s, b: (s * n_blk + b, h)),
            pl.BlockSpec((None, None, RET_DK, RET_DV), lambda h, s, b: (s, h, 0, 0)),
        ),
        scratch_shapes=[pltpu.VMEM((RET_DK, RET_DV), F32), pltpu.VMEM((T, T), F32)],
        compiler_params=_params(3),
        name="retention_init" if has_init else "retention",
    )(*args)


def _ssm_prep_kernel(ar_ref, ai_ref, ldt_ref, br_ref, bi_ref, cr_ref, ci_ref,
                     bc_ref, cc_ref, lag_ref, a16r_ref, a16i_ref, vr_s, vi_s):
    ar = ar_ref[...]
    ai = ai_ref[...]
    dt = jnp.exp(ldt_ref[...])
    mag = jnp.exp(ar * dt)
    ang = ai * dt
    abr = mag * jnp.cos(ang)
    abi = mag * jnp.sin(ang)
    den = ar * ar + ai * ai
    nr = abr - 1.0
    ni = abi
    cfr = (nr * ar + ni * ai) / den
    cfi = (ni * ar - nr * ai) / den

    def b4(z):
        return z[:, None]

    br = br_ref[...]
    bi = bi_ref[...]
    bbr = b4(cfr) * br - b4(cfi) * bi
    bbi = b4(cfr) * bi + b4(cfi) * br
    cre = cr_ref[...]
    cim = ci_ref[...]

    powers = []
    pr = jnp.ones_like(ar)
    pi = jnp.zeros_like(ar)
    for _ in range(SSM_T + 1):
        powers.append((pr, pi))
        pr, pi = pr * abr - pi * abi, pr * abi + pi * abr
    a16r_ref[...] = powers[SSM_T][0]
    a16i_ref[...] = powers[SSM_T][1]

    for m in range(SSM_T + 1):
        pr, pi = powers[m]
        vr = b4(pr) * cre - b4(pi) * cim
        vi = b4(pr) * cim + b4(pi) * cre
        if m < SSM_T:
            vr_s[:, :, m] = vr
            vi_s[:, :, m] = vi
        if m >= 1:
            cc_ref[:, :, m - 1, :, 0:LANES] = vr.astype(BF16)
            cc_ref[:, :, m - 1, :, LANES:2 * LANES] = (-vi).astype(BF16)

    for t in range(SSM_T):
        pr, pi = powers[SSM_T - 1 - t]
        bc_ref[:, :, t, :, 0:LANES] = (b4(pr) * bbr - b4(pi) * bbi).astype(BF16)
        bc_ref[:, :, t, :, LANES:2 * LANES] = (b4(pr) * bbi + b4(pi) * bbr).astype(BF16)

    width = SSM_T * SSM_GROUP
    lane = lax.broadcasted_iota(jnp.int32, (SSM_GROUP, LANES), 1)
    contract_last = (((1,), (1,)), ((), ()))
    for q in range(SSM_PB):
        for half in range(2):
            gl = 2 * q + half
            vr_all = vr_s[q, half].reshape(width, LANES)
            vi_all = vi_s[q, half].reshape(width, LANES)
            krow = (lax.dot_general(bbr[q, half], vr_all, contract_last,
                                    precision=lax.Precision.HIGHEST, preferred_element_type=F32)
                    - lax.dot_general(bbi[q, half], vi_all, contract_last,
                                      precision=lax.Precision.HIGHEST, preferred_element_type=F32))
            in_block = (lane >= gl * SSM_GROUP) & (lane < (gl + 1) * SSM_GROUP)
            for m in range(SSM_T):
                shift = ((gl - m) * SSM_GROUP) % width
                moved = krow if shift == 0 else pltpu.roll(krow, shift, axis=1)
                lag_ref[m, gl * SSM_GROUP:(gl + 1) * SSM_GROUP, :] = jnp.where(
                    in_block, moved[:, 0:LANES], 0.0).astype(BF16)


def _pair_pack(x):
    x = x.reshape(SSM_PAIRS, 2, SSM_GROUP, SSM_STATE)
    lo = jnp.pad(x[:, 0], ((0, 0), (0, 0), (0, SSM_STATE)))
    hi = jnp.pad(x[:, 1], ((0, 0), (0, 0), (SSM_STATE, 0)))
    return jnp.stack([lo, hi], axis=1)


def _ssm_prep(a_re, a_im, log_dt, b_re, b_im, c_re, c_im):
    assert 2 * SSM_PB * SSM_GROUP == LANES
    ar = a_re.reshape(SSM_PAIRS, 1, LANES)
    ai = a_im.reshape(SSM_PAIRS, 1, LANES)
    ldt = jnp.repeat(log_dt, SSM_STATE).reshape(SSM_PAIRS, 1, LANES)
    b2r = _pair_pack(jnp.transpose(b_re, (0, 2, 1)))
    b2i = _pair_pack(jnp.transpose(b_im, (0, 2, 1)))
    c2r = _pair_pack(c_re)
    c2i = _pair_pack(c_im)
    vec = pl.BlockSpec((SSM_PB, 1, LANES), lambda p: (p, 0, 0))
    mat = pl.BlockSpec((SSM_PB, 2, SSM_GROUP, LANES), lambda p: (p, 0, 0, 0))
    proj = pl.BlockSpec((SSM_PB, 2, SSM_T, SSM_GROUP, 2 * LANES), lambda p: (p, 0, 0, 0, 0))
    bc, cc, lag, a16r, a16i = pl.pallas_call(
        _ssm_prep_kernel,
        out_shape=(
            jax.ShapeDtypeStruct((SSM_PAIRS, 2, SSM_T, SSM_GROUP, 2 * LANES), BF16),
            jax.ShapeDtypeStruct((SSM_PAIRS, 2, SSM_T, SSM_GROUP, 2 * LANES), BF16),
            jax.ShapeDtypeStruct((N_LANE_TILES, SSM_T, LANES, LANES), BF16),
            jax.ShapeDtypeStruct((SSM_PAIRS, 1, LANES), F32),
            jax.ShapeDtypeStruct((SSM_PAIRS, 1, LANES), F32),
        ),
        grid=(N_LANE_TILES,),
        in_specs=[vec, vec, vec, mat, mat, mat, mat],
        out_specs=(proj, proj,
                   pl.BlockSpec((None, SSM_T, LANES, LANES), lambda p: (p, 0, 0, 0)),
                   vec, vec),
        scratch_shapes=[pltpu.VMEM((SSM_PB, 2, SSM_T, SSM_GROUP, LANES), F32)] * 2,
        compiler_params=_params(1),
        name="ssm_prep",
    )(ar, ai, ldt, b2r, b2i, c2r, c2i)
    row = SSM_GROUPS * SSM_STATE
    a16r = jnp.broadcast_to(a16r.reshape(1, row), (SSM_SEGS, row))
    a16i = jnp.broadcast_to(a16i.reshape(1, row), (SSM_SEGS, row))
    return bc, cc, lag, a16r, a16i


SSM_CHUNK_COLS = SSM_T * LANES
SSM_STATE_COLS = SSM_PB * LANES
SSM_CHUNKS_P = M_PROMPT // SSM_T
SSM_COL_BLOCKS = 4
SSM_SCAN_UNROLL = 2


def _ssm_main_kernel(x_ref, lag_ref, bc_ref, cc_ref, ar_ref, ai_ref, h0r_ref, h0i_ref,
                     y_ref, fpr_ref, fpi_ref, fsr_ref, fsi_ref,
                     w_s, bcw_s, ccw_s, d_s, s_s):
    w = SSM_STATE_COLS

    @pl.when(pl.program_id(0) == 0)
    def _():
        w_s[...] = jnp.zeros_like(w_s)
        bcw_s[...] = jnp.zeros_like(bcw_s)
        ccw_s[...] = jnp.zeros_like(ccw_s)

    for t in range(SSM_T):
        for t2 in range(t, SSM_T):
            w_s[t * LANES:(t + 1) * LANES, t2 * LANES:(t2 + 1) * LANES] = lag_ref[t2 - t]
    for q in range(SSM_PB):
        for half in range(2):
            for t in range(SSM_T):
                r0 = t * LANES + (2 * q + half) * SSM_GROUP
                rows = slice(r0, r0 + SSM_GROUP)
                bcw_s[rows, q * LANES:(q + 1) * LANES] = bc_ref[q, half, t, :, 0:LANES]
                bcw_s[rows, w + q * LANES:w + (q + 1) * LANES] = bc_ref[q, half, t, :, LANES:2 * LANES]
                ccw_s[rows, q * LANES:(q + 1) * LANES] = cc_ref[q, half, t, :, 0:LANES]
                ccw_s[rows, w + q * LANES:w + (q + 1) * LANES] = cc_ref[q, half, t, :, LANES:2 * LANES]

    x = x_ref[...]
    for k2 in range(SSM_PB):
        d = jnp.dot(x, bcw_s[:, 2 * k2 * LANES:2 * (k2 + 1) * LANES], preferred_element_type=F32)
        d_s[2 * k2] = d[:, 0:LANES]
        d_s[2 * k2 + 1] = d[:, LANES:2 * LANES]

    tokens_per_block = SSM_T // SSM_COL_BLOCKS
    col_blocks = []
    for cb in range(SSM_COL_BLOCKS):
        k_hi = (cb + 1) * tokens_per_block * LANES
        cols = slice(cb * tokens_per_block * LANES, k_hi)
        col_blocks.append(cols)
        y_ref[:, cols] = jnp.dot(x_ref[:, 0:k_hi], w_s[0:k_hi, cols], preferred_element_type=F32)

    tiles = range(SSM_PB)
    ar = [ar_ref[:, k * LANES:(k + 1) * LANES] for k in tiles]
    ai = [ai_ref[:, k * LANES:(k + 1) * LANES] for k in tiles]

    def advance(sr, si, rows, record):
        out_r, out_i = [], []
        for k in tiles:
            if record:
                s_s[k, rows, :] = sr[k]
                s_s[SSM_PB + k, rows, :] = si[k]
            dr = d_s[k, rows, :]
            di = d_s[SSM_PB + k, rows, :]
            out_r.append(ar[k] * sr[k] - ai[k] * si[k] + dr)
            out_i.append(ar[k] * si[k] + ai[k] * sr[k] + di)
        return out_r, out_i

    def prompt_rows(i):
        return pl.ds(i, SSM_SEGS, stride=SSM_STEPS_P)

    def prompt_pass(start_r, start_i, record):
        def body(i, carry):
            out_r, out_i = advance(list(carry[0]), list(carry[1]), prompt_rows(i), record)
            return tuple(out_r), tuple(out_i)
        out_r, out_i = lax.fori_loop(0, SSM_STEPS_P, body, (tuple(start_r), tuple(start_i)),
                                     unroll=SSM_SCAN_UNROLL)
        return list(out_r), list(out_i)

    zeros = [jnp.zeros((SSM_SEGS, LANES), F32) for _ in tiles]
    er, ei = prompt_pass(zeros, zeros, record=False)

    sr, si = [], []
    for k in tiles:
        pr, pi = ar[k][0:1], ai[k][0:1]
        for _ in range(int(math.log2(SSM_STEPS_P))):
            pr, pi = pr * pr - pi * pi, 2.0 * pr * pi
        cr = jnp.zeros((1, LANES), F32)
        ci = cr
        starts_r, starts_i = [], []
        for s in range(SSM_SEGS):
            starts_r.append(cr)
            starts_i.append(ci)
            cr, ci = (pr * cr - pi * ci + er[k][s:s + 1], pr * ci + pi * cr + ei[k][s:s + 1])
        cols = slice(k * LANES, (k + 1) * LANES)
        fpr_ref[:, cols] = jnp.broadcast_to(cr, (SSM_SEGS, LANES))
        fpi_ref[:, cols] = jnp.broadcast_to(ci, (SSM_SEGS, LANES))
        sr.append(jnp.concatenate(starts_r, axis=0))
        si.append(jnp.concatenate(starts_i, axis=0))

    prompt_pass(sr, si, record=True)
    sr = [h0r_ref[:, k * LANES:(k + 1) * LANES] for k in tiles]
    si = [h0i_ref[:, k * LANES:(k + 1) * LANES] for k in tiles]
    for i in range(SSM_STEPS_S):
        sr, si = advance(sr, si, pl.ds(SSM_CHUNKS_P + i, DEC_BATCH, stride=SSM_STEPS_S), record=True)
    for k in tiles:
        fsr_ref[:, k * LANES:(k + 1) * LANES] = sr[k]
        fsi_ref[:, k * LANES:(k + 1) * LANES] = si[k]

    states = jnp.concatenate([s_s[k].astype(BF16) for k in range(2 * SSM_PB)], axis=1)
    contract_last = (((1,), (1,)), ((), ()))
    for cols in col_blocks:
        y_ref[:, cols] += lax.dot_general(states, ccw_s[cols, :], contract_last,
                                          preferred_element_type=F32)


def _ssm_main(x_c, lag, bc, cc, a16r, a16i, h0r, h0i):
    w = SSM_STATE_COLS
    row = SSM_GROUPS * SSM_STATE
    st = pl.BlockSpec((SSM_SEGS, w), lambda j: (0, j))
    st_shape = jax.ShapeDtypeStruct((SSM_SEGS, row), F32)
    proj = pl.BlockSpec((SSM_PB, 2, SSM_T, SSM_GROUP, 2 * LANES), lambda j: (j, 0, 0, 0, 0))
    chunk_rows = pl.BlockSpec((None, N_CHUNKS, SSM_CHUNK_COLS), lambda j: (j, 0, 0))
    return pl.pallas_call(
        _ssm_main_kernel,
        out_shape=(jax.ShapeDtypeStruct((N_LANE_TILES, N_CHUNKS, SSM_CHUNK_COLS), F32),
                   st_shape, st_shape, st_shape, st_shape),
        grid=(N_LANE_TILES,),
        in_specs=[
            chunk_rows,
            pl.BlockSpec((None, SSM_T, LANES, LANES), lambda j: (j, 0, 0, 0)),
            proj, proj, st, st, st, st,
        ],
        out_specs=(chunk_rows, st, st, st, st),
        scratch_shapes=[
            pltpu.VMEM((SSM_CHUNK_COLS, SSM_CHUNK_COLS), BF16),
            pltpu.VMEM((SSM_CHUNK_COLS, 2 * w), BF16),
            pltpu.VMEM((SSM_CHUNK_COLS, 2 * w), BF16),
            pltpu.VMEM((2 * SSM_PB, N_CHUNKS, LANES), F32),
            pltpu.VMEM((2 * SSM_PB, N_CHUNKS, LANES), F32),
        ],
        compiler_params=_params(1),
        name="ssm_main",
    )(x_c, lag, bc, cc, a16r, a16i, h0r, h0i)


def _ssm_post_kernel(y_ref, x_ref, g_ref, d_ref, o_ref, y_s):
    for t in range(SSM_T):
        for j in range(N_LANE_TILES):
            y_s[j, pl.ds(t, CHUNKS_PER_TILE, stride=SSM_T), :] = y_ref[j, :, t * LANES:(t + 1) * LANES]
    x = x_ref[...]
    ms = jnp.mean(x * x, axis=-1, keepdims=True)
    h = (x * lax.rsqrt(ms + EPS)) * g_ref[...]
    c = math.sqrt(2.0 / math.pi)
    for j in range(N_LANE_TILES):
        cols = slice(j * LANES, (j + 1) * LANES)
        y = y_s[j] + d_ref[:, cols] * h[:, cols]
        gelu = 0.5 * y * (1.0 + jnp.tanh(c * (y + 0.044715 * (y * y * y))))
        o_ref[:, cols] = gelu.astype(BF16)


def _ssm_post(y_c, x, g, d):
    spec = pl.BlockSpec((BM, D_MODEL), lambda m: (m, 0))
    vec = pl.BlockSpec((1, D_MODEL), lambda m: (0, 0))
    return pl.pallas_call(
        _ssm_post_kernel,
        out_shape=jax.ShapeDtypeStruct((M_TOTAL, D_MODEL), BF16),
        grid=(N_ROW_TILES,),
        in_specs=[pl.BlockSpec((N_LANE_TILES, CHUNKS_PER_TILE, SSM_CHUNK_COLS), lambda m: (0, m, 0)),
                  spec, vec, vec],
        out_specs=spec,
        scratch_shapes=[pltpu.VMEM((N_LANE_TILES, BM, LANES), F32)],
        compiler_params=_params(1),
        name="ssm_post",
    )(y_c, x, g.reshape(1, D_MODEL), d.reshape(1, D_MODEL))


def kernel(x_prompt, x_sample, state_ret, state_ssm_re, state_ssm_im, cache_conv, norm_mix, norm_ffn, norm_final, ret_w_in, ret_gn, ret_w_out, ssm_a_re, ssm_a_im, ssm_log_dt, ssm_b_re, ssm_b_im, ssm_c_re, ssm_c_im, ssm_d, ssm_w_glu, ffn_w_up, ffn_conv_w, ffn_conv_b, ffn_w_down):
    x = (x_prompt.reshape(M_PROMPT, D_MODEL), x_sample.reshape(M_SAMPLE, D_MODEL))
    cos, sin = _rope_tables()

    ret_p, ret_s = [], []
    re_p, im_p, re_s, im_s = [], [], [], []
    conv_p, conv_s = [], []
    state_row = SSM_GROUPS * SSM_STATE
    for i in range(DEPTH):
        j = i // 2
        if i % 2 == 0:
            if isinstance(x, tuple):
                h = _rmsnorm_split(x[0], x[1], norm_mix[i], BF16)
            else:
                h = _rmsnorm(x, norm_mix[i], BF16)
            qk, vg = _mm_retin(h, ret_w_in, j, cos, sin)
            o_p, s_p = _retention(qk, vg, ret_gn[j], None, row0=0, n_seq=1,
                                  n_blk=M_PROMPT // (RET_T * RET_SUB), T=RET_T, sub=RET_SUB)
            o_s, s_s = _retention(qk, vg, ret_gn[j], (state_ret, j), row0=M_PROMPT, n_seq=DEC_BATCH,
                                  n_blk=1, T=DEC_SEQ)
            ret_p.append(s_p)
            ret_s.append(s_s)
            x = _mm_res((o_p, o_s), ret_w_out, j, x)
        else:
            h_c = _rmsnorm_chunked(x, norm_mix[i])
            bc, cc, lag, a16r, a16i = _ssm_prep(ssm_a_re[j], ssm_a_im[j], ssm_log_dt[j],
                                               ssm_b_re[j], ssm_b_im[j], ssm_c_re[j], ssm_c_im[j])
            y_c, fpr, fpi, fsr, fsi = _ssm_main(
                h_c, lag, bc, cc, a16r, a16i,
                state_ssm_re[j].reshape(DEC_BATCH, state_row),
                state_ssm_im[j].reshape(DEC_BATCH, state_row))
            re_p.append(fpr[0:1].reshape(1, SSM_GROUPS, SSM_STATE))
            im_p.append(fpi[0:1].reshape(1, SSM_GROUPS, SSM_STATE))
            re_s.append(fsr.reshape(DEC_BATCH, SSM_GROUPS, SSM_STATE))
            im_s.append(fsi.reshape(DEC_BATCH, SSM_GROUPS, SSM_STATE))
            gl = _ssm_post(y_c, x, norm_mix[i], ssm_d[j])
            x = _mm_glu(gl, ssm_w_glu, j, x)

        h = _rmsnorm(x, norm_ffn[i], BF16)
        u, tail_p, tail_s = _mm_ffnup(h, ffn_w_up, i, ffn_conv_w[i], ffn_conv_b[i], cache_conv[i])
        conv_p.append(tail_p[SUBLANES - (CONV_W - 1):].reshape(1, CONV_W - 1, FFN_DIM))
        conv_s.append(tail_s.reshape(DEC_BATCH, SUBLANES, FFN_DIM)[:, SUBLANES - (CONV_W - 1):])
        x = _mm_res(u, ffn_w_down, i, x)

    y_prompt = _rmsnorm(x, norm_final, F32, 0, N_PROMPT_TILES).reshape(1, SEQ, D_MODEL)
    y_sample = _rmsnorm(x, norm_final, F32, N_PROMPT_TILES, 1).reshape(DEC_BATCH, DEC_SEQ, D_MODEL)
    return (y_prompt, y_sample, jnp.stack(ret_p), jnp.stack(ret_s),
            jnp.stack(re_p), jnp.stack(im_p), jnp.stack(re_s), jnp.stack(im_s),
            jnp.stack(conv_p), jnp.stack(conv_s))
```

```python
import functools
import math

import numpy as np
import jax
import jax.numpy as jnp
from jax import lax
from jax.experimental import pallas as pl
from jax.experimental.pallas import tpu as pltpu

F32 = jnp.float32
BF16 = jnp.bfloat16

D_MODEL = 2048
SEQ = 8192
DEPTH = 4
DEC_BATCH = 8
DEC_SEQ = 64
PAST_LEN = 1024
CHUNK = 64
RET_HEADS = 8
RET_DK = D_MODEL // RET_HEADS
RET_DV = 2 * RET_DK
RET_QK = RET_HEADS * RET_DK
RET_VD = RET_HEADS * RET_DV
ROPE_BASE = 10000.0
SSM_GROUP = 16
SSM_GROUPS = D_MODEL // SSM_GROUP
SSM_STATE = 64
FFN_DIM = 2 * D_MODEL
CONV_W = 3
EPS = 1e-6

M_PROMPT = SEQ
M_SAMPLE = DEC_BATCH * DEC_SEQ
M_TOTAL = M_PROMPT + M_SAMPLE

LANES = 128
SUBLANES = 8
VMEM_LIMIT_BYTES = 56 * 1024 * 1024

BM = M_SAMPLE
N_ROW_TILES = M_TOTAL // BM
N_PROMPT_TILES = M_PROMPT // BM

RET_T = 256
RET_SUB = 4

SSM_T = 16
SSM_PAIRS = SSM_GROUPS // 2
SSM_PB = 4
SSM_SEGS = 8
SSM_STEPS_P = M_PROMPT // SSM_T // SSM_SEGS
SSM_STEPS_S = DEC_SEQ // SSM_T
SSM_NC = (SSM_STEPS_P + SSM_STEPS_S) * SSM_SEGS


def _params(n_axes):
    return pltpu.CompilerParams(dimension_semantics=("arbitrary",) * n_axes,
                                vmem_limit_bytes=VMEM_LIMIT_BYTES)


def _sigmoid(x):
    return 1.0 / (1.0 + jnp.exp(-x))


def _rope_kernel(cos_ref, sin_ref):
    m = pl.program_id(0)
    half = RET_DK // 2
    freq = lax.broadcasted_iota(jnp.int32, (1, half), 1).astype(F32)
    inv = ROPE_BASE ** (-freq / half)
    r = m * BM + lax.broadcasted_iota(jnp.int32, (BM, half), 0)
    pos = jnp.where(r < M_PROMPT, r, PAST_LEN + ((r - M_PROMPT) & (DEC_SEQ - 1)))
    ang = pos.astype(F32) * inv
    cos_ref[...] = jnp.cos(ang)
    sin_ref[...] = jnp.sin(ang)


def _rope_tables():
    half = RET_DK // 2
    spec = pl.BlockSpec((BM, half), lambda m: (m, 0))
    return pl.pallas_call(
        _rope_kernel,
        out_shape=(jax.ShapeDtypeStruct((M_TOTAL, half), F32),) * 2,
        grid=(N_ROW_TILES,),
        out_specs=(spec, spec),
        compiler_params=_params(1),
        name="rope_tables",
    )()


def _norm_kernel(x_ref, g_ref, o_ref):
    x = x_ref[...]
    ms = jnp.mean(x * x, axis=-1, keepdims=True)
    o_ref[...] = ((x * lax.rsqrt(ms + EPS)) * g_ref[...]).astype(o_ref.dtype)


def _rmsnorm(x, g, out_dtype, tile0=0, n_tiles=N_ROW_TILES):
    return pl.pallas_call(
        _norm_kernel,
        out_shape=jax.ShapeDtypeStruct((n_tiles * BM, D_MODEL), out_dtype),
        grid=(n_tiles,),
        in_specs=[pl.BlockSpec((BM, D_MODEL), lambda m: (tile0 + m, 0)),
                  pl.BlockSpec((1, D_MODEL), lambda m: (0, 0))],
        out_specs=pl.BlockSpec((BM, D_MODEL), lambda m: (m, 0)),
        compiler_params=_params(1),
        name="rmsnorm",
    )(x, g.reshape(1, D_MODEL))


def _norm_split_kernel(xp_ref, xs_ref, g_ref, o_ref):
    x = jnp.where(pl.program_id(0) < N_PROMPT_TILES, xp_ref[...], xs_ref[...])
    ms = jnp.mean(x * x, axis=-1, keepdims=True)
    o_ref[...] = ((x * lax.rsqrt(ms + EPS)) * g_ref[...]).astype(o_ref.dtype)


def _rmsnorm_split(x_prompt, x_sample, g, out_dtype):
    return pl.pallas_call(
        _norm_split_kernel,
        out_shape=jax.ShapeDtypeStruct((M_TOTAL, D_MODEL), out_dtype),
        grid=(N_ROW_TILES,),
        in_specs=[pl.BlockSpec((BM, D_MODEL), lambda m: (jnp.minimum(m, N_PROMPT_TILES - 1), 0)),
                  pl.BlockSpec((BM, D_MODEL), lambda m: (0, 0)),
                  pl.BlockSpec((1, D_MODEL), lambda m: (0, 0))],
        out_specs=pl.BlockSpec((BM, D_MODEL), lambda m: (m, 0)),
        compiler_params=_params(1),
        name="rmsnorm_split",
    )(x_prompt, x_sample, g.reshape(1, D_MODEL))


CHUNKS_PER_TILE = BM // 16
N_CHUNKS = M_TOTAL // 16
N_LANE_TILES = D_MODEL // LANES


def _norm_chunked_kernel(x_ref, g_ref, o_ref, h_s):
    x = x_ref[...]
    ms = jnp.mean(x * x, axis=-1, keepdims=True)
    h = (x * lax.rsqrt(ms + EPS)) * g_ref[...]
    for j in range(N_LANE_TILES):
        h_s[j] = h[:, j * LANES:(j + 1) * LANES]
    for t in range(SSM_T):
        for j in range(N_LANE_TILES):
            rows = h_s[j, pl.ds(t, CHUNKS_PER_TILE, stride=SSM_T), :]
            o_ref[j, :, t * LANES:(t + 1) * LANES] = rows.astype(BF16)


def _rmsnorm_chunked(x, g):
    return pl.pallas_call(
        _norm_chunked_kernel,
        out_shape=jax.ShapeDtypeStruct((N_LANE_TILES, N_CHUNKS, SSM_T * LANES), BF16),
        grid=(N_ROW_TILES,),
        in_specs=[pl.BlockSpec((BM, D_MODEL), lambda m: (m, 0)),
                  pl.BlockSpec((1, D_MODEL), lambda m: (0, 0))],
        out_specs=pl.BlockSpec((N_LANE_TILES, CHUNKS_PER_TILE, SSM_T * LANES), lambda m: (0, m, 0)),
        scratch_shapes=[pltpu.VMEM((N_LANE_TILES, BM, LANES), F32)],
        compiler_params=_params(1),
        name="rmsnorm_chunked",
    )(x, g.reshape(1, D_MODEL))


def _weight_spec(k, bn, layer, col0=0):
    return pl.BlockSpec((None, k, bn), lambda n, m: (layer, 0, col0 + n), pipeline_mode=pl.Buffered(1))


RET_IN_BN = 2048


def _emit_norm_inputs(x_new, xb_ref, ssq_ref):
    xb_ref[...] = x_new.astype(BF16)
    sq = x_new * x_new
    part = sq[:, 0:LANES]
    for t in range(1, x_new.shape[1] // LANES):
        part = part + sq[:, t * LANES:(t + 1) * LANES]
    ssq_ref[...] = part


def _row_scale(ssq_ref):
    total = jnp.sum(jnp.sum(ssq_ref[...], axis=0), axis=-1, keepdims=True)
    return lax.rsqrt(total * (1.0 / D_MODEL) + EPS)


def _norm_operands(norm):
    xb, ssq, g = norm
    args = [xb, ssq, g.reshape(D_MODEL, 1)]
    specs = [pl.BlockSpec((BM, D_MODEL), lambda n, m: (m, 0)),
             pl.BlockSpec((ssq.shape[0], BM, LANES), lambda n, m: (0, m, 0)),
             pl.BlockSpec((D_MODEL, 1), lambda n, m: (0, 0))]
    return args, specs


def _norm_outputs(n_out, bn):
    shapes = [jax.ShapeDtypeStruct((M_TOTAL, n_out), BF16),
              jax.ShapeDtypeStruct((n_out // bn, M_TOTAL, LANES), F32)]
    specs = [pl.BlockSpec((BM, bn), lambda n, m: (m, n)),
             pl.BlockSpec((None, BM, LANES), lambda n, m: (n, m, 0))]
    return shapes, specs


def _cast_ssq_kernel(xp_ref, xs_ref, xb_ref, ssq_ref):
    x = jnp.where(pl.program_id(0) < N_PROMPT_TILES, xp_ref[...], xs_ref[...])
    _emit_norm_inputs(x, xb_ref, ssq_ref)


def _cast_ssq(x_prompt, x_sample):
    return pl.pallas_call(
        _cast_ssq_kernel,
        out_shape=(jax.ShapeDtypeStruct((M_TOTAL, D_MODEL), BF16),
                   jax.ShapeDtypeStruct((1, M_TOTAL, LANES), F32)),
        grid=(N_ROW_TILES,),
        in_specs=[pl.BlockSpec((BM, D_MODEL), lambda m: (jnp.minimum(m, N_PROMPT_TILES - 1), 0)),
                  pl.BlockSpec((BM, D_MODEL), lambda m: (0, 0))],
        out_specs=(pl.BlockSpec((BM, D_MODEL), lambda m: (m, 0)),
                   pl.BlockSpec((None, BM, LANES), lambda m: (0, m, 0))),
        compiler_params=_params(1),
        name="cast_ssq",
    )(x_prompt, x_sample)


def _mm_qk_kernel(a_ref, ssq_ref, g_ref, w_ref, cos_ref, sin_ref, o_ref, wb_ref):
    n = pl.program_id(0)
    n_q = RET_QK // RET_IN_BN

    @pl.when(pl.program_id(1) == 0)
    def _():
        wb_ref[...] = (w_ref[...] * g_ref[...]).astype(BF16)

    acc = jnp.dot(a_ref[...], wb_ref[...], preferred_element_type=F32) * _row_scale(ssq_ref)
    scale = jnp.where(n >= n_q, RET_DK ** -0.5, 1.0).astype(F32)
    c = cos_ref[...]
    s = sin_ref[...]
    half = RET_DK // 2
    for hh in range(RET_IN_BN // RET_DK):
        lo = hh * RET_DK
        x1 = acc[:, lo:lo + half]
        x2 = acc[:, lo + half:lo + RET_DK]
        o_ref[:, lo:lo + half] = ((x1 * c - x2 * s) * scale).astype(BF16)
        o_ref[:, lo + half:lo + RET_DK] = ((x1 * s + x2 * c) * scale).astype(BF16)


def _mm_cast_kernel(a_ref, ssq_ref, g_ref, w_ref, o_ref, wb_ref):
    @pl.when(pl.program_id(1) == 0)
    def _():
        wb_ref[...] = (w_ref[...] * g_ref[...]).astype(BF16)

    acc = jnp.dot(a_ref[...], wb_ref[...], preferred_element_type=F32) * _row_scale(ssq_ref)
    o_ref[...] = acc.astype(BF16)


def _mm_retin(norm, w, layer, cos, sin):
    bn = RET_IN_BN
    _, k, n_out = w.shape
    half = RET_DK // 2
    n_qk = 2 * RET_QK // bn
    norm_args, norm_specs = _norm_operands(norm)
    out_spec = pl.BlockSpec((BM, bn), lambda n, m: (m, n))
    rope_spec = pl.BlockSpec((BM, half), lambda n, m: (m, 0))
    qk = pl.pallas_call(
        _mm_qk_kernel,
        out_shape=jax.ShapeDtypeStruct((M_TOTAL, 2 * RET_QK), BF16),
        grid=(n_qk, N_ROW_TILES),
        in_specs=norm_specs + [_weight_spec(k, bn, layer), rope_spec, rope_spec],
        out_specs=out_spec,
        scratch_shapes=[pltpu.VMEM((k, bn), BF16)],
        compiler_params=_params(2),
        name="mm_ret_qk",
    )(*norm_args, w, cos, sin)
    vg = pl.pallas_call(
        _mm_cast_kernel,
        out_shape=jax.ShapeDtypeStruct((M_TOTAL, 2 * RET_VD), BF16),
        grid=(n_out // bn - n_qk, N_ROW_TILES),
        in_specs=norm_specs + [_weight_spec(k, bn, layer, n_qk)],
        out_specs=out_spec,
        scratch_shapes=[pltpu.VMEM((k, bn), BF16)],
        compiler_params=_params(2),
        name="mm_ret_vg",
    )(*norm_args, w)
    return qk, vg


def _row_split_specs(cols, col_index):
    return [pl.BlockSpec((BM, cols), lambda n, m: (jnp.minimum(m, N_PROMPT_TILES - 1), col_index(n))),
            pl.BlockSpec((BM, cols), lambda n, m: (0, col_index(n)))]


def _mm_res_kernel(*refs, split_a, split_x, emit):
    refs = list(refs)
    ap_ref = refs.pop(0)
    as_ref = refs.pop(0) if split_a else None
    w_ref = refs.pop(0)
    xp_ref = refs.pop(0)
    xs_ref = refs.pop(0) if split_x else None
    o_ref = refs.pop(0)
    xb_ref, ssq_ref = (refs.pop(0), refs.pop(0)) if emit else (None, None)
    (wb_ref,) = refs
    m = pl.program_id(1)
    on_prompt = m < N_PROMPT_TILES

    @pl.when(m == 0)
    def _():
        wb_ref[...] = w_ref[...].astype(BF16)

    def body(a_ref):
        x = xp_ref[...]
        if split_x:
            x = jnp.where(on_prompt, x, xs_ref[...])
        x_new = x + jnp.dot(a_ref[...], wb_ref[...], preferred_element_type=F32)
        o_ref[...] = x_new
        if emit:
            _emit_norm_inputs(x_new, xb_ref, ssq_ref)

    if split_a:
        pl.when(on_prompt)(lambda: body(ap_ref))
        pl.when(jnp.logical_not(on_prompt))(lambda: body(as_ref))
    else:
        body(ap_ref)


def _mm_res(a, w, layer, x, emit=False):
    bn = 1024
    _, k, n_out = w.shape
    split_a = isinstance(a, tuple)
    split_x = isinstance(x, tuple)
    a_specs = _row_split_specs(k, lambda n: 0) if split_a else [pl.BlockSpec((BM, k), lambda n, m: (m, 0))]
    x_specs = _row_split_specs(bn, lambda n: n) if split_x else [pl.BlockSpec((BM, bn), lambda n, m: (m, n))]
    a_args = list(a) if split_a else [a]
    x_args = list(x) if split_x else [x]
    out_shapes = [jax.ShapeDtypeStruct((M_TOTAL, n_out), F32)]
    out_specs = [pl.BlockSpec((BM, bn), lambda n, m: (m, n))]
    if emit:
        shapes, specs = _norm_outputs(n_out, bn)
        out_shapes += shapes
        out_specs += specs
    out = pl.pallas_call(
        functools.partial(_mm_res_kernel, split_a=split_a, split_x=split_x, emit=emit),
        out_shape=tuple(out_shapes),
        grid=(n_out // bn, N_ROW_TILES),
        in_specs=a_specs + [_weight_spec(k, bn, layer)] + x_specs,
        out_specs=tuple(out_specs),
        scratch_shapes=[pltpu.VMEM((k, bn), BF16)],
        compiler_params=_params(2),
        name="mm_residual",
    )(*a_args, w, *x_args)
    return out if emit else out[0]


def _mm_glu_kernel(a_ref, wa_ref, wg_ref, x_ref, o_ref, xb_ref, ssq_ref, wab_ref, wgb_ref):
    @pl.when(pl.program_id(1) == 0)
    def _():
        wab_ref[...] = wa_ref[...].astype(BF16)
        wgb_ref[...] = wg_ref[...].astype(BF16)

    a = a_ref[...]
    ga = jnp.dot(a, wab_ref[...], preferred_element_type=F32)
    gb = jnp.dot(a, wgb_ref[...], preferred_element_type=F32)
    x_new = x_ref[...] + ga * _sigmoid(gb)
    o_ref[...] = x_new
    _emit_norm_inputs(x_new, xb_ref, ssq_ref)


def _mm_glu(a, w, layer, x):
    bn = 1024
    _, k, n2 = w.shape
    n_out = n2 // 2
    nb = n_out // bn
    norm_shapes, norm_specs = _norm_outputs(n_out, bn)
    return pl.pallas_call(
        _mm_glu_kernel,
        out_shape=(jax.ShapeDtypeStruct((M_TOTAL, n_out), F32), *norm_shapes),
        grid=(nb, N_ROW_TILES),
        in_specs=[
            pl.BlockSpec((BM, k), lambda n, m: (m, 0)),
            _weight_spec(k, bn, layer),
            _weight_spec(k, bn, layer, nb),
            pl.BlockSpec((BM, bn), lambda n, m: (m, n)),
        ],
        out_specs=(pl.BlockSpec((BM, bn), lambda n, m: (m, n)), *norm_specs),
        scratch_shapes=[pltpu.VMEM((k, bn), BF16), pltpu.VMEM((k, bn), BF16)],
        compiler_params=_params(2),
        name="mm_glu",
    )(a, w, w, x)


def _mm_ffnup_kernel(h_ref, ssq_ref, g_ref, wa_ref, wg_ref, cw_ref, cb_ref, cache_ref,
                     o_ref, tp_ref, ts_ref, wab_ref, wgb_ref, carry_ref):
    m = pl.program_id(1)

    @pl.when(m == 0)
    def _():
        g = g_ref[...]
        wab_ref[...] = (wa_ref[...] * g).astype(BF16)
        wgb_ref[...] = (wg_ref[...] * g).astype(BF16)
        carry_ref[...] = jnp.zeros_like(carry_ref)

    h = h_ref[...]
    r = _row_scale(ssq_ref)
    a = jnp.dot(h, wab_ref[...], preferred_element_type=F32) * r
    b = jnp.dot(h, wgb_ref[...], preferred_element_type=F32) * r
    r1 = pltpu.roll(a, 1, axis=0)
    r2 = pltpu.roll(a, 2, axis=0)
    cb = cb_ref[...]
    w0 = cw_ref[0:1, :]
    w1 = cw_ref[1:2, :]
    w2 = cw_ref[2:3, :]

    def gated(a_rows, prev1, prev2, b_rows):
        conv = cb + w0 * prev2 + w1 * prev1 + w2 * a_rows
        return ((conv * _sigmoid(conv)) * b_rows).astype(BF16)

    o_ref[...] = gated(a, r1, r2, b)

    row8 = lax.broadcasted_iota(jnp.int32, (SUBLANES, 1), 0)
    heads = [tuple(z[s * DEC_SEQ:s * DEC_SEQ + SUBLANES] for z in (a, r1, r2, b))
             for s in range(DEC_BATCH)]
    tails = [a[(s + 1) * DEC_SEQ - SUBLANES:(s + 1) * DEC_SEQ] for s in range(DEC_BATCH)]

    def redo_head(s, before1, before2):
        a8, r1_8, r2_8, b8 = heads[s]
        prev1 = jnp.where(row8 == 0, before1, r1_8)
        prev2 = jnp.where(row8 == 0, before2, jnp.where(row8 == 1, before1, r2_8))
        o_ref[s * DEC_SEQ:s * DEC_SEQ + SUBLANES, :] = gated(a8, prev1, prev2, b8)

    @pl.when(m < N_PROMPT_TILES)
    def _():
        c = carry_ref[...]
        redo_head(0, c[SUBLANES - 1:SUBLANES, :], c[SUBLANES - 2:SUBLANES - 1, :])
        carry_ref[...] = tails[DEC_BATCH - 1]

        @pl.when(m == N_PROMPT_TILES - 1)
        def _():
            tp_ref[...] = tails[DEC_BATCH - 1]

    @pl.when(m == N_PROMPT_TILES)
    def _():
        for s in range(DEC_BATCH):
            c = cache_ref[s]
            redo_head(s, c[1:2, :], c[0:1, :])
            ts_ref[s * SUBLANES:(s + 1) * SUBLANES, :] = tails[s]


def _mm_ffnup(norm, w, layer, conv_w, conv_b, cache):
    bn = 1024
    k = w.shape[1]
    nb = FFN_DIM // bn
    norm_args, norm_specs = _norm_operands(norm)
    return pl.pallas_call(
        _mm_ffnup_kernel,
        out_shape=(
            jax.ShapeDtypeStruct((M_TOTAL, FFN_DIM), BF16),
            jax.ShapeDtypeStruct((SUBLANES, FFN_DIM), F32),
            jax.ShapeDtypeStruct((DEC_BATCH * SUBLANES, FFN_DIM), F32),
        ),
        grid=(nb, N_ROW_TILES),
        in_specs=norm_specs + [
            _weight_spec(k, bn, layer),
            _weight_spec(k, bn, layer, nb),
            pl.BlockSpec((CONV_W, bn), lambda n, m: (0, n)),
            pl.BlockSpec((1, bn), lambda n, m: (0, n)),
            pl.BlockSpec((DEC_BATCH, CONV_W - 1, bn), lambda n, m: (0, 0, n)),
        ],
        out_specs=(
            pl.BlockSpec((BM, bn), lambda n, m: (m, n)),
            pl.BlockSpec((SUBLANES, bn), lambda n, m: (0, n)),
            pl.BlockSpec((DEC_BATCH * SUBLANES, bn), lambda n, m: (0, n)),
        ),
        scratch_shapes=[pltpu.VMEM((k, bn), BF16), pltpu.VMEM((k, bn), BF16),
                        pltpu.VMEM((SUBLANES, bn), F32)],
        compiler_params=_params(2),
        name="mm_ffn_up",
    )(*norm_args, w, w, conv_w, conv_b.reshape(1, FFN_DIM), cache)


def _ret_kernel(*refs, T, sub, has_init):
    if has_init:
        (lg_ref, q_ref, k_ref, v_ref, g_ref, gn_ref, s0_ref,
         o_ref, sl_ref, state_ref, mask_ref) = refs
    else:
        (lg_ref, q_ref, k_ref, v_ref, g_ref, gn_ref,
         o_ref, sl_ref, state_ref, mask_ref) = refs
    blk = pl.program_id(2)
    lg = lg_ref[...][:, 0:1]

    @pl.when(blk == 0)
    def _():
        if has_init:
            state_ref[...] = s0_ref[...]
        else:
            state_ref[...] = jnp.zeros_like(state_ref)
        i = lax.broadcasted_iota(jnp.int32, (T, T), 0)
        j = lax.broadcasted_iota(jnp.int32, (T, T), 1)
        dist = jnp.abs(i - j).astype(F32)
        shift = int(math.log2(CHUNK))
        visible = (j >> shift) <= (i >> shift)
        mask_ref[...] = jnp.where(visible, jnp.exp(dist * lg), 0.0)

    t = lax.broadcasted_iota(jnp.int32, (T, 1), 0).astype(F32)
    cross = jnp.exp((t + 1.0) * lg)
    k_dec = jnp.exp((T - 1.0 - t) * lg)
    decay = jnp.exp(float(T) * lg)
    mask = mask_ref[...]
    gn = gn_ref[...]
    state = state_ref[...]
    for sb in range(sub):
        rows = slice(sb * T, (sb + 1) * T)
        q = q_ref[rows, :]
        k = k_ref[rows, :]
        v = v_ref[rows, :]
        scores = lax.dot_general(q, k, (((1,), (1,)), ((), ())), preferred_element_type=F32)
        scores = scores * mask
        out = jnp.dot(scores.astype(BF16), v, preferred_element_type=F32)
        out = out + jnp.dot(q, state.astype(BF16), preferred_element_type=F32) * cross

        kd = (k.astype(F32) * k_dec).astype(BF16)
        state = decay * state + lax.dot_general(
            kd, v, (((0,), (0,)), ((), ())), preferred_element_type=F32)

        mu = jnp.mean(out, axis=-1, keepdims=True)
        oc = out - mu
        var = jnp.mean(oc * oc, axis=-1, keepdims=True)
        normed = (oc * lax.rsqrt(var + EPS)) * gn
        g = g_ref[rows, :].astype(F32)
        o_ref[rows, :] = (normed * (g * _sigmoid(g))).astype(BF16)
    state_ref[...] = state

    @pl.when(blk == pl.num_programs(2) - 1)
    def _():
        sl_ref[...] = state


def _retention(qk, vg, gn, s0, *, row0, n_seq, n_blk, T, sub=1):
    log_g = np.log1p(-np.exp2(-5.0 - np.arange(RET_HEADS, dtype=np.float32))).astype(np.float32)
    lg = jnp.asarray(np.broadcast_to(log_g[:, None, None], (RET_HEADS, 1, LANES)).copy())
    R = sub * T
    rb0 = row0 // R
    k_off = RET_QK // RET_DK
    g_off = RET_VD // RET_DV
    has_init = s0 is not None

    def rows(h, s, b):
        return rb0 + s * n_blk + b

    in_specs = [
        pl.BlockSpec((None, 1, LANES), lambda h, s, b: (h, 0, 0)),
        pl.BlockSpec((R, RET_DK), lambda h, s, b: (rows(h, s, b), h)),
        pl.BlockSpec((R, RET_DK), lambda h, s, b: (rows(h, s, b), k_off + h)),
        pl.BlockSpec((R, RET_DV), lambda h, s, b: (rows(h, s, b), h)),
        pl.BlockSpec((R, RET_DV), lambda h, s, b: (rows(h, s, b), g_off + h)),
        pl.BlockSpec((1, RET_DV), lambda h, s, b: (0, h)),
    ]
    args = [lg, qk, qk, vg, vg, gn.reshape(1, RET_VD)]
    if has_init:
        s0_stack, layer = s0
        in_specs.append(pl.BlockSpec((None, None, None, RET_DK, RET_DV),
                                     lambda h, s, b: (layer, s, h, 0, 0)))
        args.append(s0_stack)
    return pl.pallas_call(
        functools.partial(_ret_kernel, T=T, sub=sub, has_init=has_init),
        out_shape=(
            jax.ShapeDtypeStruct((n_seq * n_blk * R, RET_VD), BF16),
            jax.ShapeDtypeStruct((n_seq, RET_HEADS, RET_DK, RET_DV), F32),
        ),
        grid=(RET_HEADS, n_seq, n_blk),
        in_specs=in_specs,
        out_specs=(
            pl.BlockSpec((R, RET_DV), lambda h, s, b: (s * n_blk + b, h)),
            pl.BlockSpec((None, None, RET_DK, RET_DV), lambda h, s, b: (s, h, 0, 0)),
        ),
        scratch_shapes=[pltpu.VMEM((RET_DK, RET_DV), F32), pltpu.VMEM((T, T), F32)],
        compiler_params=_params(3),
        name="retention_init" if has_init else "retention",
    )(*args)


def _ssm_prep_kernel(ar_ref, ai_ref, ldt_ref, br_ref, bi_ref, cr_ref, ci_ref,
                     bc_ref, cc_ref, lag_ref, a16r_ref, a16i_ref, vr_s, vi_s):
    ar = ar_ref[...]
    ai = ai_ref[...]
    dt = jnp.exp(ldt_ref[...])
    mag = jnp.exp(ar * dt)
    ang = ai * dt
    abr = mag * jnp.cos(ang)
    abi = mag * jnp.sin(ang)
    den = ar * ar + ai * ai
    nr = abr - 1.0
    ni = abi
    cfr = (nr * ar + ni * ai) / den
    cfi = (ni * ar - nr * ai) / den

    def b4(z):
        return z[:, None]

    br = br_ref[...]
    bi = bi_ref[...]
    bbr = b4(cfr) * br - b4(cfi) * bi
    bbi = b4(cfr) * bi + b4(cfi) * br
    cre = cr_ref[...]
    cim = ci_ref[...]

    powers = []
    pr = jnp.ones_like(ar)
    pi = jnp.zeros_like(ar)
    for _ in range(SSM_T + 1):
        powers.append((pr, pi))
        pr, pi = pr * abr - pi * abi, pr * abi + pi * abr
    a16r_ref[...] = powers[SSM_T][0]
    a16i_ref[...] = powers[SSM_T][1]

    for m in range(SSM_T + 1):
        pr, pi = powers[m]
        vr = b4(pr) * cre - b4(pi) * cim
        vi = b4(pr) * cim + b4(pi) * cre
        if m < SSM_T:
            vr_s[:, :, m] = vr
            vi_s[:, :, m] = vi
        if m >= 1:
            cc_ref[:, :, m - 1, :, 0:LANES] = vr.astype(BF16)
            cc_ref[:, :, m - 1, :, LANES:2 * LANES] = (-vi).astype(BF16)

    for t in range(SSM_T):
        pr, pi = powers[SSM_T - 1 - t]
        bc_ref[:, :, t, :, 0:LANES] = (b4(pr) * bbr - b4(pi) * bbi).astype(BF16)
        bc_ref[:, :, t, :, LANES:2 * LANES] = (b4(pr) * bbi + b4(pi) * bbr).astype(BF16)

    width = SSM_T * SSM_GROUP
    lane = lax.broadcasted_iota(jnp.int32, (SSM_GROUP, LANES), 1)
    contract_last = (((1,), (1,)), ((), ()))
    for q in range(SSM_PB):
        for half in range(2):
            gl = 2 * q + half
            vr_all = vr_s[q, half].reshape(width, LANES)
            vi_all = vi_s[q, half].reshape(width, LANES)
            krow = (lax.dot_general(bbr[q, half], vr_all, contract_last,
                                    precision=lax.Precision.HIGHEST, preferred_element_type=F32)
                    - lax.dot_general(bbi[q, half], vi_all, contract_last,
                                      precision=lax.Precision.HIGHEST, preferred_element_type=F32))
            in_block = (lane >= gl * SSM_GROUP) & (lane < (gl + 1) * SSM_GROUP)
            for m in range(SSM_T):
                shift = ((gl - m) * SSM_GROUP) % width
                moved = krow if shift == 0 else pltpu.roll(krow, shift, axis=1)
                lag_ref[m, gl * SSM_GROUP:(gl + 1) * SSM_GROUP, :] = jnp.where(
                    in_block, moved[:, 0:LANES], 0.0).astype(BF16)


def _pair_pack(x):
    x = x.reshape(SSM_PAIRS, 2, SSM_GROUP, SSM_STATE)
    lo = jnp.pad(x[:, 0], ((0, 0), (0, 0), (0, SSM_STATE)))
    hi = jnp.pad(x[:, 1], ((0, 0), (0, 0), (SSM_STATE, 0)))
    return jnp.stack([lo, hi], axis=1)


def _ssm_prep(a_re, a_im, log_dt, b_re, b_im, c_re, c_im):
    assert 2 * SSM_PB * SSM_GROUP == LANES
    ar = a_re.reshape(SSM_PAIRS, 1, LANES)
    ai = a_im.reshape(SSM_PAIRS, 1, LANES)
    ldt = jnp.repeat(log_dt, SSM_STATE).reshape(SSM_PAIRS, 1, LANES)
    b2r = _pair_pack(jnp.transpose(b_re, (0, 2, 1)))
    b2i = _pair_pack(jnp.transpose(b_im, (0, 2, 1)))
    c2r = _pair_pack(c_re)
    c2i = _pair_pack(c_im)
    vec = pl.BlockSpec((SSM_PB, 1, LANES), lambda p: (p, 0, 0))
    mat = pl.BlockSpec((SSM_PB, 2, SSM_GROUP, LANES), lambda p: (p, 0, 0, 0))
    proj = pl.BlockSpec((SSM_PB, 2, SSM_T, SSM_GROUP, 2 * LANES), lambda p: (p, 0, 0, 0, 0))
    bc, cc, lag, a16r, a16i = pl.pallas_call(
        _ssm_prep_kernel,
        out_shape=(
            jax.ShapeDtypeStruct((SSM_PAIRS, 2, SSM_T, SSM_GROUP, 2 * LANES), BF16),
            jax.ShapeDtypeStruct((SSM_PAIRS, 2, SSM_T, SSM_GROUP, 2 * LANES), BF16),
            jax.ShapeDtypeStruct((N_LANE_TILES, SSM_T, LANES, LANES), BF16),
            jax.ShapeDtypeStruct((SSM_PAIRS, 1, LANES), F32),
            jax.ShapeDtypeStruct((SSM_PAIRS, 1, LANES), F32),
        ),
        grid=(N_LANE_TILES,),
        in_specs=[vec, vec, vec, mat, mat, mat, mat],
        out_specs=(proj, proj,
                   pl.BlockSpec((None, SSM_T, LANES, LANES), lambda p: (p, 0, 0, 0)),
                   vec, vec),
        scratch_shapes=[pltpu.VMEM((SSM_PB, 2, SSM_T, SSM_GROUP, LANES), F32)] * 2,
        compiler_params=_params(1),
        name="ssm_prep",
    )(ar, ai, ldt, b2r, b2i, c2r, c2i)
    row = SSM_GROUPS * SSM_STATE
    a16r = jnp.broadcast_to(a16r.reshape(1, row), (SSM_SEGS, row))
    a16i = jnp.broadcast_to(a16i.reshape(1, row), (SSM_SEGS, row))
    return bc, cc, lag, a16r, a16i


SSM_CHUNK_COLS = SSM_T * LANES
SSM_STATE_COLS = SSM_PB * LANES
SSM_CHUNKS_P = M_PROMPT // SSM_T
SSM_COL_BLOCKS = 4
SSM_SCAN_UNROLL = 2


def _ssm_main_kernel(x_ref, lag_ref, bc_ref, cc_ref, ar_ref, ai_ref, h0r_ref, h0i_ref,
                     y_ref, fpr_ref, fpi_ref, fsr_ref, fsi_ref,
                     w_s, bcw_s, ccw_s, d_s, s_s):
    w = SSM_STATE_COLS

    @pl.when(pl.program_id(0) == 0)
    def _():
        w_s[...] = jnp.zeros_like(w_s)
        bcw_s[...] = jnp.zeros_like(bcw_s)
        ccw_s[...] = jnp.zeros_like(ccw_s)

    for t in range(SSM_T):
        for t2 in range(t, SSM_T):
            w_s[t * LANES:(t + 1) * LANES, t2 * LANES:(t2 + 1) * LANES] = lag_ref[t2 - t]
    for q in range(SSM_PB):
        for half in range(2):
            for t in range(SSM_T):
                r0 = t * LANES + (2 * q + half) * SSM_GROUP
                rows = slice(r0, r0 + SSM_GROUP)
                bcw_s[rows, q * LANES:(q + 1) * LANES] = bc_ref[q, half, t, :, 0:LANES]
                bcw_s[rows, w + q * LANES:w + (q + 1) * LANES] = bc_ref[q, half, t, :, LANES:2 * LANES]
                ccw_s[rows, q * LANES:(q + 1) * LANES] = cc_ref[q, half, t, :, 0:LANES]
                ccw_s[rows, w + q * LANES:w + (q + 1) * LANES] = cc_ref[q, half, t, :, LANES:2 * LANES]

    x = x_ref[...]
    for k2 in range(SSM_PB):
        d = jnp.dot(x, bcw_s[:, 2 * k2 * LANES:2 * (k2 + 1) * LANES], preferred_element_type=F32)
        d_s[2 * k2] = d[:, 0:LANES]
        d_s[2 * k2 + 1] = d[:, LANES:2 * LANES]

    tokens_per_block = SSM_T // SSM_COL_BLOCKS
    col_blocks = []
    for cb in range(SSM_COL_BLOCKS):
        k_hi = (cb + 1) * tokens_per_block * LANES
        cols = slice(cb * tokens_per_block * LANES, k_hi)
        col_blocks.append(cols)
        y_ref[:, cols] = jnp.dot(x_ref[:, 0:k_hi], w_s[0:k_hi, cols], preferred_element_type=F32)

    tiles = range(SSM_PB)
    ar = [ar_ref[:, k * LANES:(k + 1) * LANES] for k in tiles]
    ai = [ai_ref[:, k * LANES:(k + 1) * LANES] for k in tiles]

    def advance(sr, si, rows, record):
        out_r, out_i = [], []
        for k in tiles:
            if record:
                s_s[k, rows, :] = sr[k]
                s_s[SSM_PB + k, rows, :] = si[k]
            dr = d_s[k, rows, :]
            di = d_s[SSM_PB + k, rows, :]
            out_r.append(ar[k] * sr[k] - ai[k] * si[k] + dr)
            out_i.append(ar[k] * si[k] + ai[k] * sr[k] + di)
        return out_r, out_i

    def prompt_rows(i):
        return pl.ds(i, SSM_SEGS, stride=SSM_STEPS_P)

    def prompt_pass(start_r, start_i, record):
        def body(i, carry):
            out_r, out_i = advance(list(carry[0]), list(carry[1]), prompt_rows(i), record)
            return tuple(out_r), tuple(out_i)
        out_r, out_i = lax.fori_loop(0, SSM_STEPS_P, body, (tuple(start_r), tuple(start_i)),
                                     unroll=SSM_SCAN_UNROLL)
        return list(out_r), list(out_i)

    zeros = [jnp.zeros((SSM_SEGS, LANES), F32) for _ in tiles]
    er, ei = prompt_pass(zeros, zeros, record=False)

    sr, si = [], []
    for k in tiles:
        pr, pi = ar[k][0:1], ai[k][0:1]
        for _ in range(int(math.log2(SSM_STEPS_P))):
            pr, pi = pr * pr - pi * pi, 2.0 * pr * pi
        cr = jnp.zeros((1, LANES), F32)
        ci = cr
        starts_r, starts_i = [], []
        for s in range(SSM_SEGS):
            starts_r.append(cr)
            starts_i.append(ci)
            cr, ci = (pr * cr - pi * ci + er[k][s:s + 1], pr * ci + pi * cr + ei[k][s:s + 1])
        cols = slice(k * LANES, (k + 1) * LANES)
        fpr_ref[:, cols] = jnp.broadcast_to(cr, (SSM_SEGS, LANES))
        fpi_ref[:, cols] = jnp.broadcast_to(ci, (SSM_SEGS, LANES))
        sr.append(jnp.concatenate(starts_r, axis=0))
        si.append(jnp.concatenate(starts_i, axis=0))

    prompt_pass(sr, si, record=True)
    sr = [h0r_ref[:, k * LANES:(k + 1) * LANES] for k in tiles]
    si = [h0i_ref[:, k * LANES:(k + 1) * LANES] for k in tiles]
    for i in range(SSM_STEPS_S):
        sr, si = advance(sr, si, pl.ds(SSM_CHUNKS_P + i, DEC_BATCH, stride=SSM_STEPS_S), record=True)
    for k in tiles:
        fsr_ref[:, k * LANES:(k + 1) * LANES] = sr[k]
        fsi_ref[:, k * LANES:(k + 1) * LANES] = si[k]

    states = jnp.concatenate([s_s[k].astype(BF16) for k in range(2 * SSM_PB)], axis=1)
    contract_last = (((1,), (1,)), ((), ()))
    for cols in col_blocks:
        y_ref[:, cols] += lax.dot_general(states, ccw_s[cols, :], contract_last,
                                          preferred_element_type=F32)


def _ssm_main(x_c, lag, bc, cc, a16r, a16i, h0r, h0i):
    w = SSM_STATE_COLS
    row = SSM_GROUPS * SSM_STATE
    st = pl.BlockSpec((SSM_SEGS, w), lambda j: (0, j))
    st_shape = jax.ShapeDtypeStruct((SSM_SEGS, row), F32)
    proj = pl.BlockSpec((SSM_PB, 2, SSM_T, SSM_GROUP, 2 * LANES), lambda j: (j, 0, 0, 0, 0))
    chunk_rows = pl.BlockSpec((None, N_CHUNKS, SSM_CHUNK_COLS), lambda j: (j, 0, 0))
    return pl.pallas_call(
        _ssm_main_kernel,
        out_shape=(jax.ShapeDtypeStruct((N_LANE_TILES, N_CHUNKS, SSM_CHUNK_COLS), F32),
                   st_shape, st_shape, st_shape, st_shape),
        grid=(N_LANE_TILES,),
        in_specs=[
            chunk_rows,
            pl.BlockSpec((None, SSM_T, LANES, LANES), lambda j: (j, 0, 0, 0)),
            proj, proj, st, st, st, st,
        ],
        out_specs=(chunk_rows, st, st, st, st),
        scratch_shapes=[
            pltpu.VMEM((SSM_CHUNK_COLS, SSM_CHUNK_COLS), BF16),
            pltpu.VMEM((SSM_CHUNK_COLS, 2 * w), BF16),
            pltpu.VMEM((SSM_CHUNK_COLS, 2 * w), BF16),
            pltpu.VMEM((2 * SSM_PB, N_CHUNKS, LANES), F32),
            pltpu.VMEM((2 * SSM_PB, N_CHUNKS, LANES), F32),
        ],
        compiler_params=_params(1),
        name="ssm_main",
    )(x_c, lag, bc, cc, a16r, a16i, h0r, h0i)


def _ssm_post_kernel(y_ref, x_ref, g_ref, d_ref, o_ref, y_s):
    for t in range(SSM_T):
        for j in range(N_LANE_TILES):
            y_s[j, pl.ds(t, CHUNKS_PER_TILE, stride=SSM_T), :] = y_ref[j, :, t * LANES:(t + 1) * LANES]
    x = x_ref[...]
    ms = jnp.mean(x * x, axis=-1, keepdims=True)
    h = (x * lax.rsqrt(ms + EPS)) * g_ref[...]
    c = math.sqrt(2.0 / math.pi)
    for j in range(N_LANE_TILES):
        cols = slice(j * LANES, (j + 1) * LANES)
        y = y_s[j] + d_ref[:, cols] * h[:, cols]
        gelu = 0.5 * y * (1.0 + jnp.tanh(c * (y + 0.044715 * (y * y * y))))
        o_ref[:, cols] = gelu.astype(BF16)


def _ssm_post(y_c, x, g, d):
    spec = pl.BlockSpec((BM, D_MODEL), lambda m: (m, 0))
    vec = pl.BlockSpec((1, D_MODEL), lambda m: (0, 0))
    return pl.pallas_call(
        _ssm_post_kernel,
        out_shape=jax.ShapeDtypeStruct((M_TOTAL, D_MODEL), BF16),
        grid=(N_ROW_TILES,),
        in_specs=[pl.BlockSpec((N_LANE_TILES, CHUNKS_PER_TILE, SSM_CHUNK_COLS), lambda m: (0, m, 0)),
                  spec, vec, vec],
        out_specs=spec,
        scratch_shapes=[pltpu.VMEM((N_LANE_TILES, BM, LANES), F32)],
        compiler_params=_params(1),
        name="ssm_post",
    )(y_c, x, g.reshape(1, D_MODEL), d.reshape(1, D_MODEL))


def kernel(x_prompt, x_sample, state_ret, state_ssm_re, state_ssm_im, cache_conv, norm_mix, norm_ffn, norm_final, ret_w_in, ret_gn, ret_w_out, ssm_a_re, ssm_a_im, ssm_log_dt, ssm_b_re, ssm_b_im, ssm_c_re, ssm_c_im, ssm_d, ssm_w_glu, ffn_w_up, ffn_conv_w, ffn_conv_b, ffn_w_down):
    x = (x_prompt.reshape(M_PROMPT, D_MODEL), x_sample.reshape(M_SAMPLE, D_MODEL))
    xb, ssq = _cast_ssq(*x)
    cos, sin = _rope_tables()

    ret_p, ret_s = [], []
    re_p, im_p, re_s, im_s = [], [], [], []
    conv_p, conv_s = [], []
    state_row = SSM_GROUPS * SSM_STATE
    for i in range(DEPTH):
        j = i // 2
        if i % 2 == 0:
            qk, vg = _mm_retin((xb, ssq, norm_mix[i]), ret_w_in, j, cos, sin)
            o_p, s_p = _retention(qk, vg, ret_gn[j], None, row0=0, n_seq=1,
                                  n_blk=M_PROMPT // (RET_T * RET_SUB), T=RET_T, sub=RET_SUB)
            o_s, s_s = _retention(qk, vg, ret_gn[j], (state_ret, j), row0=M_PROMPT, n_seq=DEC_BATCH,
                                  n_blk=1, T=DEC_SEQ)
            ret_p.append(s_p)
            ret_s.append(s_s)
            x, xb, ssq = _mm_res((o_p, o_s), ret_w_out, j, x, emit=True)
        else:
            h_c = _rmsnorm_chunked(x, norm_mix[i])
            bc, cc, lag, a16r, a16i = _ssm_prep(ssm_a_re[j], ssm_a_im[j], ssm_log_dt[j],
                                               ssm_b_re[j], ssm_b_im[j], ssm_c_re[j], ssm_c_im[j])
            y_c, fpr, fpi, fsr, fsi = _ssm_main(
                h_c, lag, bc, cc, a16r, a16i,
                state_ssm_re[j].reshape(DEC_BATCH, state_row),
                state_ssm_im[j].reshape(DEC_BATCH, state_row))
            re_p.append(fpr[0:1].reshape(1, SSM_GROUPS, SSM_STATE))
            im_p.append(fpi[0:1].reshape(1, SSM_GROUPS, SSM_STATE))
            re_s.append(fsr.reshape(DEC_BATCH, SSM_GROUPS, SSM_STATE))
            im_s.append(fsi.reshape(DEC_BATCH, SSM_GROUPS, SSM_STATE))
            gl = _ssm_post(y_c, x, norm_mix[i], ssm_d[j])
            x, xb, ssq = _mm_glu(gl, ssm_w_glu, j, x)

        u, tail_p, tail_s = _mm_ffnup((xb, ssq, norm_ffn[i]), ffn_w_up, i, ffn_conv_w[i],
                                      ffn_conv_b[i], cache_conv[i])
        conv_p.append(tail_p[SUBLANES - (CONV_W - 1):].reshape(1, CONV_W - 1, FFN_DIM))
        conv_s.append(tail_s.reshape(DEC_BATCH, SUBLANES, FFN_DIM)[:, SUBLANES - (CONV_W - 1):])
        if i + 1 < DEPTH and (i + 1) % 2 == 0:
            x, xb, ssq = _mm_res(u, ffn_w_down, i, x, emit=True)
        else:
            x = _mm_res(u, ffn_w_down, i, x)

    y_prompt = _rmsnorm(x, norm_final, F32, 0, N_PROMPT_TILES).reshape(1, SEQ, D_MODEL)
    y_sample = _rmsnorm(x, norm_final, F32, N_PROMPT_TILES, 1).reshape(DEC_BATCH, DEC_SEQ, D_MODEL)
    return (y_prompt, y_sample, jnp.stack(ret_p), jnp.stack(ret_s),
            jnp.stack(re_p), jnp.stack(im_p), jnp.stack(re_s), jnp.stack(im_s),
            jnp.stack(conv_p), jnp.stack(conv_s))
```

```python
import functools
import math

import numpy as np
import jax
import jax.numpy as jnp
from jax import lax
from jax.experimental import pallas as pl
from jax.experimental.pallas import tpu as pltpu

F32 = jnp.float32
BF16 = jnp.bfloat16

D_MODEL = 2048
SEQ = 8192
DEPTH = 4
DEC_BATCH = 8
DEC_SEQ = 64
PAST_LEN = 1024
CHUNK = 64
RET_HEADS = 8
RET_DK = D_MODEL // RET_HEADS
RET_DV = 2 * RET_DK
RET_QK = RET_HEADS * RET_DK
RET_VD = RET_HEADS * RET_DV
ROPE_BASE = 10000.0
SSM_GROUP = 16
SSM_GROUPS = D_MODEL // SSM_GROUP
SSM_STATE = 64
FFN_DIM = 2 * D_MODEL
CONV_W = 3
EPS = 1e-6

M_PROMPT = SEQ
M_SAMPLE = DEC_BATCH * DEC_SEQ
M_TOTAL = M_PROMPT + M_SAMPLE

LANES = 128
SUBLANES = 8
VMEM_LIMIT_BYTES = 56 * 1024 * 1024

BM = M_SAMPLE
N_ROW_TILES = M_TOTAL // BM
N_PROMPT_TILES = M_PROMPT // BM

RET_T = 256
RET_SUB = 4

SSM_T = 16
SSM_PAIRS = SSM_GROUPS // 2
SSM_PB = 4
SSM_SEGS = 8
SSM_STEPS_P = M_PROMPT // SSM_T // SSM_SEGS
SSM_STEPS_S = DEC_SEQ // SSM_T
SSM_NC = (SSM_STEPS_P + SSM_STEPS_S) * SSM_SEGS


def _params(n_axes):
    return pltpu.CompilerParams(dimension_semantics=("arbitrary",) * n_axes,
                                vmem_limit_bytes=VMEM_LIMIT_BYTES)


def _sigmoid(x):
    return 1.0 / (1.0 + jnp.exp(-x))


def _rope_kernel(cos_ref, sin_ref):
    m = pl.program_id(0)
    half = RET_DK // 2
    freq = lax.broadcasted_iota(jnp.int32, (1, half), 1).astype(F32)
    inv = ROPE_BASE ** (-freq / half)
    r = m * BM + lax.broadcasted_iota(jnp.int32, (BM, half), 0)
    pos = jnp.where(r < M_PROMPT, r, PAST_LEN + ((r - M_PROMPT) & (DEC_SEQ - 1)))
    ang = pos.astype(F32) * inv
    cos_ref[...] = jnp.cos(ang)
    sin_ref[...] = jnp.sin(ang)


def _rope_tables():
    half = RET_DK // 2
    spec = pl.BlockSpec((BM, half), lambda m: (m, 0))
    return pl.pallas_call(
        _rope_kernel,
        out_shape=(jax.ShapeDtypeStruct((M_TOTAL, half), F32),) * 2,
        grid=(N_ROW_TILES,),
        out_specs=(spec, spec),
        compiler_params=_params(1),
        name="rope_tables",
    )()


def _norm_kernel(x_ref, g_ref, o_ref):
    x = x_ref[...]
    ms = jnp.mean(x * x, axis=-1, keepdims=True)
    o_ref[...] = ((x * lax.rsqrt(ms + EPS)) * g_ref[...]).astype(o_ref.dtype)


def _rmsnorm(x, g, out_dtype, tile0=0, n_tiles=N_ROW_TILES):
    return pl.pallas_call(
        _norm_kernel,
        out_shape=jax.ShapeDtypeStruct((n_tiles * BM, D_MODEL), out_dtype),
        grid=(n_tiles,),
        in_specs=[pl.BlockSpec((BM, D_MODEL), lambda m: (tile0 + m, 0)),
                  pl.BlockSpec((1, D_MODEL), lambda m: (0, 0))],
        out_specs=pl.BlockSpec((BM, D_MODEL), lambda m: (m, 0)),
        compiler_params=_params(1),
        name="rmsnorm",
    )(x, g.reshape(1, D_MODEL))


CHUNKS_PER_TILE = BM // 16
N_CHUNKS = M_TOTAL // 16
N_LANE_TILES = D_MODEL // LANES


def _norm_chunked_kernel(x_ref, g_ref, o_ref, h_s):
    x = x_ref[...]
    ms = jnp.mean(x * x, axis=-1, keepdims=True)
    h = (x * lax.rsqrt(ms + EPS)) * g_ref[...]
    for j in range(N_LANE_TILES):
        h_s[j] = h[:, j * LANES:(j + 1) * LANES]
    for t in range(SSM_T):
        for j in range(N_LANE_TILES):
            rows = h_s[j, pl.ds(t, CHUNKS_PER_TILE, stride=SSM_T), :]
            o_ref[j, :, t * LANES:(t + 1) * LANES] = rows.astype(BF16)


def _rmsnorm_chunked(x, g):
    return pl.pallas_call(
        _norm_chunked_kernel,
        out_shape=jax.ShapeDtypeStruct((N_LANE_TILES, N_CHUNKS, SSM_T * LANES), BF16),
        grid=(N_ROW_TILES,),
        in_specs=[pl.BlockSpec((BM, D_MODEL), lambda m: (m, 0)),
                  pl.BlockSpec((1, D_MODEL), lambda m: (0, 0))],
        out_specs=pl.BlockSpec((N_LANE_TILES, CHUNKS_PER_TILE, SSM_T * LANES), lambda m: (0, m, 0)),
        scratch_shapes=[pltpu.VMEM((N_LANE_TILES, BM, LANES), F32)],
        compiler_params=_params(1),
        name="rmsnorm_chunked",
    )(x, g.reshape(1, D_MODEL))


def _weight_spec(k, bn, layer, col0=0):
    return pl.BlockSpec((None, k, bn), lambda n, m: (layer, 0, col0 + n), pipeline_mode=pl.Buffered(1))


RET_IN_BN = 2048


def _emit_norm_inputs(x_new, xb_ref, ssq_ref):
    xb_ref[...] = x_new.astype(BF16)
    sq = x_new * x_new
    part = sq[:, 0:LANES]
    for t in range(1, x_new.shape[1] // LANES):
        part = part + sq[:, t * LANES:(t + 1) * LANES]
    ssq_ref[...] = part


def _row_scale(ssq_ref):
    total = jnp.sum(jnp.sum(ssq_ref[...], axis=0), axis=-1, keepdims=True)
    return lax.rsqrt(total * (1.0 / D_MODEL) + EPS)


def _norm_operands(norm):
    xb, ssq, g = norm
    args = [xb, ssq, g.reshape(D_MODEL, 1)]
    specs = [pl.BlockSpec((BM, D_MODEL), lambda n, m: (m, 0)),
             pl.BlockSpec((ssq.shape[0], BM, LANES), lambda n, m: (0, m, 0)),
             pl.BlockSpec((D_MODEL, 1), lambda n, m: (0, 0))]
    return args, specs


def _norm_outputs(n_out, bn):
    shapes = [jax.ShapeDtypeStruct((M_TOTAL, n_out), BF16),
              jax.ShapeDtypeStruct((n_out // bn, M_TOTAL, LANES), F32)]
    specs = [pl.BlockSpec((BM, bn), lambda n, m: (m, n)),
             pl.BlockSpec((None, BM, LANES), lambda n, m: (n, m, 0))]
    return shapes, specs


def _cast_ssq_kernel(xp_ref, xs_ref, xb_ref, ssq_ref):
    x = jnp.where(pl.program_id(0) < N_PROMPT_TILES, xp_ref[...], xs_ref[...])
    _emit_norm_inputs(x, xb_ref, ssq_ref)


def _cast_ssq(x_prompt, x_sample):
    return pl.pallas_call(
        _cast_ssq_kernel,
        out_shape=(jax.ShapeDtypeStruct((M_TOTAL, D_MODEL), BF16),
                   jax.ShapeDtypeStruct((1, M_TOTAL, LANES), F32)),
        grid=(N_ROW_TILES,),
        in_specs=[pl.BlockSpec((BM, D_MODEL), lambda m: (jnp.minimum(m, N_PROMPT_TILES - 1), 0)),
                  pl.BlockSpec((BM, D_MODEL), lambda m: (0, 0))],
        out_specs=(pl.BlockSpec((BM, D_MODEL), lambda m: (m, 0)),
                   pl.BlockSpec((None, BM, LANES), lambda m: (0, m, 0))),
        compiler_params=_params(1),
        name="cast_ssq",
    )(x_prompt, x_sample)


def _mm_qk_kernel(a_ref, ssq_ref, g_ref, w_ref, cos_ref, sin_ref, o_ref, wb_ref):
    n = pl.program_id(0)
    n_q = RET_QK // RET_IN_BN

    @pl.when(pl.program_id(1) == 0)
    def _():
        wb_ref[...] = (w_ref[...] * g_ref[...]).astype(BF16)

    acc = jnp.dot(a_ref[...], wb_ref[...], preferred_element_type=F32) * _row_scale(ssq_ref)
    scale = jnp.where(n >= n_q, RET_DK ** -0.5, 1.0).astype(F32)
    c = cos_ref[...]
    s = sin_ref[...]
    half = RET_DK // 2
    for hh in range(RET_IN_BN // RET_DK):
        lo = hh * RET_DK
        x1 = acc[:, lo:lo + half]
        x2 = acc[:, lo + half:lo + RET_DK]
        o_ref[:, lo:lo + half] = ((x1 * c - x2 * s) * scale).astype(BF16)
        o_ref[:, lo + half:lo + RET_DK] = ((x1 * s + x2 * c) * scale).astype(BF16)


def _mm_cast_kernel(a_ref, ssq_ref, g_ref, w_ref, o_ref, wb_ref):
    @pl.when(pl.program_id(1) == 0)
    def _():
        wb_ref[...] = (w_ref[...] * g_ref[...]).astype(BF16)

    acc = jnp.dot(a_ref[...], wb_ref[...], preferred_element_type=F32) * _row_scale(ssq_ref)
    o_ref[...] = acc.astype(BF16)


def _mm_retin(norm, w, layer, cos, sin):
    bn = RET_IN_BN
    _, k, n_out = w.shape
    half = RET_DK // 2
    n_qk = 2 * RET_QK // bn
    norm_args, norm_specs = _norm_operands(norm)
    out_spec = pl.BlockSpec((BM, bn), lambda n, m: (m, n))
    rope_spec = pl.BlockSpec((BM, half), lambda n, m: (m, 0))
    qk = pl.pallas_call(
        _mm_qk_kernel,
        out_shape=jax.ShapeDtypeStruct((M_TOTAL, 2 * RET_QK), BF16),
        grid=(n_qk, N_ROW_TILES),
        in_specs=norm_specs + [_weight_spec(k, bn, layer), rope_spec, rope_spec],
        out_specs=out_spec,
        scratch_shapes=[pltpu.VMEM((k, bn), BF16)],
        compiler_params=_params(2),
        name="mm_ret_qk",
    )(*norm_args, w, cos, sin)
    vg = pl.pallas_call(
        _mm_cast_kernel,
        out_shape=jax.ShapeDtypeStruct((M_TOTAL, 2 * RET_VD), BF16),
        grid=(n_out // bn - n_qk, N_ROW_TILES),
        in_specs=norm_specs + [_weight_spec(k, bn, layer, n_qk)],
        out_specs=out_spec,
        scratch_shapes=[pltpu.VMEM((k, bn), BF16)],
        compiler_params=_params(2),
        name="mm_ret_vg",
    )(*norm_args, w)
    return qk, vg


def _row_split_specs(cols, col_index):
    return [pl.BlockSpec((BM, cols), lambda n, m: (jnp.minimum(m, N_PROMPT_TILES - 1), col_index(n))),
            pl.BlockSpec((BM, cols), lambda n, m: (0, col_index(n)))]


def _mm_res_kernel(*refs, split_a, split_x, emit):
    refs = list(refs)
    ap_ref = refs.pop(0)
    as_ref = refs.pop(0) if split_a else None
    w_ref = refs.pop(0)
    xp_ref = refs.pop(0)
    xs_ref = refs.pop(0) if split_x else None
    o_ref = refs.pop(0)
    xb_ref, ssq_ref = (refs.pop(0), refs.pop(0)) if emit else (None, None)
    (wb_ref,) = refs
    m = pl.program_id(1)
    on_prompt = m < N_PROMPT_TILES

    @pl.when(m == 0)
    def _():
        wb_ref[...] = w_ref[...].astype(BF16)

    def body(a_ref):
        x = xp_ref[...]
        if split_x:
            x = jnp.where(on_prompt, x, xs_ref[...])
        x_new = x + jnp.dot(a_ref[...], wb_ref[...], preferred_element_type=F32)
        o_ref[...] = x_new
        if emit:
            _emit_norm_inputs(x_new, xb_ref, ssq_ref)

    if split_a:
        pl.when(on_prompt)(lambda: body(ap_ref))
        pl.when(jnp.logical_not(on_prompt))(lambda: body(as_ref))
    else:
        body(ap_ref)


def _mm_res(a, w, layer, x, emit=False):
    bn = 1024
    _, k, n_out = w.shape
    split_a = isinstance(a, tuple)
    split_x = isinstance(x, tuple)
    a_specs = _row_split_specs(k, lambda n: 0) if split_a else [pl.BlockSpec((BM, k), lambda n, m: (m, 0))]
    x_specs = _row_split_specs(bn, lambda n: n) if split_x else [pl.BlockSpec((BM, bn), lambda n, m: (m, n))]
    a_args = list(a) if split_a else [a]
    x_args = list(x) if split_x else [x]
    out_shapes = [jax.ShapeDtypeStruct((M_TOTAL, n_out), F32)]
    out_specs = [pl.BlockSpec((BM, bn), lambda n, m: (m, n))]
    if emit:
        shapes, specs = _norm_outputs(n_out, bn)
        out_shapes += shapes
        out_specs += specs
    out = pl.pallas_call(
        functools.partial(_mm_res_kernel, split_a=split_a, split_x=split_x, emit=emit),
        out_shape=tuple(out_shapes),
        grid=(n_out // bn, N_ROW_TILES),
        in_specs=a_specs + [_weight_spec(k, bn, layer)] + x_specs,
        out_specs=tuple(out_specs),
        scratch_shapes=[pltpu.VMEM((k, bn), BF16)],
        compiler_params=_params(2),
        name="mm_residual",
    )(*a_args, w, *x_args)
    return out if emit else out[0]


def _mm_glu_kernel(a_ref, wa_ref, wg_ref, x_ref, o_ref, xb_ref, ssq_ref, wab_ref, wgb_ref):
    @pl.when(pl.program_id(1) == 0)
    def _():
        wab_ref[...] = wa_ref[...].astype(BF16)
        wgb_ref[...] = wg_ref[...].astype(BF16)

    a = a_ref[...]
    ga = jnp.dot(a, wab_ref[...], preferred_element_type=F32)
    gb = jnp.dot(a, wgb_ref[...], preferred_element_type=F32)
    x_new = x_ref[...] + ga * _sigmoid(gb)
    o_ref[...] = x_new
    _emit_norm_inputs(x_new, xb_ref, ssq_ref)


def _mm_glu(a, w, layer, x):
    bn = 1024
    _, k, n2 = w.shape
    n_out = n2 // 2
    nb = n_out // bn
    norm_shapes, norm_specs = _norm_outputs(n_out, bn)
    return pl.pallas_call(
        _mm_glu_kernel,
        out_shape=(jax.ShapeDtypeStruct((M_TOTAL, n_out), F32), *norm_shapes),
        grid=(nb, N_ROW_TILES),
        in_specs=[
            pl.BlockSpec((BM, k), lambda n, m: (m, 0)),
            _weight_spec(k, bn, layer),
            _weight_spec(k, bn, layer, nb),
            pl.BlockSpec((BM, bn), lambda n, m: (m, n)),
        ],
        out_specs=(pl.BlockSpec((BM, bn), lambda n, m: (m, n)), *norm_specs),
        scratch_shapes=[pltpu.VMEM((k, bn), BF16), pltpu.VMEM((k, bn), BF16)],
        compiler_params=_params(2),
        name="mm_glu",
    )(a, w, w, x)


def _mm_ffnup_kernel(h_ref, ssq_ref, g_ref, wa_ref, wg_ref, cw_ref, cb_ref, cache_ref,
                     o_ref, tp_ref, ts_ref, wab_ref, wgb_ref, carry_ref):
    m = pl.program_id(1)

    @pl.when(m == 0)
    def _():
        g = g_ref[...]
        wab_ref[...] = (wa_ref[...] * g).astype(BF16)
        wgb_ref[...] = (wg_ref[...] * g).astype(BF16)
        carry_ref[...] = jnp.zeros_like(carry_ref)

    h = h_ref[...]
    r = _row_scale(ssq_ref)
    a = jnp.dot(h, wab_ref[...], preferred_element_type=F32) * r
    b = jnp.dot(h, wgb_ref[...], preferred_element_type=F32) * r
    r1 = pltpu.roll(a, 1, axis=0)
    r2 = pltpu.roll(a, 2, axis=0)
    cb = cb_ref[...]
    w0 = cw_ref[0:1, :]
    w1 = cw_ref[1:2, :]
    w2 = cw_ref[2:3, :]

    def gated(a_rows, prev1, prev2, b_rows):
        conv = cb + w0 * prev2 + w1 * prev1 + w2 * a_rows
        return ((conv * _sigmoid(conv)) * b_rows).astype(BF16)

    o_ref[...] = gated(a, r1, r2, b)

    row8 = lax.broadcasted_iota(jnp.int32, (SUBLANES, 1), 0)
    heads = [tuple(z[s * DEC_SEQ:s * DEC_SEQ + SUBLANES] for z in (a, r1, r2, b))
             for s in range(DEC_BATCH)]
    tails = [a[(s + 1) * DEC_SEQ - SUBLANES:(s + 1) * DEC_SEQ] for s in range(DEC_BATCH)]

    def redo_head(s, before1, before2):
        a8, r1_8, r2_8, b8 = heads[s]
        prev1 = jnp.where(row8 == 0, before1, r1_8)
        prev2 = jnp.where(row8 == 0, before2, jnp.where(row8 == 1, before1, r2_8))
        o_ref[s * DEC_SEQ:s * DEC_SEQ + SUBLANES, :] = gated(a8, prev1, prev2, b8)

    @pl.when(m < N_PROMPT_TILES)
    def _():
        c = carry_ref[...]
        redo_head(0, c[SUBLANES - 1:SUBLANES, :], c[SUBLANES - 2:SUBLANES - 1, :])
        carry_ref[...] = tails[DEC_BATCH - 1]

        @pl.when(m == N_PROMPT_TILES - 1)
        def _():
            tp_ref[...] = tails[DEC_BATCH - 1]

    @pl.when(m == N_PROMPT_TILES)
    def _():
        for s in range(DEC_BATCH):
            c = cache_ref[s]
            redo_head(s, c[1:2, :], c[0:1, :])
            ts_ref[s * SUBLANES:(s + 1) * SUBLANES, :] = tails[s]


def _mm_ffnup(norm, w, layer, conv_w, conv_b, cache):
    bn = 1024
    k = w.shape[1]
    nb = FFN_DIM // bn
    norm_args, norm_specs = _norm_operands(norm)
    return pl.pallas_call(
        _mm_ffnup_kernel,
        out_shape=(
            jax.ShapeDtypeStruct((M_TOTAL, FFN_DIM), BF16),
            jax.ShapeDtypeStruct((SUBLANES, FFN_DIM), F32),
            jax.ShapeDtypeStruct((DEC_BATCH * SUBLANES, FFN_DIM), F32),
        ),
        grid=(nb, N_ROW_TILES),
        in_specs=norm_specs + [
            _weight_spec(k, bn, layer),
            _weight_spec(k, bn, layer, nb),
            pl.BlockSpec((CONV_W, bn), lambda n, m: (0, n)),
            pl.BlockSpec((1, bn), lambda n, m: (0, n)),
            pl.BlockSpec((DEC_BATCH, CONV_W - 1, bn), lambda n, m: (0, 0, n)),
        ],
        out_specs=(
            pl.BlockSpec((BM, bn), lambda n, m: (m, n)),
            pl.BlockSpec((SUBLANES, bn), lambda n, m: (0, n)),
            pl.BlockSpec((DEC_BATCH * SUBLANES, bn), lambda n, m: (0, n)),
        ),
        scratch_shapes=[pltpu.VMEM((k, bn), BF16), pltpu.VMEM((k, bn), BF16),
                        pltpu.VMEM((SUBLANES, bn), F32)],
        compiler_params=_params(2),
        name="mm_ffn_up",
    )(*norm_args, w, w, conv_w, conv_b.reshape(1, FFN_DIM), cache)


def _ret_tables(lg, T):
    i = lax.broadcasted_iota(jnp.int32, (T, T), 0)
    j = lax.broadcasted_iota(jnp.int32, (T, T), 1)
    dist = jnp.abs(i - j).astype(F32)
    shift = int(math.log2(CHUNK))
    visible = (j >> shift) <= (i >> shift)
    mask = jnp.where(visible, jnp.exp(dist * lg), 0.0)
    t = lax.broadcasted_iota(jnp.int32, (T, 1), 0).astype(F32)
    cross = jnp.exp((t + 1.0) * lg)
    k_dec = jnp.exp((T - 1.0 - t) * lg)
    decay = jnp.exp(float(T) * lg)
    return mask, cross, k_dec, decay


def _ret_block(q, k, v, g, state, tables, gn):
    mask, cross, k_dec, decay = tables
    scores = lax.dot_general(q, k, (((1,), (1,)), ((), ())), preferred_element_type=F32)
    scores = scores * mask
    out = jnp.dot(scores.astype(BF16), v, preferred_element_type=F32)
    out = out + jnp.dot(q, state.astype(BF16), preferred_element_type=F32) * cross

    kd = (k.astype(F32) * k_dec).astype(BF16)
    new_state = decay * state + lax.dot_general(
        kd, v, (((0,), (0,)), ((), ())), preferred_element_type=F32)

    mu = jnp.mean(out, axis=-1, keepdims=True)
    oc = out - mu
    var = jnp.mean(oc * oc, axis=-1, keepdims=True)
    normed = (oc * lax.rsqrt(var + EPS)) * gn
    g = g.astype(F32)
    return (normed * (g * _sigmoid(g))).astype(BF16), new_state


def _ret_prompt_kernel(lg_ref, q_ref, k_ref, v_ref, g_ref, gn_ref, o_ref, sl_ref, state_ref, mask_ref):
    blk = pl.program_id(1)
    lg = lg_ref[...][:, 0:1]

    @pl.when(blk == 0)
    def _():
        state_ref[...] = jnp.zeros_like(state_ref)
        mask_ref[...] = _ret_tables(lg, RET_T)[0]

    _, cross, k_dec, decay = _ret_tables(lg, RET_T)
    tables = (mask_ref[...], cross, k_dec, decay)
    gn = gn_ref[...]
    state = state_ref[...]
    for sb in range(RET_SUB):
        rows = slice(sb * RET_T, (sb + 1) * RET_T)
        o_ref[rows, :], state = _ret_block(q_ref[rows, :], k_ref[rows, :], v_ref[rows, :],
                                           g_ref[rows, :], state, tables, gn)
    state_ref[...] = state

    @pl.when(blk == pl.num_programs(1) - 1)
    def _():
        sl_ref[...] = state


def _ret_sample_kernel(lg_ref, q_ref, k_ref, v_ref, g_ref, gn_ref, s0_ref, o_ref, sl_ref):
    lg = lg_ref[...][:, 0:1]
    tables = _ret_tables(lg, DEC_SEQ)
    gn = gn_ref[...]
    for s in range(DEC_BATCH):
        rows = slice(s * DEC_SEQ, (s + 1) * DEC_SEQ)
        o_ref[rows, :], sl_ref[s] = _ret_block(q_ref[rows, :], k_ref[rows, :], v_ref[rows, :],
                                               g_ref[rows, :], s0_ref[s], tables, gn)


def _retention(qk, vg, gn, s0_stack, layer):
    log_g = np.log1p(-np.exp2(-5.0 - np.arange(RET_HEADS, dtype=np.float32))).astype(np.float32)
    lg = jnp.asarray(np.broadcast_to(log_g[:, None, None], (RET_HEADS, 1, LANES)).copy())
    gn = gn.reshape(1, RET_VD)
    k_off = RET_QK // RET_DK
    g_off = RET_VD // RET_DV
    rows_p = RET_T * RET_SUB
    o_p, s_p = pl.pallas_call(
        _ret_prompt_kernel,
        out_shape=(jax.ShapeDtypeStruct((M_PROMPT, RET_VD), BF16),
                   jax.ShapeDtypeStruct((1, RET_HEADS, RET_DK, RET_DV), F32)),
        grid=(RET_HEADS, M_PROMPT // rows_p),
        in_specs=[
            pl.BlockSpec((None, 1, LANES), lambda h, b: (h, 0, 0)),
            pl.BlockSpec((rows_p, RET_DK), lambda h, b: (b, h)),
            pl.BlockSpec((rows_p, RET_DK), lambda h, b: (b, k_off + h)),
            pl.BlockSpec((rows_p, RET_DV), lambda h, b: (b, h)),
            pl.BlockSpec((rows_p, RET_DV), lambda h, b: (b, g_off + h)),
            pl.BlockSpec((1, RET_DV), lambda h, b: (0, h)),
        ],
        out_specs=(pl.BlockSpec((rows_p, RET_DV), lambda h, b: (b, h)),
                   pl.BlockSpec((None, None, RET_DK, RET_DV), lambda h, b: (0, h, 0, 0))),
        scratch_shapes=[pltpu.VMEM((RET_DK, RET_DV), F32), pltpu.VMEM((RET_T, RET_T), F32)],
        compiler_params=_params(2),
        name="retention",
    )(lg, qk, qk, vg, vg, gn)
    sample_block = M_PROMPT // M_SAMPLE
    o_s, s_s = pl.pallas_call(
        _ret_sample_kernel,
        out_shape=(jax.ShapeDtypeStruct((M_SAMPLE, RET_VD), BF16),
                   jax.ShapeDtypeStruct((DEC_BATCH, RET_HEADS, RET_DK, RET_DV), F32)),
        grid=(RET_HEADS,),
        in_specs=[
            pl.BlockSpec((None, 1, LANES), lambda h: (h, 0, 0)),
            pl.BlockSpec((M_SAMPLE, RET_DK), lambda h: (sample_block, h)),
            pl.BlockSpec((M_SAMPLE, RET_DK), lambda h: (sample_block, k_off + h)),
            pl.BlockSpec((M_SAMPLE, RET_DV), lambda h: (sample_block, h)),
            pl.BlockSpec((M_SAMPLE, RET_DV), lambda h: (sample_block, g_off + h)),
            pl.BlockSpec((1, RET_DV), lambda h: (0, h)),
            pl.BlockSpec((None, DEC_BATCH, None, RET_DK, RET_DV), lambda h: (layer, 0, h, 0, 0)),
        ],
        out_specs=(pl.BlockSpec((M_SAMPLE, RET_DV), lambda h: (0, h)),
                   pl.BlockSpec((DEC_BATCH, None, RET_DK, RET_DV), lambda h: (0, h, 0, 0))),
        compiler_params=_params(1),
        name="retention_sample",
    )(lg, qk, qk, vg, vg, gn, s0_stack)
    return o_p, o_s, s_p, s_s


def _ssm_prep_kernel(ar_ref, ai_ref, ldt_ref, br_ref, bi_ref, cr_ref, ci_ref,
                     bc_ref, cc_ref, lag_ref, a16r_ref, a16i_ref, vr_s, vi_s):
    ar = ar_ref[...]
    ai = ai_ref[...]
    dt = jnp.exp(ldt_ref[...])
    mag = jnp.exp(ar * dt)
    ang = ai * dt
    abr = mag * jnp.cos(ang)
    abi = mag * jnp.sin(ang)
    den = ar * ar + ai * ai
    nr = abr - 1.0
    ni = abi
    cfr = (nr * ar + ni * ai) / den
    cfi = (ni * ar - nr * ai) / den

    def b4(z):
        return z[:, None]

    br = br_ref[...]
    bi = bi_ref[...]
    bbr = b4(cfr) * br - b4(cfi) * bi
    bbi = b4(cfr) * bi + b4(cfi) * br
    cre = cr_ref[...]
    cim = ci_ref[...]

    powers = []
    pr = jnp.ones_like(ar)
    pi = jnp.zeros_like(ar)
    for _ in range(SSM_T + 1):
        powers.append((pr, pi))
        pr, pi = pr * abr - pi * abi, pr * abi + pi * abr
    a16r_ref[...] = powers[SSM_T][0]
    a16i_ref[...] = powers[SSM_T][1]

    for m in range(SSM_T + 1):
        pr, pi = powers[m]
        vr = b4(pr) * cre - b4(pi) * cim
        vi = b4(pr) * cim + b4(pi) * cre
        if m < SSM_T:
            vr_s[:, :, m] = vr
            vi_s[:, :, m] = vi
        if m >= 1:
            cc_ref[:, :, m - 1, :, 0:LANES] = vr.astype(BF16)
            cc_ref[:, :, m - 1, :, LANES:2 * LANES] = (-vi).astype(BF16)

    for t in range(SSM_T):
        pr, pi = powers[SSM_T - 1 - t]
        bc_ref[:, :, t, :, 0:LANES] = (b4(pr) * bbr - b4(pi) * bbi).astype(BF16)
        bc_ref[:, :, t, :, LANES:2 * LANES] = (b4(pr) * bbi + b4(pi) * bbr).astype(BF16)

    width = SSM_T * SSM_GROUP
    lane = lax.broadcasted_iota(jnp.int32, (SSM_GROUP, LANES), 1)
    contract_last = (((1,), (1,)), ((), ()))
    for q in range(SSM_PB):
        for half in range(2):
            gl = 2 * q + half
            vr_all = vr_s[q, half].reshape(width, LANES)
            vi_all = vi_s[q, half].reshape(width, LANES)
            krow = (lax.dot_general(bbr[q, half], vr_all, contract_last,
                                    precision=lax.Precision.HIGHEST, preferred_element_type=F32)
                    - lax.dot_general(bbi[q, half], vi_all, contract_last,
                                      precision=lax.Precision.HIGHEST, preferred_element_type=F32))
            in_block = (lane >= gl * SSM_GROUP) & (lane < (gl + 1) * SSM_GROUP)
            for m in range(SSM_T):
                shift = ((gl - m) * SSM_GROUP) % width
                moved = krow if shift == 0 else pltpu.roll(krow, shift, axis=1)
                lag_ref[m, gl * SSM_GROUP:(gl + 1) * SSM_GROUP, :] = jnp.where(
                    in_block, moved[:, 0:LANES], 0.0).astype(BF16)


def _pair_pack(x):
    x = x.reshape(SSM_PAIRS, 2, SSM_GROUP, SSM_STATE)
    lo = jnp.pad(x[:, 0], ((0, 0), (0, 0), (0, SSM_STATE)))
    hi = jnp.pad(x[:, 1], ((0, 0), (0, 0), (SSM_STATE, 0)))
    return jnp.stack([lo, hi], axis=1)


def _ssm_prep(a_re, a_im, log_dt, b_re, b_im, c_re, c_im):
    assert 2 * SSM_PB * SSM_GROUP == LANES
    ar = a_re.reshape(SSM_PAIRS, 1, LANES)
    ai = a_im.reshape(SSM_PAIRS, 1, LANES)
    ldt = jnp.repeat(log_dt, SSM_STATE).reshape(SSM_PAIRS, 1, LANES)
    b2r = _pair_pack(jnp.transpose(b_re, (0, 2, 1)))
    b2i = _pair_pack(jnp.transpose(b_im, (0, 2, 1)))
    c2r = _pair_pack(c_re)
    c2i = _pair_pack(c_im)
    vec = pl.BlockSpec((SSM_PB, 1, LANES), lambda p: (p, 0, 0))
    mat = pl.BlockSpec((SSM_PB, 2, SSM_GROUP, LANES), lambda p: (p, 0, 0, 0))
    proj = pl.BlockSpec((SSM_PB, 2, SSM_T, SSM_GROUP, 2 * LANES), lambda p: (p, 0, 0, 0, 0))
    bc, cc, lag, a16r, a16i = pl.pallas_call(
        _ssm_prep_kernel,
        out_shape=(
            jax.ShapeDtypeStruct((SSM_PAIRS, 2, SSM_T, SSM_GROUP, 2 * LANES), BF16),
            jax.ShapeDtypeStruct((SSM_PAIRS, 2, SSM_T, SSM_GROUP, 2 * LANES), BF16),
            jax.ShapeDtypeStruct((N_LANE_TILES, SSM_T, LANES, LANES), BF16),
            jax.ShapeDtypeStruct((SSM_PAIRS, 1, LANES), F32),
            jax.ShapeDtypeStruct((SSM_PAIRS, 1, LANES), F32),
        ),
        grid=(N_LANE_TILES,),
        in_specs=[vec, vec, vec, mat, mat, mat, mat],
        out_specs=(proj, proj,
                   pl.BlockSpec((None, SSM_T, LANES, LANES), lambda p: (p, 0, 0, 0)),
                   vec, vec),
        scratch_shapes=[pltpu.VMEM((SSM_PB, 2, SSM_T, SSM_GROUP, LANES), F32)] * 2,
        compiler_params=_params(1),
        name="ssm_prep",
    )(ar, ai, ldt, b2r, b2i, c2r, c2i)
    row = SSM_GROUPS * SSM_STATE
    a16r = jnp.broadcast_to(a16r.reshape(1, row), (SSM_SEGS, row))
    a16i = jnp.broadcast_to(a16i.reshape(1, row), (SSM_SEGS, row))
    return bc, cc, lag, a16r, a16i


SSM_CHUNK_COLS = SSM_T * LANES
SSM_STATE_COLS = SSM_PB * LANES
SSM_CHUNKS_P = M_PROMPT // SSM_T
SSM_COL_BLOCKS = 4
SSM_SCAN_UNROLL = 2


def _ssm_main_kernel(x_ref, lag_ref, bc_ref, cc_ref, ar_ref, ai_ref, h0r_ref, h0i_ref,
                     y_ref, fpr_ref, fpi_ref, fsr_ref, fsi_ref,
                     w_s, bcw_s, ccw_s, d_s, s_s):
    w = SSM_STATE_COLS

    @pl.when(pl.program_id(0) == 0)
    def _():
        w_s[...] = jnp.zeros_like(w_s)
        bcw_s[...] = jnp.zeros_like(bcw_s)
        ccw_s[...] = jnp.zeros_like(ccw_s)

    for t in range(SSM_T):
        for t2 in range(t, SSM_T):
            w_s[t * LANES:(t + 1) * LANES, t2 * LANES:(t2 + 1) * LANES] = lag_ref[t2 - t]
    for q in range(SSM_PB):
        for half in range(2):
            for t in range(SSM_T):
                r0 = t * LANES + (2 * q + half) * SSM_GROUP
                rows = slice(r0, r0 + SSM_GROUP)
                bcw_s[rows, q * LANES:(q + 1) * LANES] = bc_ref[q, half, t, :, 0:LANES]
                bcw_s[rows, w + q * LANES:w + (q + 1) * LANES] = bc_ref[q, half, t, :, LANES:2 * LANES]
                ccw_s[rows, q * LANES:(q + 1) * LANES] = cc_ref[q, half, t, :, 0:LANES]
                ccw_s[rows, w + q * LANES:w + (q + 1) * LANES] = cc_ref[q, half, t, :, LANES:2 * LANES]

    x = x_ref[...]
    for k2 in range(SSM_PB):
        d = jnp.dot(x, bcw_s[:, 2 * k2 * LANES:2 * (k2 + 1) * LANES], preferred_element_type=F32)
        d_s[2 * k2] = d[:, 0:LANES]
        d_s[2 * k2 + 1] = d[:, LANES:2 * LANES]

    tokens_per_block = SSM_T // SSM_COL_BLOCKS
    col_blocks = []
    for cb in range(SSM_COL_BLOCKS):
        k_hi = (cb + 1) * tokens_per_block * LANES
        cols = slice(cb * tokens_per_block * LANES, k_hi)
        col_blocks.append(cols)
        y_ref[:, cols] = jnp.dot(x_ref[:, 0:k_hi], w_s[0:k_hi, cols], preferred_element_type=F32)

    tiles = range(SSM_PB)
    ar = [ar_ref[:, k * LANES:(k + 1) * LANES] for k in tiles]
    ai = [ai_ref[:, k * LANES:(k + 1) * LANES] for k in tiles]

    def advance(sr, si, rows, record):
        out_r, out_i = [], []
        for k in tiles:
            if record:
                s_s[k, rows, :] = sr[k]
                s_s[SSM_PB + k, rows, :] = si[k]
            dr = d_s[k, rows, :]
            di = d_s[SSM_PB + k, rows, :]
            out_r.append(ar[k] * sr[k] - ai[k] * si[k] + dr)
            out_i.append(ar[k] * si[k] + ai[k] * sr[k] + di)
        return out_r, out_i

    def prompt_rows(i):
        return pl.ds(i, SSM_SEGS, stride=SSM_STEPS_P)

    def prompt_pass(start_r, start_i, record):
        def body(i, carry):
            out_r, out_i = advance(list(carry[0]), list(carry[1]), prompt_rows(i), record)
            return tuple(out_r), tuple(out_i)
        out_r, out_i = lax.fori_loop(0, SSM_STEPS_P, body, (tuple(start_r), tuple(start_i)),
                                     unroll=SSM_SCAN_UNROLL)
        return list(out_r), list(out_i)

    zeros = [jnp.zeros((SSM_SEGS, LANES), F32) for _ in tiles]
    er, ei = prompt_pass(zeros, zeros, record=False)

    sr, si = [], []
    for k in tiles:
        pr, pi = ar[k][0:1], ai[k][0:1]
        for _ in range(int(math.log2(SSM_STEPS_P))):
            pr, pi = pr * pr - pi * pi, 2.0 * pr * pi
        cr = jnp.zeros((1, LANES), F32)
        ci = cr
        starts_r, starts_i = [], []
        for s in range(SSM_SEGS):
            starts_r.append(cr)
            starts_i.append(ci)
            cr, ci = (pr * cr - pi * ci + er[k][s:s + 1], pr * ci + pi * cr + ei[k][s:s + 1])
        cols = slice(k * LANES, (k + 1) * LANES)
        fpr_ref[:, cols] = jnp.broadcast_to(cr, (SSM_SEGS, LANES))
        fpi_ref[:, cols] = jnp.broadcast_to(ci, (SSM_SEGS, LANES))
        sr.append(jnp.concatenate(starts_r, axis=0))
        si.append(jnp.concatenate(starts_i, axis=0))

    prompt_pass(sr, si, record=True)
    sr = [h0r_ref[:, k * LANES:(k + 1) * LANES] for k in tiles]
    si = [h0i_ref[:, k * LANES:(k + 1) * LANES] for k in tiles]
    for i in range(SSM_STEPS_S):
        sr, si = advance(sr, si, pl.ds(SSM_CHUNKS_P + i, DEC_BATCH, stride=SSM_STEPS_S), record=True)
    for k in tiles:
        fsr_ref[:, k * LANES:(k + 1) * LANES] = sr[k]
        fsi_ref[:, k * LANES:(k + 1) * LANES] = si[k]

    states = jnp.concatenate([s_s[k].astype(BF16) for k in range(2 * SSM_PB)], axis=1)
    contract_last = (((1,), (1,)), ((), ()))
    for cols in col_blocks:
        y_ref[:, cols] += lax.dot_general(states, ccw_s[cols, :], contract_last,
                                          preferred_element_type=F32)


def _ssm_main(x_c, lag, bc, cc, a16r, a16i, h0r, h0i):
    w = SSM_STATE_COLS
    row = SSM_GROUPS * SSM_STATE
    st = pl.BlockSpec((SSM_SEGS, w), lambda j: (0, j))
    st_shape = jax.ShapeDtypeStruct((SSM_SEGS, row), F32)
    proj = pl.BlockSpec((SSM_PB, 2, SSM_T, SSM_GROUP, 2 * LANES), lambda j: (j, 0, 0, 0, 0))
    chunk_rows = pl.BlockSpec((None, N_CHUNKS, SSM_CHUNK_COLS), lambda j: (j, 0, 0))
    return pl.pallas_call(
        _ssm_main_kernel,
        out_shape=(jax.ShapeDtypeStruct((N_LANE_TILES, N_CHUNKS, SSM_CHUNK_COLS), F32),
                   st_shape, st_shape, st_shape, st_shape),
        grid=(N_LANE_TILES,),
        in_specs=[
            chunk_rows,
            pl.BlockSpec((None, SSM_T, LANES, LANES), lambda j: (j, 0, 0, 0)),
            proj, proj, st, st, st, st,
        ],
        out_specs=(chunk_rows, st, st, st, st),
        scratch_shapes=[
            pltpu.VMEM((SSM_CHUNK_COLS, SSM_CHUNK_COLS), BF16),
            pltpu.VMEM((SSM_CHUNK_COLS, 2 * w), BF16),
            pltpu.VMEM((SSM_CHUNK_COLS, 2 * w), BF16),
            pltpu.VMEM((2 * SSM_PB, N_CHUNKS, LANES), F32),
            pltpu.VMEM((2 * SSM_PB, N_CHUNKS, LANES), F32),
        ],
        compiler_params=_params(1),
        name="ssm_main",
    )(x_c, lag, bc, cc, a16r, a16i, h0r, h0i)


def _ssm_post_kernel(y_ref, x_ref, g_ref, d_ref, o_ref, y_s):
    for t in range(SSM_T):
        for j in range(N_LANE_TILES):
            y_s[j, pl.ds(t, CHUNKS_PER_TILE, stride=SSM_T), :] = y_ref[j, :, t * LANES:(t + 1) * LANES]
    x = x_ref[...]
    ms = jnp.mean(x * x, axis=-1, keepdims=True)
    h = (x * lax.rsqrt(ms + EPS)) * g_ref[...]
    c = math.sqrt(2.0 / math.pi)
    for j in range(N_LANE_TILES):
        cols = slice(j * LANES, (j + 1) * LANES)
        y = y_s[j] + d_ref[:, cols] * h[:, cols]
        gelu = 0.5 * y * (1.0 + jnp.tanh(c * (y + 0.044715 * (y * y * y))))
        o_ref[:, cols] = gelu.astype(BF16)


def _ssm_post(y_c, x, g, d):
    spec = pl.BlockSpec((BM, D_MODEL), lambda m: (m, 0))
    vec = pl.BlockSpec((1, D_MODEL), lambda m: (0, 0))
    return pl.pallas_call(
        _ssm_post_kernel,
        out_shape=jax.ShapeDtypeStruct((M_TOTAL, D_MODEL), BF16),
        grid=(N_ROW_TILES,),
        in_specs=[pl.BlockSpec((N_LANE_TILES, CHUNKS_PER_TILE, SSM_CHUNK_COLS), lambda m: (0, m, 0)),
                  spec, vec, vec],
        out_specs=spec,
        scratch_shapes=[pltpu.VMEM((N_LANE_TILES, BM, LANES), F32)],
        compiler_params=_params(1),
        name="ssm_post",
    )(y_c, x, g.reshape(1, D_MODEL), d.reshape(1, D_MODEL))


def kernel(x_prompt, x_sample, state_ret, state_ssm_re, state_ssm_im, cache_conv, norm_mix, norm_ffn, norm_final, ret_w_in, ret_gn, ret_w_out, ssm_a_re, ssm_a_im, ssm_log_dt, ssm_b_re, ssm_b_im, ssm_c_re, ssm_c_im, ssm_d, ssm_w_glu, ffn_w_up, ffn_conv_w, ffn_conv_b, ffn_w_down):
    x = (x_prompt.reshape(M_PROMPT, D_MODEL), x_sample.reshape(M_SAMPLE, D_MODEL))
    xb, ssq = _cast_ssq(*x)
    cos, sin = _rope_tables()

    ret_p, ret_s = [], []
    re_p, im_p, re_s, im_s = [], [], [], []
    conv_p, conv_s = [], []
    state_row = SSM_GROUPS * SSM_STATE
    for i in range(DEPTH):
        j = i // 2
        if i % 2 == 0:
            qk, vg = _mm_retin((xb, ssq, norm_mix[i]), ret_w_in, j, cos, sin)
            o_p, o_s, s_p, s_s = _retention(qk, vg, ret_gn[j], state_ret, j)
            ret_p.append(s_p)
            ret_s.append(s_s)
            x, xb, ssq = _mm_res((o_p, o_s), ret_w_out, j, x, emit=True)
        else:
            h_c = _rmsnorm_chunked(x, norm_mix[i])
            bc, cc, lag, a16r, a16i = _ssm_prep(ssm_a_re[j], ssm_a_im[j], ssm_log_dt[j],
                                               ssm_b_re[j], ssm_b_im[j], ssm_c_re[j], ssm_c_im[j])
            y_c, fpr, fpi, fsr, fsi = _ssm_main(
                h_c, lag, bc, cc, a16r, a16i,
                state_ssm_re[j].reshape(DEC_BATCH, state_row),
                state_ssm_im[j].reshape(DEC_BATCH, state_row))
            re_p.append(fpr[0:1].reshape(1, SSM_GROUPS, SSM_STATE))
            im_p.append(fpi[0:1].reshape(1, SSM_GROUPS, SSM_STATE))
            re_s.append(fsr.reshape(DEC_BATCH, SSM_GROUPS, SSM_STATE))
            im_s.append(fsi.reshape(DEC_BATCH, SSM_GROUPS, SSM_STATE))
            gl = _ssm_post(y_c, x, norm_mix[i], ssm_d[j])
            x, xb, ssq = _mm_glu(gl, ssm_w_glu, j, x)

        u, tail_p, tail_s = _mm_ffnup((xb, ssq, norm_ffn[i]), ffn_w_up, i, ffn_conv_w[i],
                                      ffn_conv_b[i], cache_conv[i])
        conv_p.append(tail_p[SUBLANES - (CONV_W - 1):].reshape(1, CONV_W - 1, FFN_DIM))
        conv_s.append(tail_s.reshape(DEC_BATCH, SUBLANES, FFN_DIM)[:, SUBLANES - (CONV_W - 1):])
        if i + 1 < DEPTH and (i + 1) % 2 == 0:
            x, xb, ssq = _mm_res(u, ffn_w_down, i, x, emit=True)
        else:
            x = _mm_res(u, ffn_w_down, i, x)

    y_prompt = _rmsnorm(x, norm_final, F32, 0, N_PROMPT_TILES).reshape(1, SEQ, D_MODEL)
    y_sample = _rmsnorm(x, norm_final, F32, N_PROMPT_TILES, 1).reshape(DEC_BATCH, DEC_SEQ, D_MODEL)
    return (y_prompt, y_sample, jnp.stack(ret_p), jnp.stack(ret_s),
            jnp.stack(re_p), jnp.stack(im_p), jnp.stack(re_s), jnp.stack(im_s),
            jnp.stack(conv_p), jnp.stack(conv_s))
```

```python
import functools
import math

import numpy as np
import jax
import jax.numpy as jnp
from jax import lax
from jax.experimental import pallas as pl
from jax.experimental.pallas import tpu as pltpu

F32 = jnp.float32
BF16 = jnp.bfloat16

D_MODEL = 2048
SEQ = 8192
DEPTH = 4
DEC_BATCH = 8
DEC_SEQ = 64
PAST_LEN = 1024
CHUNK = 64
RET_HEADS = 8
RET_DK = D_MODEL // RET_HEADS
RET_DV = 2 * RET_DK
RET_QK = RET_HEADS * RET_DK
RET_VD = RET_HEADS * RET_DV
ROPE_BASE = 10000.0
SSM_GROUP = 16
SSM_GROUPS = D_MODEL // SSM_GROUP
SSM_STATE = 64
FFN_DIM = 2 * D_MODEL
CONV_W = 3
EPS = 1e-6

M_PROMPT = SEQ
M_SAMPLE = DEC_BATCH * DEC_SEQ
M_TOTAL = M_PROMPT + M_SAMPLE

LANES = 128
SUBLANES = 8
VMEM_LIMIT_BYTES = 56 * 1024 * 1024

BM = M_SAMPLE
N_ROW_TILES = M_TOTAL // BM
N_PROMPT_TILES = M_PROMPT // BM

RET_T = 256
RET_SUB = 4

SSM_T = 16
SSM_PAIRS = SSM_GROUPS // 2
SSM_PB = 4
SSM_SEGS = 8
SSM_STEPS_P = M_PROMPT // SSM_T // SSM_SEGS
SSM_STEPS_S = DEC_SEQ // SSM_T
SSM_NC = (SSM_STEPS_P + SSM_STEPS_S) * SSM_SEGS


def _params(n_axes):
    return pltpu.CompilerParams(dimension_semantics=("arbitrary",) * n_axes,
                                vmem_limit_bytes=VMEM_LIMIT_BYTES)


def _sigmoid(x):
    return 1.0 / (1.0 + jnp.exp(-x))


def _rope_kernel(cos_ref, sin_ref):
    m = pl.program_id(0)
    half = RET_DK // 2
    freq = lax.broadcasted_iota(jnp.int32, (1, half), 1).astype(F32)
    inv = ROPE_BASE ** (-freq / half)
    r = m * BM + lax.broadcasted_iota(jnp.int32, (BM, half), 0)
    pos = jnp.where(r < M_PROMPT, r, PAST_LEN + ((r - M_PROMPT) & (DEC_SEQ - 1)))
    ang = pos.astype(F32) * inv
    cos_ref[...] = jnp.cos(ang)
    sin_ref[...] = jnp.sin(ang)


def _rope_tables():
    half = RET_DK // 2
    spec = pl.BlockSpec((BM, half), lambda m: (m, 0))
    return pl.pallas_call(
        _rope_kernel,
        out_shape=(jax.ShapeDtypeStruct((M_TOTAL, half), F32),) * 2,
        grid=(N_ROW_TILES,),
        out_specs=(spec, spec),
        compiler_params=_params(1),
        name="rope_tables",
    )()


def _norm_kernel(x_ref, g_ref, o_ref):
    x = x_ref[...]
    ms = jnp.mean(x * x, axis=-1, keepdims=True)
    o_ref[...] = ((x * lax.rsqrt(ms + EPS)) * g_ref[...]).astype(o_ref.dtype)


def _rmsnorm(x, g, out_dtype, tile0=0, n_tiles=N_ROW_TILES):
    return pl.pallas_call(
        _norm_kernel,
        out_shape=jax.ShapeDtypeStruct((n_tiles * BM, D_MODEL), out_dtype),
        grid=(n_tiles,),
        in_specs=[pl.BlockSpec((BM, D_MODEL), lambda m: (tile0 + m, 0)),
                  pl.BlockSpec((1, D_MODEL), lambda m: (0, 0))],
        out_specs=pl.BlockSpec((BM, D_MODEL), lambda m: (m, 0)),
        compiler_params=_params(1),
        name="rmsnorm",
    )(x, g.reshape(1, D_MODEL))


CHUNKS_PER_TILE = BM // 16
N_CHUNKS = M_TOTAL // 16
N_LANE_TILES = D_MODEL // LANES


def _norm_chunked_kernel(x_ref, g_ref, o_ref, h_s):
    x = x_ref[...]
    ms = jnp.mean(x * x, axis=-1, keepdims=True)
    h = (x * lax.rsqrt(ms + EPS)) * g_ref[...]
    for j in range(N_LANE_TILES):
        h_s[j] = h[:, j * LANES:(j + 1) * LANES]
    pair_of_lane = lax.broadcasted_iota(jnp.int32, (CHUNKS_PER_TILE, LANES), 1) // SSM_PAIR_CH
    for j in range(N_LANE_TILES):
        for tg in range(SSM_T // SSM_PB):
            rows = [h_s[j, pl.ds(tg * SSM_PB + u, CHUNKS_PER_TILE, stride=SSM_T), :] for u in range(SSM_PB)]
            for q in range(SSM_PB):
                out = None
                for u in range(SSM_PB):
                    shift = ((u - q) * SSM_PAIR_CH) % LANES
                    moved = rows[u] if shift == 0 else pltpu.roll(rows[u], shift, axis=1)
                    out = moved if out is None else jnp.where(pair_of_lane == u, moved, out)
                c0 = q * SSM_PAIR_COLS + tg * LANES
                o_ref[j, :, c0:c0 + LANES] = out.astype(BF16)


def _rmsnorm_chunked(x, g):
    return pl.pallas_call(
        _norm_chunked_kernel,
        out_shape=jax.ShapeDtypeStruct((N_LANE_TILES, N_CHUNKS, SSM_T * LANES), BF16),
        grid=(N_ROW_TILES,),
        in_specs=[pl.BlockSpec((BM, D_MODEL), lambda m: (m, 0)),
                  pl.BlockSpec((1, D_MODEL), lambda m: (0, 0))],
        out_specs=pl.BlockSpec((N_LANE_TILES, CHUNKS_PER_TILE, SSM_T * LANES), lambda m: (0, m, 0)),
        scratch_shapes=[pltpu.VMEM((N_LANE_TILES, BM, LANES), F32)],
        compiler_params=_params(1),
        name="rmsnorm_chunked",
    )(x, g.reshape(1, D_MODEL))


def _weight_spec(k, bn, layer, col0=0):
    return pl.BlockSpec((None, k, bn), lambda n, m: (layer, 0, col0 + n), pipeline_mode=pl.Buffered(1))


RET_IN_BN = 2048


def _emit_norm_inputs(x_new, xb_ref, ssq_ref):
    xb_ref[...] = x_new.astype(BF16)
    sq = x_new * x_new
    part = sq[:, 0:LANES]
    for t in range(1, x_new.shape[1] // LANES):
        part = part + sq[:, t * LANES:(t + 1) * LANES]
    ssq_ref[...] = part


def _row_scale(ssq_ref):
    total = jnp.sum(jnp.sum(ssq_ref[...], axis=0), axis=-1, keepdims=True)
    return lax.rsqrt(total * (1.0 / D_MODEL) + EPS)


def _norm_operands(norm):
    xb, ssq, g = norm
    args = [xb, ssq, g.reshape(D_MODEL, 1)]
    specs = [pl.BlockSpec((BM, D_MODEL), lambda n, m: (m, 0)),
             pl.BlockSpec((ssq.shape[0], BM, LANES), lambda n, m: (0, m, 0)),
             pl.BlockSpec((D_MODEL, 1), lambda n, m: (0, 0))]
    return args, specs


def _norm_outputs(n_out, bn):
    shapes = [jax.ShapeDtypeStruct((M_TOTAL, n_out), BF16),
              jax.ShapeDtypeStruct((n_out // bn, M_TOTAL, LANES), F32)]
    specs = [pl.BlockSpec((BM, bn), lambda n, m: (m, n)),
             pl.BlockSpec((None, BM, LANES), lambda n, m: (n, m, 0))]
    return shapes, specs


def _cast_ssq_kernel(xp_ref, xs_ref, xb_ref, ssq_ref):
    x = jnp.where(pl.program_id(0) < N_PROMPT_TILES, xp_ref[...], xs_ref[...])
    _emit_norm_inputs(x, xb_ref, ssq_ref)


def _cast_ssq(x_prompt, x_sample):
    return pl.pallas_call(
        _cast_ssq_kernel,
        out_shape=(jax.ShapeDtypeStruct((M_TOTAL, D_MODEL), BF16),
                   jax.ShapeDtypeStruct((1, M_TOTAL, LANES), F32)),
        grid=(N_ROW_TILES,),
        in_specs=[pl.BlockSpec((BM, D_MODEL), lambda m: (jnp.minimum(m, N_PROMPT_TILES - 1), 0)),
                  pl.BlockSpec((BM, D_MODEL), lambda m: (0, 0))],
        out_specs=(pl.BlockSpec((BM, D_MODEL), lambda m: (m, 0)),
                   pl.BlockSpec((None, BM, LANES), lambda m: (0, m, 0))),
        compiler_params=_params(1),
        name="cast_ssq",
    )(x_prompt, x_sample)


def _mm_qk_kernel(a_ref, ssq_ref, g_ref, w_ref, cos_ref, sin_ref, o_ref, wb_ref):
    n = pl.program_id(0)
    n_q = RET_QK // RET_IN_BN

    @pl.when(pl.program_id(1) == 0)
    def _():
        wb_ref[...] = (w_ref[...] * g_ref[...]).astype(BF16)

    acc = jnp.dot(a_ref[...], wb_ref[...], preferred_element_type=F32) * _row_scale(ssq_ref)
    scale = jnp.where(n >= n_q, RET_DK ** -0.5, 1.0).astype(F32)
    c = cos_ref[...]
    s = sin_ref[...]
    half = RET_DK // 2
    for hh in range(RET_IN_BN // RET_DK):
        lo = hh * RET_DK
        x1 = acc[:, lo:lo + half]
        x2 = acc[:, lo + half:lo + RET_DK]
        o_ref[:, lo:lo + half] = ((x1 * c - x2 * s) * scale).astype(BF16)
        o_ref[:, lo + half:lo + RET_DK] = ((x1 * s + x2 * c) * scale).astype(BF16)


def _mm_cast_kernel(a_ref, ssq_ref, g_ref, w_ref, o_ref, wb_ref):
    @pl.when(pl.program_id(1) == 0)
    def _():
        wb_ref[...] = (w_ref[...] * g_ref[...]).astype(BF16)

    acc = jnp.dot(a_ref[...], wb_ref[...], preferred_element_type=F32) * _row_scale(ssq_ref)
    o_ref[...] = acc.astype(BF16)


def _mm_retin(norm, w, layer, cos, sin):
    bn = RET_IN_BN
    _, k, n_out = w.shape
    half = RET_DK // 2
    n_qk = 2 * RET_QK // bn
    norm_args, norm_specs = _norm_operands(norm)
    out_spec = pl.BlockSpec((BM, bn), lambda n, m: (m, n))
    rope_spec = pl.BlockSpec((BM, half), lambda n, m: (m, 0))
    qk = pl.pallas_call(
        _mm_qk_kernel,
        out_shape=jax.ShapeDtypeStruct((M_TOTAL, 2 * RET_QK), BF16),
        grid=(n_qk, N_ROW_TILES),
        in_specs=norm_specs + [_weight_spec(k, bn, layer), rope_spec, rope_spec],
        out_specs=out_spec,
        scratch_shapes=[pltpu.VMEM((k, bn), BF16)],
        compiler_params=_params(2),
        name="mm_ret_qk",
    )(*norm_args, w, cos, sin)
    vg = pl.pallas_call(
        _mm_cast_kernel,
        out_shape=jax.ShapeDtypeStruct((M_TOTAL, 2 * RET_VD), BF16),
        grid=(n_out // bn - n_qk, N_ROW_TILES),
        in_specs=norm_specs + [_weight_spec(k, bn, layer, n_qk)],
        out_specs=out_spec,
        scratch_shapes=[pltpu.VMEM((k, bn), BF16)],
        compiler_params=_params(2),
        name="mm_ret_vg",
    )(*norm_args, w)
    return qk, vg


def _row_split_specs(cols, col_index):
    return [pl.BlockSpec((BM, cols), lambda n, m: (jnp.minimum(m, N_PROMPT_TILES - 1), col_index(n))),
            pl.BlockSpec((BM, cols), lambda n, m: (0, col_index(n)))]


def _mm_res_kernel(*refs, split_a, split_x, emit):
    refs = list(refs)
    ap_ref = refs.pop(0)
    as_ref = refs.pop(0) if split_a else None
    w_ref = refs.pop(0)
    xp_ref = refs.pop(0)
    xs_ref = refs.pop(0) if split_x else None
    o_ref = refs.pop(0)
    xb_ref, ssq_ref = (refs.pop(0), refs.pop(0)) if emit else (None, None)
    (wb_ref,) = refs
    m = pl.program_id(1)
    on_prompt = m < N_PROMPT_TILES

    @pl.when(m == 0)
    def _():
        wb_ref[...] = w_ref[...].astype(BF16)

    def body(a_ref):
        x = xp_ref[...]
        if split_x:
            x = jnp.where(on_prompt, x, xs_ref[...])
        x_new = x + jnp.dot(a_ref[...], wb_ref[...], preferred_element_type=F32)
        o_ref[...] = x_new
        if emit:
            _emit_norm_inputs(x_new, xb_ref, ssq_ref)

    if split_a:
        pl.when(on_prompt)(lambda: body(ap_ref))
        pl.when(jnp.logical_not(on_prompt))(lambda: body(as_ref))
    else:
        body(ap_ref)


def _mm_res(a, w, layer, x, emit=False):
    bn = 1024
    _, k, n_out = w.shape
    split_a = isinstance(a, tuple)
    split_x = isinstance(x, tuple)
    a_specs = _row_split_specs(k, lambda n: 0) if split_a else [pl.BlockSpec((BM, k), lambda n, m: (m, 0))]
    x_specs = _row_split_specs(bn, lambda n: n) if split_x else [pl.BlockSpec((BM, bn), lambda n, m: (m, n))]
    a_args = list(a) if split_a else [a]
    x_args = list(x) if split_x else [x]
    out_shapes = [jax.ShapeDtypeStruct((M_TOTAL, n_out), F32)]
    out_specs = [pl.BlockSpec((BM, bn), lambda n, m: (m, n))]
    if emit:
        shapes, specs = _norm_outputs(n_out, bn)
        out_shapes += shapes
        out_specs += specs
    out = pl.pallas_call(
        functools.partial(_mm_res_kernel, split_a=split_a, split_x=split_x, emit=emit),
        out_shape=tuple(out_shapes),
        grid=(n_out // bn, N_ROW_TILES),
        in_specs=a_specs + [_weight_spec(k, bn, layer)] + x_specs,
        out_specs=tuple(out_specs),
        scratch_shapes=[pltpu.VMEM((k, bn), BF16)],
        compiler_params=_params(2),
        name="mm_residual",
    )(*a_args, w, *x_args)
    return out if emit else out[0]


def _mm_glu_kernel(a_ref, wa_ref, wg_ref, x_ref, o_ref, xb_ref, ssq_ref, wab_ref, wgb_ref):
    @pl.when(pl.program_id(1) == 0)
    def _():
        wab_ref[...] = wa_ref[...].astype(BF16)
        wgb_ref[...] = wg_ref[...].astype(BF16)

    a = a_ref[...]
    ga = jnp.dot(a, wab_ref[...], preferred_element_type=F32)
    gb = jnp.dot(a, wgb_ref[...], preferred_element_type=F32)
    x_new = x_ref[...] + ga * _sigmoid(gb)
    o_ref[...] = x_new
    _emit_norm_inputs(x_new, xb_ref, ssq_ref)


def _mm_glu(a, w, layer, x):
    bn = 1024
    _, k, n2 = w.shape
    n_out = n2 // 2
    nb = n_out // bn
    norm_shapes, norm_specs = _norm_outputs(n_out, bn)
    return pl.pallas_call(
        _mm_glu_kernel,
        out_shape=(jax.ShapeDtypeStruct((M_TOTAL, n_out), F32), *norm_shapes),
        grid=(nb, N_ROW_TILES),
        in_specs=[
            pl.BlockSpec((BM, k), lambda n, m: (m, 0)),
            _weight_spec(k, bn, layer),
            _weight_spec(k, bn, layer, nb),
            pl.BlockSpec((BM, bn), lambda n, m: (m, n)),
        ],
        out_specs=(pl.BlockSpec((BM, bn), lambda n, m: (m, n)), *norm_specs),
        scratch_shapes=[pltpu.VMEM((k, bn), BF16), pltpu.VMEM((k, bn), BF16)],
        compiler_params=_params(2),
        name="mm_glu",
    )(a, w, w, x)


def _mm_ffnup_kernel(h_ref, ssq_ref, g_ref, wa_ref, wg_ref, cw_ref, cb_ref, cache_ref,
                     o_ref, tp_ref, ts_ref, wab_ref, wgb_ref, carry_ref):
    m = pl.program_id(1)

    @pl.when(m == 0)
    def _():
        g = g_ref[...]
        wab_ref[...] = (wa_ref[...] * g).astype(BF16)
        wgb_ref[...] = (wg_ref[...] * g).astype(BF16)
        carry_ref[...] = jnp.zeros_like(carry_ref)

    h = h_ref[...]
    r = _row_scale(ssq_ref)
    a = jnp.dot(h, wab_ref[...], preferred_element_type=F32) * r
    b = jnp.dot(h, wgb_ref[...], preferred_element_type=F32) * r
    r1 = pltpu.roll(a, 1, axis=0)
    r2 = pltpu.roll(a, 2, axis=0)
    cb = cb_ref[...]
    w0 = cw_ref[0:1, :]
    w1 = cw_ref[1:2, :]
    w2 = cw_ref[2:3, :]

    def gated(a_rows, prev1, prev2, b_rows):
        conv = cb + w0 * prev2 + w1 * prev1 + w2 * a_rows
        return ((conv * _sigmoid(conv)) * b_rows).astype(BF16)

    o_ref[...] = gated(a, r1, r2, b)

    row8 = lax.broadcasted_iota(jnp.int32, (SUBLANES, 1), 0)
    heads = [tuple(z[s * DEC_SEQ:s * DEC_SEQ + SUBLANES] for z in (a, r1, r2, b))
             for s in range(DEC_BATCH)]
    tails = [a[(s + 1) * DEC_SEQ - SUBLANES:(s + 1) * DEC_SEQ] for s in range(DEC_BATCH)]

    def redo_head(s, before1, before2):
        a8, r1_8, r2_8, b8 = heads[s]
        prev1 = jnp.where(row8 == 0, before1, r1_8)
        prev2 = jnp.where(row8 == 0, before2, jnp.where(row8 == 1, before1, r2_8))
        o_ref[s * DEC_SEQ:s * DEC_SEQ + SUBLANES, :] = gated(a8, prev1, prev2, b8)

    @pl.when(m < N_PROMPT_TILES)
    def _():
        c = carry_ref[...]
        redo_head(0, c[SUBLANES - 1:SUBLANES, :], c[SUBLANES - 2:SUBLANES - 1, :])
        carry_ref[...] = tails[DEC_BATCH - 1]

        @pl.when(m == N_PROMPT_TILES - 1)
        def _():
            tp_ref[...] = tails[DEC_BATCH - 1]

    @pl.when(m == N_PROMPT_TILES)
    def _():
        for s in range(DEC_BATCH):
            c = cache_ref[s]
            redo_head(s, c[1:2, :], c[0:1, :])
            ts_ref[s * SUBLANES:(s + 1) * SUBLANES, :] = tails[s]


def _mm_ffnup(norm, w, layer, conv_w, conv_b, cache):
    bn = 1024
    k = w.shape[1]
    nb = FFN_DIM // bn
    norm_args, norm_specs = _norm_operands(norm)
    return pl.pallas_call(
        _mm_ffnup_kernel,
        out_shape=(
            jax.ShapeDtypeStruct((M_TOTAL, FFN_DIM), BF16),
            jax.ShapeDtypeStruct((SUBLANES, FFN_DIM), F32),
            jax.ShapeDtypeStruct((DEC_BATCH * SUBLANES, FFN_DIM), F32),
        ),
        grid=(nb, N_ROW_TILES),
        in_specs=norm_specs + [
            _weight_spec(k, bn, layer),
            _weight_spec(k, bn, layer, nb),
            pl.BlockSpec((CONV_W, bn), lambda n, m: (0, n)),
            pl.BlockSpec((1, bn), lambda n, m: (0, n)),
            pl.BlockSpec((DEC_BATCH, CONV_W - 1, bn), lambda n, m: (0, 0, n)),
        ],
        out_specs=(
            pl.BlockSpec((BM, bn), lambda n, m: (m, n)),
            pl.BlockSpec((SUBLANES, bn), lambda n, m: (0, n)),
            pl.BlockSpec((DEC_BATCH * SUBLANES, bn), lambda n, m: (0, n)),
        ),
        scratch_shapes=[pltpu.VMEM((k, bn), BF16), pltpu.VMEM((k, bn), BF16),
                        pltpu.VMEM((SUBLANES, bn), F32)],
        compiler_params=_params(2),
        name="mm_ffn_up",
    )(*norm_args, w, w, conv_w, conv_b.reshape(1, FFN_DIM), cache)


def _ret_tables(lg, T):
    i = lax.broadcasted_iota(jnp.int32, (T, T), 0)
    j = lax.broadcasted_iota(jnp.int32, (T, T), 1)
    dist = jnp.abs(i - j).astype(F32)
    shift = int(math.log2(CHUNK))
    visible = (j >> shift) <= (i >> shift)
    mask = jnp.where(visible, jnp.exp(dist * lg), 0.0)
    t = lax.broadcasted_iota(jnp.int32, (T, 1), 0).astype(F32)
    cross = jnp.exp((t + 1.0) * lg)
    k_dec = jnp.exp((T - 1.0 - t) * lg)
    decay = jnp.exp(float(T) * lg)
    return mask, cross, k_dec, decay


def _ret_block(q, k, v, g, state, tables, gn):
    mask, cross, k_dec, decay = tables
    scores = lax.dot_general(q, k, (((1,), (1,)), ((), ())), preferred_element_type=F32)
    scores = scores * mask
    out = jnp.dot(scores.astype(BF16), v, preferred_element_type=F32)
    out = out + jnp.dot(q, state.astype(BF16), preferred_element_type=F32) * cross

    kd = (k.astype(F32) * k_dec).astype(BF16)
    new_state = decay * state + lax.dot_general(
        kd, v, (((0,), (0,)), ((), ())), preferred_element_type=F32)

    mu = jnp.mean(out, axis=-1, keepdims=True)
    oc = out - mu
    var = jnp.mean(oc * oc, axis=-1, keepdims=True)
    normed = (oc * lax.rsqrt(var + EPS)) * gn
    g = g.astype(F32)
    return (normed * (g * _sigmoid(g))).astype(BF16), new_state


def _ret_prompt_kernel(lg_ref, q_ref, k_ref, v_ref, g_ref, gn_ref, o_ref, sl_ref, state_ref, mask_ref):
    blk = pl.program_id(1)
    lg = lg_ref[...][:, 0:1]

    @pl.when(blk == 0)
    def _():
        state_ref[...] = jnp.zeros_like(state_ref)
        mask_ref[...] = _ret_tables(lg, RET_T)[0]

    _, cross, k_dec, decay = _ret_tables(lg, RET_T)
    tables = (mask_ref[...], cross, k_dec, decay)
    gn = gn_ref[...]
    state = state_ref[...]
    for sb in range(RET_SUB):
        rows = slice(sb * RET_T, (sb + 1) * RET_T)
        o_ref[rows, :], state = _ret_block(q_ref[rows, :], k_ref[rows, :], v_ref[rows, :],
                                           g_ref[rows, :], state, tables, gn)
    state_ref[...] = state

    @pl.when(blk == pl.num_programs(1) - 1)
    def _():
        sl_ref[...] = state


def _ret_sample_kernel(lg_ref, q_ref, k_ref, v_ref, g_ref, gn_ref, s0_ref, o_ref, sl_ref):
    lg = lg_ref[...][:, 0:1]
    tables = _ret_tables(lg, DEC_SEQ)
    gn = gn_ref[...]
    for s in range(DEC_BATCH):
        rows = slice(s * DEC_SEQ, (s + 1) * DEC_SEQ)
        o_ref[rows, :], sl_ref[s] = _ret_block(q_ref[rows, :], k_ref[rows, :], v_ref[rows, :],
                                               g_ref[rows, :], s0_ref[s], tables, gn)


def _retention(qk, vg, gn, s0_stack, layer):
    log_g = np.log1p(-np.exp2(-5.0 - np.arange(RET_HEADS, dtype=np.float32))).astype(np.float32)
    lg = jnp.asarray(np.broadcast_to(log_g[:, None, None], (RET_HEADS, 1, LANES)).copy())
    gn = gn.reshape(1, RET_VD)
    k_off = RET_QK // RET_DK
    g_off = RET_VD // RET_DV
    rows_p = RET_T * RET_SUB
    o_p, s_p = pl.pallas_call(
        _ret_prompt_kernel,
        out_shape=(jax.ShapeDtypeStruct((M_PROMPT, RET_VD), BF16),
                   jax.ShapeDtypeStruct((1, RET_HEADS, RET_DK, RET_DV), F32)),
        grid=(RET_HEADS, M_PROMPT // rows_p),
        in_specs=[
            pl.BlockSpec((None, 1, LANES), lambda h, b: (h, 0, 0)),
            pl.BlockSpec((rows_p, RET_DK), lambda h, b: (b, h)),
            pl.BlockSpec((rows_p, RET_DK), lambda h, b: (b, k_off + h)),
            pl.BlockSpec((rows_p, RET_DV), lambda h, b: (b, h)),
            pl.BlockSpec((rows_p, RET_DV), lambda h, b: (b, g_off + h)),
            pl.BlockSpec((1, RET_DV), lambda h, b: (0, h)),
        ],
        out_specs=(pl.BlockSpec((rows_p, RET_DV), lambda h, b: (b, h)),
                   pl.BlockSpec((None, None, RET_DK, RET_DV), lambda h, b: (0, h, 0, 0))),
        scratch_shapes=[pltpu.VMEM((RET_DK, RET_DV), F32), pltpu.VMEM((RET_T, RET_T), F32)],
        compiler_params=_params(2),
        name="retention",
    )(lg, qk, qk, vg, vg, gn)
    sample_block = M_PROMPT // M_SAMPLE
    o_s, s_s = pl.pallas_call(
        _ret_sample_kernel,
        out_shape=(jax.ShapeDtypeStruct((M_SAMPLE, RET_VD), BF16),
                   jax.ShapeDtypeStruct((DEC_BATCH, RET_HEADS, RET_DK, RET_DV), F32)),
        grid=(RET_HEADS,),
        in_specs=[
            pl.BlockSpec((None, 1, LANES), lambda h: (h, 0, 0)),
            pl.BlockSpec((M_SAMPLE, RET_DK), lambda h: (sample_block, h)),
            pl.BlockSpec((M_SAMPLE, RET_DK), lambda h: (sample_block, k_off + h)),
            pl.BlockSpec((M_SAMPLE, RET_DV), lambda h: (sample_block, h)),
            pl.BlockSpec((M_SAMPLE, RET_DV), lambda h: (sample_block, g_off + h)),
            pl.BlockSpec((1, RET_DV), lambda h: (0, h)),
            pl.BlockSpec((None, DEC_BATCH, None, RET_DK, RET_DV), lambda h: (layer, 0, h, 0, 0)),
        ],
        out_specs=(pl.BlockSpec((M_SAMPLE, RET_DV), lambda h: (0, h)),
                   pl.BlockSpec((DEC_BATCH, None, RET_DK, RET_DV), lambda h: (0, h, 0, 0))),
        compiler_params=_params(1),
        name="retention_sample",
    )(lg, qk, qk, vg, vg, gn, s0_stack)
    return o_p, o_s, s_p, s_s


def _ssm_prep_kernel(ar_ref, ai_ref, ldt_ref, br_ref, bi_ref, cr_ref, ci_ref,
                     bc_ref, cc_ref, w_ref, a16r_ref, a16i_ref, vr_s, vi_s):
    ar = ar_ref[...]
    ai = ai_ref[...]
    dt = jnp.exp(ldt_ref[...])
    mag = jnp.exp(ar * dt)
    ang = ai * dt
    abr = mag * jnp.cos(ang)
    abi = mag * jnp.sin(ang)
    den = ar * ar + ai * ai
    nr = abr - 1.0
    ni = abi
    cfr = (nr * ar + ni * ai) / den
    cfi = (ni * ar - nr * ai) / den

    def b4(z):
        return z[:, None]

    br = br_ref[...]
    bi = bi_ref[...]
    bbr = b4(cfr) * br - b4(cfi) * bi
    bbi = b4(cfr) * bi + b4(cfi) * br
    cre = cr_ref[...]
    cim = ci_ref[...]

    powers = []
    pr = jnp.ones_like(ar)
    pi = jnp.zeros_like(ar)
    for _ in range(SSM_T + 1):
        powers.append((pr, pi))
        pr, pi = pr * abr - pi * abi, pr * abi + pi * abr
    a16r_ref[...] = powers[SSM_T][0]
    a16i_ref[...] = powers[SSM_T][1]

    for m in range(SSM_T + 1):
        pr, pi = powers[m]
        vr = b4(pr) * cre - b4(pi) * cim
        vi = b4(pr) * cim + b4(pi) * cre
        if m < SSM_T:
            vr_s[:, m] = vr
            vi_s[:, m] = vi
        if m >= 1:
            cc_ref[:, m - 1, :, :, 0:LANES] = vr.astype(BF16)
            cc_ref[:, m - 1, :, :, LANES:2 * LANES] = (-vi).astype(BF16)

    for t in range(SSM_T):
        pr, pi = powers[SSM_T - 1 - t]
        bc_ref[:, t, :, :, 0:LANES] = (b4(pr) * bbr - b4(pi) * bbi).astype(BF16)
        bc_ref[:, t, :, :, LANES:2 * LANES] = (b4(pr) * bbi + b4(pi) * bbr).astype(BF16)

    lane = lax.broadcasted_iota(jnp.int32, (SSM_PAIR_CH, SSM_PAIR_COLS), 1)
    contract_last = (((1,), (1,)), ((), ()))
    for q in range(SSM_PB):
        vr_all = vr_s[q].reshape(SSM_PAIR_COLS, LANES)
        vi_all = vi_s[q].reshape(SSM_PAIR_COLS, LANES)
        lags = (lax.dot_general(bbr[q].reshape(SSM_PAIR_CH, LANES), vr_all, contract_last,
                                precision=lax.Precision.HIGHEST, preferred_element_type=F32)
                - lax.dot_general(bbi[q].reshape(SSM_PAIR_CH, LANES), vi_all, contract_last,
                                  precision=lax.Precision.HIGHEST, preferred_element_type=F32))
        for t in range(SSM_T):
            shifted = lags if t == 0 else pltpu.roll(lags, SSM_PAIR_CH * t, axis=1)
            w_ref[q, t * SSM_PAIR_CH:(t + 1) * SSM_PAIR_CH, :] = jnp.where(
                lane >= SSM_PAIR_CH * t, shifted, 0.0).astype(BF16)


def _pair_pack(x):
    x = x.reshape(SSM_PAIRS, 2, SSM_GROUP, SSM_STATE)
    lo = jnp.pad(x[:, 0], ((0, 0), (0, 0), (0, SSM_STATE)))
    hi = jnp.pad(x[:, 1], ((0, 0), (0, 0), (SSM_STATE, 0)))
    return jnp.stack([lo, hi], axis=1)


def _ssm_prep(a_re, a_im, log_dt, b_re, b_im, c_re, c_im):
    assert 2 * SSM_PB * SSM_GROUP == LANES
    ar = a_re.reshape(SSM_PAIRS, 1, LANES)
    ai = a_im.reshape(SSM_PAIRS, 1, LANES)
    ldt = jnp.repeat(log_dt, SSM_STATE).reshape(SSM_PAIRS, 1, LANES)
    b2r = _pair_pack(jnp.transpose(b_re, (0, 2, 1)))
    b2i = _pair_pack(jnp.transpose(b_im, (0, 2, 1)))
    c2r = _pair_pack(c_re)
    c2i = _pair_pack(c_im)
    vec = pl.BlockSpec((SSM_PB, 1, LANES), lambda p: (p, 0, 0))
    mat = pl.BlockSpec((SSM_PB, 2, SSM_GROUP, LANES), lambda p: (p, 0, 0, 0))
    proj = pl.BlockSpec((SSM_PB, SSM_T, 2, SSM_GROUP, 2 * LANES), lambda p: (p, 0, 0, 0, 0))
    proj_shape = jax.ShapeDtypeStruct((SSM_PAIRS, SSM_T, 2, SSM_GROUP, 2 * LANES), BF16)
    bc, cc, w, a16r, a16i = pl.pallas_call(
        _ssm_prep_kernel,
        out_shape=(
            proj_shape, proj_shape,
            jax.ShapeDtypeStruct((SSM_PAIRS, SSM_PAIR_COLS, SSM_PAIR_COLS), BF16),
            jax.ShapeDtypeStruct((SSM_PAIRS, 1, LANES), F32),
            jax.ShapeDtypeStruct((SSM_PAIRS, 1, LANES), F32),
        ),
        grid=(N_LANE_TILES,),
        in_specs=[vec, vec, vec, mat, mat, mat, mat],
        out_specs=(proj, proj,
                   pl.BlockSpec((SSM_PB, SSM_PAIR_COLS, SSM_PAIR_COLS), lambda p: (p, 0, 0)),
                   vec, vec),
        scratch_shapes=[pltpu.VMEM((SSM_PB, SSM_T, 2, SSM_GROUP, LANES), F32)] * 2,
        compiler_params=_params(1),
        name="ssm_prep",
    )(ar, ai, ldt, b2r, b2i, c2r, c2i)
    bc = bc.reshape(SSM_PAIRS, SSM_PAIR_COLS, 2 * LANES)
    cc = cc.reshape(SSM_PAIRS, SSM_PAIR_COLS, 2 * LANES)
    row = SSM_GROUPS * SSM_STATE
    a16r = jnp.broadcast_to(a16r.reshape(1, row), (SSM_SEGS, row))
    a16i = jnp.broadcast_to(a16i.reshape(1, row), (SSM_SEGS, row))
    return bc, cc, w, a16r, a16i


SSM_CHUNK_COLS = SSM_T * LANES
SSM_STATE_COLS = SSM_PB * LANES
SSM_CHUNKS_P = M_PROMPT // SSM_T
SSM_PAIR_CH = 2 * SSM_GROUP
SSM_PAIR_COLS = SSM_T * SSM_PAIR_CH
SSM_SCAN_UNROLL = 2


def _ssm_main_kernel(x_ref, w_ref, bc_ref, cc_ref, ar_ref, ai_ref, h0r_ref, h0i_ref,
                     y_ref, fpr_ref, fpi_ref, fsr_ref, fsi_ref, d_s, s_s):
    pair_cols = [slice(q * SSM_PAIR_COLS, (q + 1) * SSM_PAIR_COLS) for q in range(SSM_PB)]

    for q in range(SSM_PB):
        d = jnp.dot(x_ref[:, pair_cols[q]], bc_ref[q], preferred_element_type=F32)
        d_s[q] = d[:, 0:LANES]
        d_s[SSM_PB + q] = d[:, LANES:2 * LANES]

    tiles = range(SSM_PB)
    ar = [ar_ref[:, k * LANES:(k + 1) * LANES] for k in tiles]
    ai = [ai_ref[:, k * LANES:(k + 1) * LANES] for k in tiles]

    def advance(sr, si, rows, record):
        out_r, out_i = [], []
        for k in tiles:
            if record:
                s_s[k, rows, :] = sr[k]
                s_s[SSM_PB + k, rows, :] = si[k]
            dr = d_s[k, rows, :]
            di = d_s[SSM_PB + k, rows, :]
            out_r.append(ar[k] * sr[k] - ai[k] * si[k] + dr)
            out_i.append(ar[k] * si[k] + ai[k] * sr[k] + di)
        return out_r, out_i

    def prompt_rows(i):
        return pl.ds(i, SSM_SEGS, stride=SSM_STEPS_P)

    def prompt_pass(start_r, start_i, record):
        def body(i, carry):
            out_r, out_i = advance(list(carry[0]), list(carry[1]), prompt_rows(i), record)
            return tuple(out_r), tuple(out_i)
        out_r, out_i = lax.fori_loop(0, SSM_STEPS_P, body, (tuple(start_r), tuple(start_i)),
                                     unroll=SSM_SCAN_UNROLL)
        return list(out_r), list(out_i)

    zeros = [jnp.zeros((SSM_SEGS, LANES), F32) for _ in tiles]
    er, ei = prompt_pass(zeros, zeros, record=False)

    sr, si = [], []
    for k in tiles:
        pr, pi = ar[k][0:1], ai[k][0:1]
        for _ in range(int(math.log2(SSM_STEPS_P))):
            pr, pi = pr * pr - pi * pi, 2.0 * pr * pi
        cr = jnp.zeros((1, LANES), F32)
        ci = cr
        starts_r, starts_i = [], []
        for s in range(SSM_SEGS):
            starts_r.append(cr)
            starts_i.append(ci)
            cr, ci = (pr * cr - pi * ci + er[k][s:s + 1], pr * ci + pi * cr + ei[k][s:s + 1])
        cols = slice(k * LANES, (k + 1) * LANES)
        fpr_ref[:, cols] = jnp.broadcast_to(cr, (SSM_SEGS, LANES))
        fpi_ref[:, cols] = jnp.broadcast_to(ci, (SSM_SEGS, LANES))
        sr.append(jnp.concatenate(starts_r, axis=0))
        si.append(jnp.concatenate(starts_i, axis=0))

    prompt_pass(sr, si, record=True)
    sr = [h0r_ref[:, k * LANES:(k + 1) * LANES] for k in tiles]
    si = [h0i_ref[:, k * LANES:(k + 1) * LANES] for k in tiles]
    for i in range(SSM_STEPS_S):
        sr, si = advance(sr, si, pl.ds(SSM_CHUNKS_P + i, DEC_BATCH, stride=SSM_STEPS_S), record=True)
    for k in tiles:
        fsr_ref[:, k * LANES:(k + 1) * LANES] = sr[k]
        fsi_ref[:, k * LANES:(k + 1) * LANES] = si[k]

    contract_last = (((1,), (1,)), ((), ()))
    for q in range(SSM_PB):
        states = jnp.concatenate([s_s[q].astype(BF16), s_s[SSM_PB + q].astype(BF16)], axis=1)
        carried = lax.dot_general(states, cc_ref[q], contract_last, preferred_element_type=F32)
        local = jnp.dot(x_ref[:, pair_cols[q]], w_ref[q], preferred_element_type=F32)
        y_ref[:, pair_cols[q]] = local + carried


def _ssm_main(x_c, w, bc, cc, a16r, a16i, h0r, h0i):
    row = SSM_GROUPS * SSM_STATE
    st = pl.BlockSpec((SSM_SEGS, SSM_STATE_COLS), lambda j: (0, j))
    st_shape = jax.ShapeDtypeStruct((SSM_SEGS, row), F32)
    proj = pl.BlockSpec((SSM_PB, SSM_PAIR_COLS, 2 * LANES), lambda j: (j, 0, 0))
    chunk_rows = pl.BlockSpec((None, N_CHUNKS, SSM_CHUNK_COLS), lambda j: (j, 0, 0))
    return pl.pallas_call(
        _ssm_main_kernel,
        out_shape=(jax.ShapeDtypeStruct((N_LANE_TILES, N_CHUNKS, SSM_CHUNK_COLS), F32),
                   st_shape, st_shape, st_shape, st_shape),
        grid=(N_LANE_TILES,),
        in_specs=[
            chunk_rows,
            pl.BlockSpec((SSM_PB, SSM_PAIR_COLS, SSM_PAIR_COLS), lambda j: (j, 0, 0)),
            proj, proj, st, st, st, st,
        ],
        out_specs=(chunk_rows, st, st, st, st),
        scratch_shapes=[
            pltpu.VMEM((2 * SSM_PB, N_CHUNKS, LANES), F32),
            pltpu.VMEM((2 * SSM_PB, N_CHUNKS, LANES), F32),
        ],
        compiler_params=_params(1),
        name="ssm_main",
    )(x_c, w, bc, cc, a16r, a16i, h0r, h0i)


def _ssm_post_kernel(y_ref, x_ref, g_ref, d_ref, o_ref, y_s):
    pair_of_lane = lax.broadcasted_iota(jnp.int32, (CHUNKS_PER_TILE, LANES), 1) // SSM_PAIR_CH
    for j in range(N_LANE_TILES):
        for tg in range(SSM_T // SSM_PB):
            pieces = [y_ref[j, :, q * SSM_PAIR_COLS + tg * LANES:q * SSM_PAIR_COLS + (tg + 1) * LANES]
                      for q in range(SSM_PB)]
            for u in range(SSM_PB):
                out = None
                for q in range(SSM_PB):
                    shift = ((q - u) * SSM_PAIR_CH) % LANES
                    moved = pieces[q] if shift == 0 else pltpu.roll(pieces[q], shift, axis=1)
                    out = moved if out is None else jnp.where(pair_of_lane == q, moved, out)
                y_s[j, pl.ds(tg * SSM_PB + u, CHUNKS_PER_TILE, stride=SSM_T), :] = out
    x = x_ref[...]
    ms = jnp.mean(x * x, axis=-1, keepdims=True)
    h = (x * lax.rsqrt(ms + EPS)) * g_ref[...]
    c = math.sqrt(2.0 / math.pi)
    for j in range(N_LANE_TILES):
        cols = slice(j * LANES, (j + 1) * LANES)
        y = y_s[j] + d_ref[:, cols] * h[:, cols]
        gelu = 0.5 * y * (1.0 + jnp.tanh(c * (y + 0.044715 * (y * y * y))))
        o_ref[:, cols] = gelu.astype(BF16)


def _ssm_post(y_c, x, g, d):
    spec = pl.BlockSpec((BM, D_MODEL), lambda m: (m, 0))
    vec = pl.BlockSpec((1, D_MODEL), lambda m: (0, 0))
    return pl.pallas_call(
        _ssm_post_kernel,
        out_shape=jax.ShapeDtypeStruct((M_TOTAL, D_MODEL), BF16),
        grid=(N_ROW_TILES,),
        in_specs=[pl.BlockSpec((N_LANE_TILES, CHUNKS_PER_TILE, SSM_CHUNK_COLS), lambda m: (0, m, 0)),
                  spec, vec, vec],
        out_specs=spec,
        scratch_shapes=[pltpu.VMEM((N_LANE_TILES, BM, LANES), F32)],
        compiler_params=_params(1),
        name="ssm_post",
    )(y_c, x, g.reshape(1, D_MODEL), d.reshape(1, D_MODEL))


def kernel(x_prompt, x_sample, state_ret, state_ssm_re, state_ssm_im, cache_conv, norm_mix, norm_ffn, norm_final, ret_w_in, ret_gn, ret_w_out, ssm_a_re, ssm_a_im, ssm_log_dt, ssm_b_re, ssm_b_im, ssm_c_re, ssm_c_im, ssm_d, ssm_w_glu, ffn_w_up, ffn_conv_w, ffn_conv_b, ffn_w_down):
    x = (x_prompt.reshape(M_PROMPT, D_MODEL), x_sample.reshape(M_SAMPLE, D_MODEL))
    xb, ssq = _cast_ssq(*x)
    cos, sin = _rope_tables()

    ret_p, ret_s = [], []
    re_p, im_p, re_s, im_s = [], [], [], []
    conv_p, conv_s = [], []
    state_row = SSM_GROUPS * SSM_STATE
    for i in range(DEPTH):
        j = i // 2
        if i % 2 == 0:
            qk, vg = _mm_retin((xb, ssq, norm_mix[i]), ret_w_in, j, cos, sin)
            o_p, o_s, s_p, s_s = _retention(qk, vg, ret_gn[j], state_ret, j)
            ret_p.append(s_p)
            ret_s.append(s_s)
            x, xb, ssq = _mm_res((o_p, o_s), ret_w_out, j, x, emit=True)
        else:
            h_c = _rmsnorm_chunked(x, norm_mix[i])
            bc, cc, w_ssm, a16r, a16i = _ssm_prep(ssm_a_re[j], ssm_a_im[j], ssm_log_dt[j],
                                               ssm_b_re[j], ssm_b_im[j], ssm_c_re[j], ssm_c_im[j])
            y_c, fpr, fpi, fsr, fsi = _ssm_main(
                h_c, w_ssm, bc, cc, a16r, a16i,
                state_ssm_re[j].reshape(DEC_BATCH, state_row),
                state_ssm_im[j].reshape(DEC_BATCH, state_row))
            re_p.append(fpr[0:1].reshape(1, SSM_GROUPS, SSM_STATE))
            im_p.append(fpi[0:1].reshape(1, SSM_GROUPS, SSM_STATE))
            re_s.append(fsr.reshape(DEC_BATCH, SSM_GROUPS, SSM_STATE))
            im_s.append(fsi.reshape(DEC_BATCH, SSM_GROUPS, SSM_STATE))
            gl = _ssm_post(y_c, x, norm_mix[i], ssm_d[j])
            x, xb, ssq = _mm_glu(gl, ssm_w_glu, j, x)

        u, tail_p, tail_s = _mm_ffnup((xb, ssq, norm_ffn[i]), ffn_w_up, i, ffn_conv_w[i],
                                      ffn_conv_b[i], cache_conv[i])
        conv_p.append(tail_p[SUBLANES - (CONV_W - 1):].reshape(1, CONV_W - 1, FFN_DIM))
        conv_s.append(tail_s.reshape(DEC_BATCH, SUBLANES, FFN_DIM)[:, SUBLANES - (CONV_W - 1):])
        if i + 1 < DEPTH and (i + 1) % 2 == 0:
            x, xb, ssq = _mm_res(u, ffn_w_down, i, x, emit=True)
        else:
            x = _mm_res(u, ffn_w_down, i, x)

    y_prompt = _rmsnorm(x, norm_final, F32, 0, N_PROMPT_TILES).reshape(1, SEQ, D_MODEL)
    y_sample = _rmsnorm(x, norm_final, F32, N_PROMPT_TILES, 1).reshape(DEC_BATCH, DEC_SEQ, D_MODEL)
    return (y_prompt, y_sample, jnp.stack(ret_p), jnp.stack(ret_s),
            jnp.stack(re_p), jnp.stack(im_p), jnp.stack(re_s), jnp.stack(im_s),
            jnp.stack(conv_p), jnp.stack(conv_s))
```

```python
import functools
import math

import numpy as np
import jax
import jax.numpy as jnp
from jax import lax
from jax.experimental import pallas as pl
from jax.experimental.pallas import tpu as pltpu

F32 = jnp.float32
BF16 = jnp.bfloat16

D_MODEL = 2048
SEQ = 8192
DEPTH = 4
DEC_BATCH = 8
DEC_SEQ = 64
PAST_LEN = 1024
CHUNK = 64
RET_HEADS = 8
RET_DK = D_MODEL // RET_HEADS
RET_DV = 2 * RET_DK
RET_QK = RET_HEADS * RET_DK
RET_VD = RET_HEADS * RET_DV
ROPE_BASE = 10000.0
SSM_GROUP = 16
SSM_GROUPS = D_MODEL // SSM_GROUP
SSM_STATE = 64
FFN_DIM = 2 * D_MODEL
CONV_W = 3
EPS = 1e-6

M_PROMPT = SEQ
M_SAMPLE = DEC_BATCH * DEC_SEQ
M_TOTAL = M_PROMPT + M_SAMPLE

LANES = 128
SUBLANES = 8
V7X_VMEM_BYTES = 64 * 1024 * 1024
VMEM_LIMIT_BYTES = V7X_VMEM_BYTES // 8 * 7

BM = M_SAMPLE
N_ROW_TILES = M_TOTAL // BM
N_PROMPT_TILES = M_PROMPT // BM
PROJ_BN = 1024
RET_IN_BN = 2048

RET_T = 256
RET_SUB = 4

SSM_T = 16
SSM_PAIRS = SSM_GROUPS // 2
SSM_PB = 4
SSM_SEGS = 8
SSM_STEPS_P = M_PROMPT // SSM_T // SSM_SEGS
SSM_STEPS_S = DEC_SEQ // SSM_T
SSM_PAIR_CH = 2 * SSM_GROUP
SSM_PAIR_COLS = SSM_T * SSM_PAIR_CH
SSM_VEC_TOKENS = LANES // SSM_PAIR_CH


def _params(n_axes):
    return pltpu.CompilerParams(dimension_semantics=("arbitrary",) * n_axes,
                                vmem_limit_bytes=VMEM_LIMIT_BYTES)


GELU_TANH_SCALE = math.sqrt(2.0 / math.pi)
GELU_TANH_CUBIC = 0.044715


def _sigmoid(x):
    return 1.0 / (1.0 + jnp.exp(-x))


def _rope_kernel(cos_ref, sin_ref):
    m = pl.program_id(0)
    half = RET_DK // 2
    freq = lax.broadcasted_iota(jnp.int32, (1, half), 1).astype(F32)
    inv = ROPE_BASE ** (-freq / half)
    r = m * BM + lax.broadcasted_iota(jnp.int32, (BM, half), 0)
    pos = jnp.where(r < M_PROMPT, r, PAST_LEN + ((r - M_PROMPT) & (DEC_SEQ - 1)))
    ang = pos.astype(F32) * inv
    cos_ref[...] = jnp.cos(ang)
    sin_ref[...] = jnp.sin(ang)


def _rope_tables():
    half = RET_DK // 2
    spec = pl.BlockSpec((BM, half), lambda m: (m, 0))
    return pl.pallas_call(
        _rope_kernel,
        out_shape=(jax.ShapeDtypeStruct((M_TOTAL, half), F32),) * 2,
        grid=(N_ROW_TILES,),
        out_specs=(spec, spec),
        compiler_params=_params(1),
        name="rope_tables",
    )()


def _norm_kernel(x_ref, g_ref, o_ref):
    x = x_ref[...]
    ms = jnp.mean(x * x, axis=-1, keepdims=True)
    o_ref[...] = ((x * lax.rsqrt(ms + EPS)) * g_ref[...]).astype(o_ref.dtype)


def _rmsnorm(x, g, out_dtype, tile0=0, n_tiles=N_ROW_TILES):
    return pl.pallas_call(
        _norm_kernel,
        out_shape=jax.ShapeDtypeStruct((n_tiles * BM, D_MODEL), out_dtype),
        grid=(n_tiles,),
        in_specs=[pl.BlockSpec((BM, D_MODEL), lambda m: (tile0 + m, 0)),
                  pl.BlockSpec((1, D_MODEL), lambda m: (0, 0))],
        out_specs=pl.BlockSpec((BM, D_MODEL), lambda m: (m, 0)),
        compiler_params=_params(1),
        name="rmsnorm",
    )(x, g.reshape(1, D_MODEL))


CHUNKS_PER_TILE = BM // 16
N_CHUNKS = M_TOTAL // 16
N_LANE_TILES = D_MODEL // LANES


def _transpose_lane_blocks(v, block_of_lane):
    n = len(v)
    out = []
    for u in range(n):
        acc = None
        for q in range(n):
            shift = ((q - u) * SSM_PAIR_CH) % LANES
            moved = v[q] if shift == 0 else pltpu.roll(v[q], shift, axis=1)
            acc = moved if acc is None else jnp.where(block_of_lane == q, moved, acc)
        out.append(acc)
    return out


def _norm_chunked_kernel(x_ref, g_ref, o_ref, h_s):
    x = x_ref[...]
    ms = jnp.mean(x * x, axis=-1, keepdims=True)
    h = (x * lax.rsqrt(ms + EPS)) * g_ref[...]
    for j in range(N_LANE_TILES):
        h_s[j] = h[:, j * LANES:(j + 1) * LANES]
    block_of_lane = lax.broadcasted_iota(jnp.int32, (CHUNKS_PER_TILE, LANES), 1) // SSM_PAIR_CH
    for j in range(N_LANE_TILES):
        for tg in range(SSM_T // SSM_VEC_TOKENS):
            rows = [h_s[j, pl.ds(tg * SSM_VEC_TOKENS + u, CHUNKS_PER_TILE, stride=SSM_T), :]
                    for u in range(SSM_VEC_TOKENS)]
            for q, out in enumerate(_transpose_lane_blocks(rows, block_of_lane)):
                c0 = q * SSM_PAIR_COLS + tg * LANES
                o_ref[j, :, c0:c0 + LANES] = out.astype(BF16)


def _rmsnorm_chunked(x, g):
    return pl.pallas_call(
        _norm_chunked_kernel,
        out_shape=jax.ShapeDtypeStruct((N_LANE_TILES, N_CHUNKS, SSM_T * LANES), BF16),
        grid=(N_ROW_TILES,),
        in_specs=[pl.BlockSpec((BM, D_MODEL), lambda m: (m, 0)),
                  pl.BlockSpec((1, D_MODEL), lambda m: (0, 0))],
        out_specs=pl.BlockSpec((N_LANE_TILES, CHUNKS_PER_TILE, SSM_T * LANES), lambda m: (0, m, 0)),
        scratch_shapes=[pltpu.VMEM((N_LANE_TILES, BM, LANES), F32)],
        compiler_params=_params(1),
        name="rmsnorm_chunked",
    )(x, g.reshape(1, D_MODEL))


def _weight_spec(k, bn, layer, col0=0):
    return pl.BlockSpec((None, k, bn), lambda n, m: (layer, 0, col0 + n), pipeline_mode=pl.Buffered(1))


def _emit_norm_inputs(x_new, xb_ref, ssq_ref):
    xb_ref[...] = x_new.astype(BF16)
    sq = x_new * x_new
    part = sq[:, 0:LANES]
    for t in range(1, x_new.shape[1] // LANES):
        part = part + sq[:, t * LANES:(t + 1) * LANES]
    ssq_ref[...] = part


def _row_scale(ssq_ref):
    total = jnp.sum(jnp.sum(ssq_ref[...], axis=0), axis=-1, keepdims=True)
    return lax.rsqrt(total * (1.0 / D_MODEL) + EPS)


def _norm_operands(norm):
    xb, ssq, g = norm
    args = [xb, ssq, g.reshape(D_MODEL, 1)]
    specs = [pl.BlockSpec((BM, D_MODEL), lambda n, m: (m, 0)),
             pl.BlockSpec((ssq.shape[0], BM, LANES), lambda n, m: (0, m, 0)),
             pl.BlockSpec((D_MODEL, 1), lambda n, m: (0, 0))]
    return args, specs


def _norm_outputs(n_out, bn):
    shapes = [jax.ShapeDtypeStruct((M_TOTAL, n_out), BF16),
              jax.ShapeDtypeStruct((n_out // bn, M_TOTAL, LANES), F32)]
    specs = [pl.BlockSpec((BM, bn), lambda n, m: (m, n)),
             pl.BlockSpec((None, BM, LANES), lambda n, m: (n, m, 0))]
    return shapes, specs


def _cast_ssq_kernel(xp_ref, xs_ref, xb_ref, ssq_ref):
    x = jnp.where(pl.program_id(0) < N_PROMPT_TILES, xp_ref[...], xs_ref[...])
    _emit_norm_inputs(x, xb_ref, ssq_ref)


def _cast_ssq(x_prompt, x_sample):
    return pl.pallas_call(
        _cast_ssq_kernel,
        out_shape=(jax.ShapeDtypeStruct((M_TOTAL, D_MODEL), BF16),
                   jax.ShapeDtypeStruct((1, M_TOTAL, LANES), F32)),
        grid=(N_ROW_TILES,),
        in_specs=[pl.BlockSpec((BM, D_MODEL), lambda m: (jnp.minimum(m, N_PROMPT_TILES - 1), 0)),
                  pl.BlockSpec((BM, D_MODEL), lambda m: (0, 0))],
        out_specs=(pl.BlockSpec((BM, D_MODEL), lambda m: (m, 0)),
                   pl.BlockSpec((None, BM, LANES), lambda m: (0, m, 0))),
        compiler_params=_params(1),
        name="cast_ssq",
    )(x_prompt, x_sample)


def _mm_qk_kernel(a_ref, ssq_ref, g_ref, w_ref, cos_ref, sin_ref, o_ref, wb_ref):
    n = pl.program_id(0)
    n_q = RET_QK // RET_IN_BN

    @pl.when(pl.program_id(1) == 0)
    def _():
        wb_ref[...] = (w_ref[...] * g_ref[...]).astype(BF16)

    acc = jnp.dot(a_ref[...], wb_ref[...], preferred_element_type=F32) * _row_scale(ssq_ref)
    scale = jnp.where(n >= n_q, RET_DK ** -0.5, 1.0).astype(F32)
    c = cos_ref[...]
    s = sin_ref[...]
    half = RET_DK // 2
    for hh in range(RET_IN_BN // RET_DK):
        lo = hh * RET_DK
        x1 = acc[:, lo:lo + half]
        x2 = acc[:, lo + half:lo + RET_DK]
        o_ref[:, lo:lo + half] = ((x1 * c - x2 * s) * scale).astype(BF16)
        o_ref[:, lo + half:lo + RET_DK] = ((x1 * s + x2 * c) * scale).astype(BF16)


def _mm_cast_kernel(a_ref, ssq_ref, g_ref, w_ref, o_ref, wb_ref):
    @pl.when(pl.program_id(1) == 0)
    def _():
        wb_ref[...] = (w_ref[...] * g_ref[...]).astype(BF16)

    acc = jnp.dot(a_ref[...], wb_ref[...], preferred_element_type=F32) * _row_scale(ssq_ref)
    o_ref[...] = acc.astype(BF16)


def _mm_retin(norm, w, layer, cos, sin):
    bn = RET_IN_BN
    _, k, n_out = w.shape
    half = RET_DK // 2
    n_qk = 2 * RET_QK // bn
    norm_args, norm_specs = _norm_operands(norm)
    out_spec = pl.BlockSpec((BM, bn), lambda n, m: (m, n))
    rope_spec = pl.BlockSpec((BM, half), lambda n, m: (m, 0))
    qk = pl.pallas_call(
        _mm_qk_kernel,
        out_shape=jax.ShapeDtypeStruct((M_TOTAL, 2 * RET_QK), BF16),
        grid=(n_qk, N_ROW_TILES),
        in_specs=norm_specs + [_weight_spec(k, bn, layer), rope_spec, rope_spec],
        out_specs=out_spec,
        scratch_shapes=[pltpu.VMEM((k, bn), BF16)],
        compiler_params=_params(2),
        name="mm_ret_qk",
    )(*norm_args, w, cos, sin)
    vg = pl.pallas_call(
        _mm_cast_kernel,
        out_shape=jax.ShapeDtypeStruct((M_TOTAL, 2 * RET_VD), BF16),
        grid=(n_out // bn - n_qk, N_ROW_TILES),
        in_specs=norm_specs + [_weight_spec(k, bn, layer, n_qk)],
        out_specs=out_spec,
        scratch_shapes=[pltpu.VMEM((k, bn), BF16)],
        compiler_params=_params(2),
        name="mm_ret_vg",
    )(*norm_args, w)
    return qk, vg


def _row_split_specs(cols, col_index):
    return [pl.BlockSpec((BM, cols), lambda n, m: (jnp.minimum(m, N_PROMPT_TILES - 1), col_index(n))),
            pl.BlockSpec((BM, cols), lambda n, m: (0, col_index(n)))]


def _mm_res_kernel(*refs, split_a, split_x, emit):
    refs = list(refs)
    ap_ref = refs.pop(0)
    as_ref = refs.pop(0) if split_a else None
    w_ref = refs.pop(0)
    xp_ref = refs.pop(0)
    xs_ref = refs.pop(0) if split_x else None
    o_ref = refs.pop(0)
    xb_ref, ssq_ref = (refs.pop(0), refs.pop(0)) if emit else (None, None)
    (wb_ref,) = refs
    m = pl.program_id(1)
    on_prompt = m < N_PROMPT_TILES

    @pl.when(m == 0)
    def _():
        wb_ref[...] = w_ref[...].astype(BF16)

    def body(a_ref):
        x = xp_ref[...]
        if split_x:
            x = jnp.where(on_prompt, x, xs_ref[...])
        x_new = x + jnp.dot(a_ref[...], wb_ref[...], preferred_element_type=F32)
        o_ref[...] = x_new
        if emit:
            _emit_norm_inputs(x_new, xb_ref, ssq_ref)

    if split_a:
        pl.when(on_prompt)(lambda: body(ap_ref))
        pl.when(jnp.logical_not(on_prompt))(lambda: body(as_ref))
    else:
        body(ap_ref)


def _mm_res(a, w, layer, x, emit=False):
    bn = PROJ_BN
    _, k, n_out = w.shape
    split_a = isinstance(a, tuple)
    split_x = isinstance(x, tuple)
    a_specs = _row_split_specs(k, lambda n: 0) if split_a else [pl.BlockSpec((BM, k), lambda n, m: (m, 0))]
    x_specs = _row_split_specs(bn, lambda n: n) if split_x else [pl.BlockSpec((BM, bn), lambda n, m: (m, n))]
    a_args = list(a) if split_a else [a]
    x_args = list(x) if split_x else [x]
    out_shapes = [jax.ShapeDtypeStruct((M_TOTAL, n_out), F32)]
    out_specs = [pl.BlockSpec((BM, bn), lambda n, m: (m, n))]
    if emit:
        shapes, specs = _norm_outputs(n_out, bn)
        out_shapes += shapes
        out_specs += specs
    out = pl.pallas_call(
        functools.partial(_mm_res_kernel, split_a=split_a, split_x=split_x, emit=emit),
        out_shape=tuple(out_shapes),
        grid=(n_out // bn, N_ROW_TILES),
        in_specs=a_specs + [_weight_spec(k, bn, layer)] + x_specs,
        out_specs=tuple(out_specs),
        scratch_shapes=[pltpu.VMEM((k, bn), BF16)],
        compiler_params=_params(2),
        name="mm_residual",
    )(*a_args, w, *x_args)
    return out if emit else out[0]


def _mm_glu_kernel(a_ref, wa_ref, wg_ref, x_ref, o_ref, xb_ref, ssq_ref, wab_ref, wgb_ref):
    @pl.when(pl.program_id(1) == 0)
    def _():
        wab_ref[...] = wa_ref[...].astype(BF16)
        wgb_ref[...] = wg_ref[...].astype(BF16)

    a = a_ref[...]
    ga = jnp.dot(a, wab_ref[...], preferred_element_type=F32)
    gb = jnp.dot(a, wgb_ref[...], preferred_element_type=F32)
    x_new = x_ref[...] + ga * _sigmoid(gb)
    o_ref[...] = x_new
    _emit_norm_inputs(x_new, xb_ref, ssq_ref)


def _mm_glu(a, w, layer, x):
    bn = PROJ_BN
    _, k, n2 = w.shape
    n_out = n2 // 2
    nb = n_out // bn
    norm_shapes, norm_specs = _norm_outputs(n_out, bn)
    return pl.pallas_call(
        _mm_glu_kernel,
        out_shape=(jax.ShapeDtypeStruct((M_TOTAL, n_out), F32), *norm_shapes),
        grid=(nb, N_ROW_TILES),
        in_specs=[
            pl.BlockSpec((BM, k), lambda n, m: (m, 0)),
            _weight_spec(k, bn, layer),
            _weight_spec(k, bn, layer, nb),
            pl.BlockSpec((BM, bn), lambda n, m: (m, n)),
        ],
        out_specs=(pl.BlockSpec((BM, bn), lambda n, m: (m, n)), *norm_specs),
        scratch_shapes=[pltpu.VMEM((k, bn), BF16), pltpu.VMEM((k, bn), BF16)],
        compiler_params=_params(2),
        name="mm_glu",
    )(a, w, w, x)


def _mm_ffnup_kernel(h_ref, ssq_ref, g_ref, wa_ref, wg_ref, cw_ref, cb_ref, cache_ref,
                     o_ref, tp_ref, ts_ref, wab_ref, wgb_ref, carry_ref):
    m = pl.program_id(1)

    @pl.when(m == 0)
    def _():
        g = g_ref[...]
        wab_ref[...] = (wa_ref[...] * g).astype(BF16)
        wgb_ref[...] = (wg_ref[...] * g).astype(BF16)
        carry_ref[...] = jnp.zeros_like(carry_ref)

    h = h_ref[...]
    r = _row_scale(ssq_ref)
    a = jnp.dot(h, wab_ref[...], preferred_element_type=F32) * r
    b = jnp.dot(h, wgb_ref[...], preferred_element_type=F32) * r
    r1 = pltpu.roll(a, 1, axis=0)
    r2 = pltpu.roll(a, 2, axis=0)
    cb = cb_ref[...]
    w0 = cw_ref[0:1, :]
    w1 = cw_ref[1:2, :]
    w2 = cw_ref[2:3, :]

    def gated(a_rows, prev1, prev2, b_rows):
        conv = cb + w0 * prev2 + w1 * prev1 + w2 * a_rows
        return ((conv * _sigmoid(conv)) * b_rows).astype(BF16)

    o_ref[...] = gated(a, r1, r2, b)

    row8 = lax.broadcasted_iota(jnp.int32, (SUBLANES, 1), 0)
    heads = [tuple(z[s * DEC_SEQ:s * DEC_SEQ + SUBLANES] for z in (a, r1, r2, b))
             for s in range(DEC_BATCH)]
    tails = [a[(s + 1) * DEC_SEQ - SUBLANES:(s + 1) * DEC_SEQ] for s in range(DEC_BATCH)]

    def redo_head(s, before1, before2):
        a8, r1_8, r2_8, b8 = heads[s]
        prev1 = jnp.where(row8 == 0, before1, r1_8)
        prev2 = jnp.where(row8 == 0, before2, jnp.where(row8 == 1, before1, r2_8))
        o_ref[s * DEC_SEQ:s * DEC_SEQ + SUBLANES, :] = gated(a8, prev1, prev2, b8)

    @pl.when(m < N_PROMPT_TILES)
    def _():
        c = carry_ref[...]
        redo_head(0, c[SUBLANES - 1:SUBLANES, :], c[SUBLANES - 2:SUBLANES - 1, :])
        carry_ref[...] = tails[DEC_BATCH - 1]

        @pl.when(m == N_PROMPT_TILES - 1)
        def _():
            tp_ref[...] = tails[DEC_BATCH - 1]

    @pl.when(m == N_PROMPT_TILES)
    def _():
        for s in range(DEC_BATCH):
            c = cache_ref[s]
            redo_head(s, c[1:2, :], c[0:1, :])
            ts_ref[s * SUBLANES:(s + 1) * SUBLANES, :] = tails[s]


def _mm_ffnup(norm, w, layer, conv_w, conv_b, cache):
    bn = PROJ_BN
    k = w.shape[1]
    nb = FFN_DIM // bn
    norm_args, norm_specs = _norm_operands(norm)
    return pl.pallas_call(
        _mm_ffnup_kernel,
        out_shape=(
            jax.ShapeDtypeStruct((M_TOTAL, FFN_DIM), BF16),
            jax.ShapeDtypeStruct((SUBLANES, FFN_DIM), F32),
            jax.ShapeDtypeStruct((DEC_BATCH * SUBLANES, FFN_DIM), F32),
        ),
        grid=(nb, N_ROW_TILES),
        in_specs=norm_specs + [
            _weight_spec(k, bn, layer),
            _weight_spec(k, bn, layer, nb),
            pl.BlockSpec((CONV_W, bn), lambda n, m: (0, n)),
            pl.BlockSpec((1, bn), lambda n, m: (0, n)),
            pl.BlockSpec((DEC_BATCH, CONV_W - 1, bn), lambda n, m: (0, 0, n)),
        ],
        out_specs=(
            pl.BlockSpec((BM, bn), lambda n, m: (m, n)),
            pl.BlockSpec((SUBLANES, bn), lambda n, m: (0, n)),
            pl.BlockSpec((DEC_BATCH * SUBLANES, bn), lambda n, m: (0, n)),
        ),
        scratch_shapes=[pltpu.VMEM((k, bn), BF16), pltpu.VMEM((k, bn), BF16),
                        pltpu.VMEM((SUBLANES, bn), F32)],
        compiler_params=_params(2),
        name="mm_ffn_up",
    )(*norm_args, w, w, conv_w, conv_b.reshape(1, FFN_DIM), cache)


def _ret_tables(lg, T):
    i = lax.broadcasted_iota(jnp.int32, (T, T), 0)
    j = lax.broadcasted_iota(jnp.int32, (T, T), 1)
    dist = jnp.abs(i - j).astype(F32)
    shift = int(math.log2(CHUNK))
    visible = (j >> shift) <= (i >> shift)
    mask = jnp.where(visible, jnp.exp(dist * lg), 0.0)
    t = lax.broadcasted_iota(jnp.int32, (T, 1), 0).astype(F32)
    cross = jnp.exp((t + 1.0) * lg)
    k_dec = jnp.exp((T - 1.0 - t) * lg)
    decay = jnp.exp(float(T) * lg)
    return mask, cross, k_dec, decay


def _ret_block(q, k, v, g, state, tables, gn):
    mask, cross, k_dec, decay = tables
    scores = lax.dot_general(q, k, (((1,), (1,)), ((), ())), preferred_element_type=F32)
    scores = scores * mask
    out = jnp.dot(scores.astype(BF16), v, preferred_element_type=F32)
    out = out + jnp.dot(q, state.astype(BF16), preferred_element_type=F32) * cross

    kd = (k.astype(F32) * k_dec).astype(BF16)
    new_state = decay * state + lax.dot_general(
        kd, v, (((0,), (0,)), ((), ())), preferred_element_type=F32)

    mu = jnp.mean(out, axis=-1, keepdims=True)
    oc = out - mu
    var = jnp.mean(oc * oc, axis=-1, keepdims=True)
    normed = (oc * lax.rsqrt(var + EPS)) * gn
    g = g.astype(F32)
    return (normed * (g * _sigmoid(g))).astype(BF16), new_state


def _ret_prompt_kernel(lg_ref, q_ref, k_ref, v_ref, g_ref, gn_ref, o_ref, sl_ref, state_ref, mask_ref):
    blk = pl.program_id(1)
    lg = lg_ref[...][:, 0:1]

    @pl.when(blk == 0)
    def _():
        state_ref[...] = jnp.zeros_like(state_ref)
        mask_ref[...] = _ret_tables(lg, RET_T)[0]

    _, cross, k_dec, decay = _ret_tables(lg, RET_T)
    tables = (mask_ref[...], cross, k_dec, decay)
    gn = gn_ref[...]
    state = state_ref[...]
    for sb in range(RET_SUB):
        rows = slice(sb * RET_T, (sb + 1) * RET_T)
        o_ref[rows, :], state = _ret_block(q_ref[rows, :], k_ref[rows, :], v_ref[rows, :],
                                           g_ref[rows, :], state, tables, gn)
    state_ref[...] = state

    @pl.when(blk == pl.num_programs(1) - 1)
    def _():
        sl_ref[...] = state


def _ret_sample_kernel(lg_ref, q_ref, k_ref, v_ref, g_ref, gn_ref, s0_ref, o_ref, sl_ref):
    lg = lg_ref[...][:, 0:1]
    tables = _ret_tables(lg, DEC_SEQ)
    gn = gn_ref[...]
    for s in range(DEC_BATCH):
        rows = slice(s * DEC_SEQ, (s + 1) * DEC_SEQ)
        o_ref[rows, :], sl_ref[s] = _ret_block(q_ref[rows, :], k_ref[rows, :], v_ref[rows, :],
                                               g_ref[rows, :], s0_ref[s], tables, gn)


def _retention(qk, vg, gn, s0_stack, layer):
    log_g = np.log1p(-np.exp2(-5.0 - np.arange(RET_HEADS, dtype=np.float32))).astype(np.float32)
    lg = jnp.asarray(np.broadcast_to(log_g[:, None, None], (RET_HEADS, 1, LANES)).copy())
    gn = gn.reshape(1, RET_VD)
    k_off = RET_QK // RET_DK
    g_off = RET_VD // RET_DV
    rows_p = RET_T * RET_SUB
    o_p, s_p = pl.pallas_call(
        _ret_prompt_kernel,
        out_shape=(jax.ShapeDtypeStruct((M_PROMPT, RET_VD), BF16),
                   jax.ShapeDtypeStruct((1, RET_HEADS, RET_DK, RET_DV), F32)),
        grid=(RET_HEADS, M_PROMPT // rows_p),
        in_specs=[
            pl.BlockSpec((None, 1, LANES), lambda h, b: (h, 0, 0)),
            pl.BlockSpec((rows_p, RET_DK), lambda h, b: (b, h)),
            pl.BlockSpec((rows_p, RET_DK), lambda h, b: (b, k_off + h)),
            pl.BlockSpec((rows_p, RET_DV), lambda h, b: (b, h)),
            pl.BlockSpec((rows_p, RET_DV), lambda h, b: (b, g_off + h)),
            pl.BlockSpec((1, RET_DV), lambda h, b: (0, h)),
        ],
        out_specs=(pl.BlockSpec((rows_p, RET_DV), lambda h, b: (b, h)),
                   pl.BlockSpec((None, None, RET_DK, RET_DV), lambda h, b: (0, h, 0, 0))),
        scratch_shapes=[pltpu.VMEM((RET_DK, RET_DV), F32), pltpu.VMEM((RET_T, RET_T), F32)],
        compiler_params=_params(2),
        name="retention",
    )(lg, qk, qk, vg, vg, gn)
    sample_block = M_PROMPT // M_SAMPLE
    o_s, s_s = pl.pallas_call(
        _ret_sample_kernel,
        out_shape=(jax.ShapeDtypeStruct((M_SAMPLE, RET_VD), BF16),
                   jax.ShapeDtypeStruct((DEC_BATCH, RET_HEADS, RET_DK, RET_DV), F32)),
        grid=(RET_HEADS,),
        in_specs=[
            pl.BlockSpec((None, 1, LANES), lambda h: (h, 0, 0)),
            pl.BlockSpec((M_SAMPLE, RET_DK), lambda h: (sample_block, h)),
            pl.BlockSpec((M_SAMPLE, RET_DK), lambda h: (sample_block, k_off + h)),
            pl.BlockSpec((M_SAMPLE, RET_DV), lambda h: (sample_block, h)),
            pl.BlockSpec((M_SAMPLE, RET_DV), lambda h: (sample_block, g_off + h)),
            pl.BlockSpec((1, RET_DV), lambda h: (0, h)),
            pl.BlockSpec((None, DEC_BATCH, None, RET_DK, RET_DV), lambda h: (layer, 0, h, 0, 0)),
        ],
        out_specs=(pl.BlockSpec((M_SAMPLE, RET_DV), lambda h: (0, h)),
                   pl.BlockSpec((DEC_BATCH, None, RET_DK, RET_DV), lambda h: (0, h, 0, 0))),
        compiler_params=_params(1),
        name="retention_sample",
    )(lg, qk, qk, vg, vg, gn, s0_stack)
    return o_p, o_s, s_p, s_s


def _ssm_prep_kernel(ar_ref, ai_ref, ldt_ref, br_ref, bi_ref, cr_ref, ci_ref,
                     bc_ref, cc_ref, w_ref, a16r_ref, a16i_ref, vr_s, vi_s):
    ar = ar_ref[...]
    ai = ai_ref[...]
    dt = jnp.exp(ldt_ref[...])
    mag = jnp.exp(ar * dt)
    ang = ai * dt
    abr = mag * jnp.cos(ang)
    abi = mag * jnp.sin(ang)
    den = ar * ar + ai * ai
    nr = abr - 1.0
    ni = abi
    cfr = (nr * ar + ni * ai) / den
    cfi = (ni * ar - nr * ai) / den

    def b4(z):
        return z[:, None]

    br = br_ref[...]
    bi = bi_ref[...]
    bbr = b4(cfr) * br - b4(cfi) * bi
    bbi = b4(cfr) * bi + b4(cfi) * br
    cre = cr_ref[...]
    cim = ci_ref[...]

    powers = []
    pr = jnp.ones_like(ar)
    pi = jnp.zeros_like(ar)
    for _ in range(SSM_T + 1):
        powers.append((pr, pi))
        pr, pi = pr * abr - pi * abi, pr * abi + pi * abr
    a16r_ref[...] = powers[SSM_T][0]
    a16i_ref[...] = powers[SSM_T][1]

    for m in range(SSM_T + 1):
        pr, pi = powers[m]
        vr = b4(pr) * cre - b4(pi) * cim
        vi = b4(pr) * cim + b4(pi) * cre
        if m < SSM_T:
            vr_s[:, m] = vr
            vi_s[:, m] = vi
        if m >= 1:
            cc_ref[:, m - 1, :, :, 0:LANES] = vr.astype(BF16)
            cc_ref[:, m - 1, :, :, LANES:2 * LANES] = (-vi).astype(BF16)

    for t in range(SSM_T):
        pr, pi = powers[SSM_T - 1 - t]
        bc_ref[:, t, :, :, 0:LANES] = (b4(pr) * bbr - b4(pi) * bbi).astype(BF16)
        bc_ref[:, t, :, :, LANES:2 * LANES] = (b4(pr) * bbi + b4(pi) * bbr).astype(BF16)

    lane = lax.broadcasted_iota(jnp.int32, (SSM_PAIR_CH, SSM_PAIR_COLS), 1)
    contract_last = (((1,), (1,)), ((), ()))
    for q in range(SSM_PB):
        vr_all = vr_s[q].reshape(SSM_PAIR_COLS, LANES)
        vi_all = vi_s[q].reshape(SSM_PAIR_COLS, LANES)
        lags = (lax.dot_general(bbr[q].reshape(SSM_PAIR_CH, LANES), vr_all, contract_last,
                                precision=lax.Precision.HIGHEST, preferred_element_type=F32)
                - lax.dot_general(bbi[q].reshape(SSM_PAIR_CH, LANES), vi_all, contract_last,
                                  precision=lax.Precision.HIGHEST, preferred_element_type=F32))
        for t in range(SSM_T):
            shifted = lags if t == 0 else pltpu.roll(lags, SSM_PAIR_CH * t, axis=1)
            w_ref[q, t * SSM_PAIR_CH:(t + 1) * SSM_PAIR_CH, :] = jnp.where(
                lane >= SSM_PAIR_CH * t, shifted, 0.0).astype(BF16)


def _pair_pack(x):
    x = x.reshape(SSM_PAIRS, 2, SSM_GROUP, SSM_STATE)
    lo = jnp.pad(x[:, 0], ((0, 0), (0, 0), (0, SSM_STATE)))
    hi = jnp.pad(x[:, 1], ((0, 0), (0, 0), (SSM_STATE, 0)))
    return jnp.stack([lo, hi], axis=1)


def _ssm_prep(a_re, a_im, log_dt, b_re, b_im, c_re, c_im):
    assert 2 * SSM_PB * SSM_GROUP == LANES
    ar = a_re.reshape(SSM_PAIRS, 1, LANES)
    ai = a_im.reshape(SSM_PAIRS, 1, LANES)
    ldt = jnp.repeat(log_dt, SSM_STATE).reshape(SSM_PAIRS, 1, LANES)
    b2r = _pair_pack(jnp.transpose(b_re, (0, 2, 1)))
    b2i = _pair_pack(jnp.transpose(b_im, (0, 2, 1)))
    c2r = _pair_pack(c_re)
    c2i = _pair_pack(c_im)
    vec = pl.BlockSpec((SSM_PB, 1, LANES), lambda p: (p, 0, 0))
    mat = pl.BlockSpec((SSM_PB, 2, SSM_GROUP, LANES), lambda p: (p, 0, 0, 0))
    proj = pl.BlockSpec((SSM_PB, SSM_T, 2, SSM_GROUP, 2 * LANES), lambda p: (p, 0, 0, 0, 0))
    proj_shape = jax.ShapeDtypeStruct((SSM_PAIRS, SSM_T, 2, SSM_GROUP, 2 * LANES), BF16)
    bc, cc, w, a16r, a16i = pl.pallas_call(
        _ssm_prep_kernel,
        out_shape=(
            proj_shape, proj_shape,
            jax.ShapeDtypeStruct((SSM_PAIRS, SSM_PAIR_COLS, SSM_PAIR_COLS), BF16),
            jax.ShapeDtypeStruct((SSM_PAIRS, 1, LANES), F32),
            jax.ShapeDtypeStruct((SSM_PAIRS, 1, LANES), F32),
        ),
        grid=(N_LANE_TILES,),
        in_specs=[vec, vec, vec, mat, mat, mat, mat],
        out_specs=(proj, proj,
                   pl.BlockSpec((SSM_PB, SSM_PAIR_COLS, SSM_PAIR_COLS), lambda p: (p, 0, 0)),
                   vec, vec),
        scratch_shapes=[pltpu.VMEM((SSM_PB, SSM_T, 2, SSM_GROUP, LANES), F32)] * 2,
        compiler_params=_params(1),
        name="ssm_prep",
    )(ar, ai, ldt, b2r, b2i, c2r, c2i)
    bc = bc.reshape(SSM_PAIRS, SSM_PAIR_COLS, 2 * LANES)
    cc = cc.reshape(SSM_PAIRS, SSM_PAIR_COLS, 2 * LANES)
    row = SSM_GROUPS * SSM_STATE
    a16r = jnp.broadcast_to(a16r.reshape(1, row), (SSM_SEGS, row))
    a16i = jnp.broadcast_to(a16i.reshape(1, row), (SSM_SEGS, row))
    return bc, cc, w, a16r, a16i


SSM_CHUNK_COLS = SSM_T * LANES
SSM_STATE_COLS = SSM_PB * LANES
SSM_CHUNKS_P = M_PROMPT // SSM_T
SSM_SCAN_UNROLL = 8


def _ssm_main_kernel(x_ref, w_ref, bc_ref, cc_ref, ar_ref, ai_ref, h0r_ref, h0i_ref,
                     y_ref, fpr_ref, fpi_ref, fsr_ref, fsi_ref, d_s, s_s):
    pair_cols = [slice(q * SSM_PAIR_COLS, (q + 1) * SSM_PAIR_COLS) for q in range(SSM_PB)]

    for q in range(SSM_PB):
        d = jnp.dot(x_ref[:, pair_cols[q]], bc_ref[q], preferred_element_type=F32)
        d_s[q] = d[:, 0:LANES]
        d_s[SSM_PB + q] = d[:, LANES:2 * LANES]

    tiles = range(SSM_PB)
    ar = [ar_ref[:, k * LANES:(k + 1) * LANES] for k in tiles]
    ai = [ai_ref[:, k * LANES:(k + 1) * LANES] for k in tiles]

    def advance(sr, si, rows, record):
        out_r, out_i = [], []
        for k in tiles:
            if record:
                s_s[k, rows, :] = sr[k]
                s_s[SSM_PB + k, rows, :] = si[k]
            dr = d_s[k, rows, :]
            di = d_s[SSM_PB + k, rows, :]
            out_r.append(ar[k] * sr[k] - ai[k] * si[k] + dr)
            out_i.append(ar[k] * si[k] + ai[k] * sr[k] + di)
        return out_r, out_i

    def prompt_rows(i):
        return pl.ds(i, SSM_SEGS, stride=SSM_STEPS_P)

    def prompt_pass(start_r, start_i, record):
        def body(i, carry):
            out_r, out_i = advance(list(carry[0]), list(carry[1]), prompt_rows(i), record)
            return tuple(out_r), tuple(out_i)
        out_r, out_i = lax.fori_loop(0, SSM_STEPS_P, body, (tuple(start_r), tuple(start_i)),
                                     unroll=SSM_SCAN_UNROLL)
        return list(out_r), list(out_i)

    zeros = [jnp.zeros((SSM_SEGS, LANES), F32) for _ in tiles]
    er, ei = prompt_pass(zeros, zeros, record=False)

    sr, si = [], []
    for k in tiles:
        pr, pi = ar[k][0:1], ai[k][0:1]
        for _ in range(int(math.log2(SSM_STEPS_P))):
            pr, pi = pr * pr - pi * pi, 2.0 * pr * pi
        cr = jnp.zeros((1, LANES), F32)
        ci = cr
        starts_r, starts_i = [], []
        for s in range(SSM_SEGS):
            starts_r.append(cr)
            starts_i.append(ci)
            cr, ci = (pr * cr - pi * ci + er[k][s:s + 1], pr * ci + pi * cr + ei[k][s:s + 1])
        cols = slice(k * LANES, (k + 1) * LANES)
        fpr_ref[:, cols] = jnp.broadcast_to(cr, (SSM_SEGS, LANES))
        fpi_ref[:, cols] = jnp.broadcast_to(ci, (SSM_SEGS, LANES))
        sr.append(jnp.concatenate(starts_r, axis=0))
        si.append(jnp.concatenate(starts_i, axis=0))

    prompt_pass(sr, si, record=True)
    sr = [h0r_ref[:, k * LANES:(k + 1) * LANES] for k in tiles]
    si = [h0i_ref[:, k * LANES:(k + 1) * LANES] for k in tiles]
    for i in range(SSM_STEPS_S):
        sr, si = advance(sr, si, pl.ds(SSM_CHUNKS_P + i, DEC_BATCH, stride=SSM_STEPS_S), record=True)
    for k in tiles:
        fsr_ref[:, k * LANES:(k + 1) * LANES] = sr[k]
        fsi_ref[:, k * LANES:(k + 1) * LANES] = si[k]

    contract_last = (((1,), (1,)), ((), ()))
    for q in range(SSM_PB):
        states = jnp.concatenate([s_s[q].astype(BF16), s_s[SSM_PB + q].astype(BF16)], axis=1)
        carried = lax.dot_general(states, cc_ref[q], contract_last, preferred_element_type=F32)
        local = jnp.dot(x_ref[:, pair_cols[q]], w_ref[q], preferred_element_type=F32)
        y_ref[:, pair_cols[q]] = local + carried


def _ssm_main(x_c, w, bc, cc, a16r, a16i, h0r, h0i):
    row = SSM_GROUPS * SSM_STATE
    st = pl.BlockSpec((SSM_SEGS, SSM_STATE_COLS), lambda j: (0, j))
    st_shape = jax.ShapeDtypeStruct((SSM_SEGS, row), F32)
    proj = pl.BlockSpec((SSM_PB, SSM_PAIR_COLS, 2 * LANES), lambda j: (j, 0, 0))
    chunk_rows = pl.BlockSpec((None, N_CHUNKS, SSM_CHUNK_COLS), lambda j: (j, 0, 0))
    return pl.pallas_call(
        _ssm_main_kernel,
        out_shape=(jax.ShapeDtypeStruct((N_LANE_TILES, N_CHUNKS, SSM_CHUNK_COLS), F32),
                   st_shape, st_shape, st_shape, st_shape),
        grid=(N_LANE_TILES,),
        in_specs=[
            chunk_rows,
            pl.BlockSpec((SSM_PB, SSM_PAIR_COLS, SSM_PAIR_COLS), lambda j: (j, 0, 0)),
            proj, proj, st, st, st, st,
        ],
        out_specs=(chunk_rows, st, st, st, st),
        scratch_shapes=[
            pltpu.VMEM((2 * SSM_PB, N_CHUNKS, LANES), F32),
            pltpu.VMEM((2 * SSM_PB, N_CHUNKS, LANES), F32),
        ],
        compiler_params=_params(1),
        name="ssm_main",
    )(x_c, w, bc, cc, a16r, a16i, h0r, h0i)


def _ssm_post_kernel(y_ref, x_ref, g_ref, d_ref, o_ref, y_s):
    block_of_lane = lax.broadcasted_iota(jnp.int32, (CHUNKS_PER_TILE, LANES), 1) // SSM_PAIR_CH
    for j in range(N_LANE_TILES):
        for tg in range(SSM_T // SSM_VEC_TOKENS):
            pieces = [y_ref[j, :, q * SSM_PAIR_COLS + tg * LANES:q * SSM_PAIR_COLS + (tg + 1) * LANES]
                      for q in range(SSM_PB)]
            for u, out in enumerate(_transpose_lane_blocks(pieces, block_of_lane)):
                y_s[j, pl.ds(tg * SSM_VEC_TOKENS + u, CHUNKS_PER_TILE, stride=SSM_T), :] = out
    x = x_ref[...]
    ms = jnp.mean(x * x, axis=-1, keepdims=True)
    h = (x * lax.rsqrt(ms + EPS)) * g_ref[...]
    for j in range(N_LANE_TILES):
        cols = slice(j * LANES, (j + 1) * LANES)
        y = y_s[j] + d_ref[:, cols] * h[:, cols]
        gelu = 0.5 * y * (1.0 + jnp.tanh(GELU_TANH_SCALE * (y + GELU_TANH_CUBIC * (y * y * y))))
        o_ref[:, cols] = gelu.astype(BF16)


def _ssm_post(y_c, x, g, d):
    spec = pl.BlockSpec((BM, D_MODEL), lambda m: (m, 0))
    vec = pl.BlockSpec((1, D_MODEL), lambda m: (0, 0))
    return pl.pallas_call(
        _ssm_post_kernel,
        out_shape=jax.ShapeDtypeStruct((M_TOTAL, D_MODEL), BF16),
        grid=(N_ROW_TILES,),
        in_specs=[pl.BlockSpec((N_LANE_TILES, CHUNKS_PER_TILE, SSM_CHUNK_COLS), lambda m: (0, m, 0)),
                  spec, vec, vec],
        out_specs=spec,
        scratch_shapes=[pltpu.VMEM((N_LANE_TILES, BM, LANES), F32)],
        compiler_params=_params(1),
        name="ssm_post",
    )(y_c, x, g.reshape(1, D_MODEL), d.reshape(1, D_MODEL))


def kernel(x_prompt, x_sample, state_ret, state_ssm_re, state_ssm_im, cache_conv, norm_mix, norm_ffn, norm_final, ret_w_in, ret_gn, ret_w_out, ssm_a_re, ssm_a_im, ssm_log_dt, ssm_b_re, ssm_b_im, ssm_c_re, ssm_c_im, ssm_d, ssm_w_glu, ffn_w_up, ffn_conv_w, ffn_conv_b, ffn_w_down):
    x = (x_prompt.reshape(M_PROMPT, D_MODEL), x_sample.reshape(M_SAMPLE, D_MODEL))
    xb, ssq = _cast_ssq(*x)
    cos, sin = _rope_tables()

    ret_p, ret_s = [], []
    re_p, im_p, re_s, im_s = [], [], [], []
    conv_p, conv_s = [], []
    state_row = SSM_GROUPS * SSM_STATE
    for i in range(DEPTH):
        j = i // 2
        if i % 2 == 0:
            qk, vg = _mm_retin((xb, ssq, norm_mix[i]), ret_w_in, j, cos, sin)
            o_p, o_s, s_p, s_s = _retention(qk, vg, ret_gn[j], state_ret, j)
            ret_p.append(s_p)
            ret_s.append(s_s)
            x, xb, ssq = _mm_res((o_p, o_s), ret_w_out, j, x, emit=True)
        else:
            h_c = _rmsnorm_chunked(x, norm_mix[i])
            bc, cc, w_ssm, a16r, a16i = _ssm_prep(ssm_a_re[j], ssm_a_im[j], ssm_log_dt[j],
                                               ssm_b_re[j], ssm_b_im[j], ssm_c_re[j], ssm_c_im[j])
            y_c, fpr, fpi, fsr, fsi = _ssm_main(
                h_c, w_ssm, bc, cc, a16r, a16i,
                state_ssm_re[j].reshape(DEC_BATCH, state_row),
                state_ssm_im[j].reshape(DEC_BATCH, state_row))
            re_p.append(fpr[0:1].reshape(1, SSM_GROUPS, SSM_STATE))
            im_p.append(fpi[0:1].reshape(1, SSM_GROUPS, SSM_STATE))
            re_s.append(fsr.reshape(DEC_BATCH, SSM_GROUPS, SSM_STATE))
            im_s.append(fsi.reshape(DEC_BATCH, SSM_GROUPS, SSM_STATE))
            gl = _ssm_post(y_c, x, norm_mix[i], ssm_d[j])
            x, xb, ssq = _mm_glu(gl, ssm_w_glu, j, x)

        u, tail_p, tail_s = _mm_ffnup((xb, ssq, norm_ffn[i]), ffn_w_up, i, ffn_conv_w[i],
                                      ffn_conv_b[i], cache_conv[i])
        conv_p.append(tail_p[SUBLANES - (CONV_W - 1):].reshape(1, CONV_W - 1, FFN_DIM))
        conv_s.append(tail_s.reshape(DEC_BATCH, SUBLANES, FFN_DIM)[:, SUBLANES - (CONV_W - 1):])
        if i + 1 < DEPTH and (i + 1) % 2 == 0:
            x, xb, ssq = _mm_res(u, ffn_w_down, i, x, emit=True)
        else:
            x = _mm_res(u, ffn_w_down, i, x)

    y_prompt = _rmsnorm(x, norm_final, F32, 0, N_PROMPT_TILES).reshape(1, SEQ, D_MODEL)
    y_sample = _rmsnorm(x, norm_final, F32, N_PROMPT_TILES, 1).reshape(DEC_BATCH, DEC_SEQ, D_MODEL)
    return (y_prompt, y_sample, jnp.stack(ret_p), jnp.stack(ret_s),
            jnp.stack(re_p), jnp.stack(im_p), jnp.stack(re_s), jnp.stack(im_s),
            jnp.stack(conv_p), jnp.stack(conv_s))
```

```python
import functools
import math

import numpy as np
import jax
import jax.numpy as jnp
from jax import lax
from jax.experimental import pallas as pl
from jax.experimental.pallas import tpu as pltpu

F32 = jnp.float32
BF16 = jnp.bfloat16

D_MODEL = 2048
SEQ = 8192
DEPTH = 4
DEC_BATCH = 8
DEC_SEQ = 64
PAST_LEN = 1024
CHUNK = 64
RET_HEADS = 8
RET_DK = D_MODEL // RET_HEADS
RET_DV = 2 * RET_DK
RET_QK = RET_HEADS * RET_DK
RET_VD = RET_HEADS * RET_DV
ROPE_BASE = 10000.0
SSM_GROUP = 16
SSM_GROUPS = D_MODEL // SSM_GROUP
SSM_STATE = 64
FFN_DIM = 2 * D_MODEL
CONV_W = 3
EPS = 1e-6

M_PROMPT = SEQ
M_SAMPLE = DEC_BATCH * DEC_SEQ
M_TOTAL = M_PROMPT + M_SAMPLE

LANES = 128
SUBLANES = 8
V7X_VMEM_BYTES = 64 * 1024 * 1024
VMEM_LIMIT_BYTES = V7X_VMEM_BYTES // 8 * 7

BM = M_SAMPLE
N_ROW_TILES = M_TOTAL // BM
N_PROMPT_TILES = M_PROMPT // BM
PROJ_BN = 1024
RET_IN_BN = 2048

RET_T = 256
RET_SUB = 4

SSM_T = 16
SSM_PAIRS = SSM_GROUPS // 2
SSM_PB = 4
SSM_SEGS = 8
SSM_STEPS_P = M_PROMPT // SSM_T // SSM_SEGS
SSM_STEPS_S = DEC_SEQ // SSM_T
SSM_PAIR_CH = 2 * SSM_GROUP
SSM_PAIR_COLS = SSM_T * SSM_PAIR_CH
SSM_VEC_TOKENS = LANES // SSM_PAIR_CH


def _params(n_axes):
    return pltpu.CompilerParams(dimension_semantics=("arbitrary",) * n_axes,
                                vmem_limit_bytes=VMEM_LIMIT_BYTES)


GELU_TANH_SCALE = math.sqrt(2.0 / math.pi)
GELU_TANH_CUBIC = 0.044715


def _sigmoid(x):
    return 1.0 / (1.0 + jnp.exp(-x))


def _rope_kernel(cos_ref, sin_ref):
    m = pl.program_id(0)
    half = RET_DK // 2
    freq = lax.broadcasted_iota(jnp.int32, (1, half), 1).astype(F32)
    inv = ROPE_BASE ** (-freq / half)
    r = m * BM + lax.broadcasted_iota(jnp.int32, (BM, half), 0)
    pos = jnp.where(r < M_PROMPT, r, PAST_LEN + ((r - M_PROMPT) & (DEC_SEQ - 1)))
    ang = pos.astype(F32) * inv
    cos_ref[...] = jnp.cos(ang)
    sin_ref[...] = jnp.sin(ang)


def _rope_tables():
    half = RET_DK // 2
    spec = pl.BlockSpec((BM, half), lambda m: (m, 0))
    return pl.pallas_call(
        _rope_kernel,
        out_shape=(jax.ShapeDtypeStruct((M_TOTAL, half), F32),) * 2,
        grid=(N_ROW_TILES,),
        out_specs=(spec, spec),
        compiler_params=_params(1),
        name="rope_tables",
    )()


def _norm_kernel(x_ref, g_ref, o_ref):
    x = x_ref[...]
    ms = jnp.mean(x * x, axis=-1, keepdims=True)
    o_ref[...] = ((x * lax.rsqrt(ms + EPS)) * g_ref[...]).astype(o_ref.dtype)


def _rmsnorm(x, g, out_dtype, tile0=0, n_tiles=N_ROW_TILES):
    return pl.pallas_call(
        _norm_kernel,
        out_shape=jax.ShapeDtypeStruct((n_tiles * BM, D_MODEL), out_dtype),
        grid=(n_tiles,),
        in_specs=[pl.BlockSpec((BM, D_MODEL), lambda m: (tile0 + m, 0)),
                  pl.BlockSpec((1, D_MODEL), lambda m: (0, 0))],
        out_specs=pl.BlockSpec((BM, D_MODEL), lambda m: (m, 0)),
        compiler_params=_params(1),
        name="rmsnorm",
    )(x, g.reshape(1, D_MODEL))


CHUNKS_PER_TILE = BM // 16
N_CHUNKS = M_TOTAL // 16
N_LANE_TILES = D_MODEL // LANES


def _transpose_lane_blocks(v, block_of_lane):
    n = len(v)
    out = []
    for u in range(n):
        acc = None
        for q in range(n):
            shift = ((q - u) * SSM_PAIR_CH) % LANES
            moved = v[q] if shift == 0 else pltpu.roll(v[q], shift, axis=1)
            acc = moved if acc is None else jnp.where(block_of_lane == q, moved, acc)
        out.append(acc)
    return out


def _norm_chunked_kernel(x_ref, g_ref, o_ref, h_s):
    x = x_ref[...]
    ms = jnp.mean(x * x, axis=-1, keepdims=True)
    h = (x * lax.rsqrt(ms + EPS)) * g_ref[...]
    for j in range(N_LANE_TILES):
        h_s[j] = h[:, j * LANES:(j + 1) * LANES]
    block_of_lane = lax.broadcasted_iota(jnp.int32, (CHUNKS_PER_TILE, LANES), 1) // SSM_PAIR_CH
    for j in range(N_LANE_TILES):
        for tg in range(SSM_T // SSM_VEC_TOKENS):
            rows = [h_s[j, pl.ds(tg * SSM_VEC_TOKENS + u, CHUNKS_PER_TILE, stride=SSM_T), :]
                    for u in range(SSM_VEC_TOKENS)]
            for q, out in enumerate(_transpose_lane_blocks(rows, block_of_lane)):
                c0 = q * SSM_PAIR_COLS + tg * LANES
                o_ref[j, :, c0:c0 + LANES] = out.astype(BF16)


def _rmsnorm_chunked(x, g):
    return pl.pallas_call(
        _norm_chunked_kernel,
        out_shape=jax.ShapeDtypeStruct((N_LANE_TILES, N_CHUNKS, SSM_T * LANES), BF16),
        grid=(N_ROW_TILES,),
        in_specs=[pl.BlockSpec((BM, D_MODEL), lambda m: (m, 0)),
                  pl.BlockSpec((1, D_MODEL), lambda m: (0, 0))],
        out_specs=pl.BlockSpec((N_LANE_TILES, CHUNKS_PER_TILE, SSM_T * LANES), lambda m: (0, m, 0)),
        scratch_shapes=[pltpu.VMEM((N_LANE_TILES, BM, LANES), F32)],
        compiler_params=_params(1),
        name="rmsnorm_chunked",
    )(x, g.reshape(1, D_MODEL))


def _weight_spec(k, bn, layer, col0=0):
    return pl.BlockSpec((None, k, bn), lambda n, m: (layer, 0, col0 + n), pipeline_mode=pl.Buffered(1))


def _emit_norm_inputs(x_new, xb_ref, ssq_ref):
    xb_ref[...] = x_new.astype(BF16)
    sq = x_new * x_new
    part = sq[:, 0:LANES]
    for t in range(1, x_new.shape[1] // LANES):
        part = part + sq[:, t * LANES:(t + 1) * LANES]
    ssq_ref[...] = part


def _row_scale(ssq_ref):
    total = jnp.sum(jnp.sum(ssq_ref[...], axis=0), axis=-1, keepdims=True)
    return lax.rsqrt(total * (1.0 / D_MODEL) + EPS)


def _norm_operands(norm):
    xb, ssq, g = norm
    args = [xb, ssq, g.reshape(D_MODEL, 1)]
    specs = [pl.BlockSpec((BM, D_MODEL), lambda n, m: (m, 0)),
             pl.BlockSpec((ssq.shape[0], BM, LANES), lambda n, m: (0, m, 0)),
             pl.BlockSpec((D_MODEL, 1), lambda n, m: (0, 0))]
    return args, specs


def _norm_outputs(n_out, bn):
    shapes = [jax.ShapeDtypeStruct((M_TOTAL, n_out), BF16),
              jax.ShapeDtypeStruct((n_out // bn, M_TOTAL, LANES), F32)]
    specs = [pl.BlockSpec((BM, bn), lambda n, m: (m, n)),
             pl.BlockSpec((None, BM, LANES), lambda n, m: (n, m, 0))]
    return shapes, specs


def _cast_ssq_kernel(xp_ref, xs_ref, xb_ref, ssq_ref):
    x = jnp.where(pl.program_id(0) < N_PROMPT_TILES, xp_ref[...], xs_ref[...])
    _emit_norm_inputs(x, xb_ref, ssq_ref)


def _cast_ssq(x_prompt, x_sample):
    return pl.pallas_call(
        _cast_ssq_kernel,
        out_shape=(jax.ShapeDtypeStruct((M_TOTAL, D_MODEL), BF16),
                   jax.ShapeDtypeStruct((1, M_TOTAL, LANES), F32)),
        grid=(N_ROW_TILES,),
        in_specs=[pl.BlockSpec((BM, D_MODEL), lambda m: (jnp.minimum(m, N_PROMPT_TILES - 1), 0)),
                  pl.BlockSpec((BM, D_MODEL), lambda m: (0, 0))],
        out_specs=(pl.BlockSpec((BM, D_MODEL), lambda m: (m, 0)),
                   pl.BlockSpec((None, BM, LANES), lambda m: (0, m, 0))),
        compiler_params=_params(1),
        name="cast_ssq",
    )(x_prompt, x_sample)


def _mm_qk_kernel(a_ref, ssq_ref, g_ref, w_ref, cos_ref, sin_ref, o_ref, wb_ref):
    n = pl.program_id(0)
    n_q = RET_QK // RET_IN_BN

    @pl.when(pl.program_id(1) == 0)
    def _():
        wb_ref[...] = (w_ref[...] * g_ref[...]).astype(BF16)

    acc = jnp.dot(a_ref[...], wb_ref[...], preferred_element_type=F32) * _row_scale(ssq_ref)
    scale = jnp.where(n >= n_q, RET_DK ** -0.5, 1.0).astype(F32)
    c = cos_ref[...]
    s = sin_ref[...]
    half = RET_DK // 2
    for hh in range(RET_IN_BN // RET_DK):
        lo = hh * RET_DK
        x1 = acc[:, lo:lo + half]
        x2 = acc[:, lo + half:lo + RET_DK]
        o_ref[:, lo:lo + half] = ((x1 * c - x2 * s) * scale).astype(BF16)
        o_ref[:, lo + half:lo + RET_DK] = ((x1 * s + x2 * c) * scale).astype(BF16)


def _mm_cast_kernel(a_ref, ssq_ref, g_ref, w_ref, o_ref, wb_ref):
    @pl.when(pl.program_id(1) == 0)
    def _():
        wb_ref[...] = (w_ref[...] * g_ref[...]).astype(BF16)

    acc = jnp.dot(a_ref[...], wb_ref[...], preferred_element_type=F32) * _row_scale(ssq_ref)
    o_ref[...] = acc.astype(BF16)


def _mm_retin(norm, w, layer, cos, sin):
    bn = RET_IN_BN
    _, k, n_out = w.shape
    half = RET_DK // 2
    n_qk = 2 * RET_QK // bn
    norm_args, norm_specs = _norm_operands(norm)
    out_spec = pl.BlockSpec((BM, bn), lambda n, m: (m, n))
    rope_spec = pl.BlockSpec((BM, half), lambda n, m: (m, 0))
    qk = pl.pallas_call(
        _mm_qk_kernel,
        out_shape=jax.ShapeDtypeStruct((M_TOTAL, 2 * RET_QK), BF16),
        grid=(n_qk, N_ROW_TILES),
        in_specs=norm_specs + [_weight_spec(k, bn, layer), rope_spec, rope_spec],
        out_specs=out_spec,
        scratch_shapes=[pltpu.VMEM((k, bn), BF16)],
        compiler_params=_params(2),
        name="mm_ret_qk",
    )(*norm_args, w, cos, sin)
    vg = pl.pallas_call(
        _mm_cast_kernel,
        out_shape=jax.ShapeDtypeStruct((M_TOTAL, 2 * RET_VD), BF16),
        grid=(n_out // bn - n_qk, N_ROW_TILES),
        in_specs=norm_specs + [_weight_spec(k, bn, layer, n_qk)],
        out_specs=out_spec,
        scratch_shapes=[pltpu.VMEM((k, bn), BF16)],
        compiler_params=_params(2),
        name="mm_ret_vg",
    )(*norm_args, w)
    return qk, vg


def _row_split_specs(cols, col_index):
    return [pl.BlockSpec((BM, cols), lambda n, m: (jnp.minimum(m, N_PROMPT_TILES - 1), col_index(n))),
            pl.BlockSpec((BM, cols), lambda n, m: (0, col_index(n)))]


def _mm_res_kernel(*refs, split_a, split_x, emit):
    refs = list(refs)
    ap_ref = refs.pop(0)
    as_ref = refs.pop(0) if split_a else None
    w_ref = refs.pop(0)
    xp_ref = refs.pop(0)
    xs_ref = refs.pop(0) if split_x else None
    o_ref = refs.pop(0)
    xb_ref, ssq_ref = (refs.pop(0), refs.pop(0)) if emit else (None, None)
    (wb_ref,) = refs
    m = pl.program_id(1)
    on_prompt = m < N_PROMPT_TILES

    @pl.when(m == 0)
    def _():
        wb_ref[...] = w_ref[...].astype(BF16)

    def body(a_ref):
        x = xp_ref[...]
        if split_x:
            x = jnp.where(on_prompt, x, xs_ref[...])
        x_new = x + jnp.dot(a_ref[...], wb_ref[...], preferred_element_type=F32)
        o_ref[...] = x_new
        if emit:
            _emit_norm_inputs(x_new, xb_ref, ssq_ref)

    if split_a:
        pl.when(on_prompt)(lambda: body(ap_ref))
        pl.when(jnp.logical_not(on_prompt))(lambda: body(as_ref))
    else:
        body(ap_ref)


def _mm_res(a, w, layer, x, emit=False):
    bn = PROJ_BN
    _, k, n_out = w.shape
    split_a = isinstance(a, tuple)
    split_x = isinstance(x, tuple)
    a_specs = _row_split_specs(k, lambda n: 0) if split_a else [pl.BlockSpec((BM, k), lambda n, m: (m, 0))]
    x_specs = _row_split_specs(bn, lambda n: n) if split_x else [pl.BlockSpec((BM, bn), lambda n, m: (m, n))]
    a_args = list(a) if split_a else [a]
    x_args = list(x) if split_x else [x]
    out_shapes = [jax.ShapeDtypeStruct((M_TOTAL, n_out), F32)]
    out_specs = [pl.BlockSpec((BM, bn), lambda n, m: (m, n))]
    if emit:
        shapes, specs = _norm_outputs(n_out, bn)
        out_shapes += shapes
        out_specs += specs
    out = pl.pallas_call(
        functools.partial(_mm_res_kernel, split_a=split_a, split_x=split_x, emit=emit),
        out_shape=tuple(out_shapes),
        grid=(n_out // bn, N_ROW_TILES),
        in_specs=a_specs + [_weight_spec(k, bn, layer)] + x_specs,
        out_specs=tuple(out_specs),
        scratch_shapes=[pltpu.VMEM((k, bn), BF16)],
        compiler_params=_params(2),
        name="mm_residual",
    )(*a_args, w, *x_args)
    return out if emit else out[0]


def _mm_glu_kernel(a_ref, wa_ref, wg_ref, x_ref, o_ref, xb_ref, ssq_ref, wab_ref, wgb_ref):
    @pl.when(pl.program_id(1) == 0)
    def _():
        wab_ref[...] = wa_ref[...].astype(BF16)
        wgb_ref[...] = wg_ref[...].astype(BF16)

    a = a_ref[...]
    ga = jnp.dot(a, wab_ref[...], preferred_element_type=F32)
    gb = jnp.dot(a, wgb_ref[...], preferred_element_type=F32)
    x_new = x_ref[...] + ga * _sigmoid(gb)
    o_ref[...] = x_new
    _emit_norm_inputs(x_new, xb_ref, ssq_ref)


def _mm_glu(a, w, layer, x):
    bn = PROJ_BN
    _, k, n2 = w.shape
    n_out = n2 // 2
    nb = n_out // bn
    norm_shapes, norm_specs = _norm_outputs(n_out, bn)
    return pl.pallas_call(
        _mm_glu_kernel,
        out_shape=(jax.ShapeDtypeStruct((M_TOTAL, n_out), F32), *norm_shapes),
        grid=(nb, N_ROW_TILES),
        in_specs=[
            pl.BlockSpec((BM, k), lambda n, m: (m, 0)),
            _weight_spec(k, bn, layer),
            _weight_spec(k, bn, layer, nb),
            pl.BlockSpec((BM, bn), lambda n, m: (m, n)),
        ],
        out_specs=(pl.BlockSpec((BM, bn), lambda n, m: (m, n)), *norm_specs),
        scratch_shapes=[pltpu.VMEM((k, bn), BF16), pltpu.VMEM((k, bn), BF16)],
        compiler_params=_params(2),
        name="mm_glu",
    )(a, w, w, x)


def _mm_ffnup_kernel(h_ref, ssq_ref, g_ref, wa_ref, wg_ref, cw_ref, cb_ref, cache_ref,
                     o_ref, tp_ref, ts_ref, wab_ref, wgb_ref, carry_ref):
    m = pl.program_id(1)

    @pl.when(m == 0)
    def _():
        g = g_ref[...]
        wab_ref[...] = (wa_ref[...] * g).astype(BF16)
        wgb_ref[...] = (wg_ref[...] * g).astype(BF16)
        carry_ref[...] = jnp.zeros_like(carry_ref)

    h = h_ref[...]
    r = _row_scale(ssq_ref)
    a = jnp.dot(h, wab_ref[...], preferred_element_type=F32) * r
    b = jnp.dot(h, wgb_ref[...], preferred_element_type=F32) * r
    r1 = pltpu.roll(a, 1, axis=0)
    r2 = pltpu.roll(a, 2, axis=0)
    cb = cb_ref[...]
    w0 = cw_ref[0:1, :]
    w1 = cw_ref[1:2, :]
    w2 = cw_ref[2:3, :]

    def gated(a_rows, prev1, prev2, b_rows):
        conv = cb + w0 * prev2 + w1 * prev1 + w2 * a_rows
        return ((conv * _sigmoid(conv)) * b_rows).astype(BF16)

    o_ref[...] = gated(a, r1, r2, b)

    row8 = lax.broadcasted_iota(jnp.int32, (SUBLANES, 1), 0)
    heads = [tuple(z[s * DEC_SEQ:s * DEC_SEQ + SUBLANES] for z in (a, r1, r2, b))
             for s in range(DEC_BATCH)]
    tails = [a[(s + 1) * DEC_SEQ - SUBLANES:(s + 1) * DEC_SEQ] for s in range(DEC_BATCH)]

    def redo_head(s, before1, before2):
        a8, r1_8, r2_8, b8 = heads[s]
        prev1 = jnp.where(row8 == 0, before1, r1_8)
        prev2 = jnp.where(row8 == 0, before2, jnp.where(row8 == 1, before1, r2_8))
        o_ref[s * DEC_SEQ:s * DEC_SEQ + SUBLANES, :] = gated(a8, prev1, prev2, b8)

    @pl.when(m < N_PROMPT_TILES)
    def _():
        c = carry_ref[...]
        redo_head(0, c[SUBLANES - 1:SUBLANES, :], c[SUBLANES - 2:SUBLANES - 1, :])
        carry_ref[...] = tails[DEC_BATCH - 1]

        @pl.when(m == N_PROMPT_TILES - 1)
        def _():
            tp_ref[...] = tails[DEC_BATCH - 1]

    @pl.when(m == N_PROMPT_TILES)
    def _():
        for s in range(DEC_BATCH):
            c = cache_ref[s]
            redo_head(s, c[1:2, :], c[0:1, :])
            ts_ref[s * SUBLANES:(s + 1) * SUBLANES, :] = tails[s]


def _mm_ffnup(norm, w, layer, conv_w, conv_b, cache):
    bn = PROJ_BN
    k = w.shape[1]
    nb = FFN_DIM // bn
    norm_args, norm_specs = _norm_operands(norm)
    return pl.pallas_call(
        _mm_ffnup_kernel,
        out_shape=(
            jax.ShapeDtypeStruct((M_TOTAL, FFN_DIM), BF16),
            jax.ShapeDtypeStruct((SUBLANES, FFN_DIM), F32),
            jax.ShapeDtypeStruct((DEC_BATCH * SUBLANES, FFN_DIM), F32),
        ),
        grid=(nb, N_ROW_TILES),
        in_specs=norm_specs + [
            _weight_spec(k, bn, layer),
            _weight_spec(k, bn, layer, nb),
            pl.BlockSpec((CONV_W, bn), lambda n, m: (0, n)),
            pl.BlockSpec((1, bn), lambda n, m: (0, n)),
            pl.BlockSpec((DEC_BATCH, CONV_W - 1, bn), lambda n, m: (0, 0, n)),
        ],
        out_specs=(
            pl.BlockSpec((BM, bn), lambda n, m: (m, n)),
            pl.BlockSpec((SUBLANES, bn), lambda n, m: (0, n)),
            pl.BlockSpec((DEC_BATCH * SUBLANES, bn), lambda n, m: (0, n)),
        ),
        scratch_shapes=[pltpu.VMEM((k, bn), BF16), pltpu.VMEM((k, bn), BF16),
                        pltpu.VMEM((SUBLANES, bn), F32)],
        compiler_params=_params(2),
        name="mm_ffn_up",
    )(*norm_args, w, w, conv_w, conv_b.reshape(1, FFN_DIM), cache)


def _ret_tables(lg, T):
    i = lax.broadcasted_iota(jnp.int32, (T, T), 0)
    j = lax.broadcasted_iota(jnp.int32, (T, T), 1)
    dist = jnp.abs(i - j).astype(F32)
    shift = int(math.log2(CHUNK))
    visible = (j >> shift) <= (i >> shift)
    mask = jnp.where(visible, jnp.exp(dist * lg), 0.0)
    t = lax.broadcasted_iota(jnp.int32, (T, 1), 0).astype(F32)
    cross = jnp.exp((t + 1.0) * lg)
    k_dec = jnp.exp((T - 1.0 - t) * lg)
    decay = jnp.exp(float(T) * lg)
    return mask, cross, k_dec, decay


def _ret_block(q, k, v, g, state, tables, gn):
    mask, cross, k_dec, decay = tables
    scores = lax.dot_general(q, k, (((1,), (1,)), ((), ())), preferred_element_type=F32)
    scores = scores * mask
    out = jnp.dot(scores.astype(BF16), v, preferred_element_type=F32)
    out = out + jnp.dot(q, state.astype(BF16), preferred_element_type=F32) * cross

    kd = (k.astype(F32) * k_dec).astype(BF16)
    new_state = decay * state + lax.dot_general(
        kd, v, (((0,), (0,)), ((), ())), preferred_element_type=F32)

    mu = jnp.mean(out, axis=-1, keepdims=True)
    oc = out - mu
    var = jnp.mean(oc * oc, axis=-1, keepdims=True)
    normed = (oc * lax.rsqrt(var + EPS)) * gn
    g = g.astype(F32)
    return (normed * (g * _sigmoid(g))).astype(BF16), new_state


def _ret_prompt_kernel(lg_ref, q_ref, k_ref, v_ref, g_ref, gn_ref, o_ref, sl_ref, state_ref, mask_ref):
    blk = pl.program_id(1)
    lg = lg_ref[...][:, 0:1]

    @pl.when(blk == 0)
    def _():
        state_ref[...] = jnp.zeros_like(state_ref)
        mask_ref[...] = _ret_tables(lg, RET_T)[0]

    _, cross, k_dec, decay = _ret_tables(lg, RET_T)
    tables = (mask_ref[...], cross, k_dec, decay)
    gn = gn_ref[...]
    state = state_ref[...]
    for sb in range(RET_SUB):
        rows = slice(sb * RET_T, (sb + 1) * RET_T)
        o_ref[rows, :], state = _ret_block(q_ref[rows, :], k_ref[rows, :], v_ref[rows, :],
                                           g_ref[rows, :], state, tables, gn)
    state_ref[...] = state

    @pl.when(blk == pl.num_programs(1) - 1)
    def _():
        sl_ref[...] = state


def _ret_sample_kernel(lg_ref, q_ref, k_ref, v_ref, g_ref, gn_ref, s0_ref, o_ref, sl_ref):
    lg = lg_ref[...][:, 0:1]
    tables = _ret_tables(lg, DEC_SEQ)
    gn = gn_ref[...]
    for s in range(DEC_BATCH):
        rows = slice(s * DEC_SEQ, (s + 1) * DEC_SEQ)
        o_ref[rows, :], sl_ref[s] = _ret_block(q_ref[rows, :], k_ref[rows, :], v_ref[rows, :],
                                               g_ref[rows, :], s0_ref[s], tables, gn)


def _retention(qk, vg, gn, s0_stack, layer):
    log_g = np.log1p(-np.exp2(-5.0 - np.arange(RET_HEADS, dtype=np.float32))).astype(np.float32)
    lg = jnp.asarray(np.broadcast_to(log_g[:, None, None], (RET_HEADS, 1, LANES)).copy())
    gn = gn.reshape(1, RET_VD)
    k_off = RET_QK // RET_DK
    g_off = RET_VD // RET_DV
    rows_p = RET_T * RET_SUB
    o_p, s_p = pl.pallas_call(
        _ret_prompt_kernel,
        out_shape=(jax.ShapeDtypeStruct((M_PROMPT, RET_VD), BF16),
                   jax.ShapeDtypeStruct((1, RET_HEADS, RET_DK, RET_DV), F32)),
        grid=(RET_HEADS, M_PROMPT // rows_p),
        in_specs=[
            pl.BlockSpec((None, 1, LANES), lambda h, b: (h, 0, 0)),
            pl.BlockSpec((rows_p, RET_DK), lambda h, b: (b, h)),
            pl.BlockSpec((rows_p, RET_DK), lambda h, b: (b, k_off + h)),
            pl.BlockSpec((rows_p, RET_DV), lambda h, b: (b, h)),
            pl.BlockSpec((rows_p, RET_DV), lambda h, b: (b, g_off + h)),
            pl.BlockSpec((1, RET_DV), lambda h, b: (0, h)),
        ],
        out_specs=(pl.BlockSpec((rows_p, RET_DV), lambda h, b: (b, h)),
                   pl.BlockSpec((None, None, RET_DK, RET_DV), lambda h, b: (0, h, 0, 0))),
        scratch_shapes=[pltpu.VMEM((RET_DK, RET_DV), F32), pltpu.VMEM((RET_T, RET_T), F32)],
        compiler_params=_params(2),
        name="retention",
    )(lg, qk, qk, vg, vg, gn)
    sample_block = M_PROMPT // M_SAMPLE
    o_s, s_s = pl.pallas_call(
        _ret_sample_kernel,
        out_shape=(jax.ShapeDtypeStruct((M_SAMPLE, RET_VD), BF16),
                   jax.ShapeDtypeStruct((DEC_BATCH, RET_HEADS, RET_DK, RET_DV), F32)),
        grid=(RET_HEADS,),
        in_specs=[
            pl.BlockSpec((None, 1, LANES), lambda h: (h, 0, 0)),
            pl.BlockSpec((M_SAMPLE, RET_DK), lambda h: (sample_block, h)),
            pl.BlockSpec((M_SAMPLE, RET_DK), lambda h: (sample_block, k_off + h)),
            pl.BlockSpec((M_SAMPLE, RET_DV), lambda h: (sample_block, h)),
            pl.BlockSpec((M_SAMPLE, RET_DV), lambda h: (sample_block, g_off + h)),
            pl.BlockSpec((1, RET_DV), lambda h: (0, h)),
            pl.BlockSpec((None, DEC_BATCH, None, RET_DK, RET_DV), lambda h: (layer, 0, h, 0, 0)),
        ],
        out_specs=(pl.BlockSpec((M_SAMPLE, RET_DV), lambda h: (0, h)),
                   pl.BlockSpec((DEC_BATCH, None, RET_DK, RET_DV), lambda h: (0, h, 0, 0))),
        compiler_params=_params(1),
        name="retention_sample",
    )(lg, qk, qk, vg, vg, gn, s0_stack)
    return o_p, o_s, s_p, s_s


def _ssm_prep_kernel(ar_ref, ai_ref, ldt_ref, br_ref, bi_ref, cr_ref, ci_ref,
                     bc_ref, cc_ref, w_ref, a16r_ref, a16i_ref, vr_s, vi_s):
    ar = ar_ref[...]
    ai = ai_ref[...]
    dt = jnp.exp(ldt_ref[...])
    mag = jnp.exp(ar * dt)
    ang = ai * dt
    abr = mag * jnp.cos(ang)
    abi = mag * jnp.sin(ang)
    den = ar * ar + ai * ai
    nr = abr - 1.0
    ni = abi
    cfr = (nr * ar + ni * ai) / den
    cfi = (ni * ar - nr * ai) / den

    def b4(z):
        return z[:, None]

    br = br_ref[...]
    bi = bi_ref[...]
    bbr = b4(cfr) * br - b4(cfi) * bi
    bbi = b4(cfr) * bi + b4(cfi) * br
    cre = cr_ref[...]
    cim = ci_ref[...]

    powers = []
    pr = jnp.ones_like(ar)
    pi = jnp.zeros_like(ar)
    for _ in range(SSM_T + 1):
        powers.append((pr, pi))
        pr, pi = pr * abr - pi * abi, pr * abi + pi * abr
    a16r_ref[...] = powers[SSM_T][0]
    a16i_ref[...] = powers[SSM_T][1]

    for m in range(SSM_T + 1):
        pr, pi = powers[m]
        vr = b4(pr) * cre - b4(pi) * cim
        vi = b4(pr) * cim + b4(pi) * cre
        if m < SSM_T:
            vr_s[:, m] = vr
            vi_s[:, m] = vi
        if m >= 1:
            cc_ref[:, m - 1, :, :, 0:LANES] = vr.astype(BF16)
            cc_ref[:, m - 1, :, :, LANES:2 * LANES] = (-vi).astype(BF16)

    for t in range(SSM_T):
        pr, pi = powers[SSM_T - 1 - t]
        bc_ref[:, t, :, :, 0:LANES] = (b4(pr) * bbr - b4(pi) * bbi).astype(BF16)
        bc_ref[:, t, :, :, LANES:2 * LANES] = (b4(pr) * bbi + b4(pi) * bbr).astype(BF16)

    lane = lax.broadcasted_iota(jnp.int32, (SSM_PAIR_CH, SSM_PAIR_COLS), 1)
    contract_last = (((1,), (1,)), ((), ()))
    for q in range(SSM_PB):
        vr_all = vr_s[q].reshape(SSM_PAIR_COLS, LANES)
        vi_all = vi_s[q].reshape(SSM_PAIR_COLS, LANES)
        lags = (lax.dot_general(bbr[q].reshape(SSM_PAIR_CH, LANES), vr_all, contract_last,
                                precision=lax.Precision.HIGHEST, preferred_element_type=F32)
                - lax.dot_general(bbi[q].reshape(SSM_PAIR_CH, LANES), vi_all, contract_last,
                                  precision=lax.Precision.HIGHEST, preferred_element_type=F32))
        for t in range(SSM_T):
            shifted = lags if t == 0 else pltpu.roll(lags, SSM_PAIR_CH * t, axis=1)
            w_ref[q, t * SSM_PAIR_CH:(t + 1) * SSM_PAIR_CH, :] = jnp.where(
                lane >= SSM_PAIR_CH * t, shifted, 0.0).astype(BF16)


def _pair_pack(x):
    x = x.reshape(SSM_PAIRS, 2, SSM_GROUP, SSM_STATE)
    lo = jnp.pad(x[:, 0], ((0, 0), (0, 0), (0, SSM_STATE)))
    hi = jnp.pad(x[:, 1], ((0, 0), (0, 0), (SSM_STATE, 0)))
    return jnp.stack([lo, hi], axis=1)


def _ssm_prep(a_re, a_im, log_dt, b_re, b_im, c_re, c_im):
    assert 2 * SSM_PB * SSM_GROUP == LANES
    ar = a_re.reshape(SSM_PAIRS, 1, LANES)
    ai = a_im.reshape(SSM_PAIRS, 1, LANES)
    ldt = jnp.repeat(log_dt, SSM_STATE).reshape(SSM_PAIRS, 1, LANES)
    b2r = _pair_pack(jnp.transpose(b_re, (0, 2, 1)))
    b2i = _pair_pack(jnp.transpose(b_im, (0, 2, 1)))
    c2r = _pair_pack(c_re)
    c2i = _pair_pack(c_im)
    vec = pl.BlockSpec((SSM_PB, 1, LANES), lambda p: (p, 0, 0))
    mat = pl.BlockSpec((SSM_PB, 2, SSM_GROUP, LANES), lambda p: (p, 0, 0, 0))
    proj = pl.BlockSpec((SSM_PB, SSM_T, 2, SSM_GROUP, 2 * LANES), lambda p: (p, 0, 0, 0, 0))
    proj_shape = jax.ShapeDtypeStruct((SSM_PAIRS, SSM_T, 2, SSM_GROUP, 2 * LANES), BF16)
    bc, cc, w, a16r, a16i = pl.pallas_call(
        _ssm_prep_kernel,
        out_shape=(
            proj_shape, proj_shape,
            jax.ShapeDtypeStruct((SSM_PAIRS, SSM_PAIR_COLS, SSM_PAIR_COLS), BF16),
            jax.ShapeDtypeStruct((SSM_PAIRS, 1, LANES), F32),
            jax.ShapeDtypeStruct((SSM_PAIRS, 1, LANES), F32),
        ),
        grid=(N_LANE_TILES,),
        in_specs=[vec, vec, vec, mat, mat, mat, mat],
        out_specs=(proj, proj,
                   pl.BlockSpec((SSM_PB, SSM_PAIR_COLS, SSM_PAIR_COLS), lambda p: (p, 0, 0)),
                   vec, vec),
        scratch_shapes=[pltpu.VMEM((SSM_PB, SSM_T, 2, SSM_GROUP, LANES), F32)] * 2,
        compiler_params=_params(1),
        name="ssm_prep",
    )(ar, ai, ldt, b2r, b2i, c2r, c2i)
    bc = bc.reshape(SSM_PAIRS, SSM_PAIR_COLS, 2 * LANES)
    cc = cc.reshape(SSM_PAIRS, SSM_PAIR_COLS, 2 * LANES)
    row = SSM_GROUPS * SSM_STATE
    a16r = jnp.broadcast_to(a16r.reshape(1, row), (SSM_SEGS, row))
    a16i = jnp.broadcast_to(a16i.reshape(1, row), (SSM_SEGS, row))
    return bc, cc, w, a16r, a16i


SSM_CHUNK_COLS = SSM_T * LANES
SSM_STATE_COLS = SSM_PB * LANES
SSM_CHUNKS_P = M_PROMPT // SSM_T
SSM_SCAN_UNROLL = 8


def _ssm_main_kernel(x_ref, w_ref, bc_ref, cc_ref, ar_ref, ai_ref, h0r_ref, h0i_ref,
                     y_ref, fpr_ref, fpi_ref, fsr_ref, fsi_ref, d_s, s_s, t_s):
    pair_cols = [slice(q * SSM_PAIR_COLS, (q + 1) * SSM_PAIR_COLS) for q in range(SSM_PB)]

    for q in range(SSM_PB):
        d = jnp.dot(x_ref[:, pair_cols[q]], bc_ref[q], preferred_element_type=F32)
        d_s[q] = d[:, 0:LANES]
        d_s[SSM_PB + q] = d[:, LANES:2 * LANES]

    tiles = range(SSM_PB)
    ar = [ar_ref[:, k * LANES:(k + 1) * LANES] for k in tiles]
    ai = [ai_ref[:, k * LANES:(k + 1) * LANES] for k in tiles]

    def advance(sr, si, rows, record=None):
        out_r, out_i = [], []
        for k in tiles:
            if record is not None:
                ref, rec_rows = record
                ref[k, rec_rows, :] = sr[k]
                ref[SSM_PB + k, rec_rows, :] = si[k]
            dr = d_s[k, rows, :]
            di = d_s[SSM_PB + k, rows, :]
            out_r.append(ar[k] * sr[k] - ai[k] * si[k] + dr)
            out_i.append(ar[k] * si[k] + ai[k] * sr[k] + di)
        return out_r, out_i

    def prompt_rows(i):
        return pl.ds(i, SSM_SEGS, stride=SSM_STEPS_P)

    def prompt_pass(start_r, start_i, record):
        def body(i, carry):
            rec = (t_s, pl.ds(pl.multiple_of(i * SSM_SEGS, SSM_SEGS), SSM_SEGS)) if record else None
            out_r, out_i = advance(list(carry[0]), list(carry[1]), prompt_rows(i), rec)
            return tuple(out_r), tuple(out_i)
        out_r, out_i = lax.fori_loop(0, SSM_STEPS_P, body, (tuple(start_r), tuple(start_i)),
                                     unroll=SSM_SCAN_UNROLL)
        return list(out_r), list(out_i)

    zeros = [jnp.zeros((SSM_SEGS, LANES), F32) for _ in tiles]
    er, ei = prompt_pass(zeros, zeros, record=False)

    sr, si = [], []
    for k in tiles:
        pr, pi = ar[k][0:1], ai[k][0:1]
        for _ in range(int(math.log2(SSM_STEPS_P))):
            pr, pi = pr * pr - pi * pi, 2.0 * pr * pi
        cr = jnp.zeros((1, LANES), F32)
        ci = cr
        starts_r, starts_i = [], []
        for s in range(SSM_SEGS):
            starts_r.append(cr)
            starts_i.append(ci)
            cr, ci = (pr * cr - pi * ci + er[k][s:s + 1], pr * ci + pi * cr + ei[k][s:s + 1])
        cols = slice(k * LANES, (k + 1) * LANES)
        fpr_ref[:, cols] = jnp.broadcast_to(cr, (SSM_SEGS, LANES))
        fpi_ref[:, cols] = jnp.broadcast_to(ci, (SSM_SEGS, LANES))
        sr.append(jnp.concatenate(starts_r, axis=0))
        si.append(jnp.concatenate(starts_i, axis=0))

    prompt_pass(sr, si, record=True)
    for k in range(2 * SSM_PB):
        for s in range(SSM_SEGS):
            s_s[k, s * SSM_STEPS_P:(s + 1) * SSM_STEPS_P, :] = t_s[k, pl.ds(s, SSM_STEPS_P, stride=SSM_SEGS), :]
    sr = [h0r_ref[:, k * LANES:(k + 1) * LANES] for k in tiles]
    si = [h0i_ref[:, k * LANES:(k + 1) * LANES] for k in tiles]
    for i in range(SSM_STEPS_S):
        rows = pl.ds(SSM_CHUNKS_P + i, DEC_BATCH, stride=SSM_STEPS_S)
        sr, si = advance(sr, si, rows, (s_s, rows))
    for k in tiles:
        fsr_ref[:, k * LANES:(k + 1) * LANES] = sr[k]
        fsi_ref[:, k * LANES:(k + 1) * LANES] = si[k]

    contract_last = (((1,), (1,)), ((), ()))
    for q in range(SSM_PB):
        states = jnp.concatenate([s_s[q].astype(BF16), s_s[SSM_PB + q].astype(BF16)], axis=1)
        carried = lax.dot_general(states, cc_ref[q], contract_last, preferred_element_type=F32)
        local = jnp.dot(x_ref[:, pair_cols[q]], w_ref[q], preferred_element_type=F32)
        y_ref[:, pair_cols[q]] = local + carried


def _ssm_main(x_c, w, bc, cc, a16r, a16i, h0r, h0i):
    row = SSM_GROUPS * SSM_STATE
    st = pl.BlockSpec((SSM_SEGS, SSM_STATE_COLS), lambda j: (0, j))
    st_shape = jax.ShapeDtypeStruct((SSM_SEGS, row), F32)
    proj = pl.BlockSpec((SSM_PB, SSM_PAIR_COLS, 2 * LANES), lambda j: (j, 0, 0))
    chunk_rows = pl.BlockSpec((None, N_CHUNKS, SSM_CHUNK_COLS), lambda j: (j, 0, 0))
    return pl.pallas_call(
        _ssm_main_kernel,
        out_shape=(jax.ShapeDtypeStruct((N_LANE_TILES, N_CHUNKS, SSM_CHUNK_COLS), F32),
                   st_shape, st_shape, st_shape, st_shape),
        grid=(N_LANE_TILES,),
        in_specs=[
            chunk_rows,
            pl.BlockSpec((SSM_PB, SSM_PAIR_COLS, SSM_PAIR_COLS), lambda j: (j, 0, 0)),
            proj, proj, st, st, st, st,
        ],
        out_specs=(chunk_rows, st, st, st, st),
        scratch_shapes=[
            pltpu.VMEM((2 * SSM_PB, N_CHUNKS, LANES), F32),
            pltpu.VMEM((2 * SSM_PB, N_CHUNKS, LANES), F32),
            pltpu.VMEM((2 * SSM_PB, SSM_CHUNKS_P, LANES), F32),
        ],
        compiler_params=_params(1),
        name="ssm_main",
    )(x_c, w, bc, cc, a16r, a16i, h0r, h0i)


def _ssm_post_kernel(y_ref, x_ref, g_ref, d_ref, o_ref, y_s):
    block_of_lane = lax.broadcasted_iota(jnp.int32, (CHUNKS_PER_TILE, LANES), 1) // SSM_PAIR_CH
    for j in range(N_LANE_TILES):
        for tg in range(SSM_T // SSM_VEC_TOKENS):
            pieces = [y_ref[j, :, q * SSM_PAIR_COLS + tg * LANES:q * SSM_PAIR_COLS + (tg + 1) * LANES]
                      for q in range(SSM_PB)]
            for u, out in enumerate(_transpose_lane_blocks(pieces, block_of_lane)):
                y_s[j, pl.ds(tg * SSM_VEC_TOKENS + u, CHUNKS_PER_TILE, stride=SSM_T), :] = out
    x = x_ref[...]
    ms = jnp.mean(x * x, axis=-1, keepdims=True)
    h = (x * lax.rsqrt(ms + EPS)) * g_ref[...]
    for j in range(N_LANE_TILES):
        cols = slice(j * LANES, (j + 1) * LANES)
        y = y_s[j] + d_ref[:, cols] * h[:, cols]
        gelu = 0.5 * y * (1.0 + jnp.tanh(GELU_TANH_SCALE * (y + GELU_TANH_CUBIC * (y * y * y))))
        o_ref[:, cols] = gelu.astype(BF16)


def _ssm_post(y_c, x, g, d):
    spec = pl.BlockSpec((BM, D_MODEL), lambda m: (m, 0))
    vec = pl.BlockSpec((1, D_MODEL), lambda m: (0, 0))
    return pl.pallas_call(
        _ssm_post_kernel,
        out_shape=jax.ShapeDtypeStruct((M_TOTAL, D_MODEL), BF16),
        grid=(N_ROW_TILES,),
        in_specs=[pl.BlockSpec((N_LANE_TILES, CHUNKS_PER_TILE, SSM_CHUNK_COLS), lambda m: (0, m, 0)),
                  spec, vec, vec],
        out_specs=spec,
        scratch_shapes=[pltpu.VMEM((N_LANE_TILES, BM, LANES), F32)],
        compiler_params=_params(1),
        name="ssm_post",
    )(y_c, x, g.reshape(1, D_MODEL), d.reshape(1, D_MODEL))


def kernel(x_prompt, x_sample, state_ret, state_ssm_re, state_ssm_im, cache_conv, norm_mix, norm_ffn, norm_final, ret_w_in, ret_gn, ret_w_out, ssm_a_re, ssm_a_im, ssm_log_dt, ssm_b_re, ssm_b_im, ssm_c_re, ssm_c_im, ssm_d, ssm_w_glu, ffn_w_up, ffn_conv_w, ffn_conv_b, ffn_w_down):
    x = (x_prompt.reshape(M_PROMPT, D_MODEL), x_sample.reshape(M_SAMPLE, D_MODEL))
    xb, ssq = _cast_ssq(*x)
    cos, sin = _rope_tables()

    ret_p, ret_s = [], []
    re_p, im_p, re_s, im_s = [], [], [], []
    conv_p, conv_s = [], []
    state_row = SSM_GROUPS * SSM_STATE
    for i in range(DEPTH):
        j = i // 2
        if i % 2 == 0:
            qk, vg = _mm_retin((xb, ssq, norm_mix[i]), ret_w_in, j, cos, sin)
            o_p, o_s, s_p, s_s = _retention(qk, vg, ret_gn[j], state_ret, j)
            ret_p.append(s_p)
            ret_s.append(s_s)
            x, xb, ssq = _mm_res((o_p, o_s), ret_w_out, j, x, emit=True)
        else:
            h_c = _rmsnorm_chunked(x, norm_mix[i])
            bc, cc, w_ssm, a16r, a16i = _ssm_prep(ssm_a_re[j], ssm_a_im[j], ssm_log_dt[j],
                                               ssm_b_re[j], ssm_b_im[j], ssm_c_re[j], ssm_c_im[j])
            y_c, fpr, fpi, fsr, fsi = _ssm_main(
                h_c, w_ssm, bc, cc, a16r, a16i,
                state_ssm_re[j].reshape(DEC_BATCH, state_row),
                state_ssm_im[j].reshape(DEC_BATCH, state_row))
            re_p.append(fpr[0:1].reshape(1, SSM_GROUPS, SSM_STATE))
            im_p.append(fpi[0:1].reshape(1, SSM_GROUPS, SSM_STATE))
            re_s.append(fsr.reshape(DEC_BATCH, SSM_GROUPS, SSM_STATE))
            im_s.append(fsi.reshape(DEC_BATCH, SSM_GROUPS, SSM_STATE))
            gl = _ssm_post(y_c, x, norm_mix[i], ssm_d[j])
            x, xb, ssq = _mm_glu(gl, ssm_w_glu, j, x)

        u, tail_p, tail_s = _mm_ffnup((xb, ssq, norm_ffn[i]), ffn_w_up, i, ffn_conv_w[i],
                                      ffn_conv_b[i], cache_conv[i])
        conv_p.append(tail_p[SUBLANES - (CONV_W - 1):].reshape(1, CONV_W - 1, FFN_DIM))
        conv_s.append(tail_s.reshape(DEC_BATCH, SUBLANES, FFN_DIM)[:, SUBLANES - (CONV_W - 1):])
        if i + 1 < DEPTH and (i + 1) % 2 == 0:
            x, xb, ssq = _mm_res(u, ffn_w_down, i, x, emit=True)
        else:
            x = _mm_res(u, ffn_w_down, i, x)

    y_prompt = _rmsnorm(x, norm_final, F32, 0, N_PROMPT_TILES).reshape(1, SEQ, D_MODEL)
    y_sample = _rmsnorm(x, norm_final, F32, N_PROMPT_TILES, 1).reshape(DEC_BATCH, DEC_SEQ, D_MODEL)
    return (y_prompt, y_sample, jnp.stack(ret_p), jnp.stack(ret_s),
            jnp.stack(re_p), jnp.stack(im_p), jnp.stack(re_s), jnp.stack(im_s),
            jnp.stack(conv_p), jnp.stack(conv_s))
```

```python
import functools
import math

import numpy as np
import jax
import jax.numpy as jnp
from jax import lax
from jax.experimental import pallas as pl
from jax.experimental.pallas import tpu as pltpu

F32 = jnp.float32
BF16 = jnp.bfloat16

D_MODEL = 2048
SEQ = 8192
DEPTH = 4
DEC_BATCH = 8
DEC_SEQ = 64
PAST_LEN = 1024
CHUNK = 64
RET_HEADS = 8
RET_DK = D_MODEL // RET_HEADS
RET_DV = 2 * RET_DK
RET_QK = RET_HEADS * RET_DK
RET_VD = RET_HEADS * RET_DV
N_RET = (DEPTH + 1) // 2
ROPE_BASE = 10000.0
SSM_GROUP = 16
SSM_GROUPS = D_MODEL // SSM_GROUP
SSM_STATE = 64
FFN_DIM = 2 * D_MODEL
CONV_W = 3
EPS = 1e-6

M_PROMPT = SEQ
M_SAMPLE = DEC_BATCH * DEC_SEQ
M_TOTAL = M_PROMPT + M_SAMPLE

LANES = 128
SUBLANES = 8
V7X_VMEM_BYTES = 64 * 1024 * 1024
VMEM_LIMIT_BYTES = V7X_VMEM_BYTES // 8 * 7

BM = M_SAMPLE
N_ROW_TILES = M_TOTAL // BM
N_PROMPT_TILES = M_PROMPT // BM
PROJ_BN = 1024
RET_IN_BN = 2048

RET_T = 256
RET_SUB = 4

SSM_T = 16
SSM_PAIRS = SSM_GROUPS // 2
SSM_PB = 4
SSM_SEGS = 8
SSM_STEPS_P = M_PROMPT // SSM_T // SSM_SEGS
SSM_STEPS_S = DEC_SEQ // SSM_T
SSM_PAIR_CH = 2 * SSM_GROUP
SSM_PAIR_COLS = SSM_T * SSM_PAIR_CH
SSM_VEC_TOKENS = LANES // SSM_PAIR_CH


def _params(n_axes):
    return pltpu.CompilerParams(dimension_semantics=("arbitrary",) * n_axes,
                                vmem_limit_bytes=VMEM_LIMIT_BYTES)


GELU_TANH_SCALE = math.sqrt(2.0 / math.pi)
GELU_TANH_CUBIC = 0.044715


def _sigmoid(x):
    return 1.0 / (1.0 + jnp.exp(-x))


def _rope_kernel(cos_ref, sin_ref):
    m = pl.program_id(0)
    half = RET_DK // 2
    freq = lax.broadcasted_iota(jnp.int32, (1, half), 1).astype(F32)
    inv = ROPE_BASE ** (-freq / half)
    r = m * BM + lax.broadcasted_iota(jnp.int32, (BM, half), 0)
    pos = jnp.where(r < M_PROMPT, r, PAST_LEN + ((r - M_PROMPT) & (DEC_SEQ - 1)))
    ang = pos.astype(F32) * inv
    cos_ref[...] = jnp.cos(ang)
    sin_ref[...] = jnp.sin(ang)


def _rope_tables():
    half = RET_DK // 2
    spec = pl.BlockSpec((BM, half), lambda m: (m, 0))
    return pl.pallas_call(
        _rope_kernel,
        out_shape=(jax.ShapeDtypeStruct((M_TOTAL, half), F32),) * 2,
        grid=(N_ROW_TILES,),
        out_specs=(spec, spec),
        compiler_params=_params(1),
        name="rope_tables",
    )()


def _norm_kernel(x_ref, g_ref, o_ref):
    x = x_ref[...]
    ms = jnp.mean(x * x, axis=-1, keepdims=True)
    o_ref[...] = ((x * lax.rsqrt(ms + EPS)) * g_ref[...]).astype(o_ref.dtype)


def _rmsnorm(x, g, out_dtype, tile0=0, n_tiles=N_ROW_TILES):
    return pl.pallas_call(
        _norm_kernel,
        out_shape=jax.ShapeDtypeStruct((n_tiles * BM, D_MODEL), out_dtype),
        grid=(n_tiles,),
        in_specs=[pl.BlockSpec((BM, D_MODEL), lambda m: (tile0 + m, 0)),
                  pl.BlockSpec((1, D_MODEL), lambda m: (0, 0))],
        out_specs=pl.BlockSpec((BM, D_MODEL), lambda m: (m, 0)),
        compiler_params=_params(1),
        name="rmsnorm",
    )(x, g.reshape(1, D_MODEL))


CHUNKS_PER_TILE = BM // 16
N_CHUNKS = M_TOTAL // 16
N_LANE_TILES = D_MODEL // LANES


def _transpose_lane_blocks(v, block_of_lane):
    n = len(v)
    out = []
    for u in range(n):
        acc = None
        for q in range(n):
            shift = ((q - u) * SSM_PAIR_CH) % LANES
            moved = v[q] if shift == 0 else pltpu.roll(v[q], shift, axis=1)
            acc = moved if acc is None else jnp.where(block_of_lane == q, moved, acc)
        out.append(acc)
    return out


def _norm_chunked_kernel(x_ref, g_ref, o_ref, h_s):
    x = x_ref[...]
    ms = jnp.mean(x * x, axis=-1, keepdims=True)
    h = (x * lax.rsqrt(ms + EPS)) * g_ref[...]
    for j in range(N_LANE_TILES):
        h_s[j] = h[:, j * LANES:(j + 1) * LANES]
    block_of_lane = lax.broadcasted_iota(jnp.int32, (CHUNKS_PER_TILE, LANES), 1) // SSM_PAIR_CH
    for j in range(N_LANE_TILES):
        for tg in range(SSM_T // SSM_VEC_TOKENS):
            rows = [h_s[j, pl.ds(tg * SSM_VEC_TOKENS + u, CHUNKS_PER_TILE, stride=SSM_T), :]
                    for u in range(SSM_VEC_TOKENS)]
            for q, out in enumerate(_transpose_lane_blocks(rows, block_of_lane)):
                c0 = q * SSM_PAIR_COLS + tg * LANES
                o_ref[j, :, c0:c0 + LANES] = out.astype(BF16)


def _rmsnorm_chunked(x, g):
    return pl.pallas_call(
        _norm_chunked_kernel,
        out_shape=jax.ShapeDtypeStruct((N_LANE_TILES, N_CHUNKS, SSM_T * LANES), BF16),
        grid=(N_ROW_TILES,),
        in_specs=[pl.BlockSpec((BM, D_MODEL), lambda m: (m, 0)),
                  pl.BlockSpec((1, D_MODEL), lambda m: (0, 0))],
        out_specs=pl.BlockSpec((N_LANE_TILES, CHUNKS_PER_TILE, SSM_T * LANES), lambda m: (0, m, 0)),
        scratch_shapes=[pltpu.VMEM((N_LANE_TILES, BM, LANES), F32)],
        compiler_params=_params(1),
        name="rmsnorm_chunked",
    )(x, g.reshape(1, D_MODEL))


def _weight_spec(k, bn, layer, col0=0):
    return pl.BlockSpec((None, k, bn), lambda n, m: (layer, 0, col0 + n), pipeline_mode=pl.Buffered(1))


def _emit_norm_inputs(x_new, xb_ref, ssq_ref):
    xb_ref[...] = x_new.astype(BF16)
    sq = x_new * x_new
    part = sq[:, 0:LANES]
    for t in range(1, x_new.shape[1] // LANES):
        part = part + sq[:, t * LANES:(t + 1) * LANES]
    ssq_ref[...] = part


def _row_scale(ssq_ref):
    total = jnp.sum(jnp.sum(ssq_ref[...], axis=0), axis=-1, keepdims=True)
    return lax.rsqrt(total * (1.0 / D_MODEL) + EPS)


def _norm_operands(norm):
    xb, ssq, g = norm
    args = [xb, ssq, g.reshape(D_MODEL, 1)]
    specs = [pl.BlockSpec((BM, D_MODEL), lambda n, m: (m, 0)),
             pl.BlockSpec((ssq.shape[0], BM, LANES), lambda n, m: (0, m, 0)),
             pl.BlockSpec((D_MODEL, 1), lambda n, m: (0, 0))]
    return args, specs


def _norm_outputs(n_out, bn):
    shapes = [jax.ShapeDtypeStruct((M_TOTAL, n_out), BF16),
              jax.ShapeDtypeStruct((n_out // bn, M_TOTAL, LANES), F32)]
    specs = [pl.BlockSpec((BM, bn), lambda n, m: (m, n)),
             pl.BlockSpec((None, BM, LANES), lambda n, m: (n, m, 0))]
    return shapes, specs


def _cast_ssq_kernel(xp_ref, xs_ref, xb_ref, ssq_ref):
    x = jnp.where(pl.program_id(0) < N_PROMPT_TILES, xp_ref[...], xs_ref[...])
    _emit_norm_inputs(x, xb_ref, ssq_ref)


def _cast_ssq(x_prompt, x_sample):
    return pl.pallas_call(
        _cast_ssq_kernel,
        out_shape=(jax.ShapeDtypeStruct((M_TOTAL, D_MODEL), BF16),
                   jax.ShapeDtypeStruct((1, M_TOTAL, LANES), F32)),
        grid=(N_ROW_TILES,),
        in_specs=[pl.BlockSpec((BM, D_MODEL), lambda m: (jnp.minimum(m, N_PROMPT_TILES - 1), 0)),
                  pl.BlockSpec((BM, D_MODEL), lambda m: (0, 0))],
        out_specs=(pl.BlockSpec((BM, D_MODEL), lambda m: (m, 0)),
                   pl.BlockSpec((None, BM, LANES), lambda m: (0, m, 0))),
        compiler_params=_params(1),
        name="cast_ssq",
    )(x_prompt, x_sample)


def _mm_qk_kernel(a_ref, ssq_ref, g_ref, w_ref, cos_ref, sin_ref, o_ref, wb_ref):
    n = pl.program_id(0)
    n_q = RET_QK // RET_IN_BN

    @pl.when(pl.program_id(1) == 0)
    def _():
        wb_ref[...] = (w_ref[...] * g_ref[...]).astype(BF16)

    acc = jnp.dot(a_ref[...], wb_ref[...], preferred_element_type=F32) * _row_scale(ssq_ref)
    scale = jnp.where(n >= n_q, RET_DK ** -0.5, 1.0).astype(F32)
    c = cos_ref[...]
    s = sin_ref[...]
    half = RET_DK // 2
    for hh in range(RET_IN_BN // RET_DK):
        lo = hh * RET_DK
        x1 = acc[:, lo:lo + half]
        x2 = acc[:, lo + half:lo + RET_DK]
        o_ref[:, lo:lo + half] = ((x1 * c - x2 * s) * scale).astype(BF16)
        o_ref[:, lo + half:lo + RET_DK] = ((x1 * s + x2 * c) * scale).astype(BF16)


def _mm_cast_kernel(a_ref, ssq_ref, g_ref, w_ref, o_ref, wb_ref):
    @pl.when(pl.program_id(1) == 0)
    def _():
        wb_ref[...] = (w_ref[...] * g_ref[...]).astype(BF16)

    acc = jnp.dot(a_ref[...], wb_ref[...], preferred_element_type=F32) * _row_scale(ssq_ref)
    o_ref[...] = acc.astype(BF16)


def _mm_retin(norm, w, layer, cos, sin):
    bn = RET_IN_BN
    _, k, n_out = w.shape
    half = RET_DK // 2
    n_qk = 2 * RET_QK // bn
    norm_args, norm_specs = _norm_operands(norm)
    out_spec = pl.BlockSpec((BM, bn), lambda n, m: (m, n))
    rope_spec = pl.BlockSpec((BM, half), lambda n, m: (m, 0))
    qk = pl.pallas_call(
        _mm_qk_kernel,
        out_shape=jax.ShapeDtypeStruct((M_TOTAL, 2 * RET_QK), BF16),
        grid=(n_qk, N_ROW_TILES),
        in_specs=norm_specs + [_weight_spec(k, bn, layer), rope_spec, rope_spec],
        out_specs=out_spec,
        scratch_shapes=[pltpu.VMEM((k, bn), BF16)],
        compiler_params=_params(2),
        name="mm_ret_qk",
    )(*norm_args, w, cos, sin)
    vg = pl.pallas_call(
        _mm_cast_kernel,
        out_shape=jax.ShapeDtypeStruct((M_TOTAL, 2 * RET_VD), BF16),
        grid=(n_out // bn - n_qk, N_ROW_TILES),
        in_specs=norm_specs + [_weight_spec(k, bn, layer, n_qk)],
        out_specs=out_spec,
        scratch_shapes=[pltpu.VMEM((k, bn), BF16)],
        compiler_params=_params(2),
        name="mm_ret_vg",
    )(*norm_args, w)
    return qk, vg


def _row_split_specs(cols, col_index):
    return [pl.BlockSpec((BM, cols), lambda n, m: (jnp.minimum(m, N_PROMPT_TILES - 1), col_index(n))),
            pl.BlockSpec((BM, cols), lambda n, m: (0, col_index(n)))]


def _mm_res_kernel(*refs, split_a, split_x, emit):
    refs = list(refs)
    ap_ref = refs.pop(0)
    as_ref = refs.pop(0) if split_a else None
    w_ref = refs.pop(0)
    xp_ref = refs.pop(0)
    xs_ref = refs.pop(0) if split_x else None
    o_ref = refs.pop(0)
    xb_ref, ssq_ref = (refs.pop(0), refs.pop(0)) if emit else (None, None)
    (wb_ref,) = refs
    m = pl.program_id(1)
    on_prompt = m < N_PROMPT_TILES

    @pl.when(m == 0)
    def _():
        wb_ref[...] = w_ref[...].astype(BF16)

    def body(a_ref):
        x = xp_ref[...]
        if split_x:
            x = jnp.where(on_prompt, x, xs_ref[...])
        x_new = x + jnp.dot(a_ref[...], wb_ref[...], preferred_element_type=F32)
        o_ref[...] = x_new
        if emit:
            _emit_norm_inputs(x_new, xb_ref, ssq_ref)

    if split_a:
        pl.when(on_prompt)(lambda: body(ap_ref))
        pl.when(jnp.logical_not(on_prompt))(lambda: body(as_ref))
    else:
        body(ap_ref)


def _mm_res(a, w, layer, x, emit=False):
    bn = PROJ_BN
    _, k, n_out = w.shape
    split_a = isinstance(a, tuple)
    split_x = isinstance(x, tuple)
    a_specs = _row_split_specs(k, lambda n: 0) if split_a else [pl.BlockSpec((BM, k), lambda n, m: (m, 0))]
    x_specs = _row_split_specs(bn, lambda n: n) if split_x else [pl.BlockSpec((BM, bn), lambda n, m: (m, n))]
    a_args = list(a) if split_a else [a]
    x_args = list(x) if split_x else [x]
    out_shapes = [jax.ShapeDtypeStruct((M_TOTAL, n_out), F32)]
    out_specs = [pl.BlockSpec((BM, bn), lambda n, m: (m, n))]
    if emit:
        shapes, specs = _norm_outputs(n_out, bn)
        out_shapes += shapes
        out_specs += specs
    out = pl.pallas_call(
        functools.partial(_mm_res_kernel, split_a=split_a, split_x=split_x, emit=emit),
        out_shape=tuple(out_shapes),
        grid=(n_out // bn, N_ROW_TILES),
        in_specs=a_specs + [_weight_spec(k, bn, layer)] + x_specs,
        out_specs=tuple(out_specs),
        scratch_shapes=[pltpu.VMEM((k, bn), BF16)],
        compiler_params=_params(2),
        name="mm_residual",
    )(*a_args, w, *x_args)
    return out if emit else out[0]


def _mm_glu_kernel(a_ref, wa_ref, wg_ref, x_ref, o_ref, xb_ref, ssq_ref, wab_ref, wgb_ref):
    @pl.when(pl.program_id(1) == 0)
    def _():
        wab_ref[...] = wa_ref[...].astype(BF16)
        wgb_ref[...] = wg_ref[...].astype(BF16)

    a = a_ref[...]
    ga = jnp.dot(a, wab_ref[...], preferred_element_type=F32)
    gb = jnp.dot(a, wgb_ref[...], preferred_element_type=F32)
    x_new = x_ref[...] + ga * _sigmoid(gb)
    o_ref[...] = x_new
    _emit_norm_inputs(x_new, xb_ref, ssq_ref)


def _mm_glu(a, w, layer, x):
    bn = PROJ_BN
    _, k, n2 = w.shape
    n_out = n2 // 2
    nb = n_out // bn
    norm_shapes, norm_specs = _norm_outputs(n_out, bn)
    return pl.pallas_call(
        _mm_glu_kernel,
        out_shape=(jax.ShapeDtypeStruct((M_TOTAL, n_out), F32), *norm_shapes),
        grid=(nb, N_ROW_TILES),
        in_specs=[
            pl.BlockSpec((BM, k), lambda n, m: (m, 0)),
            _weight_spec(k, bn, layer),
            _weight_spec(k, bn, layer, nb),
            pl.BlockSpec((BM, bn), lambda n, m: (m, n)),
        ],
        out_specs=(pl.BlockSpec((BM, bn), lambda n, m: (m, n)), *norm_specs),
        scratch_shapes=[pltpu.VMEM((k, bn), BF16), pltpu.VMEM((k, bn), BF16)],
        compiler_params=_params(2),
        name="mm_glu",
    )(a, w, w, x)


def _mm_ffnup_kernel(h_ref, ssq_ref, g_ref, wa_ref, wg_ref, cw_ref, cb_ref, cache_ref,
                     o_ref, tp_ref, ts_ref, wab_ref, wgb_ref, carry_ref):
    m = pl.program_id(1)

    @pl.when(m == 0)
    def _():
        g = g_ref[...]
        wab_ref[...] = (wa_ref[...] * g).astype(BF16)
        wgb_ref[...] = (wg_ref[...] * g).astype(BF16)
        carry_ref[...] = jnp.zeros_like(carry_ref)

    h = h_ref[...]
    r = _row_scale(ssq_ref)
    a = jnp.dot(h, wab_ref[...], preferred_element_type=F32) * r
    b = jnp.dot(h, wgb_ref[...], preferred_element_type=F32) * r
    r1 = pltpu.roll(a, 1, axis=0)
    r2 = pltpu.roll(a, 2, axis=0)
    cb = cb_ref[...]
    w0 = cw_ref[0:1, :]
    w1 = cw_ref[1:2, :]
    w2 = cw_ref[2:3, :]

    def gated(a_rows, prev1, prev2, b_rows):
        conv = cb + w0 * prev2 + w1 * prev1 + w2 * a_rows
        return ((conv * _sigmoid(conv)) * b_rows).astype(BF16)

    o_ref[...] = gated(a, r1, r2, b)

    row8 = lax.broadcasted_iota(jnp.int32, (SUBLANES, 1), 0)
    heads = [tuple(z[s * DEC_SEQ:s * DEC_SEQ + SUBLANES] for z in (a, r1, r2, b))
             for s in range(DEC_BATCH)]
    tails = [a[(s + 1) * DEC_SEQ - SUBLANES:(s + 1) * DEC_SEQ] for s in range(DEC_BATCH)]

    def redo_head(s, before1, before2):
        a8, r1_8, r2_8, b8 = heads[s]
        prev1 = jnp.where(row8 == 0, before1, r1_8)
        prev2 = jnp.where(row8 == 0, before2, jnp.where(row8 == 1, before1, r2_8))
        o_ref[s * DEC_SEQ:s * DEC_SEQ + SUBLANES, :] = gated(a8, prev1, prev2, b8)

    @pl.when(m < N_PROMPT_TILES)
    def _():
        c = carry_ref[...]
        redo_head(0, c[SUBLANES - 1:SUBLANES, :], c[SUBLANES - 2:SUBLANES - 1, :])
        carry_ref[...] = tails[DEC_BATCH - 1]

        @pl.when(m == N_PROMPT_TILES - 1)
        def _():
            tp_ref[...] = tails[DEC_BATCH - 1]

    @pl.when(m == N_PROMPT_TILES)
    def _():
        for s in range(DEC_BATCH):
            c = cache_ref[s]
            redo_head(s, c[1:2, :], c[0:1, :])
            ts_ref[s * SUBLANES:(s + 1) * SUBLANES, :] = tails[s]


def _mm_ffnup(norm, w, layer, conv_w, conv_b, cache):
    bn = PROJ_BN
    k = w.shape[1]
    nb = FFN_DIM // bn
    norm_args, norm_specs = _norm_operands(norm)
    return pl.pallas_call(
        _mm_ffnup_kernel,
        out_shape=(
            jax.ShapeDtypeStruct((M_TOTAL, FFN_DIM), BF16),
            jax.ShapeDtypeStruct((SUBLANES, FFN_DIM), F32),
            jax.ShapeDtypeStruct((DEC_BATCH * SUBLANES, FFN_DIM), F32),
        ),
        grid=(nb, N_ROW_TILES),
        in_specs=norm_specs + [
            _weight_spec(k, bn, layer),
            _weight_spec(k, bn, layer, nb),
            pl.BlockSpec((CONV_W, bn), lambda n, m: (0, n)),
            pl.BlockSpec((1, bn), lambda n, m: (0, n)),
            pl.BlockSpec((DEC_BATCH, CONV_W - 1, bn), lambda n, m: (0, 0, n)),
        ],
        out_specs=(
            pl.BlockSpec((BM, bn), lambda n, m: (m, n)),
            pl.BlockSpec((SUBLANES, bn), lambda n, m: (0, n)),
            pl.BlockSpec((DEC_BATCH * SUBLANES, bn), lambda n, m: (0, n)),
        ),
        scratch_shapes=[pltpu.VMEM((k, bn), BF16), pltpu.VMEM((k, bn), BF16),
                        pltpu.VMEM((SUBLANES, bn), F32)],
        compiler_params=_params(2),
        name="mm_ffn_up",
    )(*norm_args, w, w, conv_w, conv_b.reshape(1, FFN_DIM), cache)


def _ret_tables(lg, T):
    i = lax.broadcasted_iota(jnp.int32, (T, T), 0)
    j = lax.broadcasted_iota(jnp.int32, (T, T), 1)
    dist = jnp.abs(i - j).astype(F32)
    shift = int(math.log2(CHUNK))
    visible = (j >> shift) <= (i >> shift)
    mask = jnp.where(visible, jnp.exp(dist * lg), 0.0)
    t = lax.broadcasted_iota(jnp.int32, (T, 1), 0).astype(F32)
    cross = jnp.exp((t + 1.0) * lg)
    k_dec = jnp.exp((T - 1.0 - t) * lg)
    decay = jnp.exp(float(T) * lg)
    return mask, cross, k_dec, decay


def _ret_block(q, k, v, g, state, tables, gn):
    mask, cross, k_dec, decay = tables
    scores = lax.dot_general(q, k, (((1,), (1,)), ((), ())), preferred_element_type=F32)
    scores = scores * mask
    out = jnp.dot(scores.astype(BF16), v, preferred_element_type=F32)
    out = out + jnp.dot(q, state.astype(BF16), preferred_element_type=F32) * cross

    kd = (k.astype(F32) * k_dec).astype(BF16)
    new_state = decay * state + lax.dot_general(
        kd, v, (((0,), (0,)), ((), ())), preferred_element_type=F32)

    mu = jnp.mean(out, axis=-1, keepdims=True)
    oc = out - mu
    var = jnp.mean(oc * oc, axis=-1, keepdims=True)
    normed = (oc * lax.rsqrt(var + EPS)) * gn
    g = g.astype(F32)
    return (normed * (g * _sigmoid(g))).astype(BF16), new_state


def _ret_prompt_kernel(lg_ref, q_ref, k_ref, v_ref, g_ref, gn_ref, o_ref, sl_ref, state_ref, mask_ref):
    blk = pl.program_id(1)
    lg = lg_ref[...][:, 0:1]

    @pl.when(blk == 0)
    def _():
        state_ref[...] = jnp.zeros_like(state_ref)
        mask_ref[...] = _ret_tables(lg, RET_T)[0]

    _, cross, k_dec, decay = _ret_tables(lg, RET_T)
    tables = (mask_ref[...], cross, k_dec, decay)
    gn = gn_ref[...]
    state = state_ref[...]
    for sb in range(RET_SUB):
        rows = slice(sb * RET_T, (sb + 1) * RET_T)
        o_ref[rows, :], state = _ret_block(q_ref[rows, :], k_ref[rows, :], v_ref[rows, :],
                                           g_ref[rows, :], state, tables, gn)
    state_ref[...] = state

    @pl.when(blk == pl.num_programs(1) - 1)
    def _():
        sl_ref[...] = state


def _ret_sample_kernel(lg_ref, q_ref, k_ref, v_ref, g_ref, gn_ref, s0_ref, *rest):
    o_ref, sl_ref = rest[-2:]
    lg = lg_ref[...][:, 0:1]
    tables = _ret_tables(lg, DEC_SEQ)
    gn = gn_ref[...]
    for s in range(DEC_BATCH):
        rows = slice(s * DEC_SEQ, (s + 1) * DEC_SEQ)
        o_ref[rows, :], sl_ref[s] = _ret_block(q_ref[rows, :], k_ref[rows, :], v_ref[rows, :],
                                               g_ref[rows, :], s0_ref[s], tables, gn)


def _retention(qk, vg, gn, s0_stack, layer, sample_states=None):
    log_g = np.log1p(-np.exp2(-5.0 - np.arange(RET_HEADS, dtype=np.float32))).astype(np.float32)
    lg = jnp.asarray(np.broadcast_to(log_g[:, None, None], (RET_HEADS, 1, LANES)).copy())
    gn = gn.reshape(1, RET_VD)
    k_off = RET_QK // RET_DK
    g_off = RET_VD // RET_DV
    rows_p = RET_T * RET_SUB
    o_p, s_p = pl.pallas_call(
        _ret_prompt_kernel,
        out_shape=(jax.ShapeDtypeStruct((M_PROMPT, RET_VD), BF16),
                   jax.ShapeDtypeStruct((1, RET_HEADS, RET_DK, RET_DV), F32)),
        grid=(RET_HEADS, M_PROMPT // rows_p),
        in_specs=[
            pl.BlockSpec((None, 1, LANES), lambda h, b: (h, 0, 0)),
            pl.BlockSpec((rows_p, RET_DK), lambda h, b: (b, h)),
            pl.BlockSpec((rows_p, RET_DK), lambda h, b: (b, k_off + h)),
            pl.BlockSpec((rows_p, RET_DV), lambda h, b: (b, h)),
            pl.BlockSpec((rows_p, RET_DV), lambda h, b: (b, g_off + h)),
            pl.BlockSpec((1, RET_DV), lambda h, b: (0, h)),
        ],
        out_specs=(pl.BlockSpec((rows_p, RET_DV), lambda h, b: (b, h)),
                   pl.BlockSpec((None, None, RET_DK, RET_DV), lambda h, b: (0, h, 0, 0))),
        scratch_shapes=[pltpu.VMEM((RET_DK, RET_DV), F32), pltpu.VMEM((RET_T, RET_T), F32)],
        compiler_params=_params(2),
        name="retention",
    )(lg, qk, qk, vg, vg, gn)
    sample_block = M_PROMPT // M_SAMPLE
    carried = [] if sample_states is None else [sample_states]
    n_in = 7
    o_s, s_s = pl.pallas_call(
        _ret_sample_kernel,
        out_shape=(jax.ShapeDtypeStruct((M_SAMPLE, RET_VD), BF16),
                   jax.ShapeDtypeStruct((N_RET, DEC_BATCH, RET_HEADS, RET_DK, RET_DV), F32)),
        grid=(RET_HEADS,),
        in_specs=[
            pl.BlockSpec((None, 1, LANES), lambda h: (h, 0, 0)),
            pl.BlockSpec((M_SAMPLE, RET_DK), lambda h: (sample_block, h)),
            pl.BlockSpec((M_SAMPLE, RET_DK), lambda h: (sample_block, k_off + h)),
            pl.BlockSpec((M_SAMPLE, RET_DV), lambda h: (sample_block, h)),
            pl.BlockSpec((M_SAMPLE, RET_DV), lambda h: (sample_block, g_off + h)),
            pl.BlockSpec((1, RET_DV), lambda h: (0, h)),
            pl.BlockSpec((None, DEC_BATCH, None, RET_DK, RET_DV), lambda h: (layer, 0, h, 0, 0)),
        ] + [pl.BlockSpec(memory_space=pl.ANY) for _ in carried],
        out_specs=(pl.BlockSpec((M_SAMPLE, RET_DV), lambda h: (0, h)),
                   pl.BlockSpec((None, DEC_BATCH, None, RET_DK, RET_DV), lambda h: (layer, 0, h, 0, 0))),
        input_output_aliases={n_in: 1} if carried else {},
        compiler_params=_params(1),
        name="retention_sample",
    )(lg, qk, qk, vg, vg, gn, s0_stack, *carried)
    return o_p, o_s, s_p, s_s


def _ssm_prep_kernel(ar_ref, ai_ref, ldt_ref, br_ref, bi_ref, cr_ref, ci_ref,
                     bc_ref, cc_ref, w_ref, a16r_ref, a16i_ref, vr_s, vi_s):
    ar = ar_ref[...]
    ai = ai_ref[...]
    dt = jnp.exp(ldt_ref[...])
    mag = jnp.exp(ar * dt)
    ang = ai * dt
    abr = mag * jnp.cos(ang)
    abi = mag * jnp.sin(ang)
    den = ar * ar + ai * ai
    nr = abr - 1.0
    ni = abi
    cfr = (nr * ar + ni * ai) / den
    cfi = (ni * ar - nr * ai) / den

    def b4(z):
        return z[:, None]

    br = br_ref[...]
    bi = bi_ref[...]
    bbr = b4(cfr) * br - b4(cfi) * bi
    bbi = b4(cfr) * bi + b4(cfi) * br
    cre = cr_ref[...]
    cim = ci_ref[...]

    powers = []
    pr = jnp.ones_like(ar)
    pi = jnp.zeros_like(ar)
    for _ in range(SSM_T + 1):
        powers.append((pr, pi))
        pr, pi = pr * abr - pi * abi, pr * abi + pi * abr
    a16r_ref[...] = powers[SSM_T][0]
    a16i_ref[...] = powers[SSM_T][1]

    for m in range(SSM_T + 1):
        pr, pi = powers[m]
        vr = b4(pr) * cre - b4(pi) * cim
        vi = b4(pr) * cim + b4(pi) * cre
        if m < SSM_T:
            vr_s[:, m] = vr
            vi_s[:, m] = vi
        if m >= 1:
            cc_ref[:, m - 1, :, :, 0:LANES] = vr.astype(BF16)
            cc_ref[:, m - 1, :, :, LANES:2 * LANES] = (-vi).astype(BF16)

    for t in range(SSM_T):
        pr, pi = powers[SSM_T - 1 - t]
        bc_ref[:, t, :, :, 0:LANES] = (b4(pr) * bbr - b4(pi) * bbi).astype(BF16)
        bc_ref[:, t, :, :, LANES:2 * LANES] = (b4(pr) * bbi + b4(pi) * bbr).astype(BF16)

    lane = lax.broadcasted_iota(jnp.int32, (SSM_PAIR_CH, SSM_PAIR_COLS), 1)
    contract_last = (((1,), (1,)), ((), ()))
    for q in range(SSM_PB):
        vr_all = vr_s[q].reshape(SSM_PAIR_COLS, LANES)
        vi_all = vi_s[q].reshape(SSM_PAIR_COLS, LANES)
        lags = (lax.dot_general(bbr[q].reshape(SSM_PAIR_CH, LANES), vr_all, contract_last,
                                precision=lax.Precision.HIGHEST, preferred_element_type=F32)
                - lax.dot_general(bbi[q].reshape(SSM_PAIR_CH, LANES), vi_all, contract_last,
                                  precision=lax.Precision.HIGHEST, preferred_element_type=F32))
        for t in range(SSM_T):
            shifted = lags if t == 0 else pltpu.roll(lags, SSM_PAIR_CH * t, axis=1)
            w_ref[q, t * SSM_PAIR_CH:(t + 1) * SSM_PAIR_CH, :] = jnp.where(
                lane >= SSM_PAIR_CH * t, shifted, 0.0).astype(BF16)


def _pair_pack(x):
    x = x.reshape(SSM_PAIRS, 2, SSM_GROUP, SSM_STATE)
    lo = jnp.pad(x[:, 0], ((0, 0), (0, 0), (0, SSM_STATE)))
    hi = jnp.pad(x[:, 1], ((0, 0), (0, 0), (SSM_STATE, 0)))
    return jnp.stack([lo, hi], axis=1)


def _ssm_prep(a_re, a_im, log_dt, b_re, b_im, c_re, c_im):
    assert 2 * SSM_PB * SSM_GROUP == LANES
    ar = a_re.reshape(SSM_PAIRS, 1, LANES)
    ai = a_im.reshape(SSM_PAIRS, 1, LANES)
    ldt = jnp.repeat(log_dt, SSM_STATE).reshape(SSM_PAIRS, 1, LANES)
    b2r = _pair_pack(jnp.transpose(b_re, (0, 2, 1)))
    b2i = _pair_pack(jnp.transpose(b_im, (0, 2, 1)))
    c2r = _pair_pack(c_re)
    c2i = _pair_pack(c_im)
    vec = pl.BlockSpec((SSM_PB, 1, LANES), lambda p: (p, 0, 0))
    mat = pl.BlockSpec((SSM_PB, 2, SSM_GROUP, LANES), lambda p: (p, 0, 0, 0))
    proj = pl.BlockSpec((SSM_PB, SSM_T, 2, SSM_GROUP, 2 * LANES), lambda p: (p, 0, 0, 0, 0))
    proj_shape = jax.ShapeDtypeStruct((SSM_PAIRS, SSM_T, 2, SSM_GROUP, 2 * LANES), BF16)
    bc, cc, w, a16r, a16i = pl.pallas_call(
        _ssm_prep_kernel,
        out_shape=(
            proj_shape, proj_shape,
            jax.ShapeDtypeStruct((SSM_PAIRS, SSM_PAIR_COLS, SSM_PAIR_COLS), BF16),
            jax.ShapeDtypeStruct((SSM_PAIRS, 1, LANES), F32),
            jax.ShapeDtypeStruct((SSM_PAIRS, 1, LANES), F32),
        ),
        grid=(N_LANE_TILES,),
        in_specs=[vec, vec, vec, mat, mat, mat, mat],
        out_specs=(proj, proj,
                   pl.BlockSpec((SSM_PB, SSM_PAIR_COLS, SSM_PAIR_COLS), lambda p: (p, 0, 0)),
                   vec, vec),
        scratch_shapes=[pltpu.VMEM((SSM_PB, SSM_T, 2, SSM_GROUP, LANES), F32)] * 2,
        compiler_params=_params(1),
        name="ssm_prep",
    )(ar, ai, ldt, b2r, b2i, c2r, c2i)
    bc = bc.reshape(SSM_PAIRS, SSM_PAIR_COLS, 2 * LANES)
    cc = cc.reshape(SSM_PAIRS, SSM_PAIR_COLS, 2 * LANES)
    row = SSM_GROUPS * SSM_STATE
    a16r = jnp.broadcast_to(a16r.reshape(1, row), (SSM_SEGS, row))
    a16i = jnp.broadcast_to(a16i.reshape(1, row), (SSM_SEGS, row))
    return bc, cc, w, a16r, a16i


SSM_CHUNK_COLS = SSM_T * LANES
SSM_STATE_COLS = SSM_PB * LANES
SSM_CHUNKS_P = M_PROMPT // SSM_T
SSM_SCAN_UNROLL = 8


def _ssm_main_kernel(x_ref, w_ref, bc_ref, cc_ref, ar_ref, ai_ref, h0r_ref, h0i_ref,
                     y_ref, fpr_ref, fpi_ref, fsr_ref, fsi_ref, d_s, s_s, t_s):
    pair_cols = [slice(q * SSM_PAIR_COLS, (q + 1) * SSM_PAIR_COLS) for q in range(SSM_PB)]

    for q in range(SSM_PB):
        d = jnp.dot(x_ref[:, pair_cols[q]], bc_ref[q], preferred_element_type=F32)
        d_s[q] = d[:, 0:LANES]
        d_s[SSM_PB + q] = d[:, LANES:2 * LANES]

    tiles = range(SSM_PB)
    ar = [ar_ref[:, k * LANES:(k + 1) * LANES] for k in tiles]
    ai = [ai_ref[:, k * LANES:(k + 1) * LANES] for k in tiles]

    def advance(sr, si, rows, record=None):
        out_r, out_i = [], []
        for k in tiles:
            if record is not None:
                ref, rec_rows = record
                ref[k, rec_rows, :] = sr[k]
                ref[SSM_PB + k, rec_rows, :] = si[k]
            dr = d_s[k, rows, :]
            di = d_s[SSM_PB + k, rows, :]
            out_r.append(ar[k] * sr[k] - ai[k] * si[k] + dr)
            out_i.append(ar[k] * si[k] + ai[k] * sr[k] + di)
        return out_r, out_i

    def prompt_rows(i):
        return pl.ds(i, SSM_SEGS, stride=SSM_STEPS_P)

    def prompt_pass(start_r, start_i, record):
        def body(i, carry):
            rec = (t_s, pl.ds(pl.multiple_of(i * SSM_SEGS, SSM_SEGS), SSM_SEGS)) if record else None
            out_r, out_i = advance(list(carry[0]), list(carry[1]), prompt_rows(i), rec)
            return tuple(out_r), tuple(out_i)
        out_r, out_i = lax.fori_loop(0, SSM_STEPS_P, body, (tuple(start_r), tuple(start_i)),
                                     unroll=SSM_SCAN_UNROLL)
        return list(out_r), list(out_i)

    zeros = [jnp.zeros((SSM_SEGS, LANES), F32) for _ in tiles]
    er, ei = prompt_pass(zeros, zeros, record=False)

    sr, si = [], []
    for k in tiles:
        pr, pi = ar[k][0:1], ai[k][0:1]
        for _ in range(int(math.log2(SSM_STEPS_P))):
            pr, pi = pr * pr - pi * pi, 2.0 * pr * pi
        cr = jnp.zeros((1, LANES), F32)
        ci = cr
        starts_r, starts_i = [], []
        for s in range(SSM_SEGS):
            starts_r.append(cr)
            starts_i.append(ci)
            cr, ci = (pr * cr - pi * ci + er[k][s:s + 1], pr * ci + pi * cr + ei[k][s:s + 1])
        cols = slice(k * LANES, (k + 1) * LANES)
        fpr_ref[:, cols] = jnp.broadcast_to(cr, (SSM_SEGS, LANES))
        fpi_ref[:, cols] = jnp.broadcast_to(ci, (SSM_SEGS, LANES))
        sr.append(jnp.concatenate(starts_r, axis=0))
        si.append(jnp.concatenate(starts_i, axis=0))

    prompt_pass(sr, si, record=True)
    for k in range(2 * SSM_PB):
        for s in range(SSM_SEGS):
            s_s[k, s * SSM_STEPS_P:(s + 1) * SSM_STEPS_P, :] = t_s[k, pl.ds(s, SSM_STEPS_P, stride=SSM_SEGS), :]
    sr = [h0r_ref[:, k * LANES:(k + 1) * LANES] for k in tiles]
    si = [h0i_ref[:, k * LANES:(k + 1) * LANES] for k in tiles]
    for i in range(SSM_STEPS_S):
        rows = pl.ds(SSM_CHUNKS_P + i, DEC_BATCH, stride=SSM_STEPS_S)
        sr, si = advance(sr, si, rows, (s_s, rows))
    for k in tiles:
        fsr_ref[:, k * LANES:(k + 1) * LANES] = sr[k]
        fsi_ref[:, k * LANES:(k + 1) * LANES] = si[k]

    contract_last = (((1,), (1,)), ((), ()))
    for q in range(SSM_PB):
        states = jnp.concatenate([s_s[q].astype(BF16), s_s[SSM_PB + q].astype(BF16)], axis=1)
        carried = lax.dot_general(states, cc_ref[q], contract_last, preferred_element_type=F32)
        local = jnp.dot(x_ref[:, pair_cols[q]], w_ref[q], preferred_element_type=F32)
        y_ref[:, pair_cols[q]] = local + carried


def _ssm_main(x_c, w, bc, cc, a16r, a16i, h0r, h0i):
    row = SSM_GROUPS * SSM_STATE
    st = pl.BlockSpec((SSM_SEGS, SSM_STATE_COLS), lambda j: (0, j))
    st_shape = jax.ShapeDtypeStruct((SSM_SEGS, row), F32)
    proj = pl.BlockSpec((SSM_PB, SSM_PAIR_COLS, 2 * LANES), lambda j: (j, 0, 0))
    chunk_rows = pl.BlockSpec((None, N_CHUNKS, SSM_CHUNK_COLS), lambda j: (j, 0, 0))
    return pl.pallas_call(
        _ssm_main_kernel,
        out_shape=(jax.ShapeDtypeStruct((N_LANE_TILES, N_CHUNKS, SSM_CHUNK_COLS), F32),
                   st_shape, st_shape, st_shape, st_shape),
        grid=(N_LANE_TILES,),
        in_specs=[
            chunk_rows,
            pl.BlockSpec((SSM_PB, SSM_PAIR_COLS, SSM_PAIR_COLS), lambda j: (j, 0, 0)),
            proj, proj, st, st, st, st,
        ],
        out_specs=(chunk_rows, st, st, st, st),
        scratch_shapes=[
            pltpu.VMEM((2 * SSM_PB, N_CHUNKS, LANES), F32),
            pltpu.VMEM((2 * SSM_PB, N_CHUNKS, LANES), F32),
            pltpu.VMEM((2 * SSM_PB, SSM_CHUNKS_P, LANES), F32),
        ],
        compiler_params=_params(1),
        name="ssm_main",
    )(x_c, w, bc, cc, a16r, a16i, h0r, h0i)


def _ssm_post_kernel(y_ref, x_ref, g_ref, d_ref, o_ref, y_s):
    block_of_lane = lax.broadcasted_iota(jnp.int32, (CHUNKS_PER_TILE, LANES), 1) // SSM_PAIR_CH
    for j in range(N_LANE_TILES):
        for tg in range(SSM_T // SSM_VEC_TOKENS):
            pieces = [y_ref[j, :, q * SSM_PAIR_COLS + tg * LANES:q * SSM_PAIR_COLS + (tg + 1) * LANES]
                      for q in range(SSM_PB)]
            for u, out in enumerate(_transpose_lane_blocks(pieces, block_of_lane)):
                y_s[j, pl.ds(tg * SSM_VEC_TOKENS + u, CHUNKS_PER_TILE, stride=SSM_T), :] = out
    x = x_ref[...]
    ms = jnp.mean(x * x, axis=-1, keepdims=True)
    h = (x * lax.rsqrt(ms + EPS)) * g_ref[...]
    for j in range(N_LANE_TILES):
        cols = slice(j * LANES, (j + 1) * LANES)
        y = y_s[j] + d_ref[:, cols] * h[:, cols]
        gelu = 0.5 * y * (1.0 + jnp.tanh(GELU_TANH_SCALE * (y + GELU_TANH_CUBIC * (y * y * y))))
        o_ref[:, cols] = gelu.astype(BF16)


def _ssm_post(y_c, x, g, d):
    spec = pl.BlockSpec((BM, D_MODEL), lambda m: (m, 0))
    vec = pl.BlockSpec((1, D_MODEL), lambda m: (0, 0))
    return pl.pallas_call(
        _ssm_post_kernel,
        out_shape=jax.ShapeDtypeStruct((M_TOTAL, D_MODEL), BF16),
        grid=(N_ROW_TILES,),
        in_specs=[pl.BlockSpec((N_LANE_TILES, CHUNKS_PER_TILE, SSM_CHUNK_COLS), lambda m: (0, m, 0)),
                  spec, vec, vec],
        out_specs=spec,
        scratch_shapes=[pltpu.VMEM((N_LANE_TILES, BM, LANES), F32)],
        compiler_params=_params(1),
        name="ssm_post",
    )(y_c, x, g.reshape(1, D_MODEL), d.reshape(1, D_MODEL))


def kernel(x_prompt, x_sample, state_ret, state_ssm_re, state_ssm_im, cache_conv, norm_mix, norm_ffn, norm_final, ret_w_in, ret_gn, ret_w_out, ssm_a_re, ssm_a_im, ssm_log_dt, ssm_b_re, ssm_b_im, ssm_c_re, ssm_c_im, ssm_d, ssm_w_glu, ffn_w_up, ffn_conv_w, ffn_conv_b, ffn_w_down):
    x = (x_prompt.reshape(M_PROMPT, D_MODEL), x_sample.reshape(M_SAMPLE, D_MODEL))
    xb, ssq = _cast_ssq(*x)
    cos, sin = _rope_tables()

    ret_p, ret_s = [], None
    re_p, im_p, re_s, im_s = [], [], [], []
    conv_p, conv_s = [], []
    state_row = SSM_GROUPS * SSM_STATE
    for i in range(DEPTH):
        j = i // 2
        if i % 2 == 0:
            qk, vg = _mm_retin((xb, ssq, norm_mix[i]), ret_w_in, j, cos, sin)
            o_p, o_s, s_p, ret_s = _retention(qk, vg, ret_gn[j], state_ret, j, ret_s)
            ret_p.append(s_p)
            x, xb, ssq = _mm_res((o_p, o_s), ret_w_out, j, x, emit=True)
        else:
            h_c = _rmsnorm_chunked(x, norm_mix[i])
            bc, cc, w_ssm, a16r, a16i = _ssm_prep(ssm_a_re[j], ssm_a_im[j], ssm_log_dt[j],
                                               ssm_b_re[j], ssm_b_im[j], ssm_c_re[j], ssm_c_im[j])
            y_c, fpr, fpi, fsr, fsi = _ssm_main(
                h_c, w_ssm, bc, cc, a16r, a16i,
                state_ssm_re[j].reshape(DEC_BATCH, state_row),
                state_ssm_im[j].reshape(DEC_BATCH, state_row))
            re_p.append(fpr[0:1].reshape(1, SSM_GROUPS, SSM_STATE))
            im_p.append(fpi[0:1].reshape(1, SSM_GROUPS, SSM_STATE))
            re_s.append(fsr.reshape(DEC_BATCH, SSM_GROUPS, SSM_STATE))
            im_s.append(fsi.reshape(DEC_BATCH, SSM_GROUPS, SSM_STATE))
            gl = _ssm_post(y_c, x, norm_mix[i], ssm_d[j])
            x, xb, ssq = _mm_glu(gl, ssm_w_glu, j, x)

        u, tail_p, tail_s = _mm_ffnup((xb, ssq, norm_ffn[i]), ffn_w_up, i, ffn_conv_w[i],
                                      ffn_conv_b[i], cache_conv[i])
        conv_p.append(tail_p[SUBLANES - (CONV_W - 1):].reshape(1, CONV_W - 1, FFN_DIM))
        conv_s.append(tail_s.reshape(DEC_BATCH, SUBLANES, FFN_DIM)[:, SUBLANES - (CONV_W - 1):])
        if i + 1 < DEPTH and (i + 1) % 2 == 0:
            x, xb, ssq = _mm_res(u, ffn_w_down, i, x, emit=True)
        else:
            x = _mm_res(u, ffn_w_down, i, x)

    y_prompt = _rmsnorm(x, norm_final, F32, 0, N_PROMPT_TILES).reshape(1, SEQ, D_MODEL)
    y_sample = _rmsnorm(x, norm_final, F32, N_PROMPT_TILES, 1).reshape(DEC_BATCH, DEC_SEQ, D_MODEL)
    return (y_prompt, y_sample, jnp.stack(ret_p), ret_s,
            jnp.stack(re_p), jnp.stack(im_p), jnp.stack(re_s), jnp.stack(im_s),
            jnp.stack(conv_p), jnp.stack(conv_s))
```

```python
import functools
import math

import numpy as np
import jax
import jax.numpy as jnp
from jax import lax
from jax.experimental import pallas as pl
from jax.experimental.pallas import tpu as pltpu

F32 = jnp.float32
BF16 = jnp.bfloat16

D_MODEL = 2048
SEQ = 8192
DEPTH = 4
DEC_BATCH = 8
DEC_SEQ = 64
PAST_LEN = 1024
CHUNK = 64
RET_HEADS = 8
RET_DK = D_MODEL // RET_HEADS
RET_DV = 2 * RET_DK
RET_QK = RET_HEADS * RET_DK
RET_VD = RET_HEADS * RET_DV
N_RET = (DEPTH + 1) // 2
ROPE_BASE = 10000.0
SSM_GROUP = 16
SSM_GROUPS = D_MODEL // SSM_GROUP
SSM_STATE = 64
FFN_DIM = 2 * D_MODEL
CONV_W = 3
EPS = 1e-6

M_PROMPT = SEQ
M_SAMPLE = DEC_BATCH * DEC_SEQ
M_TOTAL = M_PROMPT + M_SAMPLE

LANES = 128
SUBLANES = 8
V7X_VMEM_BYTES = 64 * 1024 * 1024
VMEM_LIMIT_BYTES = V7X_VMEM_BYTES // 8 * 7

BM = M_SAMPLE
N_ROW_TILES = M_TOTAL // BM
N_PROMPT_TILES = M_PROMPT // BM
PROJ_BN = 1024
RET_IN_BN = 2048

RET_T = 256
RET_SUB = 4

SSM_T = 16
SSM_PAIRS = SSM_GROUPS // 2
SSM_PB = 4
SSM_SEGS = 8
SSM_STEPS_P = M_PROMPT // SSM_T // SSM_SEGS
SSM_STEPS_S = DEC_SEQ // SSM_T
SSM_PAIR_CH = 2 * SSM_GROUP
SSM_PAIR_COLS = SSM_T * SSM_PAIR_CH
SSM_VEC_TOKENS = LANES // SSM_PAIR_CH


def _params(n_axes):
    return pltpu.CompilerParams(dimension_semantics=("arbitrary",) * n_axes,
                                vmem_limit_bytes=VMEM_LIMIT_BYTES)


GELU_TANH_SCALE = math.sqrt(2.0 / math.pi)
GELU_TANH_CUBIC = 0.044715


def _sigmoid(x):
    return 1.0 / (1.0 + jnp.exp(-x))


def _rope_kernel(cos_ref, sin_ref):
    m = pl.program_id(0)
    half = RET_DK // 2
    freq = lax.broadcasted_iota(jnp.int32, (1, half), 1).astype(F32)
    inv = ROPE_BASE ** (-freq / half)
    r = m * BM + lax.broadcasted_iota(jnp.int32, (BM, half), 0)
    pos = jnp.where(r < M_PROMPT, r, PAST_LEN + ((r - M_PROMPT) & (DEC_SEQ - 1)))
    ang = pos.astype(F32) * inv
    cos_ref[...] = jnp.cos(ang)
    sin_ref[...] = jnp.sin(ang)


def _rope_tables():
    half = RET_DK // 2
    spec = pl.BlockSpec((BM, half), lambda m: (m, 0))
    return pl.pallas_call(
        _rope_kernel,
        out_shape=(jax.ShapeDtypeStruct((M_TOTAL, half), F32),) * 2,
        grid=(N_ROW_TILES,),
        out_specs=(spec, spec),
        compiler_params=_params(1),
        name="rope_tables",
    )()


def _norm_kernel(x_ref, g_ref, o_ref):
    x = x_ref[...]
    ms = jnp.mean(x * x, axis=-1, keepdims=True)
    o_ref[...] = ((x * lax.rsqrt(ms + EPS)) * g_ref[...]).astype(o_ref.dtype)


def _rmsnorm(x, g, out_dtype, tile0=0, n_tiles=N_ROW_TILES):
    return pl.pallas_call(
        _norm_kernel,
        out_shape=jax.ShapeDtypeStruct((n_tiles * BM, D_MODEL), out_dtype),
        grid=(n_tiles,),
        in_specs=[pl.BlockSpec((BM, D_MODEL), lambda m: (tile0 + m, 0)),
                  pl.BlockSpec((1, D_MODEL), lambda m: (0, 0))],
        out_specs=pl.BlockSpec((BM, D_MODEL), lambda m: (m, 0)),
        compiler_params=_params(1),
        name="rmsnorm",
    )(x, g.reshape(1, D_MODEL))


CHUNKS_PER_TILE = BM // 16
N_CHUNKS = M_TOTAL // 16
N_LANE_TILES = D_MODEL // LANES


def _transpose_lane_blocks(v, block_of_lane):
    n = len(v)
    out = []
    for u in range(n):
        acc = None
        for q in range(n):
            shift = ((q - u) * SSM_PAIR_CH) % LANES
            moved = v[q] if shift == 0 else pltpu.roll(v[q], shift, axis=1)
            acc = moved if acc is None else jnp.where(block_of_lane == q, moved, acc)
        out.append(acc)
    return out


def _norm_chunked_kernel(x_ref, g_ref, o_ref, h_s):
    x = x_ref[...]
    ms = jnp.mean(x * x, axis=-1, keepdims=True)
    h = (x * lax.rsqrt(ms + EPS)) * g_ref[...]
    for j in range(N_LANE_TILES):
        h_s[j] = h[:, j * LANES:(j + 1) * LANES]
    block_of_lane = lax.broadcasted_iota(jnp.int32, (CHUNKS_PER_TILE, LANES), 1) // SSM_PAIR_CH
    for j in range(N_LANE_TILES):
        for tg in range(SSM_T // SSM_VEC_TOKENS):
            rows = [h_s[j, pl.ds(tg * SSM_VEC_TOKENS + u, CHUNKS_PER_TILE, stride=SSM_T), :]
                    for u in range(SSM_VEC_TOKENS)]
            for q, out in enumerate(_transpose_lane_blocks(rows, block_of_lane)):
                c0 = q * SSM_PAIR_COLS + tg * LANES
                o_ref[j, :, c0:c0 + LANES] = out.astype(BF16)


def _rmsnorm_chunked(x, g):
    return pl.pallas_call(
        _norm_chunked_kernel,
        out_shape=jax.ShapeDtypeStruct((N_LANE_TILES, N_CHUNKS, SSM_T * LANES), BF16),
        grid=(N_ROW_TILES,),
        in_specs=[pl.BlockSpec((BM, D_MODEL), lambda m: (m, 0)),
                  pl.BlockSpec((1, D_MODEL), lambda m: (0, 0))],
        out_specs=pl.BlockSpec((N_LANE_TILES, CHUNKS_PER_TILE, SSM_T * LANES), lambda m: (0, m, 0)),
        scratch_shapes=[pltpu.VMEM((N_LANE_TILES, BM, LANES), F32)],
        compiler_params=_params(1),
        name="rmsnorm_chunked",
    )(x, g.reshape(1, D_MODEL))


W_SPEC = pl.BlockSpec(memory_space=pl.ANY)


def _weight_scratch(k, bn):
    return [pltpu.VMEM((k, bn), BF16), pltpu.VMEM((k, bn), F32), pltpu.SemaphoreType.DMA(())]


def _stage_weight(w_hbm, wb_ref, wf_ref, sem, *, layer, col0, n_tiles, gain_ref=None):
    n = pl.program_id(0)
    m = pl.program_id(1)
    bn = wf_ref.shape[1]

    def copy(tile):
        cols = pl.ds(pl.multiple_of((col0 + tile) * bn, bn), bn)
        return pltpu.make_async_copy(w_hbm.at[layer, :, cols], wf_ref, sem)

    @pl.when((n == 0) & (m == 0))
    def _():
        copy(0).start()

    @pl.when(m == 0)
    def _():
        copy(n).wait()
        w = wf_ref[...]
        if gain_ref is not None:
            w = w * gain_ref[...]
        wb_ref[...] = w.astype(BF16)

    @pl.when((m == 1) & (n + 1 < n_tiles))
    def _():
        copy(n + 1).start()


def _emit_norm_inputs(x_new, xb_ref, ssq_ref):
    xb_ref[...] = x_new.astype(BF16)
    sq = x_new * x_new
    part = sq[:, 0:LANES]
    for t in range(1, x_new.shape[1] // LANES):
        part = part + sq[:, t * LANES:(t + 1) * LANES]
    ssq_ref[...] = part


def _row_scale(ssq_ref):
    total = jnp.sum(jnp.sum(ssq_ref[...], axis=0), axis=-1, keepdims=True)
    return lax.rsqrt(total * (1.0 / D_MODEL) + EPS)


def _norm_operands(norm):
    xb, ssq, g = norm
    args = [xb, ssq, g.reshape(D_MODEL, 1)]
    specs = [pl.BlockSpec((BM, D_MODEL), lambda n, m: (m, 0)),
             pl.BlockSpec((ssq.shape[0], BM, LANES), lambda n, m: (0, m, 0)),
             pl.BlockSpec((D_MODEL, 1), lambda n, m: (0, 0))]
    return args, specs


def _norm_outputs(n_out, bn):
    shapes = [jax.ShapeDtypeStruct((M_TOTAL, n_out), BF16),
              jax.ShapeDtypeStruct((n_out // bn, M_TOTAL, LANES), F32)]
    specs = [pl.BlockSpec((BM, bn), lambda n, m: (m, n)),
             pl.BlockSpec((None, BM, LANES), lambda n, m: (n, m, 0))]
    return shapes, specs


def _cast_ssq_kernel(xp_ref, xs_ref, xb_ref, ssq_ref):
    x = jnp.where(pl.program_id(0) < N_PROMPT_TILES, xp_ref[...], xs_ref[...])
    _emit_norm_inputs(x, xb_ref, ssq_ref)


def _cast_ssq(x_prompt, x_sample):
    return pl.pallas_call(
        _cast_ssq_kernel,
        out_shape=(jax.ShapeDtypeStruct((M_TOTAL, D_MODEL), BF16),
                   jax.ShapeDtypeStruct((1, M_TOTAL, LANES), F32)),
        grid=(N_ROW_TILES,),
        in_specs=[pl.BlockSpec((BM, D_MODEL), lambda m: (jnp.minimum(m, N_PROMPT_TILES - 1), 0)),
                  pl.BlockSpec((BM, D_MODEL), lambda m: (0, 0))],
        out_specs=(pl.BlockSpec((BM, D_MODEL), lambda m: (m, 0)),
                   pl.BlockSpec((None, BM, LANES), lambda m: (0, m, 0))),
        compiler_params=_params(1),
        name="cast_ssq",
    )(x_prompt, x_sample)


def _mm_qk_kernel(a_ref, ssq_ref, g_ref, w_hbm, cos_ref, sin_ref, o_ref, wb_ref, wf_ref, sem, *, stage):
    n = pl.program_id(0)
    n_q = RET_QK // RET_IN_BN
    _stage_weight(w_hbm, wb_ref, wf_ref, sem, gain_ref=g_ref, **stage)

    acc = jnp.dot(a_ref[...], wb_ref[...], preferred_element_type=F32) * _row_scale(ssq_ref)
    scale = jnp.where(n >= n_q, RET_DK ** -0.5, 1.0).astype(F32)
    c = cos_ref[...]
    s = sin_ref[...]
    half = RET_DK // 2
    for hh in range(RET_IN_BN // RET_DK):
        lo = hh * RET_DK
        x1 = acc[:, lo:lo + half]
        x2 = acc[:, lo + half:lo + RET_DK]
        o_ref[:, lo:lo + half] = ((x1 * c - x2 * s) * scale).astype(BF16)
        o_ref[:, lo + half:lo + RET_DK] = ((x1 * s + x2 * c) * scale).astype(BF16)


def _mm_cast_kernel(a_ref, ssq_ref, g_ref, w_hbm, o_ref, wb_ref, wf_ref, sem, *, stage):
    _stage_weight(w_hbm, wb_ref, wf_ref, sem, gain_ref=g_ref, **stage)

    acc = jnp.dot(a_ref[...], wb_ref[...], preferred_element_type=F32) * _row_scale(ssq_ref)
    o_ref[...] = acc.astype(BF16)


def _mm_retin(norm, w, layer, cos, sin):
    bn = RET_IN_BN
    _, k, n_out = w.shape
    half = RET_DK // 2
    n_qk = 2 * RET_QK // bn
    norm_args, norm_specs = _norm_operands(norm)
    out_spec = pl.BlockSpec((BM, bn), lambda n, m: (m, n))
    rope_spec = pl.BlockSpec((BM, half), lambda n, m: (m, 0))
    n_vg = n_out // bn - n_qk
    qk = pl.pallas_call(
        functools.partial(_mm_qk_kernel, stage=dict(layer=layer, col0=0, n_tiles=n_qk)),
        out_shape=jax.ShapeDtypeStruct((M_TOTAL, 2 * RET_QK), BF16),
        grid=(n_qk, N_ROW_TILES),
        in_specs=norm_specs + [W_SPEC, rope_spec, rope_spec],
        out_specs=out_spec,
        scratch_shapes=_weight_scratch(k, bn),
        compiler_params=_params(2),
        name="mm_ret_qk",
    )(*norm_args, w, cos, sin)
    vg = pl.pallas_call(
        functools.partial(_mm_cast_kernel, stage=dict(layer=layer, col0=n_qk, n_tiles=n_vg)),
        out_shape=jax.ShapeDtypeStruct((M_TOTAL, 2 * RET_VD), BF16),
        grid=(n_vg, N_ROW_TILES),
        in_specs=norm_specs + [W_SPEC],
        out_specs=out_spec,
        scratch_shapes=_weight_scratch(k, bn),
        compiler_params=_params(2),
        name="mm_ret_vg",
    )(*norm_args, w)
    return qk, vg


def _row_split_specs(cols, col_index):
    return [pl.BlockSpec((BM, cols), lambda n, m: (jnp.minimum(m, N_PROMPT_TILES - 1), col_index(n))),
            pl.BlockSpec((BM, cols), lambda n, m: (0, col_index(n)))]


def _mm_res_kernel(*refs, split_a, split_x, emit, stage):
    refs = list(refs)
    ap_ref = refs.pop(0)
    as_ref = refs.pop(0) if split_a else None
    w_hbm = refs.pop(0)
    xp_ref = refs.pop(0)
    xs_ref = refs.pop(0) if split_x else None
    o_ref = refs.pop(0)
    xb_ref, ssq_ref = (refs.pop(0), refs.pop(0)) if emit else (None, None)
    wb_ref, wf_ref, sem = refs
    m = pl.program_id(1)
    on_prompt = m < N_PROMPT_TILES
    _stage_weight(w_hbm, wb_ref, wf_ref, sem, **stage)

    def body(a_ref):
        x = xp_ref[...]
        if split_x:
            x = jnp.where(on_prompt, x, xs_ref[...])
        x_new = x + jnp.dot(a_ref[...], wb_ref[...], preferred_element_type=F32)
        o_ref[...] = x_new
        if emit:
            _emit_norm_inputs(x_new, xb_ref, ssq_ref)

    if split_a:
        pl.when(on_prompt)(lambda: body(ap_ref))
        pl.when(jnp.logical_not(on_prompt))(lambda: body(as_ref))
    else:
        body(ap_ref)


def _mm_res(a, w, layer, x, emit=False):
    bn = PROJ_BN
    _, k, n_out = w.shape
    split_a = isinstance(a, tuple)
    split_x = isinstance(x, tuple)
    a_specs = _row_split_specs(k, lambda n: 0) if split_a else [pl.BlockSpec((BM, k), lambda n, m: (m, 0))]
    x_specs = _row_split_specs(bn, lambda n: n) if split_x else [pl.BlockSpec((BM, bn), lambda n, m: (m, n))]
    a_args = list(a) if split_a else [a]
    x_args = list(x) if split_x else [x]
    out_shapes = [jax.ShapeDtypeStruct((M_TOTAL, n_out), F32)]
    out_specs = [pl.BlockSpec((BM, bn), lambda n, m: (m, n))]
    if emit:
        shapes, specs = _norm_outputs(n_out, bn)
        out_shapes += shapes
        out_specs += specs
    out = pl.pallas_call(
        functools.partial(_mm_res_kernel, split_a=split_a, split_x=split_x, emit=emit,
                          stage=dict(layer=layer, col0=0, n_tiles=n_out // bn)),
        out_shape=tuple(out_shapes),
        grid=(n_out // bn, N_ROW_TILES),
        in_specs=a_specs + [W_SPEC] + x_specs,
        out_specs=tuple(out_specs),
        scratch_shapes=_weight_scratch(k, bn),
        compiler_params=_params(2),
        name="mm_residual",
    )(*a_args, w, *x_args)
    return out if emit else out[0]


def _mm_glu_kernel(a_ref, w_hbm, x_ref, o_ref, xb_ref, ssq_ref,
                   wab_ref, waf_ref, sem_a, wgb_ref, wgf_ref, sem_g, *, stage_a, stage_g):
    _stage_weight(w_hbm, wab_ref, waf_ref, sem_a, **stage_a)
    _stage_weight(w_hbm, wgb_ref, wgf_ref, sem_g, **stage_g)

    a = a_ref[...]
    ga = jnp.dot(a, wab_ref[...], preferred_element_type=F32)
    gb = jnp.dot(a, wgb_ref[...], preferred_element_type=F32)
    x_new = x_ref[...] + ga * _sigmoid(gb)
    o_ref[...] = x_new
    _emit_norm_inputs(x_new, xb_ref, ssq_ref)


def _mm_glu(a, w, layer, x):
    bn = PROJ_BN
    _, k, n2 = w.shape
    n_out = n2 // 2
    nb = n_out // bn
    norm_shapes, norm_specs = _norm_outputs(n_out, bn)
    return pl.pallas_call(
        functools.partial(_mm_glu_kernel,
                          stage_a=dict(layer=layer, col0=0, n_tiles=nb),
                          stage_g=dict(layer=layer, col0=nb, n_tiles=nb)),
        out_shape=(jax.ShapeDtypeStruct((M_TOTAL, n_out), F32), *norm_shapes),
        grid=(nb, N_ROW_TILES),
        in_specs=[
            pl.BlockSpec((BM, k), lambda n, m: (m, 0)),
            W_SPEC,
            pl.BlockSpec((BM, bn), lambda n, m: (m, n)),
        ],
        out_specs=(pl.BlockSpec((BM, bn), lambda n, m: (m, n)), *norm_specs),
        scratch_shapes=_weight_scratch(k, bn) + _weight_scratch(k, bn),
        compiler_params=_params(2),
        name="mm_glu",
    )(a, w, x)


def _mm_ffnup_kernel(h_ref, ssq_ref, g_ref, w_hbm, cw_ref, cb_ref, cache_ref,
                     o_ref, tp_ref, ts_ref,
                     wab_ref, waf_ref, sem_a, wgb_ref, wgf_ref, sem_g, carry_ref, *, stage_a, stage_g):
    m = pl.program_id(1)
    _stage_weight(w_hbm, wab_ref, waf_ref, sem_a, gain_ref=g_ref, **stage_a)
    _stage_weight(w_hbm, wgb_ref, wgf_ref, sem_g, gain_ref=g_ref, **stage_g)

    @pl.when(m == 0)
    def _():
        carry_ref[...] = jnp.zeros_like(carry_ref)

    h = h_ref[...]
    r = _row_scale(ssq_ref)
    a = jnp.dot(h, wab_ref[...], preferred_element_type=F32) * r
    b = jnp.dot(h, wgb_ref[...], preferred_element_type=F32) * r
    r1 = pltpu.roll(a, 1, axis=0)
    r2 = pltpu.roll(a, 2, axis=0)
    cb = cb_ref[...]
    w0 = cw_ref[0:1, :]
    w1 = cw_ref[1:2, :]
    w2 = cw_ref[2:3, :]

    def gated(a_rows, prev1, prev2, b_rows):
        conv = cb + w0 * prev2 + w1 * prev1 + w2 * a_rows
        return ((conv * _sigmoid(conv)) * b_rows).astype(BF16)

    o_ref[...] = gated(a, r1, r2, b)

    row8 = lax.broadcasted_iota(jnp.int32, (SUBLANES, 1), 0)
    heads = [tuple(z[s * DEC_SEQ:s * DEC_SEQ + SUBLANES] for z in (a, r1, r2, b))
             for s in range(DEC_BATCH)]
    tails = [a[(s + 1) * DEC_SEQ - SUBLANES:(s + 1) * DEC_SEQ] for s in range(DEC_BATCH)]

    def redo_head(s, before1, before2):
        a8, r1_8, r2_8, b8 = heads[s]
        prev1 = jnp.where(row8 == 0, before1, r1_8)
        prev2 = jnp.where(row8 == 0, before2, jnp.where(row8 == 1, before1, r2_8))
        o_ref[s * DEC_SEQ:s * DEC_SEQ + SUBLANES, :] = gated(a8, prev1, prev2, b8)

    @pl.when(m < N_PROMPT_TILES)
    def _():
        c = carry_ref[...]
        redo_head(0, c[SUBLANES - 1:SUBLANES, :], c[SUBLANES - 2:SUBLANES - 1, :])
        carry_ref[...] = tails[DEC_BATCH - 1]

        @pl.when(m == N_PROMPT_TILES - 1)
        def _():
            tp_ref[...] = tails[DEC_BATCH - 1]

    @pl.when(m == N_PROMPT_TILES)
    def _():
        for s in range(DEC_BATCH):
            c = cache_ref[s]
            redo_head(s, c[1:2, :], c[0:1, :])
            ts_ref[s * SUBLANES:(s + 1) * SUBLANES, :] = tails[s]


def _mm_ffnup(norm, w, layer, conv_w, conv_b, cache):
    bn = PROJ_BN
    k = w.shape[1]
    nb = FFN_DIM // bn
    norm_args, norm_specs = _norm_operands(norm)
    return pl.pallas_call(
        functools.partial(_mm_ffnup_kernel,
                          stage_a=dict(layer=layer, col0=0, n_tiles=nb),
                          stage_g=dict(layer=layer, col0=nb, n_tiles=nb)),
        out_shape=(
            jax.ShapeDtypeStruct((M_TOTAL, FFN_DIM), BF16),
            jax.ShapeDtypeStruct((SUBLANES, FFN_DIM), F32),
            jax.ShapeDtypeStruct((DEC_BATCH * SUBLANES, FFN_DIM), F32),
        ),
        grid=(nb, N_ROW_TILES),
        in_specs=norm_specs + [
            W_SPEC,
            pl.BlockSpec((CONV_W, bn), lambda n, m: (0, n)),
            pl.BlockSpec((1, bn), lambda n, m: (0, n)),
            pl.BlockSpec((DEC_BATCH, CONV_W - 1, bn), lambda n, m: (0, 0, n)),
        ],
        out_specs=(
            pl.BlockSpec((BM, bn), lambda n, m: (m, n)),
            pl.BlockSpec((SUBLANES, bn), lambda n, m: (0, n)),
            pl.BlockSpec((DEC_BATCH * SUBLANES, bn), lambda n, m: (0, n)),
        ),
        scratch_shapes=_weight_scratch(k, bn) + _weight_scratch(k, bn) + [pltpu.VMEM((SUBLANES, bn), F32)],
        compiler_params=_params(2),
        name="mm_ffn_up",
    )(*norm_args, w, conv_w, conv_b.reshape(1, FFN_DIM), cache)


def _ret_tables(lg, T):
    i = lax.broadcasted_iota(jnp.int32, (T, T), 0)
    j = lax.broadcasted_iota(jnp.int32, (T, T), 1)
    dist = jnp.abs(i - j).astype(F32)
    shift = int(math.log2(CHUNK))
    visible = (j >> shift) <= (i >> shift)
    mask = jnp.where(visible, jnp.exp(dist * lg), 0.0)
    t = lax.broadcasted_iota(jnp.int32, (T, 1), 0).astype(F32)
    cross = jnp.exp((t + 1.0) * lg)
    k_dec = jnp.exp((T - 1.0 - t) * lg)
    decay = jnp.exp(float(T) * lg)
    return mask, cross, k_dec, decay


def _ret_block(q, k, v, g, state, tables, gn):
    mask, cross, k_dec, decay = tables
    scores = lax.dot_general(q, k, (((1,), (1,)), ((), ())), preferred_element_type=F32)
    scores = scores * mask
    out = jnp.dot(scores.astype(BF16), v, preferred_element_type=F32)
    out = out + jnp.dot(q, state.astype(BF16), preferred_element_type=F32) * cross

    kd = (k.astype(F32) * k_dec).astype(BF16)
    new_state = decay * state + lax.dot_general(
        kd, v, (((0,), (0,)), ((), ())), preferred_element_type=F32)

    mu = jnp.mean(out, axis=-1, keepdims=True)
    oc = out - mu
    var = jnp.mean(oc * oc, axis=-1, keepdims=True)
    normed = (oc * lax.rsqrt(var + EPS)) * gn
    g = g.astype(F32)
    return (normed * (g * _sigmoid(g))).astype(BF16), new_state


def _ret_prompt_kernel(lg_ref, q_ref, k_ref, v_ref, g_ref, gn_ref, o_ref, sl_ref, state_ref, mask_ref):
    blk = pl.program_id(1)
    lg = lg_ref[...][:, 0:1]

    @pl.when(blk == 0)
    def _():
        state_ref[...] = jnp.zeros_like(state_ref)
        mask_ref[...] = _ret_tables(lg, RET_T)[0]

    _, cross, k_dec, decay = _ret_tables(lg, RET_T)
    tables = (mask_ref[...], cross, k_dec, decay)
    gn = gn_ref[...]
    state = state_ref[...]
    for sb in range(RET_SUB):
        rows = slice(sb * RET_T, (sb + 1) * RET_T)
        o_ref[rows, :], state = _ret_block(q_ref[rows, :], k_ref[rows, :], v_ref[rows, :],
                                           g_ref[rows, :], state, tables, gn)
    state_ref[...] = state

    @pl.when(blk == pl.num_programs(1) - 1)
    def _():
        sl_ref[...] = state


def _ret_sample_kernel(lg_ref, q_ref, k_ref, v_ref, g_ref, gn_ref, s0_ref, *rest, layer):
    o_ref, sl_ref = rest[-2:]
    if len(rest) == 2:
        for other in range(N_RET):
            if other != layer:
                sl_ref[other] = jnp.zeros(sl_ref.shape[1:], F32)
        sl_ref = sl_ref.at[layer]
    lg = lg_ref[...][:, 0:1]
    tables = _ret_tables(lg, DEC_SEQ)
    gn = gn_ref[...]
    for s in range(DEC_BATCH):
        rows = slice(s * DEC_SEQ, (s + 1) * DEC_SEQ)
        o_ref[rows, :], sl_ref[s] = _ret_block(q_ref[rows, :], k_ref[rows, :], v_ref[rows, :],
                                               g_ref[rows, :], s0_ref[s], tables, gn)


def _retention(qk, vg, gn, s0_stack, layer, sample_states=None):
    log_g = np.log1p(-np.exp2(-5.0 - np.arange(RET_HEADS, dtype=np.float32))).astype(np.float32)
    lg = jnp.asarray(np.broadcast_to(log_g[:, None, None], (RET_HEADS, 1, LANES)).copy())
    gn = gn.reshape(1, RET_VD)
    k_off = RET_QK // RET_DK
    g_off = RET_VD // RET_DV
    rows_p = RET_T * RET_SUB
    o_p, s_p = pl.pallas_call(
        _ret_prompt_kernel,
        out_shape=(jax.ShapeDtypeStruct((M_PROMPT, RET_VD), BF16),
                   jax.ShapeDtypeStruct((1, RET_HEADS, RET_DK, RET_DV), F32)),
        grid=(RET_HEADS, M_PROMPT // rows_p),
        in_specs=[
            pl.BlockSpec((None, 1, LANES), lambda h, b: (h, 0, 0)),
            pl.BlockSpec((rows_p, RET_DK), lambda h, b: (b, h)),
            pl.BlockSpec((rows_p, RET_DK), lambda h, b: (b, k_off + h)),
            pl.BlockSpec((rows_p, RET_DV), lambda h, b: (b, h)),
            pl.BlockSpec((rows_p, RET_DV), lambda h, b: (b, g_off + h)),
            pl.BlockSpec((1, RET_DV), lambda h, b: (0, h)),
        ],
        out_specs=(pl.BlockSpec((rows_p, RET_DV), lambda h, b: (b, h)),
                   pl.BlockSpec((None, None, RET_DK, RET_DV), lambda h, b: (0, h, 0, 0))),
        scratch_shapes=[pltpu.VMEM((RET_DK, RET_DV), F32), pltpu.VMEM((RET_T, RET_T), F32)],
        compiler_params=_params(2),
        name="retention",
    )(lg, qk, qk, vg, vg, gn)
    sample_block = M_PROMPT // M_SAMPLE
    carried = [] if sample_states is None else [sample_states]
    n_in = 7
    if carried:
        state_spec = pl.BlockSpec((None, DEC_BATCH, None, RET_DK, RET_DV), lambda h: (layer, 0, h, 0, 0))
    else:
        state_spec = pl.BlockSpec((N_RET, DEC_BATCH, None, RET_DK, RET_DV), lambda h: (0, 0, h, 0, 0))
    o_s, s_s = pl.pallas_call(
        functools.partial(_ret_sample_kernel, layer=layer),
        out_shape=(jax.ShapeDtypeStruct((M_SAMPLE, RET_VD), BF16),
                   jax.ShapeDtypeStruct((N_RET, DEC_BATCH, RET_HEADS, RET_DK, RET_DV), F32)),
        grid=(RET_HEADS,),
        in_specs=[
            pl.BlockSpec((None, 1, LANES), lambda h: (h, 0, 0)),
            pl.BlockSpec((M_SAMPLE, RET_DK), lambda h: (sample_block, h)),
            pl.BlockSpec((M_SAMPLE, RET_DK), lambda h: (sample_block, k_off + h)),
            pl.BlockSpec((M_SAMPLE, RET_DV), lambda h: (sample_block, h)),
            pl.BlockSpec((M_SAMPLE, RET_DV), lambda h: (sample_block, g_off + h)),
            pl.BlockSpec((1, RET_DV), lambda h: (0, h)),
            pl.BlockSpec((None, DEC_BATCH, None, RET_DK, RET_DV), lambda h: (layer, 0, h, 0, 0)),
        ] + [pl.BlockSpec(memory_space=pl.ANY) for _ in carried],
        out_specs=(pl.BlockSpec((M_SAMPLE, RET_DV), lambda h: (0, h)), state_spec),
        input_output_aliases={n_in: 1} if carried else {},
        compiler_params=_params(1),
        name="retention_sample",
    )(lg, qk, qk, vg, vg, gn, s0_stack, *carried)
    return o_p, o_s, s_p, s_s


def _ssm_prep_kernel(ar_ref, ai_ref, ldt_ref, br_ref, bi_ref, cr_ref, ci_ref,
                     bc_ref, cc_ref, w_ref, a16r_ref, a16i_ref, vr_s, vi_s):
    ar = ar_ref[...]
    ai = ai_ref[...]
    dt = jnp.exp(ldt_ref[...])
    mag = jnp.exp(ar * dt)
    ang = ai * dt
    abr = mag * jnp.cos(ang)
    abi = mag * jnp.sin(ang)
    den = ar * ar + ai * ai
    nr = abr - 1.0
    ni = abi
    cfr = (nr * ar + ni * ai) / den
    cfi = (ni * ar - nr * ai) / den

    def b4(z):
        return z[:, None]

    br = br_ref[...]
    bi = bi_ref[...]
    bbr = b4(cfr) * br - b4(cfi) * bi
    bbi = b4(cfr) * bi + b4(cfi) * br
    cre = cr_ref[...]
    cim = ci_ref[...]

    powers = []
    pr = jnp.ones_like(ar)
    pi = jnp.zeros_like(ar)
    for _ in range(SSM_T + 1):
        powers.append((pr, pi))
        pr, pi = pr * abr - pi * abi, pr * abi + pi * abr
    a16r_ref[...] = powers[SSM_T][0]
    a16i_ref[...] = powers[SSM_T][1]

    for m in range(SSM_T + 1):
        pr, pi = powers[m]
        vr = b4(pr) * cre - b4(pi) * cim
        vi = b4(pr) * cim + b4(pi) * cre
        if m < SSM_T:
            vr_s[:, m] = vr
            vi_s[:, m] = vi
        if m >= 1:
            cc_ref[:, m - 1, :, :, 0:LANES] = vr.astype(BF16)
            cc_ref[:, m - 1, :, :, LANES:2 * LANES] = (-vi).astype(BF16)

    for t in range(SSM_T):
        pr, pi = powers[SSM_T - 1 - t]
        bc_ref[:, t, :, :, 0:LANES] = (b4(pr) * bbr - b4(pi) * bbi).astype(BF16)
        bc_ref[:, t, :, :, LANES:2 * LANES] = (b4(pr) * bbi + b4(pi) * bbr).astype(BF16)

    lane = lax.broadcasted_iota(jnp.int32, (SSM_PAIR_CH, SSM_PAIR_COLS), 1)
    contract_last = (((1,), (1,)), ((), ()))
    for q in range(SSM_PB):
        vr_all = vr_s[q].reshape(SSM_PAIR_COLS, LANES)
        vi_all = vi_s[q].reshape(SSM_PAIR_COLS, LANES)
        lags = (lax.dot_general(bbr[q].reshape(SSM_PAIR_CH, LANES), vr_all, contract_last,
                                precision=lax.Precision.HIGHEST, preferred_element_type=F32)
                - lax.dot_general(bbi[q].reshape(SSM_PAIR_CH, LANES), vi_all, contract_last,
                                  precision=lax.Precision.HIGHEST, preferred_element_type=F32))
        for t in range(SSM_T):
            shifted = lags if t == 0 else pltpu.roll(lags, SSM_PAIR_CH * t, axis=1)
            w_ref[q, t * SSM_PAIR_CH:(t + 1) * SSM_PAIR_CH, :] = jnp.where(
                lane >= SSM_PAIR_CH * t, shifted, 0.0).astype(BF16)


def _pair_pack(x):
    x = x.reshape(SSM_PAIRS, 2, SSM_GROUP, SSM_STATE)
    lo = jnp.pad(x[:, 0], ((0, 0), (0, 0), (0, SSM_STATE)))
    hi = jnp.pad(x[:, 1], ((0, 0), (0, 0), (SSM_STATE, 0)))
    return jnp.stack([lo, hi], axis=1)


def _ssm_prep(a_re, a_im, log_dt, b_re, b_im, c_re, c_im):
    assert 2 * SSM_PB * SSM_GROUP == LANES
    ar = a_re.reshape(SSM_PAIRS, 1, LANES)
    ai = a_im.reshape(SSM_PAIRS, 1, LANES)
    ldt = jnp.repeat(log_dt, SSM_STATE).reshape(SSM_PAIRS, 1, LANES)
    b2r = _pair_pack(jnp.transpose(b_re, (0, 2, 1)))
    b2i = _pair_pack(jnp.transpose(b_im, (0, 2, 1)))
    c2r = _pair_pack(c_re)
    c2i = _pair_pack(c_im)
    vec = pl.BlockSpec((SSM_PB, 1, LANES), lambda p: (p, 0, 0))
    mat = pl.BlockSpec((SSM_PB, 2, SSM_GROUP, LANES), lambda p: (p, 0, 0, 0))
    proj = pl.BlockSpec((SSM_PB, SSM_T, 2, SSM_GROUP, 2 * LANES), lambda p: (p, 0, 0, 0, 0))
    proj_shape = jax.ShapeDtypeStruct((SSM_PAIRS, SSM_T, 2, SSM_GROUP, 2 * LANES), BF16)
    bc, cc, w, a16r, a16i = pl.pallas_call(
        _ssm_prep_kernel,
        out_shape=(
            proj_shape, proj_shape,
            jax.ShapeDtypeStruct((SSM_PAIRS, SSM_PAIR_COLS, SSM_PAIR_COLS), BF16),
            jax.ShapeDtypeStruct((SSM_PAIRS, 1, LANES), F32),
            jax.ShapeDtypeStruct((SSM_PAIRS, 1, LANES), F32),
        ),
        grid=(N_LANE_TILES,),
        in_specs=[vec, vec, vec, mat, mat, mat, mat],
        out_specs=(proj, proj,
                   pl.BlockSpec((SSM_PB, SSM_PAIR_COLS, SSM_PAIR_COLS), lambda p: (p, 0, 0)),
                   vec, vec),
        scratch_shapes=[pltpu.VMEM((SSM_PB, SSM_T, 2, SSM_GROUP, LANES), F32)] * 2,
        compiler_params=_params(1),
        name="ssm_prep",
    )(ar, ai, ldt, b2r, b2i, c2r, c2i)
    bc = bc.reshape(SSM_PAIRS, SSM_PAIR_COLS, 2 * LANES)
    cc = cc.reshape(SSM_PAIRS, SSM_PAIR_COLS, 2 * LANES)
    row = SSM_GROUPS * SSM_STATE
    a16r = jnp.broadcast_to(a16r.reshape(1, row), (SSM_SEGS, row))
    a16i = jnp.broadcast_to(a16i.reshape(1, row), (SSM_SEGS, row))
    return bc, cc, w, a16r, a16i


SSM_CHUNK_COLS = SSM_T * LANES
SSM_STATE_COLS = SSM_PB * LANES
SSM_CHUNKS_P = M_PROMPT // SSM_T
SSM_SCAN_UNROLL = 8


def _ssm_main_kernel(x_ref, w_ref, bc_ref, cc_ref, ar_ref, ai_ref, h0r_ref, h0i_ref,
                     y_ref, fpr_ref, fpi_ref, fsr_ref, fsi_ref, d_s, s_s, t_s):
    pair_cols = [slice(q * SSM_PAIR_COLS, (q + 1) * SSM_PAIR_COLS) for q in range(SSM_PB)]

    for q in range(SSM_PB):
        d = jnp.dot(x_ref[:, pair_cols[q]], bc_ref[q], preferred_element_type=F32)
        d_s[q] = d[:, 0:LANES]
        d_s[SSM_PB + q] = d[:, LANES:2 * LANES]

    tiles = range(SSM_PB)
    ar = [ar_ref[:, k * LANES:(k + 1) * LANES] for k in tiles]
    ai = [ai_ref[:, k * LANES:(k + 1) * LANES] for k in tiles]

    def advance(sr, si, rows, record=None):
        out_r, out_i = [], []
        for k in tiles:
            if record is not None:
                ref, rec_rows = record
                ref[k, rec_rows, :] = sr[k]
                ref[SSM_PB + k, rec_rows, :] = si[k]
            dr = d_s[k, rows, :]
            di = d_s[SSM_PB + k, rows, :]
            out_r.append(ar[k] * sr[k] - ai[k] * si[k] + dr)
            out_i.append(ar[k] * si[k] + ai[k] * sr[k] + di)
        return out_r, out_i

    def prompt_rows(i):
        return pl.ds(i, SSM_SEGS, stride=SSM_STEPS_P)

    def prompt_pass(start_r, start_i, record):
        def body(i, carry):
            rec = (t_s, pl.ds(pl.multiple_of(i * SSM_SEGS, SSM_SEGS), SSM_SEGS)) if record else None
            out_r, out_i = advance(list(carry[0]), list(carry[1]), prompt_rows(i), rec)
            return tuple(out_r), tuple(out_i)
        out_r, out_i = lax.fori_loop(0, SSM_STEPS_P, body, (tuple(start_r), tuple(start_i)),
                                     unroll=SSM_SCAN_UNROLL)
        return list(out_r), list(out_i)

    zeros = [jnp.zeros((SSM_SEGS, LANES), F32) for _ in tiles]
    er, ei = prompt_pass(zeros, zeros, record=False)

    sr, si = [], []
    for k in tiles:
        pr, pi = ar[k][0:1], ai[k][0:1]
        for _ in range(int(math.log2(SSM_STEPS_P))):
            pr, pi = pr * pr - pi * pi, 2.0 * pr * pi
        cr = jnp.zeros((1, LANES), F32)
        ci = cr
        starts_r, starts_i = [], []
        for s in range(SSM_SEGS):
            starts_r.append(cr)
            starts_i.append(ci)
            cr, ci = (pr * cr - pi * ci + er[k][s:s + 1], pr * ci + pi * cr + ei[k][s:s + 1])
        cols = slice(k * LANES, (k + 1) * LANES)
        fpr_ref[:, cols] = jnp.broadcast_to(cr, (SSM_SEGS, LANES))
        fpi_ref[:, cols] = jnp.broadcast_to(ci, (SSM_SEGS, LANES))
        sr.append(jnp.concatenate(starts_r, axis=0))
        si.append(jnp.concatenate(starts_i, axis=0))

    prompt_pass(sr, si, record=True)
    for k in range(2 * SSM_PB):
        for s in range(SSM_SEGS):
            s_s[k, s * SSM_STEPS_P:(s + 1) * SSM_STEPS_P, :] = t_s[k, pl.ds(s, SSM_STEPS_P, stride=SSM_SEGS), :]
    sr = [h0r_ref[:, k * LANES:(k + 1) * LANES] for k in tiles]
    si = [h0i_ref[:, k * LANES:(k + 1) * LANES] for k in tiles]
    for i in range(SSM_STEPS_S):
        rows = pl.ds(SSM_CHUNKS_P + i, DEC_BATCH, stride=SSM_STEPS_S)
        sr, si = advance(sr, si, rows, (s_s, rows))
    for k in tiles:
        fsr_ref[:, k * LANES:(k + 1) * LANES] = sr[k]
        fsi_ref[:, k * LANES:(k + 1) * LANES] = si[k]

    contract_last = (((1,), (1,)), ((), ()))
    for q in range(SSM_PB):
        states = jnp.concatenate([s_s[q].astype(BF16), s_s[SSM_PB + q].astype(BF16)], axis=1)
        carried = lax.dot_general(states, cc_ref[q], contract_last, preferred_element_type=F32)
        local = jnp.dot(x_ref[:, pair_cols[q]], w_ref[q], preferred_element_type=F32)
        y_ref[:, pair_cols[q]] = local + carried


def _ssm_main(x_c, w, bc, cc, a16r, a16i, h0r, h0i):
    row = SSM_GROUPS * SSM_STATE
    st = pl.BlockSpec((SSM_SEGS, SSM_STATE_COLS), lambda j: (0, j))
    st_shape = jax.ShapeDtypeStruct((SSM_SEGS, row), F32)
    proj = pl.BlockSpec((SSM_PB, SSM_PAIR_COLS, 2 * LANES), lambda j: (j, 0, 0))
    chunk_rows = pl.BlockSpec((None, N_CHUNKS, SSM_CHUNK_COLS), lambda j: (j, 0, 0))
    return pl.pallas_call(
        _ssm_main_kernel,
        out_shape=(jax.ShapeDtypeStruct((N_LANE_TILES, N_CHUNKS, SSM_CHUNK_COLS), F32),
                   st_shape, st_shape, st_shape, st_shape),
        grid=(N_LANE_TILES,),
        in_specs=[
            chunk_rows,
            pl.BlockSpec((SSM_PB, SSM_PAIR_COLS, SSM_PAIR_COLS), lambda j: (j, 0, 0)),
            proj, proj, st, st, st, st,
        ],
        out_specs=(chunk_rows, st, st, st, st),
        scratch_shapes=[
            pltpu.VMEM((2 * SSM_PB, N_CHUNKS, LANES), F32),
            pltpu.VMEM((2 * SSM_PB, N_CHUNKS, LANES), F32),
            pltpu.VMEM((2 * SSM_PB, SSM_CHUNKS_P, LANES), F32),
        ],
        compiler_params=_params(1),
        name="ssm_main",
    )(x_c, w, bc, cc, a16r, a16i, h0r, h0i)


def _ssm_post_kernel(y_ref, x_ref, g_ref, d_ref, o_ref, y_s):
    block_of_lane = lax.broadcasted_iota(jnp.int32, (CHUNKS_PER_TILE, LANES), 1) // SSM_PAIR_CH
    for j in range(N_LANE_TILES):
        for tg in range(SSM_T // SSM_VEC_TOKENS):
            pieces = [y_ref[j, :, q * SSM_PAIR_COLS + tg * LANES:q * SSM_PAIR_COLS + (tg + 1) * LANES]
                      for q in range(SSM_PB)]
            for u, out in enumerate(_transpose_lane_blocks(pieces, block_of_lane)):
                y_s[j, pl.ds(tg * SSM_VEC_TOKENS + u, CHUNKS_PER_TILE, stride=SSM_T), :] = out
    x = x_ref[...]
    ms = jnp.mean(x * x, axis=-1, keepdims=True)
    h = (x * lax.rsqrt(ms + EPS)) * g_ref[...]
    for j in range(N_LANE_TILES):
        cols = slice(j * LANES, (j + 1) * LANES)
        y = y_s[j] + d_ref[:, cols] * h[:, cols]
        gelu = 0.5 * y * (1.0 + jnp.tanh(GELU_TANH_SCALE * (y + GELU_TANH_CUBIC * (y * y * y))))
        o_ref[:, cols] = gelu.astype(BF16)


def _ssm_post(y_c, x, g, d):
    spec = pl.BlockSpec((BM, D_MODEL), lambda m: (m, 0))
    vec = pl.BlockSpec((1, D_MODEL), lambda m: (0, 0))
    return pl.pallas_call(
        _ssm_post_kernel,
        out_shape=jax.ShapeDtypeStruct((M_TOTAL, D_MODEL), BF16),
        grid=(N_ROW_TILES,),
        in_specs=[pl.BlockSpec((N_LANE_TILES, CHUNKS_PER_TILE, SSM_CHUNK_COLS), lambda m: (0, m, 0)),
                  spec, vec, vec],
        out_specs=spec,
        scratch_shapes=[pltpu.VMEM((N_LANE_TILES, BM, LANES), F32)],
        compiler_params=_params(1),
        name="ssm_post",
    )(y_c, x, g.reshape(1, D_MODEL), d.reshape(1, D_MODEL))


def kernel(x_prompt, x_sample, state_ret, state_ssm_re, state_ssm_im, cache_conv, norm_mix, norm_ffn, norm_final, ret_w_in, ret_gn, ret_w_out, ssm_a_re, ssm_a_im, ssm_log_dt, ssm_b_re, ssm_b_im, ssm_c_re, ssm_c_im, ssm_d, ssm_w_glu, ffn_w_up, ffn_conv_w, ffn_conv_b, ffn_w_down):
    x = (x_prompt.reshape(M_PROMPT, D_MODEL), x_sample.reshape(M_SAMPLE, D_MODEL))
    xb, ssq = _cast_ssq(*x)
    cos, sin = _rope_tables()

    ret_p, ret_s = [], None
    re_p, im_p, re_s, im_s = [], [], [], []
    conv_p, conv_s = [], []
    state_row = SSM_GROUPS * SSM_STATE
    for i in range(DEPTH):
        j = i // 2
        if i % 2 == 0:
            qk, vg = _mm_retin((xb, ssq, norm_mix[i]), ret_w_in, j, cos, sin)
            o_p, o_s, s_p, ret_s = _retention(qk, vg, ret_gn[j], state_ret, j, ret_s)
            ret_p.append(s_p)
            x, xb, ssq = _mm_res((o_p, o_s), ret_w_out, j, x, emit=True)
        else:
            h_c = _rmsnorm_chunked(x, norm_mix[i])
            bc, cc, w_ssm, a16r, a16i = _ssm_prep(ssm_a_re[j], ssm_a_im[j], ssm_log_dt[j],
                                               ssm_b_re[j], ssm_b_im[j], ssm_c_re[j], ssm_c_im[j])
            y_c, fpr, fpi, fsr, fsi = _ssm_main(
                h_c, w_ssm, bc, cc, a16r, a16i,
                state_ssm_re[j].reshape(DEC_BATCH, state_row),
                state_ssm_im[j].reshape(DEC_BATCH, state_row))
            re_p.append(fpr[0:1].reshape(1, SSM_GROUPS, SSM_STATE))
            im_p.append(fpi[0:1].reshape(1, SSM_GROUPS, SSM_STATE))
            re_s.append(fsr.reshape(DEC_BATCH, SSM_GROUPS, SSM_STATE))
            im_s.append(fsi.reshape(DEC_BATCH, SSM_GROUPS, SSM_STATE))
            gl = _ssm_post(y_c, x, norm_mix[i], ssm_d[j])
            x, xb, ssq = _mm_glu(gl, ssm_w_glu, j, x)

        u, tail_p, tail_s = _mm_ffnup((xb, ssq, norm_ffn[i]), ffn_w_up, i, ffn_conv_w[i],
                                      ffn_conv_b[i], cache_conv[i])
        conv_p.append(tail_p[SUBLANES - (CONV_W - 1):].reshape(1, CONV_W - 1, FFN_DIM))
        conv_s.append(tail_s.reshape(DEC_BATCH, SUBLANES, FFN_DIM)[:, SUBLANES - (CONV_W - 1):])
        if i + 1 < DEPTH and (i + 1) % 2 == 0:
            x, xb, ssq = _mm_res(u, ffn_w_down, i, x, emit=True)
        else:
            x = _mm_res(u, ffn_w_down, i, x)

    y_prompt = _rmsnorm(x, norm_final, F32, 0, N_PROMPT_TILES).reshape(1, SEQ, D_MODEL)
    y_sample = _rmsnorm(x, norm_final, F32, N_PROMPT_TILES, 1).reshape(DEC_BATCH, DEC_SEQ, D_MODEL)
    return (y_prompt, y_sample, jnp.stack(ret_p), ret_s,
            jnp.stack(re_p), jnp.stack(im_p), jnp.stack(re_s), jnp.stack(im_s),
            jnp.stack(conv_p), jnp.stack(conv_s))
```

```python
import functools
import math

import numpy as np
import jax
import jax.numpy as jnp
from jax import lax
from jax.experimental import pallas as pl
from jax.experimental.pallas import tpu as pltpu

F32 = jnp.float32
BF16 = jnp.bfloat16

D_MODEL = 2048
SEQ = 8192
DEPTH = 4
DEC_BATCH = 8
DEC_SEQ = 64
PAST_LEN = 1024
CHUNK = 64
RET_HEADS = 8
RET_DK = D_MODEL // RET_HEADS
RET_DV = 2 * RET_DK
RET_QK = RET_HEADS * RET_DK
RET_VD = RET_HEADS * RET_DV
N_RET = (DEPTH + 1) // 2
ROPE_BASE = 10000.0
SSM_GROUP = 16
SSM_GROUPS = D_MODEL // SSM_GROUP
SSM_STATE = 64
FFN_DIM = 2 * D_MODEL
CONV_W = 3
EPS = 1e-6

M_PROMPT = SEQ
M_SAMPLE = DEC_BATCH * DEC_SEQ
M_TOTAL = M_PROMPT + M_SAMPLE

LANES = 128
SUBLANES = 8
V7X_VMEM_BYTES = 64 * 1024 * 1024
VMEM_LIMIT_BYTES = V7X_VMEM_BYTES // 8 * 7

BM = M_SAMPLE
N_ROW_TILES = M_TOTAL // BM
N_PROMPT_TILES = M_PROMPT // BM
PROJ_BN = 1024
RET_IN_BN = 2048

RET_T = 256
RET_SUB = 4

SSM_T = 16
SSM_PAIRS = SSM_GROUPS // 2
SSM_PB = 4
SSM_SEGS = 8
SSM_STEPS_P = M_PROMPT // SSM_T // SSM_SEGS
SSM_STEPS_S = DEC_SEQ // SSM_T
SSM_PAIR_CH = 2 * SSM_GROUP
SSM_PAIR_COLS = SSM_T * SSM_PAIR_CH
SSM_VEC_TOKENS = LANES // SSM_PAIR_CH


def _params(n_axes):
    return pltpu.CompilerParams(dimension_semantics=("arbitrary",) * n_axes,
                                vmem_limit_bytes=VMEM_LIMIT_BYTES)


GELU_TANH_SCALE = math.sqrt(2.0 / math.pi)
GELU_TANH_CUBIC = 0.044715


def _sigmoid(x):
    return 1.0 / (1.0 + jnp.exp(-x))


def _rope_kernel(cos_ref, sin_ref):
    m = pl.program_id(0)
    half = RET_DK // 2
    freq = lax.broadcasted_iota(jnp.int32, (1, half), 1).astype(F32)
    inv = ROPE_BASE ** (-freq / half)
    r = m * BM + lax.broadcasted_iota(jnp.int32, (BM, half), 0)
    pos = jnp.where(r < M_PROMPT, r, PAST_LEN + ((r - M_PROMPT) & (DEC_SEQ - 1)))
    ang = pos.astype(F32) * inv
    cos_ref[...] = jnp.cos(ang)
    sin_ref[...] = jnp.sin(ang)


def _rope_tables():
    half = RET_DK // 2
    spec = pl.BlockSpec((BM, half), lambda m: (m, 0))
    return pl.pallas_call(
        _rope_kernel,
        out_shape=(jax.ShapeDtypeStruct((M_TOTAL, half), F32),) * 2,
        grid=(N_ROW_TILES,),
        out_specs=(spec, spec),
        compiler_params=_params(1),
        name="rope_tables",
    )()


def _norm_kernel(x_ref, g_ref, o_ref):
    x = x_ref[...]
    ms = jnp.mean(x * x, axis=-1, keepdims=True)
    o_ref[...] = ((x * lax.rsqrt(ms + EPS)) * g_ref[...]).astype(o_ref.dtype)


def _rmsnorm(x, g, out_dtype, tile0=0, n_tiles=N_ROW_TILES):
    return pl.pallas_call(
        _norm_kernel,
        out_shape=jax.ShapeDtypeStruct((n_tiles * BM, D_MODEL), out_dtype),
        grid=(n_tiles,),
        in_specs=[pl.BlockSpec((BM, D_MODEL), lambda m: (tile0 + m, 0)),
                  pl.BlockSpec((1, D_MODEL), lambda m: (0, 0))],
        out_specs=pl.BlockSpec((BM, D_MODEL), lambda m: (m, 0)),
        compiler_params=_params(1),
        name="rmsnorm",
    )(x, g.reshape(1, D_MODEL))


CHUNKS_PER_TILE = BM // 16
N_CHUNKS = M_TOTAL // 16
N_LANE_TILES = D_MODEL // LANES


def _transpose_lane_blocks(v, block_of_lane):
    n = len(v)
    out = []
    for u in range(n):
        acc = None
        for q in range(n):
            shift = ((q - u) * SSM_PAIR_CH) % LANES
            moved = v[q] if shift == 0 else pltpu.roll(v[q], shift, axis=1)
            acc = moved if acc is None else jnp.where(block_of_lane == q, moved, acc)
        out.append(acc)
    return out


def _norm_chunked_kernel(x_ref, g_ref, o_ref, h_s):
    x = x_ref[...]
    ms = jnp.mean(x * x, axis=-1, keepdims=True)
    h = (x * lax.rsqrt(ms + EPS)) * g_ref[...]
    for j in range(N_LANE_TILES):
        h_s[j] = h[:, j * LANES:(j + 1) * LANES]
    block_of_lane = lax.broadcasted_iota(jnp.int32, (CHUNKS_PER_TILE, LANES), 1) // SSM_PAIR_CH
    for j in range(N_LANE_TILES):
        for tg in range(SSM_T // SSM_VEC_TOKENS):
            rows = [h_s[j, pl.ds(tg * SSM_VEC_TOKENS + u, CHUNKS_PER_TILE, stride=SSM_T), :]
                    for u in range(SSM_VEC_TOKENS)]
            for q, out in enumerate(_transpose_lane_blocks(rows, block_of_lane)):
                c0 = q * SSM_PAIR_COLS + tg * LANES
                o_ref[j, :, c0:c0 + LANES] = out.astype(BF16)


def _rmsnorm_chunked(x, g):
    return pl.pallas_call(
        _norm_chunked_kernel,
        out_shape=jax.ShapeDtypeStruct((N_LANE_TILES, N_CHUNKS, SSM_T * LANES), BF16),
        grid=(N_ROW_TILES,),
        in_specs=[pl.BlockSpec((BM, D_MODEL), lambda m: (m, 0)),
                  pl.BlockSpec((1, D_MODEL), lambda m: (0, 0))],
        out_specs=pl.BlockSpec((N_LANE_TILES, CHUNKS_PER_TILE, SSM_T * LANES), lambda m: (0, m, 0)),
        scratch_shapes=[pltpu.VMEM((N_LANE_TILES, BM, LANES), F32)],
        compiler_params=_params(1),
        name="rmsnorm_chunked",
    )(x, g.reshape(1, D_MODEL))


W_SPEC = pl.BlockSpec(memory_space=pl.ANY)


def _weight_scratch(k, bn):
    return [pltpu.VMEM((k, bn), BF16), pltpu.VMEM((k, bn), F32), pltpu.SemaphoreType.DMA(())]


def _stage_weights(w_hbm, staged, *, layer, col0s, n_tiles, gain_ref=None):
    n = pl.program_id(0)
    m = pl.program_id(1)

    def copies(tile):
        out = []
        for (_, wf_ref, sem), col0 in zip(staged, col0s):
            bn = wf_ref.shape[1]
            cols = pl.ds(pl.multiple_of((col0 + tile) * bn, bn), bn)
            out.append(pltpu.make_async_copy(w_hbm.at[layer, :, cols], wf_ref, sem))
        return out

    @pl.when((n == 0) & (m == 0))
    def _():
        for copy in copies(0):
            copy.start()

    @pl.when(m == 0)
    def _():
        for copy, (wb_ref, wf_ref, _) in zip(copies(n), staged):
            copy.wait()
            w = wf_ref[...]
            if gain_ref is not None:
                w = w * gain_ref[...]
            wb_ref[...] = w.astype(BF16)

    @pl.when((m == 1) & (n + 1 < n_tiles))
    def _():
        for copy in copies(n + 1):
            copy.start()


def _emit_norm_inputs(x_new, xb_ref, ssq_ref):
    xb_ref[...] = x_new.astype(BF16)
    sq = x_new * x_new
    part = sq[:, 0:LANES]
    for t in range(1, x_new.shape[1] // LANES):
        part = part + sq[:, t * LANES:(t + 1) * LANES]
    ssq_ref[...] = part


def _row_scale(ssq_ref):
    total = jnp.sum(jnp.sum(ssq_ref[...], axis=0), axis=-1, keepdims=True)
    return lax.rsqrt(total * (1.0 / D_MODEL) + EPS)


def _norm_operands(norm):
    xb, ssq, g = norm
    args = [xb, ssq, g.reshape(D_MODEL, 1)]
    specs = [pl.BlockSpec((BM, D_MODEL), lambda n, m: (m, 0)),
             pl.BlockSpec((ssq.shape[0], BM, LANES), lambda n, m: (0, m, 0)),
             pl.BlockSpec((D_MODEL, 1), lambda n, m: (0, 0))]
    return args, specs


def _norm_outputs(n_out, bn):
    shapes = [jax.ShapeDtypeStruct((M_TOTAL, n_out), BF16),
              jax.ShapeDtypeStruct((n_out // bn, M_TOTAL, LANES), F32)]
    specs = [pl.BlockSpec((BM, bn), lambda n, m: (m, n)),
             pl.BlockSpec((None, BM, LANES), lambda n, m: (n, m, 0))]
    return shapes, specs


def _cast_ssq_kernel(xp_ref, xs_ref, xb_ref, ssq_ref):
    x = jnp.where(pl.program_id(0) < N_PROMPT_TILES, xp_ref[...], xs_ref[...])
    _emit_norm_inputs(x, xb_ref, ssq_ref)


def _cast_ssq(x_prompt, x_sample):
    return pl.pallas_call(
        _cast_ssq_kernel,
        out_shape=(jax.ShapeDtypeStruct((M_TOTAL, D_MODEL), BF16),
                   jax.ShapeDtypeStruct((1, M_TOTAL, LANES), F32)),
        grid=(N_ROW_TILES,),
        in_specs=[pl.BlockSpec((BM, D_MODEL), lambda m: (jnp.minimum(m, N_PROMPT_TILES - 1), 0)),
                  pl.BlockSpec((BM, D_MODEL), lambda m: (0, 0))],
        out_specs=(pl.BlockSpec((BM, D_MODEL), lambda m: (m, 0)),
                   pl.BlockSpec((None, BM, LANES), lambda m: (0, m, 0))),
        compiler_params=_params(1),
        name="cast_ssq",
    )(x_prompt, x_sample)


def _mm_qk_kernel(a_ref, ssq_ref, g_ref, w_hbm, cos_ref, sin_ref, o_ref, wb_ref, wf_ref, sem, *, stage):
    n = pl.program_id(0)
    n_q = RET_QK // RET_IN_BN
    _stage_weights(w_hbm, [(wb_ref, wf_ref, sem)], gain_ref=g_ref, **stage)

    acc = jnp.dot(a_ref[...], wb_ref[...], preferred_element_type=F32) * _row_scale(ssq_ref)
    scale = jnp.where(n >= n_q, RET_DK ** -0.5, 1.0).astype(F32)
    c = cos_ref[...]
    s = sin_ref[...]
    half = RET_DK // 2
    for hh in range(RET_IN_BN // RET_DK):
        lo = hh * RET_DK
        x1 = acc[:, lo:lo + half]
        x2 = acc[:, lo + half:lo + RET_DK]
        o_ref[:, lo:lo + half] = ((x1 * c - x2 * s) * scale).astype(BF16)
        o_ref[:, lo + half:lo + RET_DK] = ((x1 * s + x2 * c) * scale).astype(BF16)


def _mm_cast_kernel(a_ref, ssq_ref, g_ref, w_hbm, o_ref, wb_ref, wf_ref, sem, *, stage):
    _stage_weights(w_hbm, [(wb_ref, wf_ref, sem)], gain_ref=g_ref, **stage)

    acc = jnp.dot(a_ref[...], wb_ref[...], preferred_element_type=F32) * _row_scale(ssq_ref)
    o_ref[...] = acc.astype(BF16)


def _mm_retin(norm, w, layer, cos, sin):
    bn = RET_IN_BN
    _, k, n_out = w.shape
    half = RET_DK // 2
    n_qk = 2 * RET_QK // bn
    norm_args, norm_specs = _norm_operands(norm)
    out_spec = pl.BlockSpec((BM, bn), lambda n, m: (m, n))
    rope_spec = pl.BlockSpec((BM, half), lambda n, m: (m, 0))
    n_vg = n_out // bn - n_qk
    qk = pl.pallas_call(
        functools.partial(_mm_qk_kernel, stage=dict(layer=layer, col0s=(0,), n_tiles=n_qk)),
        out_shape=jax.ShapeDtypeStruct((M_TOTAL, 2 * RET_QK), BF16),
        grid=(n_qk, N_ROW_TILES),
        in_specs=norm_specs + [W_SPEC, rope_spec, rope_spec],
        out_specs=out_spec,
        scratch_shapes=_weight_scratch(k, bn),
        compiler_params=_params(2),
        name="mm_ret_qk",
    )(*norm_args, w, cos, sin)
    vg = pl.pallas_call(
        functools.partial(_mm_cast_kernel, stage=dict(layer=layer, col0s=(n_qk,), n_tiles=n_vg)),
        out_shape=jax.ShapeDtypeStruct((M_TOTAL, 2 * RET_VD), BF16),
        grid=(n_vg, N_ROW_TILES),
        in_specs=norm_specs + [W_SPEC],
        out_specs=out_spec,
        scratch_shapes=_weight_scratch(k, bn),
        compiler_params=_params(2),
        name="mm_ret_vg",
    )(*norm_args, w)
    return qk, vg


def _row_split_specs(cols, col_index):
    return [pl.BlockSpec((BM, cols), lambda n, m: (jnp.minimum(m, N_PROMPT_TILES - 1), col_index(n))),
            pl.BlockSpec((BM, cols), lambda n, m: (0, col_index(n)))]


def _mm_res_kernel(*refs, split_a, split_x, emit, stage):
    refs = list(refs)
    ap_ref = refs.pop(0)
    as_ref = refs.pop(0) if split_a else None
    w_hbm = refs.pop(0)
    xp_ref = refs.pop(0)
    xs_ref = refs.pop(0) if split_x else None
    o_ref = refs.pop(0)
    xb_ref, ssq_ref = (refs.pop(0), refs.pop(0)) if emit else (None, None)
    wb_ref, wf_ref, sem = refs
    m = pl.program_id(1)
    on_prompt = m < N_PROMPT_TILES
    _stage_weights(w_hbm, [(wb_ref, wf_ref, sem)], **stage)

    def body(a_ref):
        x = xp_ref[...]
        if split_x:
            x = jnp.where(on_prompt, x, xs_ref[...])
        x_new = x + jnp.dot(a_ref[...], wb_ref[...], preferred_element_type=F32)
        o_ref[...] = x_new
        if emit:
            _emit_norm_inputs(x_new, xb_ref, ssq_ref)

    if split_a:
        pl.when(on_prompt)(lambda: body(ap_ref))
        pl.when(jnp.logical_not(on_prompt))(lambda: body(as_ref))
    else:
        body(ap_ref)


def _mm_res(a, w, layer, x, emit=False):
    bn = PROJ_BN
    _, k, n_out = w.shape
    split_a = isinstance(a, tuple)
    split_x = isinstance(x, tuple)
    a_specs = _row_split_specs(k, lambda n: 0) if split_a else [pl.BlockSpec((BM, k), lambda n, m: (m, 0))]
    x_specs = _row_split_specs(bn, lambda n: n) if split_x else [pl.BlockSpec((BM, bn), lambda n, m: (m, n))]
    a_args = list(a) if split_a else [a]
    x_args = list(x) if split_x else [x]
    out_shapes = [jax.ShapeDtypeStruct((M_TOTAL, n_out), F32)]
    out_specs = [pl.BlockSpec((BM, bn), lambda n, m: (m, n))]
    if emit:
        shapes, specs = _norm_outputs(n_out, bn)
        out_shapes += shapes
        out_specs += specs
    out = pl.pallas_call(
        functools.partial(_mm_res_kernel, split_a=split_a, split_x=split_x, emit=emit,
                          stage=dict(layer=layer, col0s=(0,), n_tiles=n_out // bn)),
        out_shape=tuple(out_shapes),
        grid=(n_out // bn, N_ROW_TILES),
        in_specs=a_specs + [W_SPEC] + x_specs,
        out_specs=tuple(out_specs),
        scratch_shapes=_weight_scratch(k, bn),
        compiler_params=_params(2),
        name="mm_residual",
    )(*a_args, w, *x_args)
    return out if emit else out[0]


def _mm_glu_kernel(a_ref, w_hbm, x_ref, o_ref, xb_ref, ssq_ref,
                   wab_ref, waf_ref, sem_a, wgb_ref, wgf_ref, sem_g, *, stage):
    _stage_weights(w_hbm, [(wab_ref, waf_ref, sem_a), (wgb_ref, wgf_ref, sem_g)], **stage)

    a = a_ref[...]
    ga = jnp.dot(a, wab_ref[...], preferred_element_type=F32)
    gb = jnp.dot(a, wgb_ref[...], preferred_element_type=F32)
    x_new = x_ref[...] + ga * _sigmoid(gb)
    o_ref[...] = x_new
    _emit_norm_inputs(x_new, xb_ref, ssq_ref)


def _mm_glu(a, w, layer, x):
    bn = PROJ_BN
    _, k, n2 = w.shape
    n_out = n2 // 2
    nb = n_out // bn
    norm_shapes, norm_specs = _norm_outputs(n_out, bn)
    return pl.pallas_call(
        functools.partial(_mm_glu_kernel, stage=dict(layer=layer, col0s=(0, nb), n_tiles=nb)),
        out_shape=(jax.ShapeDtypeStruct((M_TOTAL, n_out), F32), *norm_shapes),
        grid=(nb, N_ROW_TILES),
        in_specs=[
            pl.BlockSpec((BM, k), lambda n, m: (m, 0)),
            W_SPEC,
            pl.BlockSpec((BM, bn), lambda n, m: (m, n)),
        ],
        out_specs=(pl.BlockSpec((BM, bn), lambda n, m: (m, n)), *norm_specs),
        scratch_shapes=_weight_scratch(k, bn) + _weight_scratch(k, bn),
        compiler_params=_params(2),
        name="mm_glu",
    )(a, w, x)


def _mm_ffnup_kernel(h_ref, ssq_ref, g_ref, w_hbm, cw_ref, cb_ref, cache_ref,
                     o_ref, tp_ref, ts_ref,
                     wab_ref, waf_ref, sem_a, wgb_ref, wgf_ref, sem_g, carry_ref, *, stage):
    m = pl.program_id(1)
    _stage_weights(w_hbm, [(wab_ref, waf_ref, sem_a), (wgb_ref, wgf_ref, sem_g)], gain_ref=g_ref, **stage)

    @pl.when(m == 0)
    def _():
        carry_ref[...] = jnp.zeros_like(carry_ref)

    h = h_ref[...]
    r = _row_scale(ssq_ref)
    a = jnp.dot(h, wab_ref[...], preferred_element_type=F32) * r
    b = jnp.dot(h, wgb_ref[...], preferred_element_type=F32) * r
    r1 = pltpu.roll(a, 1, axis=0)
    r2 = pltpu.roll(a, 2, axis=0)
    cb = cb_ref[...]
    w0 = cw_ref[0:1, :]
    w1 = cw_ref[1:2, :]
    w2 = cw_ref[2:3, :]

    def gated(a_rows, prev1, prev2, b_rows):
        conv = cb + w0 * prev2 + w1 * prev1 + w2 * a_rows
        return ((conv * _sigmoid(conv)) * b_rows).astype(BF16)

    o_ref[...] = gated(a, r1, r2, b)

    row8 = lax.broadcasted_iota(jnp.int32, (SUBLANES, 1), 0)
    heads = [tuple(z[s * DEC_SEQ:s * DEC_SEQ + SUBLANES] for z in (a, r1, r2, b))
             for s in range(DEC_BATCH)]
    tails = [a[(s + 1) * DEC_SEQ - SUBLANES:(s + 1) * DEC_SEQ] for s in range(DEC_BATCH)]

    def redo_head(s, before1, before2):
        a8, r1_8, r2_8, b8 = heads[s]
        prev1 = jnp.where(row8 == 0, before1, r1_8)
        prev2 = jnp.where(row8 == 0, before2, jnp.where(row8 == 1, before1, r2_8))
        o_ref[s * DEC_SEQ:s * DEC_SEQ + SUBLANES, :] = gated(a8, prev1, prev2, b8)

    @pl.when(m < N_PROMPT_TILES)
    def _():
        c = carry_ref[...]
        redo_head(0, c[SUBLANES - 1:SUBLANES, :], c[SUBLANES - 2:SUBLANES - 1, :])
        carry_ref[...] = tails[DEC_BATCH - 1]

        @pl.when(m == N_PROMPT_TILES - 1)
        def _():
            tp_ref[...] = tails[DEC_BATCH - 1]

    @pl.when(m == N_PROMPT_TILES)
    def _():
        for s in range(DEC_BATCH):
            c = cache_ref[s]
            redo_head(s, c[1:2, :], c[0:1, :])
            ts_ref[s * SUBLANES:(s + 1) * SUBLANES, :] = tails[s]


def _mm_ffnup(norm, w, layer, conv_w, conv_b, cache):
    bn = PROJ_BN
    k = w.shape[1]
    nb = FFN_DIM // bn
    norm_args, norm_specs = _norm_operands(norm)
    return pl.pallas_call(
        functools.partial(_mm_ffnup_kernel, stage=dict(layer=layer, col0s=(0, nb), n_tiles=nb)),
        out_shape=(
            jax.ShapeDtypeStruct((M_TOTAL, FFN_DIM), BF16),
            jax.ShapeDtypeStruct((SUBLANES, FFN_DIM), F32),
            jax.ShapeDtypeStruct((DEC_BATCH * SUBLANES, FFN_DIM), F32),
        ),
        grid=(nb, N_ROW_TILES),
        in_specs=norm_specs + [
            W_SPEC,
            pl.BlockSpec((CONV_W, bn), lambda n, m: (0, n)),
            pl.BlockSpec((1, bn), lambda n, m: (0, n)),
            pl.BlockSpec((DEC_BATCH, CONV_W - 1, bn), lambda n, m: (0, 0, n)),
        ],
        out_specs=(
            pl.BlockSpec((BM, bn), lambda n, m: (m, n)),
            pl.BlockSpec((SUBLANES, bn), lambda n, m: (0, n)),
            pl.BlockSpec((DEC_BATCH * SUBLANES, bn), lambda n, m: (0, n)),
        ),
        scratch_shapes=_weight_scratch(k, bn) + _weight_scratch(k, bn) + [pltpu.VMEM((SUBLANES, bn), F32)],
        compiler_params=_params(2),
        name="mm_ffn_up",
    )(*norm_args, w, conv_w, conv_b.reshape(1, FFN_DIM), cache)


def _ret_tables(lg, T):
    i = lax.broadcasted_iota(jnp.int32, (T, T), 0)
    j = lax.broadcasted_iota(jnp.int32, (T, T), 1)
    dist = jnp.abs(i - j).astype(F32)
    shift = int(math.log2(CHUNK))
    visible = (j >> shift) <= (i >> shift)
    mask = jnp.where(visible, jnp.exp(dist * lg), 0.0)
    t = lax.broadcasted_iota(jnp.int32, (T, 1), 0).astype(F32)
    cross = jnp.exp((t + 1.0) * lg)
    k_dec = jnp.exp((T - 1.0 - t) * lg)
    decay = jnp.exp(float(T) * lg)
    return mask, cross, k_dec, decay


def _ret_block(q, k, v, g, state, tables, gn):
    mask, cross, k_dec, decay = tables
    scores = lax.dot_general(q, k, (((1,), (1,)), ((), ())), preferred_element_type=F32)
    scores = scores * mask
    out = jnp.dot(scores.astype(BF16), v, preferred_element_type=F32)
    out = out + jnp.dot(q, state.astype(BF16), preferred_element_type=F32) * cross

    kd = (k.astype(F32) * k_dec).astype(BF16)
    new_state = decay * state + lax.dot_general(
        kd, v, (((0,), (0,)), ((), ())), preferred_element_type=F32)

    mu = jnp.mean(out, axis=-1, keepdims=True)
    oc = out - mu
    var = jnp.mean(oc * oc, axis=-1, keepdims=True)
    normed = (oc * lax.rsqrt(var + EPS)) * gn
    g = g.astype(F32)
    return (normed * (g * _sigmoid(g))).astype(BF16), new_state


def _ret_prompt_kernel(lg_ref, q_ref, k_ref, v_ref, g_ref, gn_ref, o_ref, sl_ref, state_ref, mask_ref):
    blk = pl.program_id(1)
    lg = lg_ref[...][:, 0:1]

    @pl.when(blk == 0)
    def _():
        state_ref[...] = jnp.zeros_like(state_ref)
        mask_ref[...] = _ret_tables(lg, RET_T)[0]

    _, cross, k_dec, decay = _ret_tables(lg, RET_T)
    tables = (mask_ref[...], cross, k_dec, decay)
    gn = gn_ref[...]
    state = state_ref[...]
    for sb in range(RET_SUB):
        rows = slice(sb * RET_T, (sb + 1) * RET_T)
        o_ref[rows, :], state = _ret_block(q_ref[rows, :], k_ref[rows, :], v_ref[rows, :],
                                           g_ref[rows, :], state, tables, gn)
    state_ref[...] = state

    @pl.when(blk == pl.num_programs(1) - 1)
    def _():
        sl_ref[...] = state


def _ret_sample_kernel(lg_ref, q_ref, k_ref, v_ref, g_ref, gn_ref, s0_ref, *rest, layer):
    o_ref, sl_ref = rest[-2:]
    if len(rest) == 2:
        for other in range(N_RET):
            if other != layer:
                sl_ref[other] = jnp.zeros(sl_ref.shape[1:], F32)
        sl_ref = sl_ref.at[layer]
    lg = lg_ref[...][:, 0:1]
    tables = _ret_tables(lg, DEC_SEQ)
    gn = gn_ref[...]
    for s in range(DEC_BATCH):
        rows = slice(s * DEC_SEQ, (s + 1) * DEC_SEQ)
        o_ref[rows, :], sl_ref[s] = _ret_block(q_ref[rows, :], k_ref[rows, :], v_ref[rows, :],
                                               g_ref[rows, :], s0_ref[s], tables, gn)


def _retention(qk, vg, gn, s0_stack, layer, sample_states=None):
    log_g = np.log1p(-np.exp2(-5.0 - np.arange(RET_HEADS, dtype=np.float32))).astype(np.float32)
    lg = jnp.asarray(np.broadcast_to(log_g[:, None, None], (RET_HEADS, 1, LANES)).copy())
    gn = gn.reshape(1, RET_VD)
    k_off = RET_QK // RET_DK
    g_off = RET_VD // RET_DV
    rows_p = RET_T * RET_SUB
    o_p, s_p = pl.pallas_call(
        _ret_prompt_kernel,
        out_shape=(jax.ShapeDtypeStruct((M_PROMPT, RET_VD), BF16),
                   jax.ShapeDtypeStruct((1, RET_HEADS, RET_DK, RET_DV), F32)),
        grid=(RET_HEADS, M_PROMPT // rows_p),
        in_specs=[
            pl.BlockSpec((None, 1, LANES), lambda h, b: (h, 0, 0)),
            pl.BlockSpec((rows_p, RET_DK), lambda h, b: (b, h)),
            pl.BlockSpec((rows_p, RET_DK), lambda h, b: (b, k_off + h)),
            pl.BlockSpec((rows_p, RET_DV), lambda h, b: (b, h)),
            pl.BlockSpec((rows_p, RET_DV), lambda h, b: (b, g_off + h)),
            pl.BlockSpec((1, RET_DV), lambda h, b: (0, h)),
        ],
        out_specs=(pl.BlockSpec((rows_p, RET_DV), lambda h, b: (b, h)),
                   pl.BlockSpec((None, None, RET_DK, RET_DV), lambda h, b: (0, h, 0, 0))),
        scratch_shapes=[pltpu.VMEM((RET_DK, RET_DV), F32), pltpu.VMEM((RET_T, RET_T), F32)],
        compiler_params=_params(2),
        name="retention",
    )(lg, qk, qk, vg, vg, gn)
    sample_block = M_PROMPT // M_SAMPLE
    carried = [] if sample_states is None else [sample_states]
    n_in = 7
    if carried:
        state_spec = pl.BlockSpec((None, DEC_BATCH, None, RET_DK, RET_DV), lambda h: (layer, 0, h, 0, 0))
    else:
        state_spec = pl.BlockSpec((N_RET, DEC_BATCH, None, RET_DK, RET_DV), lambda h: (0, 0, h, 0, 0))
    o_s, s_s = pl.pallas_call(
        functools.partial(_ret_sample_kernel, layer=layer),
        out_shape=(jax.ShapeDtypeStruct((M_SAMPLE, RET_VD), BF16),
                   jax.ShapeDtypeStruct((N_RET, DEC_BATCH, RET_HEADS, RET_DK, RET_DV), F32)),
        grid=(RET_HEADS,),
        in_specs=[
            pl.BlockSpec((None, 1, LANES), lambda h: (h, 0, 0)),
            pl.BlockSpec((M_SAMPLE, RET_DK), lambda h: (sample_block, h)),
            pl.BlockSpec((M_SAMPLE, RET_DK), lambda h: (sample_block, k_off + h)),
            pl.BlockSpec((M_SAMPLE, RET_DV), lambda h: (sample_block, h)),
            pl.BlockSpec((M_SAMPLE, RET_DV), lambda h: (sample_block, g_off + h)),
            pl.BlockSpec((1, RET_DV), lambda h: (0, h)),
            pl.BlockSpec((None, DEC_BATCH, None, RET_DK, RET_DV), lambda h: (layer, 0, h, 0, 0)),
        ] + [pl.BlockSpec(memory_space=pl.ANY) for _ in carried],
        out_specs=(pl.BlockSpec((M_SAMPLE, RET_DV), lambda h: (0, h)), state_spec),
        input_output_aliases={n_in: 1} if carried else {},
        compiler_params=_params(1),
        name="retention_sample",
    )(lg, qk, qk, vg, vg, gn, s0_stack, *carried)
    return o_p, o_s, s_p, s_s


def _ssm_prep_kernel(ar_ref, ai_ref, ldt_ref, br_ref, bi_ref, cr_ref, ci_ref,
                     bc_ref, cc_ref, w_ref, a16r_ref, a16i_ref, vr_s, vi_s):
    ar = ar_ref[...]
    ai = ai_ref[...]
    dt = jnp.exp(ldt_ref[...])
    mag = jnp.exp(ar * dt)
    ang = ai * dt
    abr = mag * jnp.cos(ang)
    abi = mag * jnp.sin(ang)
    den = ar * ar + ai * ai
    nr = abr - 1.0
    ni = abi
    cfr = (nr * ar + ni * ai) / den
    cfi = (ni * ar - nr * ai) / den

    def b4(z):
        return z[:, None]

    br = br_ref[...]
    bi = bi_ref[...]
    bbr = b4(cfr) * br - b4(cfi) * bi
    bbi = b4(cfr) * bi + b4(cfi) * br
    cre = cr_ref[...]
    cim = ci_ref[...]

    powers = []
    pr = jnp.ones_like(ar)
    pi = jnp.zeros_like(ar)
    for _ in range(SSM_T + 1):
        powers.append((pr, pi))
        pr, pi = pr * abr - pi * abi, pr * abi + pi * abr
    a16r_ref[...] = powers[SSM_T][0]
    a16i_ref[...] = powers[SSM_T][1]

    for m in range(SSM_T + 1):
        pr, pi = powers[m]
        vr = b4(pr) * cre - b4(pi) * cim
        vi = b4(pr) * cim + b4(pi) * cre
        if m < SSM_T:
            vr_s[:, m] = vr
            vi_s[:, m] = vi
        if m >= 1:
            cc_ref[:, m - 1, :, :, 0:LANES] = vr.astype(BF16)
            cc_ref[:, m - 1, :, :, LANES:2 * LANES] = (-vi).astype(BF16)

    for t in range(SSM_T):
        pr, pi = powers[SSM_T - 1 - t]
        bc_ref[:, t, :, :, 0:LANES] = (b4(pr) * bbr - b4(pi) * bbi).astype(BF16)
        bc_ref[:, t, :, :, LANES:2 * LANES] = (b4(pr) * bbi + b4(pi) * bbr).astype(BF16)

    lane = lax.broadcasted_iota(jnp.int32, (SSM_PAIR_CH, SSM_PAIR_COLS), 1)
    contract_last = (((1,), (1,)), ((), ()))
    for q in range(SSM_PB):
        vr_all = vr_s[q].reshape(SSM_PAIR_COLS, LANES)
        vi_all = vi_s[q].reshape(SSM_PAIR_COLS, LANES)
        lags = (lax.dot_general(bbr[q].reshape(SSM_PAIR_CH, LANES), vr_all, contract_last,
                                precision=lax.Precision.HIGHEST, preferred_element_type=F32)
                - lax.dot_general(bbi[q].reshape(SSM_PAIR_CH, LANES), vi_all, contract_last,
                                  precision=lax.Precision.HIGHEST, preferred_element_type=F32))
        for t in range(SSM_T):
            shifted = lags if t == 0 else pltpu.roll(lags, SSM_PAIR_CH * t, axis=1)
            w_ref[q, t * SSM_PAIR_CH:(t + 1) * SSM_PAIR_CH, :] = jnp.where(
                lane >= SSM_PAIR_CH * t, shifted, 0.0).astype(BF16)


def _pair_pack(x):
    x = x.reshape(SSM_PAIRS, 2, SSM_GROUP, SSM_STATE)
    lo = jnp.pad(x[:, 0], ((0, 0), (0, 0), (0, SSM_STATE)))
    hi = jnp.pad(x[:, 1], ((0, 0), (0, 0), (SSM_STATE, 0)))
    return jnp.stack([lo, hi], axis=1)


def _ssm_prep(a_re, a_im, log_dt, b_re, b_im, c_re, c_im):
    assert 2 * SSM_PB * SSM_GROUP == LANES
    ar = a_re.reshape(SSM_PAIRS, 1, LANES)
    ai = a_im.reshape(SSM_PAIRS, 1, LANES)
    ldt = jnp.repeat(log_dt, SSM_STATE).reshape(SSM_PAIRS, 1, LANES)
    b2r = _pair_pack(jnp.transpose(b_re, (0, 2, 1)))
    b2i = _pair_pack(jnp.transpose(b_im, (0, 2, 1)))
    c2r = _pair_pack(c_re)
    c2i = _pair_pack(c_im)
    vec = pl.BlockSpec((SSM_PB, 1, LANES), lambda p: (p, 0, 0))
    mat = pl.BlockSpec((SSM_PB, 2, SSM_GROUP, LANES), lambda p: (p, 0, 0, 0))
    proj = pl.BlockSpec((SSM_PB, SSM_T, 2, SSM_GROUP, 2 * LANES), lambda p: (p, 0, 0, 0, 0))
    proj_shape = jax.ShapeDtypeStruct((SSM_PAIRS, SSM_T, 2, SSM_GROUP, 2 * LANES), BF16)
    bc, cc, w, a16r, a16i = pl.pallas_call(
        _ssm_prep_kernel,
        out_shape=(
            proj_shape, proj_shape,
            jax.ShapeDtypeStruct((SSM_PAIRS, SSM_PAIR_COLS, SSM_PAIR_COLS), BF16),
            jax.ShapeDtypeStruct((SSM_PAIRS, 1, LANES), F32),
            jax.ShapeDtypeStruct((SSM_PAIRS, 1, LANES), F32),
        ),
        grid=(N_LANE_TILES,),
        in_specs=[vec, vec, vec, mat, mat, mat, mat],
        out_specs=(proj, proj,
                   pl.BlockSpec((SSM_PB, SSM_PAIR_COLS, SSM_PAIR_COLS), lambda p: (p, 0, 0)),
                   vec, vec),
        scratch_shapes=[pltpu.VMEM((SSM_PB, SSM_T, 2, SSM_GROUP, LANES), F32)] * 2,
        compiler_params=_params(1),
        name="ssm_prep",
    )(ar, ai, ldt, b2r, b2i, c2r, c2i)
    bc = bc.reshape(SSM_PAIRS, SSM_PAIR_COLS, 2 * LANES)
    cc = cc.reshape(SSM_PAIRS, SSM_PAIR_COLS, 2 * LANES)
    row = SSM_GROUPS * SSM_STATE
    a16r = jnp.broadcast_to(a16r.reshape(1, row), (SSM_SEGS, row))
    a16i = jnp.broadcast_to(a16i.reshape(1, row), (SSM_SEGS, row))
    return bc, cc, w, a16r, a16i


SSM_CHUNK_COLS = SSM_T * LANES
SSM_STATE_COLS = SSM_PB * LANES
SSM_CHUNKS_P = M_PROMPT // SSM_T
SSM_SCAN_UNROLL = 8


def _ssm_main_kernel(x_ref, w_ref, bc_ref, cc_ref, ar_ref, ai_ref, h0r_ref, h0i_ref,
                     y_ref, fpr_ref, fpi_ref, fsr_ref, fsi_ref, d_s, s_s, t_s):
    pair_cols = [slice(q * SSM_PAIR_COLS, (q + 1) * SSM_PAIR_COLS) for q in range(SSM_PB)]

    for q in range(SSM_PB):
        d = jnp.dot(x_ref[:, pair_cols[q]], bc_ref[q], preferred_element_type=F32)
        d_s[q] = d[:, 0:LANES]
        d_s[SSM_PB + q] = d[:, LANES:2 * LANES]

    tiles = range(SSM_PB)
    ar = [ar_ref[:, k * LANES:(k + 1) * LANES] for k in tiles]
    ai = [ai_ref[:, k * LANES:(k + 1) * LANES] for k in tiles]

    def advance(sr, si, rows, record=None):
        out_r, out_i = [], []
        for k in tiles:
            if record is not None:
                ref, rec_rows = record
                ref[k, rec_rows, :] = sr[k]
                ref[SSM_PB + k, rec_rows, :] = si[k]
            dr = d_s[k, rows, :]
            di = d_s[SSM_PB + k, rows, :]
            out_r.append(ar[k] * sr[k] - ai[k] * si[k] + dr)
            out_i.append(ar[k] * si[k] + ai[k] * sr[k] + di)
        return out_r, out_i

    def prompt_rows(i):
        return pl.ds(i, SSM_SEGS, stride=SSM_STEPS_P)

    def prompt_pass(start_r, start_i, record):
        def body(i, carry):
            rec = (t_s, pl.ds(pl.multiple_of(i * SSM_SEGS, SSM_SEGS), SSM_SEGS)) if record else None
            out_r, out_i = advance(list(carry[0]), list(carry[1]), prompt_rows(i), rec)
            return tuple(out_r), tuple(out_i)
        out_r, out_i = lax.fori_loop(0, SSM_STEPS_P, body, (tuple(start_r), tuple(start_i)),
                                     unroll=SSM_SCAN_UNROLL)
        return list(out_r), list(out_i)

    zeros = [jnp.zeros((SSM_SEGS, LANES), F32) for _ in tiles]
    er, ei = prompt_pass(zeros, zeros, record=False)

    sr, si = [], []
    for k in tiles:
        pr, pi = ar[k][0:1], ai[k][0:1]
        for _ in range(int(math.log2(SSM_STEPS_P))):
            pr, pi = pr * pr - pi * pi, 2.0 * pr * pi
        cr = jnp.zeros((1, LANES), F32)
        ci = cr
        starts_r, starts_i = [], []
        for s in range(SSM_SEGS):
            starts_r.append(cr)
            starts_i.append(ci)
            cr, ci = (pr * cr - pi * ci + er[k][s:s + 1], pr * ci + pi * cr + ei[k][s:s + 1])
        cols = slice(k * LANES, (k + 1) * LANES)
        fpr_ref[:, cols] = jnp.broadcast_to(cr, (SSM_SEGS, LANES))
        fpi_ref[:, cols] = jnp.broadcast_to(ci, (SSM_SEGS, LANES))
        sr.append(jnp.concatenate(starts_r, axis=0))
        si.append(jnp.concatenate(starts_i, axis=0))

    prompt_pass(sr, si, record=True)
    for k in range(2 * SSM_PB):
        for s in range(SSM_SEGS):
            s_s[k, s * SSM_STEPS_P:(s + 1) * SSM_STEPS_P, :] = t_s[k, pl.ds(s, SSM_STEPS_P, stride=SSM_SEGS), :]
    sr = [h0r_ref[:, k * LANES:(k + 1) * LANES] for k in tiles]
    si = [h0i_ref[:, k * LANES:(k + 1) * LANES] for k in tiles]
    for i in range(SSM_STEPS_S):
        rows = pl.ds(SSM_CHUNKS_P + i, DEC_BATCH, stride=SSM_STEPS_S)
        sr, si = advance(sr, si, rows, (s_s, rows))
    for k in tiles:
        fsr_ref[:, k * LANES:(k + 1) * LANES] = sr[k]
        fsi_ref[:, k * LANES:(k + 1) * LANES] = si[k]

    contract_last = (((1,), (1,)), ((), ()))
    for q in range(SSM_PB):
        states = jnp.concatenate([s_s[q].astype(BF16), s_s[SSM_PB + q].astype(BF16)], axis=1)
        carried = lax.dot_general(states, cc_ref[q], contract_last, preferred_element_type=F32)
        local = jnp.dot(x_ref[:, pair_cols[q]], w_ref[q], preferred_element_type=F32)
        y_ref[:, pair_cols[q]] = local + carried


def _ssm_main(x_c, w, bc, cc, a16r, a16i, h0r, h0i):
    row = SSM_GROUPS * SSM_STATE
    st = pl.BlockSpec((SSM_SEGS, SSM_STATE_COLS), lambda j: (0, j))
    st_shape = jax.ShapeDtypeStruct((SSM_SEGS, row), F32)
    proj = pl.BlockSpec((SSM_PB, SSM_PAIR_COLS, 2 * LANES), lambda j: (j, 0, 0))
    chunk_rows = pl.BlockSpec((None, N_CHUNKS, SSM_CHUNK_COLS), lambda j: (j, 0, 0))
    return pl.pallas_call(
        _ssm_main_kernel,
        out_shape=(jax.ShapeDtypeStruct((N_LANE_TILES, N_CHUNKS, SSM_CHUNK_COLS), F32),
                   st_shape, st_shape, st_shape, st_shape),
        grid=(N_LANE_TILES,),
        in_specs=[
            chunk_rows,
            pl.BlockSpec((SSM_PB, SSM_PAIR_COLS, SSM_PAIR_COLS), lambda j: (j, 0, 0)),
            proj, proj, st, st, st, st,
        ],
        out_specs=(chunk_rows, st, st, st, st),
        scratch_shapes=[
            pltpu.VMEM((2 * SSM_PB, N_CHUNKS, LANES), F32),
            pltpu.VMEM((2 * SSM_PB, N_CHUNKS, LANES), F32),
            pltpu.VMEM((2 * SSM_PB, SSM_CHUNKS_P, LANES), F32),
        ],
        compiler_params=_params(1),
        name="ssm_main",
    )(x_c, w, bc, cc, a16r, a16i, h0r, h0i)


def _ssm_post_kernel(y_ref, x_ref, g_ref, d_ref, o_ref, y_s):
    block_of_lane = lax.broadcasted_iota(jnp.int32, (CHUNKS_PER_TILE, LANES), 1) // SSM_PAIR_CH
    for j in range(N_LANE_TILES):
        for tg in range(SSM_T // SSM_VEC_TOKENS):
            pieces = [y_ref[j, :, q * SSM_PAIR_COLS + tg * LANES:q * SSM_PAIR_COLS + (tg + 1) * LANES]
                      for q in range(SSM_PB)]
            for u, out in enumerate(_transpose_lane_blocks(pieces, block_of_lane)):
                y_s[j, pl.ds(tg * SSM_VEC_TOKENS + u, CHUNKS_PER_TILE, stride=SSM_T), :] = out
    x = x_ref[...]
    ms = jnp.mean(x * x, axis=-1, keepdims=True)
    h = (x * lax.rsqrt(ms + EPS)) * g_ref[...]
    for j in range(N_LANE_TILES):
        cols = slice(j * LANES, (j + 1) * LANES)
        y = y_s[j] + d_ref[:, cols] * h[:, cols]
        gelu = 0.5 * y * (1.0 + jnp.tanh(GELU_TANH_SCALE * (y + GELU_TANH_CUBIC * (y * y * y))))
        o_ref[:, cols] = gelu.astype(BF16)


def _ssm_post(y_c, x, g, d):
    spec = pl.BlockSpec((BM, D_MODEL), lambda m: (m, 0))
    vec = pl.BlockSpec((1, D_MODEL), lambda m: (0, 0))
    return pl.pallas_call(
        _ssm_post_kernel,
        out_shape=jax.ShapeDtypeStruct((M_TOTAL, D_MODEL), BF16),
        grid=(N_ROW_TILES,),
        in_specs=[pl.BlockSpec((N_LANE_TILES, CHUNKS_PER_TILE, SSM_CHUNK_COLS), lambda m: (0, m, 0)),
                  spec, vec, vec],
        out_specs=spec,
        scratch_shapes=[pltpu.VMEM((N_LANE_TILES, BM, LANES), F32)],
        compiler_params=_params(1),
        name="ssm_post",
    )(y_c, x, g.reshape(1, D_MODEL), d.reshape(1, D_MODEL))


def kernel(x_prompt, x_sample, state_ret, state_ssm_re, state_ssm_im, cache_conv, norm_mix, norm_ffn, norm_final, ret_w_in, ret_gn, ret_w_out, ssm_a_re, ssm_a_im, ssm_log_dt, ssm_b_re, ssm_b_im, ssm_c_re, ssm_c_im, ssm_d, ssm_w_glu, ffn_w_up, ffn_conv_w, ffn_conv_b, ffn_w_down):
    x = (x_prompt.reshape(M_PROMPT, D_MODEL), x_sample.reshape(M_SAMPLE, D_MODEL))
    xb, ssq = _cast_ssq(*x)
    cos, sin = _rope_tables()

    ret_p, ret_s = [], None
    re_p, im_p, re_s, im_s = [], [], [], []
    conv_p, conv_s = [], []
    state_row = SSM_GROUPS * SSM_STATE
    for i in range(DEPTH):
        j = i // 2
        if i % 2 == 0:
            qk, vg = _mm_retin((xb, ssq, norm_mix[i]), ret_w_in, j, cos, sin)
            o_p, o_s, s_p, ret_s = _retention(qk, vg, ret_gn[j], state_ret, j, ret_s)
            ret_p.append(s_p)
            x, xb, ssq = _mm_res((o_p, o_s), ret_w_out, j, x, emit=True)
        else:
            h_c = _rmsnorm_chunked(x, norm_mix[i])
            bc, cc, w_ssm, a16r, a16i = _ssm_prep(ssm_a_re[j], ssm_a_im[j], ssm_log_dt[j],
                                               ssm_b_re[j], ssm_b_im[j], ssm_c_re[j], ssm_c_im[j])
            y_c, fpr, fpi, fsr, fsi = _ssm_main(
                h_c, w_ssm, bc, cc, a16r, a16i,
                state_ssm_re[j].reshape(DEC_BATCH, state_row),
                state_ssm_im[j].reshape(DEC_BATCH, state_row))
            re_p.append(fpr[0:1].reshape(1, SSM_GROUPS, SSM_STATE))
            im_p.append(fpi[0:1].reshape(1, SSM_GROUPS, SSM_STATE))
            re_s.append(fsr.reshape(DEC_BATCH, SSM_GROUPS, SSM_STATE))
            im_s.append(fsi.reshape(DEC_BATCH, SSM_GROUPS, SSM_STATE))
            gl = _ssm_post(y_c, x, norm_mix[i], ssm_d[j])
            x, xb, ssq = _mm_glu(gl, ssm_w_glu, j, x)

        u, tail_p, tail_s = _mm_ffnup((xb, ssq, norm_ffn[i]), ffn_w_up, i, ffn_conv_w[i],
                                      ffn_conv_b[i], cache_conv[i])
        conv_p.append(tail_p[SUBLANES - (CONV_W - 1):].reshape(1, CONV_W - 1, FFN_DIM))
        conv_s.append(tail_s.reshape(DEC_BATCH, SUBLANES, FFN_DIM)[:, SUBLANES - (CONV_W - 1):])
        if i + 1 < DEPTH and (i + 1) % 2 == 0:
            x, xb, ssq = _mm_res(u, ffn_w_down, i, x, emit=True)
        else:
            x = _mm_res(u, ffn_w_down, i, x)

    y_prompt = _rmsnorm(x, norm_final, F32, 0, N_PROMPT_TILES).reshape(1, SEQ, D_MODEL)
    y_sample = _rmsnorm(x, norm_final, F32, N_PROMPT_TILES, 1).reshape(DEC_BATCH, DEC_SEQ, D_MODEL)
    return (y_prompt, y_sample, jnp.stack(ret_p), ret_s,
            jnp.stack(re_p), jnp.stack(im_p), jnp.stack(re_s), jnp.stack(im_s),
            jnp.stack(conv_p), jnp.stack(conv_s))
```

```python
import functools
import math

import numpy as np
import jax
import jax.numpy as jnp
from jax import lax
from jax.experimental import pallas as pl
from jax.experimental.pallas import tpu as pltpu

F32 = jnp.float32
BF16 = jnp.bfloat16

D_MODEL = 2048
SEQ = 8192
DEPTH = 4
DEC_BATCH = 8
DEC_SEQ = 64
PAST_LEN = 1024
CHUNK = 64
RET_HEADS = 8
RET_DK = D_MODEL // RET_HEADS
RET_DV = 2 * RET_DK
RET_QK = RET_HEADS * RET_DK
RET_VD = RET_HEADS * RET_DV
N_RET = (DEPTH + 1) // 2
ROPE_BASE = 10000.0
SSM_GROUP = 16
SSM_GROUPS = D_MODEL // SSM_GROUP
SSM_STATE = 64
FFN_DIM = 2 * D_MODEL
CONV_W = 3
EPS = 1e-6

M_PROMPT = SEQ
M_SAMPLE = DEC_BATCH * DEC_SEQ
M_TOTAL = M_PROMPT + M_SAMPLE

LANES = 128
SUBLANES = 8
V7X_VMEM_BYTES = 64 * 1024 * 1024
VMEM_LIMIT_BYTES = V7X_VMEM_BYTES // 8 * 7

BM = M_SAMPLE
N_ROW_TILES = M_TOTAL // BM
N_PROMPT_TILES = M_PROMPT // BM
PROJ_BN = 1024
RET_IN_BN = 2048

RET_T = 256
RET_SUB = 8

SSM_T = 16
SSM_PAIRS = SSM_GROUPS // 2
SSM_PB = 4
SSM_SEGS = 8
SSM_STEPS_P = M_PROMPT // SSM_T // SSM_SEGS
SSM_STEPS_S = DEC_SEQ // SSM_T
SSM_PAIR_CH = 2 * SSM_GROUP
SSM_PAIR_COLS = SSM_T * SSM_PAIR_CH
SSM_VEC_TOKENS = LANES // SSM_PAIR_CH


def _params(n_axes):
    return pltpu.CompilerParams(dimension_semantics=("arbitrary",) * n_axes,
                                vmem_limit_bytes=VMEM_LIMIT_BYTES)


GELU_TANH_SCALE = math.sqrt(2.0 / math.pi)
GELU_TANH_CUBIC = 0.044715


def _sigmoid(x):
    return 1.0 / (1.0 + jnp.exp(-x))


def _rope_kernel(cos_ref, sin_ref):
    m = pl.program_id(0)
    half = RET_DK // 2
    freq = lax.broadcasted_iota(jnp.int32, (1, half), 1).astype(F32)
    inv = ROPE_BASE ** (-freq / half)
    r = m * BM + lax.broadcasted_iota(jnp.int32, (BM, half), 0)
    pos = jnp.where(r < M_PROMPT, r, PAST_LEN + ((r - M_PROMPT) & (DEC_SEQ - 1)))
    ang = pos.astype(F32) * inv
    cos_ref[...] = jnp.cos(ang)
    sin_ref[...] = jnp.sin(ang)


def _rope_tables():
    half = RET_DK // 2
    spec = pl.BlockSpec((BM, half), lambda m: (m, 0))
    return pl.pallas_call(
        _rope_kernel,
        out_shape=(jax.ShapeDtypeStruct((M_TOTAL, half), F32),) * 2,
        grid=(N_ROW_TILES,),
        out_specs=(spec, spec),
        compiler_params=_params(1),
        name="rope_tables",
    )()


def _norm_kernel(x_ref, g_ref, o_ref):
    x = x_ref[...]
    ms = jnp.mean(x * x, axis=-1, keepdims=True)
    o_ref[...] = ((x * lax.rsqrt(ms + EPS)) * g_ref[...]).astype(o_ref.dtype)


def _rmsnorm(x, g, out_dtype, tile0=0, n_tiles=N_ROW_TILES):
    return pl.pallas_call(
        _norm_kernel,
        out_shape=jax.ShapeDtypeStruct((n_tiles * BM, D_MODEL), out_dtype),
        grid=(n_tiles,),
        in_specs=[pl.BlockSpec((BM, D_MODEL), lambda m: (tile0 + m, 0)),
                  pl.BlockSpec((1, D_MODEL), lambda m: (0, 0))],
        out_specs=pl.BlockSpec((BM, D_MODEL), lambda m: (m, 0)),
        compiler_params=_params(1),
        name="rmsnorm",
    )(x, g.reshape(1, D_MODEL))


CHUNKS_PER_TILE = BM // 16
N_CHUNKS = M_TOTAL // 16
N_LANE_TILES = D_MODEL // LANES


def _transpose_lane_blocks(v, block_of_lane):
    n = len(v)
    out = []
    for u in range(n):
        acc = None
        for q in range(n):
            shift = ((q - u) * SSM_PAIR_CH) % LANES
            moved = v[q] if shift == 0 else pltpu.roll(v[q], shift, axis=1)
            acc = moved if acc is None else jnp.where(block_of_lane == q, moved, acc)
        out.append(acc)
    return out


def _norm_chunked_kernel(x_ref, g_ref, o_ref, h_s):
    x = x_ref[...]
    ms = jnp.mean(x * x, axis=-1, keepdims=True)
    h = (x * lax.rsqrt(ms + EPS)) * g_ref[...]
    for j in range(N_LANE_TILES):
        h_s[j] = h[:, j * LANES:(j + 1) * LANES]
    block_of_lane = lax.broadcasted_iota(jnp.int32, (CHUNKS_PER_TILE, LANES), 1) // SSM_PAIR_CH
    for j in range(N_LANE_TILES):
        for tg in range(SSM_T // SSM_VEC_TOKENS):
            rows = [h_s[j, pl.ds(tg * SSM_VEC_TOKENS + u, CHUNKS_PER_TILE, stride=SSM_T), :]
                    for u in range(SSM_VEC_TOKENS)]
            for q, out in enumerate(_transpose_lane_blocks(rows, block_of_lane)):
                c0 = q * SSM_PAIR_COLS + tg * LANES
                o_ref[j, :, c0:c0 + LANES] = out.astype(BF16)


def _rmsnorm_chunked(x, g):
    return pl.pallas_call(
        _norm_chunked_kernel,
        out_shape=jax.ShapeDtypeStruct((N_LANE_TILES, N_CHUNKS, SSM_T * LANES), BF16),
        grid=(N_ROW_TILES,),
        in_specs=[pl.BlockSpec((BM, D_MODEL), lambda m: (m, 0)),
                  pl.BlockSpec((1, D_MODEL), lambda m: (0, 0))],
        out_specs=pl.BlockSpec((N_LANE_TILES, CHUNKS_PER_TILE, SSM_T * LANES), lambda m: (0, m, 0)),
        scratch_shapes=[pltpu.VMEM((N_LANE_TILES, BM, LANES), F32)],
        compiler_params=_params(1),
        name="rmsnorm_chunked",
    )(x, g.reshape(1, D_MODEL))


W_SPEC = pl.BlockSpec(memory_space=pl.ANY)


def _weight_scratch(k, bn):
    return [pltpu.VMEM((k, bn), BF16), pltpu.VMEM((k, bn), F32), pltpu.SemaphoreType.DMA(())]


def _stage_weights(w_hbm, staged, *, layer, col0s, n_tiles, gain_ref=None):
    n = pl.program_id(0)
    m = pl.program_id(1)

    def copies(tile):
        out = []
        for (_, wf_ref, sem), col0 in zip(staged, col0s):
            bn = wf_ref.shape[1]
            cols = pl.ds(pl.multiple_of((col0 + tile) * bn, bn), bn)
            out.append(pltpu.make_async_copy(w_hbm.at[layer, :, cols], wf_ref, sem))
        return out

    @pl.when((n == 0) & (m == 0))
    def _():
        for copy in copies(0):
            copy.start()

    @pl.when(m == 0)
    def _():
        for copy, (wb_ref, wf_ref, _) in zip(copies(n), staged):
            copy.wait()
            w = wf_ref[...]
            if gain_ref is not None:
                w = w * gain_ref[...]
            wb_ref[...] = w.astype(BF16)

    @pl.when((m == 1) & (n + 1 < n_tiles))
    def _():
        for copy in copies(n + 1):
            copy.start()


def _emit_norm_inputs(x_new, xb_ref, ssq_ref):
    xb_ref[...] = x_new.astype(BF16)
    sq = x_new * x_new
    part = sq[:, 0:LANES]
    for t in range(1, x_new.shape[1] // LANES):
        part = part + sq[:, t * LANES:(t + 1) * LANES]
    ssq_ref[...] = part


def _row_scale(ssq_ref):
    total = jnp.sum(jnp.sum(ssq_ref[...], axis=0), axis=-1, keepdims=True)
    return lax.rsqrt(total * (1.0 / D_MODEL) + EPS)


def _norm_operands(norm):
    xb, ssq, g = norm
    args = [xb, ssq, g.reshape(D_MODEL, 1)]
    specs = [pl.BlockSpec((BM, D_MODEL), lambda n, m: (m, 0)),
             pl.BlockSpec((ssq.shape[0], BM, LANES), lambda n, m: (0, m, 0)),
             pl.BlockSpec((D_MODEL, 1), lambda n, m: (0, 0))]
    return args, specs


def _norm_outputs(n_out, bn):
    shapes = [jax.ShapeDtypeStruct((M_TOTAL, n_out), BF16),
              jax.ShapeDtypeStruct((n_out // bn, M_TOTAL, LANES), F32)]
    specs = [pl.BlockSpec((BM, bn), lambda n, m: (m, n)),
             pl.BlockSpec((None, BM, LANES), lambda n, m: (n, m, 0))]
    return shapes, specs


def _cast_ssq_kernel(xp_ref, xs_ref, xb_ref, ssq_ref):
    x = jnp.where(pl.program_id(0) < N_PROMPT_TILES, xp_ref[...], xs_ref[...])
    _emit_norm_inputs(x, xb_ref, ssq_ref)


def _cast_ssq(x_prompt, x_sample):
    return pl.pallas_call(
        _cast_ssq_kernel,
        out_shape=(jax.ShapeDtypeStruct((M_TOTAL, D_MODEL), BF16),
                   jax.ShapeDtypeStruct((1, M_TOTAL, LANES), F32)),
        grid=(N_ROW_TILES,),
        in_specs=[pl.BlockSpec((BM, D_MODEL), lambda m: (jnp.minimum(m, N_PROMPT_TILES - 1), 0)),
                  pl.BlockSpec((BM, D_MODEL), lambda m: (0, 0))],
        out_specs=(pl.BlockSpec((BM, D_MODEL), lambda m: (m, 0)),
                   pl.BlockSpec((None, BM, LANES), lambda m: (0, m, 0))),
        compiler_params=_params(1),
        name="cast_ssq",
    )(x_prompt, x_sample)


def _mm_qk_kernel(a_ref, ssq_ref, g_ref, w_hbm, cos_ref, sin_ref, o_ref, wb_ref, wf_ref, sem, *, stage):
    n = pl.program_id(0)
    n_q = RET_QK // RET_IN_BN
    _stage_weights(w_hbm, [(wb_ref, wf_ref, sem)], gain_ref=g_ref, **stage)

    acc = jnp.dot(a_ref[...], wb_ref[...], preferred_element_type=F32) * _row_scale(ssq_ref)
    scale = jnp.where(n >= n_q, RET_DK ** -0.5, 1.0).astype(F32)
    c = cos_ref[...]
    s = sin_ref[...]
    half = RET_DK // 2
    for hh in range(RET_IN_BN // RET_DK):
        lo = hh * RET_DK
        x1 = acc[:, lo:lo + half]
        x2 = acc[:, lo + half:lo + RET_DK]
        o_ref[:, lo:lo + half] = ((x1 * c - x2 * s) * scale).astype(BF16)
        o_ref[:, lo + half:lo + RET_DK] = ((x1 * s + x2 * c) * scale).astype(BF16)


def _mm_cast_kernel(a_ref, ssq_ref, g_ref, w_hbm, o_ref, wb_ref, wf_ref, sem, *, stage):
    _stage_weights(w_hbm, [(wb_ref, wf_ref, sem)], gain_ref=g_ref, **stage)

    acc = jnp.dot(a_ref[...], wb_ref[...], preferred_element_type=F32) * _row_scale(ssq_ref)
    o_ref[...] = acc.astype(BF16)


def _mm_retin(norm, w, layer, cos, sin):
    bn = RET_IN_BN
    _, k, n_out = w.shape
    half = RET_DK // 2
    n_qk = 2 * RET_QK // bn
    norm_args, norm_specs = _norm_operands(norm)
    out_spec = pl.BlockSpec((BM, bn), lambda n, m: (m, n))
    rope_spec = pl.BlockSpec((BM, half), lambda n, m: (m, 0))
    n_vg = n_out // bn - n_qk
    qk = pl.pallas_call(
        functools.partial(_mm_qk_kernel, stage=dict(layer=layer, col0s=(0,), n_tiles=n_qk)),
        out_shape=jax.ShapeDtypeStruct((M_TOTAL, 2 * RET_QK), BF16),
        grid=(n_qk, N_ROW_TILES),
        in_specs=norm_specs + [W_SPEC, rope_spec, rope_spec],
        out_specs=out_spec,
        scratch_shapes=_weight_scratch(k, bn),
        compiler_params=_params(2),
        name="mm_ret_qk",
    )(*norm_args, w, cos, sin)
    vg = pl.pallas_call(
        functools.partial(_mm_cast_kernel, stage=dict(layer=layer, col0s=(n_qk,), n_tiles=n_vg)),
        out_shape=jax.ShapeDtypeStruct((M_TOTAL, 2 * RET_VD), BF16),
        grid=(n_vg, N_ROW_TILES),
        in_specs=norm_specs + [W_SPEC],
        out_specs=out_spec,
        scratch_shapes=_weight_scratch(k, bn),
        compiler_params=_params(2),
        name="mm_ret_vg",
    )(*norm_args, w)
    return qk, vg


def _row_split_specs(cols, col_index):
    return [pl.BlockSpec((BM, cols), lambda n, m: (jnp.minimum(m, N_PROMPT_TILES - 1), col_index(n))),
            pl.BlockSpec((BM, cols), lambda n, m: (0, col_index(n)))]


def _mm_res_kernel(*refs, split_a, split_x, emit, stage):
    refs = list(refs)
    ap_ref = refs.pop(0)
    as_ref = refs.pop(0) if split_a else None
    w_hbm = refs.pop(0)
    xp_ref = refs.pop(0)
    xs_ref = refs.pop(0) if split_x else None
    o_ref = refs.pop(0)
    xb_ref, ssq_ref = (refs.pop(0), refs.pop(0)) if emit else (None, None)
    wb_ref, wf_ref, sem = refs
    m = pl.program_id(1)
    on_prompt = m < N_PROMPT_TILES
    _stage_weights(w_hbm, [(wb_ref, wf_ref, sem)], **stage)

    def body(a_ref):
        x = xp_ref[...]
        if split_x:
            x = jnp.where(on_prompt, x, xs_ref[...])
        x_new = x + jnp.dot(a_ref[...], wb_ref[...], preferred_element_type=F32)
        o_ref[...] = x_new
        if emit:
            _emit_norm_inputs(x_new, xb_ref, ssq_ref)

    if split_a:
        pl.when(on_prompt)(lambda: body(ap_ref))
        pl.when(jnp.logical_not(on_prompt))(lambda: body(as_ref))
    else:
        body(ap_ref)


def _mm_res(a, w, layer, x, emit=False):
    bn = PROJ_BN
    _, k, n_out = w.shape
    split_a = isinstance(a, tuple)
    split_x = isinstance(x, tuple)
    a_specs = _row_split_specs(k, lambda n: 0) if split_a else [pl.BlockSpec((BM, k), lambda n, m: (m, 0))]
    x_specs = _row_split_specs(bn, lambda n: n) if split_x else [pl.BlockSpec((BM, bn), lambda n, m: (m, n))]
    a_args = list(a) if split_a else [a]
    x_args = list(x) if split_x else [x]
    out_shapes = [jax.ShapeDtypeStruct((M_TOTAL, n_out), F32)]
    out_specs = [pl.BlockSpec((BM, bn), lambda n, m: (m, n))]
    if emit:
        shapes, specs = _norm_outputs(n_out, bn)
        out_shapes += shapes
        out_specs += specs
    out = pl.pallas_call(
        functools.partial(_mm_res_kernel, split_a=split_a, split_x=split_x, emit=emit,
                          stage=dict(layer=layer, col0s=(0,), n_tiles=n_out // bn)),
        out_shape=tuple(out_shapes),
        grid=(n_out // bn, N_ROW_TILES),
        in_specs=a_specs + [W_SPEC] + x_specs,
        out_specs=tuple(out_specs),
        scratch_shapes=_weight_scratch(k, bn),
        compiler_params=_params(2),
        name="mm_residual",
    )(*a_args, w, *x_args)
    return out if emit else out[0]


def _mm_glu_kernel(a_ref, w_hbm, x_ref, o_ref, xb_ref, ssq_ref,
                   wab_ref, waf_ref, sem_a, wgb_ref, wgf_ref, sem_g, *, stage):
    _stage_weights(w_hbm, [(wab_ref, waf_ref, sem_a), (wgb_ref, wgf_ref, sem_g)], **stage)

    a = a_ref[...]
    ga = jnp.dot(a, wab_ref[...], preferred_element_type=F32)
    gb = jnp.dot(a, wgb_ref[...], preferred_element_type=F32)
    x_new = x_ref[...] + ga * _sigmoid(gb)
    o_ref[...] = x_new
    _emit_norm_inputs(x_new, xb_ref, ssq_ref)


def _mm_glu(a, w, layer, x):
    bn = PROJ_BN
    _, k, n2 = w.shape
    n_out = n2 // 2
    nb = n_out // bn
    norm_shapes, norm_specs = _norm_outputs(n_out, bn)
    return pl.pallas_call(
        functools.partial(_mm_glu_kernel, stage=dict(layer=layer, col0s=(0, nb), n_tiles=nb)),
        out_shape=(jax.ShapeDtypeStruct((M_TOTAL, n_out), F32), *norm_shapes),
        grid=(nb, N_ROW_TILES),
        in_specs=[
            pl.BlockSpec((BM, k), lambda n, m: (m, 0)),
            W_SPEC,
            pl.BlockSpec((BM, bn), lambda n, m: (m, n)),
        ],
        out_specs=(pl.BlockSpec((BM, bn), lambda n, m: (m, n)), *norm_specs),
        scratch_shapes=_weight_scratch(k, bn) + _weight_scratch(k, bn),
        compiler_params=_params(2),
        name="mm_glu",
    )(a, w, x)


def _mm_ffnup_kernel(h_ref, ssq_ref, g_ref, w_hbm, cw_ref, cb_ref, cache_ref,
                     o_ref, tp_ref, ts_ref,
                     wab_ref, waf_ref, sem_a, wgb_ref, wgf_ref, sem_g, carry_ref, *, stage):
    m = pl.program_id(1)
    _stage_weights(w_hbm, [(wab_ref, waf_ref, sem_a), (wgb_ref, wgf_ref, sem_g)], gain_ref=g_ref, **stage)

    @pl.when(m == 0)
    def _():
        carry_ref[...] = jnp.zeros_like(carry_ref)

    h = h_ref[...]
    r = _row_scale(ssq_ref)
    a = jnp.dot(h, wab_ref[...], preferred_element_type=F32) * r
    b = jnp.dot(h, wgb_ref[...], preferred_element_type=F32) * r
    r1 = pltpu.roll(a, 1, axis=0)
    r2 = pltpu.roll(a, 2, axis=0)
    cb = cb_ref[...]
    w0 = cw_ref[0:1, :]
    w1 = cw_ref[1:2, :]
    w2 = cw_ref[2:3, :]

    def gated(a_rows, prev1, prev2, b_rows):
        conv = cb + w0 * prev2 + w1 * prev1 + w2 * a_rows
        return ((conv * _sigmoid(conv)) * b_rows).astype(BF16)

    o_ref[...] = gated(a, r1, r2, b)

    row8 = lax.broadcasted_iota(jnp.int32, (SUBLANES, 1), 0)
    heads = [tuple(z[s * DEC_SEQ:s * DEC_SEQ + SUBLANES] for z in (a, r1, r2, b))
             for s in range(DEC_BATCH)]
    tails = [a[(s + 1) * DEC_SEQ - SUBLANES:(s + 1) * DEC_SEQ] for s in range(DEC_BATCH)]

    def redo_head(s, before1, before2):
        a8, r1_8, r2_8, b8 = heads[s]
        prev1 = jnp.where(row8 == 0, before1, r1_8)
        prev2 = jnp.where(row8 == 0, before2, jnp.where(row8 == 1, before1, r2_8))
        o_ref[s * DEC_SEQ:s * DEC_SEQ + SUBLANES, :] = gated(a8, prev1, prev2, b8)

    @pl.when(m < N_PROMPT_TILES)
    def _():
        c = carry_ref[...]
        redo_head(0, c[SUBLANES - 1:SUBLANES, :], c[SUBLANES - 2:SUBLANES - 1, :])
        carry_ref[...] = tails[DEC_BATCH - 1]

        @pl.when(m == N_PROMPT_TILES - 1)
        def _():
            tp_ref[...] = tails[DEC_BATCH - 1]

    @pl.when(m == N_PROMPT_TILES)
    def _():
        for s in range(DEC_BATCH):
            c = cache_ref[s]
            redo_head(s, c[1:2, :], c[0:1, :])
            ts_ref[s * SUBLANES:(s + 1) * SUBLANES, :] = tails[s]


def _mm_ffnup(norm, w, layer, conv_w, conv_b, cache):
    bn = PROJ_BN
    k = w.shape[1]
    nb = FFN_DIM // bn
    norm_args, norm_specs = _norm_operands(norm)
    return pl.pallas_call(
        functools.partial(_mm_ffnup_kernel, stage=dict(layer=layer, col0s=(0, nb), n_tiles=nb)),
        out_shape=(
            jax.ShapeDtypeStruct((M_TOTAL, FFN_DIM), BF16),
            jax.ShapeDtypeStruct((SUBLANES, FFN_DIM), F32),
            jax.ShapeDtypeStruct((DEC_BATCH * SUBLANES, FFN_DIM), F32),
        ),
        grid=(nb, N_ROW_TILES),
        in_specs=norm_specs + [
            W_SPEC,
            pl.BlockSpec((CONV_W, bn), lambda n, m: (0, n)),
            pl.BlockSpec((1, bn), lambda n, m: (0, n)),
            pl.BlockSpec((DEC_BATCH, CONV_W - 1, bn), lambda n, m: (0, 0, n)),
        ],
        out_specs=(
            pl.BlockSpec((BM, bn), lambda n, m: (m, n)),
            pl.BlockSpec((SUBLANES, bn), lambda n, m: (0, n)),
            pl.BlockSpec((DEC_BATCH * SUBLANES, bn), lambda n, m: (0, n)),
        ),
        scratch_shapes=_weight_scratch(k, bn) + _weight_scratch(k, bn) + [pltpu.VMEM((SUBLANES, bn), F32)],
        compiler_params=_params(2),
        name="mm_ffn_up",
    )(*norm_args, w, conv_w, conv_b.reshape(1, FFN_DIM), cache)


def _ret_tables(lg, T):
    i = lax.broadcasted_iota(jnp.int32, (T, T), 0)
    j = lax.broadcasted_iota(jnp.int32, (T, T), 1)
    dist = jnp.abs(i - j).astype(F32)
    shift = int(math.log2(CHUNK))
    visible = (j >> shift) <= (i >> shift)
    mask = jnp.where(visible, jnp.exp(dist * lg), 0.0)
    t = lax.broadcasted_iota(jnp.int32, (T, 1), 0).astype(F32)
    cross = jnp.exp((t + 1.0) * lg)
    k_dec = jnp.exp((T - 1.0 - t) * lg)
    decay = jnp.exp(float(T) * lg)
    return mask, cross, k_dec, decay


def _ret_block(q, k, v, g, state, tables, gn):
    mask, cross, k_dec, decay = tables
    scores = lax.dot_general(q, k, (((1,), (1,)), ((), ())), preferred_element_type=F32)
    scores = scores * mask
    out = jnp.dot(scores.astype(BF16), v, preferred_element_type=F32)
    out = out + jnp.dot(q, state.astype(BF16), preferred_element_type=F32) * cross

    kd = (k.astype(F32) * k_dec).astype(BF16)
    new_state = decay * state + lax.dot_general(
        kd, v, (((0,), (0,)), ((), ())), preferred_element_type=F32)

    mu = jnp.mean(out, axis=-1, keepdims=True)
    oc = out - mu
    var = jnp.mean(oc * oc, axis=-1, keepdims=True)
    normed = (oc * lax.rsqrt(var + EPS)) * gn
    g = g.astype(F32)
    return (normed * (g * _sigmoid(g))).astype(BF16), new_state


def _ret_prompt_kernel(lg_ref, q_ref, k_ref, v_ref, g_ref, gn_ref, o_ref, sl_ref, state_ref, mask_ref):
    blk = pl.program_id(1)
    lg = lg_ref[...][:, 0:1]

    @pl.when(blk == 0)
    def _():
        state_ref[...] = jnp.zeros_like(state_ref)
        mask_ref[...] = _ret_tables(lg, RET_T)[0]

    _, cross, k_dec, decay = _ret_tables(lg, RET_T)
    tables = (mask_ref[...], cross, k_dec, decay)
    gn = gn_ref[...]
    state = state_ref[...]
    for sb in range(RET_SUB):
        rows = slice(sb * RET_T, (sb + 1) * RET_T)
        o_ref[rows, :], state = _ret_block(q_ref[rows, :], k_ref[rows, :], v_ref[rows, :],
                                           g_ref[rows, :], state, tables, gn)
    state_ref[...] = state

    @pl.when(blk == pl.num_programs(1) - 1)
    def _():
        sl_ref[...] = state


def _ret_sample_kernel(lg_ref, q_ref, k_ref, v_ref, g_ref, gn_ref, s0_ref, *rest, layer):
    o_ref, sl_ref = rest[-2:]
    if len(rest) == 2:
        for other in range(N_RET):
            if other != layer:
                sl_ref[other] = jnp.zeros(sl_ref.shape[1:], F32)
        sl_ref = sl_ref.at[layer]
    lg = lg_ref[...][:, 0:1]
    tables = _ret_tables(lg, DEC_SEQ)
    gn = gn_ref[...]
    for s in range(DEC_BATCH):
        rows = slice(s * DEC_SEQ, (s + 1) * DEC_SEQ)
        o_ref[rows, :], sl_ref[s] = _ret_block(q_ref[rows, :], k_ref[rows, :], v_ref[rows, :],
                                               g_ref[rows, :], s0_ref[s], tables, gn)


def _retention(qk, vg, gn, s0_stack, layer, sample_states=None):
    log_g = np.log1p(-np.exp2(-5.0 - np.arange(RET_HEADS, dtype=np.float32))).astype(np.float32)
    lg = jnp.asarray(np.broadcast_to(log_g[:, None, None], (RET_HEADS, 1, LANES)).copy())
    gn = gn.reshape(1, RET_VD)
    k_off = RET_QK // RET_DK
    g_off = RET_VD // RET_DV
    rows_p = RET_T * RET_SUB
    o_p, s_p = pl.pallas_call(
        _ret_prompt_kernel,
        out_shape=(jax.ShapeDtypeStruct((M_PROMPT, RET_VD), BF16),
                   jax.ShapeDtypeStruct((1, RET_HEADS, RET_DK, RET_DV), F32)),
        grid=(RET_HEADS, M_PROMPT // rows_p),
        in_specs=[
            pl.BlockSpec((None, 1, LANES), lambda h, b: (h, 0, 0)),
            pl.BlockSpec((rows_p, RET_DK), lambda h, b: (b, h)),
            pl.BlockSpec((rows_p, RET_DK), lambda h, b: (b, k_off + h)),
            pl.BlockSpec((rows_p, RET_DV), lambda h, b: (b, h)),
            pl.BlockSpec((rows_p, RET_DV), lambda h, b: (b, g_off + h)),
            pl.BlockSpec((1, RET_DV), lambda h, b: (0, h)),
        ],
        out_specs=(pl.BlockSpec((rows_p, RET_DV), lambda h, b: (b, h)),
                   pl.BlockSpec((None, None, RET_DK, RET_DV), lambda h, b: (0, h, 0, 0))),
        scratch_shapes=[pltpu.VMEM((RET_DK, RET_DV), F32), pltpu.VMEM((RET_T, RET_T), F32)],
        compiler_params=_params(2),
        name="retention",
    )(lg, qk, qk, vg, vg, gn)
    sample_block = M_PROMPT // M_SAMPLE
    carried = [] if sample_states is None else [sample_states]
    n_in = 7
    if carried:
        state_spec = pl.BlockSpec((None, DEC_BATCH, None, RET_DK, RET_DV), lambda h: (layer, 0, h, 0, 0))
    else:
        state_spec = pl.BlockSpec((N_RET, DEC_BATCH, None, RET_DK, RET_DV), lambda h: (0, 0, h, 0, 0))
    o_s, s_s = pl.pallas_call(
        functools.partial(_ret_sample_kernel, layer=layer),
        out_shape=(jax.ShapeDtypeStruct((M_SAMPLE, RET_VD), BF16),
                   jax.ShapeDtypeStruct((N_RET, DEC_BATCH, RET_HEADS, RET_DK, RET_DV), F32)),
        grid=(RET_HEADS,),
        in_specs=[
            pl.BlockSpec((None, 1, LANES), lambda h: (h, 0, 0)),
            pl.BlockSpec((M_SAMPLE, RET_DK), lambda h: (sample_block, h)),
            pl.BlockSpec((M_SAMPLE, RET_DK), lambda h: (sample_block, k_off + h)),
            pl.BlockSpec((M_SAMPLE, RET_DV), lambda h: (sample_block, h)),
            pl.BlockSpec((M_SAMPLE, RET_DV), lambda h: (sample_block, g_off + h)),
            pl.BlockSpec((1, RET_DV), lambda h: (0, h)),
            pl.BlockSpec((None, DEC_BATCH, None, RET_DK, RET_DV), lambda h: (layer, 0, h, 0, 0)),
        ] + [pl.BlockSpec(memory_space=pl.ANY) for _ in carried],
        out_specs=(pl.BlockSpec((M_SAMPLE, RET_DV), lambda h: (0, h)), state_spec),
        input_output_aliases={n_in: 1} if carried else {},
        compiler_params=_params(1),
        name="retention_sample",
    )(lg, qk, qk, vg, vg, gn, s0_stack, *carried)
    return o_p, o_s, s_p, s_s


def _ssm_prep_kernel(ar_ref, ai_ref, ldt_ref, br_ref, bi_ref, cr_ref, ci_ref,
                     bc_ref, cc_ref, w_ref, a16r_ref, a16i_ref, vr_s, vi_s):
    ar = ar_ref[...]
    ai = ai_ref[...]
    dt = jnp.exp(ldt_ref[...])
    mag = jnp.exp(ar * dt)
    ang = ai * dt
    abr = mag * jnp.cos(ang)
    abi = mag * jnp.sin(ang)
    den = ar * ar + ai * ai
    nr = abr - 1.0
    ni = abi
    cfr = (nr * ar + ni * ai) / den
    cfi = (ni * ar - nr * ai) / den

    def b4(z):
        return z[:, None]

    br = br_ref[...]
    bi = bi_ref[...]
    bbr = b4(cfr) * br - b4(cfi) * bi
    bbi = b4(cfr) * bi + b4(cfi) * br
    cre = cr_ref[...]
    cim = ci_ref[...]

    powers = []
    pr = jnp.ones_like(ar)
    pi = jnp.zeros_like(ar)
    for _ in range(SSM_T + 1):
        powers.append((pr, pi))
        pr, pi = pr * abr - pi * abi, pr * abi + pi * abr
    a16r_ref[...] = powers[SSM_T][0]
    a16i_ref[...] = powers[SSM_T][1]

    for m in range(SSM_T + 1):
        pr, pi = powers[m]
        vr = b4(pr) * cre - b4(pi) * cim
        vi = b4(pr) * cim + b4(pi) * cre
        if m < SSM_T:
            vr_s[:, m] = vr
            vi_s[:, m] = vi
        if m >= 1:
            cc_ref[:, m - 1, :, :, 0:LANES] = vr.astype(BF16)
            cc_ref[:, m - 1, :, :, LANES:2 * LANES] = (-vi).astype(BF16)

    for t in range(SSM_T):
        pr, pi = powers[SSM_T - 1 - t]
        bc_ref[:, t, :, :, 0:LANES] = (b4(pr) * bbr - b4(pi) * bbi).astype(BF16)
        bc_ref[:, t, :, :, LANES:2 * LANES] = (b4(pr) * bbi + b4(pi) * bbr).astype(BF16)

    lane = lax.broadcasted_iota(jnp.int32, (SSM_PAIR_CH, SSM_PAIR_COLS), 1)
    contract_last = (((1,), (1,)), ((), ()))
    for q in range(SSM_PB):
        vr_all = vr_s[q].reshape(SSM_PAIR_COLS, LANES)
        vi_all = vi_s[q].reshape(SSM_PAIR_COLS, LANES)
        lags = (lax.dot_general(bbr[q].reshape(SSM_PAIR_CH, LANES), vr_all, contract_last,
                                precision=lax.Precision.HIGHEST, preferred_element_type=F32)
                - lax.dot_general(bbi[q].reshape(SSM_PAIR_CH, LANES), vi_all, contract_last,
                                  precision=lax.Precision.HIGHEST, preferred_element_type=F32))
        for t in range(SSM_T):
            shifted = lags if t == 0 else pltpu.roll(lags, SSM_PAIR_CH * t, axis=1)
            w_ref[q, t * SSM_PAIR_CH:(t + 1) * SSM_PAIR_CH, :] = jnp.where(
                lane >= SSM_PAIR_CH * t, shifted, 0.0).astype(BF16)


def _pair_pack(x):
    x = x.reshape(SSM_PAIRS, 2, SSM_GROUP, SSM_STATE)
    lo = jnp.pad(x[:, 0], ((0, 0), (0, 0), (0, SSM_STATE)))
    hi = jnp.pad(x[:, 1], ((0, 0), (0, 0), (SSM_STATE, 0)))
    return jnp.stack([lo, hi], axis=1)


def _ssm_prep(a_re, a_im, log_dt, b_re, b_im, c_re, c_im):
    assert 2 * SSM_PB * SSM_GROUP == LANES
    ar = a_re.reshape(SSM_PAIRS, 1, LANES)
    ai = a_im.reshape(SSM_PAIRS, 1, LANES)
    ldt = jnp.repeat(log_dt, SSM_STATE).reshape(SSM_PAIRS, 1, LANES)
    b2r = _pair_pack(jnp.transpose(b_re, (0, 2, 1)))
    b2i = _pair_pack(jnp.transpose(b_im, (0, 2, 1)))
    c2r = _pair_pack(c_re)
    c2i = _pair_pack(c_im)
    vec = pl.BlockSpec((SSM_PB, 1, LANES), lambda p: (p, 0, 0))
    mat = pl.BlockSpec((SSM_PB, 2, SSM_GROUP, LANES), lambda p: (p, 0, 0, 0))
    proj = pl.BlockSpec((SSM_PB, SSM_T, 2, SSM_GROUP, 2 * LANES), lambda p: (p, 0, 0, 0, 0))
    proj_shape = jax.ShapeDtypeStruct((SSM_PAIRS, SSM_T, 2, SSM_GROUP, 2 * LANES), BF16)
    bc, cc, w, a16r, a16i = pl.pallas_call(
        _ssm_prep_kernel,
        out_shape=(
            proj_shape, proj_shape,
            jax.ShapeDtypeStruct((SSM_PAIRS, SSM_PAIR_COLS, SSM_PAIR_COLS), BF16),
            jax.ShapeDtypeStruct((SSM_PAIRS, 1, LANES), F32),
            jax.ShapeDtypeStruct((SSM_PAIRS, 1, LANES), F32),
        ),
        grid=(N_LANE_TILES,),
        in_specs=[vec, vec, vec, mat, mat, mat, mat],
        out_specs=(proj, proj,
                   pl.BlockSpec((SSM_PB, SSM_PAIR_COLS, SSM_PAIR_COLS), lambda p: (p, 0, 0)),
                   vec, vec),
        scratch_shapes=[pltpu.VMEM((SSM_PB, SSM_T, 2, SSM_GROUP, LANES), F32)] * 2,
        compiler_params=_params(1),
        name="ssm_prep",
    )(ar, ai, ldt, b2r, b2i, c2r, c2i)
    bc = bc.reshape(SSM_PAIRS, SSM_PAIR_COLS, 2 * LANES)
    cc = cc.reshape(SSM_PAIRS, SSM_PAIR_COLS, 2 * LANES)
    row = SSM_GROUPS * SSM_STATE
    a16r = jnp.broadcast_to(a16r.reshape(1, row), (SSM_SEGS, row))
    a16i = jnp.broadcast_to(a16i.reshape(1, row), (SSM_SEGS, row))
    return bc, cc, w, a16r, a16i


SSM_CHUNK_COLS = SSM_T * LANES
SSM_STATE_COLS = SSM_PB * LANES
SSM_CHUNKS_P = M_PROMPT // SSM_T
SSM_SCAN_UNROLL = 8


def _ssm_main_kernel(x_ref, w_ref, bc_ref, cc_ref, ar_ref, ai_ref, h0r_ref, h0i_ref,
                     y_ref, fpr_ref, fpi_ref, fsr_ref, fsi_ref, d_s, s_s, t_s):
    pair_cols = [slice(q * SSM_PAIR_COLS, (q + 1) * SSM_PAIR_COLS) for q in range(SSM_PB)]

    for q in range(SSM_PB):
        d = jnp.dot(x_ref[:, pair_cols[q]], bc_ref[q], preferred_element_type=F32)
        d_s[q] = d[:, 0:LANES]
        d_s[SSM_PB + q] = d[:, LANES:2 * LANES]

    tiles = range(SSM_PB)
    ar = [ar_ref[:, k * LANES:(k + 1) * LANES] for k in tiles]
    ai = [ai_ref[:, k * LANES:(k + 1) * LANES] for k in tiles]

    def advance(sr, si, rows, record=None):
        out_r, out_i = [], []
        for k in tiles:
            if record is not None:
                ref, rec_rows = record
                ref[k, rec_rows, :] = sr[k]
                ref[SSM_PB + k, rec_rows, :] = si[k]
            dr = d_s[k, rows, :]
            di = d_s[SSM_PB + k, rows, :]
            out_r.append(ar[k] * sr[k] - ai[k] * si[k] + dr)
            out_i.append(ar[k] * si[k] + ai[k] * sr[k] + di)
        return out_r, out_i

    def prompt_rows(i):
        return pl.ds(i, SSM_SEGS, stride=SSM_STEPS_P)

    def prompt_pass(start_r, start_i, record):
        def body(i, carry):
            rec = (t_s, pl.ds(pl.multiple_of(i * SSM_SEGS, SSM_SEGS), SSM_SEGS)) if record else None
            out_r, out_i = advance(list(carry[0]), list(carry[1]), prompt_rows(i), rec)
            return tuple(out_r), tuple(out_i)
        out_r, out_i = lax.fori_loop(0, SSM_STEPS_P, body, (tuple(start_r), tuple(start_i)),
                                     unroll=SSM_SCAN_UNROLL)
        return list(out_r), list(out_i)

    zeros = [jnp.zeros((SSM_SEGS, LANES), F32) for _ in tiles]
    er, ei = prompt_pass(zeros, zeros, record=False)

    sr, si = [], []
    for k in tiles:
        pr, pi = ar[k][0:1], ai[k][0:1]
        for _ in range(int(math.log2(SSM_STEPS_P))):
            pr, pi = pr * pr - pi * pi, 2.0 * pr * pi
        cr = jnp.zeros((1, LANES), F32)
        ci = cr
        starts_r, starts_i = [], []
        for s in range(SSM_SEGS):
            starts_r.append(cr)
            starts_i.append(ci)
            cr, ci = (pr * cr - pi * ci + er[k][s:s + 1], pr * ci + pi * cr + ei[k][s:s + 1])
        cols = slice(k * LANES, (k + 1) * LANES)
        fpr_ref[:, cols] = jnp.broadcast_to(cr, (SSM_SEGS, LANES))
        fpi_ref[:, cols] = jnp.broadcast_to(ci, (SSM_SEGS, LANES))
        sr.append(jnp.concatenate(starts_r, axis=0))
        si.append(jnp.concatenate(starts_i, axis=0))

    prompt_pass(sr, si, record=True)
    for k in range(2 * SSM_PB):
        for s in range(SSM_SEGS):
            s_s[k, s * SSM_STEPS_P:(s + 1) * SSM_STEPS_P, :] = t_s[k, pl.ds(s, SSM_STEPS_P, stride=SSM_SEGS), :]
    sr = [h0r_ref[:, k * LANES:(k + 1) * LANES] for k in tiles]
    si = [h0i_ref[:, k * LANES:(k + 1) * LANES] for k in tiles]
    for i in range(SSM_STEPS_S):
        rows = pl.ds(SSM_CHUNKS_P + i, DEC_BATCH, stride=SSM_STEPS_S)
        sr, si = advance(sr, si, rows, (s_s, rows))
    for k in tiles:
        fsr_ref[:, k * LANES:(k + 1) * LANES] = sr[k]
        fsi_ref[:, k * LANES:(k + 1) * LANES] = si[k]

    contract_last = (((1,), (1,)), ((), ()))
    for q in range(SSM_PB):
        states = jnp.concatenate([s_s[q].astype(BF16), s_s[SSM_PB + q].astype(BF16)], axis=1)
        carried = lax.dot_general(states, cc_ref[q], contract_last, preferred_element_type=F32)
        local = jnp.dot(x_ref[:, pair_cols[q]], w_ref[q], preferred_element_type=F32)
        y_ref[:, pair_cols[q]] = local + carried


def _ssm_main(x_c, w, bc, cc, a16r, a16i, h0r, h0i):
    row = SSM_GROUPS * SSM_STATE
    st = pl.BlockSpec((SSM_SEGS, SSM_STATE_COLS), lambda j: (0, j))
    st_shape = jax.ShapeDtypeStruct((SSM_SEGS, row), F32)
    proj = pl.BlockSpec((SSM_PB, SSM_PAIR_COLS, 2 * LANES), lambda j: (j, 0, 0))
    chunk_rows = pl.BlockSpec((None, N_CHUNKS, SSM_CHUNK_COLS), lambda j: (j, 0, 0))
    return pl.pallas_call(
        _ssm_main_kernel,
        out_shape=(jax.ShapeDtypeStruct((N_LANE_TILES, N_CHUNKS, SSM_CHUNK_COLS), F32),
                   st_shape, st_shape, st_shape, st_shape),
        grid=(N_LANE_TILES,),
        in_specs=[
            chunk_rows,
            pl.BlockSpec((SSM_PB, SSM_PAIR_COLS, SSM_PAIR_COLS), lambda j: (j, 0, 0)),
            proj, proj, st, st, st, st,
        ],
        out_specs=(chunk_rows, st, st, st, st),
        scratch_shapes=[
            pltpu.VMEM((2 * SSM_PB, N_CHUNKS, LANES), F32),
            pltpu.VMEM((2 * SSM_PB, N_CHUNKS, LANES), F32),
            pltpu.VMEM((2 * SSM_PB, SSM_CHUNKS_P, LANES), F32),
        ],
        compiler_params=_params(1),
        name="ssm_main",
    )(x_c, w, bc, cc, a16r, a16i, h0r, h0i)


def _ssm_post_kernel(y_ref, x_ref, g_ref, d_ref, o_ref, y_s):
    block_of_lane = lax.broadcasted_iota(jnp.int32, (CHUNKS_PER_TILE, LANES), 1) // SSM_PAIR_CH
    for j in range(N_LANE_TILES):
        for tg in range(SSM_T // SSM_VEC_TOKENS):
            pieces = [y_ref[j, :, q * SSM_PAIR_COLS + tg * LANES:q * SSM_PAIR_COLS + (tg + 1) * LANES]
                      for q in range(SSM_PB)]
            for u, out in enumerate(_transpose_lane_blocks(pieces, block_of_lane)):
                y_s[j, pl.ds(tg * SSM_VEC_TOKENS + u, CHUNKS_PER_TILE, stride=SSM_T), :] = out
    x = x_ref[...]
    ms = jnp.mean(x * x, axis=-1, keepdims=True)
    h = (x * lax.rsqrt(ms + EPS)) * g_ref[...]
    for j in range(N_LANE_TILES):
        cols = slice(j * LANES, (j + 1) * LANES)
        y = y_s[j] + d_ref[:, cols] * h[:, cols]
        gelu = 0.5 * y * (1.0 + jnp.tanh(GELU_TANH_SCALE * (y + GELU_TANH_CUBIC * (y * y * y))))
        o_ref[:, cols] = gelu.astype(BF16)


def _ssm_post(y_c, x, g, d):
    spec = pl.BlockSpec((BM, D_MODEL), lambda m: (m, 0))
    vec = pl.BlockSpec((1, D_MODEL), lambda m: (0, 0))
    return pl.pallas_call(
        _ssm_post_kernel,
        out_shape=jax.ShapeDtypeStruct((M_TOTAL, D_MODEL), BF16),
        grid=(N_ROW_TILES,),
        in_specs=[pl.BlockSpec((N_LANE_TILES, CHUNKS_PER_TILE, SSM_CHUNK_COLS), lambda m: (0, m, 0)),
                  spec, vec, vec],
        out_specs=spec,
        scratch_shapes=[pltpu.VMEM((N_LANE_TILES, BM, LANES), F32)],
        compiler_params=_params(1),
        name="ssm_post",
    )(y_c, x, g.reshape(1, D_MODEL), d.reshape(1, D_MODEL))


def kernel(x_prompt, x_sample, state_ret, state_ssm_re, state_ssm_im, cache_conv, norm_mix, norm_ffn, norm_final, ret_w_in, ret_gn, ret_w_out, ssm_a_re, ssm_a_im, ssm_log_dt, ssm_b_re, ssm_b_im, ssm_c_re, ssm_c_im, ssm_d, ssm_w_glu, ffn_w_up, ffn_conv_w, ffn_conv_b, ffn_w_down):
    x = (x_prompt.reshape(M_PROMPT, D_MODEL), x_sample.reshape(M_SAMPLE, D_MODEL))
    xb, ssq = _cast_ssq(*x)
    cos, sin = _rope_tables()

    ret_p, ret_s = [], None
    re_p, im_p, re_s, im_s = [], [], [], []
    conv_p, conv_s = [], []
    state_row = SSM_GROUPS * SSM_STATE
    for i in range(DEPTH):
        j = i // 2
        if i % 2 == 0:
            qk, vg = _mm_retin((xb, ssq, norm_mix[i]), ret_w_in, j, cos, sin)
            o_p, o_s, s_p, ret_s = _retention(qk, vg, ret_gn[j], state_ret, j, ret_s)
            ret_p.append(s_p)
            x, xb, ssq = _mm_res((o_p, o_s), ret_w_out, j, x, emit=True)
        else:
            h_c = _rmsnorm_chunked(x, norm_mix[i])
            bc, cc, w_ssm, a16r, a16i = _ssm_prep(ssm_a_re[j], ssm_a_im[j], ssm_log_dt[j],
                                               ssm_b_re[j], ssm_b_im[j], ssm_c_re[j], ssm_c_im[j])
            y_c, fpr, fpi, fsr, fsi = _ssm_main(
                h_c, w_ssm, bc, cc, a16r, a16i,
                state_ssm_re[j].reshape(DEC_BATCH, state_row),
                state_ssm_im[j].reshape(DEC_BATCH, state_row))
            re_p.append(fpr[0:1].reshape(1, SSM_GROUPS, SSM_STATE))
            im_p.append(fpi[0:1].reshape(1, SSM_GROUPS, SSM_STATE))
            re_s.append(fsr.reshape(DEC_BATCH, SSM_GROUPS, SSM_STATE))
            im_s.append(fsi.reshape(DEC_BATCH, SSM_GROUPS, SSM_STATE))
            gl = _ssm_post(y_c, x, norm_mix[i], ssm_d[j])
            x, xb, ssq = _mm_glu(gl, ssm_w_glu, j, x)

        u, tail_p, tail_s = _mm_ffnup((xb, ssq, norm_ffn[i]), ffn_w_up, i, ffn_conv_w[i],
                                      ffn_conv_b[i], cache_conv[i])
        conv_p.append(tail_p[SUBLANES - (CONV_W - 1):].reshape(1, CONV_W - 1, FFN_DIM))
        conv_s.append(tail_s.reshape(DEC_BATCH, SUBLANES, FFN_DIM)[:, SUBLANES - (CONV_W - 1):])
        if i + 1 < DEPTH and (i + 1) % 2 == 0:
            x, xb, ssq = _mm_res(u, ffn_w_down, i, x, emit=True)
        else:
            x = _mm_res(u, ffn_w_down, i, x)

    y_prompt = _rmsnorm(x, norm_final, F32, 0, N_PROMPT_TILES).reshape(1, SEQ, D_MODEL)
    y_sample = _rmsnorm(x, norm_final, F32, N_PROMPT_TILES, 1).reshape(DEC_BATCH, DEC_SEQ, D_MODEL)
    return (y_prompt, y_sample, jnp.stack(ret_p), ret_s,
            jnp.stack(re_p), jnp.stack(im_p), jnp.stack(re_s), jnp.stack(im_s),
            jnp.stack(conv_p), jnp.stack(conv_s))
```

```python
import functools
import math

import numpy as np
import jax
import jax.numpy as jnp
from jax import lax
from jax.experimental import pallas as pl
from jax.experimental.pallas import tpu as pltpu

F32 = jnp.float32
BF16 = jnp.bfloat16

D_MODEL = 2048
SEQ = 8192
DEPTH = 4
DEC_BATCH = 8
DEC_SEQ = 64
PAST_LEN = 1024
CHUNK = 64
RET_HEADS = 8
RET_DK = D_MODEL // RET_HEADS
RET_DV = 2 * RET_DK
RET_QK = RET_HEADS * RET_DK
RET_VD = RET_HEADS * RET_DV
N_RET = (DEPTH + 1) // 2
ROPE_BASE = 10000.0
SSM_GROUP = 16
SSM_GROUPS = D_MODEL // SSM_GROUP
SSM_STATE = 64
FFN_DIM = 2 * D_MODEL
CONV_W = 3
EPS = 1e-6

M_PROMPT = SEQ
M_SAMPLE = DEC_BATCH * DEC_SEQ
M_TOTAL = M_PROMPT + M_SAMPLE

LANES = 128
SUBLANES = 8
V7X_VMEM_BYTES = 64 * 1024 * 1024
VMEM_LIMIT_BYTES = V7X_VMEM_BYTES // 8 * 7

BM = M_SAMPLE
N_ROW_TILES = M_TOTAL // BM
N_PROMPT_TILES = M_PROMPT // BM
PROJ_BN = 1024
RET_IN_BN = 2048

RET_T = 256
RET_SUB = 8

SSM_T = 16
SSM_PAIRS = SSM_GROUPS // 2
SSM_PB = 4
SSM_SEGS = 8
SSM_STEPS_P = M_PROMPT // SSM_T // SSM_SEGS
SSM_STEPS_S = DEC_SEQ // SSM_T
SSM_PAIR_CH = 2 * SSM_GROUP
SSM_PAIR_COLS = SSM_T * SSM_PAIR_CH
SSM_VEC_TOKENS = LANES // SSM_PAIR_CH


def _params(n_axes):
    return pltpu.CompilerParams(dimension_semantics=("arbitrary",) * n_axes,
                                vmem_limit_bytes=VMEM_LIMIT_BYTES)


GELU_TANH_SCALE = math.sqrt(2.0 / math.pi)
GELU_TANH_CUBIC = 0.044715


def _sigmoid(x):
    return 1.0 / (1.0 + jnp.exp(-x))


def _rope_kernel(cos_ref, sin_ref):
    m = pl.program_id(0)
    half = RET_DK // 2
    freq = lax.broadcasted_iota(jnp.int32, (1, half), 1).astype(F32)
    inv = ROPE_BASE ** (-freq / half)
    r = m * BM + lax.broadcasted_iota(jnp.int32, (BM, half), 0)
    pos = jnp.where(r < M_PROMPT, r, PAST_LEN + ((r - M_PROMPT) & (DEC_SEQ - 1)))
    ang = pos.astype(F32) * inv
    cos_ref[...] = jnp.cos(ang)
    sin_ref[...] = jnp.sin(ang)


def _rope_tables():
    half = RET_DK // 2
    spec = pl.BlockSpec((BM, half), lambda m: (m, 0))
    return pl.pallas_call(
        _rope_kernel,
        out_shape=(jax.ShapeDtypeStruct((M_TOTAL, half), F32),) * 2,
        grid=(N_ROW_TILES,),
        out_specs=(spec, spec),
        compiler_params=_params(1),
        name="rope_tables",
    )()


def _norm_kernel(x_ref, g_ref, o_ref):
    x = x_ref[...]
    ms = jnp.mean(x * x, axis=-1, keepdims=True)
    o_ref[...] = ((x * lax.rsqrt(ms + EPS)) * g_ref[...]).astype(o_ref.dtype)


def _rmsnorm(x, g, out_dtype, tile0=0, n_tiles=N_ROW_TILES):
    return pl.pallas_call(
        _norm_kernel,
        out_shape=jax.ShapeDtypeStruct((n_tiles * BM, D_MODEL), out_dtype),
        grid=(n_tiles,),
        in_specs=[pl.BlockSpec((BM, D_MODEL), lambda m: (tile0 + m, 0)),
                  pl.BlockSpec((1, D_MODEL), lambda m: (0, 0))],
        out_specs=pl.BlockSpec((BM, D_MODEL), lambda m: (m, 0)),
        compiler_params=_params(1),
        name="rmsnorm",
    )(x, g.reshape(1, D_MODEL))


CHUNKS_PER_TILE = BM // 16
N_CHUNKS = M_TOTAL // 16
N_LANE_TILES = D_MODEL // LANES


def _transpose_lane_blocks(v, block_of_lane):
    n = len(v)
    out = []
    for u in range(n):
        acc = None
        for q in range(n):
            shift = ((q - u) * SSM_PAIR_CH) % LANES
            moved = v[q] if shift == 0 else pltpu.roll(v[q], shift, axis=1)
            acc = moved if acc is None else jnp.where(block_of_lane == q, moved, acc)
        out.append(acc)
    return out


def _norm_chunked_kernel(x_ref, g_ref, o_ref, h_s):
    x = x_ref[...]
    ms = jnp.mean(x * x, axis=-1, keepdims=True)
    h = (x * lax.rsqrt(ms + EPS)) * g_ref[...]
    for j in range(N_LANE_TILES):
        h_s[j] = h[:, j * LANES:(j + 1) * LANES]
    block_of_lane = lax.broadcasted_iota(jnp.int32, (CHUNKS_PER_TILE, LANES), 1) // SSM_PAIR_CH
    for j in range(N_LANE_TILES):
        for tg in range(SSM_T // SSM_VEC_TOKENS):
            rows = [h_s[j, pl.ds(tg * SSM_VEC_TOKENS + u, CHUNKS_PER_TILE, stride=SSM_T), :]
                    for u in range(SSM_VEC_TOKENS)]
            for q, out in enumerate(_transpose_lane_blocks(rows, block_of_lane)):
                c0 = q * SSM_PAIR_COLS + tg * LANES
                o_ref[j, :, c0:c0 + LANES] = out.astype(BF16)


def _rmsnorm_chunked(x, g):
    return pl.pallas_call(
        _norm_chunked_kernel,
        out_shape=jax.ShapeDtypeStruct((N_LANE_TILES, N_CHUNKS, SSM_T * LANES), BF16),
        grid=(N_ROW_TILES,),
        in_specs=[pl.BlockSpec((BM, D_MODEL), lambda m: (m, 0)),
                  pl.BlockSpec((1, D_MODEL), lambda m: (0, 0))],
        out_specs=pl.BlockSpec((N_LANE_TILES, CHUNKS_PER_TILE, SSM_T * LANES), lambda m: (0, m, 0)),
        scratch_shapes=[pltpu.VMEM((N_LANE_TILES, BM, LANES), F32)],
        compiler_params=_params(1),
        name="rmsnorm_chunked",
    )(x, g.reshape(1, D_MODEL))


W_SPEC = pl.BlockSpec(memory_space=pl.ANY)


def _weight_scratch(k, bn):
    return [pltpu.VMEM((k, bn), BF16), pltpu.VMEM((k, bn), F32), pltpu.SemaphoreType.DMA(())]


def _stage_weights(w_hbm, staged, *, layer, col0s, n_tiles, gain_ref=None):
    n = pl.program_id(0)
    m = pl.program_id(1)

    def copies(tile):
        out = []
        for (_, wf_ref, sem), col0 in zip(staged, col0s):
            bn = wf_ref.shape[1]
            cols = pl.ds(pl.multiple_of((col0 + tile) * bn, bn), bn)
            out.append(pltpu.make_async_copy(w_hbm.at[layer, :, cols], wf_ref, sem))
        return out

    @pl.when((n == 0) & (m == 0))
    def _():
        for copy in copies(0):
            copy.start()

    @pl.when(m == 0)
    def _():
        for copy, (wb_ref, wf_ref, _) in zip(copies(n), staged):
            copy.wait()
            w = wf_ref[...]
            if gain_ref is not None:
                w = w * gain_ref[...]
            wb_ref[...] = w.astype(BF16)

    @pl.when((m == 1) & (n + 1 < n_tiles))
    def _():
        for copy in copies(n + 1):
            copy.start(priority=1)


def _emit_norm_inputs(x_new, xb_ref, ssq_ref):
    xb_ref[...] = x_new.astype(BF16)
    sq = x_new * x_new
    part = sq[:, 0:LANES]
    for t in range(1, x_new.shape[1] // LANES):
        part = part + sq[:, t * LANES:(t + 1) * LANES]
    ssq_ref[...] = part


def _row_scale(ssq_ref):
    total = jnp.sum(jnp.sum(ssq_ref[...], axis=0), axis=-1, keepdims=True)
    return lax.rsqrt(total * (1.0 / D_MODEL) + EPS)


def _norm_operands(norm):
    xb, ssq, g = norm
    args = [xb, ssq, g.reshape(D_MODEL, 1)]
    specs = [pl.BlockSpec((BM, D_MODEL), lambda n, m: (m, 0)),
             pl.BlockSpec((ssq.shape[0], BM, LANES), lambda n, m: (0, m, 0)),
             pl.BlockSpec((D_MODEL, 1), lambda n, m: (0, 0))]
    return args, specs


def _norm_outputs(n_out, bn):
    shapes = [jax.ShapeDtypeStruct((M_TOTAL, n_out), BF16),
              jax.ShapeDtypeStruct((n_out // bn, M_TOTAL, LANES), F32)]
    specs = [pl.BlockSpec((BM, bn), lambda n, m: (m, n)),
             pl.BlockSpec((None, BM, LANES), lambda n, m: (n, m, 0))]
    return shapes, specs


def _cast_ssq_kernel(xp_ref, xs_ref, xb_ref, ssq_ref):
    x = jnp.where(pl.program_id(0) < N_PROMPT_TILES, xp_ref[...], xs_ref[...])
    _emit_norm_inputs(x, xb_ref, ssq_ref)


def _cast_ssq(x_prompt, x_sample):
    return pl.pallas_call(
        _cast_ssq_kernel,
        out_shape=(jax.ShapeDtypeStruct((M_TOTAL, D_MODEL), BF16),
                   jax.ShapeDtypeStruct((1, M_TOTAL, LANES), F32)),
        grid=(N_ROW_TILES,),
        in_specs=[pl.BlockSpec((BM, D_MODEL), lambda m: (jnp.minimum(m, N_PROMPT_TILES - 1), 0)),
                  pl.BlockSpec((BM, D_MODEL), lambda m: (0, 0))],
        out_specs=(pl.BlockSpec((BM, D_MODEL), lambda m: (m, 0)),
                   pl.BlockSpec((None, BM, LANES), lambda m: (0, m, 0))),
        compiler_params=_params(1),
        name="cast_ssq",
    )(x_prompt, x_sample)


def _mm_qk_kernel(a_ref, ssq_ref, g_ref, w_hbm, cos_ref, sin_ref, o_ref, wb_ref, wf_ref, sem, *, stage):
    n = pl.program_id(0)
    n_q = RET_QK // RET_IN_BN
    _stage_weights(w_hbm, [(wb_ref, wf_ref, sem)], gain_ref=g_ref, **stage)

    acc = jnp.dot(a_ref[...], wb_ref[...], preferred_element_type=F32) * _row_scale(ssq_ref)
    scale = jnp.where(n >= n_q, RET_DK ** -0.5, 1.0).astype(F32)
    c = cos_ref[...]
    s = sin_ref[...]
    half = RET_DK // 2
    for hh in range(RET_IN_BN // RET_DK):
        lo = hh * RET_DK
        x1 = acc[:, lo:lo + half]
        x2 = acc[:, lo + half:lo + RET_DK]
        o_ref[:, lo:lo + half] = ((x1 * c - x2 * s) * scale).astype(BF16)
        o_ref[:, lo + half:lo + RET_DK] = ((x1 * s + x2 * c) * scale).astype(BF16)


def _mm_cast_kernel(a_ref, ssq_ref, g_ref, w_hbm, o_ref, wb_ref, wf_ref, sem, *, stage):
    _stage_weights(w_hbm, [(wb_ref, wf_ref, sem)], gain_ref=g_ref, **stage)

    acc = jnp.dot(a_ref[...], wb_ref[...], preferred_element_type=F32) * _row_scale(ssq_ref)
    o_ref[...] = acc.astype(BF16)


def _mm_retin(norm, w, layer, cos, sin):
    bn = RET_IN_BN
    _, k, n_out = w.shape
    half = RET_DK // 2
    n_qk = 2 * RET_QK // bn
    norm_args, norm_specs = _norm_operands(norm)
    out_spec = pl.BlockSpec((BM, bn), lambda n, m: (m, n))
    rope_spec = pl.BlockSpec((BM, half), lambda n, m: (m, 0))
    n_vg = n_out // bn - n_qk
    qk = pl.pallas_call(
        functools.partial(_mm_qk_kernel, stage=dict(layer=layer, col0s=(0,), n_tiles=n_qk)),
        out_shape=jax.ShapeDtypeStruct((M_TOTAL, 2 * RET_QK), BF16),
        grid=(n_qk, N_ROW_TILES),
        in_specs=norm_specs + [W_SPEC, rope_spec, rope_spec],
        out_specs=out_spec,
        scratch_shapes=_weight_scratch(k, bn),
        compiler_params=_params(2),
        name="mm_ret_qk",
    )(*norm_args, w, cos, sin)
    vg = pl.pallas_call(
        functools.partial(_mm_cast_kernel, stage=dict(layer=layer, col0s=(n_qk,), n_tiles=n_vg)),
        out_shape=jax.ShapeDtypeStruct((M_TOTAL, 2 * RET_VD), BF16),
        grid=(n_vg, N_ROW_TILES),
        in_specs=norm_specs + [W_SPEC],
        out_specs=out_spec,
        scratch_shapes=_weight_scratch(k, bn),
        compiler_params=_params(2),
        name="mm_ret_vg",
    )(*norm_args, w)
    return qk, vg


def _row_split_specs(cols, col_index):
    return [pl.BlockSpec((BM, cols), lambda n, m: (jnp.minimum(m, N_PROMPT_TILES - 1), col_index(n))),
            pl.BlockSpec((BM, cols), lambda n, m: (0, col_index(n)))]


def _mm_res_kernel(*refs, split_a, split_x, emit, stage):
    refs = list(refs)
    ap_ref = refs.pop(0)
    as_ref = refs.pop(0) if split_a else None
    w_hbm = refs.pop(0)
    xp_ref = refs.pop(0)
    xs_ref = refs.pop(0) if split_x else None
    o_ref = refs.pop(0)
    xb_ref, ssq_ref = (refs.pop(0), refs.pop(0)) if emit else (None, None)
    wb_ref, wf_ref, sem = refs
    m = pl.program_id(1)
    on_prompt = m < N_PROMPT_TILES
    _stage_weights(w_hbm, [(wb_ref, wf_ref, sem)], **stage)

    def body(a_ref):
        x = xp_ref[...]
        if split_x:
            x = jnp.where(on_prompt, x, xs_ref[...])
        x_new = x + jnp.dot(a_ref[...], wb_ref[...], preferred_element_type=F32)
        o_ref[...] = x_new
        if emit:
            _emit_norm_inputs(x_new, xb_ref, ssq_ref)

    if split_a:
        pl.when(on_prompt)(lambda: body(ap_ref))
        pl.when(jnp.logical_not(on_prompt))(lambda: body(as_ref))
    else:
        body(ap_ref)


def _mm_res(a, w, layer, x, emit=False):
    bn = PROJ_BN
    _, k, n_out = w.shape
    split_a = isinstance(a, tuple)
    split_x = isinstance(x, tuple)
    a_specs = _row_split_specs(k, lambda n: 0) if split_a else [pl.BlockSpec((BM, k), lambda n, m: (m, 0))]
    x_specs = _row_split_specs(bn, lambda n: n) if split_x else [pl.BlockSpec((BM, bn), lambda n, m: (m, n))]
    a_args = list(a) if split_a else [a]
    x_args = list(x) if split_x else [x]
    out_shapes = [jax.ShapeDtypeStruct((M_TOTAL, n_out), F32)]
    out_specs = [pl.BlockSpec((BM, bn), lambda n, m: (m, n))]
    if emit:
        shapes, specs = _norm_outputs(n_out, bn)
        out_shapes += shapes
        out_specs += specs
    out = pl.pallas_call(
        functools.partial(_mm_res_kernel, split_a=split_a, split_x=split_x, emit=emit,
                          stage=dict(layer=layer, col0s=(0,), n_tiles=n_out // bn)),
        out_shape=tuple(out_shapes),
        grid=(n_out // bn, N_ROW_TILES),
        in_specs=a_specs + [W_SPEC] + x_specs,
        out_specs=tuple(out_specs),
        scratch_shapes=_weight_scratch(k, bn),
        compiler_params=_params(2),
        name="mm_residual",
    )(*a_args, w, *x_args)
    return out if emit else out[0]


def _mm_glu_kernel(a_ref, w_hbm, x_ref, o_ref, xb_ref, ssq_ref,
                   wab_ref, waf_ref, sem_a, wgb_ref, wgf_ref, sem_g, *, stage):
    _stage_weights(w_hbm, [(wab_ref, waf_ref, sem_a), (wgb_ref, wgf_ref, sem_g)], **stage)

    a = a_ref[...]
    ga = jnp.dot(a, wab_ref[...], preferred_element_type=F32)
    gb = jnp.dot(a, wgb_ref[...], preferred_element_type=F32)
    x_new = x_ref[...] + ga * _sigmoid(gb)
    o_ref[...] = x_new
    _emit_norm_inputs(x_new, xb_ref, ssq_ref)


def _mm_glu(a, w, layer, x):
    bn = PROJ_BN
    _, k, n2 = w.shape
    n_out = n2 // 2
    nb = n_out // bn
    norm_shapes, norm_specs = _norm_outputs(n_out, bn)
    return pl.pallas_call(
        functools.partial(_mm_glu_kernel, stage=dict(layer=layer, col0s=(0, nb), n_tiles=nb)),
        out_shape=(jax.ShapeDtypeStruct((M_TOTAL, n_out), F32), *norm_shapes),
        grid=(nb, N_ROW_TILES),
        in_specs=[
            pl.BlockSpec((BM, k), lambda n, m: (m, 0)),
            W_SPEC,
            pl.BlockSpec((BM, bn), lambda n, m: (m, n)),
        ],
        out_specs=(pl.BlockSpec((BM, bn), lambda n, m: (m, n)), *norm_specs),
        scratch_shapes=_weight_scratch(k, bn) + _weight_scratch(k, bn),
        compiler_params=_params(2),
        name="mm_glu",
    )(a, w, x)


def _mm_ffnup_kernel(h_ref, ssq_ref, g_ref, w_hbm, cw_ref, cb_ref, cache_ref,
                     o_ref, tp_ref, ts_ref,
                     wab_ref, waf_ref, sem_a, wgb_ref, wgf_ref, sem_g, carry_ref, *, stage):
    m = pl.program_id(1)
    _stage_weights(w_hbm, [(wab_ref, waf_ref, sem_a), (wgb_ref, wgf_ref, sem_g)], gain_ref=g_ref, **stage)

    @pl.when(m == 0)
    def _():
        carry_ref[...] = jnp.zeros_like(carry_ref)

    h = h_ref[...]
    r = _row_scale(ssq_ref)
    a = jnp.dot(h, wab_ref[...], preferred_element_type=F32) * r
    b = jnp.dot(h, wgb_ref[...], preferred_element_type=F32) * r
    r1 = pltpu.roll(a, 1, axis=0)
    r2 = pltpu.roll(a, 2, axis=0)
    cb = cb_ref[...]
    w0 = cw_ref[0:1, :]
    w1 = cw_ref[1:2, :]
    w2 = cw_ref[2:3, :]

    def gated(a_rows, prev1, prev2, b_rows):
        conv = cb + w0 * prev2 + w1 * prev1 + w2 * a_rows
        return ((conv * _sigmoid(conv)) * b_rows).astype(BF16)

    o_ref[...] = gated(a, r1, r2, b)

    row8 = lax.broadcasted_iota(jnp.int32, (SUBLANES, 1), 0)
    heads = [tuple(z[s * DEC_SEQ:s * DEC_SEQ + SUBLANES] for z in (a, r1, r2, b))
             for s in range(DEC_BATCH)]
    tails = [a[(s + 1) * DEC_SEQ - SUBLANES:(s + 1) * DEC_SEQ] for s in range(DEC_BATCH)]

    def redo_head(s, before1, before2):
        a8, r1_8, r2_8, b8 = heads[s]
        prev1 = jnp.where(row8 == 0, before1, r1_8)
        prev2 = jnp.where(row8 == 0, before2, jnp.where(row8 == 1, before1, r2_8))
        o_ref[s * DEC_SEQ:s * DEC_SEQ + SUBLANES, :] = gated(a8, prev1, prev2, b8)

    @pl.when(m < N_PROMPT_TILES)
    def _():
        c = carry_ref[...]
        redo_head(0, c[SUBLANES - 1:SUBLANES, :], c[SUBLANES - 2:SUBLANES - 1, :])
        carry_ref[...] = tails[DEC_BATCH - 1]

        @pl.when(m == N_PROMPT_TILES - 1)
        def _():
            tp_ref[...] = tails[DEC_BATCH - 1]

    @pl.when(m == N_PROMPT_TILES)
    def _():
        for s in range(DEC_BATCH):
            c = cache_ref[s]
            redo_head(s, c[1:2, :], c[0:1, :])
            ts_ref[s * SUBLANES:(s + 1) * SUBLANES, :] = tails[s]


def _mm_ffnup(norm, w, layer, conv_w, conv_b, cache):
    bn = PROJ_BN
    k = w.shape[1]
    nb = FFN_DIM // bn
    norm_args, norm_specs = _norm_operands(norm)
    return pl.pallas_call(
        functools.partial(_mm_ffnup_kernel, stage=dict(layer=layer, col0s=(0, nb), n_tiles=nb)),
        out_shape=(
            jax.ShapeDtypeStruct((M_TOTAL, FFN_DIM), BF16),
            jax.ShapeDtypeStruct((SUBLANES, FFN_DIM), F32),
            jax.ShapeDtypeStruct((DEC_BATCH * SUBLANES, FFN_DIM), F32),
        ),
        grid=(nb, N_ROW_TILES),
        in_specs=norm_specs + [
            W_SPEC,
            pl.BlockSpec((CONV_W, bn), lambda n, m: (0, n)),
            pl.BlockSpec((1, bn), lambda n, m: (0, n)),
            pl.BlockSpec((DEC_BATCH, CONV_W - 1, bn), lambda n, m: (0, 0, n)),
        ],
        out_specs=(
            pl.BlockSpec((BM, bn), lambda n, m: (m, n)),
            pl.BlockSpec((SUBLANES, bn), lambda n, m: (0, n)),
            pl.BlockSpec((DEC_BATCH * SUBLANES, bn), lambda n, m: (0, n)),
        ),
        scratch_shapes=_weight_scratch(k, bn) + _weight_scratch(k, bn) + [pltpu.VMEM((SUBLANES, bn), F32)],
        compiler_params=_params(2),
        name="mm_ffn_up",
    )(*norm_args, w, conv_w, conv_b.reshape(1, FFN_DIM), cache)


def _ret_tables(lg, T):
    i = lax.broadcasted_iota(jnp.int32, (T, T), 0)
    j = lax.broadcasted_iota(jnp.int32, (T, T), 1)
    dist = jnp.abs(i - j).astype(F32)
    shift = int(math.log2(CHUNK))
    visible = (j >> shift) <= (i >> shift)
    mask = jnp.where(visible, jnp.exp(dist * lg), 0.0)
    t = lax.broadcasted_iota(jnp.int32, (T, 1), 0).astype(F32)
    cross = jnp.exp((t + 1.0) * lg)
    k_dec = jnp.exp((T - 1.0 - t) * lg)
    decay = jnp.exp(float(T) * lg)
    return mask, cross, k_dec, decay


def _ret_block(q, k, v, g, state, tables, gn):
    mask, cross, k_dec, decay = tables
    scores = lax.dot_general(q, k, (((1,), (1,)), ((), ())), preferred_element_type=F32)
    scores = scores * mask
    out = jnp.dot(scores.astype(BF16), v, preferred_element_type=F32)
    out = out + jnp.dot(q, state.astype(BF16), preferred_element_type=F32) * cross

    kd = (k.astype(F32) * k_dec).astype(BF16)
    new_state = decay * state + lax.dot_general(
        kd, v, (((0,), (0,)), ((), ())), preferred_element_type=F32)

    mu = jnp.mean(out, axis=-1, keepdims=True)
    oc = out - mu
    var = jnp.mean(oc * oc, axis=-1, keepdims=True)
    normed = (oc * lax.rsqrt(var + EPS)) * gn
    g = g.astype(F32)
    return (normed * (g * _sigmoid(g))).astype(BF16), new_state


def _ret_prompt_kernel(lg_ref, q_ref, k_ref, v_ref, g_ref, gn_ref, o_ref, sl_ref, state_ref, mask_ref):
    blk = pl.program_id(1)
    lg = lg_ref[...][:, 0:1]

    @pl.when(blk == 0)
    def _():
        state_ref[...] = jnp.zeros_like(state_ref)
        mask_ref[...] = _ret_tables(lg, RET_T)[0]

    _, cross, k_dec, decay = _ret_tables(lg, RET_T)
    tables = (mask_ref[...], cross, k_dec, decay)
    gn = gn_ref[...]
    state = state_ref[...]
    for sb in range(RET_SUB):
        rows = slice(sb * RET_T, (sb + 1) * RET_T)
        o_ref[rows, :], state = _ret_block(q_ref[rows, :], k_ref[rows, :], v_ref[rows, :],
                                           g_ref[rows, :], state, tables, gn)
    state_ref[...] = state

    @pl.when(blk == pl.num_programs(1) - 1)
    def _():
        sl_ref[...] = state


def _ret_sample_kernel(lg_ref, q_ref, k_ref, v_ref, g_ref, gn_ref, s0_ref, *rest, layer):
    o_ref, sl_ref = rest[-2:]
    if len(rest) == 2:
        for other in range(N_RET):
            if other != layer:
                sl_ref[other] = jnp.zeros(sl_ref.shape[1:], F32)
        sl_ref = sl_ref.at[layer]
    lg = lg_ref[...][:, 0:1]
    tables = _ret_tables(lg, DEC_SEQ)
    gn = gn_ref[...]
    for s in range(DEC_BATCH):
        rows = slice(s * DEC_SEQ, (s + 1) * DEC_SEQ)
        o_ref[rows, :], sl_ref[s] = _ret_block(q_ref[rows, :], k_ref[rows, :], v_ref[rows, :],
                                               g_ref[rows, :], s0_ref[s], tables, gn)


def _retention(qk, vg, gn, s0_stack, layer, sample_states=None):
    log_g = np.log1p(-np.exp2(-5.0 - np.arange(RET_HEADS, dtype=np.float32))).astype(np.float32)
    lg = jnp.asarray(np.broadcast_to(log_g[:, None, None], (RET_HEADS, 1, LANES)).copy())
    gn = gn.reshape(1, RET_VD)
    k_off = RET_QK // RET_DK
    g_off = RET_VD // RET_DV
    rows_p = RET_T * RET_SUB
    o_p, s_p = pl.pallas_call(
        _ret_prompt_kernel,
        out_shape=(jax.ShapeDtypeStruct((M_PROMPT, RET_VD), BF16),
                   jax.ShapeDtypeStruct((1, RET_HEADS, RET_DK, RET_DV), F32)),
        grid=(RET_HEADS, M_PROMPT // rows_p),
        in_specs=[
            pl.BlockSpec((None, 1, LANES), lambda h, b: (h, 0, 0)),
            pl.BlockSpec((rows_p, RET_DK), lambda h, b: (b, h)),
            pl.BlockSpec((rows_p, RET_DK), lambda h, b: (b, k_off + h)),
            pl.BlockSpec((rows_p, RET_DV), lambda h, b: (b, h)),
            pl.BlockSpec((rows_p, RET_DV), lambda h, b: (b, g_off + h)),
            pl.BlockSpec((1, RET_DV), lambda h, b: (0, h)),
        ],
        out_specs=(pl.BlockSpec((rows_p, RET_DV), lambda h, b: (b, h)),
                   pl.BlockSpec((None, None, RET_DK, RET_DV), lambda h, b: (0, h, 0, 0))),
        scratch_shapes=[pltpu.VMEM((RET_DK, RET_DV), F32), pltpu.VMEM((RET_T, RET_T), F32)],
        compiler_params=_params(2),
        name="retention",
    )(lg, qk, qk, vg, vg, gn)
    sample_block = M_PROMPT // M_SAMPLE
    carried = [] if sample_states is None else [sample_states]
    n_in = 7
    if carried:
        state_spec = pl.BlockSpec((None, DEC_BATCH, None, RET_DK, RET_DV), lambda h: (layer, 0, h, 0, 0))
    else:
        state_spec = pl.BlockSpec((N_RET, DEC_BATCH, None, RET_DK, RET_DV), lambda h: (0, 0, h, 0, 0))
    o_s, s_s = pl.pallas_call(
        functools.partial(_ret_sample_kernel, layer=layer),
        out_shape=(jax.ShapeDtypeStruct((M_SAMPLE, RET_VD), BF16),
                   jax.ShapeDtypeStruct((N_RET, DEC_BATCH, RET_HEADS, RET_DK, RET_DV), F32)),
        grid=(RET_HEADS,),
        in_specs=[
            pl.BlockSpec((None, 1, LANES), lambda h: (h, 0, 0)),
            pl.BlockSpec((M_SAMPLE, RET_DK), lambda h: (sample_block, h)),
            pl.BlockSpec((M_SAMPLE, RET_DK), lambda h: (sample_block, k_off + h)),
            pl.BlockSpec((M_SAMPLE, RET_DV), lambda h: (sample_block, h)),
            pl.BlockSpec((M_SAMPLE, RET_DV), lambda h: (sample_block, g_off + h)),
            pl.BlockSpec((1, RET_DV), lambda h: (0, h)),
            pl.BlockSpec((None, DEC_BATCH, None, RET_DK, RET_DV), lambda h: (layer, 0, h, 0, 0)),
        ] + [pl.BlockSpec(memory_space=pl.ANY) for _ in carried],
        out_specs=(pl.BlockSpec((M_SAMPLE, RET_DV), lambda h: (0, h)), state_spec),
        input_output_aliases={n_in: 1} if carried else {},
        compiler_params=_params(1),
        name="retention_sample",
    )(lg, qk, qk, vg, vg, gn, s0_stack, *carried)
    return o_p, o_s, s_p, s_s


def _ssm_prep_kernel(ar_ref, ai_ref, ldt_ref, br_ref, bi_ref, cr_ref, ci_ref,
                     bc_ref, cc_ref, w_ref, a16r_ref, a16i_ref, vr_s, vi_s):
    ar = ar_ref[...]
    ai = ai_ref[...]
    dt = jnp.exp(ldt_ref[...])
    mag = jnp.exp(ar * dt)
    ang = ai * dt
    abr = mag * jnp.cos(ang)
    abi = mag * jnp.sin(ang)
    den = ar * ar + ai * ai
    nr = abr - 1.0
    ni = abi
    cfr = (nr * ar + ni * ai) / den
    cfi = (ni * ar - nr * ai) / den

    def b4(z):
        return z[:, None]

    br = br_ref[...]
    bi = bi_ref[...]
    bbr = b4(cfr) * br - b4(cfi) * bi
    bbi = b4(cfr) * bi + b4(cfi) * br
    cre = cr_ref[...]
    cim = ci_ref[...]

    powers = []
    pr = jnp.ones_like(ar)
    pi = jnp.zeros_like(ar)
    for _ in range(SSM_T + 1):
        powers.append((pr, pi))
        pr, pi = pr * abr - pi * abi, pr * abi + pi * abr
    a16r_ref[...] = powers[SSM_T][0]
    a16i_ref[...] = powers[SSM_T][1]

    for m in range(SSM_T + 1):
        pr, pi = powers[m]
        vr = b4(pr) * cre - b4(pi) * cim
        vi = b4(pr) * cim + b4(pi) * cre
        if m < SSM_T:
            vr_s[:, m] = vr
            vi_s[:, m] = vi
        if m >= 1:
            cc_ref[:, m - 1, :, :, 0:LANES] = vr.astype(BF16)
            cc_ref[:, m - 1, :, :, LANES:2 * LANES] = (-vi).astype(BF16)

    for t in range(SSM_T):
        pr, pi = powers[SSM_T - 1 - t]
        bc_ref[:, t, :, :, 0:LANES] = (b4(pr) * bbr - b4(pi) * bbi).astype(BF16)
        bc_ref[:, t, :, :, LANES:2 * LANES] = (b4(pr) * bbi + b4(pi) * bbr).astype(BF16)

    lane = lax.broadcasted_iota(jnp.int32, (SSM_PAIR_CH, SSM_PAIR_COLS), 1)
    contract_last = (((1,), (1,)), ((), ()))
    for q in range(SSM_PB):
        vr_all = vr_s[q].reshape(SSM_PAIR_COLS, LANES)
        vi_all = vi_s[q].reshape(SSM_PAIR_COLS, LANES)
        lags = (lax.dot_general(bbr[q].reshape(SSM_PAIR_CH, LANES), vr_all, contract_last,
                                precision=lax.Precision.HIGHEST, preferred_element_type=F32)
                - lax.dot_general(bbi[q].reshape(SSM_PAIR_CH, LANES), vi_all, contract_last,
                                  precision=lax.Precision.HIGHEST, preferred_element_type=F32))
        for t in range(SSM_T):
            shifted = lags if t == 0 else pltpu.roll(lags, SSM_PAIR_CH * t, axis=1)
            w_ref[q, t * SSM_PAIR_CH:(t + 1) * SSM_PAIR_CH, :] = jnp.where(
                lane >= SSM_PAIR_CH * t, shifted, 0.0).astype(BF16)


def _pair_pack(x):
    x = x.reshape(SSM_PAIRS, 2, SSM_GROUP, SSM_STATE)
    lo = jnp.pad(x[:, 0], ((0, 0), (0, 0), (0, SSM_STATE)))
    hi = jnp.pad(x[:, 1], ((0, 0), (0, 0), (SSM_STATE, 0)))
    return jnp.stack([lo, hi], axis=1)


def _ssm_prep(a_re, a_im, log_dt, b_re, b_im, c_re, c_im):
    assert 2 * SSM_PB * SSM_GROUP == LANES
    ar = a_re.reshape(SSM_PAIRS, 1, LANES)
    ai = a_im.reshape(SSM_PAIRS, 1, LANES)
    ldt = jnp.repeat(log_dt, SSM_STATE).reshape(SSM_PAIRS, 1, LANES)
    b2r = _pair_pack(jnp.transpose(b_re, (0, 2, 1)))
    b2i = _pair_pack(jnp.transpose(b_im, (0, 2, 1)))
    c2r = _pair_pack(c_re)
    c2i = _pair_pack(c_im)
    vec = pl.BlockSpec((SSM_PB, 1, LANES), lambda p: (p, 0, 0))
    mat = pl.BlockSpec((SSM_PB, 2, SSM_GROUP, LANES), lambda p: (p, 0, 0, 0))
    proj = pl.BlockSpec((SSM_PB, SSM_T, 2, SSM_GROUP, 2 * LANES), lambda p: (p, 0, 0, 0, 0))
    proj_shape = jax.ShapeDtypeStruct((SSM_PAIRS, SSM_T, 2, SSM_GROUP, 2 * LANES), BF16)
    bc, cc, w, a16r, a16i = pl.pallas_call(
        _ssm_prep_kernel,
        out_shape=(
            proj_shape, proj_shape,
            jax.ShapeDtypeStruct((SSM_PAIRS, SSM_PAIR_COLS, SSM_PAIR_COLS), BF16),
            jax.ShapeDtypeStruct((SSM_PAIRS, 1, LANES), F32),
            jax.ShapeDtypeStruct((SSM_PAIRS, 1, LANES), F32),
        ),
        grid=(N_LANE_TILES,),
        in_specs=[vec, vec, vec, mat, mat, mat, mat],
        out_specs=(proj, proj,
                   pl.BlockSpec((SSM_PB, SSM_PAIR_COLS, SSM_PAIR_COLS), lambda p: (p, 0, 0)),
                   vec, vec),
        scratch_shapes=[pltpu.VMEM((SSM_PB, SSM_T, 2, SSM_GROUP, LANES), F32)] * 2,
        compiler_params=_params(1),
        name="ssm_prep",
    )(ar, ai, ldt, b2r, b2i, c2r, c2i)
    bc = bc.reshape(SSM_PAIRS, SSM_PAIR_COLS, 2 * LANES)
    cc = cc.reshape(SSM_PAIRS, SSM_PAIR_COLS, 2 * LANES)
    row = SSM_GROUPS * SSM_STATE
    a16r = jnp.broadcast_to(a16r.reshape(1, row), (SSM_SEGS, row))
    a16i = jnp.broadcast_to(a16i.reshape(1, row), (SSM_SEGS, row))
    return bc, cc, w, a16r, a16i


SSM_CHUNK_COLS = SSM_T * LANES
SSM_STATE_COLS = SSM_PB * LANES
SSM_CHUNKS_P = M_PROMPT // SSM_T
SSM_SCAN_UNROLL = 8


def _ssm_main_kernel(x_ref, w_ref, bc_ref, cc_ref, ar_ref, ai_ref, h0r_ref, h0i_ref,
                     y_ref, fpr_ref, fpi_ref, fsr_ref, fsi_ref, d_s, s_s, t_s):
    pair_cols = [slice(q * SSM_PAIR_COLS, (q + 1) * SSM_PAIR_COLS) for q in range(SSM_PB)]

    for q in range(SSM_PB):
        d = jnp.dot(x_ref[:, pair_cols[q]], bc_ref[q], preferred_element_type=F32)
        d_s[q] = d[:, 0:LANES]
        d_s[SSM_PB + q] = d[:, LANES:2 * LANES]

    tiles = range(SSM_PB)
    ar = [ar_ref[:, k * LANES:(k + 1) * LANES] for k in tiles]
    ai = [ai_ref[:, k * LANES:(k + 1) * LANES] for k in tiles]

    def advance(sr, si, rows, record=None):
        out_r, out_i = [], []
        for k in tiles:
            if record is not None:
                ref, rec_rows = record
                ref[k, rec_rows, :] = sr[k]
                ref[SSM_PB + k, rec_rows, :] = si[k]
            dr = d_s[k, rows, :]
            di = d_s[SSM_PB + k, rows, :]
            out_r.append(ar[k] * sr[k] - ai[k] * si[k] + dr)
            out_i.append(ar[k] * si[k] + ai[k] * sr[k] + di)
        return out_r, out_i

    def prompt_rows(i):
        return pl.ds(i, SSM_SEGS, stride=SSM_STEPS_P)

    def prompt_pass(start_r, start_i, record):
        def body(i, carry):
            rec = (t_s, pl.ds(pl.multiple_of(i * SSM_SEGS, SSM_SEGS), SSM_SEGS)) if record else None
            out_r, out_i = advance(list(carry[0]), list(carry[1]), prompt_rows(i), rec)
            return tuple(out_r), tuple(out_i)
        out_r, out_i = lax.fori_loop(0, SSM_STEPS_P, body, (tuple(start_r), tuple(start_i)),
                                     unroll=SSM_SCAN_UNROLL)
        return list(out_r), list(out_i)

    zeros = [jnp.zeros((SSM_SEGS, LANES), F32) for _ in tiles]
    er, ei = prompt_pass(zeros, zeros, record=False)

    sr, si = [], []
    for k in tiles:
        pr, pi = ar[k][0:1], ai[k][0:1]
        for _ in range(int(math.log2(SSM_STEPS_P))):
            pr, pi = pr * pr - pi * pi, 2.0 * pr * pi
        cr = jnp.zeros((1, LANES), F32)
        ci = cr
        starts_r, starts_i = [], []
        for s in range(SSM_SEGS):
            starts_r.append(cr)
            starts_i.append(ci)
            cr, ci = (pr * cr - pi * ci + er[k][s:s + 1], pr * ci + pi * cr + ei[k][s:s + 1])
        cols = slice(k * LANES, (k + 1) * LANES)
        fpr_ref[:, cols] = jnp.broadcast_to(cr, (SSM_SEGS, LANES))
        fpi_ref[:, cols] = jnp.broadcast_to(ci, (SSM_SEGS, LANES))
        sr.append(jnp.concatenate(starts_r, axis=0))
        si.append(jnp.concatenate(starts_i, axis=0))

    prompt_pass(sr, si, record=True)
    for k in range(2 * SSM_PB):
        for s in range(SSM_SEGS):
            s_s[k, s * SSM_STEPS_P:(s + 1) * SSM_STEPS_P, :] = t_s[k, pl.ds(s, SSM_STEPS_P, stride=SSM_SEGS), :]
    sr = [h0r_ref[:, k * LANES:(k + 1) * LANES] for k in tiles]
    si = [h0i_ref[:, k * LANES:(k + 1) * LANES] for k in tiles]
    for i in range(SSM_STEPS_S):
        rows = pl.ds(SSM_CHUNKS_P + i, DEC_BATCH, stride=SSM_STEPS_S)
        sr, si = advance(sr, si, rows, (s_s, rows))
    for k in tiles:
        fsr_ref[:, k * LANES:(k + 1) * LANES] = sr[k]
        fsi_ref[:, k * LANES:(k + 1) * LANES] = si[k]

    contract_last = (((1,), (1,)), ((), ()))
    for q in range(SSM_PB):
        states = jnp.concatenate([s_s[q].astype(BF16), s_s[SSM_PB + q].astype(BF16)], axis=1)
        carried = lax.dot_general(states, cc_ref[q], contract_last, preferred_element_type=F32)
        local = jnp.dot(x_ref[:, pair_cols[q]], w_ref[q], preferred_element_type=F32)
        y_ref[:, pair_cols[q]] = local + carried


def _ssm_main(x_c, w, bc, cc, a16r, a16i, h0r, h0i):
    row = SSM_GROUPS * SSM_STATE
    st = pl.BlockSpec((SSM_SEGS, SSM_STATE_COLS), lambda j: (0, j))
    st_shape = jax.ShapeDtypeStruct((SSM_SEGS, row), F32)
    proj = pl.BlockSpec((SSM_PB, SSM_PAIR_COLS, 2 * LANES), lambda j: (j, 0, 0))
    chunk_rows = pl.BlockSpec((None, N_CHUNKS, SSM_CHUNK_COLS), lambda j: (j, 0, 0))
    return pl.pallas_call(
        _ssm_main_kernel,
        out_shape=(jax.ShapeDtypeStruct((N_LANE_TILES, N_CHUNKS, SSM_CHUNK_COLS), F32),
                   st_shape, st_shape, st_shape, st_shape),
        grid=(N_LANE_TILES,),
        in_specs=[
            chunk_rows,
            pl.BlockSpec((SSM_PB, SSM_PAIR_COLS, SSM_PAIR_COLS), lambda j: (j, 0, 0)),
            proj, proj, st, st, st, st,
        ],
        out_specs=(chunk_rows, st, st, st, st),
        scratch_shapes=[
            pltpu.VMEM((2 * SSM_PB, N_CHUNKS, LANES), F32),
            pltpu.VMEM((2 * SSM_PB, N_CHUNKS, LANES), F32),
            pltpu.VMEM((2 * SSM_PB, SSM_CHUNKS_P, LANES), F32),
        ],
        compiler_params=_params(1),
        name="ssm_main",
    )(x_c, w, bc, cc, a16r, a16i, h0r, h0i)


def _ssm_post_kernel(y_ref, x_ref, g_ref, d_ref, o_ref, y_s):
    block_of_lane = lax.broadcasted_iota(jnp.int32, (CHUNKS_PER_TILE, LANES), 1) // SSM_PAIR_CH
    for j in range(N_LANE_TILES):
        for tg in range(SSM_T // SSM_VEC_TOKENS):
            pieces = [y_ref[j, :, q * SSM_PAIR_COLS + tg * LANES:q * SSM_PAIR_COLS + (tg + 1) * LANES]
                      for q in range(SSM_PB)]
            for u, out in enumerate(_transpose_lane_blocks(pieces, block_of_lane)):
                y_s[j, pl.ds(tg * SSM_VEC_TOKENS + u, CHUNKS_PER_TILE, stride=SSM_T), :] = out
    x = x_ref[...]
    ms = jnp.mean(x * x, axis=-1, keepdims=True)
    h = (x * lax.rsqrt(ms + EPS)) * g_ref[...]
    for j in range(N_LANE_TILES):
        cols = slice(j * LANES, (j + 1) * LANES)
        y = y_s[j] + d_ref[:, cols] * h[:, cols]
        gelu = 0.5 * y * (1.0 + jnp.tanh(GELU_TANH_SCALE * (y + GELU_TANH_CUBIC * (y * y * y))))
        o_ref[:, cols] = gelu.astype(BF16)


def _ssm_post(y_c, x, g, d):
    spec = pl.BlockSpec((BM, D_MODEL), lambda m: (m, 0))
    vec = pl.BlockSpec((1, D_MODEL), lambda m: (0, 0))
    return pl.pallas_call(
        _ssm_post_kernel,
        out_shape=jax.ShapeDtypeStruct((M_TOTAL, D_MODEL), BF16),
        grid=(N_ROW_TILES,),
        in_specs=[pl.BlockSpec((N_LANE_TILES, CHUNKS_PER_TILE, SSM_CHUNK_COLS), lambda m: (0, m, 0)),
                  spec, vec, vec],
        out_specs=spec,
        scratch_shapes=[pltpu.VMEM((N_LANE_TILES, BM, LANES), F32)],
        compiler_params=_params(1),
        name="ssm_post",
    )(y_c, x, g.reshape(1, D_MODEL), d.reshape(1, D_MODEL))


def kernel(x_prompt, x_sample, state_ret, state_ssm_re, state_ssm_im, cache_conv, norm_mix, norm_ffn, norm_final, ret_w_in, ret_gn, ret_w_out, ssm_a_re, ssm_a_im, ssm_log_dt, ssm_b_re, ssm_b_im, ssm_c_re, ssm_c_im, ssm_d, ssm_w_glu, ffn_w_up, ffn_conv_w, ffn_conv_b, ffn_w_down):
    x = (x_prompt.reshape(M_PROMPT, D_MODEL), x_sample.reshape(M_SAMPLE, D_MODEL))
    xb, ssq = _cast_ssq(*x)
    cos, sin = _rope_tables()

    ret_p, ret_s = [], None
    re_p, im_p, re_s, im_s = [], [], [], []
    conv_p, conv_s = [], []
    state_row = SSM_GROUPS * SSM_STATE
    for i in range(DEPTH):
        j = i // 2
        if i % 2 == 0:
            qk, vg = _mm_retin((xb, ssq, norm_mix[i]), ret_w_in, j, cos, sin)
            o_p, o_s, s_p, ret_s = _retention(qk, vg, ret_gn[j], state_ret, j, ret_s)
            ret_p.append(s_p)
            x, xb, ssq = _mm_res((o_p, o_s), ret_w_out, j, x, emit=True)
        else:
            h_c = _rmsnorm_chunked(x, norm_mix[i])
            bc, cc, w_ssm, a16r, a16i = _ssm_prep(ssm_a_re[j], ssm_a_im[j], ssm_log_dt[j],
                                               ssm_b_re[j], ssm_b_im[j], ssm_c_re[j], ssm_c_im[j])
            y_c, fpr, fpi, fsr, fsi = _ssm_main(
                h_c, w_ssm, bc, cc, a16r, a16i,
                state_ssm_re[j].reshape(DEC_BATCH, state_row),
                state_ssm_im[j].reshape(DEC_BATCH, state_row))
            re_p.append(fpr[0:1].reshape(1, SSM_GROUPS, SSM_STATE))
            im_p.append(fpi[0:1].reshape(1, SSM_GROUPS, SSM_STATE))
            re_s.append(fsr.reshape(DEC_BATCH, SSM_GROUPS, SSM_STATE))
            im_s.append(fsi.reshape(DEC_BATCH, SSM_GROUPS, SSM_STATE))
            gl = _ssm_post(y_c, x, norm_mix[i], ssm_d[j])
            x, xb, ssq = _mm_glu(gl, ssm_w_glu, j, x)

        u, tail_p, tail_s = _mm_ffnup((xb, ssq, norm_ffn[i]), ffn_w_up, i, ffn_conv_w[i],
                                      ffn_conv_b[i], cache_conv[i])
        conv_p.append(tail_p[SUBLANES - (CONV_W - 1):].reshape(1, CONV_W - 1, FFN_DIM))
        conv_s.append(tail_s.reshape(DEC_BATCH, SUBLANES, FFN_DIM)[:, SUBLANES - (CONV_W - 1):])
        if i + 1 < DEPTH and (i + 1) % 2 == 0:
            x, xb, ssq = _mm_res(u, ffn_w_down, i, x, emit=True)
        else:
            x = _mm_res(u, ffn_w_down, i, x)

    y_prompt = _rmsnorm(x, norm_final, F32, 0, N_PROMPT_TILES).reshape(1, SEQ, D_MODEL)
    y_sample = _rmsnorm(x, norm_final, F32, N_PROMPT_TILES, 1).reshape(DEC_BATCH, DEC_SEQ, D_MODEL)
    return (y_prompt, y_sample, jnp.stack(ret_p), ret_s,
            jnp.stack(re_p), jnp.stack(im_p), jnp.stack(re_s), jnp.stack(im_s),
            jnp.stack(conv_p), jnp.stack(conv_s))
```

```python
import functools
import math

import numpy as np
import jax
import jax.numpy as jnp
from jax import lax
from jax.experimental import pallas as pl
from jax.experimental.pallas import tpu as pltpu

F32 = jnp.float32
BF16 = jnp.bfloat16

D_MODEL = 2048
SEQ = 8192
DEPTH = 4
DEC_BATCH = 8
DEC_SEQ = 64
PAST_LEN = 1024
CHUNK = 64
RET_HEADS = 8
RET_DK = D_MODEL // RET_HEADS
RET_DV = 2 * RET_DK
RET_QK = RET_HEADS * RET_DK
RET_VD = RET_HEADS * RET_DV
N_RET = (DEPTH + 1) // 2
ROPE_BASE = 10000.0
SSM_GROUP = 16
SSM_GROUPS = D_MODEL // SSM_GROUP
SSM_STATE = 64
FFN_DIM = 2 * D_MODEL
CONV_W = 3
EPS = 1e-6

M_PROMPT = SEQ
M_SAMPLE = DEC_BATCH * DEC_SEQ
M_TOTAL = M_PROMPT + M_SAMPLE

LANES = 128
SUBLANES = 8
V7X_VMEM_BYTES = 64 * 1024 * 1024
VMEM_LIMIT_BYTES = V7X_VMEM_BYTES // 8 * 7

BM = M_SAMPLE
N_ROW_TILES = M_TOTAL // BM
N_PROMPT_TILES = M_PROMPT // BM
PROJ_BN = 1024
RET_IN_BN = 2048

RET_T = 256
RET_SUB = 8

SSM_T = 16
SSM_PAIRS = SSM_GROUPS // 2
SSM_PB = 4
SSM_SEGS = 8
SSM_STEPS_P = M_PROMPT // SSM_T // SSM_SEGS
SSM_STEPS_S = DEC_SEQ // SSM_T
SSM_PAIR_CH = 2 * SSM_GROUP
SSM_PAIR_COLS = SSM_T * SSM_PAIR_CH
SSM_VEC_TOKENS = LANES // SSM_PAIR_CH


def _params(n_axes):
    return pltpu.CompilerParams(dimension_semantics=("arbitrary",) * n_axes,
                                vmem_limit_bytes=VMEM_LIMIT_BYTES)


GELU_TANH_SCALE = math.sqrt(2.0 / math.pi)
GELU_TANH_CUBIC = 0.044715


def _sigmoid(x):
    return 1.0 / (1.0 + jnp.exp(-x))


def _rope_tile(cos_ref, sin_ref):
    m = pl.program_id(0)
    half = RET_DK // 2
    freq = lax.broadcasted_iota(jnp.int32, (1, half), 1).astype(F32)
    inv = ROPE_BASE ** (-freq / half)
    r = m * BM + lax.broadcasted_iota(jnp.int32, (BM, half), 0)
    pos = jnp.where(r < M_PROMPT, r, PAST_LEN + ((r - M_PROMPT) & (DEC_SEQ - 1)))
    ang = pos.astype(F32) * inv
    cos_ref[...] = jnp.cos(ang)
    sin_ref[...] = jnp.sin(ang)


def _norm_kernel(x_ref, g_ref, o_ref):
    x = x_ref[...]
    ms = jnp.mean(x * x, axis=-1, keepdims=True)
    o_ref[...] = ((x * lax.rsqrt(ms + EPS)) * g_ref[...]).astype(o_ref.dtype)


def _rmsnorm(x, g, out_dtype, tile0=0, n_tiles=N_ROW_TILES):
    return pl.pallas_call(
        _norm_kernel,
        out_shape=jax.ShapeDtypeStruct((n_tiles * BM, D_MODEL), out_dtype),
        grid=(n_tiles,),
        in_specs=[pl.BlockSpec((BM, D_MODEL), lambda m: (tile0 + m, 0)),
                  pl.BlockSpec((1, D_MODEL), lambda m: (0, 0))],
        out_specs=pl.BlockSpec((BM, D_MODEL), lambda m: (m, 0)),
        compiler_params=_params(1),
        name="rmsnorm",
    )(x, g.reshape(1, D_MODEL))


CHUNKS_PER_TILE = BM // 16
N_CHUNKS = M_TOTAL // 16
N_LANE_TILES = D_MODEL // LANES


def _transpose_lane_blocks(v, block_of_lane):
    n = len(v)
    out = []
    for u in range(n):
        acc = None
        for q in range(n):
            shift = ((q - u) * SSM_PAIR_CH) % LANES
            moved = v[q] if shift == 0 else pltpu.roll(v[q], shift, axis=1)
            acc = moved if acc is None else jnp.where(block_of_lane == q, moved, acc)
        out.append(acc)
    return out


def _norm_chunked_kernel(x_ref, g_ref, o_ref, h_s):
    x = x_ref[...]
    ms = jnp.mean(x * x, axis=-1, keepdims=True)
    h = (x * lax.rsqrt(ms + EPS)) * g_ref[...]
    for j in range(N_LANE_TILES):
        h_s[j] = h[:, j * LANES:(j + 1) * LANES]
    block_of_lane = lax.broadcasted_iota(jnp.int32, (CHUNKS_PER_TILE, LANES), 1) // SSM_PAIR_CH
    for j in range(N_LANE_TILES):
        for tg in range(SSM_T // SSM_VEC_TOKENS):
            rows = [h_s[j, pl.ds(tg * SSM_VEC_TOKENS + u, CHUNKS_PER_TILE, stride=SSM_T), :]
                    for u in range(SSM_VEC_TOKENS)]
            for q, out in enumerate(_transpose_lane_blocks(rows, block_of_lane)):
                c0 = q * SSM_PAIR_COLS + tg * LANES
                o_ref[j, :, c0:c0 + LANES] = out.astype(BF16)


def _rmsnorm_chunked(x, g):
    return pl.pallas_call(
        _norm_chunked_kernel,
        out_shape=jax.ShapeDtypeStruct((N_LANE_TILES, N_CHUNKS, SSM_T * LANES), BF16),
        grid=(N_ROW_TILES,),
        in_specs=[pl.BlockSpec((BM, D_MODEL), lambda m: (m, 0)),
                  pl.BlockSpec((1, D_MODEL), lambda m: (0, 0))],
        out_specs=pl.BlockSpec((N_LANE_TILES, CHUNKS_PER_TILE, SSM_T * LANES), lambda m: (0, m, 0)),
        scratch_shapes=[pltpu.VMEM((N_LANE_TILES, BM, LANES), F32)],
        compiler_params=_params(1),
        name="rmsnorm_chunked",
    )(x, g.reshape(1, D_MODEL))


W_SPEC = pl.BlockSpec(memory_space=pl.ANY)


def _weight_scratch(k, bn):
    return [pltpu.VMEM((k, bn), BF16), pltpu.VMEM((k, bn), F32), pltpu.SemaphoreType.DMA(())]


def _stage_weights(w_hbm, staged, *, layer, col0s, n_tiles, gain_ref=None):
    n = pl.program_id(0)
    m = pl.program_id(1)

    def copies(tile):
        out = []
        for (_, wf_ref, sem), col0 in zip(staged, col0s):
            bn = wf_ref.shape[1]
            cols = pl.ds(pl.multiple_of((col0 + tile) * bn, bn), bn)
            out.append(pltpu.make_async_copy(w_hbm.at[layer, :, cols], wf_ref, sem))
        return out

    @pl.when((n == 0) & (m == 0))
    def _():
        for copy in copies(0):
            copy.start()

    @pl.when(m == 0)
    def _():
        for copy, (wb_ref, wf_ref, _) in zip(copies(n), staged):
            copy.wait()
            w = wf_ref[...]
            if gain_ref is not None:
                w = w * gain_ref[...]
            wb_ref[...] = w.astype(BF16)

    @pl.when((m == 1) & (n + 1 < n_tiles))
    def _():
        for copy in copies(n + 1):
            copy.start(priority=1)


def _emit_norm_inputs(x_new, xb_ref, ssq_ref):
    xb_ref[...] = x_new.astype(BF16)
    sq = x_new * x_new
    part = sq[:, 0:LANES]
    for t in range(1, x_new.shape[1] // LANES):
        part = part + sq[:, t * LANES:(t + 1) * LANES]
    ssq_ref[...] = part


def _row_scale(ssq_ref):
    total = jnp.sum(jnp.sum(ssq_ref[...], axis=0), axis=-1, keepdims=True)
    return lax.rsqrt(total * (1.0 / D_MODEL) + EPS)


def _norm_operands(norm):
    xb, ssq, g = norm
    args = [xb, ssq, g.reshape(D_MODEL, 1)]
    specs = [pl.BlockSpec((BM, D_MODEL), lambda n, m: (m, 0)),
             pl.BlockSpec((ssq.shape[0], BM, LANES), lambda n, m: (0, m, 0)),
             pl.BlockSpec((D_MODEL, 1), lambda n, m: (0, 0))]
    return args, specs


def _norm_outputs(n_out, bn):
    shapes = [jax.ShapeDtypeStruct((M_TOTAL, n_out), BF16),
              jax.ShapeDtypeStruct((n_out // bn, M_TOTAL, LANES), F32)]
    specs = [pl.BlockSpec((BM, bn), lambda n, m: (m, n)),
             pl.BlockSpec((None, BM, LANES), lambda n, m: (n, m, 0))]
    return shapes, specs


def _prologue_kernel(xp_ref, xs_ref, xb_ref, ssq_ref, cos_ref, sin_ref):
    x = jnp.where(pl.program_id(0) < N_PROMPT_TILES, xp_ref[...], xs_ref[...])
    _emit_norm_inputs(x, xb_ref, ssq_ref)
    _rope_tile(cos_ref, sin_ref)


def _prologue(x_prompt, x_sample):
    half = RET_DK // 2
    rope_spec = pl.BlockSpec((BM, half), lambda m: (m, 0))
    rope_shape = jax.ShapeDtypeStruct((M_TOTAL, half), F32)
    return pl.pallas_call(
        _prologue_kernel,
        out_shape=(jax.ShapeDtypeStruct((M_TOTAL, D_MODEL), BF16),
                   jax.ShapeDtypeStruct((1, M_TOTAL, LANES), F32), rope_shape, rope_shape),
        grid=(N_ROW_TILES,),
        in_specs=[pl.BlockSpec((BM, D_MODEL), lambda m: (jnp.minimum(m, N_PROMPT_TILES - 1), 0)),
                  pl.BlockSpec((BM, D_MODEL), lambda m: (0, 0))],
        out_specs=(pl.BlockSpec((BM, D_MODEL), lambda m: (m, 0)),
                   pl.BlockSpec((None, BM, LANES), lambda m: (0, m, 0)), rope_spec, rope_spec),
        compiler_params=_params(1),
        name="prologue",
    )(x_prompt, x_sample)


def _mm_qk_kernel(a_ref, ssq_ref, g_ref, w_hbm, cos_ref, sin_ref, o_ref, wb_ref, wf_ref, sem, *, stage):
    n = pl.program_id(0)
    n_q = RET_QK // RET_IN_BN
    _stage_weights(w_hbm, [(wb_ref, wf_ref, sem)], gain_ref=g_ref, **stage)

    acc = jnp.dot(a_ref[...], wb_ref[...], preferred_element_type=F32) * _row_scale(ssq_ref)
    scale = jnp.where(n >= n_q, RET_DK ** -0.5, 1.0).astype(F32)
    c = cos_ref[...]
    s = sin_ref[...]
    half = RET_DK // 2
    for hh in range(RET_IN_BN // RET_DK):
        lo = hh * RET_DK
        x1 = acc[:, lo:lo + half]
        x2 = acc[:, lo + half:lo + RET_DK]
        o_ref[:, lo:lo + half] = ((x1 * c - x2 * s) * scale).astype(BF16)
        o_ref[:, lo + half:lo + RET_DK] = ((x1 * s + x2 * c) * scale).astype(BF16)


def _mm_cast_kernel(a_ref, ssq_ref, g_ref, w_hbm, o_ref, wb_ref, wf_ref, sem, *, stage):
    _stage_weights(w_hbm, [(wb_ref, wf_ref, sem)], gain_ref=g_ref, **stage)

    acc = jnp.dot(a_ref[...], wb_ref[...], preferred_element_type=F32) * _row_scale(ssq_ref)
    o_ref[...] = acc.astype(BF16)


def _mm_retin(norm, w, layer, cos, sin):
    bn = RET_IN_BN
    _, k, n_out = w.shape
    half = RET_DK // 2
    n_qk = 2 * RET_QK // bn
    norm_args, norm_specs = _norm_operands(norm)
    out_spec = pl.BlockSpec((BM, bn), lambda n, m: (m, n))
    rope_spec = pl.BlockSpec((BM, half), lambda n, m: (m, 0))
    n_vg = n_out // bn - n_qk
    qk = pl.pallas_call(
        functools.partial(_mm_qk_kernel, stage=dict(layer=layer, col0s=(0,), n_tiles=n_qk)),
        out_shape=jax.ShapeDtypeStruct((M_TOTAL, 2 * RET_QK), BF16),
        grid=(n_qk, N_ROW_TILES),
        in_specs=norm_specs + [W_SPEC, rope_spec, rope_spec],
        out_specs=out_spec,
        scratch_shapes=_weight_scratch(k, bn),
        compiler_params=_params(2),
        name="mm_ret_qk",
    )(*norm_args, w, cos, sin)
    vg = pl.pallas_call(
        functools.partial(_mm_cast_kernel, stage=dict(layer=layer, col0s=(n_qk,), n_tiles=n_vg)),
        out_shape=jax.ShapeDtypeStruct((M_TOTAL, 2 * RET_VD), BF16),
        grid=(n_vg, N_ROW_TILES),
        in_specs=norm_specs + [W_SPEC],
        out_specs=out_spec,
        scratch_shapes=_weight_scratch(k, bn),
        compiler_params=_params(2),
        name="mm_ret_vg",
    )(*norm_args, w)
    return qk, vg


def _row_split_specs(cols, col_index):
    return [pl.BlockSpec((BM, cols), lambda n, m: (jnp.minimum(m, N_PROMPT_TILES - 1), col_index(n))),
            pl.BlockSpec((BM, cols), lambda n, m: (0, col_index(n)))]


def _mm_res_kernel(*refs, split_a, split_x, emit, stage):
    refs = list(refs)
    ap_ref = refs.pop(0)
    as_ref = refs.pop(0) if split_a else None
    w_hbm = refs.pop(0)
    xp_ref = refs.pop(0)
    xs_ref = refs.pop(0) if split_x else None
    o_ref = refs.pop(0)
    xb_ref, ssq_ref = (refs.pop(0), refs.pop(0)) if emit else (None, None)
    wb_ref, wf_ref, sem = refs
    m = pl.program_id(1)
    on_prompt = m < N_PROMPT_TILES
    _stage_weights(w_hbm, [(wb_ref, wf_ref, sem)], **stage)

    def body(a_ref):
        x = xp_ref[...]
        if split_x:
            x = jnp.where(on_prompt, x, xs_ref[...])
        x_new = x + jnp.dot(a_ref[...], wb_ref[...], preferred_element_type=F32)
        o_ref[...] = x_new
        if emit:
            _emit_norm_inputs(x_new, xb_ref, ssq_ref)

    if split_a:
        pl.when(on_prompt)(lambda: body(ap_ref))
        pl.when(jnp.logical_not(on_prompt))(lambda: body(as_ref))
    else:
        body(ap_ref)


def _mm_res(a, w, layer, x, emit=False):
    bn = PROJ_BN
    _, k, n_out = w.shape
    split_a = isinstance(a, tuple)
    split_x = isinstance(x, tuple)
    a_specs = _row_split_specs(k, lambda n: 0) if split_a else [pl.BlockSpec((BM, k), lambda n, m: (m, 0))]
    x_specs = _row_split_specs(bn, lambda n: n) if split_x else [pl.BlockSpec((BM, bn), lambda n, m: (m, n))]
    a_args = list(a) if split_a else [a]
    x_args = list(x) if split_x else [x]
    out_shapes = [jax.ShapeDtypeStruct((M_TOTAL, n_out), F32)]
    out_specs = [pl.BlockSpec((BM, bn), lambda n, m: (m, n))]
    if emit:
        shapes, specs = _norm_outputs(n_out, bn)
        out_shapes += shapes
        out_specs += specs
    out = pl.pallas_call(
        functools.partial(_mm_res_kernel, split_a=split_a, split_x=split_x, emit=emit,
                          stage=dict(layer=layer, col0s=(0,), n_tiles=n_out // bn)),
        out_shape=tuple(out_shapes),
        grid=(n_out // bn, N_ROW_TILES),
        in_specs=a_specs + [W_SPEC] + x_specs,
        out_specs=tuple(out_specs),
        scratch_shapes=_weight_scratch(k, bn),
        compiler_params=_params(2),
        name="mm_residual",
    )(*a_args, w, *x_args)
    return out if emit else out[0]


def _mm_glu_kernel(a_ref, w_hbm, x_ref, o_ref, xb_ref, ssq_ref,
                   wab_ref, waf_ref, sem_a, wgb_ref, wgf_ref, sem_g, *, stage):
    _stage_weights(w_hbm, [(wab_ref, waf_ref, sem_a), (wgb_ref, wgf_ref, sem_g)], **stage)

    a = a_ref[...]
    ga = jnp.dot(a, wab_ref[...], preferred_element_type=F32)
    gb = jnp.dot(a, wgb_ref[...], preferred_element_type=F32)
    x_new = x_ref[...] + ga * _sigmoid(gb)
    o_ref[...] = x_new
    _emit_norm_inputs(x_new, xb_ref, ssq_ref)


def _mm_glu(a, w, layer, x):
    bn = PROJ_BN
    _, k, n2 = w.shape
    n_out = n2 // 2
    nb = n_out // bn
    norm_shapes, norm_specs = _norm_outputs(n_out, bn)
    return pl.pallas_call(
        functools.partial(_mm_glu_kernel, stage=dict(layer=layer, col0s=(0, nb), n_tiles=nb)),
        out_shape=(jax.ShapeDtypeStruct((M_TOTAL, n_out), F32), *norm_shapes),
        grid=(nb, N_ROW_TILES),
        in_specs=[
            pl.BlockSpec((BM, k), lambda n, m: (m, 0)),
            W_SPEC,
            pl.BlockSpec((BM, bn), lambda n, m: (m, n)),
        ],
        out_specs=(pl.BlockSpec((BM, bn), lambda n, m: (m, n)), *norm_specs),
        scratch_shapes=_weight_scratch(k, bn) + _weight_scratch(k, bn),
        compiler_params=_params(2),
        name="mm_glu",
    )(a, w, x)


def _mm_ffnup_kernel(h_ref, ssq_ref, g_ref, w_hbm, cw_ref, cb_ref, cache_ref,
                     o_ref, tp_ref, ts_ref,
                     wab_ref, waf_ref, sem_a, wgb_ref, wgf_ref, sem_g, carry_ref, *, stage):
    m = pl.program_id(1)
    _stage_weights(w_hbm, [(wab_ref, waf_ref, sem_a), (wgb_ref, wgf_ref, sem_g)], gain_ref=g_ref, **stage)

    @pl.when(m == 0)
    def _():
        carry_ref[...] = jnp.zeros_like(carry_ref)

    h = h_ref[...]
    r = _row_scale(ssq_ref)
    a = jnp.dot(h, wab_ref[...], preferred_element_type=F32) * r
    b = jnp.dot(h, wgb_ref[...], preferred_element_type=F32) * r
    r1 = pltpu.roll(a, 1, axis=0)
    r2 = pltpu.roll(a, 2, axis=0)
    cb = cb_ref[...]
    w0 = cw_ref[0:1, :]
    w1 = cw_ref[1:2, :]
    w2 = cw_ref[2:3, :]

    def gated(a_rows, prev1, prev2, b_rows):
        conv = cb + w0 * prev2 + w1 * prev1 + w2 * a_rows
        return ((conv * _sigmoid(conv)) * b_rows).astype(BF16)

    o_ref[...] = gated(a, r1, r2, b)

    row8 = lax.broadcasted_iota(jnp.int32, (SUBLANES, 1), 0)
    heads = [tuple(z[s * DEC_SEQ:s * DEC_SEQ + SUBLANES] for z in (a, r1, r2, b))
             for s in range(DEC_BATCH)]
    tails = [a[(s + 1) * DEC_SEQ - SUBLANES:(s + 1) * DEC_SEQ] for s in range(DEC_BATCH)]

    def redo_head(s, before1, before2):
        a8, r1_8, r2_8, b8 = heads[s]
        prev1 = jnp.where(row8 == 0, before1, r1_8)
        prev2 = jnp.where(row8 == 0, before2, jnp.where(row8 == 1, before1, r2_8))
        o_ref[s * DEC_SEQ:s * DEC_SEQ + SUBLANES, :] = gated(a8, prev1, prev2, b8)

    @pl.when(m < N_PROMPT_TILES)
    def _():
        c = carry_ref[...]
        redo_head(0, c[SUBLANES - 1:SUBLANES, :], c[SUBLANES - 2:SUBLANES - 1, :])
        carry_ref[...] = tails[DEC_BATCH - 1]

        @pl.when(m == N_PROMPT_TILES - 1)
        def _():
            tp_ref[...] = tails[DEC_BATCH - 1]

    @pl.when(m == N_PROMPT_TILES)
    def _():
        for s in range(DEC_BATCH):
            c = cache_ref[s]
            redo_head(s, c[1:2, :], c[0:1, :])
            ts_ref[s * SUBLANES:(s + 1) * SUBLANES, :] = tails[s]


def _mm_ffnup(norm, w, layer, conv_w, conv_b, cache):
    bn = PROJ_BN
    k = w.shape[1]
    nb = FFN_DIM // bn
    norm_args, norm_specs = _norm_operands(norm)
    return pl.pallas_call(
        functools.partial(_mm_ffnup_kernel, stage=dict(layer=layer, col0s=(0, nb), n_tiles=nb)),
        out_shape=(
            jax.ShapeDtypeStruct((M_TOTAL, FFN_DIM), BF16),
            jax.ShapeDtypeStruct((SUBLANES, FFN_DIM), F32),
            jax.ShapeDtypeStruct((DEC_BATCH * SUBLANES, FFN_DIM), F32),
        ),
        grid=(nb, N_ROW_TILES),
        in_specs=norm_specs + [
            W_SPEC,
            pl.BlockSpec((CONV_W, bn), lambda n, m: (0, n)),
            pl.BlockSpec((1, bn), lambda n, m: (0, n)),
            pl.BlockSpec((DEC_BATCH, CONV_W - 1, bn), lambda n, m: (0, 0, n)),
        ],
        out_specs=(
            pl.BlockSpec((BM, bn), lambda n, m: (m, n)),
            pl.BlockSpec((SUBLANES, bn), lambda n, m: (0, n)),
            pl.BlockSpec((DEC_BATCH * SUBLANES, bn), lambda n, m: (0, n)),
        ),
        scratch_shapes=_weight_scratch(k, bn) + _weight_scratch(k, bn) + [pltpu.VMEM((SUBLANES, bn), F32)],
        compiler_params=_params(2),
        name="mm_ffn_up",
    )(*norm_args, w, conv_w, conv_b.reshape(1, FFN_DIM), cache)


def _ret_tables(lg, T):
    i = lax.broadcasted_iota(jnp.int32, (T, T), 0)
    j = lax.broadcasted_iota(jnp.int32, (T, T), 1)
    dist = jnp.abs(i - j).astype(F32)
    shift = int(math.log2(CHUNK))
    visible = (j >> shift) <= (i >> shift)
    mask = jnp.where(visible, jnp.exp(dist * lg), 0.0)
    t = lax.broadcasted_iota(jnp.int32, (T, 1), 0).astype(F32)
    cross = jnp.exp((t + 1.0) * lg)
    k_dec = jnp.exp((T - 1.0 - t) * lg)
    decay = jnp.exp(float(T) * lg)
    return mask, cross, k_dec, decay


def _ret_block(q, k, v, g, state, tables, gn):
    mask, cross, k_dec, decay = tables
    scores = lax.dot_general(q, k, (((1,), (1,)), ((), ())), preferred_element_type=F32)
    scores = scores * mask
    out = jnp.dot(scores.astype(BF16), v, preferred_element_type=F32)
    out = out + jnp.dot(q, state.astype(BF16), preferred_element_type=F32) * cross

    kd = (k.astype(F32) * k_dec).astype(BF16)
    new_state = decay * state + lax.dot_general(
        kd, v, (((0,), (0,)), ((), ())), preferred_element_type=F32)

    mu = jnp.mean(out, axis=-1, keepdims=True)
    oc = out - mu
    var = jnp.mean(oc * oc, axis=-1, keepdims=True)
    normed = (oc * lax.rsqrt(var + EPS)) * gn
    g = g.astype(F32)
    return (normed * (g * _sigmoid(g))).astype(BF16), new_state


def _ret_prompt_kernel(lg_ref, q_ref, k_ref, v_ref, g_ref, gn_ref, o_ref, sl_ref, state_ref, mask_ref):
    blk = pl.program_id(1)
    lg = lg_ref[...][:, 0:1]

    @pl.when(blk == 0)
    def _():
        state_ref[...] = jnp.zeros_like(state_ref)
        mask_ref[...] = _ret_tables(lg, RET_T)[0]

    _, cross, k_dec, decay = _ret_tables(lg, RET_T)
    tables = (mask_ref[...], cross, k_dec, decay)
    gn = gn_ref[...]
    state = state_ref[...]
    for sb in range(RET_SUB):
        rows = slice(sb * RET_T, (sb + 1) * RET_T)
        o_ref[rows, :], state = _ret_block(q_ref[rows, :], k_ref[rows, :], v_ref[rows, :],
                                           g_ref[rows, :], state, tables, gn)
    state_ref[...] = state

    @pl.when(blk == pl.num_programs(1) - 1)
    def _():
        sl_ref[...] = state


def _ret_sample_kernel(lg_ref, q_ref, k_ref, v_ref, g_ref, gn_ref, s0_ref, *rest, layer):
    o_ref, sl_ref = rest[-2:]
    if len(rest) == 2:
        for other in range(N_RET):
            if other != layer:
                sl_ref[other] = jnp.zeros(sl_ref.shape[1:], F32)
        sl_ref = sl_ref.at[layer]
    lg = lg_ref[...][:, 0:1]
    tables = _ret_tables(lg, DEC_SEQ)
    gn = gn_ref[...]
    for s in range(DEC_BATCH):
        rows = slice(s * DEC_SEQ, (s + 1) * DEC_SEQ)
        o_ref[rows, :], sl_ref[s] = _ret_block(q_ref[rows, :], k_ref[rows, :], v_ref[rows, :],
                                               g_ref[rows, :], s0_ref[s], tables, gn)


def _retention(qk, vg, gn, s0_stack, layer, sample_states=None):
    log_g = np.log1p(-np.exp2(-5.0 - np.arange(RET_HEADS, dtype=np.float32))).astype(np.float32)
    lg = jnp.asarray(np.broadcast_to(log_g[:, None, None], (RET_HEADS, 1, LANES)).copy())
    gn = gn.reshape(1, RET_VD)
    k_off = RET_QK // RET_DK
    g_off = RET_VD // RET_DV
    rows_p = RET_T * RET_SUB
    o_p, s_p = pl.pallas_call(
        _ret_prompt_kernel,
        out_shape=(jax.ShapeDtypeStruct((M_PROMPT, RET_VD), BF16),
                   jax.ShapeDtypeStruct((1, RET_HEADS, RET_DK, RET_DV), F32)),
        grid=(RET_HEADS, M_PROMPT // rows_p),
        in_specs=[
            pl.BlockSpec((None, 1, LANES), lambda h, b: (h, 0, 0)),
            pl.BlockSpec((rows_p, RET_DK), lambda h, b: (b, h)),
            pl.BlockSpec((rows_p, RET_DK), lambda h, b: (b, k_off + h)),
            pl.BlockSpec((rows_p, RET_DV), lambda h, b: (b, h)),
            pl.BlockSpec((rows_p, RET_DV), lambda h, b: (b, g_off + h)),
            pl.BlockSpec((1, RET_DV), lambda h, b: (0, h)),
        ],
        out_specs=(pl.BlockSpec((rows_p, RET_DV), lambda h, b: (b, h)),
                   pl.BlockSpec((None, None, RET_DK, RET_DV), lambda h, b: (0, h, 0, 0))),
        scratch_shapes=[pltpu.VMEM((RET_DK, RET_DV), F32), pltpu.VMEM((RET_T, RET_T), F32)],
        compiler_params=_params(2),
        name="retention",
    )(lg, qk, qk, vg, vg, gn)
    sample_block = M_PROMPT // M_SAMPLE
    carried = [] if sample_states is None else [sample_states]
    n_in = 7
    if carried:
        state_spec = pl.BlockSpec((None, DEC_BATCH, None, RET_DK, RET_DV), lambda h: (layer, 0, h, 0, 0))
    else:
        state_spec = pl.BlockSpec((N_RET, DEC_BATCH, None, RET_DK, RET_DV), lambda h: (0, 0, h, 0, 0))
    o_s, s_s = pl.pallas_call(
        functools.partial(_ret_sample_kernel, layer=layer),
        out_shape=(jax.ShapeDtypeStruct((M_SAMPLE, RET_VD), BF16),
                   jax.ShapeDtypeStruct((N_RET, DEC_BATCH, RET_HEADS, RET_DK, RET_DV), F32)),
        grid=(RET_HEADS,),
        in_specs=[
            pl.BlockSpec((None, 1, LANES), lambda h: (h, 0, 0)),
            pl.BlockSpec((M_SAMPLE, RET_DK), lambda h: (sample_block, h)),
            pl.BlockSpec((M_SAMPLE, RET_DK), lambda h: (sample_block, k_off + h)),
            pl.BlockSpec((M_SAMPLE, RET_DV), lambda h: (sample_block, h)),
            pl.BlockSpec((M_SAMPLE, RET_DV), lambda h: (sample_block, g_off + h)),
            pl.BlockSpec((1, RET_DV), lambda h: (0, h)),
            pl.BlockSpec((None, DEC_BATCH, None, RET_DK, RET_DV), lambda h: (layer, 0, h, 0, 0)),
        ] + [pl.BlockSpec(memory_space=pl.ANY) for _ in carried],
        out_specs=(pl.BlockSpec((M_SAMPLE, RET_DV), lambda h: (0, h)), state_spec),
        input_output_aliases={n_in: 1} if carried else {},
        compiler_params=_params(1),
        name="retention_sample",
    )(lg, qk, qk, vg, vg, gn, s0_stack, *carried)
    return o_p, o_s, s_p, s_s


def _ssm_prep_kernel(ar_ref, ai_ref, ldt_ref, br_ref, bi_ref, cr_ref, ci_ref,
                     bc_ref, cc_ref, w_ref, a16r_ref, a16i_ref, vr_s, vi_s):
    ar = ar_ref[...]
    ai = ai_ref[...]
    dt = jnp.exp(ldt_ref[...])
    mag = jnp.exp(ar * dt)
    ang = ai * dt
    abr = mag * jnp.cos(ang)
    abi = mag * jnp.sin(ang)
    den = ar * ar + ai * ai
    nr = abr - 1.0
    ni = abi
    cfr = (nr * ar + ni * ai) / den
    cfi = (ni * ar - nr * ai) / den

    def b4(z):
        return z[:, None]

    br = br_ref[...]
    bi = bi_ref[...]
    bbr = b4(cfr) * br - b4(cfi) * bi
    bbi = b4(cfr) * bi + b4(cfi) * br
    cre = cr_ref[...]
    cim = ci_ref[...]

    powers = []
    pr = jnp.ones_like(ar)
    pi = jnp.zeros_like(ar)
    for _ in range(SSM_T + 1):
        powers.append((pr, pi))
        pr, pi = pr * abr - pi * abi, pr * abi + pi * abr
    a16r_ref[...] = powers[SSM_T][0]
    a16i_ref[...] = powers[SSM_T][1]

    for m in range(SSM_T + 1):
        pr, pi = powers[m]
        vr = b4(pr) * cre - b4(pi) * cim
        vi = b4(pr) * cim + b4(pi) * cre
        if m < SSM_T:
            vr_s[:, m] = vr
            vi_s[:, m] = vi
        if m >= 1:
            cc_ref[:, m - 1, :, :, 0:LANES] = vr.astype(BF16)
            cc_ref[:, m - 1, :, :, LANES:2 * LANES] = (-vi).astype(BF16)

    for t in range(SSM_T):
        pr, pi = powers[SSM_T - 1 - t]
        bc_ref[:, t, :, :, 0:LANES] = (b4(pr) * bbr - b4(pi) * bbi).astype(BF16)
        bc_ref[:, t, :, :, LANES:2 * LANES] = (b4(pr) * bbi + b4(pi) * bbr).astype(BF16)

    lane = lax.broadcasted_iota(jnp.int32, (SSM_PAIR_CH, SSM_PAIR_COLS), 1)
    contract_last = (((1,), (1,)), ((), ()))
    for q in range(SSM_PB):
        vr_all = vr_s[q].reshape(SSM_PAIR_COLS, LANES)
        vi_all = vi_s[q].reshape(SSM_PAIR_COLS, LANES)
        lags = (lax.dot_general(bbr[q].reshape(SSM_PAIR_CH, LANES), vr_all, contract_last,
                                precision=lax.Precision.HIGHEST, preferred_element_type=F32)
                - lax.dot_general(bbi[q].reshape(SSM_PAIR_CH, LANES), vi_all, contract_last,
                                  precision=lax.Precision.HIGHEST, preferred_element_type=F32))
        for t in range(SSM_T):
            shifted = lags if t == 0 else pltpu.roll(lags, SSM_PAIR_CH * t, axis=1)
            w_ref[q, t * SSM_PAIR_CH:(t + 1) * SSM_PAIR_CH, :] = jnp.where(
                lane >= SSM_PAIR_CH * t, shifted, 0.0).astype(BF16)


def _pair_pack(x):
    x = x.reshape(SSM_PAIRS, 2, SSM_GROUP, SSM_STATE)
    lo = jnp.pad(x[:, 0], ((0, 0), (0, 0), (0, SSM_STATE)))
    hi = jnp.pad(x[:, 1], ((0, 0), (0, 0), (SSM_STATE, 0)))
    return jnp.stack([lo, hi], axis=1)


def _ssm_prep(a_re, a_im, log_dt, b_re, b_im, c_re, c_im):
    assert 2 * SSM_PB * SSM_GROUP == LANES
    ar = a_re.reshape(SSM_PAIRS, 1, LANES)
    ai = a_im.reshape(SSM_PAIRS, 1, LANES)
    ldt = jnp.repeat(log_dt, SSM_STATE).reshape(SSM_PAIRS, 1, LANES)
    b2r = _pair_pack(jnp.transpose(b_re, (0, 2, 1)))
    b2i = _pair_pack(jnp.transpose(b_im, (0, 2, 1)))
    c2r = _pair_pack(c_re)
    c2i = _pair_pack(c_im)
    vec = pl.BlockSpec((SSM_PB, 1, LANES), lambda p: (p, 0, 0))
    mat = pl.BlockSpec((SSM_PB, 2, SSM_GROUP, LANES), lambda p: (p, 0, 0, 0))
    proj = pl.BlockSpec((SSM_PB, SSM_T, 2, SSM_GROUP, 2 * LANES), lambda p: (p, 0, 0, 0, 0))
    proj_shape = jax.ShapeDtypeStruct((SSM_PAIRS, SSM_T, 2, SSM_GROUP, 2 * LANES), BF16)
    bc, cc, w, a16r, a16i = pl.pallas_call(
        _ssm_prep_kernel,
        out_shape=(
            proj_shape, proj_shape,
            jax.ShapeDtypeStruct((SSM_PAIRS, SSM_PAIR_COLS, SSM_PAIR_COLS), BF16),
            jax.ShapeDtypeStruct((SSM_PAIRS, 1, LANES), F32),
            jax.ShapeDtypeStruct((SSM_PAIRS, 1, LANES), F32),
        ),
        grid=(N_LANE_TILES,),
        in_specs=[vec, vec, vec, mat, mat, mat, mat],
        out_specs=(proj, proj,
                   pl.BlockSpec((SSM_PB, SSM_PAIR_COLS, SSM_PAIR_COLS), lambda p: (p, 0, 0)),
                   vec, vec),
        scratch_shapes=[pltpu.VMEM((SSM_PB, SSM_T, 2, SSM_GROUP, LANES), F32)] * 2,
        compiler_params=_params(1),
        name="ssm_prep",
    )(ar, ai, ldt, b2r, b2i, c2r, c2i)
    bc = bc.reshape(SSM_PAIRS, SSM_PAIR_COLS, 2 * LANES)
    cc = cc.reshape(SSM_PAIRS, SSM_PAIR_COLS, 2 * LANES)
    row = SSM_GROUPS * SSM_STATE
    a16r = jnp.broadcast_to(a16r.reshape(1, row), (SSM_SEGS, row))
    a16i = jnp.broadcast_to(a16i.reshape(1, row), (SSM_SEGS, row))
    return bc, cc, w, a16r, a16i


SSM_CHUNK_COLS = SSM_T * LANES
SSM_STATE_COLS = SSM_PB * LANES
SSM_CHUNKS_P = M_PROMPT // SSM_T
SSM_SCAN_UNROLL = 8


def _ssm_main_kernel(x_ref, w_ref, bc_ref, cc_ref, ar_ref, ai_ref, h0r_ref, h0i_ref,
                     y_ref, fpr_ref, fpi_ref, fsr_ref, fsi_ref, d_s, s_s, t_s):
    pair_cols = [slice(q * SSM_PAIR_COLS, (q + 1) * SSM_PAIR_COLS) for q in range(SSM_PB)]

    for q in range(SSM_PB):
        d = jnp.dot(x_ref[:, pair_cols[q]], bc_ref[q], preferred_element_type=F32)
        d_s[q] = d[:, 0:LANES]
        d_s[SSM_PB + q] = d[:, LANES:2 * LANES]

    tiles = range(SSM_PB)
    ar = [ar_ref[:, k * LANES:(k + 1) * LANES] for k in tiles]
    ai = [ai_ref[:, k * LANES:(k + 1) * LANES] for k in tiles]

    def advance(sr, si, rows, record=None):
        out_r, out_i = [], []
        for k in tiles:
            if record is not None:
                ref, rec_rows = record
                ref[k, rec_rows, :] = sr[k]
                ref[SSM_PB + k, rec_rows, :] = si[k]
            dr = d_s[k, rows, :]
            di = d_s[SSM_PB + k, rows, :]
            out_r.append(ar[k] * sr[k] - ai[k] * si[k] + dr)
            out_i.append(ar[k] * si[k] + ai[k] * sr[k] + di)
        return out_r, out_i

    def prompt_rows(i):
        return pl.ds(i, SSM_SEGS, stride=SSM_STEPS_P)

    def prompt_pass(start_r, start_i, record):
        def body(i, carry):
            rec = (t_s, pl.ds(pl.multiple_of(i * SSM_SEGS, SSM_SEGS), SSM_SEGS)) if record else None
            out_r, out_i = advance(list(carry[0]), list(carry[1]), prompt_rows(i), rec)
            return tuple(out_r), tuple(out_i)
        out_r, out_i = lax.fori_loop(0, SSM_STEPS_P, body, (tuple(start_r), tuple(start_i)),
                                     unroll=SSM_SCAN_UNROLL)
        return list(out_r), list(out_i)

    zeros = [jnp.zeros((SSM_SEGS, LANES), F32) for _ in tiles]
    er, ei = prompt_pass(zeros, zeros, record=False)

    sr, si = [], []
    for k in tiles:
        pr, pi = ar[k][0:1], ai[k][0:1]
        for _ in range(int(math.log2(SSM_STEPS_P))):
            pr, pi = pr * pr - pi * pi, 2.0 * pr * pi
        cr = jnp.zeros((1, LANES), F32)
        ci = cr
        starts_r, starts_i = [], []
        for s in range(SSM_SEGS):
            starts_r.append(cr)
            starts_i.append(ci)
            cr, ci = (pr * cr - pi * ci + er[k][s:s + 1], pr * ci + pi * cr + ei[k][s:s + 1])
        cols = slice(k * LANES, (k + 1) * LANES)
        fpr_ref[:, cols] = jnp.broadcast_to(cr, (SSM_SEGS, LANES))
        fpi_ref[:, cols] = jnp.broadcast_to(ci, (SSM_SEGS, LANES))
        sr.append(jnp.concatenate(starts_r, axis=0))
        si.append(jnp.concatenate(starts_i, axis=0))

    prompt_pass(sr, si, record=True)
    for k in range(2 * SSM_PB):
        for s in range(SSM_SEGS):
            s_s[k, s * SSM_STEPS_P:(s + 1) * SSM_STEPS_P, :] = t_s[k, pl.ds(s, SSM_STEPS_P, stride=SSM_SEGS), :]
    sr = [h0r_ref[:, k * LANES:(k + 1) * LANES] for k in tiles]
    si = [h0i_ref[:, k * LANES:(k + 1) * LANES] for k in tiles]
    for i in range(SSM_STEPS_S):
        rows = pl.ds(SSM_CHUNKS_P + i, DEC_BATCH, stride=SSM_STEPS_S)
        sr, si = advance(sr, si, rows, (s_s, rows))
    for k in tiles:
        fsr_ref[:, k * LANES:(k + 1) * LANES] = sr[k]
        fsi_ref[:, k * LANES:(k + 1) * LANES] = si[k]

    contract_last = (((1,), (1,)), ((), ()))
    for q in range(SSM_PB):
        states = jnp.concatenate([s_s[q].astype(BF16), s_s[SSM_PB + q].astype(BF16)], axis=1)
        carried = lax.dot_general(states, cc_ref[q], contract_last, preferred_element_type=F32)
        local = jnp.dot(x_ref[:, pair_cols[q]], w_ref[q], preferred_element_type=F32)
        y_ref[:, pair_cols[q]] = local + carried


def _ssm_main(x_c, w, bc, cc, a16r, a16i, h0r, h0i):
    row = SSM_GROUPS * SSM_STATE
    st = pl.BlockSpec((SSM_SEGS, SSM_STATE_COLS), lambda j: (0, j))
    st_shape = jax.ShapeDtypeStruct((SSM_SEGS, row), F32)
    proj = pl.BlockSpec((SSM_PB, SSM_PAIR_COLS, 2 * LANES), lambda j: (j, 0, 0))
    chunk_rows = pl.BlockSpec((None, N_CHUNKS, SSM_CHUNK_COLS), lambda j: (j, 0, 0))
    return pl.pallas_call(
        _ssm_main_kernel,
        out_shape=(jax.ShapeDtypeStruct((N_LANE_TILES, N_CHUNKS, SSM_CHUNK_COLS), F32),
                   st_shape, st_shape, st_shape, st_shape),
        grid=(N_LANE_TILES,),
        in_specs=[
            chunk_rows,
            pl.BlockSpec((SSM_PB, SSM_PAIR_COLS, SSM_PAIR_COLS), lambda j: (j, 0, 0)),
            proj, proj, st, st, st, st,
        ],
        out_specs=(chunk_rows, st, st, st, st),
        scratch_shapes=[
            pltpu.VMEM((2 * SSM_PB, N_CHUNKS, LANES), F32),
            pltpu.VMEM((2 * SSM_PB, N_CHUNKS, LANES), F32),
            pltpu.VMEM((2 * SSM_PB, SSM_CHUNKS_P, LANES), F32),
        ],
        compiler_params=_params(1),
        name="ssm_main",
    )(x_c, w, bc, cc, a16r, a16i, h0r, h0i)


def _ssm_post_kernel(y_ref, x_ref, g_ref, d_ref, o_ref, y_s):
    block_of_lane = lax.broadcasted_iota(jnp.int32, (CHUNKS_PER_TILE, LANES), 1) // SSM_PAIR_CH
    for j in range(N_LANE_TILES):
        for tg in range(SSM_T // SSM_VEC_TOKENS):
            pieces = [y_ref[j, :, q * SSM_PAIR_COLS + tg * LANES:q * SSM_PAIR_COLS + (tg + 1) * LANES]
                      for q in range(SSM_PB)]
            for u, out in enumerate(_transpose_lane_blocks(pieces, block_of_lane)):
                y_s[j, pl.ds(tg * SSM_VEC_TOKENS + u, CHUNKS_PER_TILE, stride=SSM_T), :] = out
    x = x_ref[...]
    ms = jnp.mean(x * x, axis=-1, keepdims=True)
    h = (x * lax.rsqrt(ms + EPS)) * g_ref[...]
    for j in range(N_LANE_TILES):
        cols = slice(j * LANES, (j + 1) * LANES)
        y = y_s[j] + d_ref[:, cols] * h[:, cols]
        gelu = 0.5 * y * (1.0 + jnp.tanh(GELU_TANH_SCALE * (y + GELU_TANH_CUBIC * (y * y * y))))
        o_ref[:, cols] = gelu.astype(BF16)


def _ssm_post(y_c, x, g, d):
    spec = pl.BlockSpec((BM, D_MODEL), lambda m: (m, 0))
    vec = pl.BlockSpec((1, D_MODEL), lambda m: (0, 0))
    return pl.pallas_call(
        _ssm_post_kernel,
        out_shape=jax.ShapeDtypeStruct((M_TOTAL, D_MODEL), BF16),
        grid=(N_ROW_TILES,),
        in_specs=[pl.BlockSpec((N_LANE_TILES, CHUNKS_PER_TILE, SSM_CHUNK_COLS), lambda m: (0, m, 0)),
                  spec, vec, vec],
        out_specs=spec,
        scratch_shapes=[pltpu.VMEM((N_LANE_TILES, BM, LANES), F32)],
        compiler_params=_params(1),
        name="ssm_post",
    )(y_c, x, g.reshape(1, D_MODEL), d.reshape(1, D_MODEL))


def kernel(x_prompt, x_sample, state_ret, state_ssm_re, state_ssm_im, cache_conv, norm_mix, norm_ffn, norm_final, ret_w_in, ret_gn, ret_w_out, ssm_a_re, ssm_a_im, ssm_log_dt, ssm_b_re, ssm_b_im, ssm_c_re, ssm_c_im, ssm_d, ssm_w_glu, ffn_w_up, ffn_conv_w, ffn_conv_b, ffn_w_down):
    x = (x_prompt.reshape(M_PROMPT, D_MODEL), x_sample.reshape(M_SAMPLE, D_MODEL))
    xb, ssq, cos, sin = _prologue(*x)

    ret_p, ret_s = [], None
    re_p, im_p, re_s, im_s = [], [], [], []
    conv_p, conv_s = [], []
    state_row = SSM_GROUPS * SSM_STATE
    for i in range(DEPTH):
        j = i // 2
        if i % 2 == 0:
            qk, vg = _mm_retin((xb, ssq, norm_mix[i]), ret_w_in, j, cos, sin)
            o_p, o_s, s_p, ret_s = _retention(qk, vg, ret_gn[j], state_ret, j, ret_s)
            ret_p.append(s_p)
            x, xb, ssq = _mm_res((o_p, o_s), ret_w_out, j, x, emit=True)
        else:
            h_c = _rmsnorm_chunked(x, norm_mix[i])
            bc, cc, w_ssm, a16r, a16i = _ssm_prep(ssm_a_re[j], ssm_a_im[j], ssm_log_dt[j],
                                               ssm_b_re[j], ssm_b_im[j], ssm_c_re[j], ssm_c_im[j])
            y_c, fpr, fpi, fsr, fsi = _ssm_main(
                h_c, w_ssm, bc, cc, a16r, a16i,
                state_ssm_re[j].reshape(DEC_BATCH, state_row),
                state_ssm_im[j].reshape(DEC_BATCH, state_row))
            re_p.append(fpr[0:1].reshape(1, SSM_GROUPS, SSM_STATE))
            im_p.append(fpi[0:1].reshape(1, SSM_GROUPS, SSM_STATE))
            re_s.append(fsr.reshape(DEC_BATCH, SSM_GROUPS, SSM_STATE))
            im_s.append(fsi.reshape(DEC_BATCH, SSM_GROUPS, SSM_STATE))
            gl = _ssm_post(y_c, x, norm_mix[i], ssm_d[j])
            x, xb, ssq = _mm_glu(gl, ssm_w_glu, j, x)

        u, tail_p, tail_s = _mm_ffnup((xb, ssq, norm_ffn[i]), ffn_w_up, i, ffn_conv_w[i],
                                      ffn_conv_b[i], cache_conv[i])
        conv_p.append(tail_p[SUBLANES - (CONV_W - 1):].reshape(1, CONV_W - 1, FFN_DIM))
        conv_s.append(tail_s.reshape(DEC_BATCH, SUBLANES, FFN_DIM)[:, SUBLANES - (CONV_W - 1):])
        if i + 1 < DEPTH and (i + 1) % 2 == 0:
            x, xb, ssq = _mm_res(u, ffn_w_down, i, x, emit=True)
        else:
            x = _mm_res(u, ffn_w_down, i, x)

    y_prompt = _rmsnorm(x, norm_final, F32, 0, N_PROMPT_TILES).reshape(1, SEQ, D_MODEL)
    y_sample = _rmsnorm(x, norm_final, F32, N_PROMPT_TILES, 1).reshape(DEC_BATCH, DEC_SEQ, D_MODEL)
    return (y_prompt, y_sample, jnp.stack(ret_p), ret_s,
            jnp.stack(re_p), jnp.stack(im_p), jnp.stack(re_s), jnp.stack(im_s),
            jnp.stack(conv_p), jnp.stack(conv_s))
```

```python
import functools
import math

import numpy as np
import jax
import jax.numpy as jnp
from jax import lax
from jax.experimental import pallas as pl
from jax.experimental.pallas import tpu as pltpu

F32 = jnp.float32
BF16 = jnp.bfloat16

D_MODEL = 2048
SEQ = 8192
DEPTH = 4
DEC_BATCH = 8
DEC_SEQ = 64
PAST_LEN = 1024
CHUNK = 64
RET_HEADS = 8
RET_DK = D_MODEL // RET_HEADS
RET_DV = 2 * RET_DK
RET_QK = RET_HEADS * RET_DK
RET_VD = RET_HEADS * RET_DV
N_RET = (DEPTH + 1) // 2
ROPE_BASE = 10000.0
SSM_GROUP = 16
SSM_GROUPS = D_MODEL // SSM_GROUP
SSM_STATE = 64
FFN_DIM = 2 * D_MODEL
CONV_W = 3
EPS = 1e-6

M_PROMPT = SEQ
M_SAMPLE = DEC_BATCH * DEC_SEQ
M_TOTAL = M_PROMPT + M_SAMPLE

LANES = 128
SUBLANES = 8
V7X_VMEM_BYTES = 64 * 1024 * 1024
VMEM_LIMIT_BYTES = V7X_VMEM_BYTES // 8 * 7

BM = M_SAMPLE
N_ROW_TILES = M_TOTAL // BM
N_PROMPT_TILES = M_PROMPT // BM
PROJ_BN = 1024
RET_IN_BN = 2048

RET_T = 256
RET_SUB = 8

SSM_T = 16
SSM_PAIRS = SSM_GROUPS // 2
SSM_PB = 4
SSM_SEGS = 8
SSM_STEPS_P = M_PROMPT // SSM_T // SSM_SEGS
SSM_STEPS_S = DEC_SEQ // SSM_T
SSM_PAIR_CH = 2 * SSM_GROUP
SSM_PAIR_COLS = SSM_T * SSM_PAIR_CH
SSM_VEC_TOKENS = LANES // SSM_PAIR_CH


def _params(n_axes):
    return pltpu.CompilerParams(dimension_semantics=("arbitrary",) * n_axes,
                                vmem_limit_bytes=VMEM_LIMIT_BYTES)


GELU_TANH_SCALE = math.sqrt(2.0 / math.pi)
GELU_TANH_CUBIC = 0.044715


def _sigmoid(x):
    return 1.0 / (1.0 + jnp.exp(-x))


def _rope_tile(cos_ref, sin_ref):
    m = pl.program_id(0)
    half = RET_DK // 2
    freq = lax.broadcasted_iota(jnp.int32, (1, half), 1).astype(F32)
    inv = ROPE_BASE ** (-freq / half)
    r = m * BM + lax.broadcasted_iota(jnp.int32, (BM, half), 0)
    pos = jnp.where(r < M_PROMPT, r, PAST_LEN + ((r - M_PROMPT) & (DEC_SEQ - 1)))
    ang = pos.astype(F32) * inv
    cos_ref[...] = jnp.cos(ang)
    sin_ref[...] = jnp.sin(ang)


def _norm_kernel(x_ref, g_ref, o_ref):
    x = x_ref[...]
    ms = jnp.mean(x * x, axis=-1, keepdims=True)
    o_ref[...] = ((x * lax.rsqrt(ms + EPS)) * g_ref[...]).astype(o_ref.dtype)


def _rmsnorm(x, g, out_dtype, tile0=0, n_tiles=N_ROW_TILES):
    return pl.pallas_call(
        _norm_kernel,
        out_shape=jax.ShapeDtypeStruct((n_tiles * BM, D_MODEL), out_dtype),
        grid=(n_tiles,),
        in_specs=[pl.BlockSpec((BM, D_MODEL), lambda m: (tile0 + m, 0)),
                  pl.BlockSpec((1, D_MODEL), lambda m: (0, 0))],
        out_specs=pl.BlockSpec((BM, D_MODEL), lambda m: (m, 0)),
        compiler_params=_params(1),
        name="rmsnorm",
    )(x, g.reshape(1, D_MODEL))


CHUNKS_PER_TILE = BM // 16
N_CHUNKS = M_TOTAL // 16
N_LANE_TILES = D_MODEL // LANES


def _transpose_lane_blocks(v, block_of_lane):
    n = len(v)
    out = []
    for u in range(n):
        acc = None
        for q in range(n):
            shift = ((q - u) * SSM_PAIR_CH) % LANES
            moved = v[q] if shift == 0 else pltpu.roll(v[q], shift, axis=1)
            acc = moved if acc is None else jnp.where(block_of_lane == q, moved, acc)
        out.append(acc)
    return out


def _norm_chunked_kernel(x_ref, g_ref, o_ref, h_s):
    x = x_ref[...]
    ms = jnp.mean(x * x, axis=-1, keepdims=True)
    h = (x * lax.rsqrt(ms + EPS)) * g_ref[...]
    for j in range(N_LANE_TILES):
        h_s[j] = h[:, j * LANES:(j + 1) * LANES]
    block_of_lane = lax.broadcasted_iota(jnp.int32, (CHUNKS_PER_TILE, LANES), 1) // SSM_PAIR_CH
    for j in range(N_LANE_TILES):
        for tg in range(SSM_T // SSM_VEC_TOKENS):
            rows = [h_s[j, pl.ds(tg * SSM_VEC_TOKENS + u, CHUNKS_PER_TILE, stride=SSM_T), :]
                    for u in range(SSM_VEC_TOKENS)]
            for q, out in enumerate(_transpose_lane_blocks(rows, block_of_lane)):
                c0 = q * SSM_PAIR_COLS + tg * LANES
                o_ref[j, :, c0:c0 + LANES] = out.astype(BF16)


def _rmsnorm_chunked(x, g):
    return pl.pallas_call(
        _norm_chunked_kernel,
        out_shape=jax.ShapeDtypeStruct((N_LANE_TILES, N_CHUNKS, SSM_T * LANES), BF16),
        grid=(N_ROW_TILES,),
        in_specs=[pl.BlockSpec((BM, D_MODEL), lambda m: (m, 0)),
                  pl.BlockSpec((1, D_MODEL), lambda m: (0, 0))],
        out_specs=pl.BlockSpec((N_LANE_TILES, CHUNKS_PER_TILE, SSM_T * LANES), lambda m: (0, m, 0)),
        scratch_shapes=[pltpu.VMEM((N_LANE_TILES, BM, LANES), F32)],
        compiler_params=_params(1),
        name="rmsnorm_chunked",
    )(x, g.reshape(1, D_MODEL))


W_SPEC = pl.BlockSpec(memory_space=pl.ANY)


def _weight_scratch(k, bn):
    return [pltpu.VMEM((k, bn), BF16), pltpu.VMEM((k, bn), F32), pltpu.SemaphoreType.DMA(())]


def _stage_weights(w_hbm, staged, *, layer, col0s, n_tiles, gain_ref=None):
    n = pl.program_id(0)
    m = pl.program_id(1)

    def copies(tile):
        out = []
        for (_, wf_ref, sem), col0 in zip(staged, col0s):
            bn = wf_ref.shape[1]
            cols = pl.ds(pl.multiple_of((col0 + tile) * bn, bn), bn)
            out.append(pltpu.make_async_copy(w_hbm.at[layer, :, cols], wf_ref, sem))
        return out

    @pl.when((n == 0) & (m == 0))
    def _():
        for copy in copies(0):
            copy.start()

    @pl.when(m == 0)
    def _():
        for copy, (wb_ref, wf_ref, _) in zip(copies(n), staged):
            copy.wait()
            w = wf_ref[...]
            if gain_ref is not None:
                w = w * gain_ref[...]
            wb_ref[...] = w.astype(BF16)

    @pl.when((m == 1) & (n + 1 < n_tiles))
    def _():
        for copy in copies(n + 1):
            copy.start(priority=1)


def _emit_norm_inputs(x_new, xb_ref, ssq_ref):
    xb_ref[...] = x_new.astype(BF16)
    sq = x_new * x_new
    part = sq[:, 0:LANES]
    for t in range(1, x_new.shape[1] // LANES):
        part = part + sq[:, t * LANES:(t + 1) * LANES]
    ssq_ref[...] = part


def _row_scale(ssq_ref):
    total = jnp.sum(jnp.sum(ssq_ref[...], axis=0), axis=-1, keepdims=True)
    return lax.rsqrt(total * (1.0 / D_MODEL) + EPS)


def _norm_operands(norm):
    xb, ssq, g = norm
    args = [xb, ssq, g.reshape(D_MODEL, 1)]
    specs = [pl.BlockSpec((BM, D_MODEL), lambda n, m: (m, 0)),
             pl.BlockSpec((ssq.shape[0], BM, LANES), lambda n, m: (0, m, 0)),
             pl.BlockSpec((D_MODEL, 1), lambda n, m: (0, 0))]
    return args, specs


def _norm_outputs(n_out, bn):
    shapes = [jax.ShapeDtypeStruct((M_TOTAL, n_out), BF16),
              jax.ShapeDtypeStruct((n_out // bn, M_TOTAL, LANES), F32)]
    specs = [pl.BlockSpec((BM, bn), lambda n, m: (m, n)),
             pl.BlockSpec((None, BM, LANES), lambda n, m: (n, m, 0))]
    return shapes, specs


def _prologue_kernel(xp_ref, xs_ref, xb_ref, ssq_ref, cos_ref, sin_ref):
    x = jnp.where(pl.program_id(0) < N_PROMPT_TILES, xp_ref[...], xs_ref[...])
    _emit_norm_inputs(x, xb_ref, ssq_ref)
    _rope_tile(cos_ref, sin_ref)


def _prologue(x_prompt, x_sample):
    half = RET_DK // 2
    rope_spec = pl.BlockSpec((BM, half), lambda m: (m, 0))
    rope_shape = jax.ShapeDtypeStruct((M_TOTAL, half), F32)
    return pl.pallas_call(
        _prologue_kernel,
        out_shape=(jax.ShapeDtypeStruct((M_TOTAL, D_MODEL), BF16),
                   jax.ShapeDtypeStruct((1, M_TOTAL, LANES), F32), rope_shape, rope_shape),
        grid=(N_ROW_TILES,),
        in_specs=[pl.BlockSpec((BM, D_MODEL), lambda m: (jnp.minimum(m, N_PROMPT_TILES - 1), 0)),
                  pl.BlockSpec((BM, D_MODEL), lambda m: (0, 0))],
        out_specs=(pl.BlockSpec((BM, D_MODEL), lambda m: (m, 0)),
                   pl.BlockSpec((None, BM, LANES), lambda m: (0, m, 0)), rope_spec, rope_spec),
        compiler_params=_params(1),
        name="prologue",
    )(x_prompt, x_sample)


def _mm_qk_kernel(a_ref, ssq_ref, g_ref, w_hbm, cos_ref, sin_ref, o_ref, wb_ref, wf_ref, sem, *, stage):
    n = pl.program_id(0)
    n_q = RET_QK // RET_IN_BN
    _stage_weights(w_hbm, [(wb_ref, wf_ref, sem)], gain_ref=g_ref, **stage)

    acc = jnp.dot(a_ref[...], wb_ref[...], preferred_element_type=F32) * _row_scale(ssq_ref)
    scale = jnp.where(n >= n_q, RET_DK ** -0.5, 1.0).astype(F32)
    c = cos_ref[...]
    s = sin_ref[...]
    half = RET_DK // 2
    for hh in range(RET_IN_BN // RET_DK):
        lo = hh * RET_DK
        x1 = acc[:, lo:lo + half]
        x2 = acc[:, lo + half:lo + RET_DK]
        o_ref[:, lo:lo + half] = ((x1 * c - x2 * s) * scale).astype(BF16)
        o_ref[:, lo + half:lo + RET_DK] = ((x1 * s + x2 * c) * scale).astype(BF16)


def _mm_cast_kernel(a_ref, ssq_ref, g_ref, w_hbm, o_ref, wb_ref, wf_ref, sem, *, stage):
    _stage_weights(w_hbm, [(wb_ref, wf_ref, sem)], gain_ref=g_ref, **stage)

    acc = jnp.dot(a_ref[...], wb_ref[...], preferred_element_type=F32) * _row_scale(ssq_ref)
    o_ref[...] = acc.astype(BF16)


def _mm_retin(norm, w, layer, cos, sin):
    bn = RET_IN_BN
    _, k, n_out = w.shape
    half = RET_DK // 2
    n_qk = 2 * RET_QK // bn
    norm_args, norm_specs = _norm_operands(norm)
    out_spec = pl.BlockSpec((BM, bn), lambda n, m: (m, n))
    rope_spec = pl.BlockSpec((BM, half), lambda n, m: (m, 0))
    n_vg = n_out // bn - n_qk
    qk = pl.pallas_call(
        functools.partial(_mm_qk_kernel, stage=dict(layer=layer, col0s=(0,), n_tiles=n_qk)),
        out_shape=jax.ShapeDtypeStruct((M_TOTAL, 2 * RET_QK), BF16),
        grid=(n_qk, N_ROW_TILES),
        in_specs=norm_specs + [W_SPEC, rope_spec, rope_spec],
        out_specs=out_spec,
        scratch_shapes=_weight_scratch(k, bn),
        compiler_params=_params(2),
        name="mm_ret_qk",
    )(*norm_args, w, cos, sin)
    vg = pl.pallas_call(
        functools.partial(_mm_cast_kernel, stage=dict(layer=layer, col0s=(n_qk,), n_tiles=n_vg)),
        out_shape=jax.ShapeDtypeStruct((M_TOTAL, 2 * RET_VD), BF16),
        grid=(n_vg, N_ROW_TILES),
        in_specs=norm_specs + [W_SPEC],
        out_specs=out_spec,
        scratch_shapes=_weight_scratch(k, bn),
        compiler_params=_params(2),
        name="mm_ret_vg",
    )(*norm_args, w)
    return qk, vg


def _row_split_specs(cols, col_index):
    return [pl.BlockSpec((BM, cols), lambda n, m: (jnp.minimum(m, N_PROMPT_TILES - 1), col_index(n))),
            pl.BlockSpec((BM, cols), lambda n, m: (0, col_index(n)))]


def _mm_res_kernel(*refs, split_a, split_x, emit, stage):
    refs = list(refs)
    ap_ref = refs.pop(0)
    as_ref = refs.pop(0) if split_a else None
    w_hbm = refs.pop(0)
    xp_ref = refs.pop(0)
    xs_ref = refs.pop(0) if split_x else None
    o_ref = refs.pop(0)
    xb_ref, ssq_ref = (refs.pop(0), refs.pop(0)) if emit else (None, None)
    wb_ref, wf_ref, sem = refs
    m = pl.program_id(1)
    on_prompt = m < N_PROMPT_TILES
    _stage_weights(w_hbm, [(wb_ref, wf_ref, sem)], **stage)

    def body(a_ref):
        x = xp_ref[...]
        if split_x:
            x = jnp.where(on_prompt, x, xs_ref[...])
        x_new = x + jnp.dot(a_ref[...], wb_ref[...], preferred_element_type=F32)
        o_ref[...] = x_new
        if emit:
            _emit_norm_inputs(x_new, xb_ref, ssq_ref)

    if split_a:
        pl.when(on_prompt)(lambda: body(ap_ref))
        pl.when(jnp.logical_not(on_prompt))(lambda: body(as_ref))
    else:
        body(ap_ref)


def _mm_res(a, w, layer, x, emit=False):
    bn = PROJ_BN
    _, k, n_out = w.shape
    split_a = isinstance(a, tuple)
    split_x = isinstance(x, tuple)
    a_specs = _row_split_specs(k, lambda n: 0) if split_a else [pl.BlockSpec((BM, k), lambda n, m: (m, 0))]
    x_specs = _row_split_specs(bn, lambda n: n) if split_x else [pl.BlockSpec((BM, bn), lambda n, m: (m, n))]
    a_args = list(a) if split_a else [a]
    x_args = list(x) if split_x else [x]
    out_shapes = [jax.ShapeDtypeStruct((M_TOTAL, n_out), F32)]
    out_specs = [pl.BlockSpec((BM, bn), lambda n, m: (m, n))]
    if emit:
        shapes, specs = _norm_outputs(n_out, bn)
        out_shapes += shapes
        out_specs += specs
    out = pl.pallas_call(
        functools.partial(_mm_res_kernel, split_a=split_a, split_x=split_x, emit=emit,
                          stage=dict(layer=layer, col0s=(0,), n_tiles=n_out // bn)),
        out_shape=tuple(out_shapes),
        grid=(n_out // bn, N_ROW_TILES),
        in_specs=a_specs + [W_SPEC] + x_specs,
        out_specs=tuple(out_specs),
        scratch_shapes=_weight_scratch(k, bn),
        compiler_params=_params(2),
        name="mm_residual",
    )(*a_args, w, *x_args)
    return out if emit else out[0]


def _mm_glu_kernel(a_ref, w_hbm, x_ref, o_ref, xb_ref, ssq_ref,
                   wab_ref, waf_ref, sem_a, wgb_ref, wgf_ref, sem_g, *, stage):
    _stage_weights(w_hbm, [(wab_ref, waf_ref, sem_a), (wgb_ref, wgf_ref, sem_g)], **stage)

    a = a_ref[...]
    ga = jnp.dot(a, wab_ref[...], preferred_element_type=F32)
    gb = jnp.dot(a, wgb_ref[...], preferred_element_type=F32)
    x_new = x_ref[...] + ga * _sigmoid(gb)
    o_ref[...] = x_new
    _emit_norm_inputs(x_new, xb_ref, ssq_ref)


def _mm_glu(a, w, layer, x):
    bn = PROJ_BN
    _, k, n2 = w.shape
    n_out = n2 // 2
    nb = n_out // bn
    norm_shapes, norm_specs = _norm_outputs(n_out, bn)
    return pl.pallas_call(
        functools.partial(_mm_glu_kernel, stage=dict(layer=layer, col0s=(0, nb), n_tiles=nb)),
        out_shape=(jax.ShapeDtypeStruct((M_TOTAL, n_out), F32), *norm_shapes),
        grid=(nb, N_ROW_TILES),
        in_specs=[
            pl.BlockSpec((BM, k), lambda n, m: (m, 0)),
            W_SPEC,
            pl.BlockSpec((BM, bn), lambda n, m: (m, n)),
        ],
        out_specs=(pl.BlockSpec((BM, bn), lambda n, m: (m, n)), *norm_specs),
        scratch_shapes=_weight_scratch(k, bn) + _weight_scratch(k, bn),
        compiler_params=_params(2),
        name="mm_glu",
    )(a, w, x)


def _mm_ffnup_kernel(h_ref, ssq_ref, g_ref, w_hbm, cw_ref, cb_ref, cache_ref,
                     o_ref, tp_ref, ts_ref,
                     wab_ref, waf_ref, sem_a, wgb_ref, wgf_ref, sem_g, carry_ref, *, stage):
    m = pl.program_id(1)
    _stage_weights(w_hbm, [(wab_ref, waf_ref, sem_a), (wgb_ref, wgf_ref, sem_g)], gain_ref=g_ref, **stage)

    @pl.when(m == 0)
    def _():
        carry_ref[...] = jnp.zeros_like(carry_ref)

    h = h_ref[...]
    r = _row_scale(ssq_ref)
    a = jnp.dot(h, wab_ref[...], preferred_element_type=F32) * r
    b = jnp.dot(h, wgb_ref[...], preferred_element_type=F32) * r
    r1 = pltpu.roll(a, 1, axis=0)
    r2 = pltpu.roll(a, 2, axis=0)
    cb = cb_ref[...]
    w0 = cw_ref[0:1, :]
    w1 = cw_ref[1:2, :]
    w2 = cw_ref[2:3, :]

    def gated(a_rows, prev1, prev2, b_rows):
        conv = cb + w0 * prev2 + w1 * prev1 + w2 * a_rows
        return ((conv * _sigmoid(conv)) * b_rows).astype(BF16)

    o_ref[...] = gated(a, r1, r2, b)

    row8 = lax.broadcasted_iota(jnp.int32, (SUBLANES, 1), 0)
    heads = [tuple(z[s * DEC_SEQ:s * DEC_SEQ + SUBLANES] for z in (a, r1, r2, b))
             for s in range(DEC_BATCH)]
    tails = [a[(s + 1) * DEC_SEQ - SUBLANES:(s + 1) * DEC_SEQ] for s in range(DEC_BATCH)]

    def redo_head(s, before1, before2):
        a8, r1_8, r2_8, b8 = heads[s]
        prev1 = jnp.where(row8 == 0, before1, r1_8)
        prev2 = jnp.where(row8 == 0, before2, jnp.where(row8 == 1, before1, r2_8))
        o_ref[s * DEC_SEQ:s * DEC_SEQ + SUBLANES, :] = gated(a8, prev1, prev2, b8)

    @pl.when(m < N_PROMPT_TILES)
    def _():
        c = carry_ref[...]
        redo_head(0, c[SUBLANES - 1:SUBLANES, :], c[SUBLANES - 2:SUBLANES - 1, :])
        carry_ref[...] = tails[DEC_BATCH - 1]

        @pl.when(m == N_PROMPT_TILES - 1)
        def _():
            tp_ref[...] = tails[DEC_BATCH - 1]

    @pl.when(m == N_PROMPT_TILES)
    def _():
        for s in range(DEC_BATCH):
            c = cache_ref[s]
            redo_head(s, c[1:2, :], c[0:1, :])
            ts_ref[s * SUBLANES:(s + 1) * SUBLANES, :] = tails[s]


def _mm_ffnup(norm, w, layer, conv_w, conv_b, cache):
    bn = PROJ_BN
    k = w.shape[1]
    nb = FFN_DIM // bn
    norm_args, norm_specs = _norm_operands(norm)
    return pl.pallas_call(
        functools.partial(_mm_ffnup_kernel, stage=dict(layer=layer, col0s=(0, nb), n_tiles=nb)),
        out_shape=(
            jax.ShapeDtypeStruct((M_TOTAL, FFN_DIM), BF16),
            jax.ShapeDtypeStruct((SUBLANES, FFN_DIM), F32),
            jax.ShapeDtypeStruct((DEC_BATCH * SUBLANES, FFN_DIM), F32),
        ),
        grid=(nb, N_ROW_TILES),
        in_specs=norm_specs + [
            W_SPEC,
            pl.BlockSpec((CONV_W, bn), lambda n, m: (0, n)),
            pl.BlockSpec((1, bn), lambda n, m: (0, n)),
            pl.BlockSpec((DEC_BATCH, CONV_W - 1, bn), lambda n, m: (0, 0, n)),
        ],
        out_specs=(
            pl.BlockSpec((BM, bn), lambda n, m: (m, n)),
            pl.BlockSpec((SUBLANES, bn), lambda n, m: (0, n)),
            pl.BlockSpec((DEC_BATCH * SUBLANES, bn), lambda n, m: (0, n)),
        ),
        scratch_shapes=_weight_scratch(k, bn) + _weight_scratch(k, bn) + [pltpu.VMEM((SUBLANES, bn), F32)],
        compiler_params=_params(2),
        name="mm_ffn_up",
    )(*norm_args, w, conv_w, conv_b.reshape(1, FFN_DIM), cache)


def _ret_tables(lg, T):
    i = lax.broadcasted_iota(jnp.int32, (T, T), 0)
    j = lax.broadcasted_iota(jnp.int32, (T, T), 1)
    dist = jnp.abs(i - j).astype(F32)
    shift = int(math.log2(CHUNK))
    visible = (j >> shift) <= (i >> shift)
    mask = jnp.where(visible, jnp.exp(dist * lg), 0.0)
    t = lax.broadcasted_iota(jnp.int32, (T, 1), 0).astype(F32)
    cross = jnp.exp((t + 1.0) * lg)
    k_dec = jnp.exp((T - 1.0 - t) * lg)
    decay = jnp.exp(float(T) * lg)
    return mask, cross, k_dec, decay


def _ret_block(q, k, v, g, state, tables, gn):
    mask, cross, k_dec, decay = tables
    scores = lax.dot_general(q, k, (((1,), (1,)), ((), ())), preferred_element_type=F32)
    scores = scores * mask
    out = jnp.dot(scores.astype(BF16), v, preferred_element_type=F32)
    out = out + jnp.dot(q, state.astype(BF16), preferred_element_type=F32) * cross

    kd = (k.astype(F32) * k_dec).astype(BF16)
    new_state = decay * state + lax.dot_general(
        kd, v, (((0,), (0,)), ((), ())), preferred_element_type=F32)

    mu = jnp.mean(out, axis=-1, keepdims=True)
    oc = out - mu
    var = jnp.mean(oc * oc, axis=-1, keepdims=True)
    normed = (oc * lax.rsqrt(var + EPS)) * gn
    g = g.astype(F32)
    return (normed * (g * _sigmoid(g))).astype(BF16), new_state


def _ret_prompt_kernel(lg_ref, q_ref, k_ref, v_ref, g_ref, gn_ref, o_ref, sl_ref, state_ref, mask_ref):
    blk = pl.program_id(1)
    lg = lg_ref[...][:, 0:1]

    @pl.when(blk == 0)
    def _():
        state_ref[...] = jnp.zeros_like(state_ref)
        mask_ref[...] = _ret_tables(lg, RET_T)[0]

    _, cross, k_dec, decay = _ret_tables(lg, RET_T)
    tables = (mask_ref[...], cross, k_dec, decay)
    gn = gn_ref[...]
    state = state_ref[...]
    for sb in range(RET_SUB):
        rows = slice(sb * RET_T, (sb + 1) * RET_T)
        o_ref[rows, :], state = _ret_block(q_ref[rows, :], k_ref[rows, :], v_ref[rows, :],
                                           g_ref[rows, :], state, tables, gn)
    state_ref[...] = state

    @pl.when(blk == pl.num_programs(1) - 1)
    def _():
        sl_ref[...] = state


def _ret_sample_kernel(lg_ref, q_ref, k_ref, v_ref, g_ref, gn_ref, s0_ref, *rest, layer):
    o_ref, sl_ref = rest[-2:]
    if len(rest) == 2:
        for other in range(N_RET):
            if other != layer:
                sl_ref[other] = jnp.zeros(sl_ref.shape[1:], F32)
        sl_ref = sl_ref.at[layer]
    lg = lg_ref[...][:, 0:1]
    tables = _ret_tables(lg, DEC_SEQ)
    gn = gn_ref[...]
    for s in range(DEC_BATCH):
        rows = slice(s * DEC_SEQ, (s + 1) * DEC_SEQ)
        o_ref[rows, :], sl_ref[s] = _ret_block(q_ref[rows, :], k_ref[rows, :], v_ref[rows, :],
                                               g_ref[rows, :], s0_ref[s], tables, gn)


def _retention(qk, vg, gn, s0_stack, layer, sample_states=None):
    log_g = np.log1p(-np.exp2(-5.0 - np.arange(RET_HEADS, dtype=np.float32))).astype(np.float32)
    lg = jnp.asarray(np.broadcast_to(log_g[:, None, None], (RET_HEADS, 1, LANES)).copy())
    gn = gn.reshape(1, RET_VD)
    k_off = RET_QK // RET_DK
    g_off = RET_VD // RET_DV
    rows_p = RET_T * RET_SUB
    o_p, s_p = pl.pallas_call(
        _ret_prompt_kernel,
        out_shape=(jax.ShapeDtypeStruct((M_PROMPT, RET_VD), BF16),
                   jax.ShapeDtypeStruct((1, RET_HEADS, RET_DK, RET_DV), F32)),
        grid=(RET_HEADS, M_PROMPT // rows_p),
        in_specs=[
            pl.BlockSpec((None, 1, LANES), lambda h, b: (h, 0, 0)),
            pl.BlockSpec((rows_p, RET_DK), lambda h, b: (b, h)),
            pl.BlockSpec((rows_p, RET_DK), lambda h, b: (b, k_off + h)),
            pl.BlockSpec((rows_p, RET_DV), lambda h, b: (b, h)),
            pl.BlockSpec((rows_p, RET_DV), lambda h, b: (b, g_off + h)),
            pl.BlockSpec((1, RET_DV), lambda h, b: (0, h)),
        ],
        out_specs=(pl.BlockSpec((rows_p, RET_DV), lambda h, b: (b, h)),
                   pl.BlockSpec((None, None, RET_DK, RET_DV), lambda h, b: (0, h, 0, 0))),
        scratch_shapes=[pltpu.VMEM((RET_DK, RET_DV), F32), pltpu.VMEM((RET_T, RET_T), F32)],
        compiler_params=_params(2),
        name="retention",
    )(lg, qk, qk, vg, vg, gn)
    sample_block = M_PROMPT // M_SAMPLE
    carried = [] if sample_states is None else [sample_states]
    n_in = 7
    if carried:
        state_spec = pl.BlockSpec((None, DEC_BATCH, None, RET_DK, RET_DV), lambda h: (layer, 0, h, 0, 0))
    else:
        state_spec = pl.BlockSpec((N_RET, DEC_BATCH, None, RET_DK, RET_DV), lambda h: (0, 0, h, 0, 0))
    o_s, s_s = pl.pallas_call(
        functools.partial(_ret_sample_kernel, layer=layer),
        out_shape=(jax.ShapeDtypeStruct((M_SAMPLE, RET_VD), BF16),
                   jax.ShapeDtypeStruct((N_RET, DEC_BATCH, RET_HEADS, RET_DK, RET_DV), F32)),
        grid=(RET_HEADS,),
        in_specs=[
            pl.BlockSpec((None, 1, LANES), lambda h: (h, 0, 0)),
            pl.BlockSpec((M_SAMPLE, RET_DK), lambda h: (sample_block, h)),
            pl.BlockSpec((M_SAMPLE, RET_DK), lambda h: (sample_block, k_off + h)),
            pl.BlockSpec((M_SAMPLE, RET_DV), lambda h: (sample_block, h)),
            pl.BlockSpec((M_SAMPLE, RET_DV), lambda h: (sample_block, g_off + h)),
            pl.BlockSpec((1, RET_DV), lambda h: (0, h)),
            pl.BlockSpec((None, DEC_BATCH, None, RET_DK, RET_DV), lambda h: (layer, 0, h, 0, 0)),
        ] + [pl.BlockSpec(memory_space=pl.ANY) for _ in carried],
        out_specs=(pl.BlockSpec((M_SAMPLE, RET_DV), lambda h: (0, h)), state_spec),
        input_output_aliases={n_in: 1} if carried else {},
        compiler_params=_params(1),
        name="retention_sample",
    )(lg, qk, qk, vg, vg, gn, s0_stack, *carried)
    return o_p, o_s, s_p, s_s


def _ssm_prep_kernel(ar_ref, ai_ref, ldt_ref, br_ref, bi_ref, cr_ref, ci_ref,
                     bc_ref, cc_ref, w_ref, a16r_ref, a16i_ref, vr_s, vi_s):
    ar = ar_ref[...]
    ai = ai_ref[...]
    dt = jnp.exp(ldt_ref[...])
    mag = jnp.exp(ar * dt)
    ang = ai * dt
    abr = mag * jnp.cos(ang)
    abi = mag * jnp.sin(ang)
    den = ar * ar + ai * ai
    nr = abr - 1.0
    ni = abi
    cfr = (nr * ar + ni * ai) / den
    cfi = (ni * ar - nr * ai) / den

    def b4(z):
        return z[:, None]

    br = br_ref[...]
    bi = bi_ref[...]
    bbr = b4(cfr) * br - b4(cfi) * bi
    bbi = b4(cfr) * bi + b4(cfi) * br
    cre = cr_ref[...]
    cim = ci_ref[...]

    powers = []
    pr = jnp.ones_like(ar)
    pi = jnp.zeros_like(ar)
    for _ in range(SSM_T + 1):
        powers.append((pr, pi))
        pr, pi = pr * abr - pi * abi, pr * abi + pi * abr
    a16r_ref[...] = powers[SSM_T][0]
    a16i_ref[...] = powers[SSM_T][1]

    for m in range(SSM_T + 1):
        pr, pi = powers[m]
        vr = b4(pr) * cre - b4(pi) * cim
        vi = b4(pr) * cim + b4(pi) * cre
        if m < SSM_T:
            vr_s[:, m] = vr
            vi_s[:, m] = vi
        if m >= 1:
            cc_ref[:, m - 1, :, :, 0:LANES] = vr.astype(BF16)
            cc_ref[:, m - 1, :, :, LANES:2 * LANES] = (-vi).astype(BF16)

    for t in range(SSM_T):
        pr, pi = powers[SSM_T - 1 - t]
        bc_ref[:, t, :, :, 0:LANES] = (b4(pr) * bbr - b4(pi) * bbi).astype(BF16)
        bc_ref[:, t, :, :, LANES:2 * LANES] = (b4(pr) * bbi + b4(pi) * bbr).astype(BF16)

    lane = lax.broadcasted_iota(jnp.int32, (SSM_PAIR_CH, SSM_PAIR_COLS), 1)
    contract_last = (((1,), (1,)), ((), ()))
    for q in range(SSM_PB):
        vr_all = vr_s[q].reshape(SSM_PAIR_COLS, LANES)
        vi_all = vi_s[q].reshape(SSM_PAIR_COLS, LANES)
        lags = (lax.dot_general(bbr[q].reshape(SSM_PAIR_CH, LANES), vr_all, contract_last,
                                precision=lax.Precision.HIGHEST, preferred_element_type=F32)
                - lax.dot_general(bbi[q].reshape(SSM_PAIR_CH, LANES), vi_all, contract_last,
                                  precision=lax.Precision.HIGHEST, preferred_element_type=F32))
        for t in range(SSM_T):
            shifted = lags if t == 0 else pltpu.roll(lags, SSM_PAIR_CH * t, axis=1)
            w_ref[q, t * SSM_PAIR_CH:(t + 1) * SSM_PAIR_CH, :] = jnp.where(
                lane >= SSM_PAIR_CH * t, shifted, 0.0).astype(BF16)


def _pair_pack(x):
    x = x.reshape(SSM_PAIRS, 2, SSM_GROUP, SSM_STATE)
    lo = jnp.pad(x[:, 0], ((0, 0), (0, 0), (0, SSM_STATE)))
    hi = jnp.pad(x[:, 1], ((0, 0), (0, 0), (SSM_STATE, 0)))
    return jnp.stack([lo, hi], axis=1)


def _ssm_prep(a_re, a_im, log_dt, b_re, b_im, c_re, c_im):
    assert 2 * SSM_PB * SSM_GROUP == LANES
    ar = a_re.reshape(SSM_PAIRS, 1, LANES)
    ai = a_im.reshape(SSM_PAIRS, 1, LANES)
    ldt = jnp.repeat(log_dt, SSM_STATE).reshape(SSM_PAIRS, 1, LANES)
    b2r = _pair_pack(jnp.transpose(b_re, (0, 2, 1)))
    b2i = _pair_pack(jnp.transpose(b_im, (0, 2, 1)))
    c2r = _pair_pack(c_re)
    c2i = _pair_pack(c_im)
    vec = pl.BlockSpec((SSM_PB, 1, LANES), lambda p: (p, 0, 0))
    mat = pl.BlockSpec((SSM_PB, 2, SSM_GROUP, LANES), lambda p: (p, 0, 0, 0))
    proj = pl.BlockSpec((SSM_PB, SSM_T, 2, SSM_GROUP, 2 * LANES), lambda p: (p, 0, 0, 0, 0))
    proj_shape = jax.ShapeDtypeStruct((SSM_PAIRS, SSM_T, 2, SSM_GROUP, 2 * LANES), BF16)
    bc, cc, w, a16r, a16i = pl.pallas_call(
        _ssm_prep_kernel,
        out_shape=(
            proj_shape, proj_shape,
            jax.ShapeDtypeStruct((SSM_PAIRS, SSM_PAIR_COLS, SSM_PAIR_COLS), BF16),
            jax.ShapeDtypeStruct((SSM_PAIRS, 1, LANES), F32),
            jax.ShapeDtypeStruct((SSM_PAIRS, 1, LANES), F32),
        ),
        grid=(N_LANE_TILES,),
        in_specs=[vec, vec, vec, mat, mat, mat, mat],
        out_specs=(proj, proj,
                   pl.BlockSpec((SSM_PB, SSM_PAIR_COLS, SSM_PAIR_COLS), lambda p: (p, 0, 0)),
                   vec, vec),
        scratch_shapes=[pltpu.VMEM((SSM_PB, SSM_T, 2, SSM_GROUP, LANES), F32)] * 2,
        compiler_params=_params(1),
        name="ssm_prep",
    )(ar, ai, ldt, b2r, b2i, c2r, c2i)
    bc = bc.reshape(SSM_PAIRS, SSM_PAIR_COLS, 2 * LANES)
    cc = cc.reshape(SSM_PAIRS, SSM_PAIR_COLS, 2 * LANES)
    row = SSM_GROUPS * SSM_STATE
    a16r = jnp.broadcast_to(a16r.reshape(1, row), (SSM_SEGS, row))
    a16i = jnp.broadcast_to(a16i.reshape(1, row), (SSM_SEGS, row))
    return bc, cc, w, a16r, a16i


SSM_CHUNK_COLS = SSM_T * LANES
SSM_STATE_COLS = SSM_PB * LANES
SSM_CHUNKS_P = M_PROMPT // SSM_T
SSM_SCAN_UNROLL = 8


def _ssm_main_kernel(x_ref, w_ref, bc_ref, cc_ref, ar_ref, ai_ref, h0r_ref, h0i_ref,
                     y_ref, fpr_ref, fpi_ref, fsr_ref, fsi_ref, d_s, s_s, t_s):
    pair_cols = [slice(q * SSM_PAIR_COLS, (q + 1) * SSM_PAIR_COLS) for q in range(SSM_PB)]

    for q in range(SSM_PB):
        d = jnp.dot(x_ref[:, pair_cols[q]], bc_ref[q], preferred_element_type=F32)
        d_s[q] = d[:, 0:LANES]
        d_s[SSM_PB + q] = d[:, LANES:2 * LANES]

    tiles = range(SSM_PB)
    ar = [ar_ref[:, k * LANES:(k + 1) * LANES] for k in tiles]
    ai = [ai_ref[:, k * LANES:(k + 1) * LANES] for k in tiles]

    def advance(sr, si, rows, record=None):
        out_r, out_i = [], []
        for k in tiles:
            if record is not None:
                ref, rec_rows = record
                ref[k, rec_rows, :] = sr[k]
                ref[SSM_PB + k, rec_rows, :] = si[k]
            dr = d_s[k, rows, :]
            di = d_s[SSM_PB + k, rows, :]
            out_r.append(ar[k] * sr[k] - ai[k] * si[k] + dr)
            out_i.append(ar[k] * si[k] + ai[k] * sr[k] + di)
        return out_r, out_i

    def prompt_rows(i):
        return pl.ds(i, SSM_SEGS, stride=SSM_STEPS_P)

    def segment_ends(k):
        a1r, a1i = ar[k][0:1], ai[k][0:1]
        p8r, p8i = ar[k], ai[k]
        for _ in range(int(math.log2(SUBLANES))):
            p8r, p8i = p8r * p8r - p8i * p8i, 2.0 * p8r * p8i
        wr, wi = [jnp.ones((1, LANES), F32)], [jnp.zeros((1, LANES), F32)]
        for _ in range(SUBLANES - 1):
            wr, wi = wr + [wr[-1] * a1r - wi[-1] * a1i], wi + [wr[-1] * a1i + wi[-1] * a1r]
        row_r = jnp.concatenate(wr[::-1], axis=0)
        row_i = jnp.concatenate(wi[::-1], axis=0)
        ends_r, ends_i = [], []
        for s in range(SSM_SEGS):
            acc_r = acc_i = None
            for b in range(SSM_STEPS_P // SUBLANES):
                rows = slice(s * SSM_STEPS_P + b * SUBLANES, s * SSM_STEPS_P + (b + 1) * SUBLANES)
                dr = d_s[k, rows, :]
                di = d_s[SSM_PB + k, rows, :]
                if acc_r is None:
                    acc_r, acc_i = dr, di
                else:
                    acc_r, acc_i = p8r * acc_r - p8i * acc_i + dr, p8r * acc_i + p8i * acc_r + di
            ends_r.append(jnp.sum(row_r * acc_r - row_i * acc_i, axis=0, keepdims=True))
            ends_i.append(jnp.sum(row_r * acc_i + row_i * acc_r, axis=0, keepdims=True))
        return jnp.concatenate(ends_r, axis=0), jnp.concatenate(ends_i, axis=0)

    er, ei = zip(*[segment_ends(k) for k in tiles])

    def prompt_pass(start_r, start_i):
        def body(i, carry):
            rec = (t_s, pl.ds(pl.multiple_of(i * SSM_SEGS, SSM_SEGS), SSM_SEGS))
            out_r, out_i = advance(list(carry[0]), list(carry[1]), prompt_rows(i), rec)
            return tuple(out_r), tuple(out_i)
        lax.fori_loop(0, SSM_STEPS_P, body, (tuple(start_r), tuple(start_i)), unroll=SSM_SCAN_UNROLL)

    sr, si = [], []
    for k in tiles:
        pr, pi = ar[k][0:1], ai[k][0:1]
        for _ in range(int(math.log2(SSM_STEPS_P))):
            pr, pi = pr * pr - pi * pi, 2.0 * pr * pi
        cr = jnp.zeros((1, LANES), F32)
        ci = cr
        starts_r, starts_i = [], []
        for s in range(SSM_SEGS):
            starts_r.append(cr)
            starts_i.append(ci)
            cr, ci = (pr * cr - pi * ci + er[k][s:s + 1], pr * ci + pi * cr + ei[k][s:s + 1])
        cols = slice(k * LANES, (k + 1) * LANES)
        fpr_ref[:, cols] = jnp.broadcast_to(cr, (SSM_SEGS, LANES))
        fpi_ref[:, cols] = jnp.broadcast_to(ci, (SSM_SEGS, LANES))
        sr.append(jnp.concatenate(starts_r, axis=0))
        si.append(jnp.concatenate(starts_i, axis=0))

    prompt_pass(sr, si)
    for k in range(2 * SSM_PB):
        for s in range(SSM_SEGS):
            s_s[k, s * SSM_STEPS_P:(s + 1) * SSM_STEPS_P, :] = t_s[k, pl.ds(s, SSM_STEPS_P, stride=SSM_SEGS), :]
    sr = [h0r_ref[:, k * LANES:(k + 1) * LANES] for k in tiles]
    si = [h0i_ref[:, k * LANES:(k + 1) * LANES] for k in tiles]
    for i in range(SSM_STEPS_S):
        rows = pl.ds(SSM_CHUNKS_P + i, DEC_BATCH, stride=SSM_STEPS_S)
        sr, si = advance(sr, si, rows, (s_s, rows))
    for k in tiles:
        fsr_ref[:, k * LANES:(k + 1) * LANES] = sr[k]
        fsi_ref[:, k * LANES:(k + 1) * LANES] = si[k]

    contract_last = (((1,), (1,)), ((), ()))
    for q in range(SSM_PB):
        states = jnp.concatenate([s_s[q].astype(BF16), s_s[SSM_PB + q].astype(BF16)], axis=1)
        carried = lax.dot_general(states, cc_ref[q], contract_last, preferred_element_type=F32)
        local = jnp.dot(x_ref[:, pair_cols[q]], w_ref[q], preferred_element_type=F32)
        y_ref[:, pair_cols[q]] = local + carried


def _ssm_main(x_c, w, bc, cc, a16r, a16i, h0r, h0i):
    row = SSM_GROUPS * SSM_STATE
    st = pl.BlockSpec((SSM_SEGS, SSM_STATE_COLS), lambda j: (0, j))
    st_shape = jax.ShapeDtypeStruct((SSM_SEGS, row), F32)
    proj = pl.BlockSpec((SSM_PB, SSM_PAIR_COLS, 2 * LANES), lambda j: (j, 0, 0))
    chunk_rows = pl.BlockSpec((None, N_CHUNKS, SSM_CHUNK_COLS), lambda j: (j, 0, 0))
    return pl.pallas_call(
        _ssm_main_kernel,
        out_shape=(jax.ShapeDtypeStruct((N_LANE_TILES, N_CHUNKS, SSM_CHUNK_COLS), F32),
                   st_shape, st_shape, st_shape, st_shape),
        grid=(N_LANE_TILES,),
        in_specs=[
            chunk_rows,
            pl.BlockSpec((SSM_PB, SSM_PAIR_COLS, SSM_PAIR_COLS), lambda j: (j, 0, 0)),
            proj, proj, st, st, st, st,
        ],
        out_specs=(chunk_rows, st, st, st, st),
        scratch_shapes=[
            pltpu.VMEM((2 * SSM_PB, N_CHUNKS, LANES), F32),
            pltpu.VMEM((2 * SSM_PB, N_CHUNKS, LANES), F32),
            pltpu.VMEM((2 * SSM_PB, SSM_CHUNKS_P, LANES), F32),
        ],
        compiler_params=_params(1),
        name="ssm_main",
    )(x_c, w, bc, cc, a16r, a16i, h0r, h0i)


def _ssm_post_kernel(y_ref, x_ref, g_ref, d_ref, o_ref, y_s):
    block_of_lane = lax.broadcasted_iota(jnp.int32, (CHUNKS_PER_TILE, LANES), 1) // SSM_PAIR_CH
    for j in range(N_LANE_TILES):
        for tg in range(SSM_T // SSM_VEC_TOKENS):
            pieces = [y_ref[j, :, q * SSM_PAIR_COLS + tg * LANES:q * SSM_PAIR_COLS + (tg + 1) * LANES]
                      for q in range(SSM_PB)]
            for u, out in enumerate(_transpose_lane_blocks(pieces, block_of_lane)):
                y_s[j, pl.ds(tg * SSM_VEC_TOKENS + u, CHUNKS_PER_TILE, stride=SSM_T), :] = out
    x = x_ref[...]
    ms = jnp.mean(x * x, axis=-1, keepdims=True)
    h = (x * lax.rsqrt(ms + EPS)) * g_ref[...]
    for j in range(N_LANE_TILES):
        cols = slice(j * LANES, (j + 1) * LANES)
        y = y_s[j] + d_ref[:, cols] * h[:, cols]
        gelu = 0.5 * y * (1.0 + jnp.tanh(GELU_TANH_SCALE * (y + GELU_TANH_CUBIC * (y * y * y))))
        o_ref[:, cols] = gelu.astype(BF16)


def _ssm_post(y_c, x, g, d):
    spec = pl.BlockSpec((BM, D_MODEL), lambda m: (m, 0))
    vec = pl.BlockSpec((1, D_MODEL), lambda m: (0, 0))
    return pl.pallas_call(
        _ssm_post_kernel,
        out_shape=jax.ShapeDtypeStruct((M_TOTAL, D_MODEL), BF16),
        grid=(N_ROW_TILES,),
        in_specs=[pl.BlockSpec((N_LANE_TILES, CHUNKS_PER_TILE, SSM_CHUNK_COLS), lambda m: (0, m, 0)),
                  spec, vec, vec],
        out_specs=spec,
        scratch_shapes=[pltpu.VMEM((N_LANE_TILES, BM, LANES), F32)],
        compiler_params=_params(1),
        name="ssm_post",
    )(y_c, x, g.reshape(1, D_MODEL), d.reshape(1, D_MODEL))


def kernel(x_prompt, x_sample, state_ret, state_ssm_re, state_ssm_im, cache_conv, norm_mix, norm_ffn, norm_final, ret_w_in, ret_gn, ret_w_out, ssm_a_re, ssm_a_im, ssm_log_dt, ssm_b_re, ssm_b_im, ssm_c_re, ssm_c_im, ssm_d, ssm_w_glu, ffn_w_up, ffn_conv_w, ffn_conv_b, ffn_w_down):
    x = (x_prompt.reshape(M_PROMPT, D_MODEL), x_sample.reshape(M_SAMPLE, D_MODEL))
    xb, ssq, cos, sin = _prologue(*x)

    ret_p, ret_s = [], None
    re_p, im_p, re_s, im_s = [], [], [], []
    conv_p, conv_s = [], []
    state_row = SSM_GROUPS * SSM_STATE
    for i in range(DEPTH):
        j = i // 2
        if i % 2 == 0:
            qk, vg = _mm_retin((xb, ssq, norm_mix[i]), ret_w_in, j, cos, sin)
            o_p, o_s, s_p, ret_s = _retention(qk, vg, ret_gn[j], state_ret, j, ret_s)
            ret_p.append(s_p)
            x, xb, ssq = _mm_res((o_p, o_s), ret_w_out, j, x, emit=True)
        else:
            h_c = _rmsnorm_chunked(x, norm_mix[i])
            bc, cc, w_ssm, a16r, a16i = _ssm_prep(ssm_a_re[j], ssm_a_im[j], ssm_log_dt[j],
                                               ssm_b_re[j], ssm_b_im[j], ssm_c_re[j], ssm_c_im[j])
            y_c, fpr, fpi, fsr, fsi = _ssm_main(
                h_c, w_ssm, bc, cc, a16r, a16i,
                state_ssm_re[j].reshape(DEC_BATCH, state_row),
                state_ssm_im[j].reshape(DEC_BATCH, state_row))
            re_p.append(fpr[0:1].reshape(1, SSM_GROUPS, SSM_STATE))
            im_p.append(fpi[0:1].reshape(1, SSM_GROUPS, SSM_STATE))
            re_s.append(fsr.reshape(DEC_BATCH, SSM_GROUPS, SSM_STATE))
            im_s.append(fsi.reshape(DEC_BATCH, SSM_GROUPS, SSM_STATE))
            gl = _ssm_post(y_c, x, norm_mix[i], ssm_d[j])
            x, xb, ssq = _mm_glu(gl, ssm_w_glu, j, x)

        u, tail_p, tail_s = _mm_ffnup((xb, ssq, norm_ffn[i]), ffn_w_up, i, ffn_conv_w[i],
                                      ffn_conv_b[i], cache_conv[i])
        conv_p.append(tail_p[SUBLANES - (CONV_W - 1):].reshape(1, CONV_W - 1, FFN_DIM))
        conv_s.append(tail_s.reshape(DEC_BATCH, SUBLANES, FFN_DIM)[:, SUBLANES - (CONV_W - 1):])
        if i + 1 < DEPTH and (i + 1) % 2 == 0:
            x, xb, ssq = _mm_res(u, ffn_w_down, i, x, emit=True)
        else:
            x = _mm_res(u, ffn_w_down, i, x)

    y_prompt = _rmsnorm(x, norm_final, F32, 0, N_PROMPT_TILES).reshape(1, SEQ, D_MODEL)
    y_sample = _rmsnorm(x, norm_final, F32, N_PROMPT_TILES, 1).reshape(DEC_BATCH, DEC_SEQ, D_MODEL)
    return (y_prompt, y_sample, jnp.stack(ret_p), ret_s,
            jnp.stack(re_p), jnp.stack(im_p), jnp.stack(re_s), jnp.stack(im_s),
            jnp.stack(conv_p), jnp.stack(conv_s))
```

```python
import functools
import math

import numpy as np
import jax
import jax.numpy as jnp
from jax import lax
from jax.experimental import pallas as pl
from jax.experimental.pallas import tpu as pltpu

F32 = jnp.float32
BF16 = jnp.bfloat16

D_MODEL = 2048
SEQ = 8192
DEPTH = 4
DEC_BATCH = 8
DEC_SEQ = 64
PAST_LEN = 1024
CHUNK = 64
RET_HEADS = 8
RET_DK = D_MODEL // RET_HEADS
RET_DV = 2 * RET_DK
RET_QK = RET_HEADS * RET_DK
RET_VD = RET_HEADS * RET_DV
N_RET = (DEPTH + 1) // 2
ROPE_BASE = 10000.0
SSM_GROUP = 16
SSM_GROUPS = D_MODEL // SSM_GROUP
SSM_STATE = 64
FFN_DIM = 2 * D_MODEL
CONV_W = 3
EPS = 1e-6

M_PROMPT = SEQ
M_SAMPLE = DEC_BATCH * DEC_SEQ
M_TOTAL = M_PROMPT + M_SAMPLE

LANES = 128
SUBLANES = 8
V7X_VMEM_BYTES = 64 * 1024 * 1024
VMEM_LIMIT_BYTES = V7X_VMEM_BYTES // 8 * 7

BM = M_SAMPLE
N_ROW_TILES = M_TOTAL // BM
N_PROMPT_TILES = M_PROMPT // BM
PROJ_BN = 1024
RET_IN_BN = 2048

RET_T = 256
RET_SUB = 8

SSM_T = 16
SSM_PAIRS = SSM_GROUPS // 2
SSM_PB = 4
SSM_SEGS = 8
SSM_STEPS_P = M_PROMPT // SSM_T // SSM_SEGS
SSM_STEPS_S = DEC_SEQ // SSM_T
SSM_PAIR_CH = 2 * SSM_GROUP
SSM_PAIR_COLS = SSM_T * SSM_PAIR_CH
SSM_VEC_TOKENS = LANES // SSM_PAIR_CH


def _params(n_axes):
    return pltpu.CompilerParams(dimension_semantics=("arbitrary",) * n_axes,
                                vmem_limit_bytes=VMEM_LIMIT_BYTES)


GELU_TANH_SCALE = math.sqrt(2.0 / math.pi)
GELU_TANH_CUBIC = 0.044715


def _sigmoid(x):
    return 1.0 / (1.0 + jnp.exp(-x))


def _rope_tile(cos_ref, sin_ref):
    m = pl.program_id(0)
    half = RET_DK // 2
    freq = lax.broadcasted_iota(jnp.int32, (1, half), 1).astype(F32)
    inv = ROPE_BASE ** (-freq / half)
    r = m * BM + lax.broadcasted_iota(jnp.int32, (BM, half), 0)
    pos = jnp.where(r < M_PROMPT, r, PAST_LEN + ((r - M_PROMPT) & (DEC_SEQ - 1)))
    ang = pos.astype(F32) * inv
    cos_ref[...] = jnp.cos(ang)
    sin_ref[...] = jnp.sin(ang)


def _norm_kernel(x_ref, g_ref, o_ref):
    x = x_ref[...]
    ms = jnp.mean(x * x, axis=-1, keepdims=True)
    o_ref[...] = ((x * lax.rsqrt(ms + EPS)) * g_ref[...]).astype(o_ref.dtype)


def _rmsnorm(x, g, out_dtype, tile0=0, n_tiles=N_ROW_TILES):
    return pl.pallas_call(
        _norm_kernel,
        out_shape=jax.ShapeDtypeStruct((n_tiles * BM, D_MODEL), out_dtype),
        grid=(n_tiles,),
        in_specs=[pl.BlockSpec((BM, D_MODEL), lambda m: (tile0 + m, 0)),
                  pl.BlockSpec((1, D_MODEL), lambda m: (0, 0))],
        out_specs=pl.BlockSpec((BM, D_MODEL), lambda m: (m, 0)),
        compiler_params=_params(1),
        name="rmsnorm",
    )(x, g.reshape(1, D_MODEL))


CHUNKS_PER_TILE = BM // SSM_T
N_CHUNKS = M_TOTAL // SSM_T
N_LANE_TILES = D_MODEL // LANES


def _transpose_lane_blocks(v, block_of_lane):
    n = len(v)
    out = []
    for u in range(n):
        acc = None
        for q in range(n):
            shift = ((q - u) * SSM_PAIR_CH) % LANES
            moved = v[q] if shift == 0 else pltpu.roll(v[q], shift, axis=1)
            acc = moved if acc is None else jnp.where(block_of_lane == q, moved, acc)
        out.append(acc)
    return out


def _norm_chunked_kernel(x_ref, g_ref, o_ref, h_s):
    x = x_ref[...]
    ms = jnp.mean(x * x, axis=-1, keepdims=True)
    h = (x * lax.rsqrt(ms + EPS)) * g_ref[...]
    for j in range(N_LANE_TILES):
        h_s[j] = h[:, j * LANES:(j + 1) * LANES]
    block_of_lane = lax.broadcasted_iota(jnp.int32, (CHUNKS_PER_TILE, LANES), 1) // SSM_PAIR_CH
    for j in range(N_LANE_TILES):
        for tg in range(SSM_T // SSM_VEC_TOKENS):
            rows = [h_s[j, pl.ds(tg * SSM_VEC_TOKENS + u, CHUNKS_PER_TILE, stride=SSM_T), :]
                    for u in range(SSM_VEC_TOKENS)]
            for q, out in enumerate(_transpose_lane_blocks(rows, block_of_lane)):
                c0 = q * SSM_PAIR_COLS + tg * LANES
                o_ref[j, :, c0:c0 + LANES] = out.astype(BF16)


def _rmsnorm_chunked(x, g):
    return pl.pallas_call(
        _norm_chunked_kernel,
        out_shape=jax.ShapeDtypeStruct((N_LANE_TILES, N_CHUNKS, SSM_T * LANES), BF16),
        grid=(N_ROW_TILES,),
        in_specs=[pl.BlockSpec((BM, D_MODEL), lambda m: (m, 0)),
                  pl.BlockSpec((1, D_MODEL), lambda m: (0, 0))],
        out_specs=pl.BlockSpec((N_LANE_TILES, CHUNKS_PER_TILE, SSM_T * LANES), lambda m: (0, m, 0)),
        scratch_shapes=[pltpu.VMEM((N_LANE_TILES, BM, LANES), F32)],
        compiler_params=_params(1),
        name="rmsnorm_chunked",
    )(x, g.reshape(1, D_MODEL))


W_SPEC = pl.BlockSpec(memory_space=pl.ANY)


def _weight_scratch(k, bn):
    return [pltpu.VMEM((k, bn), BF16), pltpu.VMEM((k, bn), F32), pltpu.SemaphoreType.DMA(())]


def _stage_weights(w_hbm, staged, *, layer, col0s, n_tiles, gain_ref=None):
    n = pl.program_id(0)
    m = pl.program_id(1)

    def copies(tile):
        out = []
        for (_, wf_ref, sem), col0 in zip(staged, col0s):
            bn = wf_ref.shape[1]
            cols = pl.ds(pl.multiple_of((col0 + tile) * bn, bn), bn)
            out.append(pltpu.make_async_copy(w_hbm.at[layer, :, cols], wf_ref, sem))
        return out

    @pl.when((n == 0) & (m == 0))
    def _():
        for copy in copies(0):
            copy.start()

    @pl.when(m == 0)
    def _():
        for copy, (wb_ref, wf_ref, _) in zip(copies(n), staged):
            copy.wait()
            w = wf_ref[...]
            if gain_ref is not None:
                w = w * gain_ref[...]
            wb_ref[...] = w.astype(BF16)

    @pl.when((m == 1) & (n + 1 < n_tiles))
    def _():
        for copy in copies(n + 1):
            copy.start(priority=1)


def _emit_norm_inputs(x_new, xb_ref, ssq_ref):
    xb_ref[...] = x_new.astype(BF16)
    sq = x_new * x_new
    part = sq[:, 0:LANES]
    for t in range(1, x_new.shape[1] // LANES):
        part = part + sq[:, t * LANES:(t + 1) * LANES]
    ssq_ref[...] = part


def _row_scale(ssq_ref):
    total = jnp.sum(jnp.sum(ssq_ref[...], axis=0), axis=-1, keepdims=True)
    return lax.rsqrt(total * (1.0 / D_MODEL) + EPS)


def _norm_operands(norm):
    xb, ssq, g = norm
    args = [xb, ssq, g.reshape(D_MODEL, 1)]
    specs = [pl.BlockSpec((BM, D_MODEL), lambda n, m: (m, 0)),
             pl.BlockSpec((ssq.shape[0], BM, LANES), lambda n, m: (0, m, 0)),
             pl.BlockSpec((D_MODEL, 1), lambda n, m: (0, 0))]
    return args, specs


def _norm_outputs(n_out, bn):
    shapes = [jax.ShapeDtypeStruct((M_TOTAL, n_out), BF16),
              jax.ShapeDtypeStruct((n_out // bn, M_TOTAL, LANES), F32)]
    specs = [pl.BlockSpec((BM, bn), lambda n, m: (m, n)),
             pl.BlockSpec((None, BM, LANES), lambda n, m: (n, m, 0))]
    return shapes, specs


def _prologue_kernel(xp_ref, xs_ref, xb_ref, ssq_ref, cos_ref, sin_ref):
    x = jnp.where(pl.program_id(0) < N_PROMPT_TILES, xp_ref[...], xs_ref[...])
    _emit_norm_inputs(x, xb_ref, ssq_ref)
    _rope_tile(cos_ref, sin_ref)


def _prologue(x_prompt, x_sample):
    half = RET_DK // 2
    rope_spec = pl.BlockSpec((BM, half), lambda m: (m, 0))
    rope_shape = jax.ShapeDtypeStruct((M_TOTAL, half), F32)
    return pl.pallas_call(
        _prologue_kernel,
        out_shape=(jax.ShapeDtypeStruct((M_TOTAL, D_MODEL), BF16),
                   jax.ShapeDtypeStruct((1, M_TOTAL, LANES), F32), rope_shape, rope_shape),
        grid=(N_ROW_TILES,),
        in_specs=[pl.BlockSpec((BM, D_MODEL), lambda m: (jnp.minimum(m, N_PROMPT_TILES - 1), 0)),
                  pl.BlockSpec((BM, D_MODEL), lambda m: (0, 0))],
        out_specs=(pl.BlockSpec((BM, D_MODEL), lambda m: (m, 0)),
                   pl.BlockSpec((None, BM, LANES), lambda m: (0, m, 0)), rope_spec, rope_spec),
        compiler_params=_params(1),
        name="prologue",
    )(x_prompt, x_sample)


def _mm_qk_kernel(a_ref, ssq_ref, g_ref, w_hbm, cos_ref, sin_ref, o_ref, wb_ref, wf_ref, sem, *, stage):
    n = pl.program_id(0)
    n_q = RET_QK // RET_IN_BN
    _stage_weights(w_hbm, [(wb_ref, wf_ref, sem)], gain_ref=g_ref, **stage)

    acc = jnp.dot(a_ref[...], wb_ref[...], preferred_element_type=F32) * _row_scale(ssq_ref)
    scale = jnp.where(n >= n_q, RET_DK ** -0.5, 1.0).astype(F32)
    c = cos_ref[...]
    s = sin_ref[...]
    half = RET_DK // 2
    for hh in range(RET_IN_BN // RET_DK):
        lo = hh * RET_DK
        x1 = acc[:, lo:lo + half]
        x2 = acc[:, lo + half:lo + RET_DK]
        o_ref[:, lo:lo + half] = ((x1 * c - x2 * s) * scale).astype(BF16)
        o_ref[:, lo + half:lo + RET_DK] = ((x1 * s + x2 * c) * scale).astype(BF16)


def _mm_cast_kernel(a_ref, ssq_ref, g_ref, w_hbm, o_ref, wb_ref, wf_ref, sem, *, stage):
    _stage_weights(w_hbm, [(wb_ref, wf_ref, sem)], gain_ref=g_ref, **stage)

    acc = jnp.dot(a_ref[...], wb_ref[...], preferred_element_type=F32) * _row_scale(ssq_ref)
    o_ref[...] = acc.astype(BF16)


def _mm_retin(norm, w, layer, cos, sin):
    bn = RET_IN_BN
    _, k, n_out = w.shape
    half = RET_DK // 2
    n_qk = 2 * RET_QK // bn
    norm_args, norm_specs = _norm_operands(norm)
    out_spec = pl.BlockSpec((BM, bn), lambda n, m: (m, n))
    rope_spec = pl.BlockSpec((BM, half), lambda n, m: (m, 0))
    n_vg = n_out // bn - n_qk
    qk = pl.pallas_call(
        functools.partial(_mm_qk_kernel, stage=dict(layer=layer, col0s=(0,), n_tiles=n_qk)),
        out_shape=jax.ShapeDtypeStruct((M_TOTAL, 2 * RET_QK), BF16),
        grid=(n_qk, N_ROW_TILES),
        in_specs=norm_specs + [W_SPEC, rope_spec, rope_spec],
        out_specs=out_spec,
        scratch_shapes=_weight_scratch(k, bn),
        compiler_params=_params(2),
        name="mm_ret_qk",
    )(*norm_args, w, cos, sin)
    vg = pl.pallas_call(
        functools.partial(_mm_cast_kernel, stage=dict(layer=layer, col0s=(n_qk,), n_tiles=n_vg)),
        out_shape=jax.ShapeDtypeStruct((M_TOTAL, 2 * RET_VD), BF16),
        grid=(n_vg, N_ROW_TILES),
        in_specs=norm_specs + [W_SPEC],
        out_specs=out_spec,
        scratch_shapes=_weight_scratch(k, bn),
        compiler_params=_params(2),
        name="mm_ret_vg",
    )(*norm_args, w)
    return qk, vg


def _row_split_specs(cols, col_index):
    return [pl.BlockSpec((BM, cols), lambda n, m: (jnp.minimum(m, N_PROMPT_TILES - 1), col_index(n))),
            pl.BlockSpec((BM, cols), lambda n, m: (0, col_index(n)))]


def _mm_res_kernel(*refs, split_a, split_x, emit, stage):
    refs = list(refs)
    ap_ref = refs.pop(0)
    as_ref = refs.pop(0) if split_a else None
    w_hbm = refs.pop(0)
    xp_ref = refs.pop(0)
    xs_ref = refs.pop(0) if split_x else None
    o_ref = refs.pop(0)
    xb_ref, ssq_ref = (refs.pop(0), refs.pop(0)) if emit else (None, None)
    wb_ref, wf_ref, sem = refs
    m = pl.program_id(1)
    on_prompt = m < N_PROMPT_TILES
    _stage_weights(w_hbm, [(wb_ref, wf_ref, sem)], **stage)

    def body(a_ref):
        x = xp_ref[...]
        if split_x:
            x = jnp.where(on_prompt, x, xs_ref[...])
        x_new = x + jnp.dot(a_ref[...], wb_ref[...], preferred_element_type=F32)
        o_ref[...] = x_new
        if emit:
            _emit_norm_inputs(x_new, xb_ref, ssq_ref)

    if split_a:
        pl.when(on_prompt)(lambda: body(ap_ref))
        pl.when(jnp.logical_not(on_prompt))(lambda: body(as_ref))
    else:
        body(ap_ref)


def _mm_res(a, w, layer, x, emit=False):
    bn = PROJ_BN
    _, k, n_out = w.shape
    split_a = isinstance(a, tuple)
    split_x = isinstance(x, tuple)
    a_specs = _row_split_specs(k, lambda n: 0) if split_a else [pl.BlockSpec((BM, k), lambda n, m: (m, 0))]
    x_specs = _row_split_specs(bn, lambda n: n) if split_x else [pl.BlockSpec((BM, bn), lambda n, m: (m, n))]
    a_args = list(a) if split_a else [a]
    x_args = list(x) if split_x else [x]
    out_shapes = [jax.ShapeDtypeStruct((M_TOTAL, n_out), F32)]
    out_specs = [pl.BlockSpec((BM, bn), lambda n, m: (m, n))]
    if emit:
        shapes, specs = _norm_outputs(n_out, bn)
        out_shapes += shapes
        out_specs += specs
    out = pl.pallas_call(
        functools.partial(_mm_res_kernel, split_a=split_a, split_x=split_x, emit=emit,
                          stage=dict(layer=layer, col0s=(0,), n_tiles=n_out // bn)),
        out_shape=tuple(out_shapes),
        grid=(n_out // bn, N_ROW_TILES),
        in_specs=a_specs + [W_SPEC] + x_specs,
        out_specs=tuple(out_specs),
        scratch_shapes=_weight_scratch(k, bn),
        compiler_params=_params(2),
        name="mm_residual",
    )(*a_args, w, *x_args)
    return out if emit else out[0]


def _mm_glu_kernel(a_ref, w_hbm, x_ref, o_ref, xb_ref, ssq_ref,
                   wab_ref, waf_ref, sem_a, wgb_ref, wgf_ref, sem_g, *, stage):
    _stage_weights(w_hbm, [(wab_ref, waf_ref, sem_a), (wgb_ref, wgf_ref, sem_g)], **stage)

    a = a_ref[...]
    ga = jnp.dot(a, wab_ref[...], preferred_element_type=F32)
    gb = jnp.dot(a, wgb_ref[...], preferred_element_type=F32)
    x_new = x_ref[...] + ga * _sigmoid(gb)
    o_ref[...] = x_new
    _emit_norm_inputs(x_new, xb_ref, ssq_ref)


def _mm_glu(a, w, layer, x):
    bn = PROJ_BN
    _, k, n2 = w.shape
    n_out = n2 // 2
    nb = n_out // bn
    norm_shapes, norm_specs = _norm_outputs(n_out, bn)
    return pl.pallas_call(
        functools.partial(_mm_glu_kernel, stage=dict(layer=layer, col0s=(0, nb), n_tiles=nb)),
        out_shape=(jax.ShapeDtypeStruct((M_TOTAL, n_out), F32), *norm_shapes),
        grid=(nb, N_ROW_TILES),
        in_specs=[
            pl.BlockSpec((BM, k), lambda n, m: (m, 0)),
            W_SPEC,
            pl.BlockSpec((BM, bn), lambda n, m: (m, n)),
        ],
        out_specs=(pl.BlockSpec((BM, bn), lambda n, m: (m, n)), *norm_specs),
        scratch_shapes=_weight_scratch(k, bn) + _weight_scratch(k, bn),
        compiler_params=_params(2),
        name="mm_glu",
    )(a, w, x)


def _mm_ffnup_kernel(h_ref, ssq_ref, g_ref, w_hbm, cw_ref, cb_ref, cache_ref,
                     o_ref, tp_ref, ts_ref,
                     wab_ref, waf_ref, sem_a, wgb_ref, wgf_ref, sem_g, carry_ref, *, stage):
    m = pl.program_id(1)
    _stage_weights(w_hbm, [(wab_ref, waf_ref, sem_a), (wgb_ref, wgf_ref, sem_g)], gain_ref=g_ref, **stage)

    @pl.when(m == 0)
    def _():
        carry_ref[...] = jnp.zeros_like(carry_ref)

    h = h_ref[...]
    r = _row_scale(ssq_ref)
    a = jnp.dot(h, wab_ref[...], preferred_element_type=F32) * r
    b = jnp.dot(h, wgb_ref[...], preferred_element_type=F32) * r
    r1 = pltpu.roll(a, 1, axis=0)
    r2 = pltpu.roll(a, 2, axis=0)
    cb = cb_ref[...]
    w0 = cw_ref[0:1, :]
    w1 = cw_ref[1:2, :]
    w2 = cw_ref[2:3, :]

    def gated(a_rows, prev1, prev2, b_rows):
        conv = cb + w0 * prev2 + w1 * prev1 + w2 * a_rows
        return ((conv * _sigmoid(conv)) * b_rows).astype(BF16)

    o_ref[...] = gated(a, r1, r2, b)

    row8 = lax.broadcasted_iota(jnp.int32, (SUBLANES, 1), 0)
    heads = [tuple(z[s * DEC_SEQ:s * DEC_SEQ + SUBLANES] for z in (a, r1, r2, b))
             for s in range(DEC_BATCH)]
    tails = [a[(s + 1) * DEC_SEQ - SUBLANES:(s + 1) * DEC_SEQ] for s in range(DEC_BATCH)]

    def redo_head(s, before1, before2):
        a8, r1_8, r2_8, b8 = heads[s]
        prev1 = jnp.where(row8 == 0, before1, r1_8)
        prev2 = jnp.where(row8 == 0, before2, jnp.where(row8 == 1, before1, r2_8))
        o_ref[s * DEC_SEQ:s * DEC_SEQ + SUBLANES, :] = gated(a8, prev1, prev2, b8)

    @pl.when(m < N_PROMPT_TILES)
    def _():
        c = carry_ref[...]
        redo_head(0, c[SUBLANES - 1:SUBLANES, :], c[SUBLANES - 2:SUBLANES - 1, :])
        carry_ref[...] = tails[DEC_BATCH - 1]

        @pl.when(m == N_PROMPT_TILES - 1)
        def _():
            tp_ref[...] = tails[DEC_BATCH - 1]

    @pl.when(m == N_PROMPT_TILES)
    def _():
        for s in range(DEC_BATCH):
            c = cache_ref[s]
            redo_head(s, c[1:2, :], c[0:1, :])
            ts_ref[s * SUBLANES:(s + 1) * SUBLANES, :] = tails[s]


def _mm_ffnup(norm, w, layer, conv_w, conv_b, cache):
    bn = PROJ_BN
    k = w.shape[1]
    nb = FFN_DIM // bn
    norm_args, norm_specs = _norm_operands(norm)
    return pl.pallas_call(
        functools.partial(_mm_ffnup_kernel, stage=dict(layer=layer, col0s=(0, nb), n_tiles=nb)),
        out_shape=(
            jax.ShapeDtypeStruct((M_TOTAL, FFN_DIM), BF16),
            jax.ShapeDtypeStruct((SUBLANES, FFN_DIM), F32),
            jax.ShapeDtypeStruct((DEC_BATCH * SUBLANES, FFN_DIM), F32),
        ),
        grid=(nb, N_ROW_TILES),
        in_specs=norm_specs + [
            W_SPEC,
            pl.BlockSpec((CONV_W, bn), lambda n, m: (0, n)),
            pl.BlockSpec((1, bn), lambda n, m: (0, n)),
            pl.BlockSpec((DEC_BATCH, CONV_W - 1, bn), lambda n, m: (0, 0, n)),
        ],
        out_specs=(
            pl.BlockSpec((BM, bn), lambda n, m: (m, n)),
            pl.BlockSpec((SUBLANES, bn), lambda n, m: (0, n)),
            pl.BlockSpec((DEC_BATCH * SUBLANES, bn), lambda n, m: (0, n)),
        ),
        scratch_shapes=_weight_scratch(k, bn) + _weight_scratch(k, bn) + [pltpu.VMEM((SUBLANES, bn), F32)],
        compiler_params=_params(2),
        name="mm_ffn_up",
    )(*norm_args, w, conv_w, conv_b.reshape(1, FFN_DIM), cache)


def _ret_tables(lg, T):
    i = lax.broadcasted_iota(jnp.int32, (T, T), 0)
    j = lax.broadcasted_iota(jnp.int32, (T, T), 1)
    dist = jnp.abs(i - j).astype(F32)
    shift = int(math.log2(CHUNK))
    visible = (j >> shift) <= (i >> shift)
    mask = jnp.where(visible, jnp.exp(dist * lg), 0.0)
    t = lax.broadcasted_iota(jnp.int32, (T, 1), 0).astype(F32)
    cross = jnp.exp((t + 1.0) * lg)
    k_dec = jnp.exp((T - 1.0 - t) * lg)
    decay = jnp.exp(float(T) * lg)
    return mask, cross, k_dec, decay


def _ret_block(q, k, v, g, state, tables, gn):
    mask, cross, k_dec, decay = tables
    scores = lax.dot_general(q, k, (((1,), (1,)), ((), ())), preferred_element_type=F32)
    scores = scores * mask
    out = jnp.dot(scores.astype(BF16), v, preferred_element_type=F32)
    out = out + jnp.dot(q, state.astype(BF16), preferred_element_type=F32) * cross

    kd = (k.astype(F32) * k_dec).astype(BF16)
    new_state = decay * state + lax.dot_general(
        kd, v, (((0,), (0,)), ((), ())), preferred_element_type=F32)

    mu = jnp.mean(out, axis=-1, keepdims=True)
    oc = out - mu
    var = jnp.mean(oc * oc, axis=-1, keepdims=True)
    normed = (oc * lax.rsqrt(var + EPS)) * gn
    g = g.astype(F32)
    return (normed * (g * _sigmoid(g))).astype(BF16), new_state


def _ret_prompt_kernel(lg_ref, q_ref, k_ref, v_ref, g_ref, gn_ref, o_ref, sl_ref, state_ref, mask_ref):
    blk = pl.program_id(1)
    lg = lg_ref[...][:, 0:1]

    @pl.when(blk == 0)
    def _():
        state_ref[...] = jnp.zeros_like(state_ref)
        mask_ref[...] = _ret_tables(lg, RET_T)[0]

    _, cross, k_dec, decay = _ret_tables(lg, RET_T)
    tables = (mask_ref[...], cross, k_dec, decay)
    gn = gn_ref[...]
    state = state_ref[...]
    for sb in range(RET_SUB):
        rows = slice(sb * RET_T, (sb + 1) * RET_T)
        o_ref[rows, :], state = _ret_block(q_ref[rows, :], k_ref[rows, :], v_ref[rows, :],
                                           g_ref[rows, :], state, tables, gn)
    state_ref[...] = state

    @pl.when(blk == pl.num_programs(1) - 1)
    def _():
        sl_ref[...] = state


def _ret_sample_kernel(lg_ref, q_ref, k_ref, v_ref, g_ref, gn_ref, s0_ref, *rest, layer):
    o_ref, sl_ref = rest[-2:]
    if len(rest) == 2:
        for other in range(N_RET):
            if other != layer:
                sl_ref[other] = jnp.zeros(sl_ref.shape[1:], F32)
        sl_ref = sl_ref.at[layer]
    lg = lg_ref[...][:, 0:1]
    tables = _ret_tables(lg, DEC_SEQ)
    gn = gn_ref[...]
    for s in range(DEC_BATCH):
        rows = slice(s * DEC_SEQ, (s + 1) * DEC_SEQ)
        o_ref[rows, :], sl_ref[s] = _ret_block(q_ref[rows, :], k_ref[rows, :], v_ref[rows, :],
                                               g_ref[rows, :], s0_ref[s], tables, gn)


def _retention(qk, vg, gn, s0_stack, layer, sample_states=None):
    log_g = np.log1p(-np.exp2(-5.0 - np.arange(RET_HEADS, dtype=np.float32))).astype(np.float32)
    lg = jnp.asarray(np.broadcast_to(log_g[:, None, None], (RET_HEADS, 1, LANES)).copy())
    gn = gn.reshape(1, RET_VD)
    k_off = RET_QK // RET_DK
    g_off = RET_VD // RET_DV
    rows_p = RET_T * RET_SUB
    o_p, s_p = pl.pallas_call(
        _ret_prompt_kernel,
        out_shape=(jax.ShapeDtypeStruct((M_PROMPT, RET_VD), BF16),
                   jax.ShapeDtypeStruct((1, RET_HEADS, RET_DK, RET_DV), F32)),
        grid=(RET_HEADS, M_PROMPT // rows_p),
        in_specs=[
            pl.BlockSpec((None, 1, LANES), lambda h, b: (h, 0, 0)),
            pl.BlockSpec((rows_p, RET_DK), lambda h, b: (b, h)),
            pl.BlockSpec((rows_p, RET_DK), lambda h, b: (b, k_off + h)),
            pl.BlockSpec((rows_p, RET_DV), lambda h, b: (b, h)),
            pl.BlockSpec((rows_p, RET_DV), lambda h, b: (b, g_off + h)),
            pl.BlockSpec((1, RET_DV), lambda h, b: (0, h)),
        ],
        out_specs=(pl.BlockSpec((rows_p, RET_DV), lambda h, b: (b, h)),
                   pl.BlockSpec((None, None, RET_DK, RET_DV), lambda h, b: (0, h, 0, 0))),
        scratch_shapes=[pltpu.VMEM((RET_DK, RET_DV), F32), pltpu.VMEM((RET_T, RET_T), F32)],
        compiler_params=_params(2),
        name="retention",
    )(lg, qk, qk, vg, vg, gn)
    sample_block = M_PROMPT // M_SAMPLE
    carried = [] if sample_states is None else [sample_states]
    n_in = 7
    if carried:
        state_spec = pl.BlockSpec((None, DEC_BATCH, None, RET_DK, RET_DV), lambda h: (layer, 0, h, 0, 0))
    else:
        state_spec = pl.BlockSpec((N_RET, DEC_BATCH, None, RET_DK, RET_DV), lambda h: (0, 0, h, 0, 0))
    o_s, s_s = pl.pallas_call(
        functools.partial(_ret_sample_kernel, layer=layer),
        out_shape=(jax.ShapeDtypeStruct((M_SAMPLE, RET_VD), BF16),
                   jax.ShapeDtypeStruct((N_RET, DEC_BATCH, RET_HEADS, RET_DK, RET_DV), F32)),
        grid=(RET_HEADS,),
        in_specs=[
            pl.BlockSpec((None, 1, LANES), lambda h: (h, 0, 0)),
            pl.BlockSpec((M_SAMPLE, RET_DK), lambda h: (sample_block, h)),
            pl.BlockSpec((M_SAMPLE, RET_DK), lambda h: (sample_block, k_off + h)),
            pl.BlockSpec((M_SAMPLE, RET_DV), lambda h: (sample_block, h)),
            pl.BlockSpec((M_SAMPLE, RET_DV), lambda h: (sample_block, g_off + h)),
            pl.BlockSpec((1, RET_DV), lambda h: (0, h)),
            pl.BlockSpec((None, DEC_BATCH, None, RET_DK, RET_DV), lambda h: (layer, 0, h, 0, 0)),
        ] + [pl.BlockSpec(memory_space=pl.ANY) for _ in carried],
        out_specs=(pl.BlockSpec((M_SAMPLE, RET_DV), lambda h: (0, h)), state_spec),
        input_output_aliases={n_in: 1} if carried else {},
        compiler_params=_params(1),
        name="retention_sample",
    )(lg, qk, qk, vg, vg, gn, s0_stack, *carried)
    return o_p, o_s, s_p, s_s


def _ssm_prep_kernel(ar_ref, ai_ref, ldt_ref, br_ref, bi_ref, cr_ref, ci_ref,
                     bc_ref, cc_ref, w_ref, a16r_ref, a16i_ref, vr_s, vi_s):
    ar = ar_ref[...]
    ai = ai_ref[...]
    dt = jnp.exp(ldt_ref[...])
    mag = jnp.exp(ar * dt)
    ang = ai * dt
    abr = mag * jnp.cos(ang)
    abi = mag * jnp.sin(ang)
    den = ar * ar + ai * ai
    nr = abr - 1.0
    ni = abi
    cfr = (nr * ar + ni * ai) / den
    cfi = (ni * ar - nr * ai) / den

    def b4(z):
        return z[:, None]

    br = br_ref[...]
    bi = bi_ref[...]
    bbr = b4(cfr) * br - b4(cfi) * bi
    bbi = b4(cfr) * bi + b4(cfi) * br
    cre = cr_ref[...]
    cim = ci_ref[...]

    powers = []
    pr = jnp.ones_like(ar)
    pi = jnp.zeros_like(ar)
    for _ in range(SSM_T + 1):
        powers.append((pr, pi))
        pr, pi = pr * abr - pi * abi, pr * abi + pi * abr
    a16r_ref[...] = powers[SSM_T][0]
    a16i_ref[...] = powers[SSM_T][1]

    for m in range(SSM_T + 1):
        pr, pi = powers[m]
        vr = b4(pr) * cre - b4(pi) * cim
        vi = b4(pr) * cim + b4(pi) * cre
        if m < SSM_T:
            vr_s[:, m] = vr
            vi_s[:, m] = vi
        if m >= 1:
            cc_ref[:, m - 1, :, :, 0:LANES] = vr.astype(BF16)
            cc_ref[:, m - 1, :, :, LANES:2 * LANES] = (-vi).astype(BF16)

    for t in range(SSM_T):
        pr, pi = powers[SSM_T - 1 - t]
        bc_ref[:, t, :, :, 0:LANES] = (b4(pr) * bbr - b4(pi) * bbi).astype(BF16)
        bc_ref[:, t, :, :, LANES:2 * LANES] = (b4(pr) * bbi + b4(pi) * bbr).astype(BF16)

    lane = lax.broadcasted_iota(jnp.int32, (SSM_PAIR_CH, SSM_PAIR_COLS), 1)
    contract_last = (((1,), (1,)), ((), ()))
    for q in range(SSM_PB):
        vr_all = vr_s[q].reshape(SSM_PAIR_COLS, LANES)
        vi_all = vi_s[q].reshape(SSM_PAIR_COLS, LANES)
        lags = (lax.dot_general(bbr[q].reshape(SSM_PAIR_CH, LANES), vr_all, contract_last,
                                precision=lax.Precision.HIGHEST, preferred_element_type=F32)
                - lax.dot_general(bbi[q].reshape(SSM_PAIR_CH, LANES), vi_all, contract_last,
                                  precision=lax.Precision.HIGHEST, preferred_element_type=F32))
        for t in range(SSM_T):
            shifted = lags if t == 0 else pltpu.roll(lags, SSM_PAIR_CH * t, axis=1)
            w_ref[q, t * SSM_PAIR_CH:(t + 1) * SSM_PAIR_CH, :] = jnp.where(
                lane >= SSM_PAIR_CH * t, shifted, 0.0).astype(BF16)


def _pair_pack(x):
    x = x.reshape(SSM_PAIRS, 2, SSM_GROUP, SSM_STATE)
    lo = jnp.pad(x[:, 0], ((0, 0), (0, 0), (0, SSM_STATE)))
    hi = jnp.pad(x[:, 1], ((0, 0), (0, 0), (SSM_STATE, 0)))
    return jnp.stack([lo, hi], axis=1)


def _ssm_prep(a_re, a_im, log_dt, b_re, b_im, c_re, c_im):
    assert 2 * SSM_PB * SSM_GROUP == LANES
    ar = a_re.reshape(SSM_PAIRS, 1, LANES)
    ai = a_im.reshape(SSM_PAIRS, 1, LANES)
    ldt = jnp.repeat(log_dt, SSM_STATE).reshape(SSM_PAIRS, 1, LANES)
    b2r = _pair_pack(jnp.transpose(b_re, (0, 2, 1)))
    b2i = _pair_pack(jnp.transpose(b_im, (0, 2, 1)))
    c2r = _pair_pack(c_re)
    c2i = _pair_pack(c_im)
    vec = pl.BlockSpec((SSM_PB, 1, LANES), lambda p: (p, 0, 0))
    mat = pl.BlockSpec((SSM_PB, 2, SSM_GROUP, LANES), lambda p: (p, 0, 0, 0))
    proj = pl.BlockSpec((SSM_PB, SSM_T, 2, SSM_GROUP, 2 * LANES), lambda p: (p, 0, 0, 0, 0))
    proj_shape = jax.ShapeDtypeStruct((SSM_PAIRS, SSM_T, 2, SSM_GROUP, 2 * LANES), BF16)
    bc, cc, w, a16r, a16i = pl.pallas_call(
        _ssm_prep_kernel,
        out_shape=(
            proj_shape, proj_shape,
            jax.ShapeDtypeStruct((SSM_PAIRS, SSM_PAIR_COLS, SSM_PAIR_COLS), BF16),
            jax.ShapeDtypeStruct((SSM_PAIRS, 1, LANES), F32),
            jax.ShapeDtypeStruct((SSM_PAIRS, 1, LANES), F32),
        ),
        grid=(N_LANE_TILES,),
        in_specs=[vec, vec, vec, mat, mat, mat, mat],
        out_specs=(proj, proj,
                   pl.BlockSpec((SSM_PB, SSM_PAIR_COLS, SSM_PAIR_COLS), lambda p: (p, 0, 0)),
                   vec, vec),
        scratch_shapes=[pltpu.VMEM((SSM_PB, SSM_T, 2, SSM_GROUP, LANES), F32)] * 2,
        compiler_params=_params(1),
        name="ssm_prep",
    )(ar, ai, ldt, b2r, b2i, c2r, c2i)
    bc = bc.reshape(SSM_PAIRS, SSM_PAIR_COLS, 2 * LANES)
    cc = cc.reshape(SSM_PAIRS, SSM_PAIR_COLS, 2 * LANES)
    row = SSM_GROUPS * SSM_STATE
    a16r = jnp.broadcast_to(a16r.reshape(1, row), (SSM_SEGS, row))
    a16i = jnp.broadcast_to(a16i.reshape(1, row), (SSM_SEGS, row))
    return bc, cc, w, a16r, a16i


SSM_CHUNK_COLS = SSM_T * LANES
SSM_STATE_COLS = SSM_PB * LANES
SSM_CHUNKS_P = M_PROMPT // SSM_T
SSM_SCAN_UNROLL = 8


def _ssm_main_kernel(x_ref, w_ref, bc_ref, cc_ref, ar_ref, ai_ref, h0r_ref, h0i_ref,
                     y_ref, fpr_ref, fpi_ref, fsr_ref, fsi_ref, d_s, s_s, t_s):
    pair_cols = [slice(q * SSM_PAIR_COLS, (q + 1) * SSM_PAIR_COLS) for q in range(SSM_PB)]

    for q in range(SSM_PB):
        d = jnp.dot(x_ref[:, pair_cols[q]], bc_ref[q], preferred_element_type=F32)
        d_s[q] = d[:, 0:LANES]
        d_s[SSM_PB + q] = d[:, LANES:2 * LANES]

    tiles = range(SSM_PB)
    ar = [ar_ref[:, k * LANES:(k + 1) * LANES] for k in tiles]
    ai = [ai_ref[:, k * LANES:(k + 1) * LANES] for k in tiles]

    def advance(sr, si, rows, record=None):
        out_r, out_i = [], []
        for k in tiles:
            if record is not None:
                ref, rec_rows = record
                ref[k, rec_rows, :] = sr[k]
                ref[SSM_PB + k, rec_rows, :] = si[k]
            dr = d_s[k, rows, :]
            di = d_s[SSM_PB + k, rows, :]
            out_r.append(ar[k] * sr[k] - ai[k] * si[k] + dr)
            out_i.append(ar[k] * si[k] + ai[k] * sr[k] + di)
        return out_r, out_i

    def prompt_rows(i):
        return pl.ds(i, SSM_SEGS, stride=SSM_STEPS_P)

    def segment_ends(k):
        a1r, a1i = ar[k][0:1], ai[k][0:1]
        p8r, p8i = ar[k], ai[k]
        for _ in range(int(math.log2(SUBLANES))):
            p8r, p8i = p8r * p8r - p8i * p8i, 2.0 * p8r * p8i
        wr, wi = [jnp.ones((1, LANES), F32)], [jnp.zeros((1, LANES), F32)]
        for _ in range(SUBLANES - 1):
            wr, wi = wr + [wr[-1] * a1r - wi[-1] * a1i], wi + [wr[-1] * a1i + wi[-1] * a1r]
        row_r = jnp.concatenate(wr[::-1], axis=0)
        row_i = jnp.concatenate(wi[::-1], axis=0)
        ends_r, ends_i = [], []
        for s in range(SSM_SEGS):
            acc_r = acc_i = None
            for b in range(SSM_STEPS_P // SUBLANES):
                rows = slice(s * SSM_STEPS_P + b * SUBLANES, s * SSM_STEPS_P + (b + 1) * SUBLANES)
                dr = d_s[k, rows, :]
                di = d_s[SSM_PB + k, rows, :]
                if acc_r is None:
                    acc_r, acc_i = dr, di
                else:
                    acc_r, acc_i = p8r * acc_r - p8i * acc_i + dr, p8r * acc_i + p8i * acc_r + di
            ends_r.append(jnp.sum(row_r * acc_r - row_i * acc_i, axis=0, keepdims=True))
            ends_i.append(jnp.sum(row_r * acc_i + row_i * acc_r, axis=0, keepdims=True))
        return jnp.concatenate(ends_r, axis=0), jnp.concatenate(ends_i, axis=0)

    er, ei = zip(*[segment_ends(k) for k in tiles])

    def prompt_pass(start_r, start_i):
        def body(i, carry):
            rec = (t_s, pl.ds(pl.multiple_of(i * SSM_SEGS, SSM_SEGS), SSM_SEGS))
            out_r, out_i = advance(list(carry[0]), list(carry[1]), prompt_rows(i), rec)
            return tuple(out_r), tuple(out_i)
        lax.fori_loop(0, SSM_STEPS_P, body, (tuple(start_r), tuple(start_i)), unroll=SSM_SCAN_UNROLL)

    sr, si = [], []
    for k in tiles:
        pr, pi = ar[k][0:1], ai[k][0:1]
        for _ in range(int(math.log2(SSM_STEPS_P))):
            pr, pi = pr * pr - pi * pi, 2.0 * pr * pi
        cr = jnp.zeros((1, LANES), F32)
        ci = cr
        starts_r, starts_i = [], []
        for s in range(SSM_SEGS):
            starts_r.append(cr)
            starts_i.append(ci)
            cr, ci = (pr * cr - pi * ci + er[k][s:s + 1], pr * ci + pi * cr + ei[k][s:s + 1])
        cols = slice(k * LANES, (k + 1) * LANES)
        fpr_ref[:, cols] = jnp.broadcast_to(cr, (SSM_SEGS, LANES))
        fpi_ref[:, cols] = jnp.broadcast_to(ci, (SSM_SEGS, LANES))
        sr.append(jnp.concatenate(starts_r, axis=0))
        si.append(jnp.concatenate(starts_i, axis=0))

    prompt_pass(sr, si)
    for k in range(2 * SSM_PB):
        for s in range(SSM_SEGS):
            s_s[k, s * SSM_STEPS_P:(s + 1) * SSM_STEPS_P, :] = t_s[k, pl.ds(s, SSM_STEPS_P, stride=SSM_SEGS), :]
    sr = [h0r_ref[:, k * LANES:(k + 1) * LANES] for k in tiles]
    si = [h0i_ref[:, k * LANES:(k + 1) * LANES] for k in tiles]
    for i in range(SSM_STEPS_S):
        rows = pl.ds(SSM_CHUNKS_P + i, DEC_BATCH, stride=SSM_STEPS_S)
        sr, si = advance(sr, si, rows, (s_s, rows))
    for k in tiles:
        fsr_ref[:, k * LANES:(k + 1) * LANES] = sr[k]
        fsi_ref[:, k * LANES:(k + 1) * LANES] = si[k]

    contract_last = (((1,), (1,)), ((), ()))
    for q in range(SSM_PB):
        states = jnp.concatenate([s_s[q].astype(BF16), s_s[SSM_PB + q].astype(BF16)], axis=1)
        carried = lax.dot_general(states, cc_ref[q], contract_last, preferred_element_type=F32)
        local = jnp.dot(x_ref[:, pair_cols[q]], w_ref[q], preferred_element_type=F32)
        y_ref[:, pair_cols[q]] = local + carried


def _ssm_main(x_c, w, bc, cc, a16r, a16i, h0r, h0i):
    row = SSM_GROUPS * SSM_STATE
    st = pl.BlockSpec((SSM_SEGS, SSM_STATE_COLS), lambda j: (0, j))
    st_shape = jax.ShapeDtypeStruct((SSM_SEGS, row), F32)
    proj = pl.BlockSpec((SSM_PB, SSM_PAIR_COLS, 2 * LANES), lambda j: (j, 0, 0))
    chunk_rows = pl.BlockSpec((None, N_CHUNKS, SSM_CHUNK_COLS), lambda j: (j, 0, 0))
    return pl.pallas_call(
        _ssm_main_kernel,
        out_shape=(jax.ShapeDtypeStruct((N_LANE_TILES, N_CHUNKS, SSM_CHUNK_COLS), F32),
                   st_shape, st_shape, st_shape, st_shape),
        grid=(N_LANE_TILES,),
        in_specs=[
            chunk_rows,
            pl.BlockSpec((SSM_PB, SSM_PAIR_COLS, SSM_PAIR_COLS), lambda j: (j, 0, 0)),
            proj, proj, st, st, st, st,
        ],
        out_specs=(chunk_rows, st, st, st, st),
        scratch_shapes=[
            pltpu.VMEM((2 * SSM_PB, N_CHUNKS, LANES), F32),
            pltpu.VMEM((2 * SSM_PB, N_CHUNKS, LANES), F32),
            pltpu.VMEM((2 * SSM_PB, SSM_CHUNKS_P, LANES), F32),
        ],
        compiler_params=_params(1),
        name="ssm_main",
    )(x_c, w, bc, cc, a16r, a16i, h0r, h0i)


def _ssm_post_kernel(y_ref, x_ref, g_ref, d_ref, o_ref, y_s):
    block_of_lane = lax.broadcasted_iota(jnp.int32, (CHUNKS_PER_TILE, LANES), 1) // SSM_PAIR_CH
    for j in range(N_LANE_TILES):
        for tg in range(SSM_T // SSM_VEC_TOKENS):
            pieces = [y_ref[j, :, q * SSM_PAIR_COLS + tg * LANES:q * SSM_PAIR_COLS + (tg + 1) * LANES]
                      for q in range(SSM_PB)]
            for u, out in enumerate(_transpose_lane_blocks(pieces, block_of_lane)):
                y_s[j, pl.ds(tg * SSM_VEC_TOKENS + u, CHUNKS_PER_TILE, stride=SSM_T), :] = out
    x = x_ref[...]
    ms = jnp.mean(x * x, axis=-1, keepdims=True)
    h = (x * lax.rsqrt(ms + EPS)) * g_ref[...]
    for j in range(N_LANE_TILES):
        cols = slice(j * LANES, (j + 1) * LANES)
        y = y_s[j] + d_ref[:, cols] * h[:, cols]
        gelu = 0.5 * y * (1.0 + jnp.tanh(GELU_TANH_SCALE * (y + GELU_TANH_CUBIC * (y * y * y))))
        o_ref[:, cols] = gelu.astype(BF16)


def _ssm_post(y_c, x, g, d):
    spec = pl.BlockSpec((BM, D_MODEL), lambda m: (m, 0))
    vec = pl.BlockSpec((1, D_MODEL), lambda m: (0, 0))
    return pl.pallas_call(
        _ssm_post_kernel,
        out_shape=jax.ShapeDtypeStruct((M_TOTAL, D_MODEL), BF16),
        grid=(N_ROW_TILES,),
        in_specs=[pl.BlockSpec((N_LANE_TILES, CHUNKS_PER_TILE, SSM_CHUNK_COLS), lambda m: (0, m, 0)),
                  spec, vec, vec],
        out_specs=spec,
        scratch_shapes=[pltpu.VMEM((N_LANE_TILES, BM, LANES), F32)],
        compiler_params=_params(1),
        name="ssm_post",
    )(y_c, x, g.reshape(1, D_MODEL), d.reshape(1, D_MODEL))


def kernel(x_prompt, x_sample, state_ret, state_ssm_re, state_ssm_im, cache_conv, norm_mix, norm_ffn, norm_final, ret_w_in, ret_gn, ret_w_out, ssm_a_re, ssm_a_im, ssm_log_dt, ssm_b_re, ssm_b_im, ssm_c_re, ssm_c_im, ssm_d, ssm_w_glu, ffn_w_up, ffn_conv_w, ffn_conv_b, ffn_w_down):
    x = (x_prompt.reshape(M_PROMPT, D_MODEL), x_sample.reshape(M_SAMPLE, D_MODEL))
    xb, ssq, cos, sin = _prologue(*x)

    ret_p, ret_s = [], None
    re_p, im_p, re_s, im_s = [], [], [], []
    conv_p, conv_s = [], []
    state_row = SSM_GROUPS * SSM_STATE
    for i in range(DEPTH):
        j = i // 2
        if i % 2 == 0:
            qk, vg = _mm_retin((xb, ssq, norm_mix[i]), ret_w_in, j, cos, sin)
            o_p, o_s, s_p, ret_s = _retention(qk, vg, ret_gn[j], state_ret, j, ret_s)
            ret_p.append(s_p)
            x, xb, ssq = _mm_res((o_p, o_s), ret_w_out, j, x, emit=True)
        else:
            h_c = _rmsnorm_chunked(x, norm_mix[i])
            bc, cc, w_ssm, a16r, a16i = _ssm_prep(ssm_a_re[j], ssm_a_im[j], ssm_log_dt[j],
                                               ssm_b_re[j], ssm_b_im[j], ssm_c_re[j], ssm_c_im[j])
            y_c, fpr, fpi, fsr, fsi = _ssm_main(
                h_c, w_ssm, bc, cc, a16r, a16i,
                state_ssm_re[j].reshape(DEC_BATCH, state_row),
                state_ssm_im[j].reshape(DEC_BATCH, state_row))
            re_p.append(fpr[0:1].reshape(1, SSM_GROUPS, SSM_STATE))
            im_p.append(fpi[0:1].reshape(1, SSM_GROUPS, SSM_STATE))
            re_s.append(fsr.reshape(DEC_BATCH, SSM_GROUPS, SSM_STATE))
            im_s.append(fsi.reshape(DEC_BATCH, SSM_GROUPS, SSM_STATE))
            gl = _ssm_post(y_c, x, norm_mix[i], ssm_d[j])
            x, xb, ssq = _mm_glu(gl, ssm_w_glu, j, x)

        u, tail_p, tail_s = _mm_ffnup((xb, ssq, norm_ffn[i]), ffn_w_up, i, ffn_conv_w[i],
                                      ffn_conv_b[i], cache_conv[i])
        conv_p.append(tail_p[SUBLANES - (CONV_W - 1):].reshape(1, CONV_W - 1, FFN_DIM))
        conv_s.append(tail_s.reshape(DEC_BATCH, SUBLANES, FFN_DIM)[:, SUBLANES - (CONV_W - 1):])
        if i + 1 < DEPTH and (i + 1) % 2 == 0:
            x, xb, ssq = _mm_res(u, ffn_w_down, i, x, emit=True)
        else:
            x = _mm_res(u, ffn_w_down, i, x)

    y_prompt = _rmsnorm(x, norm_final, F32, 0, N_PROMPT_TILES).reshape(1, SEQ, D_MODEL)
    y_sample = _rmsnorm(x, norm_final, F32, N_PROMPT_TILES, 1).reshape(DEC_BATCH, DEC_SEQ, D_MODEL)
    return (y_prompt, y_sample, jnp.stack(ret_p), ret_s,
            jnp.stack(re_p), jnp.stack(im_p), jnp.stack(re_s), jnp.stack(im_s),
            jnp.stack(conv_p), jnp.stack(conv_s))
```

```python
import functools
import math

import numpy as np
import jax
import jax.numpy as jnp
from jax import lax
from jax.experimental import pallas as pl
from jax.experimental.pallas import tpu as pltpu

F32 = jnp.float32
BF16 = jnp.bfloat16

D_MODEL = 2048
SEQ = 8192
DEPTH = 4
DEC_BATCH = 8
DEC_SEQ = 64
PAST_LEN = 1024
CHUNK = 64
RET_HEADS = 8
RET_DK = D_MODEL // RET_HEADS
RET_DV = 2 * RET_DK
RET_QK = RET_HEADS * RET_DK
RET_VD = RET_HEADS * RET_DV
N_RET = (DEPTH + 1) // 2
ROPE_BASE = 10000.0
SSM_GROUP = 16
SSM_GROUPS = D_MODEL // SSM_GROUP
SSM_STATE = 64
FFN_DIM = 2 * D_MODEL
CONV_W = 3
EPS = 1e-6

M_PROMPT = SEQ
M_SAMPLE = DEC_BATCH * DEC_SEQ
M_TOTAL = M_PROMPT + M_SAMPLE

LANES = 128
SUBLANES = 8
V7X_VMEM_BYTES = 64 * 1024 * 1024
VMEM_LIMIT_BYTES = V7X_VMEM_BYTES // 8 * 7

BM = M_SAMPLE
N_ROW_TILES = M_TOTAL // BM
N_PROMPT_TILES = M_PROMPT // BM
PROJ_BN = 1024
RET_IN_BN = 2048

RET_T = 256
RET_SUB = 16

SSM_T = 16
SSM_PAIRS = SSM_GROUPS // 2
SSM_PB = 4
SSM_SEGS = 8
SSM_STEPS_P = M_PROMPT // SSM_T // SSM_SEGS
SSM_STEPS_S = DEC_SEQ // SSM_T
SSM_PAIR_CH = 2 * SSM_GROUP
SSM_PAIR_COLS = SSM_T * SSM_PAIR_CH
SSM_VEC_TOKENS = LANES // SSM_PAIR_CH


def _params(n_axes):
    return pltpu.CompilerParams(dimension_semantics=("arbitrary",) * n_axes,
                                vmem_limit_bytes=VMEM_LIMIT_BYTES)


GELU_TANH_SCALE = math.sqrt(2.0 / math.pi)
GELU_TANH_CUBIC = 0.044715


def _sigmoid(x):
    return 1.0 / (1.0 + jnp.exp(-x))


def _rope_tile(cos_ref, sin_ref):
    m = pl.program_id(0)
    half = RET_DK // 2
    freq = lax.broadcasted_iota(jnp.int32, (1, half), 1).astype(F32)
    inv = ROPE_BASE ** (-freq / half)
    r = m * BM + lax.broadcasted_iota(jnp.int32, (BM, half), 0)
    pos = jnp.where(r < M_PROMPT, r, PAST_LEN + ((r - M_PROMPT) & (DEC_SEQ - 1)))
    ang = pos.astype(F32) * inv
    cos_ref[...] = jnp.cos(ang)
    sin_ref[...] = jnp.sin(ang)


def _norm_kernel(x_ref, g_ref, o_ref):
    x = x_ref[...]
    ms = jnp.mean(x * x, axis=-1, keepdims=True)
    o_ref[...] = ((x * lax.rsqrt(ms + EPS)) * g_ref[...]).astype(o_ref.dtype)


def _rmsnorm(x, g, out_dtype, tile0=0, n_tiles=N_ROW_TILES):
    return pl.pallas_call(
        _norm_kernel,
        out_shape=jax.ShapeDtypeStruct((n_tiles * BM, D_MODEL), out_dtype),
        grid=(n_tiles,),
        in_specs=[pl.BlockSpec((BM, D_MODEL), lambda m: (tile0 + m, 0)),
                  pl.BlockSpec((1, D_MODEL), lambda m: (0, 0))],
        out_specs=pl.BlockSpec((BM, D_MODEL), lambda m: (m, 0)),
        compiler_params=_params(1),
        name="rmsnorm",
    )(x, g.reshape(1, D_MODEL))


CHUNKS_PER_TILE = BM // SSM_T
N_CHUNKS = M_TOTAL // SSM_T
N_LANE_TILES = D_MODEL // LANES


def _transpose_lane_blocks(v, block_of_lane):
    n = len(v)
    out = []
    for u in range(n):
        acc = None
        for q in range(n):
            shift = ((q - u) * SSM_PAIR_CH) % LANES
            moved = v[q] if shift == 0 else pltpu.roll(v[q], shift, axis=1)
            acc = moved if acc is None else jnp.where(block_of_lane == q, moved, acc)
        out.append(acc)
    return out


def _norm_chunked_kernel(x_ref, g_ref, o_ref, h_s):
    x = x_ref[...]
    ms = jnp.mean(x * x, axis=-1, keepdims=True)
    h = (x * lax.rsqrt(ms + EPS)) * g_ref[...]
    for j in range(N_LANE_TILES):
        h_s[j] = h[:, j * LANES:(j + 1) * LANES]
    block_of_lane = lax.broadcasted_iota(jnp.int32, (CHUNKS_PER_TILE, LANES), 1) // SSM_PAIR_CH
    for j in range(N_LANE_TILES):
        for tg in range(SSM_T // SSM_VEC_TOKENS):
            rows = [h_s[j, pl.ds(tg * SSM_VEC_TOKENS + u, CHUNKS_PER_TILE, stride=SSM_T), :]
                    for u in range(SSM_VEC_TOKENS)]
            for q, out in enumerate(_transpose_lane_blocks(rows, block_of_lane)):
                c0 = q * SSM_PAIR_COLS + tg * LANES
                o_ref[j, :, c0:c0 + LANES] = out.astype(BF16)


def _rmsnorm_chunked(x, g):
    return pl.pallas_call(
        _norm_chunked_kernel,
        out_shape=jax.ShapeDtypeStruct((N_LANE_TILES, N_CHUNKS, SSM_T * LANES), BF16),
        grid=(N_ROW_TILES,),
        in_specs=[pl.BlockSpec((BM, D_MODEL), lambda m: (m, 0)),
                  pl.BlockSpec((1, D_MODEL), lambda m: (0, 0))],
        out_specs=pl.BlockSpec((N_LANE_TILES, CHUNKS_PER_TILE, SSM_T * LANES), lambda m: (0, m, 0)),
        scratch_shapes=[pltpu.VMEM((N_LANE_TILES, BM, LANES), F32)],
        compiler_params=_params(1),
        name="rmsnorm_chunked",
    )(x, g.reshape(1, D_MODEL))


W_SPEC = pl.BlockSpec(memory_space=pl.ANY)


def _weight_scratch(k, bn):
    return [pltpu.VMEM((k, bn), BF16), pltpu.VMEM((k, bn), F32), pltpu.SemaphoreType.DMA(())]


def _stage_weights(w_hbm, staged, *, layer, col0s, n_tiles, gain_ref=None):
    n = pl.program_id(0)
    m = pl.program_id(1)

    def copies(tile):
        out = []
        for (_, wf_ref, sem), col0 in zip(staged, col0s):
            bn = wf_ref.shape[1]
            cols = pl.ds(pl.multiple_of((col0 + tile) * bn, bn), bn)
            out.append(pltpu.make_async_copy(w_hbm.at[layer, :, cols], wf_ref, sem))
        return out

    @pl.when((n == 0) & (m == 0))
    def _():
        for copy in copies(0):
            copy.start()

    @pl.when(m == 0)
    def _():
        for copy, (wb_ref, wf_ref, _) in zip(copies(n), staged):
            copy.wait()
            w = wf_ref[...]
            if gain_ref is not None:
                w = w * gain_ref[...]
            wb_ref[...] = w.astype(BF16)

    @pl.when((m == 1) & (n + 1 < n_tiles))
    def _():
        for copy in copies(n + 1):
            copy.start(priority=1)


def _emit_norm_inputs(x_new, xb_ref, ssq_ref):
    xb_ref[...] = x_new.astype(BF16)
    sq = x_new * x_new
    part = sq[:, 0:LANES]
    for t in range(1, x_new.shape[1] // LANES):
        part = part + sq[:, t * LANES:(t + 1) * LANES]
    ssq_ref[...] = part


def _row_scale(ssq_ref):
    total = jnp.sum(jnp.sum(ssq_ref[...], axis=0), axis=-1, keepdims=True)
    return lax.rsqrt(total * (1.0 / D_MODEL) + EPS)


def _norm_operands(norm):
    xb, ssq, g = norm
    args = [xb, ssq, g.reshape(D_MODEL, 1)]
    specs = [pl.BlockSpec((BM, D_MODEL), lambda n, m: (m, 0)),
             pl.BlockSpec((ssq.shape[0], BM, LANES), lambda n, m: (0, m, 0)),
             pl.BlockSpec((D_MODEL, 1), lambda n, m: (0, 0))]
    return args, specs


def _norm_outputs(n_out, bn):
    shapes = [jax.ShapeDtypeStruct((M_TOTAL, n_out), BF16),
              jax.ShapeDtypeStruct((n_out // bn, M_TOTAL, LANES), F32)]
    specs = [pl.BlockSpec((BM, bn), lambda n, m: (m, n)),
             pl.BlockSpec((None, BM, LANES), lambda n, m: (n, m, 0))]
    return shapes, specs


def _prologue_kernel(xp_ref, xs_ref, xb_ref, ssq_ref, cos_ref, sin_ref):
    x = jnp.where(pl.program_id(0) < N_PROMPT_TILES, xp_ref[...], xs_ref[...])
    _emit_norm_inputs(x, xb_ref, ssq_ref)
    _rope_tile(cos_ref, sin_ref)


def _prologue(x_prompt, x_sample):
    half = RET_DK // 2
    rope_spec = pl.BlockSpec((BM, half), lambda m: (m, 0))
    rope_shape = jax.ShapeDtypeStruct((M_TOTAL, half), F32)
    return pl.pallas_call(
        _prologue_kernel,
        out_shape=(jax.ShapeDtypeStruct((M_TOTAL, D_MODEL), BF16),
                   jax.ShapeDtypeStruct((1, M_TOTAL, LANES), F32), rope_shape, rope_shape),
        grid=(N_ROW_TILES,),
        in_specs=[pl.BlockSpec((BM, D_MODEL), lambda m: (jnp.minimum(m, N_PROMPT_TILES - 1), 0)),
                  pl.BlockSpec((BM, D_MODEL), lambda m: (0, 0))],
        out_specs=(pl.BlockSpec((BM, D_MODEL), lambda m: (m, 0)),
                   pl.BlockSpec((None, BM, LANES), lambda m: (0, m, 0)), rope_spec, rope_spec),
        compiler_params=_params(1),
        name="prologue",
    )(x_prompt, x_sample)


def _mm_qk_kernel(a_ref, ssq_ref, g_ref, w_hbm, cos_ref, sin_ref, o_ref, wb_ref, wf_ref, sem, *, stage):
    n = pl.program_id(0)
    n_q = RET_QK // RET_IN_BN
    _stage_weights(w_hbm, [(wb_ref, wf_ref, sem)], gain_ref=g_ref, **stage)

    acc = jnp.dot(a_ref[...], wb_ref[...], preferred_element_type=F32) * _row_scale(ssq_ref)
    scale = jnp.where(n >= n_q, RET_DK ** -0.5, 1.0).astype(F32)
    c = cos_ref[...]
    s = sin_ref[...]
    half = RET_DK // 2
    for hh in range(RET_IN_BN // RET_DK):
        lo = hh * RET_DK
        x1 = acc[:, lo:lo + half]
        x2 = acc[:, lo + half:lo + RET_DK]
        o_ref[:, lo:lo + half] = ((x1 * c - x2 * s) * scale).astype(BF16)
        o_ref[:, lo + half:lo + RET_DK] = ((x1 * s + x2 * c) * scale).astype(BF16)


def _mm_cast_kernel(a_ref, ssq_ref, g_ref, w_hbm, o_ref, wb_ref, wf_ref, sem, *, stage):
    _stage_weights(w_hbm, [(wb_ref, wf_ref, sem)], gain_ref=g_ref, **stage)

    acc = jnp.dot(a_ref[...], wb_ref[...], preferred_element_type=F32) * _row_scale(ssq_ref)
    o_ref[...] = acc.astype(BF16)


def _mm_retin(norm, w, layer, cos, sin):
    bn = RET_IN_BN
    _, k, n_out = w.shape
    half = RET_DK // 2
    n_qk = 2 * RET_QK // bn
    norm_args, norm_specs = _norm_operands(norm)
    out_spec = pl.BlockSpec((BM, bn), lambda n, m: (m, n))
    rope_spec = pl.BlockSpec((BM, half), lambda n, m: (m, 0))
    n_vg = n_out // bn - n_qk
    qk = pl.pallas_call(
        functools.partial(_mm_qk_kernel, stage=dict(layer=layer, col0s=(0,), n_tiles=n_qk)),
        out_shape=jax.ShapeDtypeStruct((M_TOTAL, 2 * RET_QK), BF16),
        grid=(n_qk, N_ROW_TILES),
        in_specs=norm_specs + [W_SPEC, rope_spec, rope_spec],
        out_specs=out_spec,
        scratch_shapes=_weight_scratch(k, bn),
        compiler_params=_params(2),
        name="mm_ret_qk",
    )(*norm_args, w, cos, sin)
    vg = pl.pallas_call(
        functools.partial(_mm_cast_kernel, stage=dict(layer=layer, col0s=(n_qk,), n_tiles=n_vg)),
        out_shape=jax.ShapeDtypeStruct((M_TOTAL, 2 * RET_VD), BF16),
        grid=(n_vg, N_ROW_TILES),
        in_specs=norm_specs + [W_SPEC],
        out_specs=out_spec,
        scratch_shapes=_weight_scratch(k, bn),
        compiler_params=_params(2),
        name="mm_ret_vg",
    )(*norm_args, w)
    return qk, vg


def _row_split_specs(cols, col_index):
    return [pl.BlockSpec((BM, cols), lambda n, m: (jnp.minimum(m, N_PROMPT_TILES - 1), col_index(n))),
            pl.BlockSpec((BM, cols), lambda n, m: (0, col_index(n)))]


def _mm_res_kernel(*refs, split_a, split_x, emit, stage):
    refs = list(refs)
    ap_ref = refs.pop(0)
    as_ref = refs.pop(0) if split_a else None
    w_hbm = refs.pop(0)
    xp_ref = refs.pop(0)
    xs_ref = refs.pop(0) if split_x else None
    o_ref = refs.pop(0)
    xb_ref, ssq_ref = (refs.pop(0), refs.pop(0)) if emit else (None, None)
    wb_ref, wf_ref, sem = refs
    m = pl.program_id(1)
    on_prompt = m < N_PROMPT_TILES
    _stage_weights(w_hbm, [(wb_ref, wf_ref, sem)], **stage)

    def body(a_ref):
        x = xp_ref[...]
        if split_x:
            x = jnp.where(on_prompt, x, xs_ref[...])
        x_new = x + jnp.dot(a_ref[...], wb_ref[...], preferred_element_type=F32)
        o_ref[...] = x_new
        if emit:
            _emit_norm_inputs(x_new, xb_ref, ssq_ref)

    if split_a:
        pl.when(on_prompt)(lambda: body(ap_ref))
        pl.when(jnp.logical_not(on_prompt))(lambda: body(as_ref))
    else:
        body(ap_ref)


def _mm_res(a, w, layer, x, emit=False):
    bn = PROJ_BN
    _, k, n_out = w.shape
    split_a = isinstance(a, tuple)
    split_x = isinstance(x, tuple)
    a_specs = _row_split_specs(k, lambda n: 0) if split_a else [pl.BlockSpec((BM, k), lambda n, m: (m, 0))]
    x_specs = _row_split_specs(bn, lambda n: n) if split_x else [pl.BlockSpec((BM, bn), lambda n, m: (m, n))]
    a_args = list(a) if split_a else [a]
    x_args = list(x) if split_x else [x]
    out_shapes = [jax.ShapeDtypeStruct((M_TOTAL, n_out), F32)]
    out_specs = [pl.BlockSpec((BM, bn), lambda n, m: (m, n))]
    if emit:
        shapes, specs = _norm_outputs(n_out, bn)
        out_shapes += shapes
        out_specs += specs
    out = pl.pallas_call(
        functools.partial(_mm_res_kernel, split_a=split_a, split_x=split_x, emit=emit,
                          stage=dict(layer=layer, col0s=(0,), n_tiles=n_out // bn)),
        out_shape=tuple(out_shapes),
        grid=(n_out // bn, N_ROW_TILES),
        in_specs=a_specs + [W_SPEC] + x_specs,
        out_specs=tuple(out_specs),
        scratch_shapes=_weight_scratch(k, bn),
        compiler_params=_params(2),
        name="mm_residual",
    )(*a_args, w, *x_args)
    return out if emit else out[0]


def _mm_glu_kernel(a_ref, w_hbm, x_ref, o_ref, xb_ref, ssq_ref,
                   wab_ref, waf_ref, sem_a, wgb_ref, wgf_ref, sem_g, *, stage):
    _stage_weights(w_hbm, [(wab_ref, waf_ref, sem_a), (wgb_ref, wgf_ref, sem_g)], **stage)

    a = a_ref[...]
    ga = jnp.dot(a, wab_ref[...], preferred_element_type=F32)
    gb = jnp.dot(a, wgb_ref[...], preferred_element_type=F32)
    x_new = x_ref[...] + ga * _sigmoid(gb)
    o_ref[...] = x_new
    _emit_norm_inputs(x_new, xb_ref, ssq_ref)


def _mm_glu(a, w, layer, x):
    bn = PROJ_BN
    _, k, n2 = w.shape
    n_out = n2 // 2
    nb = n_out // bn
    norm_shapes, norm_specs = _norm_outputs(n_out, bn)
    return pl.pallas_call(
        functools.partial(_mm_glu_kernel, stage=dict(layer=layer, col0s=(0, nb), n_tiles=nb)),
        out_shape=(jax.ShapeDtypeStruct((M_TOTAL, n_out), F32), *norm_shapes),
        grid=(nb, N_ROW_TILES),
        in_specs=[
            pl.BlockSpec((BM, k), lambda n, m: (m, 0)),
            W_SPEC,
            pl.BlockSpec((BM, bn), lambda n, m: (m, n)),
        ],
        out_specs=(pl.BlockSpec((BM, bn), lambda n, m: (m, n)), *norm_specs),
        scratch_shapes=_weight_scratch(k, bn) + _weight_scratch(k, bn),
        compiler_params=_params(2),
        name="mm_glu",
    )(a, w, x)


def _mm_ffnup_kernel(h_ref, ssq_ref, g_ref, w_hbm, cw_ref, cb_ref, cache_ref,
                     o_ref, tp_ref, ts_ref,
                     wab_ref, waf_ref, sem_a, wgb_ref, wgf_ref, sem_g, carry_ref, *, stage):
    m = pl.program_id(1)
    _stage_weights(w_hbm, [(wab_ref, waf_ref, sem_a), (wgb_ref, wgf_ref, sem_g)], gain_ref=g_ref, **stage)

    @pl.when(m == 0)
    def _():
        carry_ref[...] = jnp.zeros_like(carry_ref)

    h = h_ref[...]
    r = _row_scale(ssq_ref)
    a = jnp.dot(h, wab_ref[...], preferred_element_type=F32) * r
    b = jnp.dot(h, wgb_ref[...], preferred_element_type=F32) * r
    r1 = pltpu.roll(a, 1, axis=0)
    r2 = pltpu.roll(a, 2, axis=0)
    cb = cb_ref[...]
    w0 = cw_ref[0:1, :]
    w1 = cw_ref[1:2, :]
    w2 = cw_ref[2:3, :]

    def gated(a_rows, prev1, prev2, b_rows):
        conv = cb + w0 * prev2 + w1 * prev1 + w2 * a_rows
        return ((conv * _sigmoid(conv)) * b_rows).astype(BF16)

    o_ref[...] = gated(a, r1, r2, b)

    row8 = lax.broadcasted_iota(jnp.int32, (SUBLANES, 1), 0)
    heads = [tuple(z[s * DEC_SEQ:s * DEC_SEQ + SUBLANES] for z in (a, r1, r2, b))
             for s in range(DEC_BATCH)]
    tails = [a[(s + 1) * DEC_SEQ - SUBLANES:(s + 1) * DEC_SEQ] for s in range(DEC_BATCH)]

    def redo_head(s, before1, before2):
        a8, r1_8, r2_8, b8 = heads[s]
        prev1 = jnp.where(row8 == 0, before1, r1_8)
        prev2 = jnp.where(row8 == 0, before2, jnp.where(row8 == 1, before1, r2_8))
        o_ref[s * DEC_SEQ:s * DEC_SEQ + SUBLANES, :] = gated(a8, prev1, prev2, b8)

    @pl.when(m < N_PROMPT_TILES)
    def _():
        c = carry_ref[...]
        redo_head(0, c[SUBLANES - 1:SUBLANES, :], c[SUBLANES - 2:SUBLANES - 1, :])
        carry_ref[...] = tails[DEC_BATCH - 1]

        @pl.when(m == N_PROMPT_TILES - 1)
        def _():
            tp_ref[...] = tails[DEC_BATCH - 1]

    @pl.when(m == N_PROMPT_TILES)
    def _():
        for s in range(DEC_BATCH):
            c = cache_ref[s]
            redo_head(s, c[1:2, :], c[0:1, :])
            ts_ref[s * SUBLANES:(s + 1) * SUBLANES, :] = tails[s]


def _mm_ffnup(norm, w, layer, conv_w, conv_b, cache):
    bn = PROJ_BN
    k = w.shape[1]
    nb = FFN_DIM // bn
    norm_args, norm_specs = _norm_operands(norm)
    return pl.pallas_call(
        functools.partial(_mm_ffnup_kernel, stage=dict(layer=layer, col0s=(0, nb), n_tiles=nb)),
        out_shape=(
            jax.ShapeDtypeStruct((M_TOTAL, FFN_DIM), BF16),
            jax.ShapeDtypeStruct((SUBLANES, FFN_DIM), F32),
            jax.ShapeDtypeStruct((DEC_BATCH * SUBLANES, FFN_DIM), F32),
        ),
        grid=(nb, N_ROW_TILES),
        in_specs=norm_specs + [
            W_SPEC,
            pl.BlockSpec((CONV_W, bn), lambda n, m: (0, n)),
            pl.BlockSpec((1, bn), lambda n, m: (0, n)),
            pl.BlockSpec((DEC_BATCH, CONV_W - 1, bn), lambda n, m: (0, 0, n)),
        ],
        out_specs=(
            pl.BlockSpec((BM, bn), lambda n, m: (m, n)),
            pl.BlockSpec((SUBLANES, bn), lambda n, m: (0, n)),
            pl.BlockSpec((DEC_BATCH * SUBLANES, bn), lambda n, m: (0, n)),
        ),
        scratch_shapes=_weight_scratch(k, bn) + _weight_scratch(k, bn) + [pltpu.VMEM((SUBLANES, bn), F32)],
        compiler_params=_params(2),
        name="mm_ffn_up",
    )(*norm_args, w, conv_w, conv_b.reshape(1, FFN_DIM), cache)


def _ret_tables(lg, T):
    i = lax.broadcasted_iota(jnp.int32, (T, T), 0)
    j = lax.broadcasted_iota(jnp.int32, (T, T), 1)
    dist = jnp.abs(i - j).astype(F32)
    shift = int(math.log2(CHUNK))
    visible = (j >> shift) <= (i >> shift)
    mask = jnp.where(visible, jnp.exp(dist * lg), 0.0)
    t = lax.broadcasted_iota(jnp.int32, (T, 1), 0).astype(F32)
    cross = jnp.exp((t + 1.0) * lg)
    k_dec = jnp.exp((T - 1.0 - t) * lg)
    decay = jnp.exp(float(T) * lg)
    return mask, cross, k_dec, decay


def _ret_block(q, k, v, g, state, tables, gn):
    mask, cross, k_dec, decay = tables
    scores = lax.dot_general(q, k, (((1,), (1,)), ((), ())), preferred_element_type=F32)
    scores = scores * mask
    out = jnp.dot(scores.astype(BF16), v, preferred_element_type=F32)
    out = out + jnp.dot(q, state.astype(BF16), preferred_element_type=F32) * cross

    kd = (k.astype(F32) * k_dec).astype(BF16)
    new_state = decay * state + lax.dot_general(
        kd, v, (((0,), (0,)), ((), ())), preferred_element_type=F32)

    mu = jnp.mean(out, axis=-1, keepdims=True)
    oc = out - mu
    var = jnp.mean(oc * oc, axis=-1, keepdims=True)
    normed = (oc * lax.rsqrt(var + EPS)) * gn
    g = g.astype(F32)
    return (normed * (g * _sigmoid(g))).astype(BF16), new_state


def _ret_prompt_kernel(lg_ref, q_ref, k_ref, v_ref, g_ref, gn_ref, o_ref, sl_ref, state_ref, mask_ref):
    blk = pl.program_id(1)
    lg = lg_ref[...][:, 0:1]

    @pl.when(blk == 0)
    def _():
        state_ref[...] = jnp.zeros_like(state_ref)
        mask_ref[...] = _ret_tables(lg, RET_T)[0]

    _, cross, k_dec, decay = _ret_tables(lg, RET_T)
    tables = (mask_ref[...], cross, k_dec, decay)
    gn = gn_ref[...]
    state = state_ref[...]
    for sb in range(RET_SUB):
        rows = slice(sb * RET_T, (sb + 1) * RET_T)
        o_ref[rows, :], state = _ret_block(q_ref[rows, :], k_ref[rows, :], v_ref[rows, :],
                                           g_ref[rows, :], state, tables, gn)
    state_ref[...] = state

    @pl.when(blk == pl.num_programs(1) - 1)
    def _():
        sl_ref[...] = state


def _ret_sample_kernel(lg_ref, q_ref, k_ref, v_ref, g_ref, gn_ref, s0_ref, *rest, layer):
    o_ref, sl_ref = rest[-2:]
    if len(rest) == 2:
        for other in range(N_RET):
            if other != layer:
                sl_ref[other] = jnp.zeros(sl_ref.shape[1:], F32)
        sl_ref = sl_ref.at[layer]
    lg = lg_ref[...][:, 0:1]
    tables = _ret_tables(lg, DEC_SEQ)
    gn = gn_ref[...]
    for s in range(DEC_BATCH):
        rows = slice(s * DEC_SEQ, (s + 1) * DEC_SEQ)
        o_ref[rows, :], sl_ref[s] = _ret_block(q_ref[rows, :], k_ref[rows, :], v_ref[rows, :],
                                               g_ref[rows, :], s0_ref[s], tables, gn)


def _retention(qk, vg, gn, s0_stack, layer, sample_states=None):
    log_g = np.log1p(-np.exp2(-5.0 - np.arange(RET_HEADS, dtype=np.float32))).astype(np.float32)
    lg = jnp.asarray(np.broadcast_to(log_g[:, None, None], (RET_HEADS, 1, LANES)).copy())
    gn = gn.reshape(1, RET_VD)
    k_off = RET_QK // RET_DK
    g_off = RET_VD // RET_DV
    rows_p = RET_T * RET_SUB
    o_p, s_p = pl.pallas_call(
        _ret_prompt_kernel,
        out_shape=(jax.ShapeDtypeStruct((M_PROMPT, RET_VD), BF16),
                   jax.ShapeDtypeStruct((1, RET_HEADS, RET_DK, RET_DV), F32)),
        grid=(RET_HEADS, M_PROMPT // rows_p),
        in_specs=[
            pl.BlockSpec((None, 1, LANES), lambda h, b: (h, 0, 0)),
            pl.BlockSpec((rows_p, RET_DK), lambda h, b: (b, h)),
            pl.BlockSpec((rows_p, RET_DK), lambda h, b: (b, k_off + h)),
            pl.BlockSpec((rows_p, RET_DV), lambda h, b: (b, h)),
            pl.BlockSpec((rows_p, RET_DV), lambda h, b: (b, g_off + h)),
            pl.BlockSpec((1, RET_DV), lambda h, b: (0, h)),
        ],
        out_specs=(pl.BlockSpec((rows_p, RET_DV), lambda h, b: (b, h)),
                   pl.BlockSpec((None, None, RET_DK, RET_DV), lambda h, b: (0, h, 0, 0))),
        scratch_shapes=[pltpu.VMEM((RET_DK, RET_DV), F32), pltpu.VMEM((RET_T, RET_T), F32)],
        compiler_params=_params(2),
        name="retention",
    )(lg, qk, qk, vg, vg, gn)
    sample_block = M_PROMPT // M_SAMPLE
    carried = [] if sample_states is None else [sample_states]
    n_in = 7
    if carried:
        state_spec = pl.BlockSpec((None, DEC_BATCH, None, RET_DK, RET_DV), lambda h: (layer, 0, h, 0, 0))
    else:
        state_spec = pl.BlockSpec((N_RET, DEC_BATCH, None, RET_DK, RET_DV), lambda h: (0, 0, h, 0, 0))
    o_s, s_s = pl.pallas_call(
        functools.partial(_ret_sample_kernel, layer=layer),
        out_shape=(jax.ShapeDtypeStruct((M_SAMPLE, RET_VD), BF16),
                   jax.ShapeDtypeStruct((N_RET, DEC_BATCH, RET_HEADS, RET_DK, RET_DV), F32)),
        grid=(RET_HEADS,),
        in_specs=[
            pl.BlockSpec((None, 1, LANES), lambda h: (h, 0, 0)),
            pl.BlockSpec((M_SAMPLE, RET_DK), lambda h: (sample_block, h)),
            pl.BlockSpec((M_SAMPLE, RET_DK), lambda h: (sample_block, k_off + h)),
            pl.BlockSpec((M_SAMPLE, RET_DV), lambda h: (sample_block, h)),
            pl.BlockSpec((M_SAMPLE, RET_DV), lambda h: (sample_block, g_off + h)),
            pl.BlockSpec((1, RET_DV), lambda h: (0, h)),
            pl.BlockSpec((None, DEC_BATCH, None, RET_DK, RET_DV), lambda h: (layer, 0, h, 0, 0)),
        ] + [pl.BlockSpec(memory_space=pl.ANY) for _ in carried],
        out_specs=(pl.BlockSpec((M_SAMPLE, RET_DV), lambda h: (0, h)), state_spec),
        input_output_aliases={n_in: 1} if carried else {},
        compiler_params=_params(1),
        name="retention_sample",
    )(lg, qk, qk, vg, vg, gn, s0_stack, *carried)
    return o_p, o_s, s_p, s_s


def _ssm_prep_kernel(ar_ref, ai_ref, ldt_ref, br_ref, bi_ref, cr_ref, ci_ref,
                     bc_ref, cc_ref, w_ref, a16r_ref, a16i_ref, vr_s, vi_s):
    ar = ar_ref[...]
    ai = ai_ref[...]
    dt = jnp.exp(ldt_ref[...])
    mag = jnp.exp(ar * dt)
    ang = ai * dt
    abr = mag * jnp.cos(ang)
    abi = mag * jnp.sin(ang)
    den = ar * ar + ai * ai
    nr = abr - 1.0
    ni = abi
    cfr = (nr * ar + ni * ai) / den
    cfi = (ni * ar - nr * ai) / den

    def b4(z):
        return z[:, None]

    br = br_ref[...]
    bi = bi_ref[...]
    bbr = b4(cfr) * br - b4(cfi) * bi
    bbi = b4(cfr) * bi + b4(cfi) * br
    cre = cr_ref[...]
    cim = ci_ref[...]

    powers = []
    pr = jnp.ones_like(ar)
    pi = jnp.zeros_like(ar)
    for _ in range(SSM_T + 1):
        powers.append((pr, pi))
        pr, pi = pr * abr - pi * abi, pr * abi + pi * abr
    a16r_ref[...] = powers[SSM_T][0]
    a16i_ref[...] = powers[SSM_T][1]

    for m in range(SSM_T + 1):
        pr, pi = powers[m]
        vr = b4(pr) * cre - b4(pi) * cim
        vi = b4(pr) * cim + b4(pi) * cre
        if m < SSM_T:
            vr_s[:, m] = vr
            vi_s[:, m] = vi
        if m >= 1:
            cc_ref[:, m - 1, :, :, 0:LANES] = vr.astype(BF16)
            cc_ref[:, m - 1, :, :, LANES:2 * LANES] = (-vi).astype(BF16)

    for t in range(SSM_T):
        pr, pi = powers[SSM_T - 1 - t]
        bc_ref[:, t, :, :, 0:LANES] = (b4(pr) * bbr - b4(pi) * bbi).astype(BF16)
        bc_ref[:, t, :, :, LANES:2 * LANES] = (b4(pr) * bbi + b4(pi) * bbr).astype(BF16)

    lane = lax.broadcasted_iota(jnp.int32, (SSM_PAIR_CH, SSM_PAIR_COLS), 1)
    contract_last = (((1,), (1,)), ((), ()))
    for q in range(SSM_PB):
        vr_all = vr_s[q].reshape(SSM_PAIR_COLS, LANES)
        vi_all = vi_s[q].reshape(SSM_PAIR_COLS, LANES)
        lags = (lax.dot_general(bbr[q].reshape(SSM_PAIR_CH, LANES), vr_all, contract_last,
                                precision=lax.Precision.HIGHEST, preferred_element_type=F32)
                - lax.dot_general(bbi[q].reshape(SSM_PAIR_CH, LANES), vi_all, contract_last,
                                  precision=lax.Precision.HIGHEST, preferred_element_type=F32))
        for t in range(SSM_T):
            shifted = lags if t == 0 else pltpu.roll(lags, SSM_PAIR_CH * t, axis=1)
            w_ref[q, t * SSM_PAIR_CH:(t + 1) * SSM_PAIR_CH, :] = jnp.where(
                lane >= SSM_PAIR_CH * t, shifted, 0.0).astype(BF16)


def _pair_pack(x):
    x = x.reshape(SSM_PAIRS, 2, SSM_GROUP, SSM_STATE)
    lo = jnp.pad(x[:, 0], ((0, 0), (0, 0), (0, SSM_STATE)))
    hi = jnp.pad(x[:, 1], ((0, 0), (0, 0), (SSM_STATE, 0)))
    return jnp.stack([lo, hi], axis=1)


def _ssm_prep(a_re, a_im, log_dt, b_re, b_im, c_re, c_im):
    assert 2 * SSM_PB * SSM_GROUP == LANES
    ar = a_re.reshape(SSM_PAIRS, 1, LANES)
    ai = a_im.reshape(SSM_PAIRS, 1, LANES)
    ldt = jnp.repeat(log_dt, SSM_STATE).reshape(SSM_PAIRS, 1, LANES)
    b2r = _pair_pack(jnp.transpose(b_re, (0, 2, 1)))
    b2i = _pair_pack(jnp.transpose(b_im, (0, 2, 1)))
    c2r = _pair_pack(c_re)
    c2i = _pair_pack(c_im)
    vec = pl.BlockSpec((SSM_PB, 1, LANES), lambda p: (p, 0, 0))
    mat = pl.BlockSpec((SSM_PB, 2, SSM_GROUP, LANES), lambda p: (p, 0, 0, 0))
    proj = pl.BlockSpec((SSM_PB, SSM_T, 2, SSM_GROUP, 2 * LANES), lambda p: (p, 0, 0, 0, 0))
    proj_shape = jax.ShapeDtypeStruct((SSM_PAIRS, SSM_T, 2, SSM_GROUP, 2 * LANES), BF16)
    bc, cc, w, a16r, a16i = pl.pallas_call(
        _ssm_prep_kernel,
        out_shape=(
            proj_shape, proj_shape,
            jax.ShapeDtypeStruct((SSM_PAIRS, SSM_PAIR_COLS, SSM_PAIR_COLS), BF16),
            jax.ShapeDtypeStruct((SSM_PAIRS, 1, LANES), F32),
            jax.ShapeDtypeStruct((SSM_PAIRS, 1, LANES), F32),
        ),
        grid=(N_LANE_TILES,),
        in_specs=[vec, vec, vec, mat, mat, mat, mat],
        out_specs=(proj, proj,
                   pl.BlockSpec((SSM_PB, SSM_PAIR_COLS, SSM_PAIR_COLS), lambda p: (p, 0, 0)),
                   vec, vec),
        scratch_shapes=[pltpu.VMEM((SSM_PB, SSM_T, 2, SSM_GROUP, LANES), F32)] * 2,
        compiler_params=_params(1),
        name="ssm_prep",
    )(ar, ai, ldt, b2r, b2i, c2r, c2i)
    bc = bc.reshape(SSM_PAIRS, SSM_PAIR_COLS, 2 * LANES)
    cc = cc.reshape(SSM_PAIRS, SSM_PAIR_COLS, 2 * LANES)
    row = SSM_GROUPS * SSM_STATE
    a16r = jnp.broadcast_to(a16r.reshape(1, row), (SSM_SEGS, row))
    a16i = jnp.broadcast_to(a16i.reshape(1, row), (SSM_SEGS, row))
    return bc, cc, w, a16r, a16i


SSM_CHUNK_COLS = SSM_T * LANES
SSM_STATE_COLS = SSM_PB * LANES
SSM_CHUNKS_P = M_PROMPT // SSM_T
SSM_SCAN_UNROLL = 8


def _ssm_main_kernel(x_ref, w_ref, bc_ref, cc_ref, ar_ref, ai_ref, h0r_ref, h0i_ref,
                     y_ref, fpr_ref, fpi_ref, fsr_ref, fsi_ref, d_s, s_s, t_s):
    pair_cols = [slice(q * SSM_PAIR_COLS, (q + 1) * SSM_PAIR_COLS) for q in range(SSM_PB)]

    for q in range(SSM_PB):
        d = jnp.dot(x_ref[:, pair_cols[q]], bc_ref[q], preferred_element_type=F32)
        d_s[q] = d[:, 0:LANES]
        d_s[SSM_PB + q] = d[:, LANES:2 * LANES]

    tiles = range(SSM_PB)
    ar = [ar_ref[:, k * LANES:(k + 1) * LANES] for k in tiles]
    ai = [ai_ref[:, k * LANES:(k + 1) * LANES] for k in tiles]

    def advance(sr, si, rows, record=None):
        out_r, out_i = [], []
        for k in tiles:
            if record is not None:
                ref, rec_rows = record
                ref[k, rec_rows, :] = sr[k]
                ref[SSM_PB + k, rec_rows, :] = si[k]
            dr = d_s[k, rows, :]
            di = d_s[SSM_PB + k, rows, :]
            out_r.append(ar[k] * sr[k] - ai[k] * si[k] + dr)
            out_i.append(ar[k] * si[k] + ai[k] * sr[k] + di)
        return out_r, out_i

    def prompt_rows(i):
        return pl.ds(i, SSM_SEGS, stride=SSM_STEPS_P)

    def segment_ends(k):
        a1r, a1i = ar[k][0:1], ai[k][0:1]
        p8r, p8i = ar[k], ai[k]
        for _ in range(int(math.log2(SUBLANES))):
            p8r, p8i = p8r * p8r - p8i * p8i, 2.0 * p8r * p8i
        wr, wi = [jnp.ones((1, LANES), F32)], [jnp.zeros((1, LANES), F32)]
        for _ in range(SUBLANES - 1):
            wr, wi = wr + [wr[-1] * a1r - wi[-1] * a1i], wi + [wr[-1] * a1i + wi[-1] * a1r]
        row_r = jnp.concatenate(wr[::-1], axis=0)
        row_i = jnp.concatenate(wi[::-1], axis=0)
        ends_r, ends_i = [], []
        for s in range(SSM_SEGS):
            acc_r = acc_i = None
            for b in range(SSM_STEPS_P // SUBLANES):
                rows = slice(s * SSM_STEPS_P + b * SUBLANES, s * SSM_STEPS_P + (b + 1) * SUBLANES)
                dr = d_s[k, rows, :]
                di = d_s[SSM_PB + k, rows, :]
                if acc_r is None:
                    acc_r, acc_i = dr, di
                else:
                    acc_r, acc_i = p8r * acc_r - p8i * acc_i + dr, p8r * acc_i + p8i * acc_r + di
            ends_r.append(jnp.sum(row_r * acc_r - row_i * acc_i, axis=0, keepdims=True))
            ends_i.append(jnp.sum(row_r * acc_i + row_i * acc_r, axis=0, keepdims=True))
        return jnp.concatenate(ends_r, axis=0), jnp.concatenate(ends_i, axis=0)

    er, ei = zip(*[segment_ends(k) for k in tiles])

    def prompt_pass(start_r, start_i):
        def body(i, carry):
            rec = (t_s, pl.ds(pl.multiple_of(i * SSM_SEGS, SSM_SEGS), SSM_SEGS))
            out_r, out_i = advance(list(carry[0]), list(carry[1]), prompt_rows(i), rec)
            return tuple(out_r), tuple(out_i)
        lax.fori_loop(0, SSM_STEPS_P, body, (tuple(start_r), tuple(start_i)), unroll=SSM_SCAN_UNROLL)

    sr, si = [], []
    for k in tiles:
        pr, pi = ar[k][0:1], ai[k][0:1]
        for _ in range(int(math.log2(SSM_STEPS_P))):
            pr, pi = pr * pr - pi * pi, 2.0 * pr * pi
        cr = jnp.zeros((1, LANES), F32)
        ci = cr
        starts_r, starts_i = [], []
        for s in range(SSM_SEGS):
            starts_r.append(cr)
            starts_i.append(ci)
            cr, ci = (pr * cr - pi * ci + er[k][s:s + 1], pr * ci + pi * cr + ei[k][s:s + 1])
        cols = slice(k * LANES, (k + 1) * LANES)
        fpr_ref[:, cols] = jnp.broadcast_to(cr, (SSM_SEGS, LANES))
        fpi_ref[:, cols] = jnp.broadcast_to(ci, (SSM_SEGS, LANES))
        sr.append(jnp.concatenate(starts_r, axis=0))
        si.append(jnp.concatenate(starts_i, axis=0))

    prompt_pass(sr, si)
    for k in range(2 * SSM_PB):
        for s in range(SSM_SEGS):
            s_s[k, s * SSM_STEPS_P:(s + 1) * SSM_STEPS_P, :] = t_s[k, pl.ds(s, SSM_STEPS_P, stride=SSM_SEGS), :]
    sr = [h0r_ref[:, k * LANES:(k + 1) * LANES] for k in tiles]
    si = [h0i_ref[:, k * LANES:(k + 1) * LANES] for k in tiles]
    for i in range(SSM_STEPS_S):
        rows = pl.ds(SSM_CHUNKS_P + i, DEC_BATCH, stride=SSM_STEPS_S)
        sr, si = advance(sr, si, rows, (s_s, rows))
    for k in tiles:
        fsr_ref[:, k * LANES:(k + 1) * LANES] = sr[k]
        fsi_ref[:, k * LANES:(k + 1) * LANES] = si[k]

    contract_last = (((1,), (1,)), ((), ()))
    for q in range(SSM_PB):
        states = jnp.concatenate([s_s[q].astype(BF16), s_s[SSM_PB + q].astype(BF16)], axis=1)
        carried = lax.dot_general(states, cc_ref[q], contract_last, preferred_element_type=F32)
        local = jnp.dot(x_ref[:, pair_cols[q]], w_ref[q], preferred_element_type=F32)
        y_ref[:, pair_cols[q]] = local + carried


def _ssm_main(x_c, w, bc, cc, a16r, a16i, h0r, h0i):
    row = SSM_GROUPS * SSM_STATE
    st = pl.BlockSpec((SSM_SEGS, SSM_STATE_COLS), lambda j: (0, j))
    st_shape = jax.ShapeDtypeStruct((SSM_SEGS, row), F32)
    proj = pl.BlockSpec((SSM_PB, SSM_PAIR_COLS, 2 * LANES), lambda j: (j, 0, 0))
    chunk_rows = pl.BlockSpec((None, N_CHUNKS, SSM_CHUNK_COLS), lambda j: (j, 0, 0))
    return pl.pallas_call(
        _ssm_main_kernel,
        out_shape=(jax.ShapeDtypeStruct((N_LANE_TILES, N_CHUNKS, SSM_CHUNK_COLS), F32),
                   st_shape, st_shape, st_shape, st_shape),
        grid=(N_LANE_TILES,),
        in_specs=[
            chunk_rows,
            pl.BlockSpec((SSM_PB, SSM_PAIR_COLS, SSM_PAIR_COLS), lambda j: (j, 0, 0)),
            proj, proj, st, st, st, st,
        ],
        out_specs=(chunk_rows, st, st, st, st),
        scratch_shapes=[
            pltpu.VMEM((2 * SSM_PB, N_CHUNKS, LANES), F32),
            pltpu.VMEM((2 * SSM_PB, N_CHUNKS, LANES), F32),
            pltpu.VMEM((2 * SSM_PB, SSM_CHUNKS_P, LANES), F32),
        ],
        compiler_params=_params(1),
        name="ssm_main",
    )(x_c, w, bc, cc, a16r, a16i, h0r, h0i)


def _ssm_post_kernel(y_ref, x_ref, g_ref, d_ref, o_ref, y_s):
    block_of_lane = lax.broadcasted_iota(jnp.int32, (CHUNKS_PER_TILE, LANES), 1) // SSM_PAIR_CH
    for j in range(N_LANE_TILES):
        for tg in range(SSM_T // SSM_VEC_TOKENS):
            pieces = [y_ref[j, :, q * SSM_PAIR_COLS + tg * LANES:q * SSM_PAIR_COLS + (tg + 1) * LANES]
                      for q in range(SSM_PB)]
            for u, out in enumerate(_transpose_lane_blocks(pieces, block_of_lane)):
                y_s[j, pl.ds(tg * SSM_VEC_TOKENS + u, CHUNKS_PER_TILE, stride=SSM_T), :] = out
    x = x_ref[...]
    ms = jnp.mean(x * x, axis=-1, keepdims=True)
    h = (x * lax.rsqrt(ms + EPS)) * g_ref[...]
    for j in range(N_LANE_TILES):
        cols = slice(j * LANES, (j + 1) * LANES)
        y = y_s[j] + d_ref[:, cols] * h[:, cols]
        gelu = 0.5 * y * (1.0 + jnp.tanh(GELU_TANH_SCALE * (y + GELU_TANH_CUBIC * (y * y * y))))
        o_ref[:, cols] = gelu.astype(BF16)


def _ssm_post(y_c, x, g, d):
    spec = pl.BlockSpec((BM, D_MODEL), lambda m: (m, 0))
    vec = pl.BlockSpec((1, D_MODEL), lambda m: (0, 0))
    return pl.pallas_call(
        _ssm_post_kernel,
        out_shape=jax.ShapeDtypeStruct((M_TOTAL, D_MODEL), BF16),
        grid=(N_ROW_TILES,),
        in_specs=[pl.BlockSpec((N_LANE_TILES, CHUNKS_PER_TILE, SSM_CHUNK_COLS), lambda m: (0, m, 0)),
                  spec, vec, vec],
        out_specs=spec,
        scratch_shapes=[pltpu.VMEM((N_LANE_TILES, BM, LANES), F32)],
        compiler_params=_params(1),
        name="ssm_post",
    )(y_c, x, g.reshape(1, D_MODEL), d.reshape(1, D_MODEL))


def kernel(x_prompt, x_sample, state_ret, state_ssm_re, state_ssm_im, cache_conv, norm_mix, norm_ffn, norm_final, ret_w_in, ret_gn, ret_w_out, ssm_a_re, ssm_a_im, ssm_log_dt, ssm_b_re, ssm_b_im, ssm_c_re, ssm_c_im, ssm_d, ssm_w_glu, ffn_w_up, ffn_conv_w, ffn_conv_b, ffn_w_down):
    x = (x_prompt.reshape(M_PROMPT, D_MODEL), x_sample.reshape(M_SAMPLE, D_MODEL))
    xb, ssq, cos, sin = _prologue(*x)

    ret_p, ret_s = [], None
    re_p, im_p, re_s, im_s = [], [], [], []
    conv_p, conv_s = [], []
    state_row = SSM_GROUPS * SSM_STATE
    for i in range(DEPTH):
        j = i // 2
        if i % 2 == 0:
            qk, vg = _mm_retin((xb, ssq, norm_mix[i]), ret_w_in, j, cos, sin)
            o_p, o_s, s_p, ret_s = _retention(qk, vg, ret_gn[j], state_ret, j, ret_s)
            ret_p.append(s_p)
            x, xb, ssq = _mm_res((o_p, o_s), ret_w_out, j, x, emit=True)
        else:
            h_c = _rmsnorm_chunked(x, norm_mix[i])
            bc, cc, w_ssm, a16r, a16i = _ssm_prep(ssm_a_re[j], ssm_a_im[j], ssm_log_dt[j],
                                               ssm_b_re[j], ssm_b_im[j], ssm_c_re[j], ssm_c_im[j])
            y_c, fpr, fpi, fsr, fsi = _ssm_main(
                h_c, w_ssm, bc, cc, a16r, a16i,
                state_ssm_re[j].reshape(DEC_BATCH, state_row),
                state_ssm_im[j].reshape(DEC_BATCH, state_row))
            re_p.append(fpr[0:1].reshape(1, SSM_GROUPS, SSM_STATE))
            im_p.append(fpi[0:1].reshape(1, SSM_GROUPS, SSM_STATE))
            re_s.append(fsr.reshape(DEC_BATCH, SSM_GROUPS, SSM_STATE))
            im_s.append(fsi.reshape(DEC_BATCH, SSM_GROUPS, SSM_STATE))
            gl = _ssm_post(y_c, x, norm_mix[i], ssm_d[j])
            x, xb, ssq = _mm_glu(gl, ssm_w_glu, j, x)

        u, tail_p, tail_s = _mm_ffnup((xb, ssq, norm_ffn[i]), ffn_w_up, i, ffn_conv_w[i],
                                      ffn_conv_b[i], cache_conv[i])
        conv_p.append(tail_p[SUBLANES - (CONV_W - 1):].reshape(1, CONV_W - 1, FFN_DIM))
        conv_s.append(tail_s.reshape(DEC_BATCH, SUBLANES, FFN_DIM)[:, SUBLANES - (CONV_W - 1):])
        if i + 1 < DEPTH and (i + 1) % 2 == 0:
            x, xb, ssq = _mm_res(u, ffn_w_down, i, x, emit=True)
        else:
            x = _mm_res(u, ffn_w_down, i, x)

    y_prompt = _rmsnorm(x, norm_final, F32, 0, N_PROMPT_TILES).reshape(1, SEQ, D_MODEL)
    y_sample = _rmsnorm(x, norm_final, F32, N_PROMPT_TILES, 1).reshape(DEC_BATCH, DEC_SEQ, D_MODEL)
    return (y_prompt, y_sample, jnp.stack(ret_p), ret_s,
            jnp.stack(re_p), jnp.stack(im_p), jnp.stack(re_s), jnp.stack(im_s),
            jnp.stack(conv_p), jnp.stack(conv_s))
```
